```python
import jax, jax.numpy as jnp
from jax import lax
import numpy as np

D_MODEL = 2048
BATCH = 8
SEQ = 8192
DEPTH = 1

GRID_W = 64
CTX_LEN = 256
HEAD_DIM = 128
N_Q_HEADS = 16
N_KV_HEADS = 4
Q_PER_KV = N_Q_HEADS // N_KV_HEADS
WINDOW = 128
BLOCK = 128
ROPE_THETA = 10000.0
GLA_HEADS = 4
GLA_DK = D_MODEL // 2 // GLA_HEADS
GLA_DV = D_MODEL // GLA_HEADS
GLA_LOWRANK = 16
GLA_GATE_NORM = 16.0
GLA_CHUNK = 64
D_FF = 5632
CONV_W = 3
EPS = 1e-6
ATTN_WIDTH = N_Q_HEADS * HEAD_DIM
KV_WIDTH = N_KV_HEADS * HEAD_DIM
GLA_K_WIDTH = GLA_HEADS * GLA_DK
GLA_V_WIDTH = GLA_HEADS * GLA_DV
IN_SPLITS = (ATTN_WIDTH, KV_WIDTH, KV_WIDTH, GLA_K_WIDTH, GLA_K_WIDTH, GLA_V_WIDTH, GLA_V_WIDTH,
             GLA_LOWRANK, GLA_LOWRANK, D_MODEL, D_MODEL)
IN_WIDTH = sum(IN_SPLITS)

kernel_name = "hybrid_swa_gla_convffn_prefix_dit"


def rms_norm(x, g):
    xf = x.astype(jnp.float32)
    y = xf * lax.rsqrt(jnp.mean(xf * xf, axis=-1, keepdims=True) + EPS)
    return (y * g.astype(jnp.float32)).astype(x.dtype)


def modulate(x, g, shift, scale):
    return rms_norm(x, g) * (1 + scale) + shift


def split_heads(a, n_heads):
    return a.reshape(*a.shape[:-1], n_heads, -1)


def axial_rope_angles(n):
    rows = n // GRID_W
    row = jnp.repeat(jnp.arange(rows), GRID_W)
    col = jnp.tile(jnp.arange(GRID_W), rows)
    n_freq = HEAD_DIM // 4
    inv = ROPE_THETA ** (-jnp.arange(n_freq, dtype=jnp.float32) / n_freq)
    ang = jnp.concatenate([row[:, None] * inv, col[:, None] * inv], axis=-1)
    return jnp.cos(ang), jnp.sin(ang)


def apply_rope(x, cos, sin):
    half = HEAD_DIM // 2
    x1, x2 = x[..., :half].astype(jnp.float32), x[..., half:].astype(jnp.float32)
    c, s = cos[None, :, None, :], sin[None, :, None, :]
    return jnp.concatenate([x1 * c - x2 * s, x2 * c + x1 * s], axis=-1).astype(x.dtype)


def gqa_scores(q5, k):
    return jnp.einsum("bqhgd,bkhd->bhgqk", q5, k).astype(jnp.float32) * (HEAD_DIM ** -0.5)


def attend_with_sink(parts, sink_g):
    sink_b = sink_g[None, :, :, None]
    m = sink_b
    for s, _ in parts:
        m = jnp.maximum(m, s.max(axis=-1))
    denom = jnp.exp(sink_b - m)
    out = None
    for s, val in parts:
        p = jnp.exp(s - m[..., None])
        denom = denom + p.sum(axis=-1)
        o = jnp.einsum("bhgqk,bkhd->bqhgd", p, val.astype(jnp.float32))
        out = o if out is None else out + o
    return out / jnp.transpose(denom, (0, 3, 1, 2))[..., None]


def windowed_attention(q, k, v, k_ctx, v_ctx, sink):
    B, n = q.shape[:2]
    nb = n // BLOCK
    span = BLOCK + 2 * WINDOW
    pad = ((0, 0), (WINDOW, WINDOW), (0, 0), (0, 0))
    kp, vp = jnp.pad(k, pad), jnp.pad(v, pad)
    qb = jnp.swapaxes(q.reshape(B, nb, BLOCK, N_KV_HEADS, Q_PER_KV, HEAD_DIM), 0, 1)
    sink_g = sink.reshape(N_KV_HEADS, Q_PER_KV).astype(jnp.float32)
    s_ctx_all = None

    def block(args):
        qi, i = args
        start = i * BLOCK
        kw = lax.dynamic_slice_in_dim(kp, start, span, axis=1)
        vw = lax.dynamic_slice_in_dim(vp, start, span, axis=1)
        qpos = start + jnp.arange(BLOCK)
        kpos = start - WINDOW + jnp.arange(span)
        valid = ((jnp.abs(qpos[:, None] - kpos[None, :]) <= WINDOW)
                 & (kpos >= 0)[None, :] & (kpos < n)[None, :])
        s_lat = jnp.where(valid, gqa_scores(qi, kw), -jnp.inf)
        s_ctx = gqa_scores(qi, k_ctx)
        return attend_with_sink([(s_lat, vw), (s_ctx, v_ctx)], sink_g)

    o = lax.map(block, (qb, jnp.arange(nb)))
    return jnp.swapaxes(o, 0, 1).reshape(B, n, ATTN_WIDTH).astype(q.dtype)


def context_attention(q, k, v, sink):
    B, L = q.shape[:2]
    q5 = q.reshape(B, L, N_KV_HEADS, Q_PER_KV, HEAD_DIM)
    sink_g = sink.reshape(N_KV_HEADS, Q_PER_KV).astype(jnp.float32)
    o = attend_with_sink([(gqa_scores(q5, k), v)], sink_g)
    return o.reshape(B, L, ATTN_WIDTH).astype(q.dtype)


def gla_chunked(q, k, v, g, s0):
    B, T, H, _ = q.shape
    dv = v.shape[-1]
    nc = T // GLA_CHUNK

    def to_chunks(a):
        return jnp.transpose(a.reshape(B, nc, GLA_CHUNK, H, a.shape[-1]), (1, 0, 3, 2, 4)).astype(jnp.float32)

    causal = jnp.tril(jnp.ones((GLA_CHUNK, GLA_CHUNK), dtype=bool))[..., None]

    def step(S, inp):
        qc, kc, vc, gc = inp
        b = jnp.cumsum(gc, axis=2)
        b_last = b[:, :, -1:]
        o_inter = jnp.einsum("bhcd,bhde->bhce", qc * jnp.exp(b), S)
        rel = jnp.where(causal, b[:, :, :, None, :] - b[:, :, None, :, :], -jnp.inf)
        A = jnp.einsum("bhtd,bhsd,bhtsd->bhts", qc, kc, jnp.exp(rel))
        o = o_inter + jnp.einsum("bhts,bhse->bhte", A, vc)
        S = (jnp.exp(b_last[:, :, 0])[..., None] * S
             + jnp.einsum("bhsd,bhse->bhde", kc * jnp.exp(b_last - b), vc))
        return S, o

    S, o = lax.scan(step, s0, (to_chunks(q), to_chunks(k), to_chunks(v), to_chunks(g)))
    o = jnp.transpose(o, (1, 0, 3, 2, 4)).reshape(B, T, H, dv)
    return o.astype(v.dtype), S


def gla_final_state(k, v, g):
    b = jnp.cumsum(g.astype(jnp.float32), axis=1)
    w = jnp.exp(b[:, -1:] - b)
    return jnp.einsum("bthd,bthe->bhde", k.astype(jnp.float32) * w, v.astype(jnp.float32))


def flip(a):
    return a[:, ::-1]


def context_gla(q, k, v, gf, gb, need_out):
    if need_out:
        s0 = jnp.zeros((q.shape[0], GLA_HEADS, GLA_DK, GLA_DV), jnp.float32)
        of, sf = gla_chunked(q, k, v, gf, s0)
        ob, sb = gla_chunked(flip(q), flip(k), flip(v), flip(gb), s0)
        return sf, sb, of + flip(ob)
    return gla_final_state(k, v, gf), gla_final_state(flip(k), flip(v), flip(gb)), None


def latent_gla(q, k, v, gf, gb, sf, sb):
    of, _ = gla_chunked(q, k, v, gf, sf)
    ob, _ = gla_chunked(flip(q), flip(k), flip(v), flip(gb), sb)
    return of + flip(ob)


def project_heads(h, w_in, q_norm, k_norm, w_gate_f, b_gate_f, w_gate_b, b_gate_b):
    offs = np.cumsum(IN_SPLITS)[:-1].tolist()
    qa, ka, va, qb, kb, vb, rb, lrf, lrb, gate_a, gate_b = jnp.split(h @ w_in, offs, axis=-1)
    qa = rms_norm(split_heads(qa, N_Q_HEADS), q_norm)
    ka = rms_norm(split_heads(ka, N_KV_HEADS), k_norm)
    va = split_heads(va, N_KV_HEADS)
    qb = split_heads(qb, GLA_HEADS) * (GLA_DK ** -0.5)
    kb = split_heads(kb, GLA_HEADS)
    vb = split_heads(vb, GLA_HEADS)
    gf = split_heads(jax.nn.log_sigmoid((lrf @ w_gate_f + b_gate_f).astype(jnp.float32)) / GLA_GATE_NORM, GLA_HEADS)
    gb = split_heads(jax.nn.log_sigmoid((lrb @ w_gate_b + b_gate_b).astype(jnp.float32)) / GLA_GATE_NORM, GLA_HEADS)
    return qa, ka, va, qb, kb, vb, rb, gf, gb, gate_a, gate_b


def merge_branches(o_attn, o_gla, rb, gate_a, gate_b, gla_norm, w_attn_o, w_gla_o, w_out):
    B, T = o_attn.shape[:2]
    y_att = o_attn @ w_attn_o
    y_gla = (rms_norm(o_gla, gla_norm).reshape(B, T, GLA_V_WIDTH) * jax.nn.silu(rb)) @ w_gla_o
    return (jax.nn.sigmoid(gate_a) * y_att + jax.nn.sigmoid(gate_b) * y_gla) @ w_out


def conv_ffn(h, w_up, conv_w, conv_b, w_down):
    T = h.shape[1]
    u = h @ w_up
    up = jnp.pad(u, ((0, 0), (CONV_W // 2, CONV_W // 2), (0, 0)))
    u = sum(up[:, j:j + T] * conv_w[j] for j in range(CONV_W)) + conv_b
    a, g = jnp.split(u, 2, axis=-1)
    return (jax.nn.silu(a) * g) @ w_down


def _fwd_setup_inputs(seed: int = 0) -> dict:
    key = jax.random.key(seed)
    ks = jax.random.split(key, 24)
    D, L = D_MODEL, DEPTH

    def nrm(k, shape, s):
        return jax.random.normal(k, shape, jnp.float32) * s

    return {
        "x": nrm(ks[0], (BATCH, SEQ, D), 1.0),
        "c": nrm(ks[1], (BATCH, D), 1.0),
        "ctx": nrm(ks[2], (BATCH, CTX_LEN, D), 1.0),
        "c_ctx": nrm(ks[3], (D,), 1.0),
        "w_mod": nrm(ks[4], (L, D, 6 * D), 0.5 * D ** -0.5),
        "b_mod": nrm(ks[5], (L, 6 * D), 0.02),
        "g_mix": 1.0 + nrm(ks[6], (L, D), 0.05),
        "w_in": nrm(ks[7], (L, D, IN_WIDTH), D ** -0.5),
        "q_norm": 1.0 + nrm(ks[8], (L, HEAD_DIM), 0.05),
        "k_norm": 1.0 + nrm(ks[9], (L, HEAD_DIM), 0.05),
        "attn_sink": nrm(ks[10], (L, N_Q_HEADS), 0.5),
        "w_gate_f": nrm(ks[11], (L, GLA_LOWRANK, GLA_K_WIDTH), GLA_LOWRANK ** -0.5),
        "b_gate_f": nrm(ks[12], (L, GLA_K_WIDTH), 0.1),
        "w_gate_b": nrm(ks[13], (L, GLA_LOWRANK, GLA_K_WIDTH), GLA_LOWRANK ** -0.5),
        "b_gate_b": nrm(ks[14], (L, GLA_K_WIDTH), 0.1),
        "gla_norm": 1.0 + nrm(ks[15], (L, GLA_DV), 0.05),
        "w_attn_o": nrm(ks[16], (L, ATTN_WIDTH, D), ATTN_WIDTH ** -0.5),
        "w_gla_o": nrm(ks[17], (L, GLA_V_WIDTH, D), GLA_V_WIDTH ** -0.5),
        "w_out": nrm(ks[18], (L, D, D), D ** -0.5),
        "g_ffn": 1.0 + nrm(ks[19], (L, D), 0.05),
        "w_up": nrm(ks[20], (L, D, 2 * D_FF), D ** -0.5),
        "conv_w": nrm(ks[21], (L, CONV_W, 2 * D_FF), CONV_W ** -0.5),
        "conv_b": nrm(ks[22], (L, 2 * D_FF), 0.02),
        "w_down": nrm(ks[23], (L, D_FF, D), D_FF ** -0.5),
    }


def _fwd_reference(x, c, ctx, c_ctx, w_mod, b_mod, g_mix, w_in, q_norm, k_norm, attn_sink,
              w_gate_f, b_gate_f, w_gate_b, b_gate_b, gla_norm, w_attn_o, w_gla_o, w_out,
              g_ffn, w_up, conv_w, conv_b, w_down):
    n = x.shape[1]
    cos, sin = axial_rope_angles(n)
    for l in range(DEPTH):
        last = l == DEPTH - 1
        mod_x = jnp.split((jax.nn.silu(c) @ w_mod[l] + b_mod[l])[:, None, :], 6, axis=-1)
        mod_c = jnp.split((jax.nn.silu(c_ctx) @ w_mod[l] + b_mod[l])[None, None, :], 6, axis=-1)
        proj = lambda h: project_heads(h, w_in[l], q_norm[l], k_norm[l],
                                       w_gate_f[l], b_gate_f[l], w_gate_b[l], b_gate_b[l])

        qa, ka, va, qb, kb, vb, rb, gf, gb, gate_a, gate_b = proj(modulate(x, g_mix[l], mod_x[0], mod_x[1]))
        qa, ka = apply_rope(qa, cos, sin), apply_rope(ka, cos, sin)
        cqa, cka, cva, cqb, ckb, cvb, crb, cgf, cgb, cgate_a, cgate_b = proj(
            modulate(ctx, g_mix[l], mod_c[0], mod_c[1]))

        o_attn = windowed_attention(qa, ka, va, cka, cva, attn_sink[l])
        sf, sb, o_gla_c = context_gla(cqb, ckb, cvb, cgf, cgb, need_out=not last)
        o_gla = latent_gla(qb, kb, vb, gf, gb, sf, sb)
        x = x + mod_x[2] * merge_branches(o_attn, o_gla, rb, gate_a, gate_b, gla_norm[l],
                                          w_attn_o[l], w_gla_o[l], w_out[l])
        x = x + mod_x[5] * conv_ffn(modulate(x, g_ffn[l], mod_x[3], mod_x[4]),
                                    w_up[l], conv_w[l], conv_b[l], w_down[l])

        if not last:
            o_attn_c = context_attention(cqa, cka, cva, attn_sink[l])
            ctx = ctx + mod_c[2] * merge_branches(o_attn_c, o_gla_c, crb, cgate_a, cgate_b, gla_norm[l],
                                                  w_attn_o[l], w_gla_o[l], w_out[l])
            ctx = ctx + mod_c[5] * conv_ffn(modulate(ctx, g_ffn[l], mod_c[3], mod_c[4]),
                                            w_up[l], conv_w[l], conv_b[l], w_down[l])
    return x


import jax as _jax
import jax.numpy as _jnp

TWIN_FORMAT = 'train_step'
FWD_PARAMS = ['x', 'c', 'ctx', 'c_ctx', 'w_mod', 'b_mod', 'g_mix', 'w_in', 'q_norm', 'k_norm', 'attn_sink', 'w_gate_f', 'b_gate_f', 'w_gate_b', 'b_gate_b', 'gla_norm', 'w_attn_o', 'w_gla_o', 'w_out', 'g_ffn', 'w_up', 'conv_w', 'conv_b', 'w_down']
TWIN_WEIGHTS = ['c_ctx', 'w_mod', 'b_mod', 'g_mix', 'w_in', 'q_norm', 'k_norm', 'attn_sink', 'w_gate_f', 'b_gate_f', 'w_gate_b', 'b_gate_b', 'gla_norm', 'w_attn_o', 'w_gla_o', 'w_out', 'g_ffn', 'w_up', 'conv_w', 'conv_b', 'w_down']
TWIN_DIFF_INPUT = 'x'
TWIN_INPUTS = ['x', 'c', 'ctx', 'c_ctx', 'w_mod', 'b_mod', 'g_mix', 'w_in', 'q_norm', 'k_norm', 'attn_sink', 'w_gate_f', 'b_gate_f', 'w_gate_b', 'b_gate_b', 'gla_norm', 'w_attn_o', 'w_gla_o', 'w_out', 'g_ffn', 'w_up', 'conv_w', 'conv_b', 'w_down', 'loss_target', 'm_c_ctx', 'm_w_mod', 'm_b_mod', 'm_g_mix', 'm_w_in', 'm_q_norm', 'm_k_norm', 'm_attn_sink', 'm_w_gate_f', 'm_b_gate_f', 'm_w_gate_b', 'm_b_gate_b', 'm_gla_norm', 'm_w_attn_o', 'm_w_gla_o', 'm_w_out', 'm_g_ffn', 'm_w_up', 'm_conv_w', 'm_conv_b', 'm_w_down', 'v_c_ctx', 'v_w_mod', 'v_b_mod', 'v_g_mix', 'v_w_in', 'v_q_norm', 'v_k_norm', 'v_attn_sink', 'v_w_gate_f', 'v_b_gate_f', 'v_w_gate_b', 'v_b_gate_b', 'v_gla_norm', 'v_w_attn_o', 'v_w_gla_o', 'v_w_out', 'v_g_ffn', 'v_w_up', 'v_conv_w', 'v_conv_b', 'v_w_down']
TWIN_OUTPUTS = ['loss', 'grad_x', 'grad_c_ctx', 'grad_w_mod', 'grad_b_mod', 'grad_g_mix', 'grad_w_in', 'grad_q_norm', 'grad_k_norm', 'grad_attn_sink', 'grad_w_gate_f', 'grad_b_gate_f', 'grad_w_gate_b', 'grad_b_gate_b', 'grad_gla_norm', 'grad_w_attn_o', 'grad_w_gla_o', 'grad_w_out', 'grad_g_ffn', 'grad_w_up', 'grad_conv_w', 'grad_conv_b', 'grad_w_down', 'delta_c_ctx', 'delta_w_mod', 'delta_b_mod', 'delta_g_mix', 'delta_w_in', 'delta_q_norm', 'delta_k_norm', 'delta_attn_sink', 'delta_w_gate_f', 'delta_b_gate_f', 'delta_w_gate_b', 'delta_b_gate_b', 'delta_gla_norm', 'delta_w_attn_o', 'delta_w_gla_o', 'delta_w_out', 'delta_g_ffn', 'delta_w_up', 'delta_conv_w', 'delta_conv_b', 'delta_w_down', 'new_m_c_ctx', 'new_m_w_mod', 'new_m_b_mod', 'new_m_g_mix', 'new_m_w_in', 'new_m_q_norm', 'new_m_k_norm', 'new_m_attn_sink', 'new_m_w_gate_f', 'new_m_b_gate_f', 'new_m_w_gate_b', 'new_m_b_gate_b', 'new_m_gla_norm', 'new_m_w_attn_o', 'new_m_w_gla_o', 'new_m_w_out', 'new_m_g_ffn', 'new_m_w_up', 'new_m_conv_w', 'new_m_conv_b', 'new_m_w_down', 'new_v_c_ctx', 'new_v_w_mod', 'new_v_b_mod', 'new_v_g_mix', 'new_v_w_in', 'new_v_q_norm', 'new_v_k_norm', 'new_v_attn_sink', 'new_v_w_gate_f', 'new_v_b_gate_f', 'new_v_w_gate_b', 'new_v_b_gate_b', 'new_v_gla_norm', 'new_v_w_attn_o', 'new_v_w_gla_o', 'new_v_w_out', 'new_v_g_ffn', 'new_v_w_up', 'new_v_conv_w', 'new_v_conv_b', 'new_v_w_down']
TWIN_LEAF_KINDS = {'loss': 'loss', 'grad_x': 'grad_x', 'grad_c_ctx': 'grad_w', 'grad_w_mod': 'grad_w', 'grad_b_mod': 'grad_w', 'grad_g_mix': 'grad_w', 'grad_w_in': 'grad_w', 'grad_q_norm': 'grad_w', 'grad_k_norm': 'grad_w', 'grad_attn_sink': 'grad_w', 'grad_w_gate_f': 'grad_w', 'grad_b_gate_f': 'grad_w', 'grad_w_gate_b': 'grad_w', 'grad_b_gate_b': 'grad_w', 'grad_gla_norm': 'grad_w', 'grad_w_attn_o': 'grad_w', 'grad_w_gla_o': 'grad_w', 'grad_w_out': 'grad_w', 'grad_g_ffn': 'grad_w', 'grad_w_up': 'grad_w', 'grad_conv_w': 'grad_w', 'grad_conv_b': 'grad_w', 'grad_w_down': 'grad_w', 'delta_c_ctx': 'delta_w', 'delta_w_mod': 'delta_w', 'delta_b_mod': 'delta_w', 'delta_g_mix': 'delta_w', 'delta_w_in': 'delta_w', 'delta_q_norm': 'delta_w', 'delta_k_norm': 'delta_w', 'delta_attn_sink': 'delta_w', 'delta_w_gate_f': 'delta_w', 'delta_b_gate_f': 'delta_w', 'delta_w_gate_b': 'delta_w', 'delta_b_gate_b': 'delta_w', 'delta_gla_norm': 'delta_w', 'delta_w_attn_o': 'delta_w', 'delta_w_gla_o': 'delta_w', 'delta_w_out': 'delta_w', 'delta_g_ffn': 'delta_w', 'delta_w_up': 'delta_w', 'delta_conv_w': 'delta_w', 'delta_conv_b': 'delta_w', 'delta_w_down': 'delta_w', 'new_m_c_ctx': 'new_m', 'new_m_w_mod': 'new_m', 'new_m_b_mod': 'new_m', 'new_m_g_mix': 'new_m', 'new_m_w_in': 'new_m', 'new_m_q_norm': 'new_m', 'new_m_k_norm': 'new_m', 'new_m_attn_sink': 'new_m', 'new_m_w_gate_f': 'new_m', 'new_m_b_gate_f': 'new_m', 'new_m_w_gate_b': 'new_m', 'new_m_b_gate_b': 'new_m', 'new_m_gla_norm': 'new_m', 'new_m_w_attn_o': 'new_m', 'new_m_w_gla_o': 'new_m', 'new_m_w_out': 'new_m', 'new_m_g_ffn': 'new_m', 'new_m_w_up': 'new_m', 'new_m_conv_w': 'new_m', 'new_m_conv_b': 'new_m', 'new_m_w_down': 'new_m', 'new_v_c_ctx': 'new_v', 'new_v_w_mod': 'new_v', 'new_v_b_mod': 'new_v', 'new_v_g_mix': 'new_v', 'new_v_w_in': 'new_v', 'new_v_q_norm': 'new_v', 'new_v_k_norm': 'new_v', 'new_v_attn_sink': 'new_v', 'new_v_w_gate_f': 'new_v', 'new_v_b_gate_f': 'new_v', 'new_v_w_gate_b': 'new_v', 'new_v_b_gate_b': 'new_v', 'new_v_gla_norm': 'new_v', 'new_v_w_attn_o': 'new_v', 'new_v_w_gla_o': 'new_v', 'new_v_w_out': 'new_v', 'new_v_g_ffn': 'new_v', 'new_v_w_up': 'new_v', 'new_v_conv_w': 'new_v', 'new_v_conv_b': 'new_v', 'new_v_w_down': 'new_v'}


def _forward(args):
    return _fwd_reference(*[args[k] for k in FWD_PARAMS])


def _output_shape():
    def fwd():
        inp = _fwd_setup_inputs(0)
        return _fwd_reference(*[inp[k] for k in FWD_PARAMS])
    out = _jax.eval_shape(fwd)
    return out.shape, out.dtype

N_MICROBATCH = 1
ADAM_LR = 0.001
ADAM_B1 = 0.9
ADAM_B2 = 0.999
ADAM_EPS = 1e-08
ADAM_WD = 0.01
ADAM_STEP = 10
PER_EXAMPLE_BATCH_AXIS = {'x': 0, 'c': 0, 'ctx': 0, 'loss_target': 0}
SHARED_INPUTS = []
_WEIGHT_DTYPES = {'c_ctx': _jnp.float32, 'w_mod': _jnp.float32, 'b_mod': _jnp.float32, 'g_mix': _jnp.float32, 'w_in': _jnp.float32, 'q_norm': _jnp.float32, 'k_norm': _jnp.float32, 'attn_sink': _jnp.float32, 'w_gate_f': _jnp.float32, 'b_gate_f': _jnp.float32, 'w_gate_b': _jnp.float32, 'b_gate_b': _jnp.float32, 'gla_norm': _jnp.float32, 'w_attn_o': _jnp.float32, 'w_gla_o': _jnp.float32, 'w_out': _jnp.float32, 'g_ffn': _jnp.float32, 'w_up': _jnp.float32, 'conv_w': _jnp.float32, 'conv_b': _jnp.float32, 'w_down': _jnp.float32}
MOMENT_SCALE = {'c_ctx': 6.425253e-02, 'w_mod': 5.909014e-01, 'b_mod': 1.647830e+00, 'g_mix': 4.237352e-01, 'w_in': 2.711691e-02, 'q_norm': 7.231045e-02, 'k_norm': 7.308432e-02, 'attn_sink': 8.880213e-03, 'w_gate_f': 3.683477e-03, 'b_gate_f': 9.825175e-03, 'w_gate_b': 3.431781e-03, 'b_gate_b': 9.652250e-03, 'gla_norm': 1.484431e+00, 'w_attn_o': 3.926706e-02, 'w_gla_o': 2.230808e-02, 'w_out': 3.298916e-02, 'g_ffn': 3.176731e+00, 'w_up': 7.025529e-02, 'conv_w': 4.653724e-01, 'conv_b': 3.773585e-01, 'w_down': 4.907548e-02}


def _to_microbatches(a, axis):
    t = _jnp.moveaxis(a, axis, 0)
    t = t.reshape((N_MICROBATCH, t.shape[0] // N_MICROBATCH) + t.shape[1:])
    return _jnp.moveaxis(t, 1, axis + 1)


def setup_inputs(seed: int = 0) -> dict:
    inp = _fwd_setup_inputs(seed)
    key = _jax.random.fold_in(_jax.random.key(seed), 7919)
    shape, _ = _output_shape()
    out = dict(inp)
    out["loss_target"] = _jax.random.normal(_jax.random.fold_in(key, 0), shape, _jnp.float32)
    for i, name in enumerate(TWIN_WEIGHTS):
        w = inp[name].astype(_jnp.float32)
        if MOMENT_SCALE is None:
            s = _jnp.sqrt(_jnp.mean(_jnp.square(w)) + 1e-30)
        else:
            s = MOMENT_SCALE[name]
        km, kv = _jax.random.split(_jax.random.fold_in(key, i + 1))
        out[name] = w
        out["m_" + name] = s * _jax.random.normal(km, w.shape, _jnp.float32)
        out["v_" + name] = (s * s) * _jax.random.uniform(kv, w.shape, _jnp.float32, 0.5, 1.5)
    if N_MICROBATCH > 1:
        for name, axis in PER_EXAMPLE_BATCH_AXIS.items():
            out[name] = _to_microbatches(out[name], axis)
    return {'x': out['x'], 'c': out['c'], 'ctx': out['ctx'], 'c_ctx': out['c_ctx'], 'w_mod': out['w_mod'], 'b_mod': out['b_mod'], 'g_mix': out['g_mix'], 'w_in': out['w_in'], 'q_norm': out['q_norm'], 'k_norm': out['k_norm'], 'attn_sink': out['attn_sink'], 'w_gate_f': out['w_gate_f'], 'b_gate_f': out['b_gate_f'], 'w_gate_b': out['w_gate_b'], 'b_gate_b': out['b_gate_b'], 'gla_norm': out['gla_norm'], 'w_attn_o': out['w_attn_o'], 'w_gla_o': out['w_gla_o'], 'w_out': out['w_out'], 'g_ffn': out['g_ffn'], 'w_up': out['w_up'], 'conv_w': out['conv_w'], 'conv_b': out['conv_b'], 'w_down': out['w_down'], 'loss_target': out['loss_target'], 'm_c_ctx': out['m_c_ctx'], 'm_w_mod': out['m_w_mod'], 'm_b_mod': out['m_b_mod'], 'm_g_mix': out['m_g_mix'], 'm_w_in': out['m_w_in'], 'm_q_norm': out['m_q_norm'], 'm_k_norm': out['m_k_norm'], 'm_attn_sink': out['m_attn_sink'], 'm_w_gate_f': out['m_w_gate_f'], 'm_b_gate_f': out['m_b_gate_f'], 'm_w_gate_b': out['m_w_gate_b'], 'm_b_gate_b': out['m_b_gate_b'], 'm_gla_norm': out['m_gla_norm'], 'm_w_attn_o': out['m_w_attn_o'], 'm_w_gla_o': out['m_w_gla_o'], 'm_w_out': out['m_w_out'], 'm_g_ffn': out['m_g_ffn'], 'm_w_up': out['m_w_up'], 'm_conv_w': out['m_conv_w'], 'm_conv_b': out['m_conv_b'], 'm_w_down': out['m_w_down'], 'v_c_ctx': out['v_c_ctx'], 'v_w_mod': out['v_w_mod'], 'v_b_mod': out['v_b_mod'], 'v_g_mix': out['v_g_mix'], 'v_w_in': out['v_w_in'], 'v_q_norm': out['v_q_norm'], 'v_k_norm': out['v_k_norm'], 'v_attn_sink': out['v_attn_sink'], 'v_w_gate_f': out['v_w_gate_f'], 'v_b_gate_f': out['v_b_gate_f'], 'v_w_gate_b': out['v_w_gate_b'], 'v_b_gate_b': out['v_b_gate_b'], 'v_gla_norm': out['v_gla_norm'], 'v_w_attn_o': out['v_w_attn_o'], 'v_w_gla_o': out['v_w_gla_o'], 'v_w_out': out['v_w_out'], 'v_g_ffn': out['v_g_ffn'], 'v_w_up': out['v_w_up'], 'v_conv_w': out['v_conv_w'], 'v_conv_b': out['v_conv_b'], 'v_w_down': out['v_w_down']}


def _loss(weights, diff, rest, loss_target):
    with _jax.named_scope("forward"):
        args = {**rest, TWIN_DIFF_INPUT: diff, **{k: w.astype(_WEIGHT_DTYPES[k]) for k, w in weights.items()}}
        y = _forward(args)
    with _jax.named_scope("loss_head"):
        err = _jnp.square(y.astype(_jnp.float32) - loss_target)
        return 0.5 * _jnp.sum(_jnp.mean(err, axis=-1)) if err.ndim else 0.5 * err


def _adamw(w, g, m, v):
    m = ADAM_B1 * m + (1.0 - ADAM_B1) * g
    v = ADAM_B2 * v + (1.0 - ADAM_B2) * _jnp.square(g)
    m_hat = m / (1.0 - ADAM_B1 ** ADAM_STEP)
    v_hat = v / (1.0 - ADAM_B2 ** ADAM_STEP)
    delta = -ADAM_LR * (m_hat / (_jnp.sqrt(v_hat) + ADAM_EPS) + ADAM_WD * w)
    return delta, m, v


def reference(x, c, ctx, c_ctx, w_mod, b_mod, g_mix, w_in, q_norm, k_norm, attn_sink, w_gate_f, b_gate_f, w_gate_b, b_gate_b, gla_norm, w_attn_o, w_gla_o, w_out, g_ffn, w_up, conv_w, conv_b, w_down, loss_target, m_c_ctx, m_w_mod, m_b_mod, m_g_mix, m_w_in, m_q_norm, m_k_norm, m_attn_sink, m_w_gate_f, m_b_gate_f, m_w_gate_b, m_b_gate_b, m_gla_norm, m_w_attn_o, m_w_gla_o, m_w_out, m_g_ffn, m_w_up, m_conv_w, m_conv_b, m_w_down, v_c_ctx, v_w_mod, v_b_mod, v_g_mix, v_w_in, v_q_norm, v_k_norm, v_attn_sink, v_w_gate_f, v_b_gate_f, v_w_gate_b, v_b_gate_b, v_gla_norm, v_w_attn_o, v_w_gla_o, v_w_out, v_g_ffn, v_w_up, v_conv_w, v_conv_b, v_w_down):
    given = dict(x=x, c=c, ctx=ctx, c_ctx=c_ctx, w_mod=w_mod, b_mod=b_mod, g_mix=g_mix, w_in=w_in, q_norm=q_norm, k_norm=k_norm, attn_sink=attn_sink, w_gate_f=w_gate_f, b_gate_f=b_gate_f, w_gate_b=w_gate_b, b_gate_b=b_gate_b, gla_norm=gla_norm, w_attn_o=w_attn_o, w_gla_o=w_gla_o, w_out=w_out, g_ffn=g_ffn, w_up=w_up, conv_w=conv_w, conv_b=conv_b, w_down=w_down, loss_target=loss_target, m_c_ctx=m_c_ctx, m_w_mod=m_w_mod, m_b_mod=m_b_mod, m_g_mix=m_g_mix, m_w_in=m_w_in, m_q_norm=m_q_norm, m_k_norm=m_k_norm, m_attn_sink=m_attn_sink, m_w_gate_f=m_w_gate_f, m_b_gate_f=m_b_gate_f, m_w_gate_b=m_w_gate_b, m_b_gate_b=m_b_gate_b, m_gla_norm=m_gla_norm, m_w_attn_o=m_w_attn_o, m_w_gla_o=m_w_gla_o, m_w_out=m_w_out, m_g_ffn=m_g_ffn, m_w_up=m_w_up, m_conv_w=m_conv_w, m_conv_b=m_conv_b, m_w_down=m_w_down, v_c_ctx=v_c_ctx, v_w_mod=v_w_mod, v_b_mod=v_b_mod, v_g_mix=v_g_mix, v_w_in=v_w_in, v_q_norm=v_q_norm, v_k_norm=v_k_norm, v_attn_sink=v_attn_sink, v_w_gate_f=v_w_gate_f, v_b_gate_f=v_b_gate_f, v_w_gate_b=v_w_gate_b, v_b_gate_b=v_b_gate_b, v_gla_norm=v_gla_norm, v_w_attn_o=v_w_attn_o, v_w_gla_o=v_w_gla_o, v_w_out=v_w_out, v_g_ffn=v_g_ffn, v_w_up=v_w_up, v_conv_w=v_conv_w, v_conv_b=v_conv_b, v_w_down=v_w_down)
    weights = {n: given[n] for n in TWIN_WEIGHTS}
    shared = {n: given[n] for n in SHARED_INPUTS}
    per_example = {n: given[n] for n in ['x', 'c', 'ctx']}
    grad_fn = _jax.value_and_grad(_loss, argnums=(0, 1))

    def one_microbatch(ex, loss_target):
        ex = dict(ex)
        diff = ex.pop(TWIN_DIFF_INPUT)
        return grad_fn(weights, diff, {**shared, **ex}, loss_target)

    if N_MICROBATCH == 1:
        loss, (grad_w, grad_x) = one_microbatch(per_example, given["loss_target"])
    else:
        def body(carry, xs):
            loss_sum, grad_sum = carry
            l_k, (gw_k, gx_k) = one_microbatch(xs[0], xs[1])
            with _jax.named_scope("update"):
                return (loss_sum + l_k, _jax.tree.map(_jnp.add, grad_sum, gw_k)), gx_k

        init = (_jnp.zeros((), _jnp.float32), _jax.tree.map(_jnp.zeros_like, weights))
        (loss, grad_w), grad_x = _jax.lax.scan(body, init, (per_example, given["loss_target"]))
    with _jax.named_scope("update"):
        delta_w, new_m, new_v = {}, {}, {}
        for n in TWIN_WEIGHTS:
            delta_w[n], new_m[n], new_v[n] = _adamw(weights[n], grad_w[n], given["m_" + n], given["v_" + n])
    return (loss, grad_x, *[grad_w[n] for n in TWIN_WEIGHTS], *[delta_w[n] for n in TWIN_WEIGHTS],
            *[new_m[n] for n in TWIN_WEIGHTS], *[new_v[n] for n in TWIN_WEIGHTS])
```

```python
import functools
import math

import jax
import jax.numpy as jnp
import numpy as np
from jax import lax
from jax.experimental import pallas as pl
from jax.experimental.pallas import tpu as pltpu

F32 = jnp.float32
BF16 = jnp.bfloat16

NDEV = 8
EPS = 1e-6
WINDOW = 128
BLOCK = 128
GRID_W = 64
ROPE_THETA = 10000.0
GLA_CHUNK = 64
GLA_GATE_NORM = 16.0
ADAM_LR = 0.001
ADAM_B1 = 0.9
ADAM_B2 = 0.999
ADAM_EPS = 1e-08
ADAM_WD = 0.01
ADAM_STEP = 10
V7X_VMEM_LIMIT = 56 * 1024 * 1024
NEG = -1e30

NN = ((1,), (0,))
NT = ((1,), (1,))
TN = ((0,), (0,))


def _dot(a, b, dims):
    return lax.dot_general(a, b, (dims, ((), ())), preferred_element_type=F32)


def _cp(sem):
    return pltpu.CompilerParams(dimension_semantics=sem, vmem_limit_bytes=V7X_VMEM_LIMIT)


def _pick(n, cands):
    for c in cands:
        if n % c == 0:
            return c
    return n


def _sig(x):
    return 1.0 / (1.0 + jnp.exp(-x))


def _rstd(x):
    return lax.rsqrt(jnp.mean(x * x, axis=-1, keepdims=True) + EPS)


def exchange(srcs, bcast, name):
    n = len(srcs)
    out_shape = [jax.ShapeDtypeStruct((NDEV,) + (s.shape if bcast else s.shape[1:]), s.dtype) for s in srcs]

    def body(*refs):
        src, dst = refs[:n], refs[n:2 * n]
        send_sems, recv_sems, loc_sems = refs[2 * n:]
        x, y, c = lax.axis_index("x"), lax.axis_index("y"), lax.axis_index("c")
        me = 4 * x + 2 * y + c
        copies = []
        for a in range(n):
            own = src[a] if bcast else src[a].at[me]
            cp = pltpu.make_async_copy(own, dst[a].at[me], loc_sems.at[a])
            cp.start()
            copies.append(cp)
        for k in range(1, NDEV):
            px, py, pc = x ^ ((k >> 2) & 1), y ^ ((k >> 1) & 1), c ^ (k & 1)
            peer = 4 * px + 2 * py + pc
            for a in range(n):
                cp = pltpu.make_async_remote_copy(
                    src_ref=src[a] if bcast else src[a].at[peer],
                    dst_ref=dst[a].at[me],
                    send_sem=send_sems.at[a, k - 1],
                    recv_sem=recv_sems.at[a, k - 1],
                    device_id=(px, py, pc),
                    device_id_type=pl.DeviceIdType.MESH,
                )
                cp.start()
                copies.append(cp)
        for cp in copies:
            cp.wait()

    any_spec = pl.BlockSpec(memory_space=pl.ANY)
    return pl.pallas_call(
        body,
        out_shape=out_shape,
        in_specs=[any_spec] * n,
        out_specs=[any_spec] * n,
        scratch_shapes=[
            pltpu.SemaphoreType.DMA((n, NDEV - 1)),
            pltpu.SemaphoreType.DMA((n, NDEV - 1)),
            pltpu.SemaphoreType.DMA((n,)),
        ],
        name=name,
    )(*srcs)


def matmul(a, b, mode, out_dtype, name, add=None):
    if mode == "nn":
        (M, K), N = a.shape, b.shape[1]
    elif mode == "nt":
        (M, K), N = a.shape, b.shape[0]
    else:
        (K, M), N = a.shape, b.shape[1]
    tm = _pick(M, (1024, 512, 256, 128))
    tn = _pick(N, (1024, 512, 256, 128))
    tk = _pick(K, (512, 256, 128))
    nk = K // tk
    dims = {"nn": NN, "nt": NT, "tn": TN}[mode]

    def body(*refs):
        if add is None:
            a_ref, b_ref, o_ref, acc = refs
            c_ref = None
        else:
            a_ref, b_ref, c_ref, o_ref, acc = refs
        k = pl.program_id(2)

        @pl.when(k == 0)
        def _():
            acc[...] = jnp.zeros_like(acc)

        acc[...] += _dot(a_ref[...].astype(BF16), b_ref[...].astype(BF16), dims)

        @pl.when(k == nk - 1)
        def _():
            r = acc[...]
            if c_ref is not None:
                r = r + c_ref[...].astype(F32)
            o_ref[...] = r.astype(o_ref.dtype)

    a_spec = pl.BlockSpec((tk, tm), lambda i, j, k: (k, i)) if mode == "tn" else pl.BlockSpec((tm, tk), lambda i, j, k: (i, k))
    b_spec = pl.BlockSpec((tn, tk), lambda i, j, k: (j, k)) if mode == "nt" else pl.BlockSpec((tk, tn), lambda i, j, k: (k, j))
    o_spec = pl.BlockSpec((tm, tn), lambda i, j, k: (i, j))
    ins, specs = [a, b], [a_spec, b_spec]
    if add is not None:
        ins.append(add)
        specs.append(o_spec)
    return pl.pallas_call(
        body,
        grid=(M // tm, N // tn, nk),
        in_specs=specs,
        out_specs=o_spec,
        out_shape=jax.ShapeDtypeStruct((M, N), out_dtype),
        scratch_shapes=[pltpu.VMEM((tm, tn), F32)],
        compiler_params=_cp(("parallel", "parallel", "arbitrary")),
        name=name,
    )(*ins)


def rowwise(fn, tiled, full, out_tiled, out_acc, tile, name):
    tiled = [t if isinstance(t, tuple) else (t, t.shape[1], 0) for t in tiled]
    rows = tiled[0][0].shape[0]
    tile = min(tile, rows)
    assert rows % tile == 0
    nt, nf, no = len(tiled), len(full), len(out_tiled)

    def body(*refs):
        ins = [r[...] for r in refs[:nt + nf]]
        res = fn(*ins)
        if not isinstance(res, (tuple, list)):
            res = (res,)
        outs = refs[nt + nf:]
        for r, v in zip(outs[:no], res[:no]):
            r[...] = v.astype(r.dtype)
        if out_acc:
            @pl.when(pl.program_id(0) == 0)
            def _():
                for r in outs[no:]:
                    r[...] = jnp.zeros_like(r)

            for r, v in zip(outs[no:], res[no:]):
                r[...] += v

    in_specs = [pl.BlockSpec((tile, w), lambda i, cb=cb: (i, cb)) for (_, w, cb) in tiled]
    in_specs += [pl.BlockSpec(f.shape, lambda i, nd=f.ndim: (0,) * nd) for f in full]
    out_specs = [pl.BlockSpec((tile, w), lambda i: (i, 0)) for (w, _) in out_tiled]
    out_specs += [pl.BlockSpec(s, lambda i, nd=len(s): (0,) * nd) for s in out_acc]
    out_shape = [jax.ShapeDtypeStruct((rows, w), dt) for (w, dt) in out_tiled]
    out_shape += [jax.ShapeDtypeStruct(s, F32) for s in out_acc]
    res = pl.pallas_call(
        body,
        grid=(rows // tile,),
        in_specs=in_specs,
        out_specs=out_specs,
        out_shape=out_shape,
        compiler_params=_cp(("arbitrary",) if out_acc else ("parallel",)),
        name=name,
    )(*[t[0] for t in tiled], *full)
    return res


def adam_reduce(parts, w, m, v, name):
    P, R, C = parts.shape
    tr = _pick(R, (64, 32, 16, 8))
    c1 = 1.0 - ADAM_B1 ** ADAM_STEP
    c2 = 1.0 - ADAM_B2 ** ADAM_STEP

    def body(p_ref, w_ref, m_ref, v_ref, g_ref, d_ref, nm_ref, nv_ref):
        g = p_ref[0]
        for j in range(1, P):
            g = g + p_ref[j]
        mm = ADAM_B1 * m_ref[...] + (1.0 - ADAM_B1) * g
        vv = ADAM_B2 * v_ref[...] + (1.0 - ADAM_B2) * (g * g)
        m_hat = mm / c1
        v_hat = vv / c2
        g_ref[...] = g
        d_ref[...] = -ADAM_LR * (m_hat / (jnp.sqrt(v_hat) + ADAM_EPS) + ADAM_WD * w_ref[...])
        nm_ref[...] = mm
        nv_ref[...] = vv

    spec = pl.BlockSpec((tr, C), lambda i: (i, 0))
    return pl.pallas_call(
        body,
        grid=(R // tr,),
        in_specs=[pl.BlockSpec((P, tr, C), lambda i: (0, i, 0)), spec, spec, spec],
        out_specs=[spec] * 4,
        out_shape=[jax.ShapeDtypeStruct((R, C), F32)] * 4,
        compiler_params=_cp(("parallel",)),
        name=name,
    )(parts, w, m, v)


def reduce_parts(parts, name):
    P, R, C = parts.shape
    tr = _pick(R, (64, 32, 16, 8))

    def body(p_ref, g_ref):
        g = p_ref[0]
        for j in range(1, P):
            g = g + p_ref[j]
        g_ref[...] = g

    return pl.pallas_call(
        body,
        grid=(R // tr,),
        in_specs=[pl.BlockSpec((P, tr, C), lambda i: (0, i, 0))],
        out_specs=pl.BlockSpec((tr, C), lambda i: (i, 0)),
        out_shape=jax.ShapeDtypeStruct((R, C), F32),
        compiler_params=_cp(("parallel",)),
        name=name,
    )(parts)


def pack(arrs):
    flat = jnp.concatenate([a.reshape(-1).astype(F32) for a in arrs])
    n = flat.shape[0]
    padded = -(-n // 1024) * 1024
    return jnp.pad(flat, (0, padded - n)).reshape(padded // 128, 128)


def unpack(slab, shapes):
    flat = slab.reshape(-1)
    out, off = [], 0
    for s in shapes:
        size = int(np.prod(s))
        out.append(flat[off:off + size].reshape(s))
        off += size
    return out


def modulate_fwd(x, g, sh, sc, name):
    def fn(x, g, sh, sc):
        return x * _rstd(x) * g * (1.0 + sc) + sh

    return rowwise(fn, [x], [g, sh, sc], [(x.shape[1], BF16)], [], 256, name)[0]


def norm_rope_fwd(p, width, cb, w, cosf, sinf, hd, name):
    nh = width // hd

    def fn(x, cosf, sinf, w):
        outs = []
        for h in range(nh):
            xh = x[:, h * hd:(h + 1) * hd]
            y = xh * _rstd(xh) * w
            outs.append(y * cosf + pltpu.roll(y, hd // 2, 1) * sinf)
        return jnp.concatenate(outs, axis=1) if nh > 1 else outs[0]

    return rowwise(fn, [(p, width, cb), cosf, sinf], [w], [(width, BF16)], [], 256, name)[0]


def norm_rope_bwd(p, width, cb, d, w, cosf, sinf, hd, name):
    nh = width // hd

    def fn(x, d, cosf, sinf, w):
        outs = []
        dw = jnp.zeros((1, hd), F32)
        for h in range(nh):
            xh = x[:, h * hd:(h + 1) * hd]
            dh = d[:, h * hd:(h + 1) * hd].astype(F32)
            r = _rstd(xh)
            n = xh * r
            dy = dh * cosf + pltpu.roll(dh * sinf, hd // 2, 1)
            dw = dw + jnp.sum(dy * n, axis=0, keepdims=True)
            dn = dy * w
            outs.append(r * (dn - n * jnp.mean(dn * n, axis=-1, keepdims=True)))
        return (jnp.concatenate(outs, axis=1) if nh > 1 else outs[0]), dw

    return rowwise(fn, [(p, width, cb), d, cosf, sinf], [w], [(width, BF16)], [(1, hd)], 256, name)


def attention_fwd(qr, kr, pkv, kcr, pkv_c, sink, hkv, hd, name):
    T, L = qr.shape[0], kcr.shape[0]
    G = qr.shape[1] // (hkv * hd)
    nb = T // BLOCK
    scale = hd ** -0.5

    def body(q_ref, kp, kc, kn, vp, vc, vn, ck_ref, cv_ref, sink_ref, o_ref, lse_ref):
        i = pl.program_id(1)
        kwin = jnp.concatenate([kp[...], kc[...], kn[...]], axis=0)
        vwin = jnp.concatenate([vp[...], vc[...], vn[...]], axis=0).astype(BF16)
        ck, cv = ck_ref[...], cv_ref[...].astype(BF16)
        row = lax.broadcasted_iota(jnp.int32, (BLOCK, 3 * BLOCK), 0)
        col = lax.broadcasted_iota(jnp.int32, (BLOCK, 3 * BLOCK), 1)
        rel = col - BLOCK - row
        valid = (jnp.abs(rel) <= WINDOW) & ((col >= BLOCK) | (i > 0)) & ((col < 2 * BLOCK) | (i < nb - 1))
        for g in range(G):
            q = q_ref[:, g * hd:(g + 1) * hd]
            s = jnp.where(valid, _dot(q, kwin, NT) * scale, NEG)
            sc = _dot(q, ck, NT) * scale
            sk = sink_ref[g]
            m = jnp.maximum(jnp.maximum(jnp.max(s, axis=1, keepdims=True), jnp.max(sc, axis=1, keepdims=True)), sk)
            p = jnp.exp(s - m)
            pc = jnp.exp(sc - m)
            den = jnp.exp(sk - m) + jnp.sum(p, axis=1, keepdims=True) + jnp.sum(pc, axis=1, keepdims=True)
            o = (_dot(p.astype(BF16), vwin, NN) + _dot(pc.astype(BF16), cv, NN)) / den
            o_ref[:, g * hd:(g + 1) * hd] = o.astype(o_ref.dtype)
            lse_ref[g] = m + jnp.log(den)

    kv_specs = [
        pl.BlockSpec((BLOCK, hd), lambda h, i: (jnp.maximum(i - 1, 0), h)),
        pl.BlockSpec((BLOCK, hd), lambda h, i: (i, h)),
        pl.BlockSpec((BLOCK, hd), lambda h, i: (jnp.minimum(i + 1, nb - 1), h)),
    ]
    v_specs = [
        pl.BlockSpec((BLOCK, hd), lambda h, i: (jnp.maximum(i - 1, 0), hkv + h)),
        pl.BlockSpec((BLOCK, hd), lambda h, i: (i, hkv + h)),
        pl.BlockSpec((BLOCK, hd), lambda h, i: (jnp.minimum(i + 1, nb - 1), hkv + h)),
    ]
    return pl.pallas_call(
        body,
        grid=(hkv, nb),
        in_specs=[pl.BlockSpec((BLOCK, G * hd), lambda h, i: (i, h))] + kv_specs + v_specs + [
            pl.BlockSpec((L, hd), lambda h, i: (0, h)),
            pl.BlockSpec((L, hd), lambda h, i: (0, hkv + h)),
            pl.BlockSpec((G, 1, 1), lambda h, i: (h, 0, 0)),
        ],
        out_specs=[
            pl.BlockSpec((BLOCK, G * hd), lambda h, i: (i, h)),
            pl.BlockSpec((G, BLOCK, 1), lambda h, i: (h, i, 0)),
        ],
        out_shape=[jax.ShapeDtypeStruct(qr.shape, BF16), jax.ShapeDtypeStruct((hkv * G, T, 1), F32)],
        compiler_params=_cp(("parallel", "parallel")),
        name=name,
    )(qr, kr, kr, kr, pkv, pkv, pkv, kcr, pkv_c, sink)


def attention_bwd_q(qr, kr, pkv, kcr, pkv_c, sink, do, o, lse, hkv, hd, name):
    T, L = qr.shape[0], kcr.shape[0]
    G = qr.shape[1] // (hkv * hd)
    nb = T // BLOCK
    scale = hd ** -0.5

    def body(q_ref, kp, kc, kn, vp, vc, vn, ck_ref, cv_ref, sink_ref, do_ref, o_ref, lse_ref,
             dq_ref, dck_ref, dcv_ref, dsink_ref):
        i = pl.program_id(1)

        @pl.when(i == 0)
        def _():
            dck_ref[...] = jnp.zeros_like(dck_ref)
            dcv_ref[...] = jnp.zeros_like(dcv_ref)
            dsink_ref[...] = jnp.zeros_like(dsink_ref)

        kwin = jnp.concatenate([kp[...], kc[...], kn[...]], axis=0)
        vwin = jnp.concatenate([vp[...], vc[...], vn[...]], axis=0).astype(BF16)
        ck, cv = ck_ref[...], cv_ref[...].astype(BF16)
        row = lax.broadcasted_iota(jnp.int32, (BLOCK, 3 * BLOCK), 0)
        col = lax.broadcasted_iota(jnp.int32, (BLOCK, 3 * BLOCK), 1)
        rel = col - BLOCK - row
        valid = (jnp.abs(rel) <= WINDOW) & ((col >= BLOCK) | (i > 0)) & ((col < 2 * BLOCK) | (i < nb - 1))
        dck = jnp.zeros((L, hd), F32)
        dcv = jnp.zeros((L, hd), F32)
        for g in range(G):
            q = q_ref[:, g * hd:(g + 1) * hd]
            dog = do_ref[:, g * hd:(g + 1) * hd]
            og = o_ref[:, g * hd:(g + 1) * hd]
            lg = lse_ref[g]
            sk = sink_ref[g]
            s = jnp.where(valid, _dot(q, kwin, NT) * scale, NEG)
            sc = _dot(q, ck, NT) * scale
            p = jnp.exp(s - lg)
            pc = jnp.exp(sc - lg)
            dr = jnp.sum(dog.astype(F32) * og.astype(F32), axis=1, keepdims=True)
            dp = _dot(dog, vwin, NT)
            dpc = _dot(dog, cv, NT)
            ds = (p * (dp - dr) * scale).astype(BF16)
            dsc = (pc * (dpc - dr) * scale).astype(BF16)
            dq_ref[:, g * hd:(g + 1) * hd] = _dot(ds, kwin, NN) + _dot(dsc, ck, NN)
            dck = dck + _dot(dsc, q, TN)
            dcv = dcv + _dot(pc.astype(BF16), dog, TN)
            dsink_ref[g] += -jnp.sum(jnp.exp(sk - lg) * dr, axis=0, keepdims=True)
        dck_ref[...] += dck
        dcv_ref[...] += dcv

    kv_specs = [
        pl.BlockSpec((BLOCK, hd), lambda h, i: (jnp.maximum(i - 1, 0), h)),
        pl.BlockSpec((BLOCK, hd), lambda h, i: (i, h)),
        pl.BlockSpec((BLOCK, hd), lambda h, i: (jnp.minimum(i + 1, nb - 1), h)),
    ]
    v_specs = [
        pl.BlockSpec((BLOCK, hd), lambda h, i: (jnp.maximum(i - 1, 0), hkv + h)),
        pl.BlockSpec((BLOCK, hd), lambda h, i: (i, hkv + h)),
        pl.BlockSpec((BLOCK, hd), lambda h, i: (jnp.minimum(i + 1, nb - 1), hkv + h)),
    ]
    qspec = pl.BlockSpec((BLOCK, G * hd), lambda h, i: (i, h))
    return pl.pallas_call(
        body,
        grid=(hkv, nb),
        in_specs=[qspec] + kv_specs + v_specs + [
            pl.BlockSpec((L, hd), lambda h, i: (0, h)),
            pl.BlockSpec((L, hd), lambda h, i: (0, hkv + h)),
            pl.BlockSpec((G, 1, 1), lambda h, i: (h, 0, 0)),
            qspec, qspec,
            pl.BlockSpec((G, BLOCK, 1), lambda h, i: (h, i, 0)),
        ],
        out_specs=[
            qspec,
            pl.BlockSpec((L, hd), lambda h, i: (0, h)),
            pl.BlockSpec((L, hd), lambda h, i: (0, h)),
            pl.BlockSpec((G, 1, 1), lambda h, i: (h, 0, 0)),
        ],
        out_shape=[
            jax.ShapeDtypeStruct(qr.shape, F32),
            jax.ShapeDtypeStruct((L, hkv * hd), F32),
            jax.ShapeDtypeStruct((L, hkv * hd), F32),
            jax.ShapeDtypeStruct((hkv * G, 1, 1), F32),
        ],
        compiler_params=_cp(("parallel", "arbitrary")),
        name=name,
    )(qr, kr, kr, kr, pkv, pkv, pkv, kcr, pkv_c, sink, do, o, lse)


def attention_bwd_kv(qr, kr, pkv, do, o, lse, hkv, hd, name):
    T = qr.shape[0]
    G = qr.shape[1] // (hkv * hd)
    nb = T // BLOCK
    scale = hd ** -0.5

    def body(k_ref, v_ref, *refs):
        qs, dos, os_, lses = refs[0:3], refs[3:6], refs[6:9], refs[9:12]
        dk_ref, dv_ref = refs[12:]
        j = pl.program_id(1)
        k = k_ref[...]
        v = v_ref[...].astype(BF16)
        row = lax.broadcasted_iota(jnp.int32, (BLOCK, BLOCK), 0)
        col = lax.broadcasted_iota(jnp.int32, (BLOCK, BLOCK), 1)
        dk = jnp.zeros((BLOCK, hd), F32)
        dv = jnp.zeros((BLOCK, hd), F32)
        for d in range(3):
            delta = d - 1
            iq = j + delta
            ok = (iq >= 0) & (iq < nb)
            rel = col - row - delta * BLOCK
            valid = (jnp.abs(rel) <= WINDOW) & ok
            for g in range(G):
                q = qs[d][:, g * hd:(g + 1) * hd]
                dog = dos[d][:, g * hd:(g + 1) * hd]
                og = os_[d][:, g * hd:(g + 1) * hd]
                lg = lses[d][g]
                s = jnp.where(valid, _dot(q, k, NT) * scale, NEG)
                p = jnp.exp(s - lg)
                dr = jnp.sum(dog.astype(F32) * og.astype(F32), axis=1, keepdims=True)
                dp = _dot(dog, v, NT)
                ds = (p * (dp - dr) * scale).astype(BF16)
                dk = dk + _dot(ds, q, TN)
                dv = dv + _dot(p.astype(BF16), dog, TN)
        dk_ref[...] = dk.astype(dk_ref.dtype)
        dv_ref[...] = dv.astype(dv_ref.dtype)

    def q3(width_block):
        return [
            pl.BlockSpec(width_block, lambda h, j: (jnp.maximum(j - 1, 0), h)),
            pl.BlockSpec(width_block, lambda h, j: (j, h)),
            pl.BlockSpec(width_block, lambda h, j: (jnp.minimum(j + 1, nb - 1), h)),
        ]

    lse3 = [
        pl.BlockSpec((G, BLOCK, 1), lambda h, j: (h, jnp.maximum(j - 1, 0), 0)),
        pl.BlockSpec((G, BLOCK, 1), lambda h, j: (h, j, 0)),
        pl.BlockSpec((G, BLOCK, 1), lambda h, j: (h, jnp.minimum(j + 1, nb - 1), 0)),
    ]
    qb = (BLOCK, G * hd)
    return pl.pallas_call(
        body,
        grid=(hkv, nb),
        in_specs=[pl.BlockSpec((BLOCK, hd), lambda h, j: (j, h)), pl.BlockSpec((BLOCK, hd), lambda h, j: (j, hkv + h))]
        + q3(qb) + q3(qb) + q3(qb) + lse3,
        out_specs=[pl.BlockSpec((BLOCK, hd), lambda h, j: (j, h))] * 2,
        out_shape=[jax.ShapeDtypeStruct((T, hkv * hd), BF16)] * 2,
        compiler_params=_cp(("parallel", "parallel")),
        name=name,
    )(kr, pkv, qr, qr, qr, do, do, do, o, o, o, lse, lse, lse)


def gate_fwd(plr, wf, wb, bf, bb, name):
    n = wf.shape[1]

    def fn(lr, wf, wb, bf, bb):
        lrb = lr.astype(BF16)
        outs = []
        for w, b in ((wf, bf), (wb, bb)):
            z = _dot(lrb, w.astype(BF16), NN) + b
            outs.append((jnp.minimum(z, 0.0) - jnp.log(1.0 + jnp.exp(-jnp.abs(z)))) / GLA_GATE_NORM)
        return outs

    return rowwise(fn, [plr], [wf, wb, bf, bb], [(n, F32), (n, F32)], [], 256, name)


def gate_bwd(plr, dgf, dgb, wf, wb, bf, bb, name):
    n = wf.shape[1]

    def fn(lr, dgf, dgb, wf, wb, bf, bb):
        lrb = lr.astype(BF16)
        dlr = jnp.zeros(lr.shape, F32)
        res = []
        for w, b, dg in ((wf, bf, dgf), (wb, bb, dgb)):
            wb16 = w.astype(BF16)
            z = _dot(lrb, wb16, NN) + b
            dz = dg * _sig(-z) / GLA_GATE_NORM
            dzb = dz.astype(BF16)
            dlr = dlr + _dot(dzb, wb16, NT)
            res += [_dot(lrb, dzb, TN), jnp.sum(dz, axis=0, keepdims=True)]
        return [dlr] + res

    return rowwise(fn, [plr, dgf, dgb], [wf, wb, bf, bb], [(128, BF16)],
                   [(128, n), (1, n), (128, n), (1, n)], 256, name)


def _tri_dot(tri_b, x):
    x1 = x.astype(BF16)
    r1 = x - x1.astype(F32)
    x2 = r1.astype(BF16)
    x3 = (r1 - x2.astype(F32)).astype(BF16)
    return _dot(tri_b, x1, NN) + _dot(tri_b, x2, NN) + _dot(tri_b, x3, NN)


def gla_fwd(pqk, pv, gl, s0, heads, reverse, name, o_add=None):
    T = pqk.shape[0]
    dk = pqk.shape[1] // (2 * heads)
    dv = pv.shape[1] // heads
    C = GLA_CHUNK
    nc = T // C
    qscale = dk ** -0.5

    def body(*refs):
        if o_add is None:
            q_ref, k_ref, v_ref, g_ref, s0_ref, o_ref, st_ref, sf_ref, S = refs
            oa_ref = None
        else:
            q_ref, k_ref, v_ref, g_ref, s0_ref, oa_ref, o_ref, st_ref, sf_ref, S = refs
        n = pl.program_id(1)

        @pl.when(n == 0)
        def _():
            S[...] = s0_ref[0]

        r = lax.broadcasted_iota(jnp.int32, (C, C), 0)
        c = lax.broadcasted_iota(jnp.int32, (C, C), 1)
        tri = (r <= c) if reverse else (r >= c)
        g = g_ref[...]
        b = _tri_dot(tri.astype(BF16), g)
        bl = jnp.sum(g, axis=0, keepdims=True)
        q = q_ref[...].astype(F32) * qscale
        k = k_ref[...].astype(F32)
        v = v_ref[...].astype(BF16)
        qe = (q * jnp.exp(b)).astype(BF16)
        ke = (k * jnp.exp(-b)).astype(BF16)
        kl = (k * jnp.exp(bl - b)).astype(BF16)
        st = S[...]
        st_ref[0, 0] = st
        a = jnp.where(tri, _dot(qe, ke, NT), 0.0)
        o = _dot(qe, st.astype(BF16), NT) + _dot(a.astype(BF16), v, NN)
        if oa_ref is not None:
            o = o + oa_ref[...]
        o_ref[...] = o
        snew = st * jnp.exp(bl) + _dot(v, kl, TN)
        S[...] = snew
        sf_ref[0] = snew

    def ci(n):
        return (nc - 1 - n) if reverse else n

    specs = [
        pl.BlockSpec((C, dk), lambda h, n: (ci(n), h)),
        pl.BlockSpec((C, dk), lambda h, n: (ci(n), heads + h)),
        pl.BlockSpec((C, dv), lambda h, n: (ci(n), h)),
        pl.BlockSpec((C, dk), lambda h, n: (ci(n), h)),
        pl.BlockSpec((1, dv, dk), lambda h, n: (h, 0, 0)),
    ]
    ins = [pqk, pqk, pv, gl, s0]
    if o_add is not None:
        specs.append(pl.BlockSpec((C, dv), lambda h, n: (ci(n), h)))
        ins.append(o_add)
    return pl.pallas_call(
        body,
        grid=(heads, nc),
        in_specs=specs,
        out_specs=[
            pl.BlockSpec((C, dv), lambda h, n: (ci(n), h)),
            pl.BlockSpec((1, 1, dv, dk), lambda h, n: (ci(n), h, 0, 0)),
            pl.BlockSpec((1, dv, dk), lambda h, n: (h, 0, 0)),
        ],
        out_shape=[
            jax.ShapeDtypeStruct((T, heads * dv), F32),
            jax.ShapeDtypeStruct((nc, heads, dv, dk), F32),
            jax.ShapeDtypeStruct((heads, dv, dk), F32),
        ],
        scratch_shapes=[pltpu.VMEM((dv, dk), F32)],
        compiler_params=_cp(("parallel", "arbitrary")),
        name=name,
    )(*ins)


def gla_bwd(pqk, pv, gl, states, do, dsf, heads, reverse, name, acc=None):
    T = pqk.shape[0]
    dk = pqk.shape[1] // (2 * heads)
    dv = pv.shape[1] // heads
    C = GLA_CHUNK
    nc = T // C
    qscale = dk ** -0.5

    def body(*refs):
        if acc is None:
            q_ref, k_ref, v_ref, g_ref, st_ref, do_ref, dsf_ref, dq_ref, dk_ref, dv_ref, dg_ref, ds0_ref, dS = refs
            aq = ak = av = None
        else:
            (q_ref, k_ref, v_ref, g_ref, st_ref, do_ref, dsf_ref, aq, ak, av,
             dq_ref, dk_ref, dv_ref, dg_ref, ds0_ref, dS) = refs
        n = pl.program_id(1)

        @pl.when(n == 0)
        def _():
            dS[...] = dsf_ref[0]

        r = lax.broadcasted_iota(jnp.int32, (C, C), 0)
        c = lax.broadcasted_iota(jnp.int32, (C, C), 1)
        tri = (r <= c) if reverse else (r >= c)
        tri_t = (r >= c) if reverse else (r <= c)
        g = g_ref[...]
        b = _tri_dot(tri.astype(BF16), g)
        bl = jnp.sum(g, axis=0, keepdims=True)
        eb, enb, elb, ebl = jnp.exp(b), jnp.exp(-b), jnp.exp(bl - b), jnp.exp(bl)
        q = q_ref[...].astype(F32) * qscale
        k = k_ref[...].astype(F32)
        vb = v_ref[...].astype(BF16)
        qe, ke, kl = q * eb, k * enb, k * elb
        qeb, keb, klb = qe.astype(BF16), ke.astype(BF16), kl.astype(BF16)
        st = st_ref[0, 0]
        dstp = dS[...]
        stb, dstb = st.astype(BF16), dstp.astype(BF16)
        dob = do_ref[...].astype(BF16)
        p = jnp.where(tri, _dot(qeb, keb, NT), 0.0).astype(BF16)
        dp = jnp.where(tri, _dot(dob, vb, NT), 0.0).astype(BF16)
        dqe = _dot(dob, stb, NN) + _dot(dp, keb, NN)
        dke = _dot(dp, qeb, TN)
        dvv = _dot(p, dob, TN) + _dot(klb, dstb, NT)
        dkl = _dot(vb, dstb, NN)
        dbl = ebl * jnp.sum(dstp * st, axis=0, keepdims=True) + jnp.sum(dkl * kl, axis=0, keepdims=True)
        dsn = _dot(dob, qeb, TN) + dstp * ebl
        dS[...] = dsn
        ds0_ref[0] = dsn
        dq = dqe * eb * qscale
        dkk = dke * enb + dkl * elb
        db = dqe * qe - dke * ke - dkl * kl
        dg_ref[...] = _tri_dot(tri_t.astype(BF16), db) + dbl
        if aq is not None:
            dq = dq + aq[...].astype(F32)
            dkk = dkk + ak[...].astype(F32)
            dvv = dvv + av[...].astype(F32)
        dq_ref[...] = dq.astype(dq_ref.dtype)
        dk_ref[...] = dkk.astype(dk_ref.dtype)
        dv_ref[...] = dvv.astype(dv_ref.dtype)

    def ci(n):
        return n if reverse else (nc - 1 - n)

    kspec = pl.BlockSpec((C, dk), lambda h, n: (ci(n), h))
    vspec = pl.BlockSpec((C, dv), lambda h, n: (ci(n), h))
    sspec = pl.BlockSpec((1, dv, dk), lambda h, n: (h, 0, 0))
    specs = [
        kspec,
        pl.BlockSpec((C, dk), lambda h, n: (ci(n), heads + h)),
        vspec,
        kspec,
        pl.BlockSpec((1, 1, dv, dk), lambda h, n: (ci(n), h, 0, 0)),
        vspec,
        sspec,
    ]
    ins = [pqk, pqk, pv, gl, states, do, dsf]
    odt = F32 if acc is None else BF16
    if acc is not None:
        specs += [kspec, kspec, vspec]
        ins += list(acc)
    return pl.pallas_call(
        body,
        grid=(heads, nc),
        in_specs=specs,
        out_specs=[kspec, kspec, vspec, kspec, sspec],
        out_shape=[
            jax.ShapeDtypeStruct((T, heads * dk), odt),
            jax.ShapeDtypeStruct((T, heads * dk), odt),
            jax.ShapeDtypeStruct((T, heads * dv), odt),
            jax.ShapeDtypeStruct((T, heads * dk), F32),
            jax.ShapeDtypeStruct((heads, dv, dk), F32),
        ],
        scratch_shapes=[pltpu.VMEM((dv, dk), F32)],
        compiler_params=_cp(("parallel", "arbitrary")),
        name=name,
    )(*ins)


def gla_out_fwd(og, prb, gn, heads, name):
    dv = og.shape[1] // heads

    def fn(og, rb, gn):
        outs = []
        for h in range(heads):
            oh = og[:, h * dv:(h + 1) * dv]
            outs.append(oh * _rstd(oh) * gn)
        y = jnp.concatenate(outs, axis=1)
        return y * (rb * _sig(rb))

    return rowwise(fn, [og, prb], [gn], [(og.shape[1], BF16)], [], 256, name)[0]


def gla_out_bwd(og, prb, du, gn, heads, name):
    dv = og.shape[1] // heads

    def fn(og, rb, du, gn):
        sg = _sig(rb)
        silu = rb * sg
        dsilu = sg * (1.0 + rb * (1.0 - sg))
        dog, ys = [], []
        dgn = jnp.zeros((1, dv), F32)
        for h in range(heads):
            sl = slice(h * dv, (h + 1) * dv)
            oh = og[:, sl]
            r = _rstd(oh)
            n = oh * r
            ys.append(n * gn)
            dy = du[:, sl] * silu[:, sl]
            dgn = dgn + jnp.sum(dy * n, axis=0, keepdims=True)
            dn = dy * gn
            dog.append(r * (dn - n * jnp.mean(dn * n, axis=-1, keepdims=True)))
        y = jnp.concatenate(ys, axis=1)
        return jnp.concatenate(dog, axis=1), du * y * dsilu, dgn

    return rowwise(fn, [og, prb, du], [gn], [(og.shape[1], F32), (og.shape[1], BF16)], [(1, dv)], 128, name)


def conv_specs(T, tt, tc, off, order):
    r8 = tt // 8
    last8 = T // 8 - 1
    if order == "ij":
        return [
            pl.BlockSpec((tt, tc), lambda i, j: (i, j + off)),
            pl.BlockSpec((8, tc), lambda i, j: (jnp.maximum(i * r8 - 1, 0), j + off)),
            pl.BlockSpec((8, tc), lambda i, j: (jnp.minimum((i + 1) * r8, last8), j + off)),
        ]
    return [
        pl.BlockSpec((tt, tc), lambda j, i: (i, j + off)),
        pl.BlockSpec((8, tc), lambda j, i: (jnp.maximum(i * r8 - 1, 0), j + off)),
        pl.BlockSpec((8, tc), lambda j, i: (jnp.minimum((i + 1) * r8, last8), j + off)),
    ]


def _shifted(u, hp, hn, i, nt_):
    tt = u.shape[0]
    row = lax.broadcasted_iota(jnp.int32, u.shape, 0)
    r8 = lax.broadcasted_iota(jnp.int32, hp.shape, 0)
    prev = jnp.sum(jnp.where(r8 == 7, hp, 0.0), axis=0, keepdims=True) * (i > 0).astype(F32)
    nxt = jnp.sum(jnp.where(r8 == 0, hn, 0.0), axis=0, keepdims=True) * (i < nt_ - 1).astype(F32)
    down = jnp.where(row == 0, prev, pltpu.roll(u, 1, 0))
    up = jnp.where(row == tt - 1, nxt, pltpu.roll(u, tt - 1, 0))
    return down, up


def conv_swiglu_fwd(u, cw, cb, name):
    T, F2 = u.shape
    F = F2 // 2
    tt = min(256, T)
    tc = _pick(F, (512, 256, 128))
    nt_, ncol = T // tt, F // tc

    def body(ua, uap, uan, ug, ugp, ugn, wa, wg, ba, bg, f_ref):
        i = pl.program_id(0)
        res = []
        for um, up_, un, w, b in ((ua, uap, uan, wa, ba), (ug, ugp, ugn, wg, bg)):
            x = um[...]
            down, up = _shifted(x, up_[...], un[...], i, nt_)
            res.append(w[0] * down + w[1] * x + w[2] * up + b[...])
        a, g = res
        f_ref[...] = (a * _sig(a) * g).astype(f_ref.dtype)

    wspec = lambda off: pl.BlockSpec((3, 1, tc), lambda i, j: (0, 0, j + off))
    bspec = lambda off: pl.BlockSpec((1, tc), lambda i, j: (0, j + off))
    return pl.pallas_call(
        body,
        grid=(nt_, ncol),
        in_specs=conv_specs(T, tt, tc, 0, "ij") + conv_specs(T, tt, tc, ncol, "ij")
        + [wspec(0), wspec(ncol), bspec(0), bspec(ncol)],
        out_specs=pl.BlockSpec((tt, tc), lambda i, j: (i, j)),
        out_shape=jax.ShapeDtypeStruct((T, F), BF16),
        compiler_params=_cp(("parallel", "parallel")),
        name=name,
    )(u, u, u, u, u, u, cw, cw, cb, cb)


def conv_swiglu_bwd(u, cw, cb, df, name):
    T, F2 = u.shape
    F = F2 // 2
    tt = min(256, T)
    tc = _pick(F, (512, 256, 128))
    nt_, ncol = T // tt, F // tc

    def body(ua, uap, uan, ug, ugp, ugn, wa, wg, ba, bg, df_ref, da_ref, dg_ref, dwa, dwg, dba, dbg):
        i = pl.program_id(1)

        @pl.when(i == 0)
        def _():
            for r in (dwa, dwg, dba, dbg):
                r[...] = jnp.zeros_like(r)

        sh = []
        res = []
        for um, up_, un, w, b in ((ua, uap, uan, wa, ba), (ug, ugp, ugn, wg, bg)):
            x = um[...]
            down, up = _shifted(x, up_[...], un[...], i, nt_)
            sh.append((down, x, up))
            res.append(w[0] * down + w[1] * x + w[2] * up + b[...])
        a, g = res
        d = df_ref[...].astype(F32)
        sg = _sig(a)
        da = d * g * sg * (1.0 + a * (1.0 - sg))
        dg = d * a * sg
        da_ref[...] = da
        dg_ref[...] = dg
        for dd, (down, x, up), dw, db in ((da, sh[0], dwa, dba), (dg, sh[1], dwg, dbg)):
            dw[0] += jnp.sum(dd * down, axis=0, keepdims=True)
            dw[1] += jnp.sum(dd * x, axis=0, keepdims=True)
            dw[2] += jnp.sum(dd * up, axis=0, keepdims=True)
            db[...] += jnp.sum(dd, axis=0, keepdims=True)

    wspec = lambda off: pl.BlockSpec((3, 1, tc), lambda j, i: (0, 0, j + off))
    bspec = lambda off: pl.BlockSpec((1, tc), lambda j, i: (0, j + off))
    tile = pl.BlockSpec((tt, tc), lambda j, i: (i, j))
    return pl.pallas_call(
        body,
        grid=(ncol, nt_),
        in_specs=conv_specs(T, tt, tc, 0, "ji") + conv_specs(T, tt, tc, ncol, "ji")
        + [wspec(0), wspec(ncol), bspec(0), bspec(ncol), tile],
        out_specs=[tile, tile, wspec(0), wspec(0), bspec(0), bspec(0)],
        out_shape=[
            jax.ShapeDtypeStruct((T, F), F32), jax.ShapeDtypeStruct((T, F), F32),
            jax.ShapeDtypeStruct((3, 1, F), F32), jax.ShapeDtypeStruct((3, 1, F), F32),
            jax.ShapeDtypeStruct((1, F), F32), jax.ShapeDtypeStruct((1, F), F32),
        ],
        compiler_params=_cp(("parallel", "arbitrary")),
        name=name,
    )(u, u, u, u, u, u, cw, cw, cb, cb, df)


def conv_transpose(d, cw, off, name):
    T, F = d.shape
    tt = min(256, T)
    tc = _pick(F, (512, 256, 128))
    nt_, ncol = T // tt, F // tc
    offb = off // tc

    def body(dm, dp_, dn, w, o_ref):
        i = pl.program_id(0)
        x = dm[...]
        down, up = _shifted(x, dp_[...], dn[...], i, nt_)
        o_ref[...] = (w[0] * up + w[1] * x + w[2] * down).astype(o_ref.dtype)

    return pl.pallas_call(
        body,
        grid=(nt_, ncol),
        in_specs=conv_specs(T, tt, tc, 0, "ij") + [pl.BlockSpec((3, 1, tc), lambda i, j: (0, 0, j + offb))],
        out_specs=pl.BlockSpec((tt, tc), lambda i, j: (i, j)),
        out_shape=jax.ShapeDtypeStruct((T, F), BF16),
        compiler_params=_cp(("parallel", "parallel")),
        name=name,
    )(d, d, d, cw)


def rope_tables(n, hd):
    rows = n // GRID_W
    row = jnp.repeat(jnp.arange(rows), GRID_W)
    col = jnp.tile(jnp.arange(GRID_W), rows)
    n_freq = hd // 4
    inv = ROPE_THETA ** (-jnp.arange(n_freq, dtype=F32) / n_freq)
    ang = jnp.concatenate([row[:, None] * inv, col[:, None] * inv], axis=-1)
    cos, sin = jnp.cos(ang), jnp.sin(ang)
    return jnp.concatenate([cos, cos], axis=-1), jnp.concatenate([-sin, sin], axis=-1)


def local_step(x, ctx, tgt, mod, modc, W, P):
    T, D = x.shape
    L = ctx.shape[0]
    hd, hq, hkv, gh = P["hd"], P["hq"], P["hkv"], P["gh"]
    sh1, sc1, g1, sh2, sc2, g2 = mod
    csh1, csc1 = modc
    kvw = hkv * hd
    gkw = W["gqk"].shape[1] // 2
    gdv = D // gh
    gdk = gkw // gh

    h = modulate_fwd(x, P["g_mix"], sh1, sc1, "mod1")
    hc = modulate_fwd(ctx, P["g_mix"], csh1, csc1, "mod1_ctx")
    pq = matmul(h, W["q"], "nn", F32, "proj_q")
    pkv = matmul(h, W["kv"], "nn", F32, "proj_kv")
    pgqk = matmul(h, W["gqk"], "nn", F32, "proj_gqk")
    pgv = matmul(h, W["gv"], "nn", F32, "proj_gv")
    prb = matmul(h, W["rb"], "nn", F32, "proj_rb")
    plr = matmul(h, W["lr"], "nn", F32, "proj_lr")
    pgab = matmul(h, W["gab"], "nn", F32, "proj_gab")
    pkv_c = matmul(hc, W["kv"], "nn", F32, "proj_kv_ctx")
    pgqk_c = matmul(hc, W["gqk"], "nn", F32, "proj_gqk_ctx")
    pgv_c = matmul(hc, W["gv"], "nn", F32, "proj_gv_ctx")
    plr_c = matmul(hc, W["lr"], "nn", F32, "proj_lr_ctx")

    cosf, sinf = rope_tables(T, hd)
    one_c, zero_c = jnp.ones((L, hd), F32), jnp.zeros((L, hd), F32)
    qr = norm_rope_fwd(pq, hq * hd, 0, P["q_norm"], cosf, sinf, hd, "qnorm")
    kr = norm_rope_fwd(pkv, kvw, 0, P["k_norm"], cosf, sinf, hd, "knorm")
    kcr = norm_rope_fwd(pkv_c, kvw, 0, P["k_norm"], one_c, zero_c, hd, "knorm_ctx")
    sink = P["attn_sink"].reshape(hq, 1, 1)
    o_attn, lse = attention_fwd(qr, kr, pkv, kcr, pkv_c, sink, hkv, hd, "attn_fwd")

    gf, gb = gate_fwd(plr, W["gate_f"], W["gate_b"], P["b_gate_f"], P["b_gate_b"], "gates")
    gfc, gbc = gate_fwd(plr_c, W["gate_f"], W["gate_b"], P["b_gate_f"], P["b_gate_b"], "gates_ctx")
    zero_state = jnp.zeros((gh, gdv, gdk), F32)
    _, st_cf, s_cf = gla_fwd(pgqk_c, pgv_c, gfc, zero_state, gh, False, "gla_ctx_f")
    _, st_cb, s_cb = gla_fwd(pgqk_c, pgv_c, gbc, zero_state, gh, True, "gla_ctx_b")
    of, st_f, _ = gla_fwd(pgqk, pgv, gf, s_cf, gh, False, "gla_f")
    og, st_b, _ = gla_fwd(pgqk, pgv, gb, s_cb, gh, True, "gla_b", o_add=of)
    ug = gla_out_fwd(og, prb, P["gla_norm"], gh, "gla_out")

    ya = matmul(o_attn, W["attn_o"], "nn", F32, "attn_o")
    yg = matmul(ug, W["gla_o"], "nn", F32, "gla_o")

    def merge_fn(ya, yg, ga, gb_):
        return _sig(ga) * ya + _sig(gb_) * yg

    z = rowwise(merge_fn, [ya, yg, (pgab, D, 0), (pgab, D, 1)], [], [(D, BF16)], [], 256, "merge")[0]
    mo = matmul(z, W["out"], "nn", F32, "w_out")

    def res_fn(x, mo, g1, gffn, sh2, sc2):
        x1 = x + g1 * mo
        return x1, x1 * _rstd(x1) * gffn * (1.0 + sc2) + sh2

    x1, h2 = rowwise(res_fn, [x, mo], [g1, P["g_ffn"], sh2, sc2], [(D, F32), (D, BF16)], [], 256, "res_mod2")
    u = matmul(h2, W["up"], "nn", F32, "w_up")
    cw3 = W["conv_w"].reshape(3, 1, -1)
    f = conv_swiglu_fwd(u, cw3, P["conv_b"], "conv_swiglu")
    fo = matmul(f, W["down"], "nn", F32, "w_down")

    def final_fn(x1, fo, tgt, g2):
        e = x1 + g2 * fo - tgt
        dy = e * (1.0 / D)
        lsum = jnp.sum(jnp.sum(e * e, axis=1, keepdims=True), axis=0, keepdims=True)
        return dy, dy * g2, jnp.broadcast_to(lsum, (1, 128)), jnp.sum(dy * fo, axis=0, keepdims=True)

    dy, dfo, lsum, dg2 = rowwise(final_fn, [x1, fo, tgt], [g2], [(D, F32), (D, BF16)], [(1, 128), (1, D)], 256, "loss")
    df = matmul(dfo, W["down"], "nt", BF16, "d_f")
    dw_down = matmul(f, dfo, "tn", F32, "dw_down")
    da, dgg, dcw_a, dcw_g, dcb_a, dcb_g = conv_swiglu_bwd(u, cw3, P["conv_b"], df, "conv_swiglu_bwd")
    Fh = da.shape[1]
    du_a = conv_transpose(da, cw3, 0, "conv_t_a")
    du_g = conv_transpose(dgg, cw3, Fh, "conv_t_g")
    dh2 = matmul(du_a, W["up"][:, :Fh], "nt", F32, "d_h2_a")
    dh2 = matmul(du_g, W["up"][:, Fh:], "nt", F32, "d_h2_g", add=dh2)
    dw_up = jnp.concatenate([matmul(h2, du_a, "tn", F32, "dw_up_a"), matmul(h2, du_g, "tn", F32, "dw_up_g")], axis=1)

    def mod2_bwd_fn(x1, dh, dy, mo, gffn, sc2, g1):
        r = _rstd(x1)
        n = x1 * r
        dyy = dh * (1.0 + sc2)
        dn = dyy * gffn
        dx1 = dy + r * (dn - n * jnp.mean(dn * n, axis=-1, keepdims=True))
        s0 = lambda a: jnp.sum(a, axis=0, keepdims=True)
        return dx1, dx1 * g1, s0(dyy * n), s0(dh), s0(dh * n * gffn), s0(dx1 * mo)

    dx1, dmo, dg_ffn, dsh2, dsc2, dg1 = rowwise(
        mod2_bwd_fn, [x1, dh2, dy, mo], [P["g_ffn"], sc2, g1], [(D, F32), (D, BF16)], [(1, D)] * 4, 128, "mod2_bwd")
    dz = matmul(dmo, W["out"], "nt", F32, "d_z")
    dw_out = matmul(z, dmo, "tn", F32, "dw_out")

    def merge_bwd_fn(dz, ya, yg, ga, gb_):
        sa, sb = _sig(ga), _sig(gb_)
        return dz * sa, dz * sb, jnp.concatenate([dz * ya * sa * (1.0 - sa), dz * yg * sb * (1.0 - sb)], axis=1)

    dya, dyg, dpgab = rowwise(merge_bwd_fn, [dz, ya, yg, (pgab, D, 0), (pgab, D, 1)], [],
                              [(D, BF16), (D, BF16), (2 * D, BF16)], [], 128, "merge_bwd")
    do_attn = matmul(dya, W["attn_o"], "nt", BF16, "d_oattn")
    dw_attn_o = matmul(o_attn, dya, "tn", F32, "dw_attn_o")
    dug = matmul(dyg, W["gla_o"], "nt", F32, "d_ug")
    dw_gla_o = matmul(ug, dyg, "tn", F32, "dw_gla_o")
    dog, dprb, dgn = gla_out_bwd(og, prb, dug, P["gla_norm"], gh, "gla_out_bwd")

    dq1, dk1, dv1, dgf, ds_cf = gla_bwd(pgqk, pgv, gf, st_f, dog, zero_state, gh, False, "gla_f_bwd")
    dgq, dgk, dpgv, dgb, ds_cb = gla_bwd(pgqk, pgv, gb, st_b, dog, zero_state, gh, True, "gla_b_bwd",
                                          acc=(dq1, dk1, dv1))
    dpgqk = jnp.concatenate([dgq, dgk], axis=1)
    zero_do = jnp.zeros((L, gh * gdv), F32)
    cq1, ck1, cv1, dgfc, _ = gla_bwd(pgqk_c, pgv_c, gfc, st_cf, zero_do, ds_cf, gh, False, "gla_ctx_f_bwd")
    cq, ck, dpgv_c, dgbc, _ = gla_bwd(pgqk_c, pgv_c, gbc, st_cb, zero_do, ds_cb, gh, True, "gla_ctx_b_bwd",
                                      acc=(cq1, ck1, cv1))
    dpgqk_c = jnp.concatenate([cq, ck], axis=1)
    dplr, dwgf, dbgf, dwgb, dbgb = gate_bwd(plr, dgf, dgb, W["gate_f"], W["gate_b"], P["b_gate_f"], P["b_gate_b"], "gates_bwd")
    dplr_c, dwgf_c, dbgf_c, dwgb_c, dbgb_c = gate_bwd(plr_c, dgfc, dgbc, W["gate_f"], W["gate_b"], P["b_gate_f"],
                                                      P["b_gate_b"], "gates_ctx_bwd")

    dqr, dkc_r, dvc, dsink = attention_bwd_q(qr, kr, pkv, kcr, pkv_c, sink, do_attn, o_attn, lse, hkv, hd, "attn_bwd_q")
    dkr, dv = attention_bwd_kv(qr, kr, pkv, do_attn, o_attn, lse, hkv, hd, "attn_bwd_kv")
    dpq, dqn = norm_rope_bwd(pq, hq * hd, 0, dqr, P["q_norm"], cosf, sinf, hd, "qnorm_bwd")
    dpk, dkn = norm_rope_bwd(pkv, kvw, 0, dkr, P["k_norm"], cosf, sinf, hd, "knorm_bwd")
    dpk_c, dkn_c = norm_rope_bwd(pkv_c, kvw, 0, dkc_r, P["k_norm"], one_c, zero_c, hd, "knorm_ctx_bwd")
    dpkv = jnp.concatenate([dpk, dv], axis=1)
    dpkv_c = jnp.concatenate([dpk_c, dvc.astype(BF16)], axis=1)

    dw_q = matmul(h, dpq, "tn", F32, "dw_q")
    dw_kv = matmul(h, dpkv, "tn", F32, "dw_kv", add=matmul(hc, dpkv_c, "tn", F32, "dw_kv_ctx"))
    dw_gqk = matmul(h, dpgqk, "tn", F32, "dw_gqk", add=matmul(hc, dpgqk_c, "tn", F32, "dw_gqk_ctx"))
    dw_gv = matmul(h, dpgv, "tn", F32, "dw_gv", add=matmul(hc, dpgv_c, "tn", F32, "dw_gv_ctx"))
    dw_rb = matmul(h, dprb, "tn", F32, "dw_rb")
    dw_lr = matmul(h, dplr, "tn", F32, "dw_lr", add=matmul(hc, dplr_c, "tn", F32, "dw_lr_ctx"))
    dw_gab = matmul(h, dpgab, "tn", F32, "dw_gab")
    lrw = P["lowrank"]
    dw_in = jnp.concatenate([dw_q, dw_kv, dw_gqk, dw_gv, dw_rb, dw_lr[:, :2 * lrw], dw_gab], axis=1)

    dh = matmul(dpq, W["q"], "nt", F32, "dh_q")
    dh = matmul(dpkv, W["kv"], "nt", F32, "dh_kv", add=dh)
    dh = matmul(dpgqk, W["gqk"], "nt", F32, "dh_gqk", add=dh)
    dh = matmul(dpgv, W["gv"], "nt", F32, "dh_gv", add=dh)
    dh = matmul(dprb, W["rb"], "nt", F32, "dh_rb", add=dh)
    dh = matmul(dplr, W["lr"], "nt", F32, "dh_lr", add=dh)
    dh = matmul(dpgab, W["gab"], "nt", F32, "dh_gab", add=dh)
    dhc = matmul(dpkv_c, W["kv"], "nt", F32, "dhc_kv")
    dhc = matmul(dpgqk_c, W["gqk"], "nt", F32, "dhc_gqk", add=dhc)
    dhc = matmul(dpgv_c, W["gv"], "nt", F32, "dhc_gv", add=dhc)
    dhc = matmul(dplr_c, W["lr"], "nt", F32, "dhc_lr", add=dhc)

    def mod1_bwd_fn(x, dh, dres, g, sc):
        r = _rstd(x)
        n = x * r
        dyy = dh * (1.0 + sc)
        dn = dyy * g
        dx = dres + r * (dn - n * jnp.mean(dn * n, axis=-1, keepdims=True))
        s0 = lambda a: jnp.sum(a, axis=0, keepdims=True)
        return dx, s0(dyy * n), s0(dh), s0(dh * n * g)

    grad_x, dgmix, dsh1, dsc1 = rowwise(mod1_bwd_fn, [x, dh, dx1], [P["g_mix"], sc1], [(D, F32)], [(1, D)] * 3,
                                        128, "mod1_bwd")
    _, dgmix_c, dcsh1, dcsc1 = rowwise(mod1_bwd_fn, [ctx, dhc, jnp.zeros_like(ctx)], [P["g_mix"], csc1], [(D, F32)],
                                       [(1, D)] * 3, 128, "mod1_ctx_bwd")

    zD = jnp.zeros((1, D), F32)
    grads = dict(
        w_in=dw_in, w_attn_o=dw_attn_o, w_gla_o=dw_gla_o, w_out=dw_out, w_up=dw_up, w_down=dw_down,
        dmod_x=jnp.concatenate([dsh1, dsc1, dg1, dsh2, dsc2, dg2], axis=1),
        dmod_c=jnp.concatenate([dcsh1, dcsc1, zD, zD, zD, zD], axis=1),
        g_mix=dgmix + dgmix_c, q_norm=dqn, k_norm=dkn + dkn_c, attn_sink=dsink.reshape(1, hq),
        w_gate_f=(dwgf + dwgf_c)[:lrw], b_gate_f=dbgf + dbgf_c,
        w_gate_b=(dwgb + dwgb_c)[lrw:2 * lrw], b_gate_b=dbgb + dbgb_c,
        gla_norm=dgn, g_ffn=dg_ffn,
        conv_w=jnp.concatenate([dcw_a, dcw_g], axis=2).reshape(3, -1),
        conv_b=jnp.concatenate([dcb_a, dcb_g], axis=1),
    )
    return lsum[0, 0], grad_x, grads


BIG = ("w_in", "w_o3", "w_up", "w_down")
SMALL_REPL = ("c_ctx", "b_mod", "g_mix", "q_norm", "k_norm", "attn_sink", "b_gate_f", "b_gate_b", "gla_norm", "g_ffn",
              "conv_b")
SMALL_SHARD = ("w_gate_f", "w_gate_b", "conv_w")
ORDER = ("c_ctx", "w_mod", "b_mod", "g_mix", "w_in", "q_norm", "k_norm", "attn_sink", "w_gate_f", "b_gate_f",
         "w_gate_b", "b_gate_b", "gla_norm", "w_attn_o", "w_gla_o", "w_out", "g_ffn", "w_up", "conv_w", "conv_b",
         "w_down")


def kernel(x, c, ctx, c_ctx, w_mod, b_mod, g_mix, w_in, q_norm, k_norm, attn_sink, w_gate_f, b_gate_f, w_gate_b, b_gate_b, gla_norm, w_attn_o, w_gla_o, w_out, g_ffn, w_up, conv_w, conv_b, w_down, loss_target, m_c_ctx, m_w_mod, m_b_mod, m_g_mix, m_w_in, m_q_norm, m_k_norm, m_attn_sink, m_w_gate_f, m_b_gate_f, m_w_gate_b, m_b_gate_b, m_gla_norm, m_w_attn_o, m_w_gla_o, m_w_out, m_g_ffn, m_w_up, m_conv_w, m_conv_b, m_w_down, v_c_ctx, v_w_mod, v_b_mod, v_g_mix, v_w_in, v_q_norm, v_k_norm, v_attn_sink, v_w_gate_f, v_b_gate_f, v_w_gate_b, v_b_gate_b, v_gla_norm, v_w_attn_o, v_w_gla_o, v_w_out, v_g_ffn, v_w_up, v_conv_w, v_conv_b, v_w_down):
    loc = dict(locals())
    Wt = {n: loc[n] for n in ORDER}
    Mt = {n: loc["m_" + n] for n in ORDER}
    Vt = {n: loc["v_" + n] for n in ORDER}
    me = 4 * lax.axis_index("x") + 2 * lax.axis_index("y") + lax.axis_index("c")

    D = x.shape[-1]
    hd = q_norm.shape[-1]
    hq = attn_sink.shape[-1]
    gdv = gla_norm.shape[-1]
    gh = D // gdv
    gdk = D // 2 // gh
    lrw = w_gate_f.shape[1]
    in_w = NDEV * w_in.shape[-1]
    kvw = (in_w - hq * hd - 2 * gh * gdk - 2 * gh * gdv - 2 * lrw - 2 * D) // 2
    hkv = kvw // hd
    gcols = w_gate_f.shape[-1]
    F2 = NDEV * w_up.shape[-1]
    mcols = w_mod.shape[-1]

    x2, ctx2, tgt2 = x[0], ctx[0], loss_target[0]

    c_all = exchange([jnp.pad(c, ((0, 7), (0, 0)))], True, "gather_c")[0][:, 0, :]
    c9 = jnp.concatenate([c_all, c_ctx[None, :], jnp.zeros((7, D), F32)], axis=0)
    s9 = rowwise(lambda a: a * _sig(a), [c9], [], [(D, F32)], [], 16, "silu_c")[0]
    bias = jnp.broadcast_to(lax.dynamic_slice_in_dim(b_mod, me * mcols, mcols, axis=1), (16, mcols))
    mod_cols = matmul(s9, w_mod[0], "nn", F32, "mod_cols", add=bias)
    mod_all = exchange([mod_cols], True, "gather_mod")[0]
    mod_all = jnp.transpose(mod_all, (1, 0, 2)).reshape(16, NDEV * mcols)
    mod_me = lax.dynamic_slice_in_dim(mod_all, me, 1, axis=0)
    mod = [mod_me[:, i * D:(i + 1) * D] for i in range(6)]
    modc = [mod_all[8:9, i * D:(i + 1) * D] for i in range(2)]

    o3 = jnp.stack([w_attn_o[0], w_gla_o[0], w_out[0]]).astype(BF16)
    small_w = pack([w_gate_f[0], w_gate_b[0], conv_w[0]])
    g_in, g_o3, g_up, g_down, g_small = exchange(
        [w_in[0].astype(BF16), o3, w_up[0].astype(BF16), w_down[0].astype(BF16), small_w], True, "gather_w")
    win = jnp.transpose(g_in, (1, 0, 2)).reshape(D, in_w)
    offs = np.cumsum([0, hq * hd, 2 * kvw, 2 * gh * gdk, gh * gdv, gh * gdv, 2 * lrw, 2 * D]).tolist()
    seg = [win[:, offs[i]:offs[i + 1]] for i in range(7)]
    small_parts = [unpack(g_small[j], [w_gate_f[0].shape, w_gate_b[0].shape, conv_w[0].shape]) for j in range(NDEV)]
    wgf = jnp.concatenate([p[0] for p in small_parts], axis=1)
    wgb = jnp.concatenate([p[1] for p in small_parts], axis=1)
    cw_full = jnp.concatenate([p[2] for p in small_parts], axis=1)
    o3f = jnp.transpose(g_o3, (1, 0, 2, 3)).reshape(3, -1, D)
    W = dict(
        q=seg[0], kv=seg[1], gqk=seg[2], gv=seg[3], rb=seg[4],
        lr=jnp.pad(seg[5], ((0, 0), (0, 128 - 2 * lrw))), gab=seg[6],
        gate_f=jnp.pad(wgf, ((0, 128 - lrw), (0, 0))),
        gate_b=jnp.pad(wgb, ((lrw, 128 - 2 * lrw), (0, 0))),
        attn_o=o3f[0], gla_o=o3f[1], out=o3f[2],
        up=jnp.transpose(g_up, (1, 0, 2)).reshape(D, F2),
        down=g_down.reshape(-1, D),
        conv_w=cw_full,
    )
    P = dict(hd=hd, hq=hq, hkv=hkv, gh=gh, lowrank=lrw, g_mix=g_mix, q_norm=q_norm, k_norm=k_norm, attn_sink=attn_sink,
             b_gate_f=b_gate_f, b_gate_b=b_gate_b, gla_norm=gla_norm, g_ffn=g_ffn, conv_b=conv_b)

    lsum, grad_x, G = local_step(x2, ctx2, tgt2, mod, modc, W, P)
    loss = lax.psum(0.5 * lsum / D, ("x", "y", "c"))

    dm = exchange([jnp.concatenate([G["dmod_x"], G["dmod_c"], jnp.zeros((6, 6 * D), F32)], axis=0)], True,
                  "gather_dmod")[0]
    dmc = reduce_parts(dm[:, 1:2, :].reshape(NDEV, 6 * D // 128, 128), "sum_dmod_ctx").reshape(1, 6 * D)
    dM = jnp.concatenate([dm[:, 0, :], dmc, jnp.zeros((7, 6 * D), F32)], axis=0)
    dM_cols = lax.dynamic_slice_in_dim(dM, me * mcols, mcols, axis=1)
    g_w_mod = matmul(s9, dM_cols, "tn", F32, "dw_mod")
    g_b_mod = reduce_parts(dM.reshape(16, 6 * D // 128, 128), "sum_db_mod").reshape(1, 6 * D)
    dsc = matmul(dM_cols[8:16], w_mod[0], "nt", F32, "d_silu_ctx")
    cc = jnp.broadcast_to(c_ctx[None, :], (8, D))

    def dsilu_fn(d, a):
        sg = _sig(a)
        return d * sg * (1.0 + a * (1.0 - sg))

    g_cctx_part = rowwise(dsilu_fn, [dsc, cc], [], [(D, F32)], [], 8, "d_c_ctx")[0][0:1]

    small_names = ("c_ctx", "g_mix", "q_norm", "k_norm", "attn_sink", "b_gate_f", "b_gate_b", "gla_norm", "g_ffn",
                   "conv_b", "w_gate_f", "w_gate_b", "conv_w")
    G["c_ctx"] = g_cctx_part
    sm_shapes = [G[n].shape for n in small_names]
    sm_all = exchange([pack([G[n] for n in small_names])], True, "gather_small_grads")[0]
    sm_tot = unpack(reduce_parts(sm_all, "sum_small_grads"), sm_shapes)
    gs = dict(zip(small_names, sm_tot))
    gs["b_mod"] = g_b_mod
    gs["w_gate_f"] = lax.dynamic_slice_in_dim(gs["w_gate_f"], me * gcols, gcols, axis=1)
    gs["w_gate_b"] = lax.dynamic_slice_in_dim(gs["w_gate_b"], me * gcols, gcols, axis=1)
    ccols = conv_w.shape[-1]
    gs["conv_w"] = lax.dynamic_slice_in_dim(gs["conv_w"], me * ccols, ccols, axis=1)

    wcols = w_in.shape[-1]
    ucols = w_up.shape[-1]
    drows = w_down.shape[1]
    orows = w_attn_o.shape[1]
    s_in = jnp.transpose(G["w_in"].reshape(D, NDEV, wcols), (1, 0, 2))
    s_o3 = jnp.stack([G["w_attn_o"].reshape(NDEV, orows, D), G["w_gla_o"].reshape(NDEV, orows, D),
                      G["w_out"].reshape(NDEV, orows, D)], axis=1).reshape(NDEV, 3 * orows, D)
    s_up = jnp.transpose(G["w_up"].reshape(D, NDEV, ucols), (1, 0, 2))
    s_down = G["w_down"].reshape(NDEV, drows, D)
    r_in, r_o3, r_up, r_down = exchange([s_in, s_o3, s_up, s_down], False, "scatter_grads")

    out = {}
    out["w_in"] = adam_reduce(r_in, w_in[0], m_w_in[0], v_w_in[0], "adam_w_in")
    o3w = jnp.concatenate([w_attn_o[0], w_gla_o[0], w_out[0]], axis=0)
    o3m = jnp.concatenate([m_w_attn_o[0], m_w_gla_o[0], m_w_out[0]], axis=0)
    o3v = jnp.concatenate([v_w_attn_o[0], v_w_gla_o[0], v_w_out[0]], axis=0)
    ro3 = adam_reduce(r_o3, o3w, o3m, o3v, "adam_o3")
    for i, n in enumerate(("w_attn_o", "w_gla_o", "w_out")):
        out[n] = [a[i * orows:(i + 1) * orows] for a in ro3]
    out["w_up"] = adam_reduce(r_up, w_up[0], m_w_up[0], v_w_up[0], "adam_w_up")
    out["w_down"] = adam_reduce(r_down, w_down[0], m_w_down[0], v_w_down[0], "adam_w_down")
    out["w_mod"] = adam_reduce(g_w_mod[None], w_mod[0], m_w_mod[0], v_w_mod[0], "adam_w_mod")
    sm_names = SMALL_REPL + SMALL_SHARD
    shapes = [Wt[n].shape for n in sm_names]
    rs = adam_reduce(pack([gs[n] for n in sm_names])[None], pack([Wt[n] for n in sm_names]),
                     pack([Mt[n] for n in sm_names]), pack([Vt[n] for n in sm_names]), "adam_small")
    rs = [unpack(a, shapes) for a in rs]
    for i, n in enumerate(sm_names):
        out[n] = [a[i] for a in rs]

    res = [loss, grad_x[None]]
    for k in range(4):
        for n in ORDER:
            res.append(out[n][k].reshape(Wt[n].shape))
    return tuple(res)
```

```python
import jax
import jax.numpy as jnp
import numpy as np
from jax import lax
from jax.experimental import pallas as pl
from jax.experimental.pallas import tpu as pltpu

F32 = jnp.float32
BF16 = jnp.bfloat16

NDEV = 8
NCHIP = 4
EPS = 1e-6
WINDOW = 128
BLOCK = 128
GRID_W = 64
ROPE_THETA = 10000.0
GLA_CHUNK = 64
GLA_GATE_NORM = 16.0
ADAM_LR = 0.001
ADAM_B1 = 0.9
ADAM_B2 = 0.999
ADAM_EPS = 1e-08
ADAM_WD = 0.01
ADAM_STEP = 10
V7X_VMEM_LIMIT = 56 * 1024 * 1024
NEG = -1e30

NN = ((1,), (0,))
NT = ((1,), (1,))
TN = ((0,), (0,))


def _dot(a, b, dims):
    return lax.dot_general(a, b, (dims, ((), ())), preferred_element_type=F32)


def _cp(sem):
    return pltpu.CompilerParams(dimension_semantics=sem, vmem_limit_bytes=V7X_VMEM_LIMIT)


def _pick(n, cands):
    for c in cands:
        if n % c == 0:
            return c
    return n


def _sig(x):
    return 1.0 / (1.0 + jnp.exp(-x))


def _rstd(x):
    return lax.rsqrt(jnp.mean(x * x, axis=-1, keepdims=True) + EPS)


_ANY = pl.BlockSpec(memory_space=pl.ANY)


def _place():
    return lax.axis_index("x"), lax.axis_index("y"), lax.axis_index("c")


def exchange(srcs, bcast, name, group="all"):
    n = len(srcs)
    ndev = NDEV if group == "all" else NCHIP
    ks = tuple(range(1, NDEV)) if group == "all" else (2, 4, 6)
    out_shape = [jax.ShapeDtypeStruct((ndev,) + (s.shape if bcast else s.shape[1:]), s.dtype) for s in srcs]

    def body(*refs):
        src, dst = refs[:n], refs[n:2 * n]
        send_sems, recv_sems, loc_sems = refs[2 * n:]
        x, y, c = _place()

        def idx(px, py, pc):
            return 4 * px + 2 * py + pc if group == "all" else 2 * px + py

        me = idx(x, y, c)
        copies = []
        for a in range(n):
            cp = pltpu.make_async_copy(src[a] if bcast else src[a].at[me], dst[a].at[me], loc_sems.at[a])
            cp.start()
            copies.append(cp)
        for s, k in enumerate(ks):
            px, py, pc = x ^ ((k >> 2) & 1), y ^ ((k >> 1) & 1), c ^ (k & 1)
            for a in range(n):
                cp = pltpu.make_async_remote_copy(
                    src_ref=src[a] if bcast else src[a].at[idx(px, py, pc)],
                    dst_ref=dst[a].at[me],
                    send_sem=send_sems.at[a, s],
                    recv_sem=recv_sems.at[a, s],
                    device_id=(px, py, pc),
                    device_id_type=pl.DeviceIdType.MESH,
                )
                cp.start()
                copies.append(cp)
        for cp in copies:
            cp.wait()

    return pl.pallas_call(
        body,
        out_shape=out_shape,
        in_specs=[_ANY] * n,
        out_specs=[_ANY] * n,
        scratch_shapes=[
            pltpu.SemaphoreType.DMA((n, len(ks))),
            pltpu.SemaphoreType.DMA((n, len(ks))),
            pltpu.SemaphoreType.DMA((n,)),
        ],
        name=name,
    )(*srcs)


def gather_two_level(srcs, name):
    n = len(srcs)
    out_shape = [jax.ShapeDtypeStruct((NDEV,) + s.shape, s.dtype) for s in srcs]
    chips = (2, 4, 6)

    def body(*refs):
        src, dst = refs[:n], refs[n:2 * n]
        send_sems, recv_sems, loc_sems = refs[2 * n:]
        x, y, c = _place()
        me = 4 * x + 2 * y + c
        sib = (x, y, 1 - c)

        def copy(a, s, block, to, from_src):
            return pltpu.make_async_remote_copy(
                src_ref=src[a] if from_src else dst[a].at[block], dst_ref=dst[a].at[block],
                send_sem=send_sems.at[a, s], recv_sem=recv_sems.at[a, s],
                device_id=to, device_id_type=pl.DeviceIdType.MESH)

        sent = []
        for a in range(n):
            cp = pltpu.make_async_copy(src[a], dst[a].at[me], loc_sems.at[a])
            cp.start()
            sent.append(cp)
        for a in range(n):
            cp = copy(a, 0, me, sib, True)
            cp.start()
            sent.append(cp)
        for j, k in enumerate(chips):
            px, py = x ^ ((k >> 2) & 1), y ^ ((k >> 1) & 1)
            for a in range(n):
                cp = copy(a, 1 + j, me, (px, py, c), True)
                cp.start()
                sent.append(cp)
        for j, k in enumerate(chips):
            px, py = x ^ ((k >> 2) & 1), y ^ ((k >> 1) & 1)
            theirs = 4 * px + 2 * py + c
            for a in range(n):
                copy(a, 1 + j, theirs, (px, py, c), False).wait_recv()
                cp = copy(a, 4 + j, theirs, sib, False)
                cp.start()
                sent.append(cp)
        for a in range(n):
            copy(a, 0, 4 * x + 2 * y + (1 - c), sib, False).wait_recv()
        for j, k in enumerate(chips):
            px, py = x ^ ((k >> 2) & 1), y ^ ((k >> 1) & 1)
            for a in range(n):
                copy(a, 4 + j, 4 * px + 2 * py + (1 - c), sib, False).wait_recv()
        for cp in sent[:n]:
            cp.wait()
        for cp in sent[n:]:
            cp.wait_send()

    return pl.pallas_call(
        body,
        out_shape=out_shape,
        in_specs=[_ANY] * n,
        out_specs=[_ANY] * n,
        scratch_shapes=[
            pltpu.SemaphoreType.DMA((n, NDEV - 1)),
            pltpu.SemaphoreType.DMA((n, NDEV - 1)),
            pltpu.SemaphoreType.DMA((n,)),
        ],
        name=name,
    )(*srcs)


def pair_swap(srcs, name):
    n = len(srcs)

    def body(*refs):
        src, dst = refs[:n], refs[n:2 * n]
        send_sems, recv_sems = refs[2 * n:]
        x, y, c = _place()
        copies = []
        for a in range(n):
            cp = pltpu.make_async_remote_copy(
                src_ref=src[a], dst_ref=dst[a], send_sem=send_sems.at[a], recv_sem=recv_sems.at[a],
                device_id=(x, y, 1 - c), device_id_type=pl.DeviceIdType.MESH)
            cp.start()
            copies.append(cp)
        for cp in copies:
            cp.wait()

    return pl.pallas_call(
        body,
        out_shape=[jax.ShapeDtypeStruct(s.shape, s.dtype) for s in srcs],
        in_specs=[_ANY] * n,
        out_specs=[_ANY] * n,
        scratch_shapes=[pltpu.SemaphoreType.DMA((n,)), pltpu.SemaphoreType.DMA((n,))],
        name=name,
    )(*srcs)


def scatter_reduce(blocks, name):
    c = lax.axis_index("c")
    halves = [b.reshape((NCHIP, 2) + b.shape[1:]) for b in blocks]
    mine = [lax.dynamic_index_in_dim(h, c, axis=1, keepdims=False) for h in halves]
    theirs = [lax.dynamic_index_in_dim(h, 1 - c, axis=1, keepdims=False) for h in halves]
    got = pair_swap(theirs, name + "_d2d")
    sums = []
    for i, (m, g) in enumerate(zip(mine, got)):
        flat = (NCHIP * m.shape[1], m.shape[2])
        s = rowwise(lambda a, b: a.astype(F32) + b.astype(F32), [m.reshape(flat), g.reshape(flat)], [],
                    [(flat[1], m.dtype)], [], 64, f"{name}_pair_sum{i}")[0]
        sums.append(s.reshape(m.shape))
    return exchange(sums, False, name + "_ici", group="core")


def matmul(a, b, mode, out_dtype, name, add=None):
    if mode == "nn":
        (M, K), N = a.shape, b.shape[1]
    elif mode == "nt":
        (M, K), N = a.shape, b.shape[0]
    else:
        (K, M), N = a.shape, b.shape[1]
    tm = _pick(M, (1024, 512, 256, 128))
    tn = _pick(N, (1024, 512, 256, 128))
    tk = _pick(K, (512, 256, 128))
    nk = K // tk
    dims = {"nn": NN, "nt": NT, "tn": TN}[mode]

    def body(*refs):
        if add is None:
            a_ref, b_ref, o_ref, acc = refs
            c_ref = None
        else:
            a_ref, b_ref, c_ref, o_ref, acc = refs
        k = pl.program_id(2)

        @pl.when(k == 0)
        def _():
            acc[...] = jnp.zeros_like(acc)

        acc[...] += _dot(a_ref[...].astype(BF16), b_ref[...].astype(BF16), dims)

        @pl.when(k == nk - 1)
        def _():
            r = acc[...]
            if c_ref is not None:
                r = r + c_ref[...].astype(F32)
            o_ref[...] = r.astype(o_ref.dtype)

    a_spec = pl.BlockSpec((tk, tm), lambda i, j, k: (k, i)) if mode == "tn" else pl.BlockSpec((tm, tk), lambda i, j, k: (i, k))
    b_spec = pl.BlockSpec((tn, tk), lambda i, j, k: (j, k)) if mode == "nt" else pl.BlockSpec((tk, tn), lambda i, j, k: (k, j))
    o_spec = pl.BlockSpec((tm, tn), lambda i, j, k: (i, j))
    ins, specs = [a, b], [a_spec, b_spec]
    if add is not None:
        ins.append(add)
        specs.append(o_spec)
    return pl.pallas_call(
        body,
        grid=(M // tm, N // tn, nk),
        in_specs=specs,
        out_specs=o_spec,
        out_shape=jax.ShapeDtypeStruct((M, N), out_dtype),
        scratch_shapes=[pltpu.VMEM((tm, tn), F32)],
        compiler_params=_cp(("parallel", "parallel", "arbitrary")),
        name=name,
    )(*ins)


def rowwise(fn, tiled, full, out_tiled, out_acc, tile, name):
    tiled = [t if isinstance(t, tuple) else (t, t.shape[1], 0) for t in tiled]
    rows = tiled[0][0].shape[0]
    tile = min(tile, rows)
    assert rows % tile == 0
    nt, nf, no = len(tiled), len(full), len(out_tiled)

    def body(*refs):
        ins = [r[...] for r in refs[:nt + nf]]
        res = fn(*ins)
        if not isinstance(res, (tuple, list)):
            res = (res,)
        outs = refs[nt + nf:]
        for r, v in zip(outs[:no], res[:no]):
            r[...] = v.astype(r.dtype)
        if out_acc:
            @pl.when(pl.program_id(0) == 0)
            def _():
                for r in outs[no:]:
                    r[...] = jnp.zeros_like(r)

            for r, v in zip(outs[no:], res[no:]):
                r[...] += v

    in_specs = [pl.BlockSpec((tile, w), lambda i, cb=cb: (i, cb)) for (_, w, cb) in tiled]
    in_specs += [pl.BlockSpec(f.shape, lambda i, nd=f.ndim: (0,) * nd) for f in full]
    out_specs = [pl.BlockSpec((tile, w), lambda i: (i, 0)) for (w, _) in out_tiled]
    out_specs += [pl.BlockSpec(s, lambda i, nd=len(s): (0,) * nd) for s in out_acc]
    out_shape = [jax.ShapeDtypeStruct((rows, w), dt) for (w, dt) in out_tiled]
    out_shape += [jax.ShapeDtypeStruct(s, F32) for s in out_acc]
    res = pl.pallas_call(
        body,
        grid=(rows // tile,),
        in_specs=in_specs,
        out_specs=out_specs,
        out_shape=out_shape,
        compiler_params=_cp(("arbitrary",) if out_acc else ("parallel",)),
        name=name,
    )(*[t[0] for t in tiled], *full)
    return res


def adam_reduce(parts, w, m, v, name):
    P, R, C = parts.shape
    tr = _pick(R, (64, 32, 16, 8))
    c1 = 1.0 - ADAM_B1 ** ADAM_STEP
    c2 = 1.0 - ADAM_B2 ** ADAM_STEP

    def body(p_ref, w_ref, m_ref, v_ref, g_ref, d_ref, nm_ref, nv_ref):
        g = p_ref[0].astype(F32)
        for j in range(1, P):
            g = g + p_ref[j].astype(F32)
        mm = ADAM_B1 * m_ref[...] + (1.0 - ADAM_B1) * g
        vv = ADAM_B2 * v_ref[...] + (1.0 - ADAM_B2) * (g * g)
        m_hat = mm / c1
        v_hat = vv / c2
        g_ref[...] = g
        d_ref[...] = -ADAM_LR * (m_hat / (jnp.sqrt(v_hat) + ADAM_EPS) + ADAM_WD * w_ref[...])
        nm_ref[...] = mm
        nv_ref[...] = vv

    spec = pl.BlockSpec((tr, C), lambda i: (i, 0))
    return pl.pallas_call(
        body,
        grid=(R // tr,),
        in_specs=[pl.BlockSpec((P, tr, C), lambda i: (0, i, 0)), spec, spec, spec],
        out_specs=[spec] * 4,
        out_shape=[jax.ShapeDtypeStruct((R, C), F32)] * 4,
        compiler_params=_cp(("parallel",)),
        name=name,
    )(parts, w, m, v)


def reduce_parts(parts, name):
    P, R, C = parts.shape
    tr = _pick(R, (64, 32, 16, 8))

    def body(p_ref, g_ref):
        g = p_ref[0]
        for j in range(1, P):
            g = g + p_ref[j]
        g_ref[...] = g

    return pl.pallas_call(
        body,
        grid=(R // tr,),
        in_specs=[pl.BlockSpec((P, tr, C), lambda i: (0, i, 0))],
        out_specs=pl.BlockSpec((tr, C), lambda i: (i, 0)),
        out_shape=jax.ShapeDtypeStruct((R, C), F32),
        compiler_params=_cp(("parallel",)),
        name=name,
    )(parts)


def pack(arrs):
    flat = jnp.concatenate([a.reshape(-1).astype(F32) for a in arrs])
    n = flat.shape[0]
    padded = -(-n // 1024) * 1024
    return jnp.pad(flat, (0, padded - n)).reshape(padded // 128, 128)


def unpack(slab, shapes):
    flat = slab.reshape(-1)
    out, off = [], 0
    for s in shapes:
        size = int(np.prod(s))
        out.append(flat[off:off + size].reshape(s))
        off += size
    return out


def modulate_fwd(x, g, sh, sc, name):
    def fn(x, g, sh, sc):
        return x * _rstd(x) * g * (1.0 + sc) + sh

    return rowwise(fn, [x], [g, sh, sc], [(x.shape[1], BF16)], [], 256, name)[0]


def norm_rope_fwd(p, width, cb, w, cosf, sinf, hd, name):
    nh = width // hd

    def fn(x, cosf, sinf, w):
        outs = []
        for h in range(nh):
            xh = x[:, h * hd:(h + 1) * hd]
            y = xh * _rstd(xh) * w
            outs.append(y * cosf + pltpu.roll(y, hd // 2, 1) * sinf)
        return jnp.concatenate(outs, axis=1) if nh > 1 else outs[0]

    return rowwise(fn, [(p, width, cb), cosf, sinf], [w], [(width, BF16)], [], 256, name)[0]


def norm_rope_bwd(p, width, cb, d, w, cosf, sinf, hd, name):
    nh = width // hd

    def fn(x, d, cosf, sinf, w):
        outs = []
        dw = jnp.zeros((1, hd), F32)
        for h in range(nh):
            xh = x[:, h * hd:(h + 1) * hd]
            dh = d[:, h * hd:(h + 1) * hd].astype(F32)
            r = _rstd(xh)
            n = xh * r
            dy = dh * cosf + pltpu.roll(dh * sinf, hd // 2, 1)
            dw = dw + jnp.sum(dy * n, axis=0, keepdims=True)
            dn = dy * w
            outs.append(r * (dn - n * jnp.mean(dn * n, axis=-1, keepdims=True)))
        return (jnp.concatenate(outs, axis=1) if nh > 1 else outs[0]), dw

    return rowwise(fn, [(p, width, cb), d, cosf, sinf], [w], [(width, BF16)], [(1, hd)], 256, name)


def attention_fwd(qr, kr, pkv, kcr, pkv_c, sink, hkv, hd, name):
    T, L = qr.shape[0], kcr.shape[0]
    G = qr.shape[1] // (hkv * hd)
    nb = T // BLOCK
    scale = hd ** -0.5

    def body(q_ref, kp, kc, kn, vp, vc, vn, ck_ref, cv_ref, sink_ref, o_ref, lse_ref):
        i = pl.program_id(1)
        kwin = jnp.concatenate([kp[...], kc[...], kn[...]], axis=0)
        vwin = jnp.concatenate([vp[...], vc[...], vn[...]], axis=0).astype(BF16)
        ck, cv = ck_ref[...], cv_ref[...].astype(BF16)
        row = lax.broadcasted_iota(jnp.int32, (BLOCK, 3 * BLOCK), 0)
        col = lax.broadcasted_iota(jnp.int32, (BLOCK, 3 * BLOCK), 1)
        rel = col - BLOCK - row
        valid = (jnp.abs(rel) <= WINDOW) & ((col >= BLOCK) | (i > 0)) & ((col < 2 * BLOCK) | (i < nb - 1))
        for g in range(G):
            q = q_ref[:, g * hd:(g + 1) * hd]
            s = jnp.where(valid, _dot(q, kwin, NT) * scale, NEG)
            sc = _dot(q, ck, NT) * scale
            sk = sink_ref[g]
            m = jnp.maximum(jnp.maximum(jnp.max(s, axis=1, keepdims=True), jnp.max(sc, axis=1, keepdims=True)), sk)
            p = jnp.exp(s - m)
            pc = jnp.exp(sc - m)
            den = jnp.exp(sk - m) + jnp.sum(p, axis=1, keepdims=True) + jnp.sum(pc, axis=1, keepdims=True)
            o = (_dot(p.astype(BF16), vwin, NN) + _dot(pc.astype(BF16), cv, NN)) / den
            o_ref[:, g * hd:(g + 1) * hd] = o.astype(o_ref.dtype)
            lse_ref[g] = m + jnp.log(den)

    kv_specs = [
        pl.BlockSpec((BLOCK, hd), lambda h, i: (jnp.maximum(i - 1, 0), h)),
        pl.BlockSpec((BLOCK, hd), lambda h, i: (i, h)),
        pl.BlockSpec((BLOCK, hd), lambda h, i: (jnp.minimum(i + 1, nb - 1), h)),
    ]
    v_specs = [
        pl.BlockSpec((BLOCK, hd), lambda h, i: (jnp.maximum(i - 1, 0), hkv + h)),
        pl.BlockSpec((BLOCK, hd), lambda h, i: (i, hkv + h)),
        pl.BlockSpec((BLOCK, hd), lambda h, i: (jnp.minimum(i + 1, nb - 1), hkv + h)),
    ]
    return pl.pallas_call(
        body,
        grid=(hkv, nb),
        in_specs=[pl.BlockSpec((BLOCK, G * hd), lambda h, i: (i, h))] + kv_specs + v_specs + [
            pl.BlockSpec((L, hd), lambda h, i: (0, h)),
            pl.BlockSpec((L, hd), lambda h, i: (0, hkv + h)),
            pl.BlockSpec((G, 1, 1), lambda h, i: (h, 0, 0)),
        ],
        out_specs=[
            pl.BlockSpec((BLOCK, G * hd), lambda h, i: (i, h)),
            pl.BlockSpec((G, BLOCK, 1), lambda h, i: (h, i, 0)),
        ],
        out_shape=[jax.ShapeDtypeStruct(qr.shape, BF16), jax.ShapeDtypeStruct((hkv * G, T, 1), F32)],
        compiler_params=_cp(("parallel", "parallel")),
        name=name,
    )(qr, kr, kr, kr, pkv, pkv, pkv, kcr, pkv_c, sink)


def attention_bwd_q(qr, kr, pkv, kcr, pkv_c, sink, do, o, lse, hkv, hd, name):
    T, L = qr.shape[0], kcr.shape[0]
    G = qr.shape[1] // (hkv * hd)
    nb = T // BLOCK
    scale = hd ** -0.5

    def body(q_ref, kp, kc, kn, vp, vc, vn, ck_ref, cv_ref, sink_ref, do_ref, o_ref, lse_ref,
             dq_ref, dck_ref, dcv_ref, dsink_ref):
        i = pl.program_id(1)

        @pl.when(i == 0)
        def _():
            dck_ref[...] = jnp.zeros_like(dck_ref)
            dcv_ref[...] = jnp.zeros_like(dcv_ref)
            dsink_ref[...] = jnp.zeros_like(dsink_ref)

        kwin = jnp.concatenate([kp[...], kc[...], kn[...]], axis=0)
        vwin = jnp.concatenate([vp[...], vc[...], vn[...]], axis=0).astype(BF16)
        ck, cv = ck_ref[...], cv_ref[...].astype(BF16)
        row = lax.broadcasted_iota(jnp.int32, (BLOCK, 3 * BLOCK), 0)
        col = lax.broadcasted_iota(jnp.int32, (BLOCK, 3 * BLOCK), 1)
        rel = col - BLOCK - row
        valid = (jnp.abs(rel) <= WINDOW) & ((col >= BLOCK) | (i > 0)) & ((col < 2 * BLOCK) | (i < nb - 1))
        dck = jnp.zeros((L, hd), F32)
        dcv = jnp.zeros((L, hd), F32)
        for g in range(G):
            q = q_ref[:, g * hd:(g + 1) * hd]
            dog = do_ref[:, g * hd:(g + 1) * hd]
            og = o_ref[:, g * hd:(g + 1) * hd]
            lg = lse_ref[g]
            sk = sink_ref[g]
            s = jnp.where(valid, _dot(q, kwin, NT) * scale, NEG)
            sc = _dot(q, ck, NT) * scale
            p = jnp.exp(s - lg)
            pc = jnp.exp(sc - lg)
            dr = jnp.sum(dog.astype(F32) * og.astype(F32), axis=1, keepdims=True)
            dp = _dot(dog, vwin, NT)
            dpc = _dot(dog, cv, NT)
            ds = (p * (dp - dr) * scale).astype(BF16)
            dsc = (pc * (dpc - dr) * scale).astype(BF16)
            dq_ref[:, g * hd:(g + 1) * hd] = _dot(ds, kwin, NN) + _dot(dsc, ck, NN)
            dck = dck + _dot(dsc, q, TN)
            dcv = dcv + _dot(pc.astype(BF16), dog, TN)
            dsink_ref[g] += -jnp.sum(jnp.exp(sk - lg) * dr, axis=0, keepdims=True)
        dck_ref[...] += dck
        dcv_ref[...] += dcv

    kv_specs = [
        pl.BlockSpec((BLOCK, hd), lambda h, i: (jnp.maximum(i - 1, 0), h)),
        pl.BlockSpec((BLOCK, hd), lambda h, i: (i, h)),
        pl.BlockSpec((BLOCK, hd), lambda h, i: (jnp.minimum(i + 1, nb - 1), h)),
    ]
    v_specs = [
        pl.BlockSpec((BLOCK, hd), lambda h, i: (jnp.maximum(i - 1, 0), hkv + h)),
        pl.BlockSpec((BLOCK, hd), lambda h, i: (i, hkv + h)),
        pl.BlockSpec((BLOCK, hd), lambda h, i: (jnp.minimum(i + 1, nb - 1), hkv + h)),
    ]
    qspec = pl.BlockSpec((BLOCK, G * hd), lambda h, i: (i, h))
    return pl.pallas_call(
        body,
        grid=(hkv, nb),
        in_specs=[qspec] + kv_specs + v_specs + [
            pl.BlockSpec((L, hd), lambda h, i: (0, h)),
            pl.BlockSpec((L, hd), lambda h, i: (0, hkv + h)),
            pl.BlockSpec((G, 1, 1), lambda h, i: (h, 0, 0)),
            qspec, qspec,
            pl.BlockSpec((G, BLOCK, 1), lambda h, i: (h, i, 0)),
        ],
        out_specs=[
            qspec,
            pl.BlockSpec((L, hd), lambda h, i: (0, h)),
            pl.BlockSpec((L, hd), lambda h, i: (0, h)),
            pl.BlockSpec((G, 1, 1), lambda h, i: (h, 0, 0)),
        ],
        out_shape=[
            jax.ShapeDtypeStruct(qr.shape, F32),
            jax.ShapeDtypeStruct((L, hkv * hd), F32),
            jax.ShapeDtypeStruct((L, hkv * hd), F32),
            jax.ShapeDtypeStruct((hkv * G, 1, 1), F32),
        ],
        compiler_params=_cp(("parallel", "arbitrary")),
        name=name,
    )(qr, kr, kr, kr, pkv, pkv, pkv, kcr, pkv_c, sink, do, o, lse)


def attention_bwd_kv(qr, kr, pkv, do, o, lse, hkv, hd, name):
    T = qr.shape[0]
    G = qr.shape[1] // (hkv * hd)
    nb = T // BLOCK
    scale = hd ** -0.5

    def body(k_ref, v_ref, *refs):
        qs, dos, os_, lses = refs[0:3], refs[3:6], refs[6:9], refs[9:12]
        dk_ref, dv_ref = refs[12:]
        j = pl.program_id(1)
        k = k_ref[...]
        v = v_ref[...].astype(BF16)
        row = lax.broadcasted_iota(jnp.int32, (BLOCK, BLOCK), 0)
        col = lax.broadcasted_iota(jnp.int32, (BLOCK, BLOCK), 1)
        dk = jnp.zeros((BLOCK, hd), F32)
        dv = jnp.zeros((BLOCK, hd), F32)
        for d in range(3):
            delta = d - 1
            iq = j + delta
            ok = (iq >= 0) & (iq < nb)
            rel = col - row - delta * BLOCK
            valid = (jnp.abs(rel) <= WINDOW) & ok
            for g in range(G):
                q = qs[d][:, g * hd:(g + 1) * hd]
                dog = dos[d][:, g * hd:(g + 1) * hd]
                og = os_[d][:, g * hd:(g + 1) * hd]
                lg = lses[d][g]
                s = jnp.where(valid, _dot(q, k, NT) * scale, NEG)
                p = jnp.exp(s - lg)
                dr = jnp.sum(dog.astype(F32) * og.astype(F32), axis=1, keepdims=True)
                dp = _dot(dog, v, NT)
                ds = (p * (dp - dr) * scale).astype(BF16)
                dk = dk + _dot(ds, q, TN)
                dv = dv + _dot(p.astype(BF16), dog, TN)
        dk_ref[...] = dk.astype(dk_ref.dtype)
        dv_ref[...] = dv.astype(dv_ref.dtype)

    def q3(width_block):
        return [
            pl.BlockSpec(width_block, lambda h, j: (jnp.maximum(j - 1, 0), h)),
            pl.BlockSpec(width_block, lambda h, j: (j, h)),
            pl.BlockSpec(width_block, lambda h, j: (jnp.minimum(j + 1, nb - 1), h)),
        ]

    lse3 = [
        pl.BlockSpec((G, BLOCK, 1), lambda h, j: (h, jnp.maximum(j - 1, 0), 0)),
        pl.BlockSpec((G, BLOCK, 1), lambda h, j: (h, j, 0)),
        pl.BlockSpec((G, BLOCK, 1), lambda h, j: (h, jnp.minimum(j + 1, nb - 1), 0)),
    ]
    qb = (BLOCK, G * hd)
    return pl.pallas_call(
        body,
        grid=(hkv, nb),
        in_specs=[pl.BlockSpec((BLOCK, hd), lambda h, j: (j, h)), pl.BlockSpec((BLOCK, hd), lambda h, j: (j, hkv + h))]
        + q3(qb) + q3(qb) + q3(qb) + lse3,
        out_specs=[pl.BlockSpec((BLOCK, hd), lambda h, j: (j, h))] * 2,
        out_shape=[jax.ShapeDtypeStruct((T, hkv * hd), BF16)] * 2,
        compiler_params=_cp(("parallel", "parallel")),
        name=name,
    )(kr, pkv, qr, qr, qr, do, do, do, o, o, o, lse, lse, lse)


def gate_fwd(plr, wf, wb, bf, bb, name):
    n = wf.shape[1]

    def fn(lr, wf, wb, bf, bb):
        lrb = lr.astype(BF16)
        outs = []
        for w, b in ((wf, bf), (wb, bb)):
            z = _dot(lrb, w.astype(BF16), NN) + b
            outs.append((jnp.minimum(z, 0.0) - jnp.log(1.0 + jnp.exp(-jnp.abs(z)))) / GLA_GATE_NORM)
        return outs

    return rowwise(fn, [plr], [wf, wb, bf, bb], [(n, F32), (n, F32)], [], 256, name)


def gate_bwd(plr, dgf, dgb, wf, wb, bf, bb, name):
    n = wf.shape[1]

    def fn(lr, dgf, dgb, wf, wb, bf, bb):
        lrb = lr.astype(BF16)
        dlr = jnp.zeros(lr.shape, F32)
        res = []
        for w, b, dg in ((wf, bf, dgf), (wb, bb, dgb)):
            wb16 = w.astype(BF16)
            z = _dot(lrb, wb16, NN) + b
            dz = dg * _sig(-z) / GLA_GATE_NORM
            dzb = dz.astype(BF16)
            dlr = dlr + _dot(dzb, wb16, NT)
            res += [_dot(lrb, dzb, TN), jnp.sum(dz, axis=0, keepdims=True)]
        return [dlr] + res

    return rowwise(fn, [plr, dgf, dgb], [wf, wb, bf, bb], [(128, BF16)],
                   [(128, n), (1, n), (128, n), (1, n)], 256, name)


def _tri_dot(tri_b, x):
    x1 = x.astype(BF16)
    r1 = x - x1.astype(F32)
    x2 = r1.astype(BF16)
    x3 = (r1 - x2.astype(F32)).astype(BF16)
    return _dot(tri_b, x1, NN) + _dot(tri_b, x2, NN) + _dot(tri_b, x3, NN)


def gla_fwd(pqk, pv, gl, s0, heads, reverse, name, o_add=None):
    T = pqk.shape[0]
    dk = pqk.shape[1] // (2 * heads)
    dv = pv.shape[1] // heads
    C = GLA_CHUNK
    nc = T // C
    qscale = dk ** -0.5

    def body(*refs):
        if o_add is None:
            q_ref, k_ref, v_ref, g_ref, s0_ref, o_ref, st_ref, sf_ref, S = refs
            oa_ref = None
        else:
            q_ref, k_ref, v_ref, g_ref, s0_ref, oa_ref, o_ref, st_ref, sf_ref, S = refs
        n = pl.program_id(1)

        @pl.when(n == 0)
        def _():
            S[...] = s0_ref[0]

        r = lax.broadcasted_iota(jnp.int32, (C, C), 0)
        c = lax.broadcasted_iota(jnp.int32, (C, C), 1)
        tri = (r <= c) if reverse else (r >= c)
        g = g_ref[...]
        b = _tri_dot(tri.astype(BF16), g)
        bl = jnp.sum(g, axis=0, keepdims=True)
        q = q_ref[...].astype(F32) * qscale
        k = k_ref[...].astype(F32)
        v = v_ref[...].astype(BF16)
        qe = (q * jnp.exp(b)).astype(BF16)
        ke = (k * jnp.exp(-b)).astype(BF16)
        kl = (k * jnp.exp(bl - b)).astype(BF16)
        st = S[...]
        st_ref[0, 0] = st
        a = jnp.where(tri, _dot(qe, ke, NT), 0.0)
        o = _dot(qe, st.astype(BF16), NT) + _dot(a.astype(BF16), v, NN)
        if oa_ref is not None:
            o = o + oa_ref[...]
        o_ref[...] = o
        snew = st * jnp.exp(bl) + _dot(v, kl, TN)
        S[...] = snew
        sf_ref[0] = snew

    def ci(n):
        return (nc - 1 - n) if reverse else n

    specs = [
        pl.BlockSpec((C, dk), lambda h, n: (ci(n), h)),
        pl.BlockSpec((C, dk), lambda h, n: (ci(n), heads + h)),
        pl.BlockSpec((C, dv), lambda h, n: (ci(n), h)),
        pl.BlockSpec((C, dk), lambda h, n: (ci(n), h)),
        pl.BlockSpec((1, dv, dk), lambda h, n: (h, 0, 0)),
    ]
    ins = [pqk, pqk, pv, gl, s0]
    if o_add is not None:
        specs.append(pl.BlockSpec((C, dv), lambda h, n: (ci(n), h)))
        ins.append(o_add)
    return pl.pallas_call(
        body,
        grid=(heads, nc),
        in_specs=specs,
        out_specs=[
            pl.BlockSpec((C, dv), lambda h, n: (ci(n), h)),
            pl.BlockSpec((1, 1, dv, dk), lambda h, n: (ci(n), h, 0, 0)),
            pl.BlockSpec((1, dv, dk), lambda h, n: (h, 0, 0)),
        ],
        out_shape=[
            jax.ShapeDtypeStruct((T, heads * dv), F32),
            jax.ShapeDtypeStruct((nc, heads, dv, dk), F32),
            jax.ShapeDtypeStruct((heads, dv, dk), F32),
        ],
        scratch_shapes=[pltpu.VMEM((dv, dk), F32)],
        compiler_params=_cp(("parallel", "arbitrary")),
        name=name,
    )(*ins)


def gla_bwd(pqk, pv, gl, states, do, dsf, heads, reverse, name, acc=None):
    T = pqk.shape[0]
    dk = pqk.shape[1] // (2 * heads)
    dv = pv.shape[1] // heads
    C = GLA_CHUNK
    nc = T // C
    qscale = dk ** -0.5

    def body(*refs):
        if acc is None:
            q_ref, k_ref, v_ref, g_ref, st_ref, do_ref, dsf_ref, dq_ref, dk_ref, dv_ref, dg_ref, ds0_ref, dS = refs
            aq = ak = av = None
        else:
            (q_ref, k_ref, v_ref, g_ref, st_ref, do_ref, dsf_ref, aq, ak, av,
             dq_ref, dk_ref, dv_ref, dg_ref, ds0_ref, dS) = refs
        n = pl.program_id(1)

        @pl.when(n == 0)
        def _():
            dS[...] = dsf_ref[0]

        r = lax.broadcasted_iota(jnp.int32, (C, C), 0)
        c = lax.broadcasted_iota(jnp.int32, (C, C), 1)
        tri = (r <= c) if reverse else (r >= c)
        tri_t = (r >= c) if reverse else (r <= c)
        g = g_ref[...]
        b = _tri_dot(tri.astype(BF16), g)
        bl = jnp.sum(g, axis=0, keepdims=True)
        eb, enb, elb, ebl = jnp.exp(b), jnp.exp(-b), jnp.exp(bl - b), jnp.exp(bl)
        q = q_ref[...].astype(F32) * qscale
        k = k_ref[...].astype(F32)
        vb = v_ref[...].astype(BF16)
        qe, ke, kl = q * eb, k * enb, k * elb
        qeb, keb, klb = qe.astype(BF16), ke.astype(BF16), kl.astype(BF16)
        st = st_ref[0, 0]
        dstp = dS[...]
        stb, dstb = st.astype(BF16), dstp.astype(BF16)
        dob = do_ref[...].astype(BF16)
        p = jnp.where(tri, _dot(qeb, keb, NT), 0.0).astype(BF16)
        dp = jnp.where(tri, _dot(dob, vb, NT), 0.0).astype(BF16)
        dqe = _dot(dob, stb, NN) + _dot(dp, keb, NN)
        dke = _dot(dp, qeb, TN)
        dvv = _dot(p, dob, TN) + _dot(klb, dstb, NT)
        dkl = _dot(vb, dstb, NN)
        dbl = ebl * jnp.sum(dstp * st, axis=0, keepdims=True) + jnp.sum(dkl * kl, axis=0, keepdims=True)
        dsn = _dot(dob, qeb, TN) + dstp * ebl
        dS[...] = dsn
        ds0_ref[0] = dsn
        dq = dqe * eb * qscale
        dkk = dke * enb + dkl * elb
        db = dqe * qe - dke * ke - dkl * kl
        dg_ref[...] = _tri_dot(tri_t.astype(BF16), db) + dbl
        if aq is not None:
            dq = dq + aq[...].astype(F32)
            dkk = dkk + ak[...].astype(F32)
            dvv = dvv + av[...].astype(F32)
        dq_ref[...] = dq.astype(dq_ref.dtype)
        dk_ref[...] = dkk.astype(dk_ref.dtype)
        dv_ref[...] = dvv.astype(dv_ref.dtype)

    def ci(n):
        return n if reverse else (nc - 1 - n)

    kspec = pl.BlockSpec((C, dk), lambda h, n: (ci(n), h))
    vspec = pl.BlockSpec((C, dv), lambda h, n: (ci(n), h))
    sspec = pl.BlockSpec((1, dv, dk), lambda h, n: (h, 0, 0))
    specs = [
        kspec,
        pl.BlockSpec((C, dk), lambda h, n: (ci(n), heads + h)),
        vspec,
        kspec,
        pl.BlockSpec((1, 1, dv, dk), lambda h, n: (ci(n), h, 0, 0)),
        vspec,
        sspec,
    ]
    ins = [pqk, pqk, pv, gl, states, do, dsf]
    odt = F32 if acc is None else BF16
    if acc is not None:
        specs += [kspec, kspec, vspec]
        ins += list(acc)
    return pl.pallas_call(
        body,
        grid=(heads, nc),
        in_specs=specs,
        out_specs=[kspec, kspec, vspec, kspec, sspec],
        out_shape=[
            jax.ShapeDtypeStruct((T, heads * dk), odt),
            jax.ShapeDtypeStruct((T, heads * dk), odt),
            jax.ShapeDtypeStruct((T, heads * dv), odt),
            jax.ShapeDtypeStruct((T, heads * dk), F32),
            jax.ShapeDtypeStruct((heads, dv, dk), F32),
        ],
        scratch_shapes=[pltpu.VMEM((dv, dk), F32)],
        compiler_params=_cp(("parallel", "arbitrary")),
        name=name,
    )(*ins)


def gla_out_fwd(og, prb, gn, heads, name):
    dv = og.shape[1] // heads

    def fn(og, rb, gn):
        outs = []
        for h in range(heads):
            oh = og[:, h * dv:(h + 1) * dv]
            outs.append(oh * _rstd(oh) * gn)
        y = jnp.concatenate(outs, axis=1)
        return y * (rb * _sig(rb))

    return rowwise(fn, [og, prb], [gn], [(og.shape[1], BF16)], [], 256, name)[0]


def gla_out_bwd(og, prb, du, gn, heads, name):
    dv = og.shape[1] // heads

    def fn(og, rb, du, gn):
        sg = _sig(rb)
        silu = rb * sg
        dsilu = sg * (1.0 + rb * (1.0 - sg))
        dog, ys = [], []
        dgn = jnp.zeros((1, dv), F32)
        for h in range(heads):
            sl = slice(h * dv, (h + 1) * dv)
            oh = og[:, sl]
            r = _rstd(oh)
            n = oh * r
            ys.append(n * gn)
            dy = du[:, sl] * silu[:, sl]
            dgn = dgn + jnp.sum(dy * n, axis=0, keepdims=True)
            dn = dy * gn
            dog.append(r * (dn - n * jnp.mean(dn * n, axis=-1, keepdims=True)))
        y = jnp.concatenate(ys, axis=1)
        return jnp.concatenate(dog, axis=1), du * y * dsilu, dgn

    return rowwise(fn, [og, prb, du], [gn], [(og.shape[1], F32), (og.shape[1], BF16)], [(1, dv)], 128, name)


def conv_specs(T, tt, tc, off, order):
    r8 = tt // 8
    last8 = T // 8 - 1
    if order == "ij":
        return [
            pl.BlockSpec((tt, tc), lambda i, j: (i, j + off)),
            pl.BlockSpec((8, tc), lambda i, j: (jnp.maximum(i * r8 - 1, 0), j + off)),
            pl.BlockSpec((8, tc), lambda i, j: (jnp.minimum((i + 1) * r8, last8), j + off)),
        ]
    return [
        pl.BlockSpec((tt, tc), lambda j, i: (i, j + off)),
        pl.BlockSpec((8, tc), lambda j, i: (jnp.maximum(i * r8 - 1, 0), j + off)),
        pl.BlockSpec((8, tc), lambda j, i: (jnp.minimum((i + 1) * r8, last8), j + off)),
    ]


def _shifted(u, hp, hn, i, nt_):
    tt = u.shape[0]
    row = lax.broadcasted_iota(jnp.int32, u.shape, 0)
    r8 = lax.broadcasted_iota(jnp.int32, hp.shape, 0)
    prev = jnp.sum(jnp.where(r8 == 7, hp, 0.0), axis=0, keepdims=True) * (i > 0).astype(F32)
    nxt = jnp.sum(jnp.where(r8 == 0, hn, 0.0), axis=0, keepdims=True) * (i < nt_ - 1).astype(F32)
    down = jnp.where(row == 0, prev, pltpu.roll(u, 1, 0))
    up = jnp.where(row == tt - 1, nxt, pltpu.roll(u, tt - 1, 0))
    return down, up


def conv_swiglu_fwd(u, cw, cb, name):
    T, F2 = u.shape
    F = F2 // 2
    tt = min(256, T)
    tc = _pick(F, (512, 256, 128))
    nt_, ncol = T // tt, F // tc

    def body(ua, uap, uan, ug, ugp, ugn, wa, wg, ba, bg, f_ref):
        i = pl.program_id(0)
        res = []
        for um, up_, un, w, b in ((ua, uap, uan, wa, ba), (ug, ugp, ugn, wg, bg)):
            x = um[...]
            down, up = _shifted(x, up_[...], un[...], i, nt_)
            res.append(w[0] * down + w[1] * x + w[2] * up + b[...])
        a, g = res
        f_ref[...] = (a * _sig(a) * g).astype(f_ref.dtype)

    wspec = lambda off: pl.BlockSpec((3, 1, tc), lambda i, j: (0, 0, j + off))
    bspec = lambda off: pl.BlockSpec((1, tc), lambda i, j: (0, j + off))
    return pl.pallas_call(
        body,
        grid=(nt_, ncol),
        in_specs=conv_specs(T, tt, tc, 0, "ij") + conv_specs(T, tt, tc, ncol, "ij")
        + [wspec(0), wspec(ncol), bspec(0), bspec(ncol)],
        out_specs=pl.BlockSpec((tt, tc), lambda i, j: (i, j)),
        out_shape=jax.ShapeDtypeStruct((T, F), BF16),
        compiler_params=_cp(("parallel", "parallel")),
        name=name,
    )(u, u, u, u, u, u, cw, cw, cb, cb)


def conv_swiglu_bwd(u, cw, cb, df, name):
    T, F2 = u.shape
    F = F2 // 2
    tt = min(256, T)
    tc = _pick(F, (512, 256, 128))
    nt_, ncol = T // tt, F // tc

    def body(ua, uap, uan, ug, ugp, ugn, wa, wg, ba, bg, df_ref, da_ref, dg_ref, dwa, dwg, dba, dbg):
        i = pl.program_id(1)

        @pl.when(i == 0)
        def _():
            for r in (dwa, dwg, dba, dbg):
                r[...] = jnp.zeros_like(r)

        sh = []
        res = []
        for um, up_, un, w, b in ((ua, uap, uan, wa, ba), (ug, ugp, ugn, wg, bg)):
            x = um[...]
            down, up = _shifted(x, up_[...], un[...], i, nt_)
            sh.append((down, x, up))
            res.append(w[0] * down + w[1] * x + w[2] * up + b[...])
        a, g = res
        d = df_ref[...].astype(F32)
        sg = _sig(a)
        da = d * g * sg * (1.0 + a * (1.0 - sg))
        dg = d * a * sg
        da_ref[...] = da
        dg_ref[...] = dg
        for dd, (down, x, up), dw, db in ((da, sh[0], dwa, dba), (dg, sh[1], dwg, dbg)):
            dw[0] += jnp.sum(dd * down, axis=0, keepdims=True)
            dw[1] += jnp.sum(dd * x, axis=0, keepdims=True)
            dw[2] += jnp.sum(dd * up, axis=0, keepdims=True)
            db[...] += jnp.sum(dd, axis=0, keepdims=True)

    wspec = lambda off: pl.BlockSpec((3, 1, tc), lambda j, i: (0, 0, j + off))
    bspec = lambda off: pl.BlockSpec((1, tc), lambda j, i: (0, j + off))
    tile = pl.BlockSpec((tt, tc), lambda j, i: (i, j))
    return pl.pallas_call(
        body,
        grid=(ncol, nt_),
        in_specs=conv_specs(T, tt, tc, 0, "ji") + conv_specs(T, tt, tc, ncol, "ji")
        + [wspec(0), wspec(ncol), bspec(0), bspec(ncol), tile],
        out_specs=[tile, tile, wspec(0), wspec(0), bspec(0), bspec(0)],
        out_shape=[
            jax.ShapeDtypeStruct((T, F), F32), jax.ShapeDtypeStruct((T, F), F32),
            jax.ShapeDtypeStruct((3, 1, F), F32), jax.ShapeDtypeStruct((3, 1, F), F32),
            jax.ShapeDtypeStruct((1, F), F32), jax.ShapeDtypeStruct((1, F), F32),
        ],
        compiler_params=_cp(("parallel", "arbitrary")),
        name=name,
    )(u, u, u, u, u, u, cw, cw, cb, cb, df)


def conv_transpose(d, cw, off, name):
    T, F = d.shape
    tt = min(256, T)
    tc = _pick(F, (512, 256, 128))
    nt_, ncol = T // tt, F // tc
    offb = off // tc

    def body(dm, dp_, dn, w, o_ref):
        i = pl.program_id(0)
        x = dm[...]
        down, up = _shifted(x, dp_[...], dn[...], i, nt_)
        o_ref[...] = (w[0] * up + w[1] * x + w[2] * down).astype(o_ref.dtype)

    return pl.pallas_call(
        body,
        grid=(nt_, ncol),
        in_specs=conv_specs(T, tt, tc, 0, "ij") + [pl.BlockSpec((3, 1, tc), lambda i, j: (0, 0, j + offb))],
        out_specs=pl.BlockSpec((tt, tc), lambda i, j: (i, j)),
        out_shape=jax.ShapeDtypeStruct((T, F), BF16),
        compiler_params=_cp(("parallel", "parallel")),
        name=name,
    )(d, d, d, cw)


def rope_tables(n, hd):
    rows = n // GRID_W
    row = jnp.repeat(jnp.arange(rows), GRID_W)
    col = jnp.tile(jnp.arange(GRID_W), rows)
    n_freq = hd // 4
    inv = ROPE_THETA ** (-jnp.arange(n_freq, dtype=F32) / n_freq)
    ang = jnp.concatenate([row[:, None] * inv, col[:, None] * inv], axis=-1)
    cos, sin = jnp.cos(ang), jnp.sin(ang)
    return jnp.concatenate([cos, cos], axis=-1), jnp.concatenate([-sin, sin], axis=-1)


def local_step(x, ctx, tgt, mod, modc, W, P):
    T, D = x.shape
    L = ctx.shape[0]
    hd, hq, hkv, gh = P["hd"], P["hq"], P["hkv"], P["gh"]
    sh1, sc1, g1, sh2, sc2, g2 = mod
    csh1, csc1 = modc
    kvw = hkv * hd
    gkw = W["gqk"].shape[1] // 2
    gdv = D // gh
    gdk = gkw // gh

    h = modulate_fwd(x, P["g_mix"], sh1, sc1, "mod1")
    hc = modulate_fwd(ctx, P["g_mix"], csh1, csc1, "mod1_ctx")
    pq = matmul(h, W["q"], "nn", F32, "proj_q")
    pkv = matmul(h, W["kv"], "nn", F32, "proj_kv")
    pgqk = matmul(h, W["gqk"], "nn", F32, "proj_gqk")
    pgv = matmul(h, W["gv"], "nn", F32, "proj_gv")
    prb = matmul(h, W["rb"], "nn", F32, "proj_rb")
    plr = matmul(h, W["lr"], "nn", F32, "proj_lr")
    pgab = matmul(h, W["gab"], "nn", F32, "proj_gab")
    pkv_c = matmul(hc, W["kv"], "nn", F32, "proj_kv_ctx")
    pgqk_c = matmul(hc, W["gqk"], "nn", F32, "proj_gqk_ctx")
    pgv_c = matmul(hc, W["gv"], "nn", F32, "proj_gv_ctx")
    plr_c = matmul(hc, W["lr"], "nn", F32, "proj_lr_ctx")

    cosf, sinf = rope_tables(T, hd)
    one_c, zero_c = jnp.ones((L, hd), F32), jnp.zeros((L, hd), F32)
    qr = norm_rope_fwd(pq, hq * hd, 0, P["q_norm"], cosf, sinf, hd, "qnorm")
    kr = norm_rope_fwd(pkv, kvw, 0, P["k_norm"], cosf, sinf, hd, "knorm")
    kcr = norm_rope_fwd(pkv_c, kvw, 0, P["k_norm"], one_c, zero_c, hd, "knorm_ctx")
    sink = P["attn_sink"].reshape(hq, 1, 1)
    o_attn, lse = attention_fwd(qr, kr, pkv, kcr, pkv_c, sink, hkv, hd, "attn_fwd")

    gf, gb = gate_fwd(plr, W["gate_f"], W["gate_b"], P["b_gate_f"], P["b_gate_b"], "gates")
    gfc, gbc = gate_fwd(plr_c, W["gate_f"], W["gate_b"], P["b_gate_f"], P["b_gate_b"], "gates_ctx")
    zero_state = jnp.zeros((gh, gdv, gdk), F32)
    _, st_cf, s_cf = gla_fwd(pgqk_c, pgv_c, gfc, zero_state, gh, False, "gla_ctx_f")
    _, st_cb, s_cb = gla_fwd(pgqk_c, pgv_c, gbc, zero_state, gh, True, "gla_ctx_b")
    of, st_f, _ = gla_fwd(pgqk, pgv, gf, s_cf, gh, False, "gla_f")
    og, st_b, _ = gla_fwd(pgqk, pgv, gb, s_cb, gh, True, "gla_b", o_add=of)
    ug = gla_out_fwd(og, prb, P["gla_norm"], gh, "gla_out")

    ya = matmul(o_attn, W["attn_o"], "nn", F32, "attn_o")
    yg = matmul(ug, W["gla_o"], "nn", F32, "gla_o")

    def merge_fn(ya, yg, ga, gb_):
        return _sig(ga) * ya + _sig(gb_) * yg

    z = rowwise(merge_fn, [ya, yg, (pgab, D, 0), (pgab, D, 1)], [], [(D, BF16)], [], 256, "merge")[0]
    mo = matmul(z, W["out"], "nn", F32, "w_out")

    def res_fn(x, mo, g1, gffn, sh2, sc2):
        x1 = x + g1 * mo
        return x1, x1 * _rstd(x1) * gffn * (1.0 + sc2) + sh2

    x1, h2 = rowwise(res_fn, [x, mo], [g1, P["g_ffn"], sh2, sc2], [(D, F32), (D, BF16)], [], 256, "res_mod2")
    u = matmul(h2, W["up"], "nn", F32, "w_up")
    cw3 = W["conv_w"].reshape(3, 1, -1)
    f = conv_swiglu_fwd(u, cw3, P["conv_b"], "conv_swiglu")
    fo = matmul(f, W["down"], "nn", F32, "w_down")

    def final_fn(x1, fo, tgt, g2):
        e = x1 + g2 * fo - tgt
        dy = e * (1.0 / D)
        lsum = jnp.sum(jnp.sum(e * e, axis=1, keepdims=True), axis=0, keepdims=True)
        return dy, dy * g2, jnp.broadcast_to(lsum, (1, 128)), jnp.sum(dy * fo, axis=0, keepdims=True)

    dy, dfo, lsum, dg2 = rowwise(final_fn, [x1, fo, tgt], [g2], [(D, F32), (D, BF16)], [(1, 128), (1, D)], 256, "loss")
    df = matmul(dfo, W["down"], "nt", BF16, "d_f")
    dw_down = matmul(f, dfo, "tn", BF16, "dw_down")
    da, dgg, dcw_a, dcw_g, dcb_a, dcb_g = conv_swiglu_bwd(u, cw3, P["conv_b"], df, "conv_swiglu_bwd")
    Fh = da.shape[1]
    du_a = conv_transpose(da, cw3, 0, "conv_t_a")
    du_g = conv_transpose(dgg, cw3, Fh, "conv_t_g")
    dh2 = matmul(du_a, W["up"][:, :Fh], "nt", F32, "d_h2_a")
    dh2 = matmul(du_g, W["up"][:, Fh:], "nt", F32, "d_h2_g", add=dh2)
    dw_up = jnp.concatenate([matmul(h2, du_a, "tn", BF16, "dw_up_a"), matmul(h2, du_g, "tn", BF16, "dw_up_g")], axis=1)

    def mod2_bwd_fn(x1, dh, dy, mo, gffn, sc2, g1):
        r = _rstd(x1)
        n = x1 * r
        dyy = dh * (1.0 + sc2)
        dn = dyy * gffn
        dx1 = dy + r * (dn - n * jnp.mean(dn * n, axis=-1, keepdims=True))
        s0 = lambda a: jnp.sum(a, axis=0, keepdims=True)
        return dx1, dx1 * g1, s0(dyy * n), s0(dh), s0(dh * n * gffn), s0(dx1 * mo)

    dx1, dmo, dg_ffn, dsh2, dsc2, dg1 = rowwise(
        mod2_bwd_fn, [x1, dh2, dy, mo], [P["g_ffn"], sc2, g1], [(D, F32), (D, BF16)], [(1, D)] * 4, 128, "mod2_bwd")
    dz = matmul(dmo, W["out"], "nt", F32, "d_z")
    dw_out = matmul(z, dmo, "tn", BF16, "dw_out")

    def merge_bwd_fn(dz, ya, yg, ga, gb_):
        sa, sb = _sig(ga), _sig(gb_)
        return dz * sa, dz * sb, jnp.concatenate([dz * ya * sa * (1.0 - sa), dz * yg * sb * (1.0 - sb)], axis=1)

    dya, dyg, dpgab = rowwise(merge_bwd_fn, [dz, ya, yg, (pgab, D, 0), (pgab, D, 1)], [],
                              [(D, BF16), (D, BF16), (2 * D, BF16)], [], 128, "merge_bwd")
    do_attn = matmul(dya, W["attn_o"], "nt", BF16, "d_oattn")
    dw_attn_o = matmul(o_attn, dya, "tn", BF16, "dw_attn_o")
    dug = matmul(dyg, W["gla_o"], "nt", F32, "d_ug")
    dw_gla_o = matmul(ug, dyg, "tn", BF16, "dw_gla_o")
    dog, dprb, dgn = gla_out_bwd(og, prb, dug, P["gla_norm"], gh, "gla_out_bwd")

    dq1, dk1, dv1, dgf, ds_cf = gla_bwd(pgqk, pgv, gf, st_f, dog, zero_state, gh, False, "gla_f_bwd")
    dgq, dgk, dpgv, dgb, ds_cb = gla_bwd(pgqk, pgv, gb, st_b, dog, zero_state, gh, True, "gla_b_bwd",
                                          acc=(dq1, dk1, dv1))
    dpgqk = jnp.concatenate([dgq, dgk], axis=1)
    zero_do = jnp.zeros((L, gh * gdv), F32)
    cq1, ck1, cv1, dgfc, _ = gla_bwd(pgqk_c, pgv_c, gfc, st_cf, zero_do, ds_cf, gh, False, "gla_ctx_f_bwd")
    cq, ck, dpgv_c, dgbc, _ = gla_bwd(pgqk_c, pgv_c, gbc, st_cb, zero_do, ds_cb, gh, True, "gla_ctx_b_bwd",
                                      acc=(cq1, ck1, cv1))
    dpgqk_c = jnp.concatenate([cq, ck], axis=1)
    dplr, dwgf, dbgf, dwgb, dbgb = gate_bwd(plr, dgf, dgb, W["gate_f"], W["gate_b"], P["b_gate_f"], P["b_gate_b"], "gates_bwd")
    dplr_c, dwgf_c, dbgf_c, dwgb_c, dbgb_c = gate_bwd(plr_c, dgfc, dgbc, W["gate_f"], W["gate_b"], P["b_gate_f"],
                                                      P["b_gate_b"], "gates_ctx_bwd")

    dqr, dkc_r, dvc, dsink = attention_bwd_q(qr, kr, pkv, kcr, pkv_c, sink, do_attn, o_attn, lse, hkv, hd, "attn_bwd_q")
    dkr, dv = attention_bwd_kv(qr, kr, pkv, do_attn, o_attn, lse, hkv, hd, "attn_bwd_kv")
    dpq, dqn = norm_rope_bwd(pq, hq * hd, 0, dqr, P["q_norm"], cosf, sinf, hd, "qnorm_bwd")
    dpk, dkn = norm_rope_bwd(pkv, kvw, 0, dkr, P["k_norm"], cosf, sinf, hd, "knorm_bwd")
    dpk_c, dkn_c = norm_rope_bwd(pkv_c, kvw, 0, dkc_r, P["k_norm"], one_c, zero_c, hd, "knorm_ctx_bwd")
    dpkv = jnp.concatenate([dpk, dv], axis=1)
    dpkv_c = jnp.concatenate([dpk_c, dvc.astype(BF16)], axis=1)

    dw_q = matmul(h, dpq, "tn", BF16, "dw_q")
    dw_kv = matmul(h, dpkv, "tn", BF16, "dw_kv", add=matmul(hc, dpkv_c, "tn", F32, "dw_kv_ctx"))
    dw_gqk = matmul(h, dpgqk, "tn", BF16, "dw_gqk", add=matmul(hc, dpgqk_c, "tn", F32, "dw_gqk_ctx"))
    dw_gv = matmul(h, dpgv, "tn", BF16, "dw_gv", add=matmul(hc, dpgv_c, "tn", F32, "dw_gv_ctx"))
    dw_rb = matmul(h, dprb, "tn", BF16, "dw_rb")
    dw_lr = matmul(h, dplr, "tn", BF16, "dw_lr", add=matmul(hc, dplr_c, "tn", F32, "dw_lr_ctx"))
    dw_gab = matmul(h, dpgab, "tn", BF16, "dw_gab")
    lrw = P["lowrank"]
    dw_in = jnp.concatenate([dw_q, dw_kv, dw_gqk, dw_gv, dw_rb, dw_lr[:, :2 * lrw], dw_gab], axis=1)

    dh = matmul(dpq, W["q"], "nt", F32, "dh_q")
    dh = matmul(dpkv, W["kv"], "nt", F32, "dh_kv", add=dh)
    dh = matmul(dpgqk, W["gqk"], "nt", F32, "dh_gqk", add=dh)
    dh = matmul(dpgv, W["gv"], "nt", F32, "dh_gv", add=dh)
    dh = matmul(dprb, W["rb"], "nt", F32, "dh_rb", add=dh)
    dh = matmul(dplr, W["lr"], "nt", F32, "dh_lr", add=dh)
    dh = matmul(dpgab, W["gab"], "nt", F32, "dh_gab", add=dh)
    dhc = matmul(dpkv_c, W["kv"], "nt", F32, "dhc_kv")
    dhc = matmul(dpgqk_c, W["gqk"], "nt", F32, "dhc_gqk", add=dhc)
    dhc = matmul(dpgv_c, W["gv"], "nt", F32, "dhc_gv", add=dhc)
    dhc = matmul(dplr_c, W["lr"], "nt", F32, "dhc_lr", add=dhc)

    def mod1_bwd_fn(x, dh, dres, g, sc):
        r = _rstd(x)
        n = x * r
        dyy = dh * (1.0 + sc)
        dn = dyy * g
        dx = dres + r * (dn - n * jnp.mean(dn * n, axis=-1, keepdims=True))
        s0 = lambda a: jnp.sum(a, axis=0, keepdims=True)
        return dx, s0(dyy * n), s0(dh), s0(dh * n * g)

    grad_x, dgmix, dsh1, dsc1 = rowwise(mod1_bwd_fn, [x, dh, dx1], [P["g_mix"], sc1], [(D, F32)], [(1, D)] * 3,
                                        128, "mod1_bwd")
    _, dgmix_c, dcsh1, dcsc1 = rowwise(mod1_bwd_fn, [ctx, dhc, jnp.zeros_like(ctx)], [P["g_mix"], csc1], [(D, F32)],
                                       [(1, D)] * 3, 128, "mod1_ctx_bwd")

    zD = jnp.zeros((1, D), F32)
    grads = dict(
        w_in=dw_in, w_attn_o=dw_attn_o, w_gla_o=dw_gla_o, w_out=dw_out, w_up=dw_up, w_down=dw_down,
        dmod_x=jnp.concatenate([dsh1, dsc1, dg1, dsh2, dsc2, dg2], axis=1),
        dmod_c=jnp.concatenate([dcsh1, dcsc1, zD, zD, zD, zD], axis=1),
        g_mix=dgmix + dgmix_c, q_norm=dqn, k_norm=dkn + dkn_c, attn_sink=dsink.reshape(1, hq),
        w_gate_f=(dwgf + dwgf_c)[:lrw], b_gate_f=dbgf + dbgf_c,
        w_gate_b=(dwgb + dwgb_c)[lrw:2 * lrw], b_gate_b=dbgb + dbgb_c,
        gla_norm=dgn, g_ffn=dg_ffn,
        conv_w=jnp.concatenate([dcw_a, dcw_g], axis=2).reshape(3, -1),
        conv_b=jnp.concatenate([dcb_a, dcb_g], axis=1),
    )
    return lsum[0, 0], grad_x, grads


SMALL_REPL = ("c_ctx", "b_mod", "g_mix", "q_norm", "k_norm", "attn_sink", "b_gate_f", "b_gate_b", "gla_norm", "g_ffn",
              "conv_b")
SMALL_SHARD = ("w_gate_f", "w_gate_b", "conv_w")
ORDER = ("c_ctx", "w_mod", "b_mod", "g_mix", "w_in", "q_norm", "k_norm", "attn_sink", "w_gate_f", "b_gate_f",
         "w_gate_b", "b_gate_b", "gla_norm", "w_attn_o", "w_gla_o", "w_out", "g_ffn", "w_up", "conv_w", "conv_b",
         "w_down")


def kernel(x, c, ctx, c_ctx, w_mod, b_mod, g_mix, w_in, q_norm, k_norm, attn_sink, w_gate_f, b_gate_f, w_gate_b, b_gate_b, gla_norm, w_attn_o, w_gla_o, w_out, g_ffn, w_up, conv_w, conv_b, w_down, loss_target, m_c_ctx, m_w_mod, m_b_mod, m_g_mix, m_w_in, m_q_norm, m_k_norm, m_attn_sink, m_w_gate_f, m_b_gate_f, m_w_gate_b, m_b_gate_b, m_gla_norm, m_w_attn_o, m_w_gla_o, m_w_out, m_g_ffn, m_w_up, m_conv_w, m_conv_b, m_w_down, v_c_ctx, v_w_mod, v_b_mod, v_g_mix, v_w_in, v_q_norm, v_k_norm, v_attn_sink, v_w_gate_f, v_b_gate_f, v_w_gate_b, v_b_gate_b, v_gla_norm, v_w_attn_o, v_w_gla_o, v_w_out, v_g_ffn, v_w_up, v_conv_w, v_conv_b, v_w_down):
    loc = dict(locals())
    Wt = {n: loc[n] for n in ORDER}
    Mt = {n: loc["m_" + n] for n in ORDER}
    Vt = {n: loc["v_" + n] for n in ORDER}
    me = 4 * lax.axis_index("x") + 2 * lax.axis_index("y") + lax.axis_index("c")

    D = x.shape[-1]
    hd = q_norm.shape[-1]
    hq = attn_sink.shape[-1]
    gdv = gla_norm.shape[-1]
    gh = D // gdv
    gdk = D // 2 // gh
    lrw = w_gate_f.shape[1]
    in_w = NDEV * w_in.shape[-1]
    kvw = (in_w - hq * hd - 2 * gh * gdk - 2 * gh * gdv - 2 * lrw - 2 * D) // 2
    hkv = kvw // hd
    gcols = w_gate_f.shape[-1]
    F2 = NDEV * w_up.shape[-1]
    mcols = w_mod.shape[-1]

    x2, ctx2, tgt2 = x[0], ctx[0], loss_target[0]

    c_all = exchange([jnp.pad(c, ((0, 7), (0, 0)))], True, "gather_c")[0][:, 0, :]
    c9 = jnp.concatenate([c_all, c_ctx[None, :], jnp.zeros((7, D), F32)], axis=0)
    s9 = rowwise(lambda a: a * _sig(a), [c9], [], [(D, F32)], [], 16, "silu_c")[0]
    bias = jnp.broadcast_to(lax.dynamic_slice_in_dim(b_mod, me * mcols, mcols, axis=1), (16, mcols))
    mod_cols = matmul(s9, w_mod[0], "nn", F32, "mod_cols", add=bias)
    mod_all = exchange([mod_cols], True, "gather_mod")[0]
    mod_all = jnp.transpose(mod_all, (1, 0, 2)).reshape(16, NDEV * mcols)
    mod_me = lax.dynamic_slice_in_dim(mod_all, me, 1, axis=0)
    mod = [mod_me[:, i * D:(i + 1) * D] for i in range(6)]
    modc = [mod_all[8:9, i * D:(i + 1) * D] for i in range(2)]

    o3 = jnp.stack([w_attn_o[0], w_gla_o[0], w_out[0]]).astype(BF16)
    small_w = pack([w_gate_f[0], w_gate_b[0], conv_w[0]])
    g_in, g_o3, g_up, g_down, g_small = gather_two_level(
        [w_in[0].astype(BF16), o3, w_up[0].astype(BF16), w_down[0].astype(BF16), small_w], "gather_w")
    win = jnp.transpose(g_in, (1, 0, 2)).reshape(D, in_w)
    offs = np.cumsum([0, hq * hd, 2 * kvw, 2 * gh * gdk, gh * gdv, gh * gdv, 2 * lrw, 2 * D]).tolist()
    seg = [win[:, offs[i]:offs[i + 1]] for i in range(7)]
    small_parts = [unpack(g_small[j], [w_gate_f[0].shape, w_gate_b[0].shape, conv_w[0].shape]) for j in range(NDEV)]
    wgf = jnp.concatenate([p[0] for p in small_parts], axis=1)
    wgb = jnp.concatenate([p[1] for p in small_parts], axis=1)
    cw_full = jnp.concatenate([p[2] for p in small_parts], axis=1)
    o3f = jnp.transpose(g_o3, (1, 0, 2, 3)).reshape(3, -1, D)
    W = dict(
        q=seg[0], kv=seg[1], gqk=seg[2], gv=seg[3], rb=seg[4],
        lr=jnp.pad(seg[5], ((0, 0), (0, 128 - 2 * lrw))), gab=seg[6],
        gate_f=jnp.pad(wgf, ((0, 128 - lrw), (0, 0))),
        gate_b=jnp.pad(wgb, ((lrw, 128 - 2 * lrw), (0, 0))),
        attn_o=o3f[0], gla_o=o3f[1], out=o3f[2],
        up=jnp.transpose(g_up, (1, 0, 2)).reshape(D, F2),
        down=g_down.reshape(-1, D),
        conv_w=cw_full,
    )
    P = dict(hd=hd, hq=hq, hkv=hkv, gh=gh, lowrank=lrw, g_mix=g_mix, q_norm=q_norm, k_norm=k_norm, attn_sink=attn_sink,
             b_gate_f=b_gate_f, b_gate_b=b_gate_b, gla_norm=gla_norm, g_ffn=g_ffn, conv_b=conv_b)

    lsum, grad_x, G = local_step(x2, ctx2, tgt2, mod, modc, W, P)
    loss = lax.psum(0.5 * lsum / D, ("x", "y", "c"))

    dm = exchange([jnp.concatenate([G["dmod_x"], G["dmod_c"], jnp.zeros((6, 6 * D), F32)], axis=0)], True,
                  "gather_dmod")[0]
    dmc = reduce_parts(dm[:, 1:2, :].reshape(NDEV, 6 * D // 128, 128), "sum_dmod_ctx").reshape(1, 6 * D)
    dM = jnp.concatenate([dm[:, 0, :], dmc, jnp.zeros((7, 6 * D), F32)], axis=0)
    dM_cols = lax.dynamic_slice_in_dim(dM, me * mcols, mcols, axis=1)
    g_w_mod = matmul(s9, dM_cols, "tn", F32, "dw_mod")
    g_b_mod = reduce_parts(dM.reshape(16, 6 * D // 128, 128), "sum_db_mod").reshape(1, 6 * D)
    dsc = matmul(dM_cols[8:16], w_mod[0], "nt", F32, "d_silu_ctx")
    cc = jnp.broadcast_to(c_ctx[None, :], (8, D))

    def dsilu_fn(d, a):
        sg = _sig(a)
        return d * sg * (1.0 + a * (1.0 - sg))

    g_cctx_part = rowwise(dsilu_fn, [dsc, cc], [], [(D, F32)], [], 8, "d_c_ctx")[0][0:1]

    small_names = ("c_ctx", "g_mix", "q_norm", "k_norm", "attn_sink", "b_gate_f", "b_gate_b", "gla_norm", "g_ffn",
                   "conv_b", "w_gate_f", "w_gate_b", "conv_w")
    G["c_ctx"] = g_cctx_part
    sm_shapes = [G[n].shape for n in small_names]
    sm_all = exchange([pack([G[n] for n in small_names])], True, "gather_small_grads")[0]
    sm_tot = unpack(reduce_parts(sm_all, "sum_small_grads"), sm_shapes)
    gs = dict(zip(small_names, sm_tot))
    gs["b_mod"] = g_b_mod
    gs["w_gate_f"] = lax.dynamic_slice_in_dim(gs["w_gate_f"], me * gcols, gcols, axis=1)
    gs["w_gate_b"] = lax.dynamic_slice_in_dim(gs["w_gate_b"], me * gcols, gcols, axis=1)
    ccols = conv_w.shape[-1]
    gs["conv_w"] = lax.dynamic_slice_in_dim(gs["conv_w"], me * ccols, ccols, axis=1)

    wcols = w_in.shape[-1]
    ucols = w_up.shape[-1]
    drows = w_down.shape[1]
    orows = w_attn_o.shape[1]
    s_in = jnp.transpose(G["w_in"].reshape(D, NDEV, wcols), (1, 0, 2))
    s_o3 = jnp.stack([G["w_attn_o"].reshape(NDEV, orows, D), G["w_gla_o"].reshape(NDEV, orows, D),
                      G["w_out"].reshape(NDEV, orows, D)], axis=1).reshape(NDEV, 3 * orows, D)
    s_up = jnp.transpose(G["w_up"].reshape(D, NDEV, ucols), (1, 0, 2))
    s_down = G["w_down"].reshape(NDEV, drows, D)
    r_in, r_o3, r_up, r_down = scatter_reduce([s_in, s_o3, s_up, s_down], "scatter_grads")

    out = {}
    out["w_in"] = adam_reduce(r_in, w_in[0], m_w_in[0], v_w_in[0], "adam_w_in")
    o3w = jnp.concatenate([w_attn_o[0], w_gla_o[0], w_out[0]], axis=0)
    o3m = jnp.concatenate([m_w_attn_o[0], m_w_gla_o[0], m_w_out[0]], axis=0)
    o3v = jnp.concatenate([v_w_attn_o[0], v_w_gla_o[0], v_w_out[0]], axis=0)
    ro3 = adam_reduce(r_o3, o3w, o3m, o3v, "adam_o3")
    for i, n in enumerate(("w_attn_o", "w_gla_o", "w_out")):
        out[n] = [a[i * orows:(i + 1) * orows] for a in ro3]
    out["w_up"] = adam_reduce(r_up, w_up[0], m_w_up[0], v_w_up[0], "adam_w_up")
    out["w_down"] = adam_reduce(r_down, w_down[0], m_w_down[0], v_w_down[0], "adam_w_down")
    out["w_mod"] = adam_reduce(g_w_mod[None], w_mod[0], m_w_mod[0], v_w_mod[0], "adam_w_mod")
    sm_names = SMALL_REPL + SMALL_SHARD
    shapes = [Wt[n].shape for n in sm_names]
    rs = adam_reduce(pack([gs[n] for n in sm_names])[None], pack([Wt[n] for n in sm_names]),
                     pack([Mt[n] for n in sm_names]), pack([Vt[n] for n in sm_names]), "adam_small")
    rs = [unpack(a, shapes) for a in rs]
    for i, n in enumerate(sm_names):
        out[n] = [a[i] for a in rs]

    res = [loss, grad_x[None]]
    for k in range(4):
        for n in ORDER:
            res.append(out[n][k].reshape(Wt[n].shape))
    return tuple(res)
```

```python
import jax
import jax.numpy as jnp
import numpy as np
from jax import lax
from jax.experimental import pallas as pl
from jax.experimental.pallas import tpu as pltpu

F32 = jnp.float32
BF16 = jnp.bfloat16

NDEV = 8
NCHIP = 4
EPS = 1e-6
WINDOW = 128
BLOCK = 128
GRID_W = 64
ROPE_THETA = 10000.0
GLA_CHUNK = 64
GLA_GATE_NORM = 16.0
ADAM_LR = 0.001
ADAM_B1 = 0.9
ADAM_B2 = 0.999
ADAM_EPS = 1e-08
ADAM_WD = 0.01
ADAM_STEP = 10
V7X_VMEM_LIMIT = 56 * 1024 * 1024
MATMUL_VMEM_BUDGET = 40 * 1024 * 1024
NEG = -1e30

NN = ((1,), (0,))
NT = ((1,), (1,))
TN = ((0,), (0,))


def _dot(a, b, dims):
    return lax.dot_general(a, b, (dims, ((), ())), preferred_element_type=F32)


def _cp(sem):
    return pltpu.CompilerParams(dimension_semantics=sem, vmem_limit_bytes=V7X_VMEM_LIMIT)


def _pick(n, cands):
    for c in cands:
        if n % c == 0:
            return c
    return n


def _sig(x):
    return 1.0 / (1.0 + jnp.exp(-x))


def _rstd(x):
    return lax.rsqrt(jnp.mean(x * x, axis=-1, keepdims=True) + EPS)


_ANY = pl.BlockSpec(memory_space=pl.ANY)


def _place():
    return lax.axis_index("x"), lax.axis_index("y"), lax.axis_index("c")


def exchange(srcs, bcast, name, group="all"):
    n = len(srcs)
    ndev = NDEV if group == "all" else NCHIP
    ks = tuple(range(1, NDEV)) if group == "all" else (2, 4, 6)
    out_shape = [jax.ShapeDtypeStruct((ndev,) + (s.shape if bcast else s.shape[1:]), s.dtype) for s in srcs]

    def body(*refs):
        src, dst = refs[:n], refs[n:2 * n]
        send_sems, recv_sems, loc_sems = refs[2 * n:]
        x, y, c = _place()

        def idx(px, py, pc):
            return 4 * px + 2 * py + pc if group == "all" else 2 * px + py

        me = idx(x, y, c)
        copies = []
        for a in range(n):
            cp = pltpu.make_async_copy(src[a] if bcast else src[a].at[me], dst[a].at[me], loc_sems.at[a])
            cp.start()
            copies.append(cp)
        for s, k in enumerate(ks):
            px, py, pc = x ^ ((k >> 2) & 1), y ^ ((k >> 1) & 1), c ^ (k & 1)
            for a in range(n):
                cp = pltpu.make_async_remote_copy(
                    src_ref=src[a] if bcast else src[a].at[idx(px, py, pc)],
                    dst_ref=dst[a].at[me],
                    send_sem=send_sems.at[a, s],
                    recv_sem=recv_sems.at[a, s],
                    device_id=(px, py, pc),
                    device_id_type=pl.DeviceIdType.MESH,
                )
                cp.start()
                copies.append(cp)
        for cp in copies:
            cp.wait()

    return pl.pallas_call(
        body,
        out_shape=out_shape,
        in_specs=[_ANY] * n,
        out_specs=[_ANY] * n,
        scratch_shapes=[
            pltpu.SemaphoreType.DMA((n, len(ks))),
            pltpu.SemaphoreType.DMA((n, len(ks))),
            pltpu.SemaphoreType.DMA((n,)),
        ],
        name=name,
    )(*srcs)


def gather_two_level(srcs, name):
    n = len(srcs)
    out_shape = [jax.ShapeDtypeStruct((NDEV,) + s.shape, s.dtype) for s in srcs]
    chips = (2, 4, 6)

    def body(*refs):
        src, dst = refs[:n], refs[n:2 * n]
        send_sems, recv_sems, loc_sems = refs[2 * n:]
        x, y, c = _place()
        me = 4 * x + 2 * y + c
        sib = (x, y, 1 - c)

        def copy(a, s, block, to, from_src):
            return pltpu.make_async_remote_copy(
                src_ref=src[a] if from_src else dst[a].at[block], dst_ref=dst[a].at[block],
                send_sem=send_sems.at[a, s], recv_sem=recv_sems.at[a, s],
                device_id=to, device_id_type=pl.DeviceIdType.MESH)

        sent = []
        for a in range(n):
            cp = pltpu.make_async_copy(src[a], dst[a].at[me], loc_sems.at[a])
            cp.start()
            sent.append(cp)
        for a in range(n):
            cp = copy(a, 0, me, sib, True)
            cp.start()
            sent.append(cp)
        for j, k in enumerate(chips):
            px, py = x ^ ((k >> 2) & 1), y ^ ((k >> 1) & 1)
            for a in range(n):
                cp = copy(a, 1 + j, me, (px, py, c), True)
                cp.start()
                sent.append(cp)
        for j, k in enumerate(chips):
            px, py = x ^ ((k >> 2) & 1), y ^ ((k >> 1) & 1)
            theirs = 4 * px + 2 * py + c
            for a in range(n):
                copy(a, 1 + j, theirs, (px, py, c), False).wait_recv()
                cp = copy(a, 4 + j, theirs, sib, False)
                cp.start()
                sent.append(cp)
        for a in range(n):
            copy(a, 0, 4 * x + 2 * y + (1 - c), sib, False).wait_recv()
        for j, k in enumerate(chips):
            px, py = x ^ ((k >> 2) & 1), y ^ ((k >> 1) & 1)
            for a in range(n):
                copy(a, 4 + j, 4 * px + 2 * py + (1 - c), sib, False).wait_recv()
        for cp in sent[:n]:
            cp.wait()
        for cp in sent[n:]:
            cp.wait_send()

    return pl.pallas_call(
        body,
        out_shape=out_shape,
        in_specs=[_ANY] * n,
        out_specs=[_ANY] * n,
        scratch_shapes=[
            pltpu.SemaphoreType.DMA((n, NDEV - 1)),
            pltpu.SemaphoreType.DMA((n, NDEV - 1)),
            pltpu.SemaphoreType.DMA((n,)),
        ],
        name=name,
    )(*srcs)


def pair_swap(srcs, name):
    n = len(srcs)

    def body(*refs):
        src, dst = refs[:n], refs[n:2 * n]
        send_sems, recv_sems = refs[2 * n:]
        x, y, c = _place()
        copies = []
        for a in range(n):
            cp = pltpu.make_async_remote_copy(
                src_ref=src[a], dst_ref=dst[a], send_sem=send_sems.at[a], recv_sem=recv_sems.at[a],
                device_id=(x, y, 1 - c), device_id_type=pl.DeviceIdType.MESH)
            cp.start()
            copies.append(cp)
        for cp in copies:
            cp.wait()

    return pl.pallas_call(
        body,
        out_shape=[jax.ShapeDtypeStruct(s.shape, s.dtype) for s in srcs],
        in_specs=[_ANY] * n,
        out_specs=[_ANY] * n,
        scratch_shapes=[pltpu.SemaphoreType.DMA((n,)), pltpu.SemaphoreType.DMA((n,))],
        name=name,
    )(*srcs)


def scatter_reduce(blocks, name):
    c = lax.axis_index("c")
    halves = [b.reshape((NCHIP, 2) + b.shape[1:]) for b in blocks]
    mine = [lax.dynamic_index_in_dim(h, c, axis=1, keepdims=False) for h in halves]
    theirs = [lax.dynamic_index_in_dim(h, 1 - c, axis=1, keepdims=False) for h in halves]
    got = pair_swap(theirs, name + "_d2d")
    sums = []
    for i, (m, g) in enumerate(zip(mine, got)):
        flat = (NCHIP * m.shape[1], m.shape[2])
        s = rowwise(lambda a, b: a.astype(F32) + b.astype(F32), [m.reshape(flat), g.reshape(flat)], [],
                    [(flat[1], m.dtype)], [], 64, f"{name}_pair_sum{i}")[0]
        sums.append(s.reshape(m.shape))
    return exchange(sums, False, name + "_ici", group="core")


def matmul(a, b, mode, out_dtype, name, add=None):
    if mode == "nn":
        (M, K), N = a.shape, b.shape[1]
    elif mode == "nt":
        (M, K), N = a.shape, b.shape[0]
    else:
        (K, M), N = a.shape, b.shape[1]
    tm = _pick(M, (1024, 512, 256, 128))
    tn = _pick(N, (1024, 512, 256, 128))
    osz = jnp.dtype(out_dtype).itemsize

    def vmem_bytes(tk):
        ops = 2 * tk * (tm * a.dtype.itemsize + tn * b.dtype.itemsize)
        return ops + tm * tn * (2 * osz + (4 if tk < K else 0) + (8 if add is not None else 0))

    tk = next((t for t in (K, 2816, 2048, 1408, 1024, 512, 256, 128) if K % t == 0 and vmem_bytes(t) <= MATMUL_VMEM_BUDGET), K)
    nk = K // tk
    dims = {"nn": NN, "nt": NT, "tn": TN}[mode]

    def body(*refs):
        if add is None:
            a_ref, b_ref, o_ref = refs[:3]
            c_ref = None
        else:
            a_ref, b_ref, c_ref, o_ref = refs[:4]

        def prod():
            return _dot(a_ref[...].astype(BF16), b_ref[...].astype(BF16), dims)

        def finish(r):
            if c_ref is not None:
                r = r + c_ref[...].astype(F32)
            o_ref[...] = r.astype(o_ref.dtype)

        if nk == 1:
            finish(prod())
            return
        acc = refs[-1]
        k = pl.program_id(2)

        @pl.when(k == 0)
        def _():
            acc[...] = prod()

        if nk > 2:
            @pl.when((k > 0) & (k < nk - 1))
            def _():
                acc[...] += prod()

        @pl.when(k == nk - 1)
        def _():
            finish(acc[...] + prod())

    a_spec = pl.BlockSpec((tk, tm), lambda i, j, k: (k, i)) if mode == "tn" else pl.BlockSpec((tm, tk), lambda i, j, k: (i, k))
    b_spec = pl.BlockSpec((tn, tk), lambda i, j, k: (j, k)) if mode == "nt" else pl.BlockSpec((tk, tn), lambda i, j, k: (k, j))
    o_spec = pl.BlockSpec((tm, tn), lambda i, j, k: (i, j))
    ins, specs = [a, b], [a_spec, b_spec]
    if add is not None:
        ins.append(add)
        specs.append(o_spec)
    return pl.pallas_call(
        body,
        grid=(M // tm, N // tn, nk),
        in_specs=specs,
        out_specs=o_spec,
        out_shape=jax.ShapeDtypeStruct((M, N), out_dtype),
        scratch_shapes=[pltpu.VMEM((tm, tn), F32)] if nk > 1 else [],
        compiler_params=_cp(("parallel", "parallel", "arbitrary")),
        name=name,
    )(*ins)


def rowwise(fn, tiled, full, out_tiled, out_acc, tile, name):
    tiled = [t if isinstance(t, tuple) else (t, t.shape[1], 0) for t in tiled]
    rows = tiled[0][0].shape[0]
    tile = min(tile, rows)
    assert rows % tile == 0
    nt, nf, no = len(tiled), len(full), len(out_tiled)

    def body(*refs):
        ins = [r[...] for r in refs[:nt + nf]]
        res = fn(*ins)
        if not isinstance(res, (tuple, list)):
            res = (res,)
        outs = refs[nt + nf:]
        for r, v in zip(outs[:no], res[:no]):
            r[...] = v.astype(r.dtype)
        if out_acc:
            @pl.when(pl.program_id(0) == 0)
            def _():
                for r in outs[no:]:
                    r[...] = jnp.zeros_like(r)

            for r, v in zip(outs[no:], res[no:]):
                r[...] += v

    in_specs = [pl.BlockSpec((tile, w), lambda i, cb=cb: (i, cb)) for (_, w, cb) in tiled]
    in_specs += [pl.BlockSpec(f.shape, lambda i, nd=f.ndim: (0,) * nd) for f in full]
    out_specs = [pl.BlockSpec((tile, w), lambda i: (i, 0)) for (w, _) in out_tiled]
    out_specs += [pl.BlockSpec(s, lambda i, nd=len(s): (0,) * nd) for s in out_acc]
    out_shape = [jax.ShapeDtypeStruct((rows, w), dt) for (w, dt) in out_tiled]
    out_shape += [jax.ShapeDtypeStruct(s, F32) for s in out_acc]
    res = pl.pallas_call(
        body,
        grid=(rows // tile,),
        in_specs=in_specs,
        out_specs=out_specs,
        out_shape=out_shape,
        compiler_params=_cp(("arbitrary",) if out_acc else ("parallel",)),
        name=name,
    )(*[t[0] for t in tiled], *full)
    return res


def adam_reduce(parts, w, m, v, name):
    P, R, C = parts.shape
    tr = _pick(R, (64, 32, 16, 8))
    c1 = 1.0 - ADAM_B1 ** ADAM_STEP
    c2 = 1.0 - ADAM_B2 ** ADAM_STEP

    def body(p_ref, w_ref, m_ref, v_ref, g_ref, d_ref, nm_ref, nv_ref):
        g = p_ref[0].astype(F32)
        for j in range(1, P):
            g = g + p_ref[j].astype(F32)
        mm = ADAM_B1 * m_ref[...] + (1.0 - ADAM_B1) * g
        vv = ADAM_B2 * v_ref[...] + (1.0 - ADAM_B2) * (g * g)
        m_hat = mm / c1
        v_hat = vv / c2
        g_ref[...] = g
        d_ref[...] = -ADAM_LR * (m_hat / (jnp.sqrt(v_hat) + ADAM_EPS) + ADAM_WD * w_ref[...])
        nm_ref[...] = mm
        nv_ref[...] = vv

    spec = pl.BlockSpec((tr, C), lambda i: (i, 0))
    return pl.pallas_call(
        body,
        grid=(R // tr,),
        in_specs=[pl.BlockSpec((P, tr, C), lambda i: (0, i, 0)), spec, spec, spec],
        out_specs=[spec] * 4,
        out_shape=[jax.ShapeDtypeStruct((R, C), F32)] * 4,
        compiler_params=_cp(("parallel",)),
        name=name,
    )(parts, w, m, v)


def reduce_parts(parts, name):
    P, R, C = parts.shape
    tr = _pick(R, (64, 32, 16, 8))

    def body(p_ref, g_ref):
        g = p_ref[0]
        for j in range(1, P):
            g = g + p_ref[j]
        g_ref[...] = g

    return pl.pallas_call(
        body,
        grid=(R // tr,),
        in_specs=[pl.BlockSpec((P, tr, C), lambda i: (0, i, 0))],
        out_specs=pl.BlockSpec((tr, C), lambda i: (i, 0)),
        out_shape=jax.ShapeDtypeStruct((R, C), F32),
        compiler_params=_cp(("parallel",)),
        name=name,
    )(parts)


def pack(arrs):
    flat = jnp.concatenate([a.reshape(-1).astype(F32) for a in arrs])
    n = flat.shape[0]
    padded = -(-n // 1024) * 1024
    return jnp.pad(flat, (0, padded - n)).reshape(padded // 128, 128)


def unpack(slab, shapes):
    flat = slab.reshape(-1)
    out, off = [], 0
    for s in shapes:
        size = int(np.prod(s))
        out.append(flat[off:off + size].reshape(s))
        off += size
    return out


def modulate_fwd(x, g, sh, sc, name):
    def fn(x, g, sh, sc):
        return x * _rstd(x) * g * (1.0 + sc) + sh

    return rowwise(fn, [x], [g, sh, sc], [(x.shape[1], BF16)], [], 256, name)[0]


def norm_rope_fwd(p, width, cb, w, cosf, sinf, hd, name):
    nh = width // hd

    def fn(x, cosf, sinf, w):
        outs = []
        for h in range(nh):
            xh = x[:, h * hd:(h + 1) * hd]
            y = xh * _rstd(xh) * w
            outs.append(y * cosf + pltpu.roll(y, hd // 2, 1) * sinf)
        return jnp.concatenate(outs, axis=1) if nh > 1 else outs[0]

    return rowwise(fn, [(p, width, cb), cosf, sinf], [w], [(width, BF16)], [], 256, name)[0]


def norm_rope_bwd(p, width, cb, d, w, cosf, sinf, hd, name):
    nh = width // hd

    def fn(x, d, cosf, sinf, w):
        outs = []
        dw = jnp.zeros((1, hd), F32)
        for h in range(nh):
            xh = x[:, h * hd:(h + 1) * hd]
            dh = d[:, h * hd:(h + 1) * hd].astype(F32)
            r = _rstd(xh)
            n = xh * r
            dy = dh * cosf + pltpu.roll(dh * sinf, hd // 2, 1)
            dw = dw + jnp.sum(dy * n, axis=0, keepdims=True)
            dn = dy * w
            outs.append(r * (dn - n * jnp.mean(dn * n, axis=-1, keepdims=True)))
        return (jnp.concatenate(outs, axis=1) if nh > 1 else outs[0]), dw

    return rowwise(fn, [(p, width, cb), d, cosf, sinf], [w], [(width, BF16)], [(1, hd)], 256, name)


def attention_fwd(qr, kr, pkv, kcr, pkv_c, sink, hkv, hd, name):
    T, L = qr.shape[0], kcr.shape[0]
    G = qr.shape[1] // (hkv * hd)
    nb = T // BLOCK
    scale = hd ** -0.5

    def body(q_ref, kp, kc, kn, vp, vc, vn, ck_ref, cv_ref, sink_ref, o_ref, lse_ref, lser_ref):
        i = pl.program_id(1)
        kwin = jnp.concatenate([kp[...], kc[...], kn[...]], axis=0)
        vwin = jnp.concatenate([vp[...], vc[...], vn[...]], axis=0).astype(BF16)
        ck, cv = ck_ref[...], cv_ref[...].astype(BF16)
        row = lax.broadcasted_iota(jnp.int32, (BLOCK, 3 * BLOCK), 0)
        col = lax.broadcasted_iota(jnp.int32, (BLOCK, 3 * BLOCK), 1)
        rel = col - BLOCK - row
        valid = (jnp.abs(rel) <= WINDOW) & ((col >= BLOCK) | (i > 0)) & ((col < 2 * BLOCK) | (i < nb - 1))
        R = range(G)
        qa = q_ref[...]
        qs = [qa[:, g * hd:(g + 1) * hd] for g in R]
        sks = [sink_ref[g] for g in R]
        ss = [jnp.where(valid, _dot(qs[g], kwin, NT) * scale, NEG) for g in R]
        scs = [_dot(qs[g], ck, NT) * scale for g in R]
        ms = [jnp.maximum(jnp.maximum(jnp.max(ss[g], axis=1, keepdims=True), jnp.max(scs[g], axis=1, keepdims=True)),
                          sks[g]) for g in R]
        ps = [jnp.exp(ss[g] - ms[g]) for g in R]
        pcs = [jnp.exp(scs[g] - ms[g]) for g in R]
        nums = [_dot(ps[g].astype(BF16), vwin, NN) + _dot(pcs[g].astype(BF16), cv, NN) for g in R]
        dens = [jnp.exp(sks[g] - ms[g]) + jnp.sum(ps[g], axis=1, keepdims=True) + jnp.sum(pcs[g], axis=1, keepdims=True)
                for g in R]
        o_ref[...] = jnp.concatenate([(nums[g] / dens[g]).astype(o_ref.dtype) for g in R], axis=1)
        eye = (lax.broadcasted_iota(jnp.int32, (BLOCK, BLOCK), 0)
               == lax.broadcasted_iota(jnp.int32, (BLOCK, BLOCK), 1)).astype(F32)
        for g in R:
            lg = ms[g] + jnp.log(dens[g])
            lse_ref[g] = lg
            lser_ref[g] = jnp.sum(lg * eye, axis=0, keepdims=True)

    kv_specs = [
        pl.BlockSpec((BLOCK, hd), lambda h, i: (jnp.maximum(i - 1, 0), h)),
        pl.BlockSpec((BLOCK, hd), lambda h, i: (i, h)),
        pl.BlockSpec((BLOCK, hd), lambda h, i: (jnp.minimum(i + 1, nb - 1), h)),
    ]
    v_specs = [
        pl.BlockSpec((BLOCK, hd), lambda h, i: (jnp.maximum(i - 1, 0), hkv + h)),
        pl.BlockSpec((BLOCK, hd), lambda h, i: (i, hkv + h)),
        pl.BlockSpec((BLOCK, hd), lambda h, i: (jnp.minimum(i + 1, nb - 1), hkv + h)),
    ]
    return pl.pallas_call(
        body,
        grid=(hkv, nb),
        in_specs=[pl.BlockSpec((BLOCK, G * hd), lambda h, i: (i, h))] + kv_specs + v_specs + [
            pl.BlockSpec((L, hd), lambda h, i: (0, h)),
            pl.BlockSpec((L, hd), lambda h, i: (0, hkv + h)),
            pl.BlockSpec((G, 1, 1), lambda h, i: (h, 0, 0)),
        ],
        out_specs=[
            pl.BlockSpec((BLOCK, G * hd), lambda h, i: (i, h)),
            pl.BlockSpec((G, BLOCK, 1), lambda h, i: (h, i, 0)),
            pl.BlockSpec((G, 1, BLOCK), lambda h, i: (h, 0, i)),
        ],
        out_shape=[jax.ShapeDtypeStruct(qr.shape, BF16), jax.ShapeDtypeStruct((hkv * G, T, 1), F32),
                   jax.ShapeDtypeStruct((hkv * G, 1, T), F32)],
        compiler_params=_cp(("parallel", "parallel")),
        name=name,
    )(qr, kr, kr, kr, pkv, pkv, pkv, kcr, pkv_c, sink)


def attention_bwd_q(qr, kr, pkv, kcr, pkv_c, sink, do, o, lse, hkv, hd, name):
    T, L = qr.shape[0], kcr.shape[0]
    G = qr.shape[1] // (hkv * hd)
    nb = T // BLOCK
    scale = hd ** -0.5

    def body(q_ref, kp, kc, kn, vp, vc, vn, ck_ref, cv_ref, sink_ref, do_ref, o_ref, lse_ref,
             dq_ref, dck_ref, dcv_ref, dsink_ref, drr_ref):
        i = pl.program_id(1)

        @pl.when(i == 0)
        def _():
            dck_ref[...] = jnp.zeros_like(dck_ref)
            dcv_ref[...] = jnp.zeros_like(dcv_ref)
            dsink_ref[...] = jnp.zeros_like(dsink_ref)

        kwin = jnp.concatenate([kp[...], kc[...], kn[...]], axis=0)
        vwin = jnp.concatenate([vp[...], vc[...], vn[...]], axis=0).astype(BF16)
        ck, cv = ck_ref[...], cv_ref[...].astype(BF16)
        row = lax.broadcasted_iota(jnp.int32, (G * BLOCK, 3 * BLOCK), 0) & (BLOCK - 1)
        col = lax.broadcasted_iota(jnp.int32, (G * BLOCK, 3 * BLOCK), 1)
        rel = col - BLOCK - row
        valid = (jnp.abs(rel) <= WINDOW) & ((col >= BLOCK) | (i > 0)) & ((col < 2 * BLOCK) | (i < nb - 1))

        def stack(ref):
            a = ref[...]
            return jnp.concatenate([a[:, g * hd:(g + 1) * hd] for g in range(G)], axis=0)

        q, dob, ob = stack(q_ref), stack(do_ref), stack(o_ref)
        lg = jnp.concatenate([lse_ref[g] for g in range(G)], axis=0)
        sk = jnp.concatenate([jnp.broadcast_to(sink_ref[g], (BLOCK, 1)) for g in range(G)], axis=0)
        s = jnp.where(valid, _dot(q, kwin, NT) * scale, NEG)
        sc = _dot(q, ck, NT) * scale
        p = jnp.exp(s - lg)
        pc = jnp.exp(sc - lg)
        dr = jnp.sum(dob.astype(F32) * ob.astype(F32), axis=1, keepdims=True)
        dp = _dot(dob, vwin, NT)
        dpc = _dot(dob, cv, NT)
        ds = (p * (dp - dr) * scale).astype(BF16)
        dsc = (pc * (dpc - dr) * scale).astype(BF16)
        dq = _dot(ds, kwin, NN) + _dot(dsc, ck, NN)
        dq_ref[...] = jnp.concatenate([dq[g * BLOCK:(g + 1) * BLOCK] for g in range(G)], axis=1)
        dck_ref[...] += _dot(dsc, q, TN)
        dcv_ref[...] += _dot(pc.astype(BF16), dob, TN)
        dsk = jnp.exp(sk - lg) * dr
        eye = (lax.broadcasted_iota(jnp.int32, (BLOCK, BLOCK), 0)
               == lax.broadcasted_iota(jnp.int32, (BLOCK, BLOCK), 1)).astype(F32)
        for g in range(G):
            dsink_ref[g] += -jnp.sum(dsk[g * BLOCK:(g + 1) * BLOCK], axis=0, keepdims=True)
            drr_ref[g] = jnp.sum(dr[g * BLOCK:(g + 1) * BLOCK] * eye, axis=0, keepdims=True)

    kv_specs = [
        pl.BlockSpec((BLOCK, hd), lambda h, i: (jnp.maximum(i - 1, 0), h)),
        pl.BlockSpec((BLOCK, hd), lambda h, i: (i, h)),
        pl.BlockSpec((BLOCK, hd), lambda h, i: (jnp.minimum(i + 1, nb - 1), h)),
    ]
    v_specs = [
        pl.BlockSpec((BLOCK, hd), lambda h, i: (jnp.maximum(i - 1, 0), hkv + h)),
        pl.BlockSpec((BLOCK, hd), lambda h, i: (i, hkv + h)),
        pl.BlockSpec((BLOCK, hd), lambda h, i: (jnp.minimum(i + 1, nb - 1), hkv + h)),
    ]
    qspec = pl.BlockSpec((BLOCK, G * hd), lambda h, i: (i, h))
    return pl.pallas_call(
        body,
        grid=(hkv, nb),
        in_specs=[qspec] + kv_specs + v_specs + [
            pl.BlockSpec((L, hd), lambda h, i: (0, h)),
            pl.BlockSpec((L, hd), lambda h, i: (0, hkv + h)),
            pl.BlockSpec((G, 1, 1), lambda h, i: (h, 0, 0)),
            qspec, qspec,
            pl.BlockSpec((G, BLOCK, 1), lambda h, i: (h, i, 0)),
        ],
        out_specs=[
            qspec,
            pl.BlockSpec((L, hd), lambda h, i: (0, h)),
            pl.BlockSpec((L, hd), lambda h, i: (0, h)),
            pl.BlockSpec((G, 1, 1), lambda h, i: (h, 0, 0)),
            pl.BlockSpec((G, 1, BLOCK), lambda h, i: (h, 0, i)),
        ],
        out_shape=[
            jax.ShapeDtypeStruct(qr.shape, F32),
            jax.ShapeDtypeStruct((L, hkv * hd), F32),
            jax.ShapeDtypeStruct((L, hkv * hd), F32),
            jax.ShapeDtypeStruct((hkv * G, 1, 1), F32),
            jax.ShapeDtypeStruct((hkv * G, 1, T), F32),
        ],
        compiler_params=_cp(("parallel", "arbitrary")),
        name=name,
    )(qr, kr, kr, kr, pkv, pkv, pkv, kcr, pkv_c, sink, do, o, lse)


def attention_bwd_kv(qr, kr, pkv, do, lse_row, dr_row, hkv, hd, name):
    T = qr.shape[0]
    G = qr.shape[1] // (hkv * hd)
    nb = T // BLOCK
    scale = hd ** -0.5

    def body(k_ref, v_ref, *refs):
        qs, dos, lses, drs = refs[0:3], refs[3:6], refs[6:9], refs[9:12]
        dk_ref, dv_ref = refs[12:]
        j = pl.program_id(1)
        k = k_ref[...]
        v = v_ref[...].astype(BF16)
        row = lax.broadcasted_iota(jnp.int32, (BLOCK, BLOCK), 0)
        col = lax.broadcasted_iota(jnp.int32, (BLOCK, BLOCK), 1)
        bias = []
        for d in range(3):
            iq = j + d - 1
            rel = row - col - (d - 1) * BLOCK
            valid = (jnp.abs(rel) <= WINDOW) & (iq >= 0) & (iq < nb)
            bias += [jnp.where(valid, 0.0, NEG)] * G
        bias = jnp.concatenate(bias, axis=1)

        def stack(refs):
            vals = [r[...] for r in refs]
            return jnp.concatenate([a[:, g * hd:(g + 1) * hd] for a in vals for g in range(G)], axis=0)

        q, dob = stack(qs), stack(dos)
        lrow = jnp.concatenate([r[g] for r in lses for g in range(G)], axis=1)
        drow = jnp.concatenate([r[g] for r in drs for g in range(G)], axis=1)
        st = _dot(k, q, NT) * scale + bias
        pt = jnp.exp(st - lrow)
        dpt = _dot(v, dob, NT)
        dst = (pt * (dpt - drow) * scale).astype(BF16)
        dk_ref[...] = _dot(dst, q, NN).astype(dk_ref.dtype)
        dv_ref[...] = _dot(pt.astype(BF16), dob, NN).astype(dv_ref.dtype)

    def q3(width_block):
        return [
            pl.BlockSpec(width_block, lambda h, j: (jnp.maximum(j - 1, 0), h)),
            pl.BlockSpec(width_block, lambda h, j: (j, h)),
            pl.BlockSpec(width_block, lambda h, j: (jnp.minimum(j + 1, nb - 1), h)),
        ]

    row3 = [
        pl.BlockSpec((G, 1, BLOCK), lambda h, j: (h, 0, jnp.maximum(j - 1, 0))),
        pl.BlockSpec((G, 1, BLOCK), lambda h, j: (h, 0, j)),
        pl.BlockSpec((G, 1, BLOCK), lambda h, j: (h, 0, jnp.minimum(j + 1, nb - 1))),
    ]
    qb = (BLOCK, G * hd)
    return pl.pallas_call(
        body,
        grid=(hkv, nb),
        in_specs=[pl.BlockSpec((BLOCK, hd), lambda h, j: (j, h)), pl.BlockSpec((BLOCK, hd), lambda h, j: (j, hkv + h))]
        + q3(qb) + q3(qb) + row3 + row3,
        out_specs=[pl.BlockSpec((BLOCK, hd), lambda h, j: (j, h))] * 2,
        out_shape=[jax.ShapeDtypeStruct((T, hkv * hd), BF16)] * 2,
        compiler_params=_cp(("parallel", "parallel")),
        name=name,
    )(kr, pkv, qr, qr, qr, do, do, do, lse_row, lse_row, lse_row, dr_row, dr_row, dr_row)


def gate_fwd(plr, wf, wb, bf, bb, name):
    n = wf.shape[1]

    def fn(lr, wf, wb, bf, bb):
        lrb = lr.astype(BF16)
        outs = []
        for w, b in ((wf, bf), (wb, bb)):
            z = _dot(lrb, w.astype(BF16), NN) + b
            outs.append((jnp.minimum(z, 0.0) - jnp.log(1.0 + jnp.exp(-jnp.abs(z)))) / GLA_GATE_NORM)
        return outs

    return rowwise(fn, [plr], [wf, wb, bf, bb], [(n, F32), (n, F32)], [], 256, name)


def gate_bwd(plr, dgf, dgb, wf, wb, bf, bb, name):
    n = wf.shape[1]

    def fn(lr, dgf, dgb, wf, wb, bf, bb):
        lrb = lr.astype(BF16)
        dlr = jnp.zeros(lr.shape, F32)
        res = []
        for w, b, dg in ((wf, bf, dgf), (wb, bb, dgb)):
            wb16 = w.astype(BF16)
            z = _dot(lrb, wb16, NN) + b
            dz = dg * _sig(-z) / GLA_GATE_NORM
            dzb = dz.astype(BF16)
            dlr = dlr + _dot(dzb, wb16, NT)
            res += [_dot(lrb, dzb, TN), jnp.sum(dz, axis=0, keepdims=True)]
        return [dlr] + res

    return rowwise(fn, [plr, dgf, dgb], [wf, wb, bf, bb], [(128, BF16)],
                   [(128, n), (1, n), (128, n), (1, n)], 256, name)


def _tri_dot(tri_b, x):
    x1 = x.astype(BF16)
    r1 = x - x1.astype(F32)
    x2 = r1.astype(BF16)
    x3 = (r1 - x2.astype(F32)).astype(BF16)
    return _dot(tri_b, x1, NN) + _dot(tri_b, x2, NN) + _dot(tri_b, x3, NN)


def gla_fwd(pqk, pv, gl, s0, heads, reverse, name, o_add=None):
    T = pqk.shape[0]
    dk = pqk.shape[1] // (2 * heads)
    dv = pv.shape[1] // heads
    C = GLA_CHUNK
    nc = T // C
    qscale = dk ** -0.5

    def body(*refs):
        if o_add is None:
            q_ref, k_ref, v_ref, g_ref, s0_ref, o_ref, st_ref, sf_ref, S = refs
            oa_ref = None
        else:
            q_ref, k_ref, v_ref, g_ref, s0_ref, oa_ref, o_ref, st_ref, sf_ref, S = refs
        n = pl.program_id(0)

        @pl.when(n == 0)
        def _():
            S[...] = s0_ref[...]

        r = lax.broadcasted_iota(jnp.int32, (C, C), 0)
        c = lax.broadcasted_iota(jnp.int32, (C, C), 1)
        tri = (r <= c) if reverse else (r >= c)
        trib = tri.astype(BF16)
        ga, qa, ka, va = g_ref[...], q_ref[...], k_ref[...], v_ref[...]
        sts = [S[h] for h in range(heads)]
        H = range(heads)
        gs = [ga[:, h * dk:(h + 1) * dk] for h in H]
        bs = [_tri_dot(trib, g) for g in gs]
        bls = [jnp.sum(g, axis=0, keepdims=True) for g in gs]
        vs = [va[:, h * dv:(h + 1) * dv].astype(BF16) for h in H]
        qes = [(qa[:, h * dk:(h + 1) * dk].astype(F32) * qscale * jnp.exp(bs[h])).astype(BF16) for h in H]
        kes = [(ka[:, h * dk:(h + 1) * dk].astype(F32) * jnp.exp(-bs[h])).astype(BF16) for h in H]
        kls = [(ka[:, h * dk:(h + 1) * dk].astype(F32) * jnp.exp(bls[h] - bs[h])).astype(BF16) for h in H]
        inter = [_dot(qes[h], sts[h].astype(BF16), NT) for h in H]
        upd = [_dot(vs[h], kls[h], TN) for h in H]
        As = [jnp.where(tri, _dot(qes[h], kes[h], NT), 0.0).astype(BF16) for h in H]
        outs = [inter[h] + _dot(As[h], vs[h], NN) for h in H]
        news = [sts[h] * jnp.exp(bls[h]) + upd[h] for h in H]
        o = jnp.concatenate(outs, axis=1)
        if oa_ref is not None:
            o = o + oa_ref[...]
        o_ref[...] = o
        for h in range(heads):
            st_ref[0, h] = sts[h]
            S[h] = news[h]

        @pl.when(n == nc - 1)
        def _():
            for h in range(heads):
                sf_ref[h] = news[h]

    def ci(n):
        return (nc - 1 - n) if reverse else n

    specs = [
        pl.BlockSpec((C, heads * dk), lambda n: (ci(n), 0)),
        pl.BlockSpec((C, heads * dk), lambda n: (ci(n), 1)),
        pl.BlockSpec((C, heads * dv), lambda n: (ci(n), 0)),
        pl.BlockSpec((C, heads * dk), lambda n: (ci(n), 0)),
        pl.BlockSpec((heads, dv, dk), lambda n: (0, 0, 0)),
    ]
    ins = [pqk, pqk, pv, gl, s0]
    if o_add is not None:
        specs.append(pl.BlockSpec((C, heads * dv), lambda n: (ci(n), 0)))
        ins.append(o_add)
    return pl.pallas_call(
        body,
        grid=(nc,),
        in_specs=specs,
        out_specs=[
            pl.BlockSpec((C, heads * dv), lambda n: (ci(n), 0)),
            pl.BlockSpec((1, heads, dv, dk), lambda n: (ci(n), 0, 0, 0)),
            pl.BlockSpec((heads, dv, dk), lambda n: (0, 0, 0)),
        ],
        out_shape=[
            jax.ShapeDtypeStruct((T, heads * dv), F32),
            jax.ShapeDtypeStruct((nc, heads, dv, dk), F32),
            jax.ShapeDtypeStruct((heads, dv, dk), F32),
        ],
        scratch_shapes=[pltpu.VMEM((heads, dv, dk), F32)],
        compiler_params=_cp(("arbitrary",)),
        name=name,
    )(*ins)


def gla_bwd(pqk, pv, gl, states, do, dsf, heads, reverse, name, acc=None):
    T = pqk.shape[0]
    dk = pqk.shape[1] // (2 * heads)
    dv = pv.shape[1] // heads
    C = GLA_CHUNK
    nc = T // C
    qscale = dk ** -0.5

    def body(*refs):
        if acc is None:
            q_ref, k_ref, v_ref, g_ref, st_ref, do_ref, dsf_ref, dq_ref, dk_ref, dv_ref, dg_ref, ds0_ref, dS = refs
            aq = ak = av = None
        else:
            (q_ref, k_ref, v_ref, g_ref, st_ref, do_ref, dsf_ref, aq, ak, av,
             dq_ref, dk_ref, dv_ref, dg_ref, ds0_ref, dS) = refs
        n = pl.program_id(0)

        @pl.when(n == 0)
        def _():
            dS[...] = dsf_ref[...]

        r = lax.broadcasted_iota(jnp.int32, (C, C), 0)
        c = lax.broadcasted_iota(jnp.int32, (C, C), 1)
        tri = (r <= c) if reverse else (r >= c)
        tri_t = (r >= c) if reverse else (r <= c)
        trib, tritb = tri.astype(BF16), tri_t.astype(BF16)
        ga, qa, ka, va, doa = g_ref[...], q_ref[...], k_ref[...], v_ref[...], do_ref[...]
        sts = [st_ref[0, h] for h in range(heads)]
        dsts = [dS[h] for h in range(heads)]
        H = range(heads)
        gs = [ga[:, h * dk:(h + 1) * dk] for h in H]
        bs = [_tri_dot(trib, g) for g in gs]
        bls = [jnp.sum(g, axis=0, keepdims=True) for g in gs]
        ebs = [jnp.exp(b) for b in bs]
        enbs = [jnp.exp(-b) for b in bs]
        elbs = [jnp.exp(bls[h] - bs[h]) for h in H]
        ebls = [jnp.exp(bl) for bl in bls]
        vbs = [va[:, h * dv:(h + 1) * dv].astype(BF16) for h in H]
        dobs = [doa[:, h * dv:(h + 1) * dv].astype(BF16) for h in H]
        qes = [qa[:, h * dk:(h + 1) * dk].astype(F32) * qscale * ebs[h] for h in H]
        kes = [ka[:, h * dk:(h + 1) * dk].astype(F32) * enbs[h] for h in H]
        kls = [ka[:, h * dk:(h + 1) * dk].astype(F32) * elbs[h] for h in H]
        qebs = [a.astype(BF16) for a in qes]
        kebs = [a.astype(BF16) for a in kes]
        klbs = [a.astype(BF16) for a in kls]
        stbs = [a.astype(BF16) for a in sts]
        dstbs = [a.astype(BF16) for a in dsts]
        ps = [jnp.where(tri, _dot(qebs[h], kebs[h], NT), 0.0).astype(BF16) for h in H]
        dps = [jnp.where(tri, _dot(dobs[h], vbs[h], NT), 0.0).astype(BF16) for h in H]
        dqe1 = [_dot(dobs[h], stbs[h], NN) for h in H]
        dkls = [_dot(vbs[h], dstbs[h], NN) for h in H]
        dv1 = [_dot(klbs[h], dstbs[h], NT) for h in H]
        dsn1 = [_dot(dobs[h], qebs[h], TN) for h in H]
        dqes = [dqe1[h] + _dot(dps[h], kebs[h], NN) for h in H]
        dkes = [_dot(dps[h], qebs[h], TN) for h in H]
        dvs = [_dot(ps[h], dobs[h], TN) + dv1[h] for h in H]
        dbls = [ebls[h] * jnp.sum(dsts[h] * sts[h], axis=0, keepdims=True)
                + jnp.sum(dkls[h] * kls[h], axis=0, keepdims=True) for h in H]
        dsns = [dsn1[h] + dsts[h] * ebls[h] for h in H]
        dqs = [dqes[h] * ebs[h] * qscale for h in H]
        dks = [dkes[h] * enbs[h] + dkls[h] * elbs[h] for h in H]
        dbs = [dqes[h] * qes[h] - dkes[h] * kes[h] - dkls[h] * kls[h] for h in H]
        dgs = [_tri_dot(tritb, dbs[h]) + dbls[h] for h in H]
        dq, dkk, dvv = (jnp.concatenate(a, axis=1) for a in (dqs, dks, dvs))
        if aq is not None:
            dq = dq + aq[...].astype(F32)
            dkk = dkk + ak[...].astype(F32)
            dvv = dvv + av[...].astype(F32)
        dq_ref[...] = dq.astype(dq_ref.dtype)
        dk_ref[...] = dkk.astype(dk_ref.dtype)
        dv_ref[...] = dvv.astype(dv_ref.dtype)
        dg_ref[...] = jnp.concatenate(dgs, axis=1)
        for h in range(heads):
            dS[h] = dsns[h]

        @pl.when(n == nc - 1)
        def _():
            for h in range(heads):
                ds0_ref[h] = dsns[h]

    def ci(n):
        return n if reverse else (nc - 1 - n)

    kspec = pl.BlockSpec((C, heads * dk), lambda n: (ci(n), 0))
    vspec = pl.BlockSpec((C, heads * dv), lambda n: (ci(n), 0))
    sspec = pl.BlockSpec((heads, dv, dk), lambda n: (0, 0, 0))
    specs = [
        kspec,
        pl.BlockSpec((C, heads * dk), lambda n: (ci(n), 1)),
        vspec,
        kspec,
        pl.BlockSpec((1, heads, dv, dk), lambda n: (ci(n), 0, 0, 0)),
        vspec,
        sspec,
    ]
    ins = [pqk, pqk, pv, gl, states, do, dsf]
    odt = F32 if acc is None else BF16
    if acc is not None:
        specs += [kspec, kspec, vspec]
        ins += list(acc)
    return pl.pallas_call(
        body,
        grid=(nc,),
        in_specs=specs,
        out_specs=[kspec, kspec, vspec, kspec, sspec],
        out_shape=[
            jax.ShapeDtypeStruct((T, heads * dk), odt),
            jax.ShapeDtypeStruct((T, heads * dk), odt),
            jax.ShapeDtypeStruct((T, heads * dv), odt),
            jax.ShapeDtypeStruct((T, heads * dk), F32),
            jax.ShapeDtypeStruct((heads, dv, dk), F32),
        ],
        scratch_shapes=[pltpu.VMEM((heads, dv, dk), F32)],
        compiler_params=_cp(("arbitrary",)),
        name=name,
    )(*ins)


def gla_out_fwd(og, prb, gn, heads, name):
    dv = og.shape[1] // heads

    def fn(og, rb, gn):
        outs = []
        for h in range(heads):
            oh = og[:, h * dv:(h + 1) * dv]
            outs.append(oh * _rstd(oh) * gn)
        y = jnp.concatenate(outs, axis=1)
        return y * (rb * _sig(rb))

    return rowwise(fn, [og, prb], [gn], [(og.shape[1], BF16)], [], 256, name)[0]


def gla_out_bwd(og, prb, du, gn, heads, name):
    dv = og.shape[1] // heads

    def fn(og, rb, du, gn):
        sg = _sig(rb)
        silu = rb * sg
        dsilu = sg * (1.0 + rb * (1.0 - sg))
        dog, ys = [], []
        dgn = jnp.zeros((1, dv), F32)
        for h in range(heads):
            sl = slice(h * dv, (h + 1) * dv)
            oh = og[:, sl]
            r = _rstd(oh)
            n = oh * r
            ys.append(n * gn)
            dy = du[:, sl] * silu[:, sl]
            dgn = dgn + jnp.sum(dy * n, axis=0, keepdims=True)
            dn = dy * gn
            dog.append(r * (dn - n * jnp.mean(dn * n, axis=-1, keepdims=True)))
        y = jnp.concatenate(ys, axis=1)
        return jnp.concatenate(dog, axis=1), du * y * dsilu, dgn

    return rowwise(fn, [og, prb, du], [gn], [(og.shape[1], F32), (og.shape[1], BF16)], [(1, dv)], 128, name)


def conv_specs(T, tt, tc, off, order):
    r8 = tt // 8
    last8 = T // 8 - 1
    if order == "ij":
        return [
            pl.BlockSpec((tt, tc), lambda i, j: (i, j + off)),
            pl.BlockSpec((8, tc), lambda i, j: (jnp.maximum(i * r8 - 1, 0), j + off)),
            pl.BlockSpec((8, tc), lambda i, j: (jnp.minimum((i + 1) * r8, last8), j + off)),
        ]
    return [
        pl.BlockSpec((tt, tc), lambda j, i: (i, j + off)),
        pl.BlockSpec((8, tc), lambda j, i: (jnp.maximum(i * r8 - 1, 0), j + off)),
        pl.BlockSpec((8, tc), lambda j, i: (jnp.minimum((i + 1) * r8, last8), j + off)),
    ]


def _shifted(u, hp, hn, i, nt_):
    tt = u.shape[0]
    row = lax.broadcasted_iota(jnp.int32, u.shape, 0)
    r8 = lax.broadcasted_iota(jnp.int32, hp.shape, 0)
    prev = jnp.sum(jnp.where(r8 == 7, hp, 0.0), axis=0, keepdims=True) * (i > 0).astype(F32)
    nxt = jnp.sum(jnp.where(r8 == 0, hn, 0.0), axis=0, keepdims=True) * (i < nt_ - 1).astype(F32)
    down = jnp.where(row == 0, prev, pltpu.roll(u, 1, 0))
    up = jnp.where(row == tt - 1, nxt, pltpu.roll(u, tt - 1, 0))
    return down, up


def conv_swiglu_fwd(u, cw, cb, name):
    T, F2 = u.shape
    F = F2 // 2
    tt = min(256, T)
    tc = _pick(F, (512, 256, 128))
    nt_, ncol = T // tt, F // tc

    def body(ua, uap, uan, ug, ugp, ugn, wa, wg, ba, bg, f_ref):
        i = pl.program_id(0)
        res = []
        for um, up_, un, w, b in ((ua, uap, uan, wa, ba), (ug, ugp, ugn, wg, bg)):
            x = um[...]
            down, up = _shifted(x, up_[...], un[...], i, nt_)
            res.append(w[0] * down + w[1] * x + w[2] * up + b[...])
        a, g = res
        f_ref[...] = (a * _sig(a) * g).astype(f_ref.dtype)

    wspec = lambda off: pl.BlockSpec((3, 1, tc), lambda i, j: (0, 0, j + off))
    bspec = lambda off: pl.BlockSpec((1, tc), lambda i, j: (0, j + off))
    return pl.pallas_call(
        body,
        grid=(nt_, ncol),
        in_specs=conv_specs(T, tt, tc, 0, "ij") + conv_specs(T, tt, tc, ncol, "ij")
        + [wspec(0), wspec(ncol), bspec(0), bspec(ncol)],
        out_specs=pl.BlockSpec((tt, tc), lambda i, j: (i, j)),
        out_shape=jax.ShapeDtypeStruct((T, F), BF16),
        compiler_params=_cp(("parallel", "parallel")),
        name=name,
    )(u, u, u, u, u, u, cw, cw, cb, cb)


def conv_swiglu_bwd(u, cw, cb, df, name):
    T, F2 = u.shape
    F = F2 // 2
    tt = min(256, T)
    tc = _pick(F, (512, 256, 128))
    nt_, ncol = T // tt, F // tc

    def body(ua, uap, uan, ug, ugp, ugn, wa, wg, ba, bg, df_ref, da_ref, dg_ref, dwa, dwg, dba, dbg):
        i = pl.program_id(1)

        @pl.when(i == 0)
        def _():
            for r in (dwa, dwg, dba, dbg):
                r[...] = jnp.zeros_like(r)

        sh = []
        res = []
        for um, up_, un, w, b in ((ua, uap, uan, wa, ba), (ug, ugp, ugn, wg, bg)):
            x = um[...]
            down, up = _shifted(x, up_[...], un[...], i, nt_)
            sh.append((down, x, up))
            res.append(w[0] * down + w[1] * x + w[2] * up + b[...])
        a, g = res
        d = df_ref[...].astype(F32)
        sg = _sig(a)
        da = d * g * sg * (1.0 + a * (1.0 - sg))
        dg = d * a * sg
        da_ref[...] = da
        dg_ref[...] = dg
        for dd, (down, x, up), dw, db in ((da, sh[0], dwa, dba), (dg, sh[1], dwg, dbg)):
            dw[0] += jnp.sum(dd * down, axis=0, keepdims=True)
            dw[1] += jnp.sum(dd * x, axis=0, keepdims=True)
            dw[2] += jnp.sum(dd * up, axis=0, keepdims=True)
            db[...] += jnp.sum(dd, axis=0, keepdims=True)

    wspec = lambda off: pl.BlockSpec((3, 1, tc), lambda j, i: (0, 0, j + off))
    bspec = lambda off: pl.BlockSpec((1, tc), lambda j, i: (0, j + off))
    tile = pl.BlockSpec((tt, tc), lambda j, i: (i, j))
    return pl.pallas_call(
        body,
        grid=(ncol, nt_),
        in_specs=conv_specs(T, tt, tc, 0, "ji") + conv_specs(T, tt, tc, ncol, "ji")
        + [wspec(0), wspec(ncol), bspec(0), bspec(ncol), tile],
        out_specs=[tile, tile, wspec(0), wspec(0), bspec(0), bspec(0)],
        out_shape=[
            jax.ShapeDtypeStruct((T, F), F32), jax.ShapeDtypeStruct((T, F), F32),
            jax.ShapeDtypeStruct((3, 1, F), F32), jax.ShapeDtypeStruct((3, 1, F), F32),
            jax.ShapeDtypeStruct((1, F), F32), jax.ShapeDtypeStruct((1, F), F32),
        ],
        compiler_params=_cp(("parallel", "arbitrary")),
        name=name,
    )(u, u, u, u, u, u, cw, cw, cb, cb, df)


def conv_transpose(d, cw, off, name):
    T, F = d.shape
    tt = min(256, T)
    tc = _pick(F, (512, 256, 128))
    nt_, ncol = T // tt, F // tc
    offb = off // tc

    def body(dm, dp_, dn, w, o_ref):
        i = pl.program_id(0)
        x = dm[...]
        down, up = _shifted(x, dp_[...], dn[...], i, nt_)
        o_ref[...] = (w[0] * up + w[1] * x + w[2] * down).astype(o_ref.dtype)

    return pl.pallas_call(
        body,
        grid=(nt_, ncol),
        in_specs=conv_specs(T, tt, tc, 0, "ij") + [pl.BlockSpec((3, 1, tc), lambda i, j: (0, 0, j + offb))],
        out_specs=pl.BlockSpec((tt, tc), lambda i, j: (i, j)),
        out_shape=jax.ShapeDtypeStruct((T, F), BF16),
        compiler_params=_cp(("parallel", "parallel")),
        name=name,
    )(d, d, d, cw)


def rope_tables(n, hd):
    rows = n // GRID_W
    row = jnp.repeat(jnp.arange(rows), GRID_W)
    col = jnp.tile(jnp.arange(GRID_W), rows)
    n_freq = hd // 4
    inv = ROPE_THETA ** (-jnp.arange(n_freq, dtype=F32) / n_freq)
    ang = jnp.concatenate([row[:, None] * inv, col[:, None] * inv], axis=-1)
    cos, sin = jnp.cos(ang), jnp.sin(ang)
    return jnp.concatenate([cos, cos], axis=-1), jnp.concatenate([-sin, sin], axis=-1)


def local_step(x, ctx, tgt, mod, modc, W, P):
    T, D = x.shape
    L = ctx.shape[0]
    hd, hq, hkv, gh = P["hd"], P["hq"], P["hkv"], P["gh"]
    sh1, sc1, g1, sh2, sc2, g2 = mod
    csh1, csc1 = modc
    kvw = hkv * hd
    gkw = W["gqk"].shape[1] // 2
    gdv = D // gh
    gdk = gkw // gh

    h = modulate_fwd(x, P["g_mix"], sh1, sc1, "mod1")
    hc = modulate_fwd(ctx, P["g_mix"], csh1, csc1, "mod1_ctx")
    pq = matmul(h, W["q"], "nn", F32, "proj_q")
    pkv = matmul(h, W["kv"], "nn", F32, "proj_kv")
    pgqk = matmul(h, W["gqk"], "nn", F32, "proj_gqk")
    pgv = matmul(h, W["gv"], "nn", F32, "proj_gv")
    prb = matmul(h, W["rb"], "nn", F32, "proj_rb")
    plr = matmul(h, W["lr"], "nn", F32, "proj_lr")
    pgab = matmul(h, W["gab"], "nn", F32, "proj_gab")
    pkv_c = matmul(hc, W["kv"], "nn", F32, "proj_kv_ctx")
    pgqk_c = matmul(hc, W["gqk"], "nn", F32, "proj_gqk_ctx")
    pgv_c = matmul(hc, W["gv"], "nn", F32, "proj_gv_ctx")
    plr_c = matmul(hc, W["lr"], "nn", F32, "proj_lr_ctx")

    cosf, sinf = rope_tables(T, hd)
    one_c, zero_c = jnp.ones((L, hd), F32), jnp.zeros((L, hd), F32)
    qr = norm_rope_fwd(pq, hq * hd, 0, P["q_norm"], cosf, sinf, hd, "qnorm")
    kr = norm_rope_fwd(pkv, kvw, 0, P["k_norm"], cosf, sinf, hd, "knorm")
    kcr = norm_rope_fwd(pkv_c, kvw, 0, P["k_norm"], one_c, zero_c, hd, "knorm_ctx")
    sink = P["attn_sink"].reshape(hq, 1, 1)
    o_attn, lse, lse_row = attention_fwd(qr, kr, pkv, kcr, pkv_c, sink, hkv, hd, "attn_fwd")

    gf, gb = gate_fwd(plr, W["gate_f"], W["gate_b"], P["b_gate_f"], P["b_gate_b"], "gates")
    gfc, gbc = gate_fwd(plr_c, W["gate_f"], W["gate_b"], P["b_gate_f"], P["b_gate_b"], "gates_ctx")
    zero_state = jnp.zeros((gh, gdv, gdk), F32)
    _, st_cf, s_cf = gla_fwd(pgqk_c, pgv_c, gfc, zero_state, gh, False, "gla_ctx_f")
    _, st_cb, s_cb = gla_fwd(pgqk_c, pgv_c, gbc, zero_state, gh, True, "gla_ctx_b")
    of, st_f, _ = gla_fwd(pgqk, pgv, gf, s_cf, gh, False, "gla_f")
    og, st_b, _ = gla_fwd(pgqk, pgv, gb, s_cb, gh, True, "gla_b", o_add=of)
    ug = gla_out_fwd(og, prb, P["gla_norm"], gh, "gla_out")

    ya = matmul(o_attn, W["attn_o"], "nn", F32, "attn_o")
    yg = matmul(ug, W["gla_o"], "nn", F32, "gla_o")

    def merge_fn(ya, yg, ga, gb_):
        return _sig(ga) * ya + _sig(gb_) * yg

    z = rowwise(merge_fn, [ya, yg, (pgab, D, 0), (pgab, D, 1)], [], [(D, BF16)], [], 256, "merge")[0]
    mo = matmul(z, W["out"], "nn", F32, "w_out")

    def res_fn(x, mo, g1, gffn, sh2, sc2):
        x1 = x + g1 * mo
        return x1, x1 * _rstd(x1) * gffn * (1.0 + sc2) + sh2

    x1, h2 = rowwise(res_fn, [x, mo], [g1, P["g_ffn"], sh2, sc2], [(D, F32), (D, BF16)], [], 256, "res_mod2")
    u = matmul(h2, W["up"], "nn", F32, "w_up")
    cw3 = W["conv_w"].reshape(3, 1, -1)
    f = conv_swiglu_fwd(u, cw3, P["conv_b"], "conv_swiglu")
    fo = matmul(f, W["down"], "nn", F32, "w_down")

    def final_fn(x1, fo, tgt, g2):
        e = x1 + g2 * fo - tgt
        dy = e * (1.0 / D)
        lsum = jnp.sum(jnp.sum(e * e, axis=1, keepdims=True), axis=0, keepdims=True)
        return dy, dy * g2, jnp.broadcast_to(lsum, (1, 128)), jnp.sum(dy * fo, axis=0, keepdims=True)

    dy, dfo, lsum, dg2 = rowwise(final_fn, [x1, fo, tgt], [g2], [(D, F32), (D, BF16)], [(1, 128), (1, D)], 256, "loss")
    df = matmul(dfo, W["down"], "nt", BF16, "d_f")
    dw_down = matmul(f, dfo, "tn", BF16, "dw_down")
    da, dgg, dcw_a, dcw_g, dcb_a, dcb_g = conv_swiglu_bwd(u, cw3, P["conv_b"], df, "conv_swiglu_bwd")
    Fh = da.shape[1]
    du_a = conv_transpose(da, cw3, 0, "conv_t_a")
    du_g = conv_transpose(dgg, cw3, Fh, "conv_t_g")
    dh2 = matmul(du_a, W["up"][:, :Fh], "nt", F32, "d_h2_a")
    dh2 = matmul(du_g, W["up"][:, Fh:], "nt", F32, "d_h2_g", add=dh2)
    dw_up = jnp.concatenate([matmul(h2, du_a, "tn", BF16, "dw_up_a"), matmul(h2, du_g, "tn", BF16, "dw_up_g")], axis=1)

    def mod2_bwd_fn(x1, dh, dy, mo, gffn, sc2, g1):
        r = _rstd(x1)
        n = x1 * r
        dyy = dh * (1.0 + sc2)
        dn = dyy * gffn
        dx1 = dy + r * (dn - n * jnp.mean(dn * n, axis=-1, keepdims=True))
        s0 = lambda a: jnp.sum(a, axis=0, keepdims=True)
        return dx1, dx1 * g1, s0(dyy * n), s0(dh), s0(dh * n * gffn), s0(dx1 * mo)

    dx1, dmo, dg_ffn, dsh2, dsc2, dg1 = rowwise(
        mod2_bwd_fn, [x1, dh2, dy, mo], [P["g_ffn"], sc2, g1], [(D, F32), (D, BF16)], [(1, D)] * 4, 128, "mod2_bwd")
    dz = matmul(dmo, W["out"], "nt", F32, "d_z")
    dw_out = matmul(z, dmo, "tn", BF16, "dw_out")

    def merge_bwd_fn(dz, ya, yg, ga, gb_):
        sa, sb = _sig(ga), _sig(gb_)
        return dz * sa, dz * sb, jnp.concatenate([dz * ya * sa * (1.0 - sa), dz * yg * sb * (1.0 - sb)], axis=1)

    dya, dyg, dpgab = rowwise(merge_bwd_fn, [dz, ya, yg, (pgab, D, 0), (pgab, D, 1)], [],
                              [(D, BF16), (D, BF16), (2 * D, BF16)], [], 128, "merge_bwd")
    do_attn = matmul(dya, W["attn_o"], "nt", BF16, "d_oattn")
    dw_attn_o = matmul(o_attn, dya, "tn", BF16, "dw_attn_o")
    dug = matmul(dyg, W["gla_o"], "nt", F32, "d_ug")
    dw_gla_o = matmul(ug, dyg, "tn", BF16, "dw_gla_o")
    dog, dprb, dgn = gla_out_bwd(og, prb, dug, P["gla_norm"], gh, "gla_out_bwd")

    dq1, dk1, dv1, dgf, ds_cf = gla_bwd(pgqk, pgv, gf, st_f, dog, zero_state, gh, False, "gla_f_bwd")
    dgq, dgk, dpgv, dgb, ds_cb = gla_bwd(pgqk, pgv, gb, st_b, dog, zero_state, gh, True, "gla_b_bwd",
                                          acc=(dq1, dk1, dv1))
    dpgqk = jnp.concatenate([dgq, dgk], axis=1)
    zero_do = jnp.zeros((L, gh * gdv), F32)
    cq1, ck1, cv1, dgfc, _ = gla_bwd(pgqk_c, pgv_c, gfc, st_cf, zero_do, ds_cf, gh, False, "gla_ctx_f_bwd")
    cq, ck, dpgv_c, dgbc, _ = gla_bwd(pgqk_c, pgv_c, gbc, st_cb, zero_do, ds_cb, gh, True, "gla_ctx_b_bwd",
                                      acc=(cq1, ck1, cv1))
    dpgqk_c = jnp.concatenate([cq, ck], axis=1)
    dplr, dwgf, dbgf, dwgb, dbgb = gate_bwd(plr, dgf, dgb, W["gate_f"], W["gate_b"], P["b_gate_f"], P["b_gate_b"], "gates_bwd")
    dplr_c, dwgf_c, dbgf_c, dwgb_c, dbgb_c = gate_bwd(plr_c, dgfc, dgbc, W["gate_f"], W["gate_b"], P["b_gate_f"],
                                                      P["b_gate_b"], "gates_ctx_bwd")

    dqr, dkc_r, dvc, dsink, dr_row = attention_bwd_q(qr, kr, pkv, kcr, pkv_c, sink, do_attn, o_attn, lse, hkv, hd,
                                                     "attn_bwd_q")
    dkr, dv = attention_bwd_kv(qr, kr, pkv, do_attn, lse_row, dr_row, hkv, hd, "attn_bwd_kv")
    dpq, dqn = norm_rope_bwd(pq, hq * hd, 0, dqr, P["q_norm"], cosf, sinf, hd, "qnorm_bwd")
    dpk, dkn = norm_rope_bwd(pkv, kvw, 0, dkr, P["k_norm"], cosf, sinf, hd, "knorm_bwd")
    dpk_c, dkn_c = norm_rope_bwd(pkv_c, kvw, 0, dkc_r, P["k_norm"], one_c, zero_c, hd, "knorm_ctx_bwd")
    dpkv = jnp.concatenate([dpk, dv], axis=1)
    dpkv_c = jnp.concatenate([dpk_c, dvc.astype(BF16)], axis=1)

    dw_q = matmul(h, dpq, "tn", BF16, "dw_q")
    dw_kv = matmul(h, dpkv, "tn", BF16, "dw_kv", add=matmul(hc, dpkv_c, "tn", F32, "dw_kv_ctx"))
    dw_gqk = matmul(h, dpgqk, "tn", BF16, "dw_gqk", add=matmul(hc, dpgqk_c, "tn", F32, "dw_gqk_ctx"))
    dw_gv = matmul(h, dpgv, "tn", BF16, "dw_gv", add=matmul(hc, dpgv_c, "tn", F32, "dw_gv_ctx"))
    dw_rb = matmul(h, dprb, "tn", BF16, "dw_rb")
    dw_lr = matmul(h, dplr, "tn", BF16, "dw_lr", add=matmul(hc, dplr_c, "tn", F32, "dw_lr_ctx"))
    dw_gab = matmul(h, dpgab, "tn", BF16, "dw_gab")
    lrw = P["lowrank"]
    dw_in = jnp.concatenate([dw_q, dw_kv, dw_gqk, dw_gv, dw_rb, dw_lr[:, :2 * lrw], dw_gab], axis=1)

    dh = matmul(dpq, W["q"], "nt", F32, "dh_q")
    dh = matmul(dpkv, W["kv"], "nt", F32, "dh_kv", add=dh)
    dh = matmul(dpgqk, W["gqk"], "nt", F32, "dh_gqk", add=dh)
    dh = matmul(dpgv, W["gv"], "nt", F32, "dh_gv", add=dh)
    dh = matmul(dprb, W["rb"], "nt", F32, "dh_rb", add=dh)
    dh = matmul(dplr, W["lr"], "nt", F32, "dh_lr", add=dh)
    dh = matmul(dpgab, W["gab"], "nt", F32, "dh_gab", add=dh)
    dhc = matmul(dpkv_c, W["kv"], "nt", F32, "dhc_kv")
    dhc = matmul(dpgqk_c, W["gqk"], "nt", F32, "dhc_gqk", add=dhc)
    dhc = matmul(dpgv_c, W["gv"], "nt", F32, "dhc_gv", add=dhc)
    dhc = matmul(dplr_c, W["lr"], "nt", F32, "dhc_lr", add=dhc)

    def mod1_bwd_fn(x, dh, dres, g, sc):
        r = _rstd(x)
        n = x * r
        dyy = dh * (1.0 + sc)
        dn = dyy * g
        dx = dres + r * (dn - n * jnp.mean(dn * n, axis=-1, keepdims=True))
        s0 = lambda a: jnp.sum(a, axis=0, keepdims=True)
        return dx, s0(dyy * n), s0(dh), s0(dh * n * g)

    grad_x, dgmix, dsh1, dsc1 = rowwise(mod1_bwd_fn, [x, dh, dx1], [P["g_mix"], sc1], [(D, F32)], [(1, D)] * 3,
                                        128, "mod1_bwd")
    _, dgmix_c, dcsh1, dcsc1 = rowwise(mod1_bwd_fn, [ctx, dhc, jnp.zeros_like(ctx)], [P["g_mix"], csc1], [(D, F32)],
                                       [(1, D)] * 3, 128, "mod1_ctx_bwd")

    zD = jnp.zeros((1, D), F32)
    grads = dict(
        w_in=dw_in, w_attn_o=dw_attn_o, w_gla_o=dw_gla_o, w_out=dw_out, w_up=dw_up, w_down=dw_down,
        dmod_x=jnp.concatenate([dsh1, dsc1, dg1, dsh2, dsc2, dg2], axis=1),
        dmod_c=jnp.concatenate([dcsh1, dcsc1, zD, zD, zD, zD], axis=1),
        g_mix=dgmix + dgmix_c, q_norm=dqn, k_norm=dkn + dkn_c, attn_sink=dsink.reshape(1, hq),
        w_gate_f=(dwgf + dwgf_c)[:lrw], b_gate_f=dbgf + dbgf_c,
        w_gate_b=(dwgb + dwgb_c)[lrw:2 * lrw], b_gate_b=dbgb + dbgb_c,
        gla_norm=dgn, g_ffn=dg_ffn,
        conv_w=jnp.concatenate([dcw_a, dcw_g], axis=2).reshape(3, -1),
        conv_b=jnp.concatenate([dcb_a, dcb_g], axis=1),
    )
    return lsum[0, 0], grad_x, grads


SMALL_REPL = ("c_ctx", "b_mod", "g_mix", "q_norm", "k_norm", "attn_sink", "b_gate_f", "b_gate_b", "gla_norm", "g_ffn",
              "conv_b")
SMALL_SHARD = ("w_gate_f", "w_gate_b", "conv_w")
ORDER = ("c_ctx", "w_mod", "b_mod", "g_mix", "w_in", "q_norm", "k_norm", "attn_sink", "w_gate_f", "b_gate_f",
         "w_gate_b", "b_gate_b", "gla_norm", "w_attn_o", "w_gla_o", "w_out", "g_ffn", "w_up", "conv_w", "conv_b",
         "w_down")


def kernel(x, c, ctx, c_ctx, w_mod, b_mod, g_mix, w_in, q_norm, k_norm, attn_sink, w_gate_f, b_gate_f, w_gate_b, b_gate_b, gla_norm, w_attn_o, w_gla_o, w_out, g_ffn, w_up, conv_w, conv_b, w_down, loss_target, m_c_ctx, m_w_mod, m_b_mod, m_g_mix, m_w_in, m_q_norm, m_k_norm, m_attn_sink, m_w_gate_f, m_b_gate_f, m_w_gate_b, m_b_gate_b, m_gla_norm, m_w_attn_o, m_w_gla_o, m_w_out, m_g_ffn, m_w_up, m_conv_w, m_conv_b, m_w_down, v_c_ctx, v_w_mod, v_b_mod, v_g_mix, v_w_in, v_q_norm, v_k_norm, v_attn_sink, v_w_gate_f, v_b_gate_f, v_w_gate_b, v_b_gate_b, v_gla_norm, v_w_attn_o, v_w_gla_o, v_w_out, v_g_ffn, v_w_up, v_conv_w, v_conv_b, v_w_down):
    loc = dict(locals())
    Wt = {n: loc[n] for n in ORDER}
    Mt = {n: loc["m_" + n] for n in ORDER}
    Vt = {n: loc["v_" + n] for n in ORDER}
    me = 4 * lax.axis_index("x") + 2 * lax.axis_index("y") + lax.axis_index("c")

    D = x.shape[-1]
    hd = q_norm.shape[-1]
    hq = attn_sink.shape[-1]
    gdv = gla_norm.shape[-1]
    gh = D // gdv
    gdk = D // 2 // gh
    lrw = w_gate_f.shape[1]
    in_w = NDEV * w_in.shape[-1]
    kvw = (in_w - hq * hd - 2 * gh * gdk - 2 * gh * gdv - 2 * lrw - 2 * D) // 2
    hkv = kvw // hd
    gcols = w_gate_f.shape[-1]
    F2 = NDEV * w_up.shape[-1]
    mcols = w_mod.shape[-1]

    x2, ctx2, tgt2 = x[0], ctx[0], loss_target[0]

    c_all = exchange([jnp.pad(c, ((0, 7), (0, 0)))], True, "gather_c")[0][:, 0, :]
    c9 = jnp.concatenate([c_all, c_ctx[None, :], jnp.zeros((7, D), F32)], axis=0)
    s9 = rowwise(lambda a: a * _sig(a), [c9], [], [(D, F32)], [], 16, "silu_c")[0]
    bias = jnp.broadcast_to(lax.dynamic_slice_in_dim(b_mod, me * mcols, mcols, axis=1), (16, mcols))
    mod_cols = matmul(s9, w_mod[0], "nn", F32, "mod_cols", add=bias)
    mod_all = exchange([mod_cols], True, "gather_mod")[0]
    mod_all = jnp.transpose(mod_all, (1, 0, 2)).reshape(16, NDEV * mcols)
    mod_me = lax.dynamic_slice_in_dim(mod_all, me, 1, axis=0)
    mod = [mod_me[:, i * D:(i + 1) * D] for i in range(6)]
    modc = [mod_all[8:9, i * D:(i + 1) * D] for i in range(2)]

    o3 = jnp.stack([w_attn_o[0], w_gla_o[0], w_out[0]]).astype(BF16)
    small_w = pack([w_gate_f[0], w_gate_b[0], conv_w[0]])
    g_in, g_o3, g_up, g_down, g_small = gather_two_level(
        [w_in[0].astype(BF16), o3, w_up[0].astype(BF16), w_down[0].astype(BF16), small_w], "gather_w")
    win = jnp.transpose(g_in, (1, 0, 2)).reshape(D, in_w)
    offs = np.cumsum([0, hq * hd, 2 * kvw, 2 * gh * gdk, gh * gdv, gh * gdv, 2 * lrw, 2 * D]).tolist()
    seg = [win[:, offs[i]:offs[i + 1]] for i in range(7)]
    small_parts = [unpack(g_small[j], [w_gate_f[0].shape, w_gate_b[0].shape, conv_w[0].shape]) for j in range(NDEV)]
    wgf = jnp.concatenate([p[0] for p in small_parts], axis=1)
    wgb = jnp.concatenate([p[1] for p in small_parts], axis=1)
    cw_full = jnp.concatenate([p[2] for p in small_parts], axis=1)
    o3f = jnp.transpose(g_o3, (1, 0, 2, 3)).reshape(3, -1, D)
    W = dict(
        q=seg[0], kv=seg[1], gqk=seg[2], gv=seg[3], rb=seg[4],
        lr=jnp.pad(seg[5], ((0, 0), (0, 128 - 2 * lrw))), gab=seg[6],
        gate_f=jnp.pad(wgf, ((0, 128 - lrw), (0, 0))),
        gate_b=jnp.pad(wgb, ((lrw, 128 - 2 * lrw), (0, 0))),
        attn_o=o3f[0], gla_o=o3f[1], out=o3f[2],
        up=jnp.transpose(g_up, (1, 0, 2)).reshape(D, F2),
        down=g_down.reshape(-1, D),
        conv_w=cw_full,
    )
    P = dict(hd=hd, hq=hq, hkv=hkv, gh=gh, lowrank=lrw, g_mix=g_mix, q_norm=q_norm, k_norm=k_norm, attn_sink=attn_sink,
             b_gate_f=b_gate_f, b_gate_b=b_gate_b, gla_norm=gla_norm, g_ffn=g_ffn, conv_b=conv_b)

    lsum, grad_x, G = local_step(x2, ctx2, tgt2, mod, modc, W, P)
    loss = lax.psum(0.5 * lsum / D, ("x", "y", "c"))

    dm = exchange([jnp.concatenate([G["dmod_x"], G["dmod_c"], jnp.zeros((6, 6 * D), F32)], axis=0)], True,
                  "gather_dmod")[0]
    dmc = reduce_parts(dm[:, 1:2, :].reshape(NDEV, 6 * D // 128, 128), "sum_dmod_ctx").reshape(1, 6 * D)
    dM = jnp.concatenate([dm[:, 0, :], dmc, jnp.zeros((7, 6 * D), F32)], axis=0)
    dM_cols = lax.dynamic_slice_in_dim(dM, me * mcols, mcols, axis=1)
    g_w_mod = matmul(s9, dM_cols, "tn", F32, "dw_mod")
    g_b_mod = reduce_parts(dM.reshape(16, 6 * D // 128, 128), "sum_db_mod").reshape(1, 6 * D)
    dsc = matmul(dM_cols[8:16], w_mod[0], "nt", F32, "d_silu_ctx")
    cc = jnp.broadcast_to(c_ctx[None, :], (8, D))

    def dsilu_fn(d, a):
        sg = _sig(a)
        return d * sg * (1.0 + a * (1.0 - sg))

    g_cctx_part = rowwise(dsilu_fn, [dsc, cc], [], [(D, F32)], [], 8, "d_c_ctx")[0][0:1]

    small_names = ("c_ctx", "g_mix", "q_norm", "k_norm", "attn_sink", "b_gate_f", "b_gate_b", "gla_norm", "g_ffn",
                   "conv_b", "w_gate_f", "w_gate_b", "conv_w")
    G["c_ctx"] = g_cctx_part
    sm_shapes = [G[n].shape for n in small_names]
    sm_all = exchange([pack([G[n] for n in small_names])], True, "gather_small_grads")[0]
    sm_tot = unpack(reduce_parts(sm_all, "sum_small_grads"), sm_shapes)
    gs = dict(zip(small_names, sm_tot))
    gs["b_mod"] = g_b_mod
    gs["w_gate_f"] = lax.dynamic_slice_in_dim(gs["w_gate_f"], me * gcols, gcols, axis=1)
    gs["w_gate_b"] = lax.dynamic_slice_in_dim(gs["w_gate_b"], me * gcols, gcols, axis=1)
    ccols = conv_w.shape[-1]
    gs["conv_w"] = lax.dynamic_slice_in_dim(gs["conv_w"], me * ccols, ccols, axis=1)

    wcols = w_in.shape[-1]
    ucols = w_up.shape[-1]
    drows = w_down.shape[1]
    orows = w_attn_o.shape[1]
    s_in = jnp.transpose(G["w_in"].reshape(D, NDEV, wcols), (1, 0, 2))
    s_o3 = jnp.stack([G["w_attn_o"].reshape(NDEV, orows, D), G["w_gla_o"].reshape(NDEV, orows, D),
                      G["w_out"].reshape(NDEV, orows, D)], axis=1).reshape(NDEV, 3 * orows, D)
    s_up = jnp.transpose(G["w_up"].reshape(D, NDEV, ucols), (1, 0, 2))
    s_down = G["w_down"].reshape(NDEV, drows, D)
    r_in, r_o3, r_up, r_down = scatter_reduce([s_in, s_o3, s_up, s_down], "scatter_grads")

    out = {}
    out["w_in"] = adam_reduce(r_in, w_in[0], m_w_in[0], v_w_in[0], "adam_w_in")
    o3w = jnp.concatenate([w_attn_o[0], w_gla_o[0], w_out[0]], axis=0)
    o3m = jnp.concatenate([m_w_attn_o[0], m_w_gla_o[0], m_w_out[0]], axis=0)
    o3v = jnp.concatenate([v_w_attn_o[0], v_w_gla_o[0], v_w_out[0]], axis=0)
    ro3 = adam_reduce(r_o3, o3w, o3m, o3v, "adam_o3")
    for i, n in enumerate(("w_attn_o", "w_gla_o", "w_out")):
        out[n] = [a[i * orows:(i + 1) * orows] for a in ro3]
    out["w_up"] = adam_reduce(r_up, w_up[0], m_w_up[0], v_w_up[0], "adam_w_up")
    out["w_down"] = adam_reduce(r_down, w_down[0], m_w_down[0], v_w_down[0], "adam_w_down")
    out["w_mod"] = adam_reduce(g_w_mod[None], w_mod[0], m_w_mod[0], v_w_mod[0], "adam_w_mod")
    sm_names = SMALL_REPL + SMALL_SHARD
    shapes = [Wt[n].shape for n in sm_names]
    rs = adam_reduce(pack([gs[n] for n in sm_names])[None], pack([Wt[n] for n in sm_names]),
                     pack([Mt[n] for n in sm_names]), pack([Vt[n] for n in sm_names]), "adam_small")
    rs = [unpack(a, shapes) for a in rs]
    for i, n in enumerate(sm_names):
        out[n] = [a[i] for a in rs]

    res = [loss, grad_x[None]]
    for k in range(4):
        for n in ORDER:
            res.append(out[n][k].reshape(Wt[n].shape))
    return tuple(res)
```

```python
import jax
import jax.numpy as jnp
import numpy as np
from jax import lax
from jax.experimental import pallas as pl
from jax.experimental.pallas import tpu as pltpu

F32 = jnp.float32
BF16 = jnp.bfloat16

NDEV = 8
NCHIP = 4
EPS = 1e-6
WINDOW = 128
BLOCK = 128
GRID_W = 64
ROPE_THETA = 10000.0
GLA_CHUNK = 64
GLA_GATE_NORM = 16.0
ADAM_LR = 0.001
ADAM_B1 = 0.9
ADAM_B2 = 0.999
ADAM_EPS = 1e-08
ADAM_WD = 0.01
ADAM_STEP = 10
V7X_VMEM_LIMIT = 56 * 1024 * 1024
MATMUL_VMEM_BUDGET = 40 * 1024 * 1024
NEG = -1e30

NN = ((1,), (0,))
NT = ((1,), (1,))
TN = ((0,), (0,))


def _dot(a, b, dims):
    return lax.dot_general(a, b, (dims, ((), ())), preferred_element_type=F32)


def _cp(sem):
    return pltpu.CompilerParams(dimension_semantics=sem, vmem_limit_bytes=V7X_VMEM_LIMIT)


def _pick(n, cands):
    for c in cands:
        if n % c == 0:
            return c
    return n


def _sig(x):
    return 1.0 / (1.0 + jnp.exp(-x))


def _rstd(x):
    return lax.rsqrt(jnp.mean(x * x, axis=-1, keepdims=True) + EPS)


_ANY = pl.BlockSpec(memory_space=pl.ANY)


def _place():
    return lax.axis_index("x"), lax.axis_index("y"), lax.axis_index("c")


def exchange(srcs, bcast, name, group="all"):
    n = len(srcs)
    ndev = NDEV if group == "all" else NCHIP
    ks = tuple(range(1, NDEV)) if group == "all" else (2, 4, 6)
    out_shape = [jax.ShapeDtypeStruct((ndev,) + (s.shape if bcast else s.shape[1:]), s.dtype) for s in srcs]

    def body(*refs):
        src, dst = refs[:n], refs[n:2 * n]
        send_sems, recv_sems, loc_sems = refs[2 * n:]
        x, y, c = _place()

        def idx(px, py, pc):
            return 4 * px + 2 * py + pc if group == "all" else 2 * px + py

        me = idx(x, y, c)
        copies = []
        for a in range(n):
            cp = pltpu.make_async_copy(src[a] if bcast else src[a].at[me], dst[a].at[me], loc_sems.at[a])
            cp.start()
            copies.append(cp)
        for s, k in enumerate(ks):
            px, py, pc = x ^ ((k >> 2) & 1), y ^ ((k >> 1) & 1), c ^ (k & 1)
            for a in range(n):
                cp = pltpu.make_async_remote_copy(
                    src_ref=src[a] if bcast else src[a].at[idx(px, py, pc)],
                    dst_ref=dst[a].at[me],
                    send_sem=send_sems.at[a, s],
                    recv_sem=recv_sems.at[a, s],
                    device_id=(px, py, pc),
                    device_id_type=pl.DeviceIdType.MESH,
                )
                cp.start()
                copies.append(cp)
        for cp in copies:
            cp.wait()

    return pl.pallas_call(
        body,
        out_shape=out_shape,
        in_specs=[_ANY] * n,
        out_specs=[_ANY] * n,
        scratch_shapes=[
            pltpu.SemaphoreType.DMA((n, len(ks))),
            pltpu.SemaphoreType.DMA((n, len(ks))),
            pltpu.SemaphoreType.DMA((n,)),
        ],
        name=name,
    )(*srcs)


def gather_two_level(srcs, name):
    n = len(srcs)
    out_shape = [jax.ShapeDtypeStruct((NDEV,) + s.shape, s.dtype) for s in srcs]
    chips = (2, 4, 6)

    def body(*refs):
        src, dst = refs[:n], refs[n:2 * n]
        send_sems, recv_sems, loc_sems = refs[2 * n:]
        x, y, c = _place()
        me = 4 * x + 2 * y + c
        sib = (x, y, 1 - c)

        def copy(a, s, block, to, from_src):
            return pltpu.make_async_remote_copy(
                src_ref=src[a] if from_src else dst[a].at[block], dst_ref=dst[a].at[block],
                send_sem=send_sems.at[a, s], recv_sem=recv_sems.at[a, s],
                device_id=to, device_id_type=pl.DeviceIdType.MESH)

        sent = []
        for a in range(n):
            cp = pltpu.make_async_copy(src[a], dst[a].at[me], loc_sems.at[a])
            cp.start()
            sent.append(cp)
        for a in range(n):
            cp = copy(a, 0, me, sib, True)
            cp.start()
            sent.append(cp)
        for j, k in enumerate(chips):
            px, py = x ^ ((k >> 2) & 1), y ^ ((k >> 1) & 1)
            for a in range(n):
                cp = copy(a, 1 + j, me, (px, py, c), True)
                cp.start()
                sent.append(cp)
        for j, k in enumerate(chips):
            px, py = x ^ ((k >> 2) & 1), y ^ ((k >> 1) & 1)
            theirs = 4 * px + 2 * py + c
            for a in range(n):
                copy(a, 1 + j, theirs, (px, py, c), False).wait_recv()
                cp = copy(a, 4 + j, theirs, sib, False)
                cp.start()
                sent.append(cp)
        for a in range(n):
            copy(a, 0, 4 * x + 2 * y + (1 - c), sib, False).wait_recv()
        for j, k in enumerate(chips):
            px, py = x ^ ((k >> 2) & 1), y ^ ((k >> 1) & 1)
            for a in range(n):
                copy(a, 4 + j, 4 * px + 2 * py + (1 - c), sib, False).wait_recv()
        for cp in sent[:n]:
            cp.wait()
        for cp in sent[n:]:
            cp.wait_send()

    return pl.pallas_call(
        body,
        out_shape=out_shape,
        in_specs=[_ANY] * n,
        out_specs=[_ANY] * n,
        scratch_shapes=[
            pltpu.SemaphoreType.DMA((n, NDEV - 1)),
            pltpu.SemaphoreType.DMA((n, NDEV - 1)),
            pltpu.SemaphoreType.DMA((n,)),
        ],
        name=name,
    )(*srcs)


def pair_swap(srcs, name):
    n = len(srcs)

    def body(*refs):
        src, dst = refs[:n], refs[n:2 * n]
        send_sems, recv_sems = refs[2 * n:]
        x, y, c = _place()
        copies = []
        for a in range(n):
            cp = pltpu.make_async_remote_copy(
                src_ref=src[a], dst_ref=dst[a], send_sem=send_sems.at[a], recv_sem=recv_sems.at[a],
                device_id=(x, y, 1 - c), device_id_type=pl.DeviceIdType.MESH)
            cp.start()
            copies.append(cp)
        for cp in copies:
            cp.wait()

    return pl.pallas_call(
        body,
        out_shape=[jax.ShapeDtypeStruct(s.shape, s.dtype) for s in srcs],
        in_specs=[_ANY] * n,
        out_specs=[_ANY] * n,
        scratch_shapes=[pltpu.SemaphoreType.DMA((n,)), pltpu.SemaphoreType.DMA((n,))],
        name=name,
    )(*srcs)


def scatter_reduce(blocks, name):
    c = lax.axis_index("c")
    halves = [b.reshape((NCHIP, 2) + b.shape[1:]) for b in blocks]
    mine = [lax.dynamic_index_in_dim(h, c, axis=1, keepdims=False) for h in halves]
    theirs = [lax.dynamic_index_in_dim(h, 1 - c, axis=1, keepdims=False) for h in halves]
    got = pair_swap(theirs, name + "_d2d")
    sums = []
    for i, (m, g) in enumerate(zip(mine, got)):
        flat = (NCHIP * m.shape[1], m.shape[2])
        s = rowwise(lambda a, b: a.astype(F32) + b.astype(F32), [m.reshape(flat), g.reshape(flat)], [],
                    [(flat[1], m.dtype)], [], 64, f"{name}_pair_sum{i}")[0]
        sums.append(s.reshape(m.shape))
    return exchange(sums, False, name + "_ici", group="core")


def matmul(a, b, mode, out_dtype, name, add=None):
    if mode == "nn":
        (M, K), N = a.shape, b.shape[1]
    elif mode == "nt":
        (M, K), N = a.shape, b.shape[0]
    else:
        (K, M), N = a.shape, b.shape[1]
    tm = _pick(M, (1024, 512, 256, 128))
    tn = _pick(N, (1024, 512, 256, 128))
    osz = jnp.dtype(out_dtype).itemsize

    def vmem_bytes(tk):
        ops = 2 * tk * (tm * a.dtype.itemsize + tn * b.dtype.itemsize)
        return ops + tm * tn * (2 * osz + (4 if tk < K else 0) + (8 if add is not None else 0))

    tk = next((t for t in (K, 2816, 2048, 1408, 1024, 512, 256, 128) if K % t == 0 and vmem_bytes(t) <= MATMUL_VMEM_BUDGET), K)
    nk = K // tk
    dims = {"nn": NN, "nt": NT, "tn": TN}[mode]

    def body(*refs):
        if add is None:
            a_ref, b_ref, o_ref = refs[:3]
            c_ref = None
        else:
            a_ref, b_ref, c_ref, o_ref = refs[:4]

        def prod():
            return _dot(a_ref[...].astype(BF16), b_ref[...].astype(BF16), dims)

        def finish(r):
            if c_ref is not None:
                r = r + c_ref[...].astype(F32)
            o_ref[...] = r.astype(o_ref.dtype)

        if nk == 1:
            finish(prod())
            return
        acc = refs[-1]
        k = pl.program_id(2)

        @pl.when(k == 0)
        def _():
            acc[...] = prod()

        if nk > 2:
            @pl.when((k > 0) & (k < nk - 1))
            def _():
                acc[...] += prod()

        @pl.when(k == nk - 1)
        def _():
            finish(acc[...] + prod())

    a_spec = pl.BlockSpec((tk, tm), lambda i, j, k: (k, i)) if mode == "tn" else pl.BlockSpec((tm, tk), lambda i, j, k: (i, k))
    b_spec = pl.BlockSpec((tn, tk), lambda i, j, k: (j, k)) if mode == "nt" else pl.BlockSpec((tk, tn), lambda i, j, k: (k, j))
    o_spec = pl.BlockSpec((tm, tn), lambda i, j, k: (i, j))
    ins, specs = [a, b], [a_spec, b_spec]
    if add is not None:
        ins.append(add)
        specs.append(o_spec)
    return pl.pallas_call(
        body,
        grid=(M // tm, N // tn, nk),
        in_specs=specs,
        out_specs=o_spec,
        out_shape=jax.ShapeDtypeStruct((M, N), out_dtype),
        scratch_shapes=[pltpu.VMEM((tm, tn), F32)] if nk > 1 else [],
        compiler_params=_cp(("parallel", "parallel", "arbitrary")),
        name=name,
    )(*ins)


def rowwise(fn, tiled, full, out_tiled, out_acc, tile, name):
    tiled = [t if isinstance(t, tuple) else (t, t.shape[1], 0) for t in tiled]
    rows = tiled[0][0].shape[0]
    tile = min(tile, rows)
    assert rows % tile == 0
    nt, nf, no = len(tiled), len(full), len(out_tiled)

    def body(*refs):
        ins = [r[...] for r in refs[:nt + nf]]
        res = fn(*ins)
        if not isinstance(res, (tuple, list)):
            res = (res,)
        outs = refs[nt + nf:]
        for r, v in zip(outs[:no], res[:no]):
            r[...] = v.astype(r.dtype)
        if out_acc:
            @pl.when(pl.program_id(0) == 0)
            def _():
                for r in outs[no:]:
                    r[...] = jnp.zeros_like(r)

            for r, v in zip(outs[no:], res[no:]):
                r[...] += v

    in_specs = [pl.BlockSpec((tile, w), lambda i, cb=cb: (i, cb)) for (_, w, cb) in tiled]
    in_specs += [pl.BlockSpec(f.shape, lambda i, nd=f.ndim: (0,) * nd) for f in full]
    out_specs = [pl.BlockSpec((tile, w), lambda i: (i, 0)) for (w, _) in out_tiled]
    out_specs += [pl.BlockSpec(s, lambda i, nd=len(s): (0,) * nd) for s in out_acc]
    out_shape = [jax.ShapeDtypeStruct((rows, w), dt) for (w, dt) in out_tiled]
    out_shape += [jax.ShapeDtypeStruct(s, F32) for s in out_acc]
    res = pl.pallas_call(
        body,
        grid=(rows // tile,),
        in_specs=in_specs,
        out_specs=out_specs,
        out_shape=out_shape,
        compiler_params=_cp(("arbitrary",) if out_acc else ("parallel",)),
        name=name,
    )(*[t[0] for t in tiled], *full)
    return res


def adam_reduce(parts, w, m, v, name):
    P, R, C = parts.shape
    tr = _pick(R, (64, 32, 16, 8))
    c1 = 1.0 - ADAM_B1 ** ADAM_STEP
    c2 = 1.0 - ADAM_B2 ** ADAM_STEP

    def body(p_ref, w_ref, m_ref, v_ref, g_ref, d_ref, nm_ref, nv_ref):
        g = p_ref[0].astype(F32)
        for j in range(1, P):
            g = g + p_ref[j].astype(F32)
        mm = ADAM_B1 * m_ref[...] + (1.0 - ADAM_B1) * g
        vv = ADAM_B2 * v_ref[...] + (1.0 - ADAM_B2) * (g * g)
        m_hat = mm / c1
        v_hat = vv / c2
        g_ref[...] = g
        d_ref[...] = -ADAM_LR * (m_hat / (jnp.sqrt(v_hat) + ADAM_EPS) + ADAM_WD * w_ref[...])
        nm_ref[...] = mm
        nv_ref[...] = vv

    spec = pl.BlockSpec((tr, C), lambda i: (i, 0))
    return pl.pallas_call(
        body,
        grid=(R // tr,),
        in_specs=[pl.BlockSpec((P, tr, C), lambda i: (0, i, 0)), spec, spec, spec],
        out_specs=[spec] * 4,
        out_shape=[jax.ShapeDtypeStruct((R, C), F32)] * 4,
        compiler_params=_cp(("parallel",)),
        name=name,
    )(parts, w, m, v)


def reduce_parts(parts, name):
    P, R, C = parts.shape
    tr = _pick(R, (64, 32, 16, 8))

    def body(p_ref, g_ref):
        g = p_ref[0]
        for j in range(1, P):
            g = g + p_ref[j]
        g_ref[...] = g

    return pl.pallas_call(
        body,
        grid=(R // tr,),
        in_specs=[pl.BlockSpec((P, tr, C), lambda i: (0, i, 0))],
        out_specs=pl.BlockSpec((tr, C), lambda i: (i, 0)),
        out_shape=jax.ShapeDtypeStruct((R, C), F32),
        compiler_params=_cp(("parallel",)),
        name=name,
    )(parts)


def pack(arrs):
    flat = jnp.concatenate([a.reshape(-1).astype(F32) for a in arrs])
    n = flat.shape[0]
    padded = -(-n // 1024) * 1024
    return jnp.pad(flat, (0, padded - n)).reshape(padded // 128, 128)


def blocks_from_segments(segs, ncols):
    offs = np.cumsum([0] + [s.shape[1] for s in segs]).tolist()
    blocks = []
    for j in range(NDEV):
        lo, hi = j * ncols, (j + 1) * ncols
        parts = [s[:, max(lo, o) - o:min(hi, o + s.shape[1]) - o]
                 for s, o in zip(segs, offs[:-1]) if max(lo, o) < min(hi, o + s.shape[1])]
        blocks.append(jnp.concatenate(parts, axis=1) if len(parts) > 1 else parts[0])
    return jnp.stack(blocks)


def segments_from_blocks(g, widths):
    ncols = g.shape[2]
    offs = np.cumsum([0] + list(widths)).tolist()
    out = []
    for o, w in zip(offs[:-1], widths):
        parts = [g[j][:, max(j * ncols, o) - j * ncols:min((j + 1) * ncols, o + w) - j * ncols]
                 for j in range(NDEV) if max(j * ncols, o) < min((j + 1) * ncols, o + w)]
        out.append(jnp.concatenate(parts, axis=1) if len(parts) > 1 else parts[0])
    return out


def unpack(slab, shapes):
    flat = slab.reshape(-1)
    out, off = [], 0
    for s in shapes:
        size = int(np.prod(s))
        out.append(flat[off:off + size].reshape(s))
        off += size
    return out


def modulate_fwd(x, g, sh, sc, name):
    def fn(x, g, sh, sc):
        return x * _rstd(x) * g * (1.0 + sc) + sh

    return rowwise(fn, [x], [g, sh, sc], [(x.shape[1], BF16)], [], 256, name)[0]


def norm_rope_fwd(p, width, cb, w, cosf, sinf, hd, name):
    nh = width // hd

    def fn(x, cosf, sinf, w):
        outs = []
        for h in range(nh):
            xh = x[:, h * hd:(h + 1) * hd]
            y = xh * _rstd(xh) * w
            outs.append(y * cosf + pltpu.roll(y, hd // 2, 1) * sinf)
        return jnp.concatenate(outs, axis=1) if nh > 1 else outs[0]

    return rowwise(fn, [(p, width, cb), cosf, sinf], [w], [(width, BF16)], [], 256, name)[0]


def norm_rope_bwd(p, width, cb, d, w, cosf, sinf, hd, name):
    nh = width // hd

    def fn(x, d, cosf, sinf, w):
        outs = []
        dw = jnp.zeros((1, hd), F32)
        for h in range(nh):
            xh = x[:, h * hd:(h + 1) * hd]
            dh = d[:, h * hd:(h + 1) * hd].astype(F32)
            r = _rstd(xh)
            n = xh * r
            dy = dh * cosf + pltpu.roll(dh * sinf, hd // 2, 1)
            dw = dw + jnp.sum(dy * n, axis=0, keepdims=True)
            dn = dy * w
            outs.append(r * (dn - n * jnp.mean(dn * n, axis=-1, keepdims=True)))
        return (jnp.concatenate(outs, axis=1) if nh > 1 else outs[0]), dw

    return rowwise(fn, [(p, width, cb), d, cosf, sinf], [w], [(width, BF16)], [(1, hd)], 256, name)


def attention_fwd(qr, kr, pkv, kcr, pkv_c, sink, hkv, hd, name):
    T, L = qr.shape[0], kcr.shape[0]
    G = qr.shape[1] // (hkv * hd)
    nb = T // BLOCK
    scale = hd ** -0.5

    def body(q_ref, kp, kc, kn, vp, vc, vn, ck_ref, cv_ref, sink_ref, o_ref, lse_ref, lser_ref):
        i = pl.program_id(1)
        kwin = jnp.concatenate([kp[...], kc[...], kn[...]], axis=0)
        vwin = jnp.concatenate([vp[...], vc[...], vn[...]], axis=0).astype(BF16)
        ck, cv = ck_ref[...], cv_ref[...].astype(BF16)
        row = lax.broadcasted_iota(jnp.int32, (BLOCK, 3 * BLOCK), 0)
        col = lax.broadcasted_iota(jnp.int32, (BLOCK, 3 * BLOCK), 1)
        rel = col - BLOCK - row
        valid = (jnp.abs(rel) <= WINDOW) & ((col >= BLOCK) | (i > 0)) & ((col < 2 * BLOCK) | (i < nb - 1))
        R = range(G)
        qa = q_ref[...]
        qs = [qa[:, g * hd:(g + 1) * hd] for g in R]
        sks = [sink_ref[g] for g in R]
        ss = [jnp.where(valid, _dot(qs[g], kwin, NT) * scale, NEG) for g in R]
        scs = [_dot(qs[g], ck, NT) * scale for g in R]
        ms = [jnp.maximum(jnp.maximum(jnp.max(ss[g], axis=1, keepdims=True), jnp.max(scs[g], axis=1, keepdims=True)),
                          sks[g]) for g in R]
        ps = [jnp.exp(ss[g] - ms[g]) for g in R]
        pcs = [jnp.exp(scs[g] - ms[g]) for g in R]
        nums = [_dot(ps[g].astype(BF16), vwin, NN) + _dot(pcs[g].astype(BF16), cv, NN) for g in R]
        dens = [jnp.exp(sks[g] - ms[g]) + jnp.sum(ps[g], axis=1, keepdims=True) + jnp.sum(pcs[g], axis=1, keepdims=True)
                for g in R]
        o_ref[...] = jnp.concatenate([(nums[g] / dens[g]).astype(o_ref.dtype) for g in R], axis=1)
        eye = (lax.broadcasted_iota(jnp.int32, (BLOCK, BLOCK), 0)
               == lax.broadcasted_iota(jnp.int32, (BLOCK, BLOCK), 1)).astype(F32)
        for g in R:
            lg = ms[g] + jnp.log(dens[g])
            lse_ref[g] = lg
            lser_ref[g] = jnp.sum(lg * eye, axis=0, keepdims=True)

    kv_specs = [
        pl.BlockSpec((BLOCK, hd), lambda h, i: (jnp.maximum(i - 1, 0), h)),
        pl.BlockSpec((BLOCK, hd), lambda h, i: (i, h)),
        pl.BlockSpec((BLOCK, hd), lambda h, i: (jnp.minimum(i + 1, nb - 1), h)),
    ]
    v_specs = [
        pl.BlockSpec((BLOCK, hd), lambda h, i: (jnp.maximum(i - 1, 0), hkv + h)),
        pl.BlockSpec((BLOCK, hd), lambda h, i: (i, hkv + h)),
        pl.BlockSpec((BLOCK, hd), lambda h, i: (jnp.minimum(i + 1, nb - 1), hkv + h)),
    ]
    return pl.pallas_call(
        body,
        grid=(hkv, nb),
        in_specs=[pl.BlockSpec((BLOCK, G * hd), lambda h, i: (i, h))] + kv_specs + v_specs + [
            pl.BlockSpec((L, hd), lambda h, i: (0, h)),
            pl.BlockSpec((L, hd), lambda h, i: (0, hkv + h)),
            pl.BlockSpec((G, 1, 1), lambda h, i: (h, 0, 0)),
        ],
        out_specs=[
            pl.BlockSpec((BLOCK, G * hd), lambda h, i: (i, h)),
            pl.BlockSpec((G, BLOCK, 1), lambda h, i: (h, i, 0)),
            pl.BlockSpec((G, 1, BLOCK), lambda h, i: (h, 0, i)),
        ],
        out_shape=[jax.ShapeDtypeStruct(qr.shape, BF16), jax.ShapeDtypeStruct((hkv * G, T, 1), F32),
                   jax.ShapeDtypeStruct((hkv * G, 1, T), F32)],
        compiler_params=_cp(("parallel", "parallel")),
        name=name,
    )(qr, kr, kr, kr, pkv, pkv, pkv, kcr, pkv_c, sink)


def attention_bwd_q(qr, kr, pkv, kcr, pkv_c, sink, do, o, lse, hkv, hd, name):
    T, L = qr.shape[0], kcr.shape[0]
    G = qr.shape[1] // (hkv * hd)
    nb = T // BLOCK
    scale = hd ** -0.5

    def body(q_ref, kp, kc, kn, vp, vc, vn, ck_ref, cv_ref, sink_ref, do_ref, o_ref, lse_ref,
             dq_ref, dck_ref, dcv_ref, dsink_ref, drr_ref):
        i = pl.program_id(1)

        @pl.when(i == 0)
        def _():
            dck_ref[...] = jnp.zeros_like(dck_ref)
            dcv_ref[...] = jnp.zeros_like(dcv_ref)
            dsink_ref[...] = jnp.zeros_like(dsink_ref)

        kwin = jnp.concatenate([kp[...], kc[...], kn[...]], axis=0)
        vwin = jnp.concatenate([vp[...], vc[...], vn[...]], axis=0).astype(BF16)
        ck, cv = ck_ref[...], cv_ref[...].astype(BF16)
        row = lax.broadcasted_iota(jnp.int32, (BLOCK, 3 * BLOCK), 0)
        col = lax.broadcasted_iota(jnp.int32, (BLOCK, 3 * BLOCK), 1)
        rel = col - BLOCK - row
        valid = (jnp.abs(rel) <= WINDOW) & ((col >= BLOCK) | (i > 0)) & ((col < 2 * BLOCK) | (i < nb - 1))
        R = range(G)
        qa, doa, oa = q_ref[...], do_ref[...], o_ref[...]
        qs = [qa[:, g * hd:(g + 1) * hd] for g in R]
        dos = [doa[:, g * hd:(g + 1) * hd] for g in R]
        lgs = [lse_ref[g] for g in R]
        sks = [sink_ref[g] for g in R]
        ss = [jnp.where(valid, _dot(qs[g], kwin, NT) * scale, NEG) for g in R]
        scs = [_dot(qs[g], ck, NT) * scale for g in R]
        dps = [_dot(dos[g], vwin, NT) for g in R]
        dpcs = [_dot(dos[g], cv, NT) for g in R]
        drs = [jnp.sum(dos[g].astype(F32) * oa[:, g * hd:(g + 1) * hd].astype(F32), axis=1, keepdims=True) for g in R]
        ps = [jnp.exp(ss[g] - lgs[g]) for g in R]
        pcs = [jnp.exp(scs[g] - lgs[g]) for g in R]
        dss = [(ps[g] * (dps[g] - drs[g]) * scale).astype(BF16) for g in R]
        dscs = [(pcs[g] * (dpcs[g] - drs[g]) * scale).astype(BF16) for g in R]
        dqs = [_dot(dss[g], kwin, NN) + _dot(dscs[g], ck, NN) for g in R]
        dcks = [_dot(dscs[g], qs[g], TN) for g in R]
        dcvs = [_dot(pcs[g].astype(BF16), dos[g], TN) for g in R]
        dq_ref[...] = jnp.concatenate(dqs, axis=1)
        dck_ref[...] += (dcks[0] + dcks[1]) + (dcks[2] + dcks[3]) if G == 4 else sum(dcks[1:], dcks[0])
        dcv_ref[...] += (dcvs[0] + dcvs[1]) + (dcvs[2] + dcvs[3]) if G == 4 else sum(dcvs[1:], dcvs[0])
        eye = (lax.broadcasted_iota(jnp.int32, (BLOCK, BLOCK), 0)
               == lax.broadcasted_iota(jnp.int32, (BLOCK, BLOCK), 1)).astype(F32)
        for g in R:
            dsink_ref[g] += -jnp.sum(jnp.exp(sks[g] - lgs[g]) * drs[g], axis=0, keepdims=True)
            drr_ref[g] = jnp.sum(drs[g] * eye, axis=0, keepdims=True)

    kv_specs = [
        pl.BlockSpec((BLOCK, hd), lambda h, i: (jnp.maximum(i - 1, 0), h)),
        pl.BlockSpec((BLOCK, hd), lambda h, i: (i, h)),
        pl.BlockSpec((BLOCK, hd), lambda h, i: (jnp.minimum(i + 1, nb - 1), h)),
    ]
    v_specs = [
        pl.BlockSpec((BLOCK, hd), lambda h, i: (jnp.maximum(i - 1, 0), hkv + h)),
        pl.BlockSpec((BLOCK, hd), lambda h, i: (i, hkv + h)),
        pl.BlockSpec((BLOCK, hd), lambda h, i: (jnp.minimum(i + 1, nb - 1), hkv + h)),
    ]
    qspec = pl.BlockSpec((BLOCK, G * hd), lambda h, i: (i, h))
    return pl.pallas_call(
        body,
        grid=(hkv, nb),
        in_specs=[qspec] + kv_specs + v_specs + [
            pl.BlockSpec((L, hd), lambda h, i: (0, h)),
            pl.BlockSpec((L, hd), lambda h, i: (0, hkv + h)),
            pl.BlockSpec((G, 1, 1), lambda h, i: (h, 0, 0)),
            qspec, qspec,
            pl.BlockSpec((G, BLOCK, 1), lambda h, i: (h, i, 0)),
        ],
        out_specs=[
            qspec,
            pl.BlockSpec((L, hd), lambda h, i: (0, h)),
            pl.BlockSpec((L, hd), lambda h, i: (0, h)),
            pl.BlockSpec((G, 1, 1), lambda h, i: (h, 0, 0)),
            pl.BlockSpec((G, 1, BLOCK), lambda h, i: (h, 0, i)),
        ],
        out_shape=[
            jax.ShapeDtypeStruct(qr.shape, F32),
            jax.ShapeDtypeStruct((L, hkv * hd), F32),
            jax.ShapeDtypeStruct((L, hkv * hd), F32),
            jax.ShapeDtypeStruct((hkv * G, 1, 1), F32),
            jax.ShapeDtypeStruct((hkv * G, 1, T), F32),
        ],
        compiler_params=_cp(("parallel", "arbitrary")),
        name=name,
    )(qr, kr, kr, kr, pkv, pkv, pkv, kcr, pkv_c, sink, do, o, lse)


def attention_bwd_kv(qr, kr, pkv, do, lse_row, dr_row, hkv, hd, name):
    T = qr.shape[0]
    G = qr.shape[1] // (hkv * hd)
    nb = T // BLOCK
    scale = hd ** -0.5

    def body(k_ref, v_ref, *refs):
        qs, dos, lses, drs = refs[0:3], refs[3:6], refs[6:9], refs[9:12]
        dk_ref, dv_ref = refs[12:]
        j = pl.program_id(1)
        k = k_ref[...]
        v = v_ref[...].astype(BF16)
        row = lax.broadcasted_iota(jnp.int32, (BLOCK, BLOCK), 0)
        col = lax.broadcasted_iota(jnp.int32, (BLOCK, BLOCK), 1)
        bias = []
        for d in range(3):
            iq = j + d - 1
            rel = row - col - (d - 1) * BLOCK
            valid = (jnp.abs(rel) <= WINDOW) & (iq >= 0) & (iq < nb)
            bias += [jnp.where(valid, 0.0, NEG)] * G
        bias = jnp.concatenate(bias, axis=1)

        def stack(refs):
            vals = [r[...] for r in refs]
            return jnp.concatenate([a[:, g * hd:(g + 1) * hd] for a in vals for g in range(G)], axis=0)

        q, dob = stack(qs), stack(dos)
        lrow = jnp.concatenate([r[g] for r in lses for g in range(G)], axis=1)
        drow = jnp.concatenate([r[g] for r in drs for g in range(G)], axis=1)
        st = _dot(k, q, NT) * scale + bias
        pt = jnp.exp(st - lrow)
        dpt = _dot(v, dob, NT)
        dst = (pt * (dpt - drow) * scale).astype(BF16)
        dk_ref[...] = _dot(dst, q, NN).astype(dk_ref.dtype)
        dv_ref[...] = _dot(pt.astype(BF16), dob, NN).astype(dv_ref.dtype)

    def q3(width_block):
        return [
            pl.BlockSpec(width_block, lambda h, j: (jnp.maximum(j - 1, 0), h)),
            pl.BlockSpec(width_block, lambda h, j: (j, h)),
            pl.BlockSpec(width_block, lambda h, j: (jnp.minimum(j + 1, nb - 1), h)),
        ]

    row3 = [
        pl.BlockSpec((G, 1, BLOCK), lambda h, j: (h, 0, jnp.maximum(j - 1, 0))),
        pl.BlockSpec((G, 1, BLOCK), lambda h, j: (h, 0, j)),
        pl.BlockSpec((G, 1, BLOCK), lambda h, j: (h, 0, jnp.minimum(j + 1, nb - 1))),
    ]
    qb = (BLOCK, G * hd)
    return pl.pallas_call(
        body,
        grid=(hkv, nb),
        in_specs=[pl.BlockSpec((BLOCK, hd), lambda h, j: (j, h)), pl.BlockSpec((BLOCK, hd), lambda h, j: (j, hkv + h))]
        + q3(qb) + q3(qb) + row3 + row3,
        out_specs=[pl.BlockSpec((BLOCK, hd), lambda h, j: (j, h))] * 2,
        out_shape=[jax.ShapeDtypeStruct((T, hkv * hd), BF16)] * 2,
        compiler_params=_cp(("parallel", "parallel")),
        name=name,
    )(kr, pkv, qr, qr, qr, do, do, do, lse_row, lse_row, lse_row, dr_row, dr_row, dr_row)


def gate_fwd(plr, wf, wb, bf, bb, name):
    n = wf.shape[1]

    def fn(lr, wf, wb, bf, bb):
        lrb = lr.astype(BF16)
        outs = []
        for w, b in ((wf, bf), (wb, bb)):
            z = _dot(lrb, w.astype(BF16), NN) + b
            outs.append((jnp.minimum(z, 0.0) - jnp.log(1.0 + jnp.exp(-jnp.abs(z)))) / GLA_GATE_NORM)
        return outs

    return rowwise(fn, [plr], [wf, wb, bf, bb], [(n, F32), (n, F32)], [], 256, name)


def gate_bwd(plr, dgf, dgb, wf, wb, bf, bb, name):
    n = wf.shape[1]

    def fn(lr, dgf, dgb, wf, wb, bf, bb):
        lrb = lr.astype(BF16)
        dlr = jnp.zeros(lr.shape, F32)
        res = []
        for w, b, dg in ((wf, bf, dgf), (wb, bb, dgb)):
            wb16 = w.astype(BF16)
            z = _dot(lrb, wb16, NN) + b
            dz = dg * _sig(-z) / GLA_GATE_NORM
            dzb = dz.astype(BF16)
            dlr = dlr + _dot(dzb, wb16, NT)
            res += [_dot(lrb, dzb, TN), jnp.sum(dz, axis=0, keepdims=True)]
        return [dlr] + res

    return rowwise(fn, [plr, dgf, dgb], [wf, wb, bf, bb], [(128, BF16)],
                   [(128, n), (1, n), (128, n), (1, n)], 256, name)


def _tri_dot(tri_b, x):
    x1 = x.astype(BF16)
    r1 = x - x1.astype(F32)
    x2 = r1.astype(BF16)
    x3 = (r1 - x2.astype(F32)).astype(BF16)
    return _dot(tri_b, x1, NN) + _dot(tri_b, x2, NN) + _dot(tri_b, x3, NN)


def gla_fwd(pqk, pv, gl, s0, heads, reverse, name, o_add=None):
    T = pqk.shape[0]
    dk = pqk.shape[1] // (2 * heads)
    dv = pv.shape[1] // heads
    C = GLA_CHUNK
    nc = T // C
    qscale = dk ** -0.5

    def body(*refs):
        if o_add is None:
            q_ref, k_ref, v_ref, g_ref, s0_ref, o_ref, st_ref, sf_ref, S = refs
            oa_ref = None
        else:
            q_ref, k_ref, v_ref, g_ref, s0_ref, oa_ref, o_ref, st_ref, sf_ref, S = refs
        n = pl.program_id(0)

        @pl.when(n == 0)
        def _():
            S[...] = s0_ref[...]

        r = lax.broadcasted_iota(jnp.int32, (C, C), 0)
        c = lax.broadcasted_iota(jnp.int32, (C, C), 1)
        tri = (r <= c) if reverse else (r >= c)
        trib = tri.astype(BF16)
        ga, qa, ka, va = g_ref[...], q_ref[...], k_ref[...], v_ref[...]
        sts = [S[h] for h in range(heads)]
        H = range(heads)
        gs = [ga[:, h * dk:(h + 1) * dk] for h in H]
        bs = [_tri_dot(trib, g) for g in gs]
        bls = [jnp.sum(g, axis=0, keepdims=True) for g in gs]
        vs = [va[:, h * dv:(h + 1) * dv].astype(BF16) for h in H]
        qes = [(qa[:, h * dk:(h + 1) * dk].astype(F32) * qscale * jnp.exp(bs[h])).astype(BF16) for h in H]
        kes = [(ka[:, h * dk:(h + 1) * dk].astype(F32) * jnp.exp(-bs[h])).astype(BF16) for h in H]
        kls = [(ka[:, h * dk:(h + 1) * dk].astype(F32) * jnp.exp(bls[h] - bs[h])).astype(BF16) for h in H]
        inter = [_dot(qes[h], sts[h].astype(BF16), NT) for h in H]
        upd = [_dot(vs[h], kls[h], TN) for h in H]
        As = [jnp.where(tri, _dot(qes[h], kes[h], NT), 0.0).astype(BF16) for h in H]
        outs = [inter[h] + _dot(As[h], vs[h], NN) for h in H]
        news = [sts[h] * jnp.exp(bls[h]) + upd[h] for h in H]
        o = jnp.concatenate(outs, axis=1)
        if oa_ref is not None:
            o = o + oa_ref[...]
        o_ref[...] = o
        for h in range(heads):
            st_ref[0, h] = sts[h]
            S[h] = news[h]

        @pl.when(n == nc - 1)
        def _():
            for h in range(heads):
                sf_ref[h] = news[h]

    def ci(n):
        return (nc - 1 - n) if reverse else n

    specs = [
        pl.BlockSpec((C, heads * dk), lambda n: (ci(n), 0)),
        pl.BlockSpec((C, heads * dk), lambda n: (ci(n), 1)),
        pl.BlockSpec((C, heads * dv), lambda n: (ci(n), 0)),
        pl.BlockSpec((C, heads * dk), lambda n: (ci(n), 0)),
        pl.BlockSpec((heads, dv, dk), lambda n: (0, 0, 0)),
    ]
    ins = [pqk, pqk, pv, gl, s0]
    if o_add is not None:
        specs.append(pl.BlockSpec((C, heads * dv), lambda n: (ci(n), 0)))
        ins.append(o_add)
    return pl.pallas_call(
        body,
        grid=(nc,),
        in_specs=specs,
        out_specs=[
            pl.BlockSpec((C, heads * dv), lambda n: (ci(n), 0)),
            pl.BlockSpec((1, heads, dv, dk), lambda n: (ci(n), 0, 0, 0)),
            pl.BlockSpec((heads, dv, dk), lambda n: (0, 0, 0)),
        ],
        out_shape=[
            jax.ShapeDtypeStruct((T, heads * dv), F32),
            jax.ShapeDtypeStruct((nc, heads, dv, dk), F32),
            jax.ShapeDtypeStruct((heads, dv, dk), F32),
        ],
        scratch_shapes=[pltpu.VMEM((heads, dv, dk), F32)],
        compiler_params=_cp(("arbitrary",)),
        name=name,
    )(*ins)


def gla_bwd(pqk, pv, gl, states, do, dsf, heads, reverse, name, acc=None):
    T = pqk.shape[0]
    dk = pqk.shape[1] // (2 * heads)
    dv = pv.shape[1] // heads
    C = GLA_CHUNK
    nc = T // C
    qscale = dk ** -0.5

    def body(*refs):
        if acc is None:
            q_ref, k_ref, v_ref, g_ref, st_ref, do_ref, dsf_ref, dq_ref, dk_ref, dv_ref, dg_ref, ds0_ref, dS = refs
            aq = ak = av = None
        else:
            (q_ref, k_ref, v_ref, g_ref, st_ref, do_ref, dsf_ref, aq, ak, av,
             dq_ref, dk_ref, dv_ref, dg_ref, ds0_ref, dS) = refs
        n = pl.program_id(0)

        @pl.when(n == 0)
        def _():
            dS[...] = dsf_ref[...]

        r = lax.broadcasted_iota(jnp.int32, (C, C), 0)
        c = lax.broadcasted_iota(jnp.int32, (C, C), 1)
        tri = (r <= c) if reverse else (r >= c)
        tri_t = (r >= c) if reverse else (r <= c)
        trib, tritb = tri.astype(BF16), tri_t.astype(BF16)
        ga, qa, ka, va, doa = g_ref[...], q_ref[...], k_ref[...], v_ref[...], do_ref[...]
        sts = [st_ref[0, h] for h in range(heads)]
        dsts = [dS[h] for h in range(heads)]
        H = range(heads)
        gs = [ga[:, h * dk:(h + 1) * dk] for h in H]
        bs = [_tri_dot(trib, g) for g in gs]
        bls = [jnp.sum(g, axis=0, keepdims=True) for g in gs]
        ebs = [jnp.exp(b) for b in bs]
        enbs = [jnp.exp(-b) for b in bs]
        elbs = [jnp.exp(bls[h] - bs[h]) for h in H]
        ebls = [jnp.exp(bl) for bl in bls]
        vbs = [va[:, h * dv:(h + 1) * dv].astype(BF16) for h in H]
        dobs = [doa[:, h * dv:(h + 1) * dv].astype(BF16) for h in H]
        qes = [qa[:, h * dk:(h + 1) * dk].astype(F32) * qscale * ebs[h] for h in H]
        kes = [ka[:, h * dk:(h + 1) * dk].astype(F32) * enbs[h] for h in H]
        kls = [ka[:, h * dk:(h + 1) * dk].astype(F32) * elbs[h] for h in H]
        qebs = [a.astype(BF16) for a in qes]
        kebs = [a.astype(BF16) for a in kes]
        klbs = [a.astype(BF16) for a in kls]
        stbs = [a.astype(BF16) for a in sts]
        dstbs = [a.astype(BF16) for a in dsts]
        ps = [jnp.where(tri, _dot(qebs[h], kebs[h], NT), 0.0).astype(BF16) for h in H]
        dps = [jnp.where(tri, _dot(dobs[h], vbs[h], NT), 0.0).astype(BF16) for h in H]
        dqe1 = [_dot(dobs[h], stbs[h], NN) for h in H]
        dkls = [_dot(vbs[h], dstbs[h], NN) for h in H]
        dv1 = [_dot(klbs[h], dstbs[h], NT) for h in H]
        dsn1 = [_dot(dobs[h], qebs[h], TN) for h in H]
        dqes = [dqe1[h] + _dot(dps[h], kebs[h], NN) for h in H]
        dkes = [_dot(dps[h], qebs[h], TN) for h in H]
        dvs = [_dot(ps[h], dobs[h], TN) + dv1[h] for h in H]
        dbls = [ebls[h] * jnp.sum(dsts[h] * sts[h], axis=0, keepdims=True)
                + jnp.sum(dkls[h] * kls[h], axis=0, keepdims=True) for h in H]
        dsns = [dsn1[h] + dsts[h] * ebls[h] for h in H]
        dqs = [dqes[h] * ebs[h] * qscale for h in H]
        dks = [dkes[h] * enbs[h] + dkls[h] * elbs[h] for h in H]
        dbs = [dqes[h] * qes[h] - dkes[h] * kes[h] - dkls[h] * kls[h] for h in H]
        dgs = [_tri_dot(tritb, dbs[h]) + dbls[h] for h in H]
        dq, dkk, dvv = (jnp.concatenate(a, axis=1) for a in (dqs, dks, dvs))
        if aq is not None:
            dq = dq + aq[...].astype(F32)
            dkk = dkk + ak[...].astype(F32)
            dvv = dvv + av[...].astype(F32)
        dq_ref[...] = dq.astype(dq_ref.dtype)
        dk_ref[...] = dkk.astype(dk_ref.dtype)
        dv_ref[...] = dvv.astype(dv_ref.dtype)
        dg_ref[...] = jnp.concatenate(dgs, axis=1)
        for h in range(heads):
            dS[h] = dsns[h]

        @pl.when(n == nc - 1)
        def _():
            for h in range(heads):
                ds0_ref[h] = dsns[h]

    def ci(n):
        return n if reverse else (nc - 1 - n)

    kspec = pl.BlockSpec((C, heads * dk), lambda n: (ci(n), 0))
    vspec = pl.BlockSpec((C, heads * dv), lambda n: (ci(n), 0))
    sspec = pl.BlockSpec((heads, dv, dk), lambda n: (0, 0, 0))
    specs = [
        kspec,
        pl.BlockSpec((C, heads * dk), lambda n: (ci(n), 1)),
        vspec,
        kspec,
        pl.BlockSpec((1, heads, dv, dk), lambda n: (ci(n), 0, 0, 0)),
        vspec,
        sspec,
    ]
    ins = [pqk, pqk, pv, gl, states, do, dsf]
    odt = F32 if acc is None else BF16
    if acc is not None:
        specs += [kspec, kspec, vspec]
        ins += list(acc)
    return pl.pallas_call(
        body,
        grid=(nc,),
        in_specs=specs,
        out_specs=[kspec, kspec, vspec, kspec, sspec],
        out_shape=[
            jax.ShapeDtypeStruct((T, heads * dk), odt),
            jax.ShapeDtypeStruct((T, heads * dk), odt),
            jax.ShapeDtypeStruct((T, heads * dv), odt),
            jax.ShapeDtypeStruct((T, heads * dk), F32),
            jax.ShapeDtypeStruct((heads, dv, dk), F32),
        ],
        scratch_shapes=[pltpu.VMEM((heads, dv, dk), F32)],
        compiler_params=_cp(("arbitrary",)),
        name=name,
    )(*ins)


def gla_out_fwd(og, prb, gn, heads, name):
    dv = og.shape[1] // heads

    def fn(og, rb, gn):
        outs = []
        for h in range(heads):
            oh = og[:, h * dv:(h + 1) * dv]
            outs.append(oh * _rstd(oh) * gn)
        y = jnp.concatenate(outs, axis=1)
        return y * (rb * _sig(rb))

    return rowwise(fn, [og, prb], [gn], [(og.shape[1], BF16)], [], 256, name)[0]


def gla_out_bwd(og, prb, du, gn, heads, name):
    dv = og.shape[1] // heads

    def fn(og, rb, du, gn):
        sg = _sig(rb)
        silu = rb * sg
        dsilu = sg * (1.0 + rb * (1.0 - sg))
        dog, ys = [], []
        dgn = jnp.zeros((1, dv), F32)
        for h in range(heads):
            sl = slice(h * dv, (h + 1) * dv)
            oh = og[:, sl]
            r = _rstd(oh)
            n = oh * r
            ys.append(n * gn)
            dy = du[:, sl] * silu[:, sl]
            dgn = dgn + jnp.sum(dy * n, axis=0, keepdims=True)
            dn = dy * gn
            dog.append(r * (dn - n * jnp.mean(dn * n, axis=-1, keepdims=True)))
        y = jnp.concatenate(ys, axis=1)
        return jnp.concatenate(dog, axis=1), du * y * dsilu, dgn

    return rowwise(fn, [og, prb, du], [gn], [(og.shape[1], F32), (og.shape[1], BF16)], [(1, dv)], 128, name)


def conv_specs(T, tt, tc, off, order):
    r8 = tt // 8
    last8 = T // 8 - 1
    if order == "ij":
        return [
            pl.BlockSpec((tt, tc), lambda i, j: (i, j + off)),
            pl.BlockSpec((8, tc), lambda i, j: (jnp.maximum(i * r8 - 1, 0), j + off)),
            pl.BlockSpec((8, tc), lambda i, j: (jnp.minimum((i + 1) * r8, last8), j + off)),
        ]
    return [
        pl.BlockSpec((tt, tc), lambda j, i: (i, j + off)),
        pl.BlockSpec((8, tc), lambda j, i: (jnp.maximum(i * r8 - 1, 0), j + off)),
        pl.BlockSpec((8, tc), lambda j, i: (jnp.minimum((i + 1) * r8, last8), j + off)),
    ]


def _shifted(u, hp, hn, i, nt_):
    tt = u.shape[0]
    row = lax.broadcasted_iota(jnp.int32, u.shape, 0)
    r8 = lax.broadcasted_iota(jnp.int32, hp.shape, 0)
    prev = jnp.sum(jnp.where(r8 == 7, hp, 0.0), axis=0, keepdims=True) * (i > 0).astype(F32)
    nxt = jnp.sum(jnp.where(r8 == 0, hn, 0.0), axis=0, keepdims=True) * (i < nt_ - 1).astype(F32)
    down = jnp.where(row == 0, prev, pltpu.roll(u, 1, 0))
    up = jnp.where(row == tt - 1, nxt, pltpu.roll(u, tt - 1, 0))
    return down, up


def conv_swiglu_fwd(u, cw, cb, name):
    T, F2 = u.shape
    F = F2 // 2
    tt = min(256, T)
    tc = _pick(F, (512, 256, 128))
    nt_, ncol = T // tt, F // tc

    def body(ua, uap, uan, ug, ugp, ugn, wa, wg, ba, bg, f_ref):
        i = pl.program_id(0)
        res = []
        for um, up_, un, w, b in ((ua, uap, uan, wa, ba), (ug, ugp, ugn, wg, bg)):
            x = um[...]
            down, up = _shifted(x, up_[...], un[...], i, nt_)
            res.append(w[0] * down + w[1] * x + w[2] * up + b[...])
        a, g = res
        f_ref[...] = (a * _sig(a) * g).astype(f_ref.dtype)

    wspec = lambda off: pl.BlockSpec((3, 1, tc), lambda i, j: (0, 0, j + off))
    bspec = lambda off: pl.BlockSpec((1, tc), lambda i, j: (0, j + off))
    return pl.pallas_call(
        body,
        grid=(nt_, ncol),
        in_specs=conv_specs(T, tt, tc, 0, "ij") + conv_specs(T, tt, tc, ncol, "ij")
        + [wspec(0), wspec(ncol), bspec(0), bspec(ncol)],
        out_specs=pl.BlockSpec((tt, tc), lambda i, j: (i, j)),
        out_shape=jax.ShapeDtypeStruct((T, F), BF16),
        compiler_params=_cp(("parallel", "parallel")),
        name=name,
    )(u, u, u, u, u, u, cw, cw, cb, cb)


def conv_swiglu_bwd(u, cw, cb, df, name):
    T, F2 = u.shape
    F = F2 // 2
    tt = min(256, T)
    tc = _pick(F, (512, 256, 128))
    nt_, ncol = T // tt, F // tc

    def body(ua, uap, uan, ug, ugp, ugn, wa, wg, ba, bg, df_ref, da_ref, dg_ref, dwa, dwg, dba, dbg):
        i = pl.program_id(1)

        @pl.when(i == 0)
        def _():
            for r in (dwa, dwg, dba, dbg):
                r[...] = jnp.zeros_like(r)

        sh = []
        res = []
        for um, up_, un, w, b in ((ua, uap, uan, wa, ba), (ug, ugp, ugn, wg, bg)):
            x = um[...]
            down, up = _shifted(x, up_[...], un[...], i, nt_)
            sh.append((down, x, up))
            res.append(w[0] * down + w[1] * x + w[2] * up + b[...])
        a, g = res
        d = df_ref[...].astype(F32)
        sg = _sig(a)
        da = d * g * sg * (1.0 + a * (1.0 - sg))
        dg = d * a * sg
        da_ref[...] = da
        dg_ref[...] = dg
        for dd, (down, x, up), dw, db in ((da, sh[0], dwa, dba), (dg, sh[1], dwg, dbg)):
            dw[0] += jnp.sum(dd * down, axis=0, keepdims=True)
            dw[1] += jnp.sum(dd * x, axis=0, keepdims=True)
            dw[2] += jnp.sum(dd * up, axis=0, keepdims=True)
            db[...] += jnp.sum(dd, axis=0, keepdims=True)

    wspec = lambda off: pl.BlockSpec((3, 1, tc), lambda j, i: (0, 0, j + off))
    bspec = lambda off: pl.BlockSpec((1, tc), lambda j, i: (0, j + off))
    tile = pl.BlockSpec((tt, tc), lambda j, i: (i, j))
    return pl.pallas_call(
        body,
        grid=(ncol, nt_),
        in_specs=conv_specs(T, tt, tc, 0, "ji") + conv_specs(T, tt, tc, ncol, "ji")
        + [wspec(0), wspec(ncol), bspec(0), bspec(ncol), tile],
        out_specs=[tile, tile, wspec(0), wspec(0), bspec(0), bspec(0)],
        out_shape=[
            jax.ShapeDtypeStruct((T, F), F32), jax.ShapeDtypeStruct((T, F), F32),
            jax.ShapeDtypeStruct((3, 1, F), F32), jax.ShapeDtypeStruct((3, 1, F), F32),
            jax.ShapeDtypeStruct((1, F), F32), jax.ShapeDtypeStruct((1, F), F32),
        ],
        compiler_params=_cp(("parallel", "arbitrary")),
        name=name,
    )(u, u, u, u, u, u, cw, cw, cb, cb, df)


CONV_HALO = 16


def conv_halo_specs(T, tt, tc, off):
    r = tt // CONV_HALO
    last = T // CONV_HALO - 1
    return [
        pl.BlockSpec((tt, tc), lambda j, i: (i, j + off)),
        pl.BlockSpec((CONV_HALO, tc), lambda j, i: (jnp.maximum(i * r - 1, 0), j + off)),
        pl.BlockSpec((CONV_HALO, tc), lambda j, i: (jnp.minimum((i + 1) * r, last), j + off)),
    ]


def conv_swiglu_bwd_fused(u, cw, cb, df, name):
    T, F2 = u.shape
    F = F2 // 2
    tt = min(512, T)
    tc = _pick(F, (512, 256, 128))
    nt_, ncol = T // tt, F // tc
    H = CONV_HALO
    n = tt + 2 * H

    def body(ua, uap, uan, ug, ugp, ugn, dm, dp_, dn, wa, wg, ba, bg, dua_ref, dug_ref, dwa, dwg, dba, dbg):
        i = pl.program_id(1)

        @pl.when(i == 0)
        def _():
            for r in (dwa, dwg, dba, dbg):
                r[...] = jnp.zeros_like(r)

        keep_p = (i > 0).astype(F32)
        keep_n = (i < nt_ - 1).astype(F32)

        def ext(m, p, nx):
            return jnp.concatenate([p[...].astype(F32) * keep_p, m[...].astype(F32), nx[...].astype(F32) * keep_n],
                                   axis=0)

        d = ext(dm, dp_, dn)
        conv, parts = [], []
        for m, p, nx, w, b in ((ua, uap, uan, wa, ba), (ug, ugp, ugn, wg, bg)):
            x = ext(m, p, nx)
            down, up = pltpu.roll(x, 1, 0), pltpu.roll(x, n - 1, 0)
            parts.append((down, x, up))
            conv.append(w[0] * down + w[1] * x + w[2] * up + b[...])
        a, g = conv
        sg = _sig(a)
        da = d * g * sg * (1.0 + a * (1.0 - sg))
        dg = d * a * sg
        for dd, w, (down, x, up), o_ref, dw, db in ((da, wa, parts[0], dua_ref, dwa, dba),
                                                    (dg, wg, parts[1], dug_ref, dwg, dbg)):
            du = w[0] * pltpu.roll(dd, n - 1, 0) + w[1] * dd + w[2] * pltpu.roll(dd, 1, 0)
            o_ref[...] = du[H:H + tt].astype(o_ref.dtype)
            ddm = dd[H:H + tt]
            dw[0] += jnp.sum(ddm * down[H:H + tt], axis=0, keepdims=True)
            dw[1] += jnp.sum(ddm * x[H:H + tt], axis=0, keepdims=True)
            dw[2] += jnp.sum(ddm * up[H:H + tt], axis=0, keepdims=True)
            db[...] += jnp.sum(ddm, axis=0, keepdims=True)

    wspec = lambda off: pl.BlockSpec((3, 1, tc), lambda j, i: (0, 0, j + off))
    bspec = lambda off: pl.BlockSpec((1, tc), lambda j, i: (0, j + off))
    tile = pl.BlockSpec((tt, tc), lambda j, i: (i, j))
    return pl.pallas_call(
        body,
        grid=(ncol, nt_),
        in_specs=conv_halo_specs(T, tt, tc, 0) + conv_halo_specs(T, tt, tc, ncol) + conv_halo_specs(T, tt, tc, 0)
        + [wspec(0), wspec(ncol), bspec(0), bspec(ncol)],
        out_specs=[tile, tile, wspec(0), wspec(0), bspec(0), bspec(0)],
        out_shape=[
            jax.ShapeDtypeStruct((T, F), BF16), jax.ShapeDtypeStruct((T, F), BF16),
            jax.ShapeDtypeStruct((3, 1, F), F32), jax.ShapeDtypeStruct((3, 1, F), F32),
            jax.ShapeDtypeStruct((1, F), F32), jax.ShapeDtypeStruct((1, F), F32),
        ],
        compiler_params=_cp(("parallel", "arbitrary")),
        name=name,
    )(u, u, u, u, u, u, df, df, df, cw, cw, cb, cb)


def conv_transpose(d, cw, off, name):
    T, F = d.shape
    tt = min(256, T)
    tc = _pick(F, (512, 256, 128))
    nt_, ncol = T // tt, F // tc
    offb = off // tc

    def body(dm, dp_, dn, w, o_ref):
        i = pl.program_id(0)
        x = dm[...]
        down, up = _shifted(x, dp_[...], dn[...], i, nt_)
        o_ref[...] = (w[0] * up + w[1] * x + w[2] * down).astype(o_ref.dtype)

    return pl.pallas_call(
        body,
        grid=(nt_, ncol),
        in_specs=conv_specs(T, tt, tc, 0, "ij") + [pl.BlockSpec((3, 1, tc), lambda i, j: (0, 0, j + offb))],
        out_specs=pl.BlockSpec((tt, tc), lambda i, j: (i, j)),
        out_shape=jax.ShapeDtypeStruct((T, F), BF16),
        compiler_params=_cp(("parallel", "parallel")),
        name=name,
    )(d, d, d, cw)


def rope_tables(n, hd):
    rows = n // GRID_W
    row = jnp.repeat(jnp.arange(rows), GRID_W)
    col = jnp.tile(jnp.arange(GRID_W), rows)
    n_freq = hd // 4
    inv = ROPE_THETA ** (-jnp.arange(n_freq, dtype=F32) / n_freq)
    ang = jnp.concatenate([row[:, None] * inv, col[:, None] * inv], axis=-1)
    cos, sin = jnp.cos(ang), jnp.sin(ang)
    return jnp.concatenate([cos, cos], axis=-1), jnp.concatenate([-sin, sin], axis=-1)


def local_step(x, ctx, tgt, mod, modc, W, P):
    T, D = x.shape
    L = ctx.shape[0]
    hd, hq, hkv, gh = P["hd"], P["hq"], P["hkv"], P["gh"]
    sh1, sc1, g1, sh2, sc2, g2 = mod
    csh1, csc1 = modc
    kvw = hkv * hd
    gkw = W["gqk"].shape[1] // 2
    gdv = D // gh
    gdk = gkw // gh

    h = modulate_fwd(x, P["g_mix"], sh1, sc1, "mod1")
    hc = modulate_fwd(ctx, P["g_mix"], csh1, csc1, "mod1_ctx")
    pq = matmul(h, W["q"], "nn", F32, "proj_q")
    pkv = matmul(h, W["kv"], "nn", F32, "proj_kv")
    pgqk = matmul(h, W["gqk"], "nn", F32, "proj_gqk")
    pgv = matmul(h, W["gv"], "nn", F32, "proj_gv")
    prb = matmul(h, W["rb"], "nn", F32, "proj_rb")
    plr = matmul(h, W["lr"], "nn", F32, "proj_lr")
    pgab = matmul(h, W["gab"], "nn", F32, "proj_gab")
    pkv_c = matmul(hc, W["kv"], "nn", F32, "proj_kv_ctx")
    pgqk_c = matmul(hc, W["gqk"], "nn", F32, "proj_gqk_ctx")
    pgv_c = matmul(hc, W["gv"], "nn", F32, "proj_gv_ctx")
    plr_c = matmul(hc, W["lr"], "nn", F32, "proj_lr_ctx")

    cosf, sinf = rope_tables(T, hd)
    one_c, zero_c = jnp.ones((L, hd), F32), jnp.zeros((L, hd), F32)
    qr = norm_rope_fwd(pq, hq * hd, 0, P["q_norm"], cosf, sinf, hd, "qnorm")
    kr = norm_rope_fwd(pkv, kvw, 0, P["k_norm"], cosf, sinf, hd, "knorm")
    kcr = norm_rope_fwd(pkv_c, kvw, 0, P["k_norm"], one_c, zero_c, hd, "knorm_ctx")
    sink = P["attn_sink"].reshape(hq, 1, 1)
    o_attn, lse, lse_row = attention_fwd(qr, kr, pkv, kcr, pkv_c, sink, hkv, hd, "attn_fwd")

    gf, gb = gate_fwd(plr, W["gate_f"], W["gate_b"], P["b_gate_f"], P["b_gate_b"], "gates")
    gfc, gbc = gate_fwd(plr_c, W["gate_f"], W["gate_b"], P["b_gate_f"], P["b_gate_b"], "gates_ctx")
    zero_state = jnp.zeros((gh, gdv, gdk), F32)
    _, st_cf, s_cf = gla_fwd(pgqk_c, pgv_c, gfc, zero_state, gh, False, "gla_ctx_f")
    _, st_cb, s_cb = gla_fwd(pgqk_c, pgv_c, gbc, zero_state, gh, True, "gla_ctx_b")
    of, st_f, _ = gla_fwd(pgqk, pgv, gf, s_cf, gh, False, "gla_f")
    og, st_b, _ = gla_fwd(pgqk, pgv, gb, s_cb, gh, True, "gla_b", o_add=of)
    ug = gla_out_fwd(og, prb, P["gla_norm"], gh, "gla_out")

    ya = matmul(o_attn, W["attn_o"], "nn", F32, "attn_o")
    yg = matmul(ug, W["gla_o"], "nn", F32, "gla_o")

    def merge_fn(ya, yg, ga, gb_):
        return _sig(ga) * ya + _sig(gb_) * yg

    z = rowwise(merge_fn, [ya, yg, (pgab, D, 0), (pgab, D, 1)], [], [(D, BF16)], [], 256, "merge")[0]
    mo = matmul(z, W["out"], "nn", F32, "w_out")

    def res_fn(x, mo, g1, gffn, sh2, sc2):
        x1 = x + g1 * mo
        return x1, x1 * _rstd(x1) * gffn * (1.0 + sc2) + sh2

    x1, h2 = rowwise(res_fn, [x, mo], [g1, P["g_ffn"], sh2, sc2], [(D, F32), (D, BF16)], [], 256, "res_mod2")
    u = matmul(h2, W["up"], "nn", F32, "w_up")
    cw3 = W["conv_w"].reshape(3, 1, -1)
    f = conv_swiglu_fwd(u, cw3, P["conv_b"], "conv_swiglu")
    fo = matmul(f, W["down"], "nn", F32, "w_down")

    def final_fn(x1, fo, tgt, g2):
        e = x1 + g2 * fo - tgt
        dy = e * (1.0 / D)
        lsum = jnp.sum(jnp.sum(e * e, axis=1, keepdims=True), axis=0, keepdims=True)
        return dy, dy * g2, jnp.broadcast_to(lsum, (1, 128)), jnp.sum(dy * fo, axis=0, keepdims=True)

    dy, dfo, lsum, dg2 = rowwise(final_fn, [x1, fo, tgt], [g2], [(D, F32), (D, BF16)], [(1, 128), (1, D)], 256, "loss")
    df = matmul(dfo, W["down"], "nt", BF16, "d_f")
    dw_down = matmul(f, dfo, "tn", BF16, "dw_down")
    du_a, du_g, dcw_a, dcw_g, dcb_a, dcb_g = conv_swiglu_bwd_fused(u, cw3, P["conv_b"], df, "conv_swiglu_bwd")
    Fh = du_a.shape[1]
    dh2 = matmul(du_a, W["up"][:, :Fh], "nt", F32, "d_h2_a")
    dh2 = matmul(du_g, W["up"][:, Fh:], "nt", F32, "d_h2_g", add=dh2)
    dw_up = [matmul(h2, du_a, "tn", BF16, "dw_up_a"), matmul(h2, du_g, "tn", BF16, "dw_up_g")]

    def mod2_bwd_fn(x1, dh, dy, mo, gffn, sc2, g1):
        r = _rstd(x1)
        n = x1 * r
        dyy = dh * (1.0 + sc2)
        dn = dyy * gffn
        dx1 = dy + r * (dn - n * jnp.mean(dn * n, axis=-1, keepdims=True))
        s0 = lambda a: jnp.sum(a, axis=0, keepdims=True)
        return dx1, dx1 * g1, s0(dyy * n), s0(dh), s0(dh * n * gffn), s0(dx1 * mo)

    dx1, dmo, dg_ffn, dsh2, dsc2, dg1 = rowwise(
        mod2_bwd_fn, [x1, dh2, dy, mo], [P["g_ffn"], sc2, g1], [(D, F32), (D, BF16)], [(1, D)] * 4, 128, "mod2_bwd")
    dz = matmul(dmo, W["out"], "nt", F32, "d_z")
    dw_out = matmul(z, dmo, "tn", BF16, "dw_out")

    def merge_bwd_fn(dz, ya, yg, ga, gb_):
        sa, sb = _sig(ga), _sig(gb_)
        return dz * sa, dz * sb, jnp.concatenate([dz * ya * sa * (1.0 - sa), dz * yg * sb * (1.0 - sb)], axis=1)

    dya, dyg, dpgab = rowwise(merge_bwd_fn, [dz, ya, yg, (pgab, D, 0), (pgab, D, 1)], [],
                              [(D, BF16), (D, BF16), (2 * D, BF16)], [], 128, "merge_bwd")
    do_attn = matmul(dya, W["attn_o"], "nt", BF16, "d_oattn")
    dw_attn_o = matmul(o_attn, dya, "tn", BF16, "dw_attn_o")
    dug = matmul(dyg, W["gla_o"], "nt", F32, "d_ug")
    dw_gla_o = matmul(ug, dyg, "tn", BF16, "dw_gla_o")
    dog, dprb, dgn = gla_out_bwd(og, prb, dug, P["gla_norm"], gh, "gla_out_bwd")

    dq1, dk1, dv1, dgf, ds_cf = gla_bwd(pgqk, pgv, gf, st_f, dog, zero_state, gh, False, "gla_f_bwd")
    dgq, dgk, dpgv, dgb, ds_cb = gla_bwd(pgqk, pgv, gb, st_b, dog, zero_state, gh, True, "gla_b_bwd",
                                          acc=(dq1, dk1, dv1))
    dpgqk = jnp.concatenate([dgq, dgk], axis=1)
    zero_do = jnp.zeros((L, gh * gdv), F32)
    cq1, ck1, cv1, dgfc, _ = gla_bwd(pgqk_c, pgv_c, gfc, st_cf, zero_do, ds_cf, gh, False, "gla_ctx_f_bwd")
    cq, ck, dpgv_c, dgbc, _ = gla_bwd(pgqk_c, pgv_c, gbc, st_cb, zero_do, ds_cb, gh, True, "gla_ctx_b_bwd",
                                      acc=(cq1, ck1, cv1))
    dpgqk_c = jnp.concatenate([cq, ck], axis=1)
    dplr, dwgf, dbgf, dwgb, dbgb = gate_bwd(plr, dgf, dgb, W["gate_f"], W["gate_b"], P["b_gate_f"], P["b_gate_b"], "gates_bwd")
    dplr_c, dwgf_c, dbgf_c, dwgb_c, dbgb_c = gate_bwd(plr_c, dgfc, dgbc, W["gate_f"], W["gate_b"], P["b_gate_f"],
                                                      P["b_gate_b"], "gates_ctx_bwd")

    dqr, dkc_r, dvc, dsink, dr_row = attention_bwd_q(qr, kr, pkv, kcr, pkv_c, sink, do_attn, o_attn, lse, hkv, hd,
                                                     "attn_bwd_q")
    dkr, dv = attention_bwd_kv(qr, kr, pkv, do_attn, lse_row, dr_row, hkv, hd, "attn_bwd_kv")
    dpq, dqn = norm_rope_bwd(pq, hq * hd, 0, dqr, P["q_norm"], cosf, sinf, hd, "qnorm_bwd")
    dpk, dkn = norm_rope_bwd(pkv, kvw, 0, dkr, P["k_norm"], cosf, sinf, hd, "knorm_bwd")
    dpk_c, dkn_c = norm_rope_bwd(pkv_c, kvw, 0, dkc_r, P["k_norm"], one_c, zero_c, hd, "knorm_ctx_bwd")
    dpkv = jnp.concatenate([dpk, dv], axis=1)
    dpkv_c = jnp.concatenate([dpk_c, dvc.astype(BF16)], axis=1)

    dw_q = matmul(h, dpq, "tn", BF16, "dw_q")
    dw_kv = matmul(h, dpkv, "tn", BF16, "dw_kv", add=matmul(hc, dpkv_c, "tn", F32, "dw_kv_ctx"))
    dw_gqk = matmul(h, dpgqk, "tn", BF16, "dw_gqk", add=matmul(hc, dpgqk_c, "tn", F32, "dw_gqk_ctx"))
    dw_gv = matmul(h, dpgv, "tn", BF16, "dw_gv", add=matmul(hc, dpgv_c, "tn", F32, "dw_gv_ctx"))
    dw_rb = matmul(h, dprb, "tn", BF16, "dw_rb")
    dw_lr = matmul(h, dplr, "tn", BF16, "dw_lr", add=matmul(hc, dplr_c, "tn", F32, "dw_lr_ctx"))
    dw_gab = matmul(h, dpgab, "tn", BF16, "dw_gab")
    lrw = P["lowrank"]
    dw_in = [dw_q, dw_kv, dw_gqk, dw_gv, dw_rb, dw_lr[:, :2 * lrw], dw_gab]

    dh = matmul(dpq, W["q"], "nt", F32, "dh_q")
    dh = matmul(dpkv, W["kv"], "nt", F32, "dh_kv", add=dh)
    dh = matmul(dpgqk, W["gqk"], "nt", F32, "dh_gqk", add=dh)
    dh = matmul(dpgv, W["gv"], "nt", F32, "dh_gv", add=dh)
    dh = matmul(dprb, W["rb"], "nt", F32, "dh_rb", add=dh)
    dh = matmul(dplr, W["lr"], "nt", F32, "dh_lr", add=dh)
    dh = matmul(dpgab, W["gab"], "nt", F32, "dh_gab", add=dh)
    dhc = matmul(dpkv_c, W["kv"], "nt", F32, "dhc_kv")
    dhc = matmul(dpgqk_c, W["gqk"], "nt", F32, "dhc_gqk", add=dhc)
    dhc = matmul(dpgv_c, W["gv"], "nt", F32, "dhc_gv", add=dhc)
    dhc = matmul(dplr_c, W["lr"], "nt", F32, "dhc_lr", add=dhc)

    def mod1_bwd_fn(x, dh, dres, g, sc):
        r = _rstd(x)
        n = x * r
        dyy = dh * (1.0 + sc)
        dn = dyy * g
        dx = dres + r * (dn - n * jnp.mean(dn * n, axis=-1, keepdims=True))
        s0 = lambda a: jnp.sum(a, axis=0, keepdims=True)
        return dx, s0(dyy * n), s0(dh), s0(dh * n * g)

    grad_x, dgmix, dsh1, dsc1 = rowwise(mod1_bwd_fn, [x, dh, dx1], [P["g_mix"], sc1], [(D, F32)], [(1, D)] * 3,
                                        128, "mod1_bwd")
    _, dgmix_c, dcsh1, dcsc1 = rowwise(mod1_bwd_fn, [ctx, dhc, jnp.zeros_like(ctx)], [P["g_mix"], csc1], [(D, F32)],
                                       [(1, D)] * 3, 128, "mod1_ctx_bwd")

    zD = jnp.zeros((1, D), F32)
    grads = dict(
        w_in=dw_in, w_attn_o=dw_attn_o, w_gla_o=dw_gla_o, w_out=dw_out, w_up=dw_up, w_down=dw_down,
        dmod_x=jnp.concatenate([dsh1, dsc1, dg1, dsh2, dsc2, dg2], axis=1),
        dmod_c=jnp.concatenate([dcsh1, dcsc1, zD, zD, zD, zD], axis=1),
        g_mix=dgmix + dgmix_c, q_norm=dqn, k_norm=dkn + dkn_c, attn_sink=dsink.reshape(1, hq),
        w_gate_f=(dwgf + dwgf_c)[:lrw], b_gate_f=dbgf + dbgf_c,
        w_gate_b=(dwgb + dwgb_c)[lrw:2 * lrw], b_gate_b=dbgb + dbgb_c,
        gla_norm=dgn, g_ffn=dg_ffn,
        conv_w=jnp.concatenate([dcw_a, dcw_g], axis=2).reshape(3, -1),
        conv_b=jnp.concatenate([dcb_a, dcb_g], axis=1),
    )
    return lsum[0, 0], grad_x, grads


SMALL_REPL = ("c_ctx", "b_mod", "g_mix", "q_norm", "k_norm", "attn_sink", "b_gate_f", "b_gate_b", "gla_norm", "g_ffn",
              "conv_b")
SMALL_SHARD = ("w_gate_f", "w_gate_b", "conv_w")
ORDER = ("c_ctx", "w_mod", "b_mod", "g_mix", "w_in", "q_norm", "k_norm", "attn_sink", "w_gate_f", "b_gate_f",
         "w_gate_b", "b_gate_b", "gla_norm", "w_attn_o", "w_gla_o", "w_out", "g_ffn", "w_up", "conv_w", "conv_b",
         "w_down")


def kernel(x, c, ctx, c_ctx, w_mod, b_mod, g_mix, w_in, q_norm, k_norm, attn_sink, w_gate_f, b_gate_f, w_gate_b, b_gate_b, gla_norm, w_attn_o, w_gla_o, w_out, g_ffn, w_up, conv_w, conv_b, w_down, loss_target, m_c_ctx, m_w_mod, m_b_mod, m_g_mix, m_w_in, m_q_norm, m_k_norm, m_attn_sink, m_w_gate_f, m_b_gate_f, m_w_gate_b, m_b_gate_b, m_gla_norm, m_w_attn_o, m_w_gla_o, m_w_out, m_g_ffn, m_w_up, m_conv_w, m_conv_b, m_w_down, v_c_ctx, v_w_mod, v_b_mod, v_g_mix, v_w_in, v_q_norm, v_k_norm, v_attn_sink, v_w_gate_f, v_b_gate_f, v_w_gate_b, v_b_gate_b, v_gla_norm, v_w_attn_o, v_w_gla_o, v_w_out, v_g_ffn, v_w_up, v_conv_w, v_conv_b, v_w_down):
    loc = dict(locals())
    Wt = {n: loc[n] for n in ORDER}
    Mt = {n: loc["m_" + n] for n in ORDER}
    Vt = {n: loc["v_" + n] for n in ORDER}
    me = 4 * lax.axis_index("x") + 2 * lax.axis_index("y") + lax.axis_index("c")

    D = x.shape[-1]
    hd = q_norm.shape[-1]
    hq = attn_sink.shape[-1]
    gdv = gla_norm.shape[-1]
    gh = D // gdv
    gdk = D // 2 // gh
    lrw = w_gate_f.shape[1]
    in_w = NDEV * w_in.shape[-1]
    kvw = (in_w - hq * hd - 2 * gh * gdk - 2 * gh * gdv - 2 * lrw - 2 * D) // 2
    hkv = kvw // hd
    gcols = w_gate_f.shape[-1]
    F2 = NDEV * w_up.shape[-1]
    mcols = w_mod.shape[-1]

    x2, ctx2, tgt2 = x[0], ctx[0], loss_target[0]

    c_all = exchange([jnp.pad(c, ((0, 7), (0, 0)))], True, "gather_c")[0][:, 0, :]
    c9 = jnp.concatenate([c_all, c_ctx[None, :], jnp.zeros((7, D), F32)], axis=0)
    s9 = rowwise(lambda a: a * _sig(a), [c9], [], [(D, F32)], [], 16, "silu_c")[0]
    bias = jnp.broadcast_to(lax.dynamic_slice_in_dim(b_mod, me * mcols, mcols, axis=1), (16, mcols))
    mod_cols = matmul(s9, w_mod[0], "nn", F32, "mod_cols", add=bias)
    mod_all = exchange([mod_cols], True, "gather_mod")[0]
    mod_all = jnp.transpose(mod_all, (1, 0, 2)).reshape(16, NDEV * mcols)
    mod_me = lax.dynamic_slice_in_dim(mod_all, me, 1, axis=0)
    mod = [mod_me[:, i * D:(i + 1) * D] for i in range(6)]
    modc = [mod_all[8:9, i * D:(i + 1) * D] for i in range(2)]

    o3 = jnp.stack([w_attn_o[0], w_gla_o[0], w_out[0]]).astype(BF16)
    small_w = pack([w_gate_f[0], w_gate_b[0], conv_w[0]])
    g_in, g_o3, g_up, g_down, g_small = gather_two_level(
        [w_in[0].astype(BF16), o3, w_up[0].astype(BF16), w_down[0].astype(BF16), small_w], "gather_w")
    seg = segments_from_blocks(g_in, [hq * hd, 2 * kvw, 2 * gh * gdk, gh * gdv, gh * gdv, 2 * lrw, 2 * D])
    small_parts = [unpack(g_small[j], [w_gate_f[0].shape, w_gate_b[0].shape, conv_w[0].shape]) for j in range(NDEV)]
    wgf = jnp.concatenate([p[0] for p in small_parts], axis=1)
    wgb = jnp.concatenate([p[1] for p in small_parts], axis=1)
    cw_full = jnp.concatenate([p[2] for p in small_parts], axis=1)
    o3f = [g_o3[:, i].reshape(-1, D) for i in range(3)]
    W = dict(
        q=seg[0], kv=seg[1], gqk=seg[2], gv=seg[3], rb=seg[4],
        lr=jnp.pad(seg[5], ((0, 0), (0, 128 - 2 * lrw))), gab=seg[6],
        gate_f=jnp.pad(wgf, ((0, 128 - lrw), (0, 0))),
        gate_b=jnp.pad(wgb, ((lrw, 128 - 2 * lrw), (0, 0))),
        attn_o=o3f[0], gla_o=o3f[1], out=o3f[2],
        up=jnp.concatenate([g_up[j] for j in range(NDEV)], axis=1),
        down=g_down.reshape(-1, D),
        conv_w=cw_full,
    )
    P = dict(hd=hd, hq=hq, hkv=hkv, gh=gh, lowrank=lrw, g_mix=g_mix, q_norm=q_norm, k_norm=k_norm, attn_sink=attn_sink,
             b_gate_f=b_gate_f, b_gate_b=b_gate_b, gla_norm=gla_norm, g_ffn=g_ffn, conv_b=conv_b)

    lsum, grad_x, G = local_step(x2, ctx2, tgt2, mod, modc, W, P)
    loss = lax.psum(0.5 * lsum / D, ("x", "y", "c"))

    dm = exchange([jnp.concatenate([G["dmod_x"], G["dmod_c"], jnp.zeros((6, 6 * D), F32)], axis=0)], True,
                  "gather_dmod")[0]
    dmc = reduce_parts(dm[:, 1:2, :].reshape(NDEV, 6 * D // 128, 128), "sum_dmod_ctx").reshape(1, 6 * D)
    dM = jnp.concatenate([dm[:, 0, :], dmc, jnp.zeros((7, 6 * D), F32)], axis=0)
    dM_cols = lax.dynamic_slice_in_dim(dM, me * mcols, mcols, axis=1)
    g_w_mod = matmul(s9, dM_cols, "tn", F32, "dw_mod")
    g_b_mod = reduce_parts(dM.reshape(16, 6 * D // 128, 128), "sum_db_mod").reshape(1, 6 * D)
    dsc = matmul(dM_cols[8:16], w_mod[0], "nt", F32, "d_silu_ctx")
    cc = jnp.broadcast_to(c_ctx[None, :], (8, D))

    def dsilu_fn(d, a):
        sg = _sig(a)
        return d * sg * (1.0 + a * (1.0 - sg))

    g_cctx_part = rowwise(dsilu_fn, [dsc, cc], [], [(D, F32)], [], 8, "d_c_ctx")[0][0:1]

    small_names = ("c_ctx", "g_mix", "q_norm", "k_norm", "attn_sink", "b_gate_f", "b_gate_b", "gla_norm", "g_ffn",
                   "conv_b", "w_gate_f", "w_gate_b", "conv_w")
    G["c_ctx"] = g_cctx_part
    sm_shapes = [G[n].shape for n in small_names]
    sm_all = exchange([pack([G[n] for n in small_names])], True, "gather_small_grads")[0]
    sm_tot = unpack(reduce_parts(sm_all, "sum_small_grads"), sm_shapes)
    gs = dict(zip(small_names, sm_tot))
    gs["b_mod"] = g_b_mod
    gs["w_gate_f"] = lax.dynamic_slice_in_dim(gs["w_gate_f"], me * gcols, gcols, axis=1)
    gs["w_gate_b"] = lax.dynamic_slice_in_dim(gs["w_gate_b"], me * gcols, gcols, axis=1)
    ccols = conv_w.shape[-1]
    gs["conv_w"] = lax.dynamic_slice_in_dim(gs["conv_w"], me * ccols, ccols, axis=1)

    wcols = w_in.shape[-1]
    ucols = w_up.shape[-1]
    drows = w_down.shape[1]
    orows = w_attn_o.shape[1]
    s_in = blocks_from_segments(G["w_in"], wcols)
    s_o3 = jnp.concatenate([G["w_attn_o"].reshape(NDEV, orows, D), G["w_gla_o"].reshape(NDEV, orows, D),
                            G["w_out"].reshape(NDEV, orows, D)], axis=1)
    s_up = blocks_from_segments(G["w_up"], ucols)
    s_down = G["w_down"].reshape(NDEV, drows, D)
    r_in, r_o3, r_up, r_down = scatter_reduce([s_in, s_o3, s_up, s_down], "scatter_grads")

    out = {}
    out["w_in"] = adam_reduce(r_in, w_in[0], m_w_in[0], v_w_in[0], "adam_w_in")
    o3w = jnp.concatenate([w_attn_o[0], w_gla_o[0], w_out[0]], axis=0)
    o3m = jnp.concatenate([m_w_attn_o[0], m_w_gla_o[0], m_w_out[0]], axis=0)
    o3v = jnp.concatenate([v_w_attn_o[0], v_w_gla_o[0], v_w_out[0]], axis=0)
    ro3 = adam_reduce(r_o3, o3w, o3m, o3v, "adam_o3")
    for i, n in enumerate(("w_attn_o", "w_gla_o", "w_out")):
        out[n] = [a[i * orows:(i + 1) * orows] for a in ro3]
    out["w_up"] = adam_reduce(r_up, w_up[0], m_w_up[0], v_w_up[0], "adam_w_up")
    out["w_down"] = adam_reduce(r_down, w_down[0], m_w_down[0], v_w_down[0], "adam_w_down")
    out["w_mod"] = adam_reduce(g_w_mod[None], w_mod[0], m_w_mod[0], v_w_mod[0], "adam_w_mod")
    sm_names = SMALL_REPL + SMALL_SHARD
    shapes = [Wt[n].shape for n in sm_names]
    rs = adam_reduce(pack([gs[n] for n in sm_names])[None], pack([Wt[n] for n in sm_names]),
                     pack([Mt[n] for n in sm_names]), pack([Vt[n] for n in sm_names]), "adam_small")
    rs = [unpack(a, shapes) for a in rs]
    for i, n in enumerate(sm_names):
        out[n] = [a[i] for a in rs]

    res = [loss, grad_x[None]]
    for k in range(4):
        for n in ORDER:
            res.append(out[n][k].reshape(Wt[n].shape))
    return tuple(res)
```

```python
import jax
import jax.numpy as jnp
import numpy as np
from jax import lax
from jax.experimental import pallas as pl
from jax.experimental.pallas import tpu as pltpu

F32 = jnp.float32
BF16 = jnp.bfloat16

NDEV = 8
NCHIP = 4
EPS = 1e-6
WINDOW = 128
BLOCK = 128
GRID_W = 64
ROPE_THETA = 10000.0
GLA_CHUNK = 128
GLA_GATE_NORM = 16.0
ADAM_LR = 0.001
ADAM_B1 = 0.9
ADAM_B2 = 0.999
ADAM_EPS = 1e-08
ADAM_WD = 0.01
ADAM_STEP = 10
V7X_VMEM_LIMIT = 56 * 1024 * 1024
MATMUL_VMEM_BUDGET = 40 * 1024 * 1024
NEG = -1e30

NN = ((1,), (0,))
NT = ((1,), (1,))
TN = ((0,), (0,))


def _dot(a, b, dims):
    return lax.dot_general(a, b, (dims, ((), ())), preferred_element_type=F32)


def _cp(sem):
    return pltpu.CompilerParams(dimension_semantics=sem, vmem_limit_bytes=V7X_VMEM_LIMIT)


def _pick(n, cands):
    for c in cands:
        if n % c == 0:
            return c
    return n


def _sig(x):
    return 1.0 / (1.0 + jnp.exp(-x))


def _sig_tanh(x):
    return 0.5 * jnp.tanh(0.5 * x) + 0.5


def _rstd(x):
    return lax.rsqrt(jnp.mean(x * x, axis=-1, keepdims=True) + EPS)


_ANY = pl.BlockSpec(memory_space=pl.ANY)


def _place():
    return lax.axis_index("x"), lax.axis_index("y"), lax.axis_index("c")


def exchange(srcs, bcast, name, group="all"):
    n = len(srcs)
    ndev = NDEV if group == "all" else NCHIP
    ks = tuple(range(1, NDEV)) if group == "all" else (2, 4, 6)
    out_shape = [jax.ShapeDtypeStruct((ndev,) + (s.shape if bcast else s.shape[1:]), s.dtype) for s in srcs]

    def body(*refs):
        src, dst = refs[:n], refs[n:2 * n]
        send_sems, recv_sems, loc_sems = refs[2 * n:]
        x, y, c = _place()

        def idx(px, py, pc):
            return 4 * px + 2 * py + pc if group == "all" else 2 * px + py

        me = idx(x, y, c)
        copies = []
        for a in range(n):
            cp = pltpu.make_async_copy(src[a] if bcast else src[a].at[me], dst[a].at[me], loc_sems.at[a])
            cp.start()
            copies.append(cp)
        for s, k in enumerate(ks):
            px, py, pc = x ^ ((k >> 2) & 1), y ^ ((k >> 1) & 1), c ^ (k & 1)
            for a in range(n):
                cp = pltpu.make_async_remote_copy(
                    src_ref=src[a] if bcast else src[a].at[idx(px, py, pc)],
                    dst_ref=dst[a].at[me],
                    send_sem=send_sems.at[a, s],
                    recv_sem=recv_sems.at[a, s],
                    device_id=(px, py, pc),
                    device_id_type=pl.DeviceIdType.MESH,
                )
                cp.start()
                copies.append(cp)
        for cp in copies:
            cp.wait()

    return pl.pallas_call(
        body,
        out_shape=out_shape,
        in_specs=[_ANY] * n,
        out_specs=[_ANY] * n,
        scratch_shapes=[
            pltpu.SemaphoreType.DMA((n, len(ks))),
            pltpu.SemaphoreType.DMA((n, len(ks))),
            pltpu.SemaphoreType.DMA((n,)),
        ],
        name=name,
    )(*srcs)


def gather_two_level(srcs, name):
    n = len(srcs)
    out_shape = [jax.ShapeDtypeStruct((NDEV,) + s.shape, s.dtype) for s in srcs]
    chips = (2, 4, 6)

    def body(*refs):
        src, dst = refs[:n], refs[n:2 * n]
        send_sems, recv_sems, loc_sems = refs[2 * n:]
        x, y, c = _place()
        me = 4 * x + 2 * y + c
        sib = (x, y, 1 - c)

        def copy(a, s, block, to, from_src):
            return pltpu.make_async_remote_copy(
                src_ref=src[a] if from_src else dst[a].at[block], dst_ref=dst[a].at[block],
                send_sem=send_sems.at[a, s], recv_sem=recv_sems.at[a, s],
                device_id=to, device_id_type=pl.DeviceIdType.MESH)

        sent = []
        for a in range(n):
            cp = pltpu.make_async_copy(src[a], dst[a].at[me], loc_sems.at[a])
            cp.start()
            sent.append(cp)
        for a in range(n):
            cp = copy(a, 0, me, sib, True)
            cp.start()
            sent.append(cp)
        for j, k in enumerate(chips):
            px, py = x ^ ((k >> 2) & 1), y ^ ((k >> 1) & 1)
            for a in range(n):
                cp = copy(a, 1 + j, me, (px, py, c), True)
                cp.start()
                sent.append(cp)
        for j, k in enumerate(chips):
            px, py = x ^ ((k >> 2) & 1), y ^ ((k >> 1) & 1)
            theirs = 4 * px + 2 * py + c
            for a in range(n):
                copy(a, 1 + j, theirs, (px, py, c), False).wait_recv()
                cp = copy(a, 4 + j, theirs, sib, False)
                cp.start()
                sent.append(cp)
        for a in range(n):
            copy(a, 0, 4 * x + 2 * y + (1 - c), sib, False).wait_recv()
        for j, k in enumerate(chips):
            px, py = x ^ ((k >> 2) & 1), y ^ ((k >> 1) & 1)
            for a in range(n):
                copy(a, 4 + j, 4 * px + 2 * py + (1 - c), sib, False).wait_recv()
        for cp in sent[:n]:
            cp.wait()
        for cp in sent[n:]:
            cp.wait_send()

    return pl.pallas_call(
        body,
        out_shape=out_shape,
        in_specs=[_ANY] * n,
        out_specs=[_ANY] * n,
        scratch_shapes=[
            pltpu.SemaphoreType.DMA((n, NDEV - 1)),
            pltpu.SemaphoreType.DMA((n, NDEV - 1)),
            pltpu.SemaphoreType.DMA((n,)),
        ],
        name=name,
    )(*srcs)


def pair_swap(srcs, name):
    n = len(srcs)

    def body(*refs):
        src, dst = refs[:n], refs[n:2 * n]
        send_sems, recv_sems = refs[2 * n:]
        x, y, c = _place()
        copies = []
        for a in range(n):
            cp = pltpu.make_async_remote_copy(
                src_ref=src[a], dst_ref=dst[a], send_sem=send_sems.at[a], recv_sem=recv_sems.at[a],
                device_id=(x, y, 1 - c), device_id_type=pl.DeviceIdType.MESH)
            cp.start()
            copies.append(cp)
        for cp in copies:
            cp.wait()

    return pl.pallas_call(
        body,
        out_shape=[jax.ShapeDtypeStruct(s.shape, s.dtype) for s in srcs],
        in_specs=[_ANY] * n,
        out_specs=[_ANY] * n,
        scratch_shapes=[pltpu.SemaphoreType.DMA((n,)), pltpu.SemaphoreType.DMA((n,))],
        name=name,
    )(*srcs)


def scatter_reduce(blocks, name):
    c = lax.axis_index("c")
    halves = [b.reshape((NCHIP, 2) + b.shape[1:]) for b in blocks]
    mine = [lax.dynamic_index_in_dim(h, c, axis=1, keepdims=False) for h in halves]
    theirs = [lax.dynamic_index_in_dim(h, 1 - c, axis=1, keepdims=False) for h in halves]
    got = pair_swap(theirs, name + "_d2d")
    sums = []
    for i, (m, g) in enumerate(zip(mine, got)):
        flat = (NCHIP * m.shape[1], m.shape[2])
        s = rowwise(lambda a, b: a.astype(F32) + b.astype(F32), [m.reshape(flat), g.reshape(flat)], [],
                    [(flat[1], m.dtype)], [], _pick(flat[0], (512, 256, 128, 64)), f"{name}_pair_sum{i}")[0]
        sums.append(s.reshape(m.shape))
    return exchange(sums, False, name + "_ici", group="core")


def matmul(a, b, mode, out_dtype, name, add=None):
    if mode == "nn":
        (M, K), N = a.shape, b.shape[1]
    elif mode == "nt":
        (M, K), N = a.shape, b.shape[0]
    else:
        (K, M), N = a.shape, b.shape[1]
    tm = _pick(M, (1024, 512, 256, 128))
    tn = _pick(N, (1024, 512, 256, 128))
    osz = jnp.dtype(out_dtype).itemsize

    def vmem_bytes(tk):
        ops = 2 * tk * (tm * a.dtype.itemsize + tn * b.dtype.itemsize)
        return ops + tm * tn * (2 * osz + (4 if tk < K else 0) + (8 if add is not None else 0))

    tk = next((t for t in (K, 2816, 2048, 1408, 1024, 512, 256, 128) if K % t == 0 and vmem_bytes(t) <= MATMUL_VMEM_BUDGET), K)
    nk = K // tk
    dims = {"nn": NN, "nt": NT, "tn": TN}[mode]

    def body(*refs):
        if add is None:
            a_ref, b_ref, o_ref = refs[:3]
            c_ref = None
        else:
            a_ref, b_ref, c_ref, o_ref = refs[:4]

        def prod():
            return _dot(a_ref[...].astype(BF16), b_ref[...].astype(BF16), dims)

        def finish(r):
            if c_ref is not None:
                r = r + c_ref[...].astype(F32)
            o_ref[...] = r.astype(o_ref.dtype)

        if nk == 1:
            finish(prod())
            return
        acc = refs[-1]
        k = pl.program_id(2)

        @pl.when(k == 0)
        def _():
            acc[...] = prod()

        if nk > 2:
            @pl.when((k > 0) & (k < nk - 1))
            def _():
                acc[...] += prod()

        @pl.when(k == nk - 1)
        def _():
            finish(acc[...] + prod())

    a_spec = pl.BlockSpec((tk, tm), lambda i, j, k: (k, i)) if mode == "tn" else pl.BlockSpec((tm, tk), lambda i, j, k: (i, k))
    b_spec = pl.BlockSpec((tn, tk), lambda i, j, k: (j, k)) if mode == "nt" else pl.BlockSpec((tk, tn), lambda i, j, k: (k, j))
    o_spec = pl.BlockSpec((tm, tn), lambda i, j, k: (i, j))
    ins, specs = [a, b], [a_spec, b_spec]
    if add is not None:
        ins.append(add)
        specs.append(o_spec)
    return pl.pallas_call(
        body,
        grid=(M // tm, N // tn, nk),
        in_specs=specs,
        out_specs=o_spec,
        out_shape=jax.ShapeDtypeStruct((M, N), out_dtype),
        scratch_shapes=[pltpu.VMEM((tm, tn), F32)] if nk > 1 else [],
        compiler_params=_cp(("parallel", "parallel", "arbitrary")),
        name=name,
    )(*ins)


def rowwise(fn, tiled, full, out_tiled, out_acc, tile, name):
    tiled = [t if isinstance(t, tuple) else (t, t.shape[1], 0) for t in tiled]
    rows = tiled[0][0].shape[0]
    tile = min(tile, rows)
    assert rows % tile == 0
    nt, nf, no = len(tiled), len(full), len(out_tiled)

    def body(*refs):
        ins = [r[...] for r in refs[:nt + nf]]
        res = fn(*ins)
        if not isinstance(res, (tuple, list)):
            res = (res,)
        outs = refs[nt + nf:]
        for r, v in zip(outs[:no], res[:no]):
            r[...] = v.astype(r.dtype)
        if out_acc:
            @pl.when(pl.program_id(0) == 0)
            def _():
                for r in outs[no:]:
                    r[...] = jnp.zeros_like(r)

            for r, v in zip(outs[no:], res[no:]):
                r[...] += v

    in_specs = [pl.BlockSpec((tile, w), lambda i, cb=cb: (i, cb)) for (_, w, cb) in tiled]
    in_specs += [pl.BlockSpec(f.shape, lambda i, nd=f.ndim: (0,) * nd) for f in full]
    out_specs = [pl.BlockSpec((tile, w), lambda i: (i, 0)) for (w, _) in out_tiled]
    out_specs += [pl.BlockSpec(s, lambda i, nd=len(s): (0,) * nd) for s in out_acc]
    out_shape = [jax.ShapeDtypeStruct((rows, w), dt) for (w, dt) in out_tiled]
    out_shape += [jax.ShapeDtypeStruct(s, F32) for s in out_acc]
    res = pl.pallas_call(
        body,
        grid=(rows // tile,),
        in_specs=in_specs,
        out_specs=out_specs,
        out_shape=out_shape,
        compiler_params=_cp(("arbitrary",) if out_acc else ("parallel",)),
        name=name,
    )(*[t[0] for t in tiled], *full)
    return res


def adam_reduce(parts, w, m, v, name):
    P, R, C = parts.shape
    tr = _pick(R, (64, 32, 16, 8))
    c1 = 1.0 - ADAM_B1 ** ADAM_STEP
    c2 = 1.0 - ADAM_B2 ** ADAM_STEP

    def body(p_ref, w_ref, m_ref, v_ref, g_ref, d_ref, nm_ref, nv_ref):
        g = p_ref[0].astype(F32)
        for j in range(1, P):
            g = g + p_ref[j].astype(F32)
        mm = ADAM_B1 * m_ref[...] + (1.0 - ADAM_B1) * g
        vv = ADAM_B2 * v_ref[...] + (1.0 - ADAM_B2) * (g * g)
        m_hat = mm / c1
        v_hat = vv / c2
        g_ref[...] = g
        d_ref[...] = -ADAM_LR * (m_hat / (jnp.sqrt(v_hat) + ADAM_EPS) + ADAM_WD * w_ref[...])
        nm_ref[...] = mm
        nv_ref[...] = vv

    spec = pl.BlockSpec((tr, C), lambda i: (i, 0))
    return pl.pallas_call(
        body,
        grid=(R // tr,),
        in_specs=[pl.BlockSpec((P, tr, C), lambda i: (0, i, 0)), spec, spec, spec],
        out_specs=[spec] * 4,
        out_shape=[jax.ShapeDtypeStruct((R, C), F32)] * 4,
        compiler_params=_cp(("parallel",)),
        name=name,
    )(parts, w, m, v)


def reduce_parts(parts, name):
    P, R, C = parts.shape
    tr = _pick(R, (64, 32, 16, 8))

    def body(p_ref, g_ref):
        g = p_ref[0]
        for j in range(1, P):
            g = g + p_ref[j]
        g_ref[...] = g

    return pl.pallas_call(
        body,
        grid=(R // tr,),
        in_specs=[pl.BlockSpec((P, tr, C), lambda i: (0, i, 0))],
        out_specs=pl.BlockSpec((tr, C), lambda i: (i, 0)),
        out_shape=jax.ShapeDtypeStruct((R, C), F32),
        compiler_params=_cp(("parallel",)),
        name=name,
    )(parts)


def pack(arrs):
    flat = jnp.concatenate([a.reshape(-1).astype(F32) for a in arrs])
    n = flat.shape[0]
    padded = -(-n // 1024) * 1024
    return jnp.pad(flat, (0, padded - n)).reshape(padded // 128, 128)


def blocks_from_segments(segs, ncols):
    offs = np.cumsum([0] + [s.shape[1] for s in segs]).tolist()
    blocks = []
    for j in range(NDEV):
        lo, hi = j * ncols, (j + 1) * ncols
        parts = [s[:, max(lo, o) - o:min(hi, o + s.shape[1]) - o]
                 for s, o in zip(segs, offs[:-1]) if max(lo, o) < min(hi, o + s.shape[1])]
        blocks.append(jnp.concatenate(parts, axis=1) if len(parts) > 1 else parts[0])
    return jnp.stack(blocks)


def rows_to_blocks(g):
    return g.reshape(NDEV, -1, g.shape[1])


def segments_from_blocks(g, widths):
    ncols = g.shape[2]
    offs = np.cumsum([0] + list(widths)).tolist()
    out = []
    for o, w in zip(offs[:-1], widths):
        parts = [g[j][:, max(j * ncols, o) - j * ncols:min((j + 1) * ncols, o + w) - j * ncols]
                 for j in range(NDEV) if max(j * ncols, o) < min((j + 1) * ncols, o + w)]
        out.append(jnp.concatenate(parts, axis=1) if len(parts) > 1 else parts[0])
    return out


def unpack(slab, shapes):
    flat = slab.reshape(-1)
    out, off = [], 0
    for s in shapes:
        size = int(np.prod(s))
        out.append(flat[off:off + size].reshape(s))
        off += size
    return out


def modulate_fwd(x, g, sh, sc, name):
    def fn(x, g, sh, sc):
        return x * _rstd(x) * g * (1.0 + sc) + sh

    return rowwise(fn, [x], [g, sh, sc], [(x.shape[1], BF16)], [], 256, name)[0]


def norm_rope_fwd(p, width, cb, w, cosf, sinf, hd, name):
    nh = width // hd

    def fn(x, cosf, sinf, w):
        outs = []
        for h in range(nh):
            xh = x[:, h * hd:(h + 1) * hd]
            y = xh * _rstd(xh) * w
            outs.append(y * cosf + pltpu.roll(y, hd // 2, 1) * sinf)
        return jnp.concatenate(outs, axis=1) if nh > 1 else outs[0]

    return rowwise(fn, [(p, width, cb), cosf, sinf], [w], [(width, BF16)], [], 256, name)[0]


def norm_rope_bwd(p, width, cb, d, w, cosf, sinf, hd, name):
    nh = width // hd

    def fn(x, d, cosf, sinf, w):
        outs = []
        dw = jnp.zeros((1, hd), F32)
        for h in range(nh):
            xh = x[:, h * hd:(h + 1) * hd]
            dh = d[:, h * hd:(h + 1) * hd].astype(F32)
            r = _rstd(xh)
            n = xh * r
            dy = dh * cosf + pltpu.roll(dh * sinf, hd // 2, 1)
            dw = dw + jnp.sum(dy * n, axis=0, keepdims=True)
            dn = dy * w
            outs.append(r * (dn - n * jnp.mean(dn * n, axis=-1, keepdims=True)))
        return (jnp.concatenate(outs, axis=1) if nh > 1 else outs[0]), dw

    return rowwise(fn, [(p, width, cb), d, cosf, sinf], [w], [(width, BF16)], [(1, hd)], 256, name)


def attention_fwd(qr, kr, pkv, kcr, pkv_c, sink, hkv, hd, name):
    T, L = qr.shape[0], kcr.shape[0]
    G = qr.shape[1] // (hkv * hd)
    nb = T // BLOCK
    scale = hd ** -0.5

    def body(q_ref, kp, kc, kn, vp, vc, vn, ck_ref, cv_ref, sink_ref, o_ref, lse_ref, lser_ref):
        i = pl.program_id(1)
        kwin = jnp.concatenate([kp[...], kc[...], kn[...]], axis=0)
        vwin = jnp.concatenate([vp[...], vc[...], vn[...]], axis=0).astype(BF16)
        ck, cv = ck_ref[...], cv_ref[...].astype(BF16)
        row = lax.broadcasted_iota(jnp.int32, (BLOCK, 3 * BLOCK), 0)
        col = lax.broadcasted_iota(jnp.int32, (BLOCK, 3 * BLOCK), 1)
        rel = col - BLOCK - row
        valid = (jnp.abs(rel) <= WINDOW) & ((col >= BLOCK) | (i > 0)) & ((col < 2 * BLOCK) | (i < nb - 1))
        R = range(G)
        qa = q_ref[...]
        qs = [qa[:, g * hd:(g + 1) * hd] for g in R]
        sks = [sink_ref[g] for g in R]
        ss = [jnp.where(valid, _dot(qs[g], kwin, NT) * scale, NEG) for g in R]
        scs = [_dot(qs[g], ck, NT) * scale for g in R]
        ms = [jnp.maximum(jnp.maximum(jnp.max(ss[g], axis=1, keepdims=True), jnp.max(scs[g], axis=1, keepdims=True)),
                          sks[g]) for g in R]
        ps = [jnp.exp(ss[g] - ms[g]) for g in R]
        pcs = [jnp.exp(scs[g] - ms[g]) for g in R]
        nums = [_dot(ps[g].astype(BF16), vwin, NN) + _dot(pcs[g].astype(BF16), cv, NN) for g in R]
        dens = [jnp.exp(sks[g] - ms[g]) + jnp.sum(ps[g], axis=1, keepdims=True) + jnp.sum(pcs[g], axis=1, keepdims=True)
                for g in R]
        o_ref[...] = jnp.concatenate([(nums[g] / dens[g]).astype(o_ref.dtype) for g in R], axis=1)
        eye = (lax.broadcasted_iota(jnp.int32, (BLOCK, BLOCK), 0)
               == lax.broadcasted_iota(jnp.int32, (BLOCK, BLOCK), 1)).astype(F32)
        for g in R:
            lg = ms[g] + jnp.log(dens[g])
            lse_ref[g] = lg
            lser_ref[g] = jnp.sum(lg * eye, axis=0, keepdims=True)

    kv_specs = [
        pl.BlockSpec((BLOCK, hd), lambda h, i: (jnp.maximum(i - 1, 0), h)),
        pl.BlockSpec((BLOCK, hd), lambda h, i: (i, h)),
        pl.BlockSpec((BLOCK, hd), lambda h, i: (jnp.minimum(i + 1, nb - 1), h)),
    ]
    v_specs = [
        pl.BlockSpec((BLOCK, hd), lambda h, i: (jnp.maximum(i - 1, 0), hkv + h)),
        pl.BlockSpec((BLOCK, hd), lambda h, i: (i, hkv + h)),
        pl.BlockSpec((BLOCK, hd), lambda h, i: (jnp.minimum(i + 1, nb - 1), hkv + h)),
    ]
    return pl.pallas_call(
        body,
        grid=(hkv, nb),
        in_specs=[pl.BlockSpec((BLOCK, G * hd), lambda h, i: (i, h))] + kv_specs + v_specs + [
            pl.BlockSpec((L, hd), lambda h, i: (0, h)),
            pl.BlockSpec((L, hd), lambda h, i: (0, hkv + h)),
            pl.BlockSpec((G, 1, 1), lambda h, i: (h, 0, 0)),
        ],
        out_specs=[
            pl.BlockSpec((BLOCK, G * hd), lambda h, i: (i, h)),
            pl.BlockSpec((G, BLOCK, 1), lambda h, i: (h, i, 0)),
            pl.BlockSpec((G, 1, BLOCK), lambda h, i: (h, 0, i)),
        ],
        out_shape=[jax.ShapeDtypeStruct(qr.shape, BF16), jax.ShapeDtypeStruct((hkv * G, T, 1), F32),
                   jax.ShapeDtypeStruct((hkv * G, 1, T), F32)],
        compiler_params=_cp(("parallel", "parallel")),
        name=name,
    )(qr, kr, kr, kr, pkv, pkv, pkv, kcr, pkv_c, sink)


def attention_bwd_q(qr, kr, pkv, kcr, pkv_c, sink, do, o, lse, hkv, hd, name):
    T, L = qr.shape[0], kcr.shape[0]
    G = qr.shape[1] // (hkv * hd)
    nb = T // BLOCK
    scale = hd ** -0.5

    def body(q_ref, kp, kc, kn, vp, vc, vn, ck_ref, cv_ref, sink_ref, do_ref, o_ref, lse_ref,
             dq_ref, dck_ref, dcv_ref, dsink_ref, drr_ref):
        i = pl.program_id(1)

        @pl.when(i == 0)
        def _():
            dck_ref[...] = jnp.zeros_like(dck_ref)
            dcv_ref[...] = jnp.zeros_like(dcv_ref)
            dsink_ref[...] = jnp.zeros_like(dsink_ref)

        kwin = jnp.concatenate([kp[...], kc[...], kn[...]], axis=0)
        vwin = jnp.concatenate([vp[...], vc[...], vn[...]], axis=0).astype(BF16)
        ck, cv = ck_ref[...], cv_ref[...].astype(BF16)
        row = lax.broadcasted_iota(jnp.int32, (BLOCK, 3 * BLOCK), 0)
        col = lax.broadcasted_iota(jnp.int32, (BLOCK, 3 * BLOCK), 1)
        rel = col - BLOCK - row
        valid = (jnp.abs(rel) <= WINDOW) & ((col >= BLOCK) | (i > 0)) & ((col < 2 * BLOCK) | (i < nb - 1))
        R = range(G)
        qa, doa, oa = q_ref[...], do_ref[...], o_ref[...]
        qs = [qa[:, g * hd:(g + 1) * hd] for g in R]
        dos = [doa[:, g * hd:(g + 1) * hd] for g in R]
        lgs = [lse_ref[g] for g in R]
        sks = [sink_ref[g] for g in R]
        ss = [jnp.where(valid, _dot(qs[g], kwin, NT) * scale, NEG) for g in R]
        scs = [_dot(qs[g], ck, NT) * scale for g in R]
        dps = [_dot(dos[g], vwin, NT) for g in R]
        dpcs = [_dot(dos[g], cv, NT) for g in R]
        drs = [jnp.sum(dos[g].astype(F32) * oa[:, g * hd:(g + 1) * hd].astype(F32), axis=1, keepdims=True) for g in R]
        ps = [jnp.exp(ss[g] - lgs[g]) for g in R]
        pcs = [jnp.exp(scs[g] - lgs[g]) for g in R]
        dss = [(ps[g] * (dps[g] - drs[g]) * scale).astype(BF16) for g in R]
        dscs = [(pcs[g] * (dpcs[g] - drs[g]) * scale).astype(BF16) for g in R]
        dqs = [_dot(dss[g], kwin, NN) + _dot(dscs[g], ck, NN) for g in R]
        dcks = [_dot(dscs[g], qs[g], TN) for g in R]
        dcvs = [_dot(pcs[g].astype(BF16), dos[g], TN) for g in R]
        dq_ref[...] = jnp.concatenate(dqs, axis=1)
        dck_ref[...] += (dcks[0] + dcks[1]) + (dcks[2] + dcks[3]) if G == 4 else sum(dcks[1:], dcks[0])
        dcv_ref[...] += (dcvs[0] + dcvs[1]) + (dcvs[2] + dcvs[3]) if G == 4 else sum(dcvs[1:], dcvs[0])
        eye = (lax.broadcasted_iota(jnp.int32, (BLOCK, BLOCK), 0)
               == lax.broadcasted_iota(jnp.int32, (BLOCK, BLOCK), 1)).astype(F32)
        for g in R:
            dsink_ref[g] += -jnp.sum(jnp.exp(sks[g] - lgs[g]) * drs[g], axis=0, keepdims=True)
            drr_ref[g] = jnp.sum(drs[g] * eye, axis=0, keepdims=True)

    kv_specs = [
        pl.BlockSpec((BLOCK, hd), lambda h, i: (jnp.maximum(i - 1, 0), h)),
        pl.BlockSpec((BLOCK, hd), lambda h, i: (i, h)),
        pl.BlockSpec((BLOCK, hd), lambda h, i: (jnp.minimum(i + 1, nb - 1), h)),
    ]
    v_specs = [
        pl.BlockSpec((BLOCK, hd), lambda h, i: (jnp.maximum(i - 1, 0), hkv + h)),
        pl.BlockSpec((BLOCK, hd), lambda h, i: (i, hkv + h)),
        pl.BlockSpec((BLOCK, hd), lambda h, i: (jnp.minimum(i + 1, nb - 1), hkv + h)),
    ]
    qspec = pl.BlockSpec((BLOCK, G * hd), lambda h, i: (i, h))
    return pl.pallas_call(
        body,
        grid=(hkv, nb),
        in_specs=[qspec] + kv_specs + v_specs + [
            pl.BlockSpec((L, hd), lambda h, i: (0, h)),
            pl.BlockSpec((L, hd), lambda h, i: (0, hkv + h)),
            pl.BlockSpec((G, 1, 1), lambda h, i: (h, 0, 0)),
            qspec, qspec,
            pl.BlockSpec((G, BLOCK, 1), lambda h, i: (h, i, 0)),
        ],
        out_specs=[
            qspec,
            pl.BlockSpec((L, hd), lambda h, i: (0, h)),
            pl.BlockSpec((L, hd), lambda h, i: (0, h)),
            pl.BlockSpec((G, 1, 1), lambda h, i: (h, 0, 0)),
            pl.BlockSpec((G, 1, BLOCK), lambda h, i: (h, 0, i)),
        ],
        out_shape=[
            jax.ShapeDtypeStruct(qr.shape, F32),
            jax.ShapeDtypeStruct((L, hkv * hd), F32),
            jax.ShapeDtypeStruct((L, hkv * hd), F32),
            jax.ShapeDtypeStruct((hkv * G, 1, 1), F32),
            jax.ShapeDtypeStruct((hkv * G, 1, T), F32),
        ],
        compiler_params=_cp(("parallel", "arbitrary")),
        name=name,
    )(qr, kr, kr, kr, pkv, pkv, pkv, kcr, pkv_c, sink, do, o, lse)


def attention_bwd_kv(qr, kr, pkv, do, lse_row, dr_row, hkv, hd, name):
    T = qr.shape[0]
    G = qr.shape[1] // (hkv * hd)
    nb = T // BLOCK
    scale = hd ** -0.5

    def body(k_ref, v_ref, *refs):
        qs, dos, lses, drs = refs[0:3], refs[3:6], refs[6:9], refs[9:12]
        dk_ref, dv_ref = refs[12:]
        j = pl.program_id(1)
        k = k_ref[...]
        v = v_ref[...].astype(BF16)
        row = lax.broadcasted_iota(jnp.int32, (BLOCK, BLOCK), 0)
        col = lax.broadcasted_iota(jnp.int32, (BLOCK, BLOCK), 1)
        bias = []
        for d in range(3):
            iq = j + d - 1
            rel = row - col - (d - 1) * BLOCK
            valid = (jnp.abs(rel) <= WINDOW) & (iq >= 0) & (iq < nb)
            bias += [jnp.where(valid, 0.0, NEG)] * G
        bias = jnp.concatenate(bias, axis=1)

        def stack(refs):
            vals = [r[...] for r in refs]
            return jnp.concatenate([a[:, g * hd:(g + 1) * hd] for a in vals for g in range(G)], axis=0)

        q, dob = stack(qs), stack(dos)
        lrow = jnp.concatenate([r[g] for r in lses for g in range(G)], axis=1)
        drow = jnp.concatenate([r[g] for r in drs for g in range(G)], axis=1)
        st = _dot(k, q, NT) * scale + bias
        pt = jnp.exp(st - lrow)
        dpt = _dot(v, dob, NT)
        dst = (pt * (dpt - drow) * scale).astype(BF16)
        dk_ref[...] = _dot(dst, q, NN).astype(dk_ref.dtype)
        dv_ref[...] = _dot(pt.astype(BF16), dob, NN).astype(dv_ref.dtype)

    def q3(width_block):
        return [
            pl.BlockSpec(width_block, lambda h, j: (jnp.maximum(j - 1, 0), h)),
            pl.BlockSpec(width_block, lambda h, j: (j, h)),
            pl.BlockSpec(width_block, lambda h, j: (jnp.minimum(j + 1, nb - 1), h)),
        ]

    row3 = [
        pl.BlockSpec((G, 1, BLOCK), lambda h, j: (h, 0, jnp.maximum(j - 1, 0))),
        pl.BlockSpec((G, 1, BLOCK), lambda h, j: (h, 0, j)),
        pl.BlockSpec((G, 1, BLOCK), lambda h, j: (h, 0, jnp.minimum(j + 1, nb - 1))),
    ]
    qb = (BLOCK, G * hd)
    return pl.pallas_call(
        body,
        grid=(hkv, nb),
        in_specs=[pl.BlockSpec((BLOCK, hd), lambda h, j: (j, h)), pl.BlockSpec((BLOCK, hd), lambda h, j: (j, hkv + h))]
        + q3(qb) + q3(qb) + row3 + row3,
        out_specs=[pl.BlockSpec((BLOCK, hd), lambda h, j: (j, h))] * 2,
        out_shape=[jax.ShapeDtypeStruct((T, hkv * hd), BF16)] * 2,
        compiler_params=_cp(("parallel", "parallel")),
        name=name,
    )(kr, pkv, qr, qr, qr, do, do, do, lse_row, lse_row, lse_row, dr_row, dr_row, dr_row)


def gate_fwd(plr, wf, wb, bf, bb, name):
    n = wf.shape[1]

    def fn(lr, wf, wb, bf, bb):
        lrb = lr.astype(BF16)
        outs = []
        for w, b in ((wf, bf), (wb, bb)):
            z = _dot(lrb, w.astype(BF16), NN) + b
            outs.append((jnp.minimum(z, 0.0) - jnp.log(1.0 + jnp.exp(-jnp.abs(z)))) / GLA_GATE_NORM)
        return outs

    return rowwise(fn, [plr], [wf, wb, bf, bb], [(n, F32), (n, F32)], [], 256, name)


def gate_bwd(plr, dgf, dgb, wf, wb, bf, bb, name):
    n = wf.shape[1]

    def fn(lr, dgf, dgb, wf, wb, bf, bb):
        lrb = lr.astype(BF16)
        dlr = jnp.zeros(lr.shape, F32)
        res = []
        for w, b, dg in ((wf, bf, dgf), (wb, bb, dgb)):
            wb16 = w.astype(BF16)
            z = _dot(lrb, wb16, NN) + b
            dz = dg * _sig(-z) / GLA_GATE_NORM
            dzb = dz.astype(BF16)
            dlr = dlr + _dot(dzb, wb16, NT)
            res += [_dot(lrb, dzb, TN), jnp.sum(dz, axis=0, keepdims=True)]
        return [dlr] + res

    return rowwise(fn, [plr, dgf, dgb], [wf, wb, bf, bb], [(128, BF16)],
                   [(128, n), (1, n), (128, n), (1, n)], 256, name)


def _tri_dot(tri_b, x):
    x1 = x.astype(BF16)
    r1 = x - x1.astype(F32)
    x2 = r1.astype(BF16)
    x3 = (r1 - x2.astype(F32)).astype(BF16)
    return _dot(tri_b, x1, NN) + _dot(tri_b, x2, NN) + _dot(tri_b, x3, NN)


def gla_fwd(pqk, pv, gl, s0, heads, reverse, name, o_add=None):
    T = pqk.shape[0]
    dk = pqk.shape[1] // (2 * heads)
    dv = pv.shape[1] // heads
    C = GLA_CHUNK
    nc = T // C
    qscale = dk ** -0.5

    def body(*refs):
        if o_add is None:
            q_ref, k_ref, v_ref, g_ref, s0_ref, o_ref, st_ref, sf_ref, S = refs
            oa_ref = None
        else:
            q_ref, k_ref, v_ref, g_ref, s0_ref, oa_ref, o_ref, st_ref, sf_ref, S = refs
        n = pl.program_id(0)

        @pl.when(n == 0)
        def _():
            S[...] = s0_ref[...]

        r = lax.broadcasted_iota(jnp.int32, (C, C), 0)
        c = lax.broadcasted_iota(jnp.int32, (C, C), 1)
        tri = (r <= c) if reverse else (r >= c)
        trib = tri.astype(BF16)
        ga, qa, ka, va = g_ref[...], q_ref[...], k_ref[...], v_ref[...]
        sts = [S[h] for h in range(heads)]
        H = range(heads)
        gs = [ga[:, h * dk:(h + 1) * dk] for h in H]
        bs = [_tri_dot(trib, g) for g in gs]
        bls = [jnp.sum(g, axis=0, keepdims=True) for g in gs]
        mid = lax.broadcasted_iota(jnp.int32, (C, 1), 0) == C // 2
        bms = [jnp.sum(jnp.where(mid, b, 0.0), axis=0, keepdims=True) for b in bs]
        vs = [va[:, h * dv:(h + 1) * dv].astype(BF16) for h in H]
        qs = [qa[:, h * dk:(h + 1) * dk].astype(F32) * qscale for h in H]
        qes = [(qs[h] * jnp.exp(bs[h])).astype(BF16) for h in H]
        qms = [(qs[h] * jnp.exp(bs[h] - bms[h])).astype(BF16) for h in H]
        kms = [(ka[:, h * dk:(h + 1) * dk].astype(F32) * jnp.exp(bms[h] - bs[h])).astype(BF16) for h in H]
        kls = [(ka[:, h * dk:(h + 1) * dk].astype(F32) * jnp.exp(bls[h] - bs[h])).astype(BF16) for h in H]
        inter = [_dot(qes[h], sts[h].astype(BF16), NT) for h in H]
        upd = [_dot(vs[h], kls[h], TN) for h in H]
        As = [jnp.where(tri, _dot(qms[h], kms[h], NT), 0.0).astype(BF16) for h in H]
        outs = [inter[h] + _dot(As[h], vs[h], NN) for h in H]
        news = [sts[h] * jnp.exp(bls[h]) + upd[h] for h in H]
        o = jnp.concatenate(outs, axis=1)
        if oa_ref is not None:
            o = o + oa_ref[...]
        o_ref[...] = o
        for h in range(heads):
            st_ref[0, h] = sts[h]
            S[h] = news[h]

        @pl.when(n == nc - 1)
        def _():
            for h in range(heads):
                sf_ref[h] = news[h]

    def ci(n):
        return (nc - 1 - n) if reverse else n

    specs = [
        pl.BlockSpec((C, heads * dk), lambda n: (ci(n), 0)),
        pl.BlockSpec((C, heads * dk), lambda n: (ci(n), 1)),
        pl.BlockSpec((C, heads * dv), lambda n: (ci(n), 0)),
        pl.BlockSpec((C, heads * dk), lambda n: (ci(n), 0)),
        pl.BlockSpec((heads, dv, dk), lambda n: (0, 0, 0)),
    ]
    ins = [pqk, pqk, pv, gl, s0]
    if o_add is not None:
        specs.append(pl.BlockSpec((C, heads * dv), lambda n: (ci(n), 0)))
        ins.append(o_add)
    return pl.pallas_call(
        body,
        grid=(nc,),
        in_specs=specs,
        out_specs=[
            pl.BlockSpec((C, heads * dv), lambda n: (ci(n), 0)),
            pl.BlockSpec((1, heads, dv, dk), lambda n: (ci(n), 0, 0, 0)),
            pl.BlockSpec((heads, dv, dk), lambda n: (0, 0, 0)),
        ],
        out_shape=[
            jax.ShapeDtypeStruct((T, heads * dv), F32),
            jax.ShapeDtypeStruct((nc, heads, dv, dk), F32),
            jax.ShapeDtypeStruct((heads, dv, dk), F32),
        ],
        scratch_shapes=[pltpu.VMEM((heads, dv, dk), F32)],
        compiler_params=_cp(("arbitrary",)),
        name=name,
    )(*ins)


def gla_bwd(pqk, pv, gl, states, do, dsf, heads, reverse, name, acc=None):
    T = pqk.shape[0]
    dk = pqk.shape[1] // (2 * heads)
    dv = pv.shape[1] // heads
    C = GLA_CHUNK
    nc = T // C
    qscale = dk ** -0.5

    def body(*refs):
        if acc is None:
            q_ref, k_ref, v_ref, g_ref, st_ref, do_ref, dsf_ref, dq_ref, dk_ref, dv_ref, dg_ref, ds0_ref, dS = refs
            aq = ak = av = None
        else:
            (q_ref, k_ref, v_ref, g_ref, st_ref, do_ref, dsf_ref, aq, ak, av,
             dq_ref, dk_ref, dv_ref, dg_ref, ds0_ref, dS) = refs
        n = pl.program_id(0)

        @pl.when(n == 0)
        def _():
            dS[...] = dsf_ref[...]

        r = lax.broadcasted_iota(jnp.int32, (C, C), 0)
        c = lax.broadcasted_iota(jnp.int32, (C, C), 1)
        tri = (r <= c) if reverse else (r >= c)
        tri_t = (r >= c) if reverse else (r <= c)
        trib, tritb = tri.astype(BF16), tri_t.astype(BF16)
        ga, qa, ka, va, doa = g_ref[...], q_ref[...], k_ref[...], v_ref[...], do_ref[...]
        sts = [st_ref[0, h] for h in range(heads)]
        dsts = [dS[h] for h in range(heads)]
        H = range(heads)
        gs = [ga[:, h * dk:(h + 1) * dk] for h in H]
        bs = [_tri_dot(trib, g) for g in gs]
        bls = [jnp.sum(g, axis=0, keepdims=True) for g in gs]
        mid = lax.broadcasted_iota(jnp.int32, (C, 1), 0) == C // 2
        bms = [jnp.sum(jnp.where(mid, b, 0.0), axis=0, keepdims=True) for b in bs]
        ebs = [jnp.exp(b) for b in bs]
        embs = [jnp.exp(bs[h] - bms[h]) for h in H]
        enbs = [jnp.exp(bms[h] - bs[h]) for h in H]
        elbs = [jnp.exp(bls[h] - bs[h]) for h in H]
        ebls = [jnp.exp(bl) for bl in bls]
        vbs = [va[:, h * dv:(h + 1) * dv].astype(BF16) for h in H]
        dobs = [doa[:, h * dv:(h + 1) * dv].astype(BF16) for h in H]
        qs = [qa[:, h * dk:(h + 1) * dk].astype(F32) * qscale for h in H]
        qes = [qs[h] * ebs[h] for h in H]
        qms = [qs[h] * embs[h] for h in H]
        kms = [ka[:, h * dk:(h + 1) * dk].astype(F32) * enbs[h] for h in H]
        kls = [ka[:, h * dk:(h + 1) * dk].astype(F32) * elbs[h] for h in H]
        qebs = [a.astype(BF16) for a in qes]
        qmbs = [a.astype(BF16) for a in qms]
        kmbs = [a.astype(BF16) for a in kms]
        klbs = [a.astype(BF16) for a in kls]
        stbs = [a.astype(BF16) for a in sts]
        dstbs = [a.astype(BF16) for a in dsts]
        ps = [jnp.where(tri, _dot(qmbs[h], kmbs[h], NT), 0.0).astype(BF16) for h in H]
        dps = [jnp.where(tri, _dot(dobs[h], vbs[h], NT), 0.0).astype(BF16) for h in H]
        dqes = [_dot(dobs[h], stbs[h], NN) for h in H]
        dkls = [_dot(vbs[h], dstbs[h], NN) for h in H]
        dv1 = [_dot(klbs[h], dstbs[h], NT) for h in H]
        dsn1 = [_dot(dobs[h], qebs[h], TN) for h in H]
        dqms = [_dot(dps[h], kmbs[h], NN) for h in H]
        dkms = [_dot(dps[h], qmbs[h], TN) for h in H]
        dvs = [_dot(ps[h], dobs[h], TN) + dv1[h] for h in H]
        dbls = [ebls[h] * jnp.sum(dsts[h] * sts[h], axis=0, keepdims=True)
                + jnp.sum(dkls[h] * kls[h], axis=0, keepdims=True) for h in H]
        dsns = [dsn1[h] + dsts[h] * ebls[h] for h in H]
        dqs = [(dqes[h] * ebs[h] + dqms[h] * embs[h]) * qscale for h in H]
        dks = [dkms[h] * enbs[h] + dkls[h] * elbs[h] for h in H]
        dbs = [dqes[h] * qes[h] + dqms[h] * qms[h] - dkms[h] * kms[h] - dkls[h] * kls[h] for h in H]
        dgs = [_tri_dot(tritb, dbs[h]) + dbls[h] for h in H]
        dq, dkk, dvv = (jnp.concatenate(a, axis=1) for a in (dqs, dks, dvs))
        if aq is not None:
            dq = dq + aq[...].astype(F32)
            dkk = dkk + ak[...].astype(F32)
            dvv = dvv + av[...].astype(F32)
        dq_ref[...] = dq.astype(dq_ref.dtype)
        dk_ref[...] = dkk.astype(dk_ref.dtype)
        dv_ref[...] = dvv.astype(dv_ref.dtype)
        dg_ref[...] = jnp.concatenate(dgs, axis=1)
        for h in range(heads):
            dS[h] = dsns[h]

        @pl.when(n == nc - 1)
        def _():
            for h in range(heads):
                ds0_ref[h] = dsns[h]

    def ci(n):
        return n if reverse else (nc - 1 - n)

    kspec = pl.BlockSpec((C, heads * dk), lambda n: (ci(n), 0))
    vspec = pl.BlockSpec((C, heads * dv), lambda n: (ci(n), 0))
    sspec = pl.BlockSpec((heads, dv, dk), lambda n: (0, 0, 0))
    specs = [
        kspec,
        pl.BlockSpec((C, heads * dk), lambda n: (ci(n), 1)),
        vspec,
        kspec,
        pl.BlockSpec((1, heads, dv, dk), lambda n: (ci(n), 0, 0, 0)),
        vspec,
        sspec,
    ]
    ins = [pqk, pqk, pv, gl, states, do, dsf]
    odt = F32 if acc is None else BF16
    if acc is not None:
        specs += [kspec, kspec, vspec]
        ins += list(acc)
    return pl.pallas_call(
        body,
        grid=(nc,),
        in_specs=specs,
        out_specs=[kspec, kspec, vspec, kspec, sspec],
        out_shape=[
            jax.ShapeDtypeStruct((T, heads * dk), odt),
            jax.ShapeDtypeStruct((T, heads * dk), odt),
            jax.ShapeDtypeStruct((T, heads * dv), odt),
            jax.ShapeDtypeStruct((T, heads * dk), F32),
            jax.ShapeDtypeStruct((heads, dv, dk), F32),
        ],
        scratch_shapes=[pltpu.VMEM((heads, dv, dk), F32)],
        compiler_params=_cp(("arbitrary",)),
        name=name,
    )(*ins)


def gla_out_fwd(og, prb, gn, heads, name):
    dv = og.shape[1] // heads

    def fn(og, rb, gn):
        outs = []
        for h in range(heads):
            oh = og[:, h * dv:(h + 1) * dv]
            outs.append(oh * _rstd(oh) * gn)
        y = jnp.concatenate(outs, axis=1)
        return y * (rb * _sig(rb))

    return rowwise(fn, [og, prb], [gn], [(og.shape[1], BF16)], [], 256, name)[0]


def gla_out_bwd(og, prb, du, gn, heads, name):
    dv = og.shape[1] // heads

    def fn(og, rb, du, gn):
        sg = _sig(rb)
        silu = rb * sg
        dsilu = sg * (1.0 + rb * (1.0 - sg))
        dog, ys = [], []
        dgn = jnp.zeros((1, dv), F32)
        for h in range(heads):
            sl = slice(h * dv, (h + 1) * dv)
            oh = og[:, sl]
            r = _rstd(oh)
            n = oh * r
            ys.append(n * gn)
            dy = du[:, sl] * silu[:, sl]
            dgn = dgn + jnp.sum(dy * n, axis=0, keepdims=True)
            dn = dy * gn
            dog.append(r * (dn - n * jnp.mean(dn * n, axis=-1, keepdims=True)))
        y = jnp.concatenate(ys, axis=1)
        return jnp.concatenate(dog, axis=1), du * y * dsilu, dgn

    return rowwise(fn, [og, prb, du], [gn], [(og.shape[1], F32), (og.shape[1], BF16)], [(1, dv)], 128, name)


def conv_specs(T, tt, tc, off, order):
    r8 = tt // 8
    last8 = T // 8 - 1
    if order == "ij":
        return [
            pl.BlockSpec((tt, tc), lambda i, j: (i, j + off)),
            pl.BlockSpec((8, tc), lambda i, j: (jnp.maximum(i * r8 - 1, 0), j + off)),
            pl.BlockSpec((8, tc), lambda i, j: (jnp.minimum((i + 1) * r8, last8), j + off)),
        ]
    return [
        pl.BlockSpec((tt, tc), lambda j, i: (i, j + off)),
        pl.BlockSpec((8, tc), lambda j, i: (jnp.maximum(i * r8 - 1, 0), j + off)),
        pl.BlockSpec((8, tc), lambda j, i: (jnp.minimum((i + 1) * r8, last8), j + off)),
    ]


def _shifted(u, hp, hn, i, nt_):
    tt = u.shape[0]
    row = lax.broadcasted_iota(jnp.int32, u.shape, 0)
    r8 = lax.broadcasted_iota(jnp.int32, hp.shape, 0)
    prev = jnp.sum(jnp.where(r8 == 7, hp, 0.0), axis=0, keepdims=True) * (i > 0).astype(F32)
    nxt = jnp.sum(jnp.where(r8 == 0, hn, 0.0), axis=0, keepdims=True) * (i < nt_ - 1).astype(F32)
    down = jnp.where(row == 0, prev, pltpu.roll(u, 1, 0))
    up = jnp.where(row == tt - 1, nxt, pltpu.roll(u, tt - 1, 0))
    return down, up


def conv_swiglu_fwd(u, cw, cb, name):
    T, F2 = u.shape
    F = F2 // 2
    tt = min(512, T)
    tc = _pick(F, (512, 256, 128))
    nt_, ncol = T // tt, F // tc
    H = CONV_HALO
    n = tt + 2 * H

    def body(ua, uap, uan, ug, ugp, ugn, wa, wg, ba, bg, f_ref):
        i = pl.program_id(1)
        keep_p = (i > 0).astype(F32)
        keep_n = (i < nt_ - 1).astype(F32)
        res = []
        for m, p, nx, w, b in ((ua, uap, uan, wa, ba), (ug, ugp, ugn, wg, bg)):
            x = jnp.concatenate([p[...] * keep_p, m[...], nx[...] * keep_n], axis=0)
            down, up = pltpu.roll(x, 1, 0)[H:H + tt], pltpu.roll(x, n - 1, 0)[H:H + tt]
            res.append(w[0] * down + w[1] * x[H:H + tt] + w[2] * up + b[...])
        a, g = res
        f_ref[...] = (a * _sig_tanh(a) * g).astype(f_ref.dtype)

    wspec = lambda off: pl.BlockSpec((3, 1, tc), lambda j, i: (0, 0, j + off))
    bspec = lambda off: pl.BlockSpec((1, tc), lambda j, i: (0, j + off))
    return pl.pallas_call(
        body,
        grid=(ncol, nt_),
        in_specs=conv_halo_specs(T, tt, tc, 0) + conv_halo_specs(T, tt, tc, ncol)
        + [wspec(0), wspec(ncol), bspec(0), bspec(ncol)],
        out_specs=pl.BlockSpec((tt, tc), lambda j, i: (i, j)),
        out_shape=jax.ShapeDtypeStruct((T, F), BF16),
        compiler_params=_cp(("parallel", "parallel")),
        name=name,
    )(u, u, u, u, u, u, cw, cw, cb, cb)


def conv_swiglu_bwd(u, cw, cb, df, name):
    T, F2 = u.shape
    F = F2 // 2
    tt = min(256, T)
    tc = _pick(F, (512, 256, 128))
    nt_, ncol = T // tt, F // tc

    def body(ua, uap, uan, ug, ugp, ugn, wa, wg, ba, bg, df_ref, da_ref, dg_ref, dwa, dwg, dba, dbg):
        i = pl.program_id(1)

        @pl.when(i == 0)
        def _():
            for r in (dwa, dwg, dba, dbg):
                r[...] = jnp.zeros_like(r)

        sh = []
        res = []
        for um, up_, un, w, b in ((ua, uap, uan, wa, ba), (ug, ugp, ugn, wg, bg)):
            x = um[...]
            down, up = _shifted(x, up_[...], un[...], i, nt_)
            sh.append((down, x, up))
            res.append(w[0] * down + w[1] * x + w[2] * up + b[...])
        a, g = res
        d = df_ref[...].astype(F32)
        sg = _sig(a)
        da = d * g * sg * (1.0 + a * (1.0 - sg))
        dg = d * a * sg
        da_ref[...] = da
        dg_ref[...] = dg
        for dd, (down, x, up), dw, db in ((da, sh[0], dwa, dba), (dg, sh[1], dwg, dbg)):
            dw[0] += jnp.sum(dd * down, axis=0, keepdims=True)
            dw[1] += jnp.sum(dd * x, axis=0, keepdims=True)
            dw[2] += jnp.sum(dd * up, axis=0, keepdims=True)
            db[...] += jnp.sum(dd, axis=0, keepdims=True)

    wspec = lambda off: pl.BlockSpec((3, 1, tc), lambda j, i: (0, 0, j + off))
    bspec = lambda off: pl.BlockSpec((1, tc), lambda j, i: (0, j + off))
    tile = pl.BlockSpec((tt, tc), lambda j, i: (i, j))
    return pl.pallas_call(
        body,
        grid=(ncol, nt_),
        in_specs=conv_specs(T, tt, tc, 0, "ji") + conv_specs(T, tt, tc, ncol, "ji")
        + [wspec(0), wspec(ncol), bspec(0), bspec(ncol), tile],
        out_specs=[tile, tile, wspec(0), wspec(0), bspec(0), bspec(0)],
        out_shape=[
            jax.ShapeDtypeStruct((T, F), F32), jax.ShapeDtypeStruct((T, F), F32),
            jax.ShapeDtypeStruct((3, 1, F), F32), jax.ShapeDtypeStruct((3, 1, F), F32),
            jax.ShapeDtypeStruct((1, F), F32), jax.ShapeDtypeStruct((1, F), F32),
        ],
        compiler_params=_cp(("parallel", "arbitrary")),
        name=name,
    )(u, u, u, u, u, u, cw, cw, cb, cb, df)


CONV_HALO = 16


def conv_halo_specs(T, tt, tc, off):
    r = tt // CONV_HALO
    last = T // CONV_HALO - 1
    return [
        pl.BlockSpec((tt, tc), lambda j, i: (i, j + off)),
        pl.BlockSpec((CONV_HALO, tc), lambda j, i: (jnp.maximum(i * r - 1, 0), j + off)),
        pl.BlockSpec((CONV_HALO, tc), lambda j, i: (jnp.minimum((i + 1) * r, last), j + off)),
    ]


def conv_swiglu_bwd_fused(u, cw, cb, df, name):
    T, F2 = u.shape
    F = F2 // 2
    tt = min(512, T)
    tc = _pick(F, (512, 256, 128))
    nt_, ncol = T // tt, F // tc
    H = CONV_HALO
    n = tt + 2 * H

    def body(ua, uap, uan, ug, ugp, ugn, dm, dp_, dn, wa, wg, ba, bg, dua_ref, dug_ref, dwa, dwg, dba, dbg):
        i = pl.program_id(1)

        @pl.when(i == 0)
        def _():
            for r in (dwa, dwg, dba, dbg):
                r[...] = jnp.zeros_like(r)

        keep_p = (i > 0).astype(F32)
        keep_n = (i < nt_ - 1).astype(F32)

        def ext(m, p, nx):
            return jnp.concatenate([p[...].astype(F32) * keep_p, m[...].astype(F32), nx[...].astype(F32) * keep_n],
                                   axis=0)

        d = ext(dm, dp_, dn)
        conv, parts = [], []
        for m, p, nx, w, b in ((ua, uap, uan, wa, ba), (ug, ugp, ugn, wg, bg)):
            x = ext(m, p, nx)
            down, up = pltpu.roll(x, 1, 0), pltpu.roll(x, n - 1, 0)
            parts.append((down, x, up))
            conv.append(w[0] * down + w[1] * x + w[2] * up + b[...])
        a, g = conv
        sg = _sig_tanh(a)
        da = d * g * sg * (1.0 + a * (1.0 - sg))
        dg = d * a * sg
        for dd, w, (down, x, up), o_ref, dw, db in ((da, wa, parts[0], dua_ref, dwa, dba),
                                                    (dg, wg, parts[1], dug_ref, dwg, dbg)):
            du = w[0] * pltpu.roll(dd, n - 1, 0) + w[1] * dd + w[2] * pltpu.roll(dd, 1, 0)
            o_ref[...] = du[H:H + tt].astype(o_ref.dtype)
            ddm = dd[H:H + tt]
            dw[0] += jnp.sum(ddm * down[H:H + tt], axis=0, keepdims=True)
            dw[1] += jnp.sum(ddm * x[H:H + tt], axis=0, keepdims=True)
            dw[2] += jnp.sum(ddm * up[H:H + tt], axis=0, keepdims=True)
            db[...] += jnp.sum(ddm, axis=0, keepdims=True)

    wspec = lambda off: pl.BlockSpec((3, 1, tc), lambda j, i: (0, 0, j + off))
    bspec = lambda off: pl.BlockSpec((1, tc), lambda j, i: (0, j + off))
    tile = pl.BlockSpec((tt, tc), lambda j, i: (i, j))
    return pl.pallas_call(
        body,
        grid=(ncol, nt_),
        in_specs=conv_halo_specs(T, tt, tc, 0) + conv_halo_specs(T, tt, tc, ncol) + conv_halo_specs(T, tt, tc, 0)
        + [wspec(0), wspec(ncol), bspec(0), bspec(ncol)],
        out_specs=[tile, tile, wspec(0), wspec(0), bspec(0), bspec(0)],
        out_shape=[
            jax.ShapeDtypeStruct((T, F), BF16), jax.ShapeDtypeStruct((T, F), BF16),
            jax.ShapeDtypeStruct((3, 1, F), F32), jax.ShapeDtypeStruct((3, 1, F), F32),
            jax.ShapeDtypeStruct((1, F), F32), jax.ShapeDtypeStruct((1, F), F32),
        ],
        compiler_params=_cp(("parallel", "arbitrary")),
        name=name,
    )(u, u, u, u, u, u, df, df, df, cw, cw, cb, cb)


def conv_transpose(d, cw, off, name):
    T, F = d.shape
    tt = min(256, T)
    tc = _pick(F, (512, 256, 128))
    nt_, ncol = T // tt, F // tc
    offb = off // tc

    def body(dm, dp_, dn, w, o_ref):
        i = pl.program_id(0)
        x = dm[...]
        down, up = _shifted(x, dp_[...], dn[...], i, nt_)
        o_ref[...] = (w[0] * up + w[1] * x + w[2] * down).astype(o_ref.dtype)

    return pl.pallas_call(
        body,
        grid=(nt_, ncol),
        in_specs=conv_specs(T, tt, tc, 0, "ij") + [pl.BlockSpec((3, 1, tc), lambda i, j: (0, 0, j + offb))],
        out_specs=pl.BlockSpec((tt, tc), lambda i, j: (i, j)),
        out_shape=jax.ShapeDtypeStruct((T, F), BF16),
        compiler_params=_cp(("parallel", "parallel")),
        name=name,
    )(d, d, d, cw)


def rope_tables(n, hd):
    rows = n // GRID_W
    row = jnp.repeat(jnp.arange(rows), GRID_W)
    col = jnp.tile(jnp.arange(GRID_W), rows)
    n_freq = hd // 4
    inv = ROPE_THETA ** (-jnp.arange(n_freq, dtype=F32) / n_freq)
    ang = jnp.concatenate([row[:, None] * inv, col[:, None] * inv], axis=-1)
    cos, sin = jnp.cos(ang), jnp.sin(ang)
    return jnp.concatenate([cos, cos], axis=-1), jnp.concatenate([-sin, sin], axis=-1)


def local_step(x, ctx, tgt, mod, modc, W, P):
    T, D = x.shape
    L = ctx.shape[0]
    hd, hq, hkv, gh = P["hd"], P["hq"], P["hkv"], P["gh"]
    sh1, sc1, g1, sh2, sc2, g2 = mod
    csh1, csc1 = modc
    kvw = hkv * hd
    gkw = W["gqk"].shape[1] // 2
    gdv = D // gh
    gdk = gkw // gh

    h = modulate_fwd(x, P["g_mix"], sh1, sc1, "mod1")
    hc = modulate_fwd(ctx, P["g_mix"], csh1, csc1, "mod1_ctx")
    pq = matmul(h, W["q"], "nn", F32, "proj_q")
    pkv = matmul(h, W["kv"], "nn", F32, "proj_kv")
    pgqk = matmul(h, W["gqk"], "nn", F32, "proj_gqk")
    pgv = matmul(h, W["gv"], "nn", F32, "proj_gv")
    prb = matmul(h, W["rb"], "nn", F32, "proj_rb")
    plr = matmul(h, W["lr"], "nn", F32, "proj_lr")
    pgab = matmul(h, W["gab"], "nn", F32, "proj_gab")
    pkv_c = matmul(hc, W["kv"], "nn", F32, "proj_kv_ctx")
    pgqk_c = matmul(hc, W["gqk"], "nn", F32, "proj_gqk_ctx")
    pgv_c = matmul(hc, W["gv"], "nn", F32, "proj_gv_ctx")
    plr_c = matmul(hc, W["lr"], "nn", F32, "proj_lr_ctx")

    cosf, sinf = rope_tables(T, hd)
    one_c, zero_c = jnp.ones((L, hd), F32), jnp.zeros((L, hd), F32)
    qr = norm_rope_fwd(pq, hq * hd, 0, P["q_norm"], cosf, sinf, hd, "qnorm")
    kr = norm_rope_fwd(pkv, kvw, 0, P["k_norm"], cosf, sinf, hd, "knorm")
    kcr = norm_rope_fwd(pkv_c, kvw, 0, P["k_norm"], one_c, zero_c, hd, "knorm_ctx")
    sink = P["attn_sink"].reshape(hq, 1, 1)
    o_attn, lse, lse_row = attention_fwd(qr, kr, pkv, kcr, pkv_c, sink, hkv, hd, "attn_fwd")

    gf, gb = gate_fwd(plr, W["gate_f"], W["gate_b"], P["b_gate_f"], P["b_gate_b"], "gates")
    gfc, gbc = gate_fwd(plr_c, W["gate_f"], W["gate_b"], P["b_gate_f"], P["b_gate_b"], "gates_ctx")
    zero_state = jnp.zeros((gh, gdv, gdk), F32)
    _, st_cf, s_cf = gla_fwd(pgqk_c, pgv_c, gfc, zero_state, gh, False, "gla_ctx_f")
    _, st_cb, s_cb = gla_fwd(pgqk_c, pgv_c, gbc, zero_state, gh, True, "gla_ctx_b")
    of, st_f, _ = gla_fwd(pgqk, pgv, gf, s_cf, gh, False, "gla_f")
    og, st_b, _ = gla_fwd(pgqk, pgv, gb, s_cb, gh, True, "gla_b", o_add=of)
    ug = gla_out_fwd(og, prb, P["gla_norm"], gh, "gla_out")

    ya = matmul(o_attn, W["attn_o"], "nn", F32, "attn_o")
    yg = matmul(ug, W["gla_o"], "nn", F32, "gla_o")

    def merge_fn(ya, yg, ga, gb_):
        return _sig(ga) * ya + _sig(gb_) * yg

    z = rowwise(merge_fn, [ya, yg, (pgab, D, 0), (pgab, D, 1)], [], [(D, BF16)], [], 256, "merge")[0]
    mo = matmul(z, W["out"], "nn", F32, "w_out")

    def res_fn(x, mo, g1, gffn, sh2, sc2):
        x1 = x + g1 * mo
        return x1, x1 * _rstd(x1) * gffn * (1.0 + sc2) + sh2

    x1, h2 = rowwise(res_fn, [x, mo], [g1, P["g_ffn"], sh2, sc2], [(D, F32), (D, BF16)], [], 256, "res_mod2")
    u = matmul(h2, W["up"], "nn", F32, "w_up")
    cw3 = W["conv_w"].reshape(3, 1, -1)
    f = conv_swiglu_fwd(u, cw3, P["conv_b"], "conv_swiglu")
    fo = matmul(f, W["down"], "nn", F32, "w_down")

    def final_fn(x1, fo, tgt, g2):
        e = x1 + g2 * fo - tgt
        dy = e * (1.0 / D)
        lsum = jnp.sum(jnp.sum(e * e, axis=1, keepdims=True), axis=0, keepdims=True)
        return dy, dy * g2, jnp.broadcast_to(lsum, (1, 128)), jnp.sum(dy * fo, axis=0, keepdims=True)

    dy, dfo, lsum, dg2 = rowwise(final_fn, [x1, fo, tgt], [g2], [(D, F32), (D, BF16)], [(1, 128), (1, D)], 256, "loss")
    df = matmul(dfo, W["down"], "nt", BF16, "d_f")
    dw_down = matmul(f, dfo, "tn", BF16, "dw_down")
    du_a, du_g, dcw_a, dcw_g, dcb_a, dcb_g = conv_swiglu_bwd_fused(u, cw3, P["conv_b"], df, "conv_swiglu_bwd")
    Fh = du_a.shape[1]
    dh2 = matmul(du_a, W["up"][:, :Fh], "nt", F32, "d_h2_a")
    dh2 = matmul(du_g, W["up"][:, Fh:], "nt", F32, "d_h2_g", add=dh2)
    dw_up = [matmul(h2, du_a, "tn", BF16, "dw_up_a"), matmul(h2, du_g, "tn", BF16, "dw_up_g")]

    def mod2_bwd_fn(x1, dh, dy, mo, gffn, sc2, g1):
        r = _rstd(x1)
        n = x1 * r
        dyy = dh * (1.0 + sc2)
        dn = dyy * gffn
        dx1 = dy + r * (dn - n * jnp.mean(dn * n, axis=-1, keepdims=True))
        s0 = lambda a: jnp.sum(a, axis=0, keepdims=True)
        return dx1, dx1 * g1, s0(dyy * n), s0(dh), s0(dh * n * gffn), s0(dx1 * mo)

    dx1, dmo, dg_ffn, dsh2, dsc2, dg1 = rowwise(
        mod2_bwd_fn, [x1, dh2, dy, mo], [P["g_ffn"], sc2, g1], [(D, F32), (D, BF16)], [(1, D)] * 4, 128, "mod2_bwd")
    dz = matmul(dmo, W["out"], "nt", F32, "d_z")
    dw_out = matmul(z, dmo, "tn", BF16, "dw_out")

    def merge_bwd_fn(dz, ya, yg, ga, gb_):
        sa, sb = _sig(ga), _sig(gb_)
        return dz * sa, dz * sb, jnp.concatenate([dz * ya * sa * (1.0 - sa), dz * yg * sb * (1.0 - sb)], axis=1)

    dya, dyg, dpgab = rowwise(merge_bwd_fn, [dz, ya, yg, (pgab, D, 0), (pgab, D, 1)], [],
                              [(D, BF16), (D, BF16), (2 * D, BF16)], [], 128, "merge_bwd")
    do_attn = matmul(dya, W["attn_o"], "nt", BF16, "d_oattn")
    dw_attn_o = matmul(o_attn, dya, "tn", BF16, "dw_attn_o")
    dug = matmul(dyg, W["gla_o"], "nt", F32, "d_ug")
    dw_gla_o = matmul(ug, dyg, "tn", BF16, "dw_gla_o")
    dog, dprb, dgn = gla_out_bwd(og, prb, dug, P["gla_norm"], gh, "gla_out_bwd")

    dq1, dk1, dv1, dgf, ds_cf = gla_bwd(pgqk, pgv, gf, st_f, dog, zero_state, gh, False, "gla_f_bwd")
    dgq, dgk, dpgv, dgb, ds_cb = gla_bwd(pgqk, pgv, gb, st_b, dog, zero_state, gh, True, "gla_b_bwd",
                                          acc=(dq1, dk1, dv1))
    dpgqk = jnp.concatenate([dgq, dgk], axis=1)
    zero_do = jnp.zeros((L, gh * gdv), F32)
    cq1, ck1, cv1, dgfc, _ = gla_bwd(pgqk_c, pgv_c, gfc, st_cf, zero_do, ds_cf, gh, False, "gla_ctx_f_bwd")
    cq, ck, dpgv_c, dgbc, _ = gla_bwd(pgqk_c, pgv_c, gbc, st_cb, zero_do, ds_cb, gh, True, "gla_ctx_b_bwd",
                                      acc=(cq1, ck1, cv1))
    dpgqk_c = jnp.concatenate([cq, ck], axis=1)
    dplr, dwgf, dbgf, dwgb, dbgb = gate_bwd(plr, dgf, dgb, W["gate_f"], W["gate_b"], P["b_gate_f"], P["b_gate_b"], "gates_bwd")
    dplr_c, dwgf_c, dbgf_c, dwgb_c, dbgb_c = gate_bwd(plr_c, dgfc, dgbc, W["gate_f"], W["gate_b"], P["b_gate_f"],
                                                      P["b_gate_b"], "gates_ctx_bwd")

    dqr, dkc_r, dvc, dsink, dr_row = attention_bwd_q(qr, kr, pkv, kcr, pkv_c, sink, do_attn, o_attn, lse, hkv, hd,
                                                     "attn_bwd_q")
    dkr, dv = attention_bwd_kv(qr, kr, pkv, do_attn, lse_row, dr_row, hkv, hd, "attn_bwd_kv")
    dpq, dqn = norm_rope_bwd(pq, hq * hd, 0, dqr, P["q_norm"], cosf, sinf, hd, "qnorm_bwd")
    dpk, dkn = norm_rope_bwd(pkv, kvw, 0, dkr, P["k_norm"], cosf, sinf, hd, "knorm_bwd")
    dpk_c, dkn_c = norm_rope_bwd(pkv_c, kvw, 0, dkc_r, P["k_norm"], one_c, zero_c, hd, "knorm_ctx_bwd")
    dpkv = jnp.concatenate([dpk, dv], axis=1)
    dpkv_c = jnp.concatenate([dpk_c, dvc.astype(BF16)], axis=1)

    dw_q = matmul(h, dpq, "tn", BF16, "dw_q")
    dw_kv = matmul(h, dpkv, "tn", BF16, "dw_kv", add=matmul(hc, dpkv_c, "tn", F32, "dw_kv_ctx"))
    dw_gqk = matmul(h, dpgqk, "tn", BF16, "dw_gqk", add=matmul(hc, dpgqk_c, "tn", F32, "dw_gqk_ctx"))
    dw_gv = matmul(h, dpgv, "tn", BF16, "dw_gv", add=matmul(hc, dpgv_c, "tn", F32, "dw_gv_ctx"))
    dw_rb = matmul(h, dprb, "tn", BF16, "dw_rb")
    dw_lr = matmul(h, dplr, "tn", BF16, "dw_lr", add=matmul(hc, dplr_c, "tn", F32, "dw_lr_ctx"))
    dw_gab = matmul(h, dpgab, "tn", BF16, "dw_gab")
    lrw = P["lowrank"]
    dw_in = [dw_q, dw_kv, dw_gqk, dw_gv, dw_rb, dw_lr[:, :2 * lrw], dw_gab]

    dh = matmul(dpq, W["q"], "nt", F32, "dh_q")
    dh = matmul(dpkv, W["kv"], "nt", F32, "dh_kv", add=dh)
    dh = matmul(dpgqk, W["gqk"], "nt", F32, "dh_gqk", add=dh)
    dh = matmul(dpgv, W["gv"], "nt", F32, "dh_gv", add=dh)
    dh = matmul(dprb, W["rb"], "nt", F32, "dh_rb", add=dh)
    dh = matmul(dplr, W["lr"], "nt", F32, "dh_lr", add=dh)
    dh = matmul(dpgab, W["gab"], "nt", F32, "dh_gab", add=dh)
    dhc = matmul(dpkv_c, W["kv"], "nt", F32, "dhc_kv")
    dhc = matmul(dpgqk_c, W["gqk"], "nt", F32, "dhc_gqk", add=dhc)
    dhc = matmul(dpgv_c, W["gv"], "nt", F32, "dhc_gv", add=dhc)
    dhc = matmul(dplr_c, W["lr"], "nt", F32, "dhc_lr", add=dhc)

    def mod1_bwd_fn(x, dh, dres, g, sc):
        r = _rstd(x)
        n = x * r
        dyy = dh * (1.0 + sc)
        dn = dyy * g
        dx = dres + r * (dn - n * jnp.mean(dn * n, axis=-1, keepdims=True))
        s0 = lambda a: jnp.sum(a, axis=0, keepdims=True)
        return dx, s0(dyy * n), s0(dh), s0(dh * n * g)

    grad_x, dgmix, dsh1, dsc1 = rowwise(mod1_bwd_fn, [x, dh, dx1], [P["g_mix"], sc1], [(D, F32)], [(1, D)] * 3,
                                        128, "mod1_bwd")
    _, dgmix_c, dcsh1, dcsc1 = rowwise(mod1_bwd_fn, [ctx, dhc, jnp.zeros_like(ctx)], [P["g_mix"], csc1], [(D, F32)],
                                       [(1, D)] * 3, 128, "mod1_ctx_bwd")

    zD = jnp.zeros((1, D), F32)
    grads = dict(
        w_in=dw_in, w_attn_o=dw_attn_o, w_gla_o=dw_gla_o, w_out=dw_out, w_up=dw_up, w_down=dw_down,
        dmod_x=jnp.concatenate([dsh1, dsc1, dg1, dsh2, dsc2, dg2], axis=1),
        dmod_c=jnp.concatenate([dcsh1, dcsc1, zD, zD, zD, zD], axis=1),
        g_mix=dgmix + dgmix_c, q_norm=dqn, k_norm=dkn + dkn_c, attn_sink=dsink.reshape(1, hq),
        w_gate_f=(dwgf + dwgf_c)[:lrw], b_gate_f=dbgf + dbgf_c,
        w_gate_b=(dwgb + dwgb_c)[lrw:2 * lrw], b_gate_b=dbgb + dbgb_c,
        gla_norm=dgn, g_ffn=dg_ffn,
        conv_w=jnp.concatenate([dcw_a, dcw_g], axis=2).reshape(3, -1),
        conv_b=jnp.concatenate([dcb_a, dcb_g], axis=1),
    )
    return lsum[0, 0], grad_x, grads


SMALL_REPL = ("c_ctx", "b_mod", "g_mix", "q_norm", "k_norm", "attn_sink", "b_gate_f", "b_gate_b", "gla_norm", "g_ffn",
              "conv_b")
SMALL_SHARD = ("w_gate_f", "w_gate_b", "conv_w")
ORDER = ("c_ctx", "w_mod", "b_mod", "g_mix", "w_in", "q_norm", "k_norm", "attn_sink", "w_gate_f", "b_gate_f",
         "w_gate_b", "b_gate_b", "gla_norm", "w_attn_o", "w_gla_o", "w_out", "g_ffn", "w_up", "conv_w", "conv_b",
         "w_down")


def kernel(x, c, ctx, c_ctx, w_mod, b_mod, g_mix, w_in, q_norm, k_norm, attn_sink, w_gate_f, b_gate_f, w_gate_b, b_gate_b, gla_norm, w_attn_o, w_gla_o, w_out, g_ffn, w_up, conv_w, conv_b, w_down, loss_target, m_c_ctx, m_w_mod, m_b_mod, m_g_mix, m_w_in, m_q_norm, m_k_norm, m_attn_sink, m_w_gate_f, m_b_gate_f, m_w_gate_b, m_b_gate_b, m_gla_norm, m_w_attn_o, m_w_gla_o, m_w_out, m_g_ffn, m_w_up, m_conv_w, m_conv_b, m_w_down, v_c_ctx, v_w_mod, v_b_mod, v_g_mix, v_w_in, v_q_norm, v_k_norm, v_attn_sink, v_w_gate_f, v_b_gate_f, v_w_gate_b, v_b_gate_b, v_gla_norm, v_w_attn_o, v_w_gla_o, v_w_out, v_g_ffn, v_w_up, v_conv_w, v_conv_b, v_w_down):
    loc = dict(locals())
    Wt = {n: loc[n] for n in ORDER}
    Mt = {n: loc["m_" + n] for n in ORDER}
    Vt = {n: loc["v_" + n] for n in ORDER}
    me = 4 * lax.axis_index("x") + 2 * lax.axis_index("y") + lax.axis_index("c")

    D = x.shape[-1]
    hd = q_norm.shape[-1]
    hq = attn_sink.shape[-1]
    gdv = gla_norm.shape[-1]
    gh = D // gdv
    gdk = D // 2 // gh
    lrw = w_gate_f.shape[1]
    in_w = NDEV * w_in.shape[-1]
    kvw = (in_w - hq * hd - 2 * gh * gdk - 2 * gh * gdv - 2 * lrw - 2 * D) // 2
    hkv = kvw // hd
    gcols = w_gate_f.shape[-1]
    F2 = NDEV * w_up.shape[-1]
    mcols = w_mod.shape[-1]

    x2, ctx2, tgt2 = x[0], ctx[0], loss_target[0]

    c_all = exchange([jnp.pad(c, ((0, 7), (0, 0)))], True, "gather_c")[0][:, 0, :]
    c9 = jnp.concatenate([c_all, c_ctx[None, :], jnp.zeros((7, D), F32)], axis=0)
    s9 = rowwise(lambda a: a * _sig(a), [c9], [], [(D, F32)], [], 16, "silu_c")[0]
    bias = jnp.broadcast_to(lax.dynamic_slice_in_dim(b_mod, me * mcols, mcols, axis=1), (16, mcols))
    mod_cols = matmul(s9, w_mod[0], "nn", F32, "mod_cols", add=bias)
    mod_all = exchange([mod_cols], True, "gather_mod")[0]
    mod_all = jnp.transpose(mod_all, (1, 0, 2)).reshape(16, NDEV * mcols)
    mod_me = lax.dynamic_slice_in_dim(mod_all, me, 1, axis=0)
    mod = [mod_me[:, i * D:(i + 1) * D] for i in range(6)]
    modc = [mod_all[8:9, i * D:(i + 1) * D] for i in range(2)]

    o3 = jnp.stack([w_attn_o[0], w_gla_o[0], w_out[0]]).astype(BF16)
    small_w = pack([w_gate_f[0], w_gate_b[0], conv_w[0]])
    g_in, g_o3, g_up, g_down, g_small = gather_two_level(
        [w_in[0].astype(BF16), o3, w_up[0].astype(BF16), w_down[0].astype(BF16), small_w], "gather_w")
    seg = segments_from_blocks(g_in, [hq * hd, 2 * kvw, 2 * gh * gdk, gh * gdv, gh * gdv, 2 * lrw, 2 * D])
    small_parts = [unpack(g_small[j], [w_gate_f[0].shape, w_gate_b[0].shape, conv_w[0].shape]) for j in range(NDEV)]
    wgf = jnp.concatenate([p[0] for p in small_parts], axis=1)
    wgb = jnp.concatenate([p[1] for p in small_parts], axis=1)
    cw_full = jnp.concatenate([p[2] for p in small_parts], axis=1)
    o3f = [g_o3[:, i].reshape(-1, D) for i in range(3)]
    W = dict(
        q=seg[0], kv=seg[1], gqk=seg[2], gv=seg[3], rb=seg[4],
        lr=jnp.pad(seg[5], ((0, 0), (0, 128 - 2 * lrw))), gab=seg[6],
        gate_f=jnp.pad(wgf, ((0, 128 - lrw), (0, 0))),
        gate_b=jnp.pad(wgb, ((lrw, 128 - 2 * lrw), (0, 0))),
        attn_o=o3f[0], gla_o=o3f[1], out=o3f[2],
        up=jnp.concatenate([g_up[j] for j in range(NDEV)], axis=1),
        down=g_down.reshape(-1, D),
        conv_w=cw_full,
    )
    P = dict(hd=hd, hq=hq, hkv=hkv, gh=gh, lowrank=lrw, g_mix=g_mix, q_norm=q_norm, k_norm=k_norm, attn_sink=attn_sink,
             b_gate_f=b_gate_f, b_gate_b=b_gate_b, gla_norm=gla_norm, g_ffn=g_ffn, conv_b=conv_b)

    lsum, grad_x, G = local_step(x2, ctx2, tgt2, mod, modc, W, P)
    loss = lax.psum(0.5 * lsum / D, ("x", "y", "c"))

    dm = exchange([jnp.concatenate([G["dmod_x"], G["dmod_c"], jnp.zeros((6, 6 * D), F32)], axis=0)], True,
                  "gather_dmod")[0]
    dmc = reduce_parts(dm[:, 1:2, :].reshape(NDEV, 6 * D // 128, 128), "sum_dmod_ctx").reshape(1, 6 * D)
    dM = jnp.concatenate([dm[:, 0, :], dmc, jnp.zeros((7, 6 * D), F32)], axis=0)
    dM_cols = lax.dynamic_slice_in_dim(dM, me * mcols, mcols, axis=1)
    g_w_mod = matmul(s9, dM_cols, "tn", F32, "dw_mod")
    g_b_mod = reduce_parts(dM.reshape(16, 6 * D // 128, 128), "sum_db_mod").reshape(1, 6 * D)
    dsc = matmul(dM_cols[8:16], w_mod[0], "nt", F32, "d_silu_ctx")
    cc = jnp.broadcast_to(c_ctx[None, :], (8, D))

    def dsilu_fn(d, a):
        sg = _sig(a)
        return d * sg * (1.0 + a * (1.0 - sg))

    g_cctx_part = rowwise(dsilu_fn, [dsc, cc], [], [(D, F32)], [], 8, "d_c_ctx")[0][0:1]

    small_names = ("c_ctx", "g_mix", "q_norm", "k_norm", "attn_sink", "b_gate_f", "b_gate_b", "gla_norm", "g_ffn",
                   "conv_b", "w_gate_f", "w_gate_b", "conv_w")
    G["c_ctx"] = g_cctx_part
    sm_shapes = [G[n].shape for n in small_names]
    sm_all = exchange([pack([G[n] for n in small_names])], True, "gather_small_grads")[0]
    sm_tot = unpack(reduce_parts(sm_all, "sum_small_grads"), sm_shapes)
    gs = dict(zip(small_names, sm_tot))
    gs["b_mod"] = g_b_mod
    gs["w_gate_f"] = lax.dynamic_slice_in_dim(gs["w_gate_f"], me * gcols, gcols, axis=1)
    gs["w_gate_b"] = lax.dynamic_slice_in_dim(gs["w_gate_b"], me * gcols, gcols, axis=1)
    ccols = conv_w.shape[-1]
    gs["conv_w"] = lax.dynamic_slice_in_dim(gs["conv_w"], me * ccols, ccols, axis=1)

    orows = w_attn_o.shape[1]
    s_in = blocks_from_segments(G["w_in"], w_in.shape[-1])
    s_o3 = jnp.concatenate([rows_to_blocks(G["w_attn_o"]), rows_to_blocks(G["w_gla_o"]), rows_to_blocks(G["w_out"])],
                           axis=1)
    s_up = blocks_from_segments(G["w_up"], w_up.shape[-1])
    s_down = rows_to_blocks(G["w_down"])
    r_in, r_o3, r_up, r_down = scatter_reduce([s_in, s_o3, s_up, s_down], "scatter_grads")

    out = {}
    out["w_in"] = adam_reduce(r_in, w_in[0], m_w_in[0], v_w_in[0], "adam_w_in")
    o3w = jnp.concatenate([w_attn_o[0], w_gla_o[0], w_out[0]], axis=0)
    o3m = jnp.concatenate([m_w_attn_o[0], m_w_gla_o[0], m_w_out[0]], axis=0)
    o3v = jnp.concatenate([v_w_attn_o[0], v_w_gla_o[0], v_w_out[0]], axis=0)
    ro3 = adam_reduce(r_o3, o3w, o3m, o3v, "adam_o3")
    for i, n in enumerate(("w_attn_o", "w_gla_o", "w_out")):
        out[n] = [a[i * orows:(i + 1) * orows] for a in ro3]
    out["w_up"] = adam_reduce(r_up, w_up[0], m_w_up[0], v_w_up[0], "adam_w_up")
    out["w_down"] = adam_reduce(r_down, w_down[0], m_w_down[0], v_w_down[0], "adam_w_down")
    out["w_mod"] = adam_reduce(g_w_mod[None], w_mod[0], m_w_mod[0], v_w_mod[0], "adam_w_mod")
    sm_names = SMALL_REPL + SMALL_SHARD
    shapes = [Wt[n].shape for n in sm_names]
    rs = adam_reduce(pack([gs[n] for n in sm_names])[None], pack([Wt[n] for n in sm_names]),
                     pack([Mt[n] for n in sm_names]), pack([Vt[n] for n in sm_names]), "adam_small")
    rs = [unpack(a, shapes) for a in rs]
    for i, n in enumerate(sm_names):
        out[n] = [a[i] for a in rs]

    res = [loss, grad_x[None]]
    for k in range(4):
        for n in ORDER:
            res.append(out[n][k].reshape(Wt[n].shape))
    return tuple(res)
```

```python
import jax
import jax.numpy as jnp
import numpy as np
from jax import lax
from jax.experimental import pallas as pl
from jax.experimental.pallas import tpu as pltpu

F32 = jnp.float32
BF16 = jnp.bfloat16

NDEV = 8
NCHIP = 4
EPS = 1e-6
WINDOW = 128
BLOCK = 128
GRID_W = 64
ROPE_THETA = 10000.0
GLA_CHUNK = 128
GLA_GATE_NORM = 16.0
ADAM_LR = 0.001
ADAM_B1 = 0.9
ADAM_B2 = 0.999
ADAM_EPS = 1e-08
ADAM_WD = 0.01
ADAM_STEP = 10
V7X_VMEM_LIMIT = 56 * 1024 * 1024
MATMUL_VMEM_BUDGET = 40 * 1024 * 1024
NEG = -1e30

NN = ((1,), (0,))
NT = ((1,), (1,))
TN = ((0,), (0,))


def _dot(a, b, dims):
    return lax.dot_general(a, b, (dims, ((), ())), preferred_element_type=F32)


def _cp(sem):
    return pltpu.CompilerParams(dimension_semantics=sem, vmem_limit_bytes=V7X_VMEM_LIMIT)


def _pick(n, cands):
    for c in cands:
        if n % c == 0:
            return c
    return n


def _sig(x):
    return 1.0 / (1.0 + jnp.exp(-x))


def _sig_tanh(x):
    return 0.5 * jnp.tanh(0.5 * x) + 0.5


def _rstd(x):
    return lax.rsqrt(jnp.mean(x * x, axis=-1, keepdims=True) + EPS)


_ANY = pl.BlockSpec(memory_space=pl.ANY)


def _place():
    return lax.axis_index("x"), lax.axis_index("y"), lax.axis_index("c")


def exchange(srcs, bcast, name, group="all"):
    n = len(srcs)
    ndev = NDEV if group == "all" else NCHIP
    ks = tuple(range(1, NDEV)) if group == "all" else (2, 4, 6)
    out_shape = [jax.ShapeDtypeStruct((ndev,) + (s.shape if bcast else s.shape[1:]), s.dtype) for s in srcs]

    def body(*refs):
        src, dst = refs[:n], refs[n:2 * n]
        send_sems, recv_sems, loc_sems = refs[2 * n:]
        x, y, c = _place()

        def idx(px, py, pc):
            return 4 * px + 2 * py + pc if group == "all" else 2 * px + py

        me = idx(x, y, c)
        copies = []
        for a in range(n):
            cp = pltpu.make_async_copy(src[a] if bcast else src[a].at[me], dst[a].at[me], loc_sems.at[a])
            cp.start()
            copies.append(cp)
        for s, k in enumerate(ks):
            px, py, pc = x ^ ((k >> 2) & 1), y ^ ((k >> 1) & 1), c ^ (k & 1)
            for a in range(n):
                cp = pltpu.make_async_remote_copy(
                    src_ref=src[a] if bcast else src[a].at[idx(px, py, pc)],
                    dst_ref=dst[a].at[me],
                    send_sem=send_sems.at[a, s],
                    recv_sem=recv_sems.at[a, s],
                    device_id=(px, py, pc),
                    device_id_type=pl.DeviceIdType.MESH,
                )
                cp.start()
                copies.append(cp)
        for cp in copies:
            cp.wait()

    return pl.pallas_call(
        body,
        out_shape=out_shape,
        in_specs=[_ANY] * n,
        out_specs=[_ANY] * n,
        scratch_shapes=[
            pltpu.SemaphoreType.DMA((n, len(ks))),
            pltpu.SemaphoreType.DMA((n, len(ks))),
            pltpu.SemaphoreType.DMA((n,)),
        ],
        name=name,
    )(*srcs)


def gather_two_level(srcs, name):
    n = len(srcs)
    out_shape = [jax.ShapeDtypeStruct((NDEV,) + s.shape, s.dtype) for s in srcs]

    def body(*refs):
        src, dst = refs[:n], refs[n:2 * n]
        send_sems, recv_sems, loc_sems = refs[2 * n:]
        x, y, c = _place()
        me = 4 * x + 2 * y + c
        sib = (x, y, 1 - c)
        first = (x ^ (1 - c), y ^ c)
        second = (x ^ c, y ^ (1 - c))
        diag = (x ^ 1, y ^ 1)

        def row(chip, core):
            return 4 * chip[0] + 2 * chip[1] + core

        def copy(a, s, block, to, from_src=False):
            return pltpu.make_async_remote_copy(
                src_ref=src[a] if from_src else dst[a].at[block], dst_ref=dst[a].at[block],
                send_sem=send_sems.at[a, s], recv_sem=recv_sems.at[a, s],
                device_id=to, device_id_type=pl.DeviceIdType.MESH)

        local = [pltpu.make_async_copy(src[a], dst[a].at[me], loc_sems.at[a]) for a in range(n)]
        sent = [copy(a, 0, me, sib, True) for a in range(n)]
        sent += [copy(a, 1, me, (*first, c), True) for a in range(n)]
        sent += [copy(a, 2, me, (*second, c), True) for a in range(n)]
        for cp in local + sent:
            cp.start()
        for a in range(n):
            copy(a, 1, row(first, c), (*first, c)).wait_recv()
            for cp in (copy(a, 3, row(first, c), (*second, c)), copy(a, 5, row(first, c), sib)):
                cp.start()
                sent.append(cp)
        for a in range(n):
            copy(a, 2, row(second, c), (*second, c)).wait_recv()
            cp = copy(a, 4, row(second, c), sib)
            cp.start()
            sent.append(cp)
        for a in range(n):
            copy(a, 3, row(diag, c), (*second, c)).wait_recv()
            cp = copy(a, 6, row(diag, c), sib)
            cp.start()
            sent.append(cp)
        for a in range(n):
            copy(a, 0, row((x, y), 1 - c), sib).wait_recv()
            copy(a, 4, row(first, 1 - c), sib).wait_recv()
            copy(a, 5, row(second, 1 - c), sib).wait_recv()
            copy(a, 6, row(diag, 1 - c), sib).wait_recv()
        for cp in local:
            cp.wait()
        for cp in sent:
            cp.wait_send()

    return pl.pallas_call(
        body,
        out_shape=out_shape,
        in_specs=[_ANY] * n,
        out_specs=[_ANY] * n,
        scratch_shapes=[
            pltpu.SemaphoreType.DMA((n, NDEV - 1)),
            pltpu.SemaphoreType.DMA((n, NDEV - 1)),
            pltpu.SemaphoreType.DMA((n,)),
        ],
        name=name,
    )(*srcs)


def pair_swap(srcs, name):
    n = len(srcs)

    def body(*refs):
        src, dst = refs[:n], refs[n:2 * n]
        send_sems, recv_sems = refs[2 * n:]
        x, y, c = _place()
        copies = []
        for a in range(n):
            cp = pltpu.make_async_remote_copy(
                src_ref=src[a], dst_ref=dst[a], send_sem=send_sems.at[a], recv_sem=recv_sems.at[a],
                device_id=(x, y, 1 - c), device_id_type=pl.DeviceIdType.MESH)
            cp.start()
            copies.append(cp)
        for cp in copies:
            cp.wait()

    return pl.pallas_call(
        body,
        out_shape=[jax.ShapeDtypeStruct(s.shape, s.dtype) for s in srcs],
        in_specs=[_ANY] * n,
        out_specs=[_ANY] * n,
        scratch_shapes=[pltpu.SemaphoreType.DMA((n,)), pltpu.SemaphoreType.DMA((n,))],
        name=name,
    )(*srcs)


def scatter_reduce(blocks, name):
    c = lax.axis_index("c")
    halves = [b.reshape((NCHIP, 2) + b.shape[1:]) for b in blocks]
    mine = [lax.dynamic_index_in_dim(h, c, axis=1, keepdims=False) for h in halves]
    theirs = [lax.dynamic_index_in_dim(h, 1 - c, axis=1, keepdims=False) for h in halves]
    got = pair_swap(theirs, name + "_d2d")
    sums = []
    for i, (m, g) in enumerate(zip(mine, got)):
        flat = (NCHIP * m.shape[1], m.shape[2])
        s = rowwise(lambda a, b: a.astype(F32) + b.astype(F32), [m.reshape(flat), g.reshape(flat)], [],
                    [(flat[1], m.dtype)], [], _pick(flat[0], (512, 256, 128, 64)), f"{name}_pair_sum{i}")[0]
        sums.append(s.reshape(m.shape))
    return exchange(sums, False, name + "_ici", group="core")


def matmul(a, b, mode, out_dtype, name, add=None):
    if mode == "nn":
        (M, K), N = a.shape, b.shape[1]
    elif mode == "nt":
        (M, K), N = a.shape, b.shape[0]
    else:
        (K, M), N = a.shape, b.shape[1]
    tm = _pick(M, (1024, 512, 256, 128))
    tn = _pick(N, (1024, 512, 256, 128))
    osz = jnp.dtype(out_dtype).itemsize

    def vmem_bytes(tk):
        ops = 2 * tk * (tm * a.dtype.itemsize + tn * b.dtype.itemsize)
        return ops + tm * tn * (2 * osz + (4 if tk < K else 0) + (8 if add is not None else 0))

    tk = next((t for t in (K, 2816, 2048, 1408, 1024, 512, 256, 128) if K % t == 0 and vmem_bytes(t) <= MATMUL_VMEM_BUDGET), K)
    nk = K // tk
    dims = {"nn": NN, "nt": NT, "tn": TN}[mode]

    def body(*refs):
        if add is None:
            a_ref, b_ref, o_ref = refs[:3]
            c_ref = None
        else:
            a_ref, b_ref, c_ref, o_ref = refs[:4]

        def prod():
            return _dot(a_ref[...].astype(BF16), b_ref[...].astype(BF16), dims)

        def finish(r):
            if c_ref is not None:
                r = r + c_ref[...].astype(F32)
            o_ref[...] = r.astype(o_ref.dtype)

        if nk == 1:
            finish(prod())
            return
        acc = refs[-1]
        k = pl.program_id(2)

        @pl.when(k == 0)
        def _():
            acc[...] = prod()

        if nk > 2:
            @pl.when((k > 0) & (k < nk - 1))
            def _():
                acc[...] += prod()

        @pl.when(k == nk - 1)
        def _():
            finish(acc[...] + prod())

    a_spec = pl.BlockSpec((tk, tm), lambda i, j, k: (k, i)) if mode == "tn" else pl.BlockSpec((tm, tk), lambda i, j, k: (i, k))
    b_spec = pl.BlockSpec((tn, tk), lambda i, j, k: (j, k)) if mode == "nt" else pl.BlockSpec((tk, tn), lambda i, j, k: (k, j))
    o_spec = pl.BlockSpec((tm, tn), lambda i, j, k: (i, j))
    ins, specs = [a, b], [a_spec, b_spec]
    if add is not None:
        ins.append(add)
        specs.append(o_spec)
    return pl.pallas_call(
        body,
        grid=(M // tm, N // tn, nk),
        in_specs=specs,
        out_specs=o_spec,
        out_shape=jax.ShapeDtypeStruct((M, N), out_dtype),
        scratch_shapes=[pltpu.VMEM((tm, tn), F32)] if nk > 1 else [],
        compiler_params=_cp(("parallel", "parallel", "arbitrary")),
        name=name,
    )(*ins)


def rowwise(fn, tiled, full, out_tiled, out_acc, tile, name):
    tiled = [t if isinstance(t, tuple) else (t, t.shape[1], 0) for t in tiled]
    rows = tiled[0][0].shape[0]
    tile = min(tile, rows)
    assert rows % tile == 0
    nt, nf, no = len(tiled), len(full), len(out_tiled)

    def body(*refs):
        ins = [r[...] for r in refs[:nt + nf]]
        res = fn(*ins)
        if not isinstance(res, (tuple, list)):
            res = (res,)
        outs = refs[nt + nf:]
        for r, v in zip(outs[:no], res[:no]):
            r[...] = v.astype(r.dtype)
        if out_acc:
            @pl.when(pl.program_id(0) == 0)
            def _():
                for r in outs[no:]:
                    r[...] = jnp.zeros_like(r)

            for r, v in zip(outs[no:], res[no:]):
                r[...] += v

    in_specs = [pl.BlockSpec((tile, w), lambda i, cb=cb: (i, cb)) for (_, w, cb) in tiled]
    in_specs += [pl.BlockSpec(f.shape, lambda i, nd=f.ndim: (0,) * nd) for f in full]
    out_specs = [pl.BlockSpec((tile, w), lambda i: (i, 0)) for (w, _) in out_tiled]
    out_specs += [pl.BlockSpec(s, lambda i, nd=len(s): (0,) * nd) for s in out_acc]
    out_shape = [jax.ShapeDtypeStruct((rows, w), dt) for (w, dt) in out_tiled]
    out_shape += [jax.ShapeDtypeStruct(s, F32) for s in out_acc]
    res = pl.pallas_call(
        body,
        grid=(rows // tile,),
        in_specs=in_specs,
        out_specs=out_specs,
        out_shape=out_shape,
        compiler_params=_cp(("arbitrary",) if out_acc else ("parallel",)),
        name=name,
    )(*[t[0] for t in tiled], *full)
    return res


def adam_reduce(parts, w, m, v, name):
    P, R, C = parts.shape
    tr = _pick(R, (64, 32, 16, 8))
    c1 = 1.0 - ADAM_B1 ** ADAM_STEP
    c2 = 1.0 - ADAM_B2 ** ADAM_STEP

    def body(p_ref, w_ref, m_ref, v_ref, g_ref, d_ref, nm_ref, nv_ref):
        g = p_ref[0].astype(F32)
        for j in range(1, P):
            g = g + p_ref[j].astype(F32)
        mm = ADAM_B1 * m_ref[...] + (1.0 - ADAM_B1) * g
        vv = ADAM_B2 * v_ref[...] + (1.0 - ADAM_B2) * (g * g)
        m_hat = mm / c1
        v_hat = vv / c2
        g_ref[...] = g
        d_ref[...] = -ADAM_LR * (m_hat / (jnp.sqrt(v_hat) + ADAM_EPS) + ADAM_WD * w_ref[...])
        nm_ref[...] = mm
        nv_ref[...] = vv

    spec = pl.BlockSpec((tr, C), lambda i: (i, 0))
    return pl.pallas_call(
        body,
        grid=(R // tr,),
        in_specs=[pl.BlockSpec((P, tr, C), lambda i: (0, i, 0)), spec, spec, spec],
        out_specs=[spec] * 4,
        out_shape=[jax.ShapeDtypeStruct((R, C), F32)] * 4,
        compiler_params=_cp(("parallel",)),
        name=name,
    )(parts, w, m, v)


def reduce_parts(parts, name):
    P, R, C = parts.shape
    tr = _pick(R, (64, 32, 16, 8))

    def body(p_ref, g_ref):
        g = p_ref[0]
        for j in range(1, P):
            g = g + p_ref[j]
        g_ref[...] = g

    return pl.pallas_call(
        body,
        grid=(R // tr,),
        in_specs=[pl.BlockSpec((P, tr, C), lambda i: (0, i, 0))],
        out_specs=pl.BlockSpec((tr, C), lambda i: (i, 0)),
        out_shape=jax.ShapeDtypeStruct((R, C), F32),
        compiler_params=_cp(("parallel",)),
        name=name,
    )(parts)


def pack(arrs):
    flat = jnp.concatenate([a.reshape(-1).astype(F32) for a in arrs])
    n = flat.shape[0]
    padded = -(-n // 1024) * 1024
    return jnp.pad(flat, (0, padded - n)).reshape(padded // 128, 128)


def blocks_from_segments(segs, ncols):
    offs = np.cumsum([0] + [s.shape[1] for s in segs]).tolist()
    blocks = []
    for j in range(NDEV):
        lo, hi = j * ncols, (j + 1) * ncols
        parts = [s[:, max(lo, o) - o:min(hi, o + s.shape[1]) - o]
                 for s, o in zip(segs, offs[:-1]) if max(lo, o) < min(hi, o + s.shape[1])]
        blocks.append(jnp.concatenate(parts, axis=1) if len(parts) > 1 else parts[0])
    return jnp.stack(blocks)


def rows_to_blocks(g):
    return g.reshape(NDEV, -1, g.shape[1])


def segments_from_blocks(g, widths):
    ncols = g.shape[2]
    offs = np.cumsum([0] + list(widths)).tolist()
    out = []
    for o, w in zip(offs[:-1], widths):
        parts = [g[j][:, max(j * ncols, o) - j * ncols:min((j + 1) * ncols, o + w) - j * ncols]
                 for j in range(NDEV) if max(j * ncols, o) < min((j + 1) * ncols, o + w)]
        out.append(jnp.concatenate(parts, axis=1) if len(parts) > 1 else parts[0])
    return out


def unpack(slab, shapes):
    flat = slab.reshape(-1)
    out, off = [], 0
    for s in shapes:
        size = int(np.prod(s))
        out.append(flat[off:off + size].reshape(s))
        off += size
    return out


def modulate_fwd(x, g, sh, sc, name):
    def fn(x, g, sh, sc):
        return x * _rstd(x) * g * (1.0 + sc) + sh

    return rowwise(fn, [x], [g, sh, sc], [(x.shape[1], BF16)], [], 256, name)[0]


def norm_rope_fwd(p, width, cb, w, cosf, sinf, hd, name):
    nh = width // hd

    def fn(x, cosf, sinf, w):
        outs = []
        for h in range(nh):
            xh = x[:, h * hd:(h + 1) * hd]
            y = xh * _rstd(xh) * w
            outs.append(y * cosf + pltpu.roll(y, hd // 2, 1) * sinf)
        return jnp.concatenate(outs, axis=1) if nh > 1 else outs[0]

    return rowwise(fn, [(p, width, cb), cosf, sinf], [w], [(width, BF16)], [], 256, name)[0]


def norm_rope_bwd(p, width, cb, d, w, cosf, sinf, hd, name):
    nh = width // hd

    def fn(x, d, cosf, sinf, w):
        outs = []
        dw = jnp.zeros((1, hd), F32)
        for h in range(nh):
            xh = x[:, h * hd:(h + 1) * hd]
            dh = d[:, h * hd:(h + 1) * hd].astype(F32)
            r = _rstd(xh)
            n = xh * r
            dy = dh * cosf + pltpu.roll(dh * sinf, hd // 2, 1)
            dw = dw + jnp.sum(dy * n, axis=0, keepdims=True)
            dn = dy * w
            outs.append(r * (dn - n * jnp.mean(dn * n, axis=-1, keepdims=True)))
        return (jnp.concatenate(outs, axis=1) if nh > 1 else outs[0]), dw

    return rowwise(fn, [(p, width, cb), d, cosf, sinf], [w], [(width, BF16)], [(1, hd)], 256, name)


def attention_fwd(qr, kr, pkv, kcr, pkv_c, sink, hkv, hd, name):
    T, L = qr.shape[0], kcr.shape[0]
    G = qr.shape[1] // (hkv * hd)
    nb = T // BLOCK
    scale = hd ** -0.5

    def body(q_ref, kp, kc, kn, vp, vc, vn, ck_ref, cv_ref, sink_ref, o_ref, lse_ref, lser_ref):
        i = pl.program_id(1)
        kwin = jnp.concatenate([kp[...], kc[...], kn[...]], axis=0)
        vwin = jnp.concatenate([vp[...], vc[...], vn[...]], axis=0).astype(BF16)
        ck, cv = ck_ref[...], cv_ref[...].astype(BF16)
        row = lax.broadcasted_iota(jnp.int32, (BLOCK, 3 * BLOCK), 0)
        col = lax.broadcasted_iota(jnp.int32, (BLOCK, 3 * BLOCK), 1)
        rel = col - BLOCK - row
        valid = (jnp.abs(rel) <= WINDOW) & ((col >= BLOCK) | (i > 0)) & ((col < 2 * BLOCK) | (i < nb - 1))
        R = range(G)
        qa = q_ref[...]
        qs = [qa[:, g * hd:(g + 1) * hd] for g in R]
        sks = [sink_ref[g] for g in R]
        ss = [jnp.where(valid, _dot(qs[g], kwin, NT) * scale, NEG) for g in R]
        scs = [_dot(qs[g], ck, NT) * scale for g in R]
        ms = [jnp.maximum(jnp.maximum(jnp.max(ss[g], axis=1, keepdims=True), jnp.max(scs[g], axis=1, keepdims=True)),
                          sks[g]) for g in R]
        ps = [jnp.exp(ss[g] - ms[g]) for g in R]
        pcs = [jnp.exp(scs[g] - ms[g]) for g in R]
        nums = [_dot(ps[g].astype(BF16), vwin, NN) + _dot(pcs[g].astype(BF16), cv, NN) for g in R]
        dens = [jnp.exp(sks[g] - ms[g]) + jnp.sum(ps[g], axis=1, keepdims=True) + jnp.sum(pcs[g], axis=1, keepdims=True)
                for g in R]
        o_ref[...] = jnp.concatenate([(nums[g] / dens[g]).astype(o_ref.dtype) for g in R], axis=1)
        eye = (lax.broadcasted_iota(jnp.int32, (BLOCK, BLOCK), 0)
               == lax.broadcasted_iota(jnp.int32, (BLOCK, BLOCK), 1)).astype(F32)
        for g in R:
            lg = ms[g] + jnp.log(dens[g])
            lse_ref[g] = lg
            lser_ref[g] = jnp.sum(lg * eye, axis=0, keepdims=True)

    kv_specs = [
        pl.BlockSpec((BLOCK, hd), lambda h, i: (jnp.maximum(i - 1, 0), h)),
        pl.BlockSpec((BLOCK, hd), lambda h, i: (i, h)),
        pl.BlockSpec((BLOCK, hd), lambda h, i: (jnp.minimum(i + 1, nb - 1), h)),
    ]
    v_specs = [
        pl.BlockSpec((BLOCK, hd), lambda h, i: (jnp.maximum(i - 1, 0), hkv + h)),
        pl.BlockSpec((BLOCK, hd), lambda h, i: (i, hkv + h)),
        pl.BlockSpec((BLOCK, hd), lambda h, i: (jnp.minimum(i + 1, nb - 1), hkv + h)),
    ]
    return pl.pallas_call(
        body,
        grid=(hkv, nb),
        in_specs=[pl.BlockSpec((BLOCK, G * hd), lambda h, i: (i, h))] + kv_specs + v_specs + [
            pl.BlockSpec((L, hd), lambda h, i: (0, h)),
            pl.BlockSpec((L, hd), lambda h, i: (0, hkv + h)),
            pl.BlockSpec((G, 1, 1), lambda h, i: (h, 0, 0)),
        ],
        out_specs=[
            pl.BlockSpec((BLOCK, G * hd), lambda h, i: (i, h)),
            pl.BlockSpec((G, BLOCK, 1), lambda h, i: (h, i, 0)),
            pl.BlockSpec((G, 1, BLOCK), lambda h, i: (h, 0, i)),
        ],
        out_shape=[jax.ShapeDtypeStruct(qr.shape, BF16), jax.ShapeDtypeStruct((hkv * G, T, 1), F32),
                   jax.ShapeDtypeStruct((hkv * G, 1, T), F32)],
        compiler_params=_cp(("parallel", "parallel")),
        name=name,
    )(qr, kr, kr, kr, pkv, pkv, pkv, kcr, pkv_c, sink)


def attention_bwd_q(qr, kr, pkv, kcr, pkv_c, sink, do, o, lse, hkv, hd, name):
    T, L = qr.shape[0], kcr.shape[0]
    G = qr.shape[1] // (hkv * hd)
    nb = T // BLOCK
    scale = hd ** -0.5

    def body(q_ref, kp, kc, kn, vp, vc, vn, ck_ref, cv_ref, sink_ref, do_ref, o_ref, lse_ref,
             dq_ref, dck_ref, dcv_ref, dsink_ref, drr_ref):
        i = pl.program_id(1)

        @pl.when(i == 0)
        def _():
            dck_ref[...] = jnp.zeros_like(dck_ref)
            dcv_ref[...] = jnp.zeros_like(dcv_ref)
            dsink_ref[...] = jnp.zeros_like(dsink_ref)

        kwin = jnp.concatenate([kp[...], kc[...], kn[...]], axis=0)
        vwin = jnp.concatenate([vp[...], vc[...], vn[...]], axis=0).astype(BF16)
        ck, cv = ck_ref[...], cv_ref[...].astype(BF16)
        row = lax.broadcasted_iota(jnp.int32, (BLOCK, 3 * BLOCK), 0)
        col = lax.broadcasted_iota(jnp.int32, (BLOCK, 3 * BLOCK), 1)
        rel = col - BLOCK - row
        valid = (jnp.abs(rel) <= WINDOW) & ((col >= BLOCK) | (i > 0)) & ((col < 2 * BLOCK) | (i < nb - 1))
        R = range(G)
        qa, doa, oa = q_ref[...], do_ref[...], o_ref[...]
        qs = [qa[:, g * hd:(g + 1) * hd] for g in R]
        dos = [doa[:, g * hd:(g + 1) * hd] for g in R]
        lgs = [lse_ref[g] for g in R]
        sks = [sink_ref[g] for g in R]
        ss = [jnp.where(valid, _dot(qs[g], kwin, NT) * scale, NEG) for g in R]
        scs = [_dot(qs[g], ck, NT) * scale for g in R]
        dps = [_dot(dos[g], vwin, NT) for g in R]
        dpcs = [_dot(dos[g], cv, NT) for g in R]
        drs = [jnp.sum(dos[g].astype(F32) * oa[:, g * hd:(g + 1) * hd].astype(F32), axis=1, keepdims=True) for g in R]
        ps = [jnp.exp(ss[g] - lgs[g]) for g in R]
        pcs = [jnp.exp(scs[g] - lgs[g]) for g in R]
        dss = [(ps[g] * (dps[g] - drs[g]) * scale).astype(BF16) for g in R]
        dscs = [(pcs[g] * (dpcs[g] - drs[g]) * scale).astype(BF16) for g in R]
        dqs = [_dot(dss[g], kwin, NN) + _dot(dscs[g], ck, NN) for g in R]
        dcks = [_dot(dscs[g], qs[g], TN) for g in R]
        dcvs = [_dot(pcs[g].astype(BF16), dos[g], TN) for g in R]
        dq_ref[...] = jnp.concatenate(dqs, axis=1)
        dck_ref[...] += (dcks[0] + dcks[1]) + (dcks[2] + dcks[3]) if G == 4 else sum(dcks[1:], dcks[0])
        dcv_ref[...] += (dcvs[0] + dcvs[1]) + (dcvs[2] + dcvs[3]) if G == 4 else sum(dcvs[1:], dcvs[0])
        eye = (lax.broadcasted_iota(jnp.int32, (BLOCK, BLOCK), 0)
               == lax.broadcasted_iota(jnp.int32, (BLOCK, BLOCK), 1)).astype(F32)
        for g in R:
            dsink_ref[g] += -jnp.sum(jnp.exp(sks[g] - lgs[g]) * drs[g], axis=0, keepdims=True)
            drr_ref[g] = jnp.sum(drs[g] * eye, axis=0, keepdims=True)

    kv_specs = [
        pl.BlockSpec((BLOCK, hd), lambda h, i: (jnp.maximum(i - 1, 0), h)),
        pl.BlockSpec((BLOCK, hd), lambda h, i: (i, h)),
        pl.BlockSpec((BLOCK, hd), lambda h, i: (jnp.minimum(i + 1, nb - 1), h)),
    ]
    v_specs = [
        pl.BlockSpec((BLOCK, hd), lambda h, i: (jnp.maximum(i - 1, 0), hkv + h)),
        pl.BlockSpec((BLOCK, hd), lambda h, i: (i, hkv + h)),
        pl.BlockSpec((BLOCK, hd), lambda h, i: (jnp.minimum(i + 1, nb - 1), hkv + h)),
    ]
    qspec = pl.BlockSpec((BLOCK, G * hd), lambda h, i: (i, h))
    return pl.pallas_call(
        body,
        grid=(hkv, nb),
        in_specs=[qspec] + kv_specs + v_specs + [
            pl.BlockSpec((L, hd), lambda h, i: (0, h)),
            pl.BlockSpec((L, hd), lambda h, i: (0, hkv + h)),
            pl.BlockSpec((G, 1, 1), lambda h, i: (h, 0, 0)),
            qspec, qspec,
            pl.BlockSpec((G, BLOCK, 1), lambda h, i: (h, i, 0)),
        ],
        out_specs=[
            qspec,
            pl.BlockSpec((L, hd), lambda h, i: (0, h)),
            pl.BlockSpec((L, hd), lambda h, i: (0, h)),
            pl.BlockSpec((G, 1, 1), lambda h, i: (h, 0, 0)),
            pl.BlockSpec((G, 1, BLOCK), lambda h, i: (h, 0, i)),
        ],
        out_shape=[
            jax.ShapeDtypeStruct(qr.shape, F32),
            jax.ShapeDtypeStruct((L, hkv * hd), F32),
            jax.ShapeDtypeStruct((L, hkv * hd), F32),
            jax.ShapeDtypeStruct((hkv * G, 1, 1), F32),
            jax.ShapeDtypeStruct((hkv * G, 1, T), F32),
        ],
        compiler_params=_cp(("parallel", "arbitrary")),
        name=name,
    )(qr, kr, kr, kr, pkv, pkv, pkv, kcr, pkv_c, sink, do, o, lse)


def attention_bwd_kv(qr, kr, pkv, do, lse_row, dr_row, hkv, hd, name):
    T = qr.shape[0]
    G = qr.shape[1] // (hkv * hd)
    nb = T // BLOCK
    scale = hd ** -0.5

    def body(k_ref, v_ref, *refs):
        qs, dos, lses, drs = refs[0:3], refs[3:6], refs[6:9], refs[9:12]
        dk_ref, dv_ref = refs[12:]
        j = pl.program_id(1)
        k = k_ref[...]
        v = v_ref[...].astype(BF16)
        row = lax.broadcasted_iota(jnp.int32, (BLOCK, BLOCK), 0)
        col = lax.broadcasted_iota(jnp.int32, (BLOCK, BLOCK), 1)
        bias = []
        for d in range(3):
            iq = j + d - 1
            rel = row - col - (d - 1) * BLOCK
            valid = (jnp.abs(rel) <= WINDOW) & (iq >= 0) & (iq < nb)
            bias += [jnp.where(valid, 0.0, NEG)] * G
        bias = jnp.concatenate(bias, axis=1)

        def stack(refs):
            vals = [r[...] for r in refs]
            return jnp.concatenate([a[:, g * hd:(g + 1) * hd] for a in vals for g in range(G)], axis=0)

        q, dob = stack(qs), stack(dos)
        lrow = jnp.concatenate([r[g] for r in lses for g in range(G)], axis=1)
        drow = jnp.concatenate([r[g] for r in drs for g in range(G)], axis=1)
        st = _dot(k, q, NT) * scale + bias
        pt = jnp.exp(st - lrow)
        dpt = _dot(v, dob, NT)
        dst = (pt * (dpt - drow) * scale).astype(BF16)
        dk_ref[...] = _dot(dst, q, NN).astype(dk_ref.dtype)
        dv_ref[...] = _dot(pt.astype(BF16), dob, NN).astype(dv_ref.dtype)

    def q3(width_block):
        return [
            pl.BlockSpec(width_block, lambda h, j: (jnp.maximum(j - 1, 0), h)),
            pl.BlockSpec(width_block, lambda h, j: (j, h)),
            pl.BlockSpec(width_block, lambda h, j: (jnp.minimum(j + 1, nb - 1), h)),
        ]

    row3 = [
        pl.BlockSpec((G, 1, BLOCK), lambda h, j: (h, 0, jnp.maximum(j - 1, 0))),
        pl.BlockSpec((G, 1, BLOCK), lambda h, j: (h, 0, j)),
        pl.BlockSpec((G, 1, BLOCK), lambda h, j: (h, 0, jnp.minimum(j + 1, nb - 1))),
    ]
    qb = (BLOCK, G * hd)
    return pl.pallas_call(
        body,
        grid=(hkv, nb),
        in_specs=[pl.BlockSpec((BLOCK, hd), lambda h, j: (j, h)), pl.BlockSpec((BLOCK, hd), lambda h, j: (j, hkv + h))]
        + q3(qb) + q3(qb) + row3 + row3,
        out_specs=[pl.BlockSpec((BLOCK, hd), lambda h, j: (j, h))] * 2,
        out_shape=[jax.ShapeDtypeStruct((T, hkv * hd), BF16)] * 2,
        compiler_params=_cp(("parallel", "parallel")),
        name=name,
    )(kr, pkv, qr, qr, qr, do, do, do, lse_row, lse_row, lse_row, dr_row, dr_row, dr_row)


def gate_fwd(plr, wf, wb, bf, bb, name):
    n = wf.shape[1]

    def fn(lr, wf, wb, bf, bb):
        lrb = lr.astype(BF16)
        outs = []
        for w, b in ((wf, bf), (wb, bb)):
            z = _dot(lrb, w.astype(BF16), NN) + b
            outs.append((jnp.minimum(z, 0.0) - jnp.log(1.0 + jnp.exp(-jnp.abs(z)))) / GLA_GATE_NORM)
        return outs

    return rowwise(fn, [plr], [wf, wb, bf, bb], [(n, F32), (n, F32)], [], 256, name)


def gate_bwd(plr, dgf, dgb, wf, wb, bf, bb, name):
    n = wf.shape[1]

    def fn(lr, dgf, dgb, wf, wb, bf, bb):
        lrb = lr.astype(BF16)
        dlr = jnp.zeros(lr.shape, F32)
        res = []
        for w, b, dg in ((wf, bf, dgf), (wb, bb, dgb)):
            wb16 = w.astype(BF16)
            z = _dot(lrb, wb16, NN) + b
            dz = dg * _sig(-z) / GLA_GATE_NORM
            dzb = dz.astype(BF16)
            dlr = dlr + _dot(dzb, wb16, NT)
            res += [_dot(lrb, dzb, TN), jnp.sum(dz, axis=0, keepdims=True)]
        return [dlr] + res

    return rowwise(fn, [plr, dgf, dgb], [wf, wb, bf, bb], [(128, BF16)],
                   [(128, n), (1, n), (128, n), (1, n)], 256, name)


def _tri_dot(tri_b, x):
    x1 = x.astype(BF16)
    r1 = x - x1.astype(F32)
    x2 = r1.astype(BF16)
    x3 = (r1 - x2.astype(F32)).astype(BF16)
    return _dot(tri_b, x1, NN) + _dot(tri_b, x2, NN) + _dot(tri_b, x3, NN)


def gla_fwd(pqk, pv, gl, s0, heads, reverse, name, o_add=None):
    T = pqk.shape[0]
    dk = pqk.shape[1] // (2 * heads)
    dv = pv.shape[1] // heads
    C = GLA_CHUNK
    nc = T // C
    qscale = dk ** -0.5

    def body(*refs):
        if o_add is None:
            q_ref, k_ref, v_ref, g_ref, s0_ref, o_ref, st_ref, sf_ref, S = refs
            oa_ref = None
        else:
            q_ref, k_ref, v_ref, g_ref, s0_ref, oa_ref, o_ref, st_ref, sf_ref, S = refs
        n = pl.program_id(0)

        @pl.when(n == 0)
        def _():
            S[...] = s0_ref[...]

        r = lax.broadcasted_iota(jnp.int32, (C, C), 0)
        c = lax.broadcasted_iota(jnp.int32, (C, C), 1)
        tri = (r <= c) if reverse else (r >= c)
        trib = tri.astype(BF16)
        ga, qa, ka, va = g_ref[...], q_ref[...], k_ref[...], v_ref[...]
        sts = [S[h] for h in range(heads)]
        H = range(heads)
        gs = [ga[:, h * dk:(h + 1) * dk] for h in H]
        bs = [_tri_dot(trib, g) for g in gs]
        bls = [jnp.sum(g, axis=0, keepdims=True) for g in gs]
        mid = lax.broadcasted_iota(jnp.int32, (C, 1), 0) == C // 2
        bms = [jnp.sum(jnp.where(mid, b, 0.0), axis=0, keepdims=True) for b in bs]
        vs = [va[:, h * dv:(h + 1) * dv].astype(BF16) for h in H]
        qs = [qa[:, h * dk:(h + 1) * dk].astype(F32) * qscale for h in H]
        qes = [(qs[h] * jnp.exp(bs[h])).astype(BF16) for h in H]
        qms = [(qs[h] * jnp.exp(bs[h] - bms[h])).astype(BF16) for h in H]
        kms = [(ka[:, h * dk:(h + 1) * dk].astype(F32) * jnp.exp(bms[h] - bs[h])).astype(BF16) for h in H]
        kls = [(ka[:, h * dk:(h + 1) * dk].astype(F32) * jnp.exp(bls[h] - bs[h])).astype(BF16) for h in H]
        inter = [_dot(qes[h], sts[h].astype(BF16), NT) for h in H]
        upd = [_dot(vs[h], kls[h], TN) for h in H]
        As = [jnp.where(tri, _dot(qms[h], kms[h], NT), 0.0).astype(BF16) for h in H]
        outs = [inter[h] + _dot(As[h], vs[h], NN) for h in H]
        news = [sts[h] * jnp.exp(bls[h]) + upd[h] for h in H]
        o = jnp.concatenate(outs, axis=1)
        if oa_ref is not None:
            o = o + oa_ref[...]
        o_ref[...] = o
        for h in range(heads):
            st_ref[0, h] = sts[h]
            S[h] = news[h]

        @pl.when(n == nc - 1)
        def _():
            for h in range(heads):
                sf_ref[h] = news[h]

    def ci(n):
        return (nc - 1 - n) if reverse else n

    specs = [
        pl.BlockSpec((C, heads * dk), lambda n: (ci(n), 0)),
        pl.BlockSpec((C, heads * dk), lambda n: (ci(n), 1)),
        pl.BlockSpec((C, heads * dv), lambda n: (ci(n), 0)),
        pl.BlockSpec((C, heads * dk), lambda n: (ci(n), 0)),
        pl.BlockSpec((heads, dv, dk), lambda n: (0, 0, 0)),
    ]
    ins = [pqk, pqk, pv, gl, s0]
    if o_add is not None:
        specs.append(pl.BlockSpec((C, heads * dv), lambda n: (ci(n), 0)))
        ins.append(o_add)
    return pl.pallas_call(
        body,
        grid=(nc,),
        in_specs=specs,
        out_specs=[
            pl.BlockSpec((C, heads * dv), lambda n: (ci(n), 0)),
            pl.BlockSpec((1, heads, dv, dk), lambda n: (ci(n), 0, 0, 0)),
            pl.BlockSpec((heads, dv, dk), lambda n: (0, 0, 0)),
        ],
        out_shape=[
            jax.ShapeDtypeStruct((T, heads * dv), F32),
            jax.ShapeDtypeStruct((nc, heads, dv, dk), F32),
            jax.ShapeDtypeStruct((heads, dv, dk), F32),
        ],
        scratch_shapes=[pltpu.VMEM((heads, dv, dk), F32)],
        compiler_params=_cp(("arbitrary",)),
        name=name,
    )(*ins)


def gla_bwd(pqk, pv, gl, states, do, dsf, heads, reverse, name, acc=None):
    T = pqk.shape[0]
    dk = pqk.shape[1] // (2 * heads)
    dv = pv.shape[1] // heads
    C = GLA_CHUNK
    nc = T // C
    qscale = dk ** -0.5

    def body(*refs):
        if acc is None:
            q_ref, k_ref, v_ref, g_ref, st_ref, do_ref, dsf_ref, dq_ref, dk_ref, dv_ref, dg_ref, ds0_ref, dS = refs
            aq = ak = av = None
        else:
            (q_ref, k_ref, v_ref, g_ref, st_ref, do_ref, dsf_ref, aq, ak, av,
             dq_ref, dk_ref, dv_ref, dg_ref, ds0_ref, dS) = refs
        n = pl.program_id(0)

        @pl.when(n == 0)
        def _():
            dS[...] = dsf_ref[...]

        r = lax.broadcasted_iota(jnp.int32, (C, C), 0)
        c = lax.broadcasted_iota(jnp.int32, (C, C), 1)
        tri = (r <= c) if reverse else (r >= c)
        tri_t = (r >= c) if reverse else (r <= c)
        trib, tritb = tri.astype(BF16), tri_t.astype(BF16)
        ga, qa, ka, va, doa = g_ref[...], q_ref[...], k_ref[...], v_ref[...], do_ref[...]
        sts = [st_ref[0, h] for h in range(heads)]
        dsts = [dS[h] for h in range(heads)]
        H = range(heads)
        gs = [ga[:, h * dk:(h + 1) * dk] for h in H]
        bs = [_tri_dot(trib, g) for g in gs]
        bls = [jnp.sum(g, axis=0, keepdims=True) for g in gs]
        mid = lax.broadcasted_iota(jnp.int32, (C, 1), 0) == C // 2
        bms = [jnp.sum(jnp.where(mid, b, 0.0), axis=0, keepdims=True) for b in bs]
        ebs = [jnp.exp(b) for b in bs]
        embs = [jnp.exp(bs[h] - bms[h]) for h in H]
        enbs = [jnp.exp(bms[h] - bs[h]) for h in H]
        elbs = [jnp.exp(bls[h] - bs[h]) for h in H]
        ebls = [jnp.exp(bl) for bl in bls]
        vbs = [va[:, h * dv:(h + 1) * dv].astype(BF16) for h in H]
        dobs = [doa[:, h * dv:(h + 1) * dv].astype(BF16) for h in H]
        qs = [qa[:, h * dk:(h + 1) * dk].astype(F32) * qscale for h in H]
        qes = [qs[h] * ebs[h] for h in H]
        qms = [qs[h] * embs[h] for h in H]
        kms = [ka[:, h * dk:(h + 1) * dk].astype(F32) * enbs[h] for h in H]
        kls = [ka[:, h * dk:(h + 1) * dk].astype(F32) * elbs[h] for h in H]
        qebs = [a.astype(BF16) for a in qes]
        qmbs = [a.astype(BF16) for a in qms]
        kmbs = [a.astype(BF16) for a in kms]
        klbs = [a.astype(BF16) for a in kls]
        stbs = [a.astype(BF16) for a in sts]
        dstbs = [a.astype(BF16) for a in dsts]
        ps = [jnp.where(tri, _dot(qmbs[h], kmbs[h], NT), 0.0).astype(BF16) for h in H]
        dps = [jnp.where(tri, _dot(dobs[h], vbs[h], NT), 0.0).astype(BF16) for h in H]
        dqes = [_dot(dobs[h], stbs[h], NN) for h in H]
        dkls = [_dot(vbs[h], dstbs[h], NN) for h in H]
        dv1 = [_dot(klbs[h], dstbs[h], NT) for h in H]
        dsn1 = [_dot(dobs[h], qebs[h], TN) for h in H]
        dqms = [_dot(dps[h], kmbs[h], NN) for h in H]
        dkms = [_dot(dps[h], qmbs[h], TN) for h in H]
        dvs = [_dot(ps[h], dobs[h], TN) + dv1[h] for h in H]
        dbls = [ebls[h] * jnp.sum(dsts[h] * sts[h], axis=0, keepdims=True)
                + jnp.sum(dkls[h] * kls[h], axis=0, keepdims=True) for h in H]
        dsns = [dsn1[h] + dsts[h] * ebls[h] for h in H]
        dqs = [(dqes[h] * ebs[h] + dqms[h] * embs[h]) * qscale for h in H]
        dks = [dkms[h] * enbs[h] + dkls[h] * elbs[h] for h in H]
        dbs = [dqes[h] * qes[h] + dqms[h] * qms[h] - dkms[h] * kms[h] - dkls[h] * kls[h] for h in H]
        dgs = [_tri_dot(tritb, dbs[h]) + dbls[h] for h in H]
        dq, dkk, dvv = (jnp.concatenate(a, axis=1) for a in (dqs, dks, dvs))
        if aq is not None:
            dq = dq + aq[...].astype(F32)
            dkk = dkk + ak[...].astype(F32)
            dvv = dvv + av[...].astype(F32)
        dq_ref[...] = dq.astype(dq_ref.dtype)
        dk_ref[...] = dkk.astype(dk_ref.dtype)
        dv_ref[...] = dvv.astype(dv_ref.dtype)
        dg_ref[...] = jnp.concatenate(dgs, axis=1)
        for h in range(heads):
            dS[h] = dsns[h]

        @pl.when(n == nc - 1)
        def _():
            for h in range(heads):
                ds0_ref[h] = dsns[h]

    def ci(n):
        return n if reverse else (nc - 1 - n)

    kspec = pl.BlockSpec((C, heads * dk), lambda n: (ci(n), 0))
    vspec = pl.BlockSpec((C, heads * dv), lambda n: (ci(n), 0))
    sspec = pl.BlockSpec((heads, dv, dk), lambda n: (0, 0, 0))
    specs = [
        kspec,
        pl.BlockSpec((C, heads * dk), lambda n: (ci(n), 1)),
        vspec,
        kspec,
        pl.BlockSpec((1, heads, dv, dk), lambda n: (ci(n), 0, 0, 0)),
        vspec,
        sspec,
    ]
    ins = [pqk, pqk, pv, gl, states, do, dsf]
    odt = F32 if acc is None else BF16
    if acc is not None:
        specs += [kspec, kspec, vspec]
        ins += list(acc)
    return pl.pallas_call(
        body,
        grid=(nc,),
        in_specs=specs,
        out_specs=[kspec, kspec, vspec, kspec, sspec],
        out_shape=[
            jax.ShapeDtypeStruct((T, heads * dk), odt),
            jax.ShapeDtypeStruct((T, heads * dk), odt),
            jax.ShapeDtypeStruct((T, heads * dv), odt),
            jax.ShapeDtypeStruct((T, heads * dk), F32),
            jax.ShapeDtypeStruct((heads, dv, dk), F32),
        ],
        scratch_shapes=[pltpu.VMEM((heads, dv, dk), F32)],
        compiler_params=_cp(("arbitrary",)),
        name=name,
    )(*ins)


def gla_out_fwd(og, prb, gn, heads, name):
    dv = og.shape[1] // heads

    def fn(og, rb, gn):
        outs = []
        for h in range(heads):
            oh = og[:, h * dv:(h + 1) * dv]
            outs.append(oh * _rstd(oh) * gn)
        y = jnp.concatenate(outs, axis=1)
        return y * (rb * _sig(rb))

    return rowwise(fn, [og, prb], [gn], [(og.shape[1], BF16)], [], 256, name)[0]


def gla_out_bwd(og, prb, du, gn, heads, name):
    dv = og.shape[1] // heads

    def fn(og, rb, du, gn):
        sg = _sig(rb)
        silu = rb * sg
        dsilu = sg * (1.0 + rb * (1.0 - sg))
        dog, ys = [], []
        dgn = jnp.zeros((1, dv), F32)
        for h in range(heads):
            sl = slice(h * dv, (h + 1) * dv)
            oh = og[:, sl]
            r = _rstd(oh)
            n = oh * r
            ys.append(n * gn)
            dy = du[:, sl] * silu[:, sl]
            dgn = dgn + jnp.sum(dy * n, axis=0, keepdims=True)
            dn = dy * gn
            dog.append(r * (dn - n * jnp.mean(dn * n, axis=-1, keepdims=True)))
        y = jnp.concatenate(ys, axis=1)
        return jnp.concatenate(dog, axis=1), du * y * dsilu, dgn

    return rowwise(fn, [og, prb, du], [gn], [(og.shape[1], F32), (og.shape[1], BF16)], [(1, dv)], 128, name)


def conv_specs(T, tt, tc, off, order):
    r8 = tt // 8
    last8 = T // 8 - 1
    if order == "ij":
        return [
            pl.BlockSpec((tt, tc), lambda i, j: (i, j + off)),
            pl.BlockSpec((8, tc), lambda i, j: (jnp.maximum(i * r8 - 1, 0), j + off)),
            pl.BlockSpec((8, tc), lambda i, j: (jnp.minimum((i + 1) * r8, last8), j + off)),
        ]
    return [
        pl.BlockSpec((tt, tc), lambda j, i: (i, j + off)),
        pl.BlockSpec((8, tc), lambda j, i: (jnp.maximum(i * r8 - 1, 0), j + off)),
        pl.BlockSpec((8, tc), lambda j, i: (jnp.minimum((i + 1) * r8, last8), j + off)),
    ]


def _shifted(u, hp, hn, i, nt_):
    tt = u.shape[0]
    row = lax.broadcasted_iota(jnp.int32, u.shape, 0)
    r8 = lax.broadcasted_iota(jnp.int32, hp.shape, 0)
    prev = jnp.sum(jnp.where(r8 == 7, hp, 0.0), axis=0, keepdims=True) * (i > 0).astype(F32)
    nxt = jnp.sum(jnp.where(r8 == 0, hn, 0.0), axis=0, keepdims=True) * (i < nt_ - 1).astype(F32)
    down = jnp.where(row == 0, prev, pltpu.roll(u, 1, 0))
    up = jnp.where(row == tt - 1, nxt, pltpu.roll(u, tt - 1, 0))
    return down, up


def conv_swiglu_fwd(u, cw, cb, name):
    T, F2 = u.shape
    F = F2 // 2
    tt = min(512, T)
    tc = _pick(F, (512, 256, 128))
    nt_, ncol = T // tt, F // tc
    H = CONV_HALO
    n = tt + 2 * H

    def body(ua, uap, uan, ug, ugp, ugn, wa, wg, ba, bg, f_ref):
        i = pl.program_id(1)
        keep_p = (i > 0).astype(F32)
        keep_n = (i < nt_ - 1).astype(F32)
        res = []
        for m, p, nx, w, b in ((ua, uap, uan, wa, ba), (ug, ugp, ugn, wg, bg)):
            x = jnp.concatenate([p[...] * keep_p, m[...], nx[...] * keep_n], axis=0)
            down, up = pltpu.roll(x, 1, 0)[H:H + tt], pltpu.roll(x, n - 1, 0)[H:H + tt]
            res.append(w[0] * down + w[1] * x[H:H + tt] + w[2] * up + b[...])
        a, g = res
        f_ref[...] = (a * _sig_tanh(a) * g).astype(f_ref.dtype)

    wspec = lambda off: pl.BlockSpec((3, 1, tc), lambda j, i: (0, 0, j + off))
    bspec = lambda off: pl.BlockSpec((1, tc), lambda j, i: (0, j + off))
    return pl.pallas_call(
        body,
        grid=(ncol, nt_),
        in_specs=conv_halo_specs(T, tt, tc, 0) + conv_halo_specs(T, tt, tc, ncol)
        + [wspec(0), wspec(ncol), bspec(0), bspec(ncol)],
        out_specs=pl.BlockSpec((tt, tc), lambda j, i: (i, j)),
        out_shape=jax.ShapeDtypeStruct((T, F), BF16),
        compiler_params=_cp(("parallel", "parallel")),
        name=name,
    )(u, u, u, u, u, u, cw, cw, cb, cb)


def conv_swiglu_bwd(u, cw, cb, df, name):
    T, F2 = u.shape
    F = F2 // 2
    tt = min(256, T)
    tc = _pick(F, (512, 256, 128))
    nt_, ncol = T // tt, F // tc

    def body(ua, uap, uan, ug, ugp, ugn, wa, wg, ba, bg, df_ref, da_ref, dg_ref, dwa, dwg, dba, dbg):
        i = pl.program_id(1)

        @pl.when(i == 0)
        def _():
            for r in (dwa, dwg, dba, dbg):
                r[...] = jnp.zeros_like(r)

        sh = []
        res = []
        for um, up_, un, w, b in ((ua, uap, uan, wa, ba), (ug, ugp, ugn, wg, bg)):
            x = um[...]
            down, up = _shifted(x, up_[...], un[...], i, nt_)
            sh.append((down, x, up))
            res.append(w[0] * down + w[1] * x + w[2] * up + b[...])
        a, g = res
        d = df_ref[...].astype(F32)
        sg = _sig(a)
        da = d * g * sg * (1.0 + a * (1.0 - sg))
        dg = d * a * sg
        da_ref[...] = da
        dg_ref[...] = dg
        for dd, (down, x, up), dw, db in ((da, sh[0], dwa, dba), (dg, sh[1], dwg, dbg)):
            dw[0] += jnp.sum(dd * down, axis=0, keepdims=True)
            dw[1] += jnp.sum(dd * x, axis=0, keepdims=True)
            dw[2] += jnp.sum(dd * up, axis=0, keepdims=True)
            db[...] += jnp.sum(dd, axis=0, keepdims=True)

    wspec = lambda off: pl.BlockSpec((3, 1, tc), lambda j, i: (0, 0, j + off))
    bspec = lambda off: pl.BlockSpec((1, tc), lambda j, i: (0, j + off))
    tile = pl.BlockSpec((tt, tc), lambda j, i: (i, j))
    return pl.pallas_call(
        body,
        grid=(ncol, nt_),
        in_specs=conv_specs(T, tt, tc, 0, "ji") + conv_specs(T, tt, tc, ncol, "ji")
        + [wspec(0), wspec(ncol), bspec(0), bspec(ncol), tile],
        out_specs=[tile, tile, wspec(0), wspec(0), bspec(0), bspec(0)],
        out_shape=[
            jax.ShapeDtypeStruct((T, F), F32), jax.ShapeDtypeStruct((T, F), F32),
            jax.ShapeDtypeStruct((3, 1, F), F32), jax.ShapeDtypeStruct((3, 1, F), F32),
            jax.ShapeDtypeStruct((1, F), F32), jax.ShapeDtypeStruct((1, F), F32),
        ],
        compiler_params=_cp(("parallel", "arbitrary")),
        name=name,
    )(u, u, u, u, u, u, cw, cw, cb, cb, df)


CONV_HALO = 16


def conv_halo_specs(T, tt, tc, off):
    r = tt // CONV_HALO
    last = T // CONV_HALO - 1
    return [
        pl.BlockSpec((tt, tc), lambda j, i: (i, j + off)),
        pl.BlockSpec((CONV_HALO, tc), lambda j, i: (jnp.maximum(i * r - 1, 0), j + off)),
        pl.BlockSpec((CONV_HALO, tc), lambda j, i: (jnp.minimum((i + 1) * r, last), j + off)),
    ]


def conv_swiglu_bwd_fused(u, cw, cb, df, name):
    T, F2 = u.shape
    F = F2 // 2
    tt = min(512, T)
    tc = _pick(F, (512, 256, 128))
    nt_, ncol = T // tt, F // tc
    H = CONV_HALO
    n = tt + 2 * H

    def body(ua, uap, uan, ug, ugp, ugn, dm, dp_, dn, wa, wg, ba, bg, dua_ref, dug_ref, dwa, dwg, dba, dbg):
        i = pl.program_id(1)

        @pl.when(i == 0)
        def _():
            for r in (dwa, dwg, dba, dbg):
                r[...] = jnp.zeros_like(r)

        keep_p = (i > 0).astype(F32)
        keep_n = (i < nt_ - 1).astype(F32)

        def ext(m, p, nx):
            return jnp.concatenate([p[...].astype(F32) * keep_p, m[...].astype(F32), nx[...].astype(F32) * keep_n],
                                   axis=0)

        d = ext(dm, dp_, dn)
        conv, parts = [], []
        for m, p, nx, w, b in ((ua, uap, uan, wa, ba), (ug, ugp, ugn, wg, bg)):
            x = ext(m, p, nx)
            down, up = pltpu.roll(x, 1, 0), pltpu.roll(x, n - 1, 0)
            parts.append((down, x, up))
            conv.append(w[0] * down + w[1] * x + w[2] * up + b[...])
        a, g = conv
        sg = _sig_tanh(a)
        da = d * g * sg * (1.0 + a * (1.0 - sg))
        dg = d * a * sg
        for dd, w, (down, x, up), o_ref, dw, db in ((da, wa, parts[0], dua_ref, dwa, dba),
                                                    (dg, wg, parts[1], dug_ref, dwg, dbg)):
            du = w[0] * pltpu.roll(dd, n - 1, 0) + w[1] * dd + w[2] * pltpu.roll(dd, 1, 0)
            o_ref[...] = du[H:H + tt].astype(o_ref.dtype)
            ddm = dd[H:H + tt]
            dw[0] += jnp.sum(ddm * down[H:H + tt], axis=0, keepdims=True)
            dw[1] += jnp.sum(ddm * x[H:H + tt], axis=0, keepdims=True)
            dw[2] += jnp.sum(ddm * up[H:H + tt], axis=0, keepdims=True)
            db[...] += jnp.sum(ddm, axis=0, keepdims=True)

    wspec = lambda off: pl.BlockSpec((3, 1, tc), lambda j, i: (0, 0, j + off))
    bspec = lambda off: pl.BlockSpec((1, tc), lambda j, i: (0, j + off))
    tile = pl.BlockSpec((tt, tc), lambda j, i: (i, j))
    return pl.pallas_call(
        body,
        grid=(ncol, nt_),
        in_specs=conv_halo_specs(T, tt, tc, 0) + conv_halo_specs(T, tt, tc, ncol) + conv_halo_specs(T, tt, tc, 0)
        + [wspec(0), wspec(ncol), bspec(0), bspec(ncol)],
        out_specs=[tile, tile, wspec(0), wspec(0), bspec(0), bspec(0)],
        out_shape=[
            jax.ShapeDtypeStruct((T, F), BF16), jax.ShapeDtypeStruct((T, F), BF16),
            jax.ShapeDtypeStruct((3, 1, F), F32), jax.ShapeDtypeStruct((3, 1, F), F32),
            jax.ShapeDtypeStruct((1, F), F32), jax.ShapeDtypeStruct((1, F), F32),
        ],
        compiler_params=_cp(("parallel", "arbitrary")),
        name=name,
    )(u, u, u, u, u, u, df, df, df, cw, cw, cb, cb)


def conv_transpose(d, cw, off, name):
    T, F = d.shape
    tt = min(256, T)
    tc = _pick(F, (512, 256, 128))
    nt_, ncol = T // tt, F // tc
    offb = off // tc

    def body(dm, dp_, dn, w, o_ref):
        i = pl.program_id(0)
        x = dm[...]
        down, up = _shifted(x, dp_[...], dn[...], i, nt_)
        o_ref[...] = (w[0] * up + w[1] * x + w[2] * down).astype(o_ref.dtype)

    return pl.pallas_call(
        body,
        grid=(nt_, ncol),
        in_specs=conv_specs(T, tt, tc, 0, "ij") + [pl.BlockSpec((3, 1, tc), lambda i, j: (0, 0, j + offb))],
        out_specs=pl.BlockSpec((tt, tc), lambda i, j: (i, j)),
        out_shape=jax.ShapeDtypeStruct((T, F), BF16),
        compiler_params=_cp(("parallel", "parallel")),
        name=name,
    )(d, d, d, cw)


def rope_tables(n, hd):
    rows = n // GRID_W
    row = jnp.repeat(jnp.arange(rows), GRID_W)
    col = jnp.tile(jnp.arange(GRID_W), rows)
    n_freq = hd // 4
    inv = ROPE_THETA ** (-jnp.arange(n_freq, dtype=F32) / n_freq)
    ang = jnp.concatenate([row[:, None] * inv, col[:, None] * inv], axis=-1)
    cos, sin = jnp.cos(ang), jnp.sin(ang)
    return jnp.concatenate([cos, cos], axis=-1), jnp.concatenate([-sin, sin], axis=-1)


def local_step(x, ctx, tgt, mod, modc, W, P):
    T, D = x.shape
    L = ctx.shape[0]
    hd, hq, hkv, gh = P["hd"], P["hq"], P["hkv"], P["gh"]
    sh1, sc1, g1, sh2, sc2, g2 = mod
    csh1, csc1 = modc
    kvw = hkv * hd
    gkw = W["gqk"].shape[1] // 2
    gdv = D // gh
    gdk = gkw // gh

    h = modulate_fwd(x, P["g_mix"], sh1, sc1, "mod1")
    hc = modulate_fwd(ctx, P["g_mix"], csh1, csc1, "mod1_ctx")
    pq = matmul(h, W["q"], "nn", F32, "proj_q")
    pkv = matmul(h, W["kv"], "nn", F32, "proj_kv")
    pgqk = matmul(h, W["gqk"], "nn", F32, "proj_gqk")
    pgv = matmul(h, W["gv"], "nn", F32, "proj_gv")
    prb = matmul(h, W["rb"], "nn", F32, "proj_rb")
    plr = matmul(h, W["lr"], "nn", F32, "proj_lr")
    pgab = matmul(h, W["gab"], "nn", F32, "proj_gab")
    pkv_c = matmul(hc, W["kv"], "nn", F32, "proj_kv_ctx")
    pgqk_c = matmul(hc, W["gqk"], "nn", F32, "proj_gqk_ctx")
    pgv_c = matmul(hc, W["gv"], "nn", F32, "proj_gv_ctx")
    plr_c = matmul(hc, W["lr"], "nn", F32, "proj_lr_ctx")

    cosf, sinf = rope_tables(T, hd)
    one_c, zero_c = jnp.ones((L, hd), F32), jnp.zeros((L, hd), F32)
    qr = norm_rope_fwd(pq, hq * hd, 0, P["q_norm"], cosf, sinf, hd, "qnorm")
    kr = norm_rope_fwd(pkv, kvw, 0, P["k_norm"], cosf, sinf, hd, "knorm")
    kcr = norm_rope_fwd(pkv_c, kvw, 0, P["k_norm"], one_c, zero_c, hd, "knorm_ctx")
    sink = P["attn_sink"].reshape(hq, 1, 1)
    o_attn, lse, lse_row = attention_fwd(qr, kr, pkv, kcr, pkv_c, sink, hkv, hd, "attn_fwd")

    gf, gb = gate_fwd(plr, W["gate_f"], W["gate_b"], P["b_gate_f"], P["b_gate_b"], "gates")
    gfc, gbc = gate_fwd(plr_c, W["gate_f"], W["gate_b"], P["b_gate_f"], P["b_gate_b"], "gates_ctx")
    zero_state = jnp.zeros((gh, gdv, gdk), F32)
    _, st_cf, s_cf = gla_fwd(pgqk_c, pgv_c, gfc, zero_state, gh, False, "gla_ctx_f")
    _, st_cb, s_cb = gla_fwd(pgqk_c, pgv_c, gbc, zero_state, gh, True, "gla_ctx_b")
    of, st_f, _ = gla_fwd(pgqk, pgv, gf, s_cf, gh, False, "gla_f")
    og, st_b, _ = gla_fwd(pgqk, pgv, gb, s_cb, gh, True, "gla_b", o_add=of)
    ug = gla_out_fwd(og, prb, P["gla_norm"], gh, "gla_out")

    ya = matmul(o_attn, W["attn_o"], "nn", F32, "attn_o")
    yg = matmul(ug, W["gla_o"], "nn", F32, "gla_o")

    def merge_fn(ya, yg, ga, gb_):
        return _sig(ga) * ya + _sig(gb_) * yg

    z = rowwise(merge_fn, [ya, yg, (pgab, D, 0), (pgab, D, 1)], [], [(D, BF16)], [], 256, "merge")[0]
    mo = matmul(z, W["out"], "nn", F32, "w_out")

    def res_fn(x, mo, g1, gffn, sh2, sc2):
        x1 = x + g1 * mo
        return x1, x1 * _rstd(x1) * gffn * (1.0 + sc2) + sh2

    x1, h2 = rowwise(res_fn, [x, mo], [g1, P["g_ffn"], sh2, sc2], [(D, F32), (D, BF16)], [], 256, "res_mod2")
    u = matmul(h2, W["up"], "nn", F32, "w_up")
    cw3 = W["conv_w"].reshape(3, 1, -1)
    f = conv_swiglu_fwd(u, cw3, P["conv_b"], "conv_swiglu")
    fo = matmul(f, W["down"], "nn", F32, "w_down")

    def final_fn(x1, fo, tgt, g2):
        e = x1 + g2 * fo - tgt
        dy = e * (1.0 / D)
        lsum = jnp.sum(jnp.sum(e * e, axis=1, keepdims=True), axis=0, keepdims=True)
        return dy, dy * g2, jnp.broadcast_to(lsum, (1, 128)), jnp.sum(dy * fo, axis=0, keepdims=True)

    dy, dfo, lsum, dg2 = rowwise(final_fn, [x1, fo, tgt], [g2], [(D, F32), (D, BF16)], [(1, 128), (1, D)], 256, "loss")
    df = matmul(dfo, W["down"], "nt", BF16, "d_f")
    dw_down = matmul(f, dfo, "tn", BF16, "dw_down")
    du_a, du_g, dcw_a, dcw_g, dcb_a, dcb_g = conv_swiglu_bwd_fused(u, cw3, P["conv_b"], df, "conv_swiglu_bwd")
    Fh = du_a.shape[1]
    dh2 = matmul(du_a, W["up"][:, :Fh], "nt", F32, "d_h2_a")
    dh2 = matmul(du_g, W["up"][:, Fh:], "nt", F32, "d_h2_g", add=dh2)
    dw_up = [matmul(h2, du_a, "tn", BF16, "dw_up_a"), matmul(h2, du_g, "tn", BF16, "dw_up_g")]

    def mod2_bwd_fn(x1, dh, dy, mo, gffn, sc2, g1):
        r = _rstd(x1)
        n = x1 * r
        dyy = dh * (1.0 + sc2)
        dn = dyy * gffn
        dx1 = dy + r * (dn - n * jnp.mean(dn * n, axis=-1, keepdims=True))
        s0 = lambda a: jnp.sum(a, axis=0, keepdims=True)
        return dx1, dx1 * g1, s0(dyy * n), s0(dh), s0(dh * n * gffn), s0(dx1 * mo)

    dx1, dmo, dg_ffn, dsh2, dsc2, dg1 = rowwise(
        mod2_bwd_fn, [x1, dh2, dy, mo], [P["g_ffn"], sc2, g1], [(D, F32), (D, BF16)], [(1, D)] * 4, 128, "mod2_bwd")
    dz = matmul(dmo, W["out"], "nt", F32, "d_z")
    dw_out = matmul(z, dmo, "tn", BF16, "dw_out")

    def merge_bwd_fn(dz, ya, yg, ga, gb_):
        sa, sb = _sig(ga), _sig(gb_)
        return dz * sa, dz * sb, jnp.concatenate([dz * ya * sa * (1.0 - sa), dz * yg * sb * (1.0 - sb)], axis=1)

    dya, dyg, dpgab = rowwise(merge_bwd_fn, [dz, ya, yg, (pgab, D, 0), (pgab, D, 1)], [],
                              [(D, BF16), (D, BF16), (2 * D, BF16)], [], 128, "merge_bwd")
    do_attn = matmul(dya, W["attn_o"], "nt", BF16, "d_oattn")
    dw_attn_o = matmul(o_attn, dya, "tn", BF16, "dw_attn_o")
    dug = matmul(dyg, W["gla_o"], "nt", F32, "d_ug")
    dw_gla_o = matmul(ug, dyg, "tn", BF16, "dw_gla_o")
    dog, dprb, dgn = gla_out_bwd(og, prb, dug, P["gla_norm"], gh, "gla_out_bwd")

    dq1, dk1, dv1, dgf, ds_cf = gla_bwd(pgqk, pgv, gf, st_f, dog, zero_state, gh, False, "gla_f_bwd")
    dgq, dgk, dpgv, dgb, ds_cb = gla_bwd(pgqk, pgv, gb, st_b, dog, zero_state, gh, True, "gla_b_bwd",
                                          acc=(dq1, dk1, dv1))
    dpgqk = jnp.concatenate([dgq, dgk], axis=1)
    zero_do = jnp.zeros((L, gh * gdv), F32)
    cq1, ck1, cv1, dgfc, _ = gla_bwd(pgqk_c, pgv_c, gfc, st_cf, zero_do, ds_cf, gh, False, "gla_ctx_f_bwd")
    cq, ck, dpgv_c, dgbc, _ = gla_bwd(pgqk_c, pgv_c, gbc, st_cb, zero_do, ds_cb, gh, True, "gla_ctx_b_bwd",
                                      acc=(cq1, ck1, cv1))
    dpgqk_c = jnp.concatenate([cq, ck], axis=1)
    dplr, dwgf, dbgf, dwgb, dbgb = gate_bwd(plr, dgf, dgb, W["gate_f"], W["gate_b"], P["b_gate_f"], P["b_gate_b"], "gates_bwd")
    dplr_c, dwgf_c, dbgf_c, dwgb_c, dbgb_c = gate_bwd(plr_c, dgfc, dgbc, W["gate_f"], W["gate_b"], P["b_gate_f"],
                                                      P["b_gate_b"], "gates_ctx_bwd")

    dqr, dkc_r, dvc, dsink, dr_row = attention_bwd_q(qr, kr, pkv, kcr, pkv_c, sink, do_attn, o_attn, lse, hkv, hd,
                                                     "attn_bwd_q")
    dkr, dv = attention_bwd_kv(qr, kr, pkv, do_attn, lse_row, dr_row, hkv, hd, "attn_bwd_kv")
    dpq, dqn = norm_rope_bwd(pq, hq * hd, 0, dqr, P["q_norm"], cosf, sinf, hd, "qnorm_bwd")
    dpk, dkn = norm_rope_bwd(pkv, kvw, 0, dkr, P["k_norm"], cosf, sinf, hd, "knorm_bwd")
    dpk_c, dkn_c = norm_rope_bwd(pkv_c, kvw, 0, dkc_r, P["k_norm"], one_c, zero_c, hd, "knorm_ctx_bwd")
    dpkv = jnp.concatenate([dpk, dv], axis=1)
    dpkv_c = jnp.concatenate([dpk_c, dvc.astype(BF16)], axis=1)

    dw_q = matmul(h, dpq, "tn", BF16, "dw_q")
    dw_kv = matmul(h, dpkv, "tn", BF16, "dw_kv", add=matmul(hc, dpkv_c, "tn", F32, "dw_kv_ctx"))
    dw_gqk = matmul(h, dpgqk, "tn", BF16, "dw_gqk", add=matmul(hc, dpgqk_c, "tn", F32, "dw_gqk_ctx"))
    dw_gv = matmul(h, dpgv, "tn", BF16, "dw_gv", add=matmul(hc, dpgv_c, "tn", F32, "dw_gv_ctx"))
    dw_rb = matmul(h, dprb, "tn", BF16, "dw_rb")
    dw_lr = matmul(h, dplr, "tn", BF16, "dw_lr", add=matmul(hc, dplr_c, "tn", F32, "dw_lr_ctx"))
    dw_gab = matmul(h, dpgab, "tn", BF16, "dw_gab")
    lrw = P["lowrank"]
    dw_in = [dw_q, dw_kv, dw_gqk, dw_gv, dw_rb, dw_lr[:, :2 * lrw], dw_gab]

    dh = matmul(dpq, W["q"], "nt", F32, "dh_q")
    dh = matmul(dpkv, W["kv"], "nt", F32, "dh_kv", add=dh)
    dh = matmul(dpgqk, W["gqk"], "nt", F32, "dh_gqk", add=dh)
    dh = matmul(dpgv, W["gv"], "nt", F32, "dh_gv", add=dh)
    dh = matmul(dprb, W["rb"], "nt", F32, "dh_rb", add=dh)
    dh = matmul(dplr, W["lr"], "nt", F32, "dh_lr", add=dh)
    dh = matmul(dpgab, W["gab"], "nt", F32, "dh_gab", add=dh)
    dhc = matmul(dpkv_c, W["kv"], "nt", F32, "dhc_kv")
    dhc = matmul(dpgqk_c, W["gqk"], "nt", F32, "dhc_gqk", add=dhc)
    dhc = matmul(dpgv_c, W["gv"], "nt", F32, "dhc_gv", add=dhc)
    dhc = matmul(dplr_c, W["lr"], "nt", F32, "dhc_lr", add=dhc)

    def mod1_bwd_fn(x, dh, dres, g, sc):
        r = _rstd(x)
        n = x * r
        dyy = dh * (1.0 + sc)
        dn = dyy * g
        dx = dres + r * (dn - n * jnp.mean(dn * n, axis=-1, keepdims=True))
        s0 = lambda a: jnp.sum(a, axis=0, keepdims=True)
        return dx, s0(dyy * n), s0(dh), s0(dh * n * g)

    grad_x, dgmix, dsh1, dsc1 = rowwise(mod1_bwd_fn, [x, dh, dx1], [P["g_mix"], sc1], [(D, F32)], [(1, D)] * 3,
                                        128, "mod1_bwd")
    _, dgmix_c, dcsh1, dcsc1 = rowwise(mod1_bwd_fn, [ctx, dhc, jnp.zeros_like(ctx)], [P["g_mix"], csc1], [(D, F32)],
                                       [(1, D)] * 3, 128, "mod1_ctx_bwd")

    zD = jnp.zeros((1, D), F32)
    grads = dict(
        w_in=dw_in, w_attn_o=dw_attn_o, w_gla_o=dw_gla_o, w_out=dw_out, w_up=dw_up, w_down=dw_down,
        dmod_x=jnp.concatenate([dsh1, dsc1, dg1, dsh2, dsc2, dg2], axis=1),
        dmod_c=jnp.concatenate([dcsh1, dcsc1, zD, zD, zD, zD], axis=1),
        g_mix=dgmix + dgmix_c, q_norm=dqn, k_norm=dkn + dkn_c, attn_sink=dsink.reshape(1, hq),
        w_gate_f=(dwgf + dwgf_c)[:lrw], b_gate_f=dbgf + dbgf_c,
        w_gate_b=(dwgb + dwgb_c)[lrw:2 * lrw], b_gate_b=dbgb + dbgb_c,
        gla_norm=dgn, g_ffn=dg_ffn,
        conv_w=jnp.concatenate([dcw_a, dcw_g], axis=2).reshape(3, -1),
        conv_b=jnp.concatenate([dcb_a, dcb_g], axis=1),
    )
    return lsum[0, 0], grad_x, grads


SMALL_REPL = ("c_ctx", "b_mod", "g_mix", "q_norm", "k_norm", "attn_sink", "b_gate_f", "b_gate_b", "gla_norm", "g_ffn",
              "conv_b")
SMALL_SHARD = ("w_gate_f", "w_gate_b", "conv_w")
ORDER = ("c_ctx", "w_mod", "b_mod", "g_mix", "w_in", "q_norm", "k_norm", "attn_sink", "w_gate_f", "b_gate_f",
         "w_gate_b", "b_gate_b", "gla_norm", "w_attn_o", "w_gla_o", "w_out", "g_ffn", "w_up", "conv_w", "conv_b",
         "w_down")


def kernel(x, c, ctx, c_ctx, w_mod, b_mod, g_mix, w_in, q_norm, k_norm, attn_sink, w_gate_f, b_gate_f, w_gate_b, b_gate_b, gla_norm, w_attn_o, w_gla_o, w_out, g_ffn, w_up, conv_w, conv_b, w_down, loss_target, m_c_ctx, m_w_mod, m_b_mod, m_g_mix, m_w_in, m_q_norm, m_k_norm, m_attn_sink, m_w_gate_f, m_b_gate_f, m_w_gate_b, m_b_gate_b, m_gla_norm, m_w_attn_o, m_w_gla_o, m_w_out, m_g_ffn, m_w_up, m_conv_w, m_conv_b, m_w_down, v_c_ctx, v_w_mod, v_b_mod, v_g_mix, v_w_in, v_q_norm, v_k_norm, v_attn_sink, v_w_gate_f, v_b_gate_f, v_w_gate_b, v_b_gate_b, v_gla_norm, v_w_attn_o, v_w_gla_o, v_w_out, v_g_ffn, v_w_up, v_conv_w, v_conv_b, v_w_down):
    loc = dict(locals())
    Wt = {n: loc[n] for n in ORDER}
    Mt = {n: loc["m_" + n] for n in ORDER}
    Vt = {n: loc["v_" + n] for n in ORDER}
    me = 4 * lax.axis_index("x") + 2 * lax.axis_index("y") + lax.axis_index("c")

    D = x.shape[-1]
    hd = q_norm.shape[-1]
    hq = attn_sink.shape[-1]
    gdv = gla_norm.shape[-1]
    gh = D // gdv
    gdk = D // 2 // gh
    lrw = w_gate_f.shape[1]
    in_w = NDEV * w_in.shape[-1]
    kvw = (in_w - hq * hd - 2 * gh * gdk - 2 * gh * gdv - 2 * lrw - 2 * D) // 2
    hkv = kvw // hd
    gcols = w_gate_f.shape[-1]
    F2 = NDEV * w_up.shape[-1]
    mcols = w_mod.shape[-1]

    x2, ctx2, tgt2 = x[0], ctx[0], loss_target[0]

    c_all = exchange([jnp.pad(c, ((0, 7), (0, 0)))], True, "gather_c")[0][:, 0, :]
    c9 = jnp.concatenate([c_all, c_ctx[None, :], jnp.zeros((7, D), F32)], axis=0)
    s9 = rowwise(lambda a: a * _sig(a), [c9], [], [(D, F32)], [], 16, "silu_c")[0]
    bias = jnp.broadcast_to(lax.dynamic_slice_in_dim(b_mod, me * mcols, mcols, axis=1), (16, mcols))
    mod_cols = matmul(s9, w_mod[0], "nn", F32, "mod_cols", add=bias)
    mod_all = exchange([mod_cols], True, "gather_mod")[0]
    mod_all = jnp.transpose(mod_all, (1, 0, 2)).reshape(16, NDEV * mcols)
    mod_me = lax.dynamic_slice_in_dim(mod_all, me, 1, axis=0)
    mod = [mod_me[:, i * D:(i + 1) * D] for i in range(6)]
    modc = [mod_all[8:9, i * D:(i + 1) * D] for i in range(2)]

    o3 = jnp.stack([w_attn_o[0], w_gla_o[0], w_out[0]]).astype(BF16)
    small_w = pack([w_gate_f[0], w_gate_b[0], conv_w[0]])
    g_in, g_o3, g_up, g_down, g_small = gather_two_level(
        [w_in[0].astype(BF16), o3, w_up[0].astype(BF16), w_down[0].astype(BF16), small_w], "gather_w")
    seg = segments_from_blocks(g_in, [hq * hd, 2 * kvw, 2 * gh * gdk, gh * gdv, gh * gdv, 2 * lrw, 2 * D])
    small_parts = [unpack(g_small[j], [w_gate_f[0].shape, w_gate_b[0].shape, conv_w[0].shape]) for j in range(NDEV)]
    wgf = jnp.concatenate([p[0] for p in small_parts], axis=1)
    wgb = jnp.concatenate([p[1] for p in small_parts], axis=1)
    cw_full = jnp.concatenate([p[2] for p in small_parts], axis=1)
    o3f = [g_o3[:, i].reshape(-1, D) for i in range(3)]
    W = dict(
        q=seg[0], kv=seg[1], gqk=seg[2], gv=seg[3], rb=seg[4],
        lr=jnp.pad(seg[5], ((0, 0), (0, 128 - 2 * lrw))), gab=seg[6],
        gate_f=jnp.pad(wgf, ((0, 128 - lrw), (0, 0))),
        gate_b=jnp.pad(wgb, ((lrw, 128 - 2 * lrw), (0, 0))),
        attn_o=o3f[0], gla_o=o3f[1], out=o3f[2],
        up=jnp.concatenate([g_up[j] for j in range(NDEV)], axis=1),
        down=g_down.reshape(-1, D),
        conv_w=cw_full,
    )
    P = dict(hd=hd, hq=hq, hkv=hkv, gh=gh, lowrank=lrw, g_mix=g_mix, q_norm=q_norm, k_norm=k_norm, attn_sink=attn_sink,
             b_gate_f=b_gate_f, b_gate_b=b_gate_b, gla_norm=gla_norm, g_ffn=g_ffn, conv_b=conv_b)

    lsum, grad_x, G = local_step(x2, ctx2, tgt2, mod, modc, W, P)
    loss = lax.psum(0.5 * lsum / D, ("x", "y", "c"))

    dm = exchange([jnp.concatenate([G["dmod_x"], G["dmod_c"], jnp.zeros((6, 6 * D), F32)], axis=0)], True,
                  "gather_dmod")[0]
    dmc = reduce_parts(dm[:, 1:2, :].reshape(NDEV, 6 * D // 128, 128), "sum_dmod_ctx").reshape(1, 6 * D)
    dM = jnp.concatenate([dm[:, 0, :], dmc, jnp.zeros((7, 6 * D), F32)], axis=0)
    dM_cols = lax.dynamic_slice_in_dim(dM, me * mcols, mcols, axis=1)
    g_w_mod = matmul(s9, dM_cols, "tn", F32, "dw_mod")
    g_b_mod = reduce_parts(dM.reshape(16, 6 * D // 128, 128), "sum_db_mod").reshape(1, 6 * D)
    dsc = matmul(dM_cols[8:16], w_mod[0], "nt", F32, "d_silu_ctx")
    cc = jnp.broadcast_to(c_ctx[None, :], (8, D))

    def dsilu_fn(d, a):
        sg = _sig(a)
        return d * sg * (1.0 + a * (1.0 - sg))

    g_cctx_part = rowwise(dsilu_fn, [dsc, cc], [], [(D, F32)], [], 8, "d_c_ctx")[0][0:1]

    small_names = ("c_ctx", "g_mix", "q_norm", "k_norm", "attn_sink", "b_gate_f", "b_gate_b", "gla_norm", "g_ffn",
                   "conv_b", "w_gate_f", "w_gate_b", "conv_w")
    G["c_ctx"] = g_cctx_part
    sm_shapes = [G[n].shape for n in small_names]
    sm_all = exchange([pack([G[n] for n in small_names])], True, "gather_small_grads")[0]
    sm_tot = unpack(reduce_parts(sm_all, "sum_small_grads"), sm_shapes)
    gs = dict(zip(small_names, sm_tot))
    gs["b_mod"] = g_b_mod
    gs["w_gate_f"] = lax.dynamic_slice_in_dim(gs["w_gate_f"], me * gcols, gcols, axis=1)
    gs["w_gate_b"] = lax.dynamic_slice_in_dim(gs["w_gate_b"], me * gcols, gcols, axis=1)
    ccols = conv_w.shape[-1]
    gs["conv_w"] = lax.dynamic_slice_in_dim(gs["conv_w"], me * ccols, ccols, axis=1)

    orows = w_attn_o.shape[1]
    s_in = blocks_from_segments(G["w_in"], w_in.shape[-1])
    s_o3 = jnp.concatenate([rows_to_blocks(G["w_attn_o"]), rows_to_blocks(G["w_gla_o"]), rows_to_blocks(G["w_out"])],
                           axis=1)
    s_up = blocks_from_segments(G["w_up"], w_up.shape[-1])
    s_down = rows_to_blocks(G["w_down"])
    r_in, r_o3, r_up, r_down = scatter_reduce([s_in, s_o3, s_up, s_down], "scatter_grads")

    out = {}
    out["w_in"] = adam_reduce(r_in, w_in[0], m_w_in[0], v_w_in[0], "adam_w_in")
    o3w = jnp.concatenate([w_attn_o[0], w_gla_o[0], w_out[0]], axis=0)
    o3m = jnp.concatenate([m_w_attn_o[0], m_w_gla_o[0], m_w_out[0]], axis=0)
    o3v = jnp.concatenate([v_w_attn_o[0], v_w_gla_o[0], v_w_out[0]], axis=0)
    ro3 = adam_reduce(r_o3, o3w, o3m, o3v, "adam_o3")
    for i, n in enumerate(("w_attn_o", "w_gla_o", "w_out")):
        out[n] = [a[i * orows:(i + 1) * orows] for a in ro3]
    out["w_up"] = adam_reduce(r_up, w_up[0], m_w_up[0], v_w_up[0], "adam_w_up")
    out["w_down"] = adam_reduce(r_down, w_down[0], m_w_down[0], v_w_down[0], "adam_w_down")
    out["w_mod"] = adam_reduce(g_w_mod[None], w_mod[0], m_w_mod[0], v_w_mod[0], "adam_w_mod")
    sm_names = SMALL_REPL + SMALL_SHARD
    shapes = [Wt[n].shape for n in sm_names]
    rs = adam_reduce(pack([gs[n] for n in sm_names])[None], pack([Wt[n] for n in sm_names]),
                     pack([Mt[n] for n in sm_names]), pack([Vt[n] for n in sm_names]), "adam_small")
    rs = [unpack(a, shapes) for a in rs]
    for i, n in enumerate(sm_names):
        out[n] = [a[i] for a in rs]

    res = [loss, grad_x[None]]
    for k in range(4):
        for n in ORDER:
            res.append(out[n][k].reshape(Wt[n].shape))
    return tuple(res)
```

```python
import jax
import jax.numpy as jnp
import numpy as np
from jax import lax
from jax.experimental import pallas as pl
from jax.experimental.pallas import tpu as pltpu

F32 = jnp.float32
BF16 = jnp.bfloat16

NDEV = 8
NCHIP = 4
EPS = 1e-6
WINDOW = 128
BLOCK = 128
GRID_W = 64
ROPE_THETA = 10000.0
GLA_CHUNK = 128
GLA_GATE_NORM = 16.0
ADAM_LR = 0.001
ADAM_B1 = 0.9
ADAM_B2 = 0.999
ADAM_EPS = 1e-08
ADAM_WD = 0.01
ADAM_STEP = 10
V7X_VMEM_LIMIT = 56 * 1024 * 1024
MATMUL_VMEM_BUDGET = 40 * 1024 * 1024
NEG = -1e30

NN = ((1,), (0,))
NT = ((1,), (1,))
TN = ((0,), (0,))


def _dot(a, b, dims):
    return lax.dot_general(a, b, (dims, ((), ())), preferred_element_type=F32)


def _cp(sem):
    return pltpu.CompilerParams(dimension_semantics=sem, vmem_limit_bytes=V7X_VMEM_LIMIT)


def _pick(n, cands):
    for c in cands:
        if n % c == 0:
            return c
    return n


def _sig(x):
    return 1.0 / (1.0 + jnp.exp(-x))


def _sig_tanh(x):
    return 0.5 * jnp.tanh(0.5 * x) + 0.5


def _rstd(x):
    return lax.rsqrt(jnp.mean(x * x, axis=-1, keepdims=True) + EPS)


_ANY = pl.BlockSpec(memory_space=pl.ANY)


def _place():
    return lax.axis_index("x"), lax.axis_index("y"), lax.axis_index("c")


def exchange(srcs, bcast, name, group="all"):
    n = len(srcs)
    ndev = NDEV if group == "all" else NCHIP
    ks = tuple(range(1, NDEV)) if group == "all" else (2, 4, 6)
    out_shape = [jax.ShapeDtypeStruct((ndev,) + (s.shape if bcast else s.shape[1:]), s.dtype) for s in srcs]

    def body(*refs):
        src, dst = refs[:n], refs[n:2 * n]
        send_sems, recv_sems, loc_sems = refs[2 * n:]
        x, y, c = _place()

        def idx(px, py, pc):
            return 4 * px + 2 * py + pc if group == "all" else 2 * px + py

        me = idx(x, y, c)
        copies = []
        for a in range(n):
            cp = pltpu.make_async_copy(src[a] if bcast else src[a].at[me], dst[a].at[me], loc_sems.at[a])
            cp.start()
            copies.append(cp)
        for s, k in enumerate(ks):
            px, py, pc = x ^ ((k >> 2) & 1), y ^ ((k >> 1) & 1), c ^ (k & 1)
            for a in range(n):
                cp = pltpu.make_async_remote_copy(
                    src_ref=src[a] if bcast else src[a].at[idx(px, py, pc)],
                    dst_ref=dst[a].at[me],
                    send_sem=send_sems.at[a, s],
                    recv_sem=recv_sems.at[a, s],
                    device_id=(px, py, pc),
                    device_id_type=pl.DeviceIdType.MESH,
                )
                cp.start()
                copies.append(cp)
        for cp in copies:
            cp.wait()

    return pl.pallas_call(
        body,
        out_shape=out_shape,
        in_specs=[_ANY] * n,
        out_specs=[_ANY] * n,
        scratch_shapes=[
            pltpu.SemaphoreType.DMA((n, len(ks))),
            pltpu.SemaphoreType.DMA((n, len(ks))),
            pltpu.SemaphoreType.DMA((n,)),
        ],
        name=name,
    )(*srcs)


def gather_two_level(srcs, name):
    n = len(srcs)
    out_shape = [jax.ShapeDtypeStruct((NDEV,) + s.shape, s.dtype) for s in srcs]

    def body(*refs):
        src, dst = refs[:n], refs[n:2 * n]
        send_sems, recv_sems, loc_sems = refs[2 * n:]
        x, y, c = _place()
        me = 4 * x + 2 * y + c
        sib = (x, y, 1 - c)
        first = (x ^ (1 - c), y ^ c)
        second = (x ^ c, y ^ (1 - c))
        diag = (x ^ 1, y ^ 1)

        def row(chip, core):
            return 4 * chip[0] + 2 * chip[1] + core

        def copy(a, s, block, to, from_src=False):
            return pltpu.make_async_remote_copy(
                src_ref=src[a] if from_src else dst[a].at[block], dst_ref=dst[a].at[block],
                send_sem=send_sems.at[a, s], recv_sem=recv_sems.at[a, s],
                device_id=to, device_id_type=pl.DeviceIdType.MESH)

        local = [pltpu.make_async_copy(src[a], dst[a].at[me], loc_sems.at[a]) for a in range(n)]
        sent = [copy(a, 0, me, sib, True) for a in range(n)]
        sent += [copy(a, 1, me, (*first, c), True) for a in range(n)]
        sent += [copy(a, 2, me, (*second, c), True) for a in range(n)]
        for cp in local + sent:
            cp.start()
        for a in range(n):
            copy(a, 1, row(first, c), (*first, c)).wait_recv()
            for cp in (copy(a, 3, row(first, c), (*second, c)), copy(a, 5, row(first, c), sib)):
                cp.start()
                sent.append(cp)
        for a in range(n):
            copy(a, 2, row(second, c), (*second, c)).wait_recv()
            cp = copy(a, 4, row(second, c), sib)
            cp.start()
            sent.append(cp)
        for a in range(n):
            copy(a, 3, row(diag, c), (*second, c)).wait_recv()
            cp = copy(a, 6, row(diag, c), sib)
            cp.start()
            sent.append(cp)
        for a in range(n):
            copy(a, 0, row((x, y), 1 - c), sib).wait_recv()
            copy(a, 4, row(first, 1 - c), sib).wait_recv()
            copy(a, 5, row(second, 1 - c), sib).wait_recv()
            copy(a, 6, row(diag, 1 - c), sib).wait_recv()
        for cp in local:
            cp.wait()
        for cp in sent:
            cp.wait_send()

    return pl.pallas_call(
        body,
        out_shape=out_shape,
        in_specs=[_ANY] * n,
        out_specs=[_ANY] * n,
        scratch_shapes=[
            pltpu.SemaphoreType.DMA((n, NDEV - 1)),
            pltpu.SemaphoreType.DMA((n, NDEV - 1)),
            pltpu.SemaphoreType.DMA((n,)),
        ],
        name=name,
    )(*srcs)


def _chip_across(core, da, db):
    x, y, _ = _place()
    return x ^ (da * (1 - core) + db * core), y ^ (db * (1 - core) + da * core)


def pair_swap(srcs, name, axis="c"):
    n = len(srcs)

    def body(*refs):
        src, dst = refs[:n], refs[n:2 * n]
        send_sems, recv_sems = refs[2 * n:]
        x, y, c = _place()
        partner = {"c": (x, y, 1 - c), "first": (*_chip_across(c, 1, 0), c), "second": (*_chip_across(c, 0, 1), c)}[axis]
        copies = []
        for a in range(n):
            cp = pltpu.make_async_remote_copy(
                src_ref=src[a], dst_ref=dst[a], send_sem=send_sems.at[a], recv_sem=recv_sems.at[a],
                device_id=partner, device_id_type=pl.DeviceIdType.MESH)
            cp.start()
            copies.append(cp)
        for cp in copies:
            cp.wait()

    return pl.pallas_call(
        body,
        out_shape=[jax.ShapeDtypeStruct(s.shape, s.dtype) for s in srcs],
        in_specs=[_ANY] * n,
        out_specs=[_ANY] * n,
        scratch_shapes=[pltpu.SemaphoreType.DMA((n,)), pltpu.SemaphoreType.DMA((n,))],
        name=name,
    )(*srcs)


def scatter_reduce(blocks, name):
    c = lax.axis_index("c")
    offsets = ((0, 0), (0, 1), (1, 0), (1, 1))

    def bound_for(core):
        res = []
        for b in blocks:
            for da, db in offsets:
                px, py = _chip_across(core, da, db)
                res.append(lax.dynamic_index_in_dim(b, 4 * px + 2 * py + core, axis=0, keepdims=False))
        return res

    def add(n_out, ins, label):
        fn = lambda *a: [a[i].astype(F32) + a[n_out + i].astype(F32) for i in range(n_out)]
        rows, cols = ins[0].shape
        return rowwise(fn, ins, [], [(cols, ins[0].dtype)] * n_out, [], _pick(rows, (256, 128, 64)), label)

    nb = len(blocks)
    mine, theirs = bound_for(c), bound_for(1 - c)
    got = list(pair_swap(theirs, name + "_d2d", "c"))
    q = [add(4, mine[4 * i:4 * i + 4] + got[4 * i:4 * i + 4], f"{name}_sum0_{i}") for i in range(nb)]
    r1 = pair_swap([q[i][j] for i in range(nb) for j in (2, 3)], name + "_ici1", "first")
    k = [add(2, [q[i][0], q[i][1], r1[2 * i], r1[2 * i + 1]], f"{name}_sum1_{i}") for i in range(nb)]
    r2 = pair_swap([k[i][1] for i in range(nb)], name + "_ici2", "second")
    return [jnp.stack([k[i][0], r2[i]]) for i in range(nb)]


def matmul(a, b, mode, out_dtype, name, add=None):
    if mode == "nn":
        (M, K), N = a.shape, b.shape[1]
    elif mode == "nt":
        (M, K), N = a.shape, b.shape[0]
    else:
        (K, M), N = a.shape, b.shape[1]
    tm = _pick(M, (1024, 512, 256, 128))
    tn = _pick(N, (1024, 512, 256, 128))
    osz = jnp.dtype(out_dtype).itemsize

    def vmem_bytes(tk):
        ops = 2 * tk * (tm * a.dtype.itemsize + tn * b.dtype.itemsize)
        return ops + tm * tn * (2 * osz + (4 if tk < K else 0) + (8 if add is not None else 0))

    tk = next((t for t in (K, 2816, 2048, 1408, 1024, 512, 256, 128) if K % t == 0 and vmem_bytes(t) <= MATMUL_VMEM_BUDGET), K)
    nk = K // tk
    dims = {"nn": NN, "nt": NT, "tn": TN}[mode]

    def body(*refs):
        if add is None:
            a_ref, b_ref, o_ref = refs[:3]
            c_ref = None
        else:
            a_ref, b_ref, c_ref, o_ref = refs[:4]

        def prod():
            return _dot(a_ref[...].astype(BF16), b_ref[...].astype(BF16), dims)

        def finish(r):
            if c_ref is not None:
                r = r + c_ref[...].astype(F32)
            o_ref[...] = r.astype(o_ref.dtype)

        if nk == 1:
            finish(prod())
            return
        acc = refs[-1]
        k = pl.program_id(2)

        @pl.when(k == 0)
        def _():
            acc[...] = prod()

        if nk > 2:
            @pl.when((k > 0) & (k < nk - 1))
            def _():
                acc[...] += prod()

        @pl.when(k == nk - 1)
        def _():
            finish(acc[...] + prod())

    a_spec = pl.BlockSpec((tk, tm), lambda i, j, k: (k, i)) if mode == "tn" else pl.BlockSpec((tm, tk), lambda i, j, k: (i, k))
    b_spec = pl.BlockSpec((tn, tk), lambda i, j, k: (j, k)) if mode == "nt" else pl.BlockSpec((tk, tn), lambda i, j, k: (k, j))
    o_spec = pl.BlockSpec((tm, tn), lambda i, j, k: (i, j))
    ins, specs = [a, b], [a_spec, b_spec]
    if add is not None:
        ins.append(add)
        specs.append(o_spec)
    return pl.pallas_call(
        body,
        grid=(M // tm, N // tn, nk),
        in_specs=specs,
        out_specs=o_spec,
        out_shape=jax.ShapeDtypeStruct((M, N), out_dtype),
        scratch_shapes=[pltpu.VMEM((tm, tn), F32)] if nk > 1 else [],
        compiler_params=_cp(("parallel", "parallel", "arbitrary")),
        name=name,
    )(*ins)


def rowwise(fn, tiled, full, out_tiled, out_acc, tile, name):
    tiled = [t if isinstance(t, tuple) else (t, t.shape[1], 0) for t in tiled]
    rows = tiled[0][0].shape[0]
    tile = min(tile, rows)
    assert rows % tile == 0
    nt, nf, no = len(tiled), len(full), len(out_tiled)

    def body(*refs):
        ins = [r[...] for r in refs[:nt + nf]]
        res = fn(*ins)
        if not isinstance(res, (tuple, list)):
            res = (res,)
        outs = refs[nt + nf:]
        for r, v in zip(outs[:no], res[:no]):
            r[...] = v.astype(r.dtype)
        if out_acc:
            @pl.when(pl.program_id(0) == 0)
            def _():
                for r in outs[no:]:
                    r[...] = jnp.zeros_like(r)

            for r, v in zip(outs[no:], res[no:]):
                r[...] += v

    in_specs = [pl.BlockSpec((tile, w), lambda i, cb=cb: (i, cb)) for (_, w, cb) in tiled]
    in_specs += [pl.BlockSpec(f.shape, lambda i, nd=f.ndim: (0,) * nd) for f in full]
    out_specs = [pl.BlockSpec((tile, w), lambda i: (i, 0)) for (w, _) in out_tiled]
    out_specs += [pl.BlockSpec(s, lambda i, nd=len(s): (0,) * nd) for s in out_acc]
    out_shape = [jax.ShapeDtypeStruct((rows, w), dt) for (w, dt) in out_tiled]
    out_shape += [jax.ShapeDtypeStruct(s, F32) for s in out_acc]
    res = pl.pallas_call(
        body,
        grid=(rows // tile,),
        in_specs=in_specs,
        out_specs=out_specs,
        out_shape=out_shape,
        compiler_params=_cp(("arbitrary",) if out_acc else ("parallel",)),
        name=name,
    )(*[t[0] for t in tiled], *full)
    return res


def adam_reduce(parts, w, m, v, name):
    P, R, C = parts.shape
    tr = _pick(R, (64, 32, 16, 8))
    c1 = 1.0 - ADAM_B1 ** ADAM_STEP
    c2 = 1.0 - ADAM_B2 ** ADAM_STEP

    def body(p_ref, w_ref, m_ref, v_ref, g_ref, d_ref, nm_ref, nv_ref):
        g = p_ref[0].astype(F32)
        for j in range(1, P):
            g = g + p_ref[j].astype(F32)
        mm = ADAM_B1 * m_ref[...] + (1.0 - ADAM_B1) * g
        vv = ADAM_B2 * v_ref[...] + (1.0 - ADAM_B2) * (g * g)
        m_hat = mm / c1
        v_hat = vv / c2
        g_ref[...] = g
        d_ref[...] = -ADAM_LR * (m_hat / (jnp.sqrt(v_hat) + ADAM_EPS) + ADAM_WD * w_ref[...])
        nm_ref[...] = mm
        nv_ref[...] = vv

    spec = pl.BlockSpec((tr, C), lambda i: (i, 0))
    return pl.pallas_call(
        body,
        grid=(R // tr,),
        in_specs=[pl.BlockSpec((P, tr, C), lambda i: (0, i, 0)), spec, spec, spec],
        out_specs=[spec] * 4,
        out_shape=[jax.ShapeDtypeStruct((R, C), F32)] * 4,
        compiler_params=_cp(("parallel",)),
        name=name,
    )(parts, w, m, v)


def reduce_parts(parts, name):
    P, R, C = parts.shape
    tr = _pick(R, (64, 32, 16, 8))

    def body(p_ref, g_ref):
        g = p_ref[0]
        for j in range(1, P):
            g = g + p_ref[j]
        g_ref[...] = g

    return pl.pallas_call(
        body,
        grid=(R // tr,),
        in_specs=[pl.BlockSpec((P, tr, C), lambda i: (0, i, 0))],
        out_specs=pl.BlockSpec((tr, C), lambda i: (i, 0)),
        out_shape=jax.ShapeDtypeStruct((R, C), F32),
        compiler_params=_cp(("parallel",)),
        name=name,
    )(parts)


def pack(arrs):
    flat = jnp.concatenate([a.reshape(-1).astype(F32) for a in arrs])
    n = flat.shape[0]
    padded = -(-n // 1024) * 1024
    return jnp.pad(flat, (0, padded - n)).reshape(padded // 128, 128)


def blocks_from_segments(segs, ncols):
    offs = np.cumsum([0] + [s.shape[1] for s in segs]).tolist()
    blocks = []
    for j in range(NDEV):
        lo, hi = j * ncols, (j + 1) * ncols
        parts = [s[:, max(lo, o) - o:min(hi, o + s.shape[1]) - o]
                 for s, o in zip(segs, offs[:-1]) if max(lo, o) < min(hi, o + s.shape[1])]
        blocks.append(jnp.concatenate(parts, axis=1) if len(parts) > 1 else parts[0])
    return jnp.stack(blocks)


def rows_to_blocks(g):
    return g.reshape(NDEV, -1, g.shape[1])


def segments_from_blocks(g, widths):
    ncols = g.shape[2]
    offs = np.cumsum([0] + list(widths)).tolist()
    out = []
    for o, w in zip(offs[:-1], widths):
        parts = [g[j][:, max(j * ncols, o) - j * ncols:min((j + 1) * ncols, o + w) - j * ncols]
                 for j in range(NDEV) if max(j * ncols, o) < min((j + 1) * ncols, o + w)]
        out.append(jnp.concatenate(parts, axis=1) if len(parts) > 1 else parts[0])
    return out


def unpack(slab, shapes):
    flat = slab.reshape(-1)
    out, off = [], 0
    for s in shapes:
        size = int(np.prod(s))
        out.append(flat[off:off + size].reshape(s))
        off += size
    return out


def modulate_fwd(x, g, sh, sc, name):
    def fn(x, g, sh, sc):
        return x * _rstd(x) * g * (1.0 + sc) + sh

    return rowwise(fn, [x], [g, sh, sc], [(x.shape[1], BF16)], [], 256, name)[0]


def norm_rope_fwd(p, width, cb, w, cosf, sinf, hd, name):
    nh = width // hd

    def fn(x, cosf, sinf, w):
        outs = []
        for h in range(nh):
            xh = x[:, h * hd:(h + 1) * hd]
            y = xh * _rstd(xh) * w
            outs.append(y * cosf + pltpu.roll(y, hd // 2, 1) * sinf)
        return jnp.concatenate(outs, axis=1) if nh > 1 else outs[0]

    return rowwise(fn, [(p, width, cb), cosf, sinf], [w], [(width, BF16)], [], 256, name)[0]


def norm_rope_bwd(p, width, cb, d, w, cosf, sinf, hd, name):
    nh = width // hd

    def fn(x, d, cosf, sinf, w):
        outs = []
        dw = jnp.zeros((1, hd), F32)
        for h in range(nh):
            xh = x[:, h * hd:(h + 1) * hd]
            dh = d[:, h * hd:(h + 1) * hd].astype(F32)
            r = _rstd(xh)
            n = xh * r
            dy = dh * cosf + pltpu.roll(dh * sinf, hd // 2, 1)
            dw = dw + jnp.sum(dy * n, axis=0, keepdims=True)
            dn = dy * w
            outs.append(r * (dn - n * jnp.mean(dn * n, axis=-1, keepdims=True)))
        return (jnp.concatenate(outs, axis=1) if nh > 1 else outs[0]), dw

    return rowwise(fn, [(p, width, cb), d, cosf, sinf], [w], [(width, BF16)], [(1, hd)], 256, name)


def attention_fwd(qr, kr, pkv, kcr, pkv_c, sink, hkv, hd, name):
    T, L = qr.shape[0], kcr.shape[0]
    G = qr.shape[1] // (hkv * hd)
    nb = T // BLOCK
    scale = hd ** -0.5

    def body(q_ref, kp, kc, kn, vp, vc, vn, ck_ref, cv_ref, sink_ref, o_ref, lse_ref, lser_ref):
        i = pl.program_id(1)
        kwin = jnp.concatenate([kp[...], kc[...], kn[...]], axis=0)
        vwin = jnp.concatenate([vp[...], vc[...], vn[...]], axis=0).astype(BF16)
        ck, cv = ck_ref[...], cv_ref[...].astype(BF16)
        row = lax.broadcasted_iota(jnp.int32, (BLOCK, 3 * BLOCK), 0)
        col = lax.broadcasted_iota(jnp.int32, (BLOCK, 3 * BLOCK), 1)
        rel = col - BLOCK - row
        valid = (jnp.abs(rel) <= WINDOW) & ((col >= BLOCK) | (i > 0)) & ((col < 2 * BLOCK) | (i < nb - 1))
        R = range(G)
        qa = q_ref[...]
        qs = [qa[:, g * hd:(g + 1) * hd] for g in R]
        sks = [sink_ref[g] for g in R]
        ss = [jnp.where(valid, _dot(qs[g], kwin, NT) * scale, NEG) for g in R]
        scs = [_dot(qs[g], ck, NT) * scale for g in R]
        ms = [jnp.maximum(jnp.maximum(jnp.max(ss[g], axis=1, keepdims=True), jnp.max(scs[g], axis=1, keepdims=True)),
                          sks[g]) for g in R]
        ps = [jnp.exp(ss[g] - ms[g]) for g in R]
        pcs = [jnp.exp(scs[g] - ms[g]) for g in R]
        nums = [_dot(ps[g].astype(BF16), vwin, NN) + _dot(pcs[g].astype(BF16), cv, NN) for g in R]
        dens = [jnp.exp(sks[g] - ms[g]) + jnp.sum(ps[g], axis=1, keepdims=True) + jnp.sum(pcs[g], axis=1, keepdims=True)
                for g in R]
        o_ref[...] = jnp.concatenate([(nums[g] / dens[g]).astype(o_ref.dtype) for g in R], axis=1)
        eye = (lax.broadcasted_iota(jnp.int32, (BLOCK, BLOCK), 0)
               == lax.broadcasted_iota(jnp.int32, (BLOCK, BLOCK), 1)).astype(F32)
        for g in R:
            lg = ms[g] + jnp.log(dens[g])
            lse_ref[g] = lg
            lser_ref[g] = jnp.sum(lg * eye, axis=0, keepdims=True)

    kv_specs = [
        pl.BlockSpec((BLOCK, hd), lambda h, i: (jnp.maximum(i - 1, 0), h)),
        pl.BlockSpec((BLOCK, hd), lambda h, i: (i, h)),
        pl.BlockSpec((BLOCK, hd), lambda h, i: (jnp.minimum(i + 1, nb - 1), h)),
    ]
    v_specs = [
        pl.BlockSpec((BLOCK, hd), lambda h, i: (jnp.maximum(i - 1, 0), hkv + h)),
        pl.BlockSpec((BLOCK, hd), lambda h, i: (i, hkv + h)),
        pl.BlockSpec((BLOCK, hd), lambda h, i: (jnp.minimum(i + 1, nb - 1), hkv + h)),
    ]
    return pl.pallas_call(
        body,
        grid=(hkv, nb),
        in_specs=[pl.BlockSpec((BLOCK, G * hd), lambda h, i: (i, h))] + kv_specs + v_specs + [
            pl.BlockSpec((L, hd), lambda h, i: (0, h)),
            pl.BlockSpec((L, hd), lambda h, i: (0, hkv + h)),
            pl.BlockSpec((G, 1, 1), lambda h, i: (h, 0, 0)),
        ],
        out_specs=[
            pl.BlockSpec((BLOCK, G * hd), lambda h, i: (i, h)),
            pl.BlockSpec((G, BLOCK, 1), lambda h, i: (h, i, 0)),
            pl.BlockSpec((G, 1, BLOCK), lambda h, i: (h, 0, i)),
        ],
        out_shape=[jax.ShapeDtypeStruct(qr.shape, BF16), jax.ShapeDtypeStruct((hkv * G, T, 1), F32),
                   jax.ShapeDtypeStruct((hkv * G, 1, T), F32)],
        compiler_params=_cp(("parallel", "parallel")),
        name=name,
    )(qr, kr, kr, kr, pkv, pkv, pkv, kcr, pkv_c, sink)


def attention_bwd_q(qr, kr, pkv, kcr, pkv_c, sink, do, o, lse, hkv, hd, name):
    T, L = qr.shape[0], kcr.shape[0]
    G = qr.shape[1] // (hkv * hd)
    nb = T // BLOCK
    scale = hd ** -0.5

    def body(q_ref, kp, kc, kn, vp, vc, vn, ck_ref, cv_ref, sink_ref, do_ref, o_ref, lse_ref,
             dq_ref, dck_ref, dcv_ref, dsink_ref, drr_ref):
        i = pl.program_id(1)

        @pl.when(i == 0)
        def _():
            dck_ref[...] = jnp.zeros_like(dck_ref)
            dcv_ref[...] = jnp.zeros_like(dcv_ref)
            dsink_ref[...] = jnp.zeros_like(dsink_ref)

        kwin = jnp.concatenate([kp[...], kc[...], kn[...]], axis=0)
        vwin = jnp.concatenate([vp[...], vc[...], vn[...]], axis=0).astype(BF16)
        ck, cv = ck_ref[...], cv_ref[...].astype(BF16)
        row = lax.broadcasted_iota(jnp.int32, (BLOCK, 3 * BLOCK), 0)
        col = lax.broadcasted_iota(jnp.int32, (BLOCK, 3 * BLOCK), 1)
        rel = col - BLOCK - row
        valid = (jnp.abs(rel) <= WINDOW) & ((col >= BLOCK) | (i > 0)) & ((col < 2 * BLOCK) | (i < nb - 1))
        R = range(G)
        qa, doa, oa = q_ref[...], do_ref[...], o_ref[...]
        qs = [qa[:, g * hd:(g + 1) * hd] for g in R]
        dos = [doa[:, g * hd:(g + 1) * hd] for g in R]
        lgs = [lse_ref[g] for g in R]
        sks = [sink_ref[g] for g in R]
        ss = [jnp.where(valid, _dot(qs[g], kwin, NT) * scale, NEG) for g in R]
        scs = [_dot(qs[g], ck, NT) * scale for g in R]
        dps = [_dot(dos[g], vwin, NT) for g in R]
        dpcs = [_dot(dos[g], cv, NT) for g in R]
        drs = [jnp.sum(dos[g].astype(F32) * oa[:, g * hd:(g + 1) * hd].astype(F32), axis=1, keepdims=True) for g in R]
        ps = [jnp.exp(ss[g] - lgs[g]) for g in R]
        pcs = [jnp.exp(scs[g] - lgs[g]) for g in R]
        dss = [(ps[g] * (dps[g] - drs[g]) * scale).astype(BF16) for g in R]
        dscs = [(pcs[g] * (dpcs[g] - drs[g]) * scale).astype(BF16) for g in R]
        dqs = [_dot(dss[g], kwin, NN) + _dot(dscs[g], ck, NN) for g in R]
        dcks = [_dot(dscs[g], qs[g], TN) for g in R]
        dcvs = [_dot(pcs[g].astype(BF16), dos[g], TN) for g in R]
        dq_ref[...] = jnp.concatenate(dqs, axis=1)
        dck_ref[...] += (dcks[0] + dcks[1]) + (dcks[2] + dcks[3]) if G == 4 else sum(dcks[1:], dcks[0])
        dcv_ref[...] += (dcvs[0] + dcvs[1]) + (dcvs[2] + dcvs[3]) if G == 4 else sum(dcvs[1:], dcvs[0])
        eye = (lax.broadcasted_iota(jnp.int32, (BLOCK, BLOCK), 0)
               == lax.broadcasted_iota(jnp.int32, (BLOCK, BLOCK), 1)).astype(F32)
        for g in R:
            dsink_ref[g] += -jnp.sum(jnp.exp(sks[g] - lgs[g]) * drs[g], axis=0, keepdims=True)
            drr_ref[g] = jnp.sum(drs[g] * eye, axis=0, keepdims=True)

    kv_specs = [
        pl.BlockSpec((BLOCK, hd), lambda h, i: (jnp.maximum(i - 1, 0), h)),
        pl.BlockSpec((BLOCK, hd), lambda h, i: (i, h)),
        pl.BlockSpec((BLOCK, hd), lambda h, i: (jnp.minimum(i + 1, nb - 1), h)),
    ]
    v_specs = [
        pl.BlockSpec((BLOCK, hd), lambda h, i: (jnp.maximum(i - 1, 0), hkv + h)),
        pl.BlockSpec((BLOCK, hd), lambda h, i: (i, hkv + h)),
        pl.BlockSpec((BLOCK, hd), lambda h, i: (jnp.minimum(i + 1, nb - 1), hkv + h)),
    ]
    qspec = pl.BlockSpec((BLOCK, G * hd), lambda h, i: (i, h))
    return pl.pallas_call(
        body,
        grid=(hkv, nb),
        in_specs=[qspec] + kv_specs + v_specs + [
            pl.BlockSpec((L, hd), lambda h, i: (0, h)),
            pl.BlockSpec((L, hd), lambda h, i: (0, hkv + h)),
            pl.BlockSpec((G, 1, 1), lambda h, i: (h, 0, 0)),
            qspec, qspec,
            pl.BlockSpec((G, BLOCK, 1), lambda h, i: (h, i, 0)),
        ],
        out_specs=[
            qspec,
            pl.BlockSpec((L, hd), lambda h, i: (0, h)),
            pl.BlockSpec((L, hd), lambda h, i: (0, h)),
            pl.BlockSpec((G, 1, 1), lambda h, i: (h, 0, 0)),
            pl.BlockSpec((G, 1, BLOCK), lambda h, i: (h, 0, i)),
        ],
        out_shape=[
            jax.ShapeDtypeStruct(qr.shape, F32),
            jax.ShapeDtypeStruct((L, hkv * hd), F32),
            jax.ShapeDtypeStruct((L, hkv * hd), F32),
            jax.ShapeDtypeStruct((hkv * G, 1, 1), F32),
            jax.ShapeDtypeStruct((hkv * G, 1, T), F32),
        ],
        compiler_params=_cp(("parallel", "arbitrary")),
        name=name,
    )(qr, kr, kr, kr, pkv, pkv, pkv, kcr, pkv_c, sink, do, o, lse)


def attention_bwd_kv(qr, kr, pkv, do, lse_row, dr_row, hkv, hd, name):
    T = qr.shape[0]
    G = qr.shape[1] // (hkv * hd)
    nb = T // BLOCK
    scale = hd ** -0.5

    def body(k_ref, v_ref, *refs):
        qs, dos, lses, drs = refs[0:3], refs[3:6], refs[6:9], refs[9:12]
        dk_ref, dv_ref = refs[12:]
        j = pl.program_id(1)
        k = k_ref[...]
        v = v_ref[...].astype(BF16)
        row = lax.broadcasted_iota(jnp.int32, (BLOCK, BLOCK), 0)
        col = lax.broadcasted_iota(jnp.int32, (BLOCK, BLOCK), 1)
        bias = []
        for d in range(3):
            iq = j + d - 1
            rel = row - col - (d - 1) * BLOCK
            valid = (jnp.abs(rel) <= WINDOW) & (iq >= 0) & (iq < nb)
            bias += [jnp.where(valid, 0.0, NEG)] * G
        bias = jnp.concatenate(bias, axis=1)

        def stack(refs):
            vals = [r[...] for r in refs]
            return jnp.concatenate([a[:, g * hd:(g + 1) * hd] for a in vals for g in range(G)], axis=0)

        q, dob = stack(qs), stack(dos)
        lrow = jnp.concatenate([r[g] for r in lses for g in range(G)], axis=1)
        drow = jnp.concatenate([r[g] for r in drs for g in range(G)], axis=1)
        st = _dot(k, q, NT) * scale + bias
        pt = jnp.exp(st - lrow)
        dpt = _dot(v, dob, NT)
        dst = (pt * (dpt - drow) * scale).astype(BF16)
        dk_ref[...] = _dot(dst, q, NN).astype(dk_ref.dtype)
        dv_ref[...] = _dot(pt.astype(BF16), dob, NN).astype(dv_ref.dtype)

    def q3(width_block):
        return [
            pl.BlockSpec(width_block, lambda h, j: (jnp.maximum(j - 1, 0), h)),
            pl.BlockSpec(width_block, lambda h, j: (j, h)),
            pl.BlockSpec(width_block, lambda h, j: (jnp.minimum(j + 1, nb - 1), h)),
        ]

    row3 = [
        pl.BlockSpec((G, 1, BLOCK), lambda h, j: (h, 0, jnp.maximum(j - 1, 0))),
        pl.BlockSpec((G, 1, BLOCK), lambda h, j: (h, 0, j)),
        pl.BlockSpec((G, 1, BLOCK), lambda h, j: (h, 0, jnp.minimum(j + 1, nb - 1))),
    ]
    qb = (BLOCK, G * hd)
    return pl.pallas_call(
        body,
        grid=(hkv, nb),
        in_specs=[pl.BlockSpec((BLOCK, hd), lambda h, j: (j, h)), pl.BlockSpec((BLOCK, hd), lambda h, j: (j, hkv + h))]
        + q3(qb) + q3(qb) + row3 + row3,
        out_specs=[pl.BlockSpec((BLOCK, hd), lambda h, j: (j, h))] * 2,
        out_shape=[jax.ShapeDtypeStruct((T, hkv * hd), BF16)] * 2,
        compiler_params=_cp(("parallel", "parallel")),
        name=name,
    )(kr, pkv, qr, qr, qr, do, do, do, lse_row, lse_row, lse_row, dr_row, dr_row, dr_row)


def gate_fwd(plr, wf, wb, bf, bb, name):
    n = wf.shape[1]

    def fn(lr, wf, wb, bf, bb):
        lrb = lr.astype(BF16)
        outs = []
        for w, b in ((wf, bf), (wb, bb)):
            z = _dot(lrb, w.astype(BF16), NN) + b
            outs.append((jnp.minimum(z, 0.0) - jnp.log(1.0 + jnp.exp(-jnp.abs(z)))) / GLA_GATE_NORM)
        return outs

    return rowwise(fn, [plr], [wf, wb, bf, bb], [(n, F32), (n, F32)], [], 256, name)


def gate_bwd(plr, dgf, dgb, wf, wb, bf, bb, name):
    n = wf.shape[1]

    def fn(lr, dgf, dgb, wf, wb, bf, bb):
        lrb = lr.astype(BF16)
        dlr = jnp.zeros(lr.shape, F32)
        res = []
        for w, b, dg in ((wf, bf, dgf), (wb, bb, dgb)):
            wb16 = w.astype(BF16)
            z = _dot(lrb, wb16, NN) + b
            dz = dg * _sig(-z) / GLA_GATE_NORM
            dzb = dz.astype(BF16)
            dlr = dlr + _dot(dzb, wb16, NT)
            res += [_dot(lrb, dzb, TN), jnp.sum(dz, axis=0, keepdims=True)]
        return [dlr] + res

    return rowwise(fn, [plr, dgf, dgb], [wf, wb, bf, bb], [(128, BF16)],
                   [(128, n), (1, n), (128, n), (1, n)], 256, name)


def _tri_dot(tri_b, x):
    x1 = x.astype(BF16)
    r1 = x - x1.astype(F32)
    x2 = r1.astype(BF16)
    x3 = (r1 - x2.astype(F32)).astype(BF16)
    return _dot(tri_b, x1, NN) + _dot(tri_b, x2, NN) + _dot(tri_b, x3, NN)


def gla_fwd(pqk, pv, gl, s0, heads, reverse, name, o_add=None):
    T = pqk.shape[0]
    dk = pqk.shape[1] // (2 * heads)
    dv = pv.shape[1] // heads
    C = GLA_CHUNK
    nc = T // C
    qscale = dk ** -0.5

    def body(*refs):
        if o_add is None:
            q_ref, k_ref, v_ref, g_ref, s0_ref, o_ref, st_ref, sf_ref, S = refs
            oa_ref = None
        else:
            q_ref, k_ref, v_ref, g_ref, s0_ref, oa_ref, o_ref, st_ref, sf_ref, S = refs
        n = pl.program_id(0)

        @pl.when(n == 0)
        def _():
            S[...] = s0_ref[...]

        r = lax.broadcasted_iota(jnp.int32, (C, C), 0)
        c = lax.broadcasted_iota(jnp.int32, (C, C), 1)
        tri = (r <= c) if reverse else (r >= c)
        trib = tri.astype(BF16)
        ga, qa, ka, va = g_ref[...], q_ref[...], k_ref[...], v_ref[...]
        sts = [S[h] for h in range(heads)]
        H = range(heads)
        gs = [ga[:, h * dk:(h + 1) * dk] for h in H]
        bs = [_tri_dot(trib, g) for g in gs]
        bls = [jnp.sum(g, axis=0, keepdims=True) for g in gs]
        mid = lax.broadcasted_iota(jnp.int32, (C, 1), 0) == C // 2
        bms = [jnp.sum(jnp.where(mid, b, 0.0), axis=0, keepdims=True) for b in bs]
        vs = [va[:, h * dv:(h + 1) * dv].astype(BF16) for h in H]
        qs = [qa[:, h * dk:(h + 1) * dk].astype(F32) * qscale for h in H]
        qes = [(qs[h] * jnp.exp(bs[h])).astype(BF16) for h in H]
        qms = [(qs[h] * jnp.exp(bs[h] - bms[h])).astype(BF16) for h in H]
        kms = [(ka[:, h * dk:(h + 1) * dk].astype(F32) * jnp.exp(bms[h] - bs[h])).astype(BF16) for h in H]
        kls = [(ka[:, h * dk:(h + 1) * dk].astype(F32) * jnp.exp(bls[h] - bs[h])).astype(BF16) for h in H]
        inter = [_dot(qes[h], sts[h].astype(BF16), NT) for h in H]
        upd = [_dot(vs[h], kls[h], TN) for h in H]
        As = [jnp.where(tri, _dot(qms[h], kms[h], NT), 0.0).astype(BF16) for h in H]
        outs = [inter[h] + _dot(As[h], vs[h], NN) for h in H]
        news = [sts[h] * jnp.exp(bls[h]) + upd[h] for h in H]
        o = jnp.concatenate(outs, axis=1)
        if oa_ref is not None:
            o = o + oa_ref[...]
        o_ref[...] = o
        for h in range(heads):
            st_ref[0, h] = sts[h]
            S[h] = news[h]

        @pl.when(n == nc - 1)
        def _():
            for h in range(heads):
                sf_ref[h] = news[h]

    def ci(n):
        return (nc - 1 - n) if reverse else n

    specs = [
        pl.BlockSpec((C, heads * dk), lambda n: (ci(n), 0)),
        pl.BlockSpec((C, heads * dk), lambda n: (ci(n), 1)),
        pl.BlockSpec((C, heads * dv), lambda n: (ci(n), 0)),
        pl.BlockSpec((C, heads * dk), lambda n: (ci(n), 0)),
        pl.BlockSpec((heads, dv, dk), lambda n: (0, 0, 0)),
    ]
    ins = [pqk, pqk, pv, gl, s0]
    if o_add is not None:
        specs.append(pl.BlockSpec((C, heads * dv), lambda n: (ci(n), 0)))
        ins.append(o_add)
    return pl.pallas_call(
        body,
        grid=(nc,),
        in_specs=specs,
        out_specs=[
            pl.BlockSpec((C, heads * dv), lambda n: (ci(n), 0)),
            pl.BlockSpec((1, heads, dv, dk), lambda n: (ci(n), 0, 0, 0)),
            pl.BlockSpec((heads, dv, dk), lambda n: (0, 0, 0)),
        ],
        out_shape=[
            jax.ShapeDtypeStruct((T, heads * dv), F32),
            jax.ShapeDtypeStruct((nc, heads, dv, dk), F32),
            jax.ShapeDtypeStruct((heads, dv, dk), F32),
        ],
        scratch_shapes=[pltpu.VMEM((heads, dv, dk), F32)],
        compiler_params=_cp(("arbitrary",)),
        name=name,
    )(*ins)


def gla_bwd(pqk, pv, gl, states, do, dsf, heads, reverse, name, acc=None):
    T = pqk.shape[0]
    dk = pqk.shape[1] // (2 * heads)
    dv = pv.shape[1] // heads
    C = GLA_CHUNK
    nc = T // C
    qscale = dk ** -0.5

    def body(*refs):
        if acc is None:
            q_ref, k_ref, v_ref, g_ref, st_ref, do_ref, dsf_ref, dq_ref, dk_ref, dv_ref, dg_ref, ds0_ref, dS = refs
            aq = ak = av = None
        else:
            (q_ref, k_ref, v_ref, g_ref, st_ref, do_ref, dsf_ref, aq, ak, av,
             dq_ref, dk_ref, dv_ref, dg_ref, ds0_ref, dS) = refs
        n = pl.program_id(0)

        @pl.when(n == 0)
        def _():
            dS[...] = dsf_ref[...]

        r = lax.broadcasted_iota(jnp.int32, (C, C), 0)
        c = lax.broadcasted_iota(jnp.int32, (C, C), 1)
        tri = (r <= c) if reverse else (r >= c)
        tri_t = (r >= c) if reverse else (r <= c)
        trib, tritb = tri.astype(BF16), tri_t.astype(BF16)
        ga, qa, ka, va, doa = g_ref[...], q_ref[...], k_ref[...], v_ref[...], do_ref[...]
        sts = [st_ref[0, h] for h in range(heads)]
        dsts = [dS[h] for h in range(heads)]
        H = range(heads)
        gs = [ga[:, h * dk:(h + 1) * dk] for h in H]
        bs = [_tri_dot(trib, g) for g in gs]
        bls = [jnp.sum(g, axis=0, keepdims=True) for g in gs]
        mid = lax.broadcasted_iota(jnp.int32, (C, 1), 0) == C // 2
        bms = [jnp.sum(jnp.where(mid, b, 0.0), axis=0, keepdims=True) for b in bs]
        ebs = [jnp.exp(b) for b in bs]
        embs = [jnp.exp(bs[h] - bms[h]) for h in H]
        enbs = [jnp.exp(bms[h] - bs[h]) for h in H]
        elbs = [jnp.exp(bls[h] - bs[h]) for h in H]
        ebls = [jnp.exp(bl) for bl in bls]
        vbs = [va[:, h * dv:(h + 1) * dv].astype(BF16) for h in H]
        dobs = [doa[:, h * dv:(h + 1) * dv].astype(BF16) for h in H]
        qs = [qa[:, h * dk:(h + 1) * dk].astype(F32) * qscale for h in H]
        qes = [qs[h] * ebs[h] for h in H]
        qms = [qs[h] * embs[h] for h in H]
        kms = [ka[:, h * dk:(h + 1) * dk].astype(F32) * enbs[h] for h in H]
        kls = [ka[:, h * dk:(h + 1) * dk].astype(F32) * elbs[h] for h in H]
        qebs = [a.astype(BF16) for a in qes]
        qmbs = [a.astype(BF16) for a in qms]
        kmbs = [a.astype(BF16) for a in kms]
        klbs = [a.astype(BF16) for a in kls]
        stbs = [a.astype(BF16) for a in sts]
        dstbs = [a.astype(BF16) for a in dsts]
        ps = [jnp.where(tri, _dot(qmbs[h], kmbs[h], NT), 0.0).astype(BF16) for h in H]
        dps = [jnp.where(tri, _dot(dobs[h], vbs[h], NT), 0.0).astype(BF16) for h in H]
        dqes = [_dot(dobs[h], stbs[h], NN) for h in H]
        dkls = [_dot(vbs[h], dstbs[h], NN) for h in H]
        dv1 = [_dot(klbs[h], dstbs[h], NT) for h in H]
        dsn1 = [_dot(dobs[h], qebs[h], TN) for h in H]
        dqms = [_dot(dps[h], kmbs[h], NN) for h in H]
        dkms = [_dot(dps[h], qmbs[h], TN) for h in H]
        dvs = [_dot(ps[h], dobs[h], TN) + dv1[h] for h in H]
        dbls = [ebls[h] * jnp.sum(dsts[h] * sts[h], axis=0, keepdims=True)
                + jnp.sum(dkls[h] * kls[h], axis=0, keepdims=True) for h in H]
        dsns = [dsn1[h] + dsts[h] * ebls[h] for h in H]
        dqs = [(dqes[h] * ebs[h] + dqms[h] * embs[h]) * qscale for h in H]
        dks = [dkms[h] * enbs[h] + dkls[h] * elbs[h] for h in H]
        dbs = [dqes[h] * qes[h] + dqms[h] * qms[h] - dkms[h] * kms[h] - dkls[h] * kls[h] for h in H]
        dgs = [_tri_dot(tritb, dbs[h]) + dbls[h] for h in H]
        dq, dkk, dvv = (jnp.concatenate(a, axis=1) for a in (dqs, dks, dvs))
        if aq is not None:
            dq = dq + aq[...].astype(F32)
            dkk = dkk + ak[...].astype(F32)
            dvv = dvv + av[...].astype(F32)
        dq_ref[...] = dq.astype(dq_ref.dtype)
        dk_ref[...] = dkk.astype(dk_ref.dtype)
        dv_ref[...] = dvv.astype(dv_ref.dtype)
        dg_ref[...] = jnp.concatenate(dgs, axis=1)
        for h in range(heads):
            dS[h] = dsns[h]

        @pl.when(n == nc - 1)
        def _():
            for h in range(heads):
                ds0_ref[h] = dsns[h]

    def ci(n):
        return n if reverse else (nc - 1 - n)

    kspec = pl.BlockSpec((C, heads * dk), lambda n: (ci(n), 0))
    vspec = pl.BlockSpec((C, heads * dv), lambda n: (ci(n), 0))
    sspec = pl.BlockSpec((heads, dv, dk), lambda n: (0, 0, 0))
    specs = [
        kspec,
        pl.BlockSpec((C, heads * dk), lambda n: (ci(n), 1)),
        vspec,
        kspec,
        pl.BlockSpec((1, heads, dv, dk), lambda n: (ci(n), 0, 0, 0)),
        vspec,
        sspec,
    ]
    ins = [pqk, pqk, pv, gl, states, do, dsf]
    odt = F32 if acc is None else BF16
    if acc is not None:
        specs += [kspec, kspec, vspec]
        ins += list(acc)
    return pl.pallas_call(
        body,
        grid=(nc,),
        in_specs=specs,
        out_specs=[kspec, kspec, vspec, kspec, sspec],
        out_shape=[
            jax.ShapeDtypeStruct((T, heads * dk), odt),
            jax.ShapeDtypeStruct((T, heads * dk), odt),
            jax.ShapeDtypeStruct((T, heads * dv), odt),
            jax.ShapeDtypeStruct((T, heads * dk), F32),
            jax.ShapeDtypeStruct((heads, dv, dk), F32),
        ],
        scratch_shapes=[pltpu.VMEM((heads, dv, dk), F32)],
        compiler_params=_cp(("arbitrary",)),
        name=name,
    )(*ins)


def gla_out_fwd(og, prb, gn, heads, name):
    dv = og.shape[1] // heads

    def fn(og, rb, gn):
        outs = []
        for h in range(heads):
            oh = og[:, h * dv:(h + 1) * dv]
            outs.append(oh * _rstd(oh) * gn)
        y = jnp.concatenate(outs, axis=1)
        return y * (rb * _sig(rb))

    return rowwise(fn, [og, prb], [gn], [(og.shape[1], BF16)], [], 256, name)[0]


def gla_out_bwd(og, prb, du, gn, heads, name):
    dv = og.shape[1] // heads

    def fn(og, rb, du, gn):
        sg = _sig(rb)
        silu = rb * sg
        dsilu = sg * (1.0 + rb * (1.0 - sg))
        dog, ys = [], []
        dgn = jnp.zeros((1, dv), F32)
        for h in range(heads):
            sl = slice(h * dv, (h + 1) * dv)
            oh = og[:, sl]
            r = _rstd(oh)
            n = oh * r
            ys.append(n * gn)
            dy = du[:, sl] * silu[:, sl]
            dgn = dgn + jnp.sum(dy * n, axis=0, keepdims=True)
            dn = dy * gn
            dog.append(r * (dn - n * jnp.mean(dn * n, axis=-1, keepdims=True)))
        y = jnp.concatenate(ys, axis=1)
        return jnp.concatenate(dog, axis=1), du * y * dsilu, dgn

    return rowwise(fn, [og, prb, du], [gn], [(og.shape[1], F32), (og.shape[1], BF16)], [(1, dv)], 128, name)


def conv_specs(T, tt, tc, off, order):
    r8 = tt // 8
    last8 = T // 8 - 1
    if order == "ij":
        return [
            pl.BlockSpec((tt, tc), lambda i, j: (i, j + off)),
            pl.BlockSpec((8, tc), lambda i, j: (jnp.maximum(i * r8 - 1, 0), j + off)),
            pl.BlockSpec((8, tc), lambda i, j: (jnp.minimum((i + 1) * r8, last8), j + off)),
        ]
    return [
        pl.BlockSpec((tt, tc), lambda j, i: (i, j + off)),
        pl.BlockSpec((8, tc), lambda j, i: (jnp.maximum(i * r8 - 1, 0), j + off)),
        pl.BlockSpec((8, tc), lambda j, i: (jnp.minimum((i + 1) * r8, last8), j + off)),
    ]


def _shifted(u, hp, hn, i, nt_):
    tt = u.shape[0]
    row = lax.broadcasted_iota(jnp.int32, u.shape, 0)
    r8 = lax.broadcasted_iota(jnp.int32, hp.shape, 0)
    prev = jnp.sum(jnp.where(r8 == 7, hp, 0.0), axis=0, keepdims=True) * (i > 0).astype(F32)
    nxt = jnp.sum(jnp.where(r8 == 0, hn, 0.0), axis=0, keepdims=True) * (i < nt_ - 1).astype(F32)
    down = jnp.where(row == 0, prev, pltpu.roll(u, 1, 0))
    up = jnp.where(row == tt - 1, nxt, pltpu.roll(u, tt - 1, 0))
    return down, up


def conv_swiglu_fwd(u, cw, cb, name):
    T, F2 = u.shape
    F = F2 // 2
    tt = min(512, T)
    tc = _pick(F, (512, 256, 128))
    nt_, ncol = T // tt, F // tc
    H = CONV_HALO
    n = tt + 2 * H

    def body(ua, uap, uan, ug, ugp, ugn, wa, wg, ba, bg, f_ref):
        i = pl.program_id(1)
        keep_p = (i > 0).astype(F32)
        keep_n = (i < nt_ - 1).astype(F32)
        res = []
        for m, p, nx, w, b in ((ua, uap, uan, wa, ba), (ug, ugp, ugn, wg, bg)):
            x = jnp.concatenate([p[...] * keep_p, m[...], nx[...] * keep_n], axis=0)
            down, up = pltpu.roll(x, 1, 0)[H:H + tt], pltpu.roll(x, n - 1, 0)[H:H + tt]
            res.append(w[0] * down + w[1] * x[H:H + tt] + w[2] * up + b[...])
        a, g = res
        f_ref[...] = (a * _sig_tanh(a) * g).astype(f_ref.dtype)

    wspec = lambda off: pl.BlockSpec((3, 1, tc), lambda j, i: (0, 0, j + off))
    bspec = lambda off: pl.BlockSpec((1, tc), lambda j, i: (0, j + off))
    return pl.pallas_call(
        body,
        grid=(ncol, nt_),
        in_specs=conv_halo_specs(T, tt, tc, 0) + conv_halo_specs(T, tt, tc, ncol)
        + [wspec(0), wspec(ncol), bspec(0), bspec(ncol)],
        out_specs=pl.BlockSpec((tt, tc), lambda j, i: (i, j)),
        out_shape=jax.ShapeDtypeStruct((T, F), BF16),
        compiler_params=_cp(("parallel", "parallel")),
        name=name,
    )(u, u, u, u, u, u, cw, cw, cb, cb)


def conv_swiglu_bwd(u, cw, cb, df, name):
    T, F2 = u.shape
    F = F2 // 2
    tt = min(256, T)
    tc = _pick(F, (512, 256, 128))
    nt_, ncol = T // tt, F // tc

    def body(ua, uap, uan, ug, ugp, ugn, wa, wg, ba, bg, df_ref, da_ref, dg_ref, dwa, dwg, dba, dbg):
        i = pl.program_id(1)

        @pl.when(i == 0)
        def _():
            for r in (dwa, dwg, dba, dbg):
                r[...] = jnp.zeros_like(r)

        sh = []
        res = []
        for um, up_, un, w, b in ((ua, uap, uan, wa, ba), (ug, ugp, ugn, wg, bg)):
            x = um[...]
            down, up = _shifted(x, up_[...], un[...], i, nt_)
            sh.append((down, x, up))
            res.append(w[0] * down + w[1] * x + w[2] * up + b[...])
        a, g = res
        d = df_ref[...].astype(F32)
        sg = _sig(a)
        da = d * g * sg * (1.0 + a * (1.0 - sg))
        dg = d * a * sg
        da_ref[...] = da
        dg_ref[...] = dg
        for dd, (down, x, up), dw, db in ((da, sh[0], dwa, dba), (dg, sh[1], dwg, dbg)):
            dw[0] += jnp.sum(dd * down, axis=0, keepdims=True)
            dw[1] += jnp.sum(dd * x, axis=0, keepdims=True)
            dw[2] += jnp.sum(dd * up, axis=0, keepdims=True)
            db[...] += jnp.sum(dd, axis=0, keepdims=True)

    wspec = lambda off: pl.BlockSpec((3, 1, tc), lambda j, i: (0, 0, j + off))
    bspec = lambda off: pl.BlockSpec((1, tc), lambda j, i: (0, j + off))
    tile = pl.BlockSpec((tt, tc), lambda j, i: (i, j))
    return pl.pallas_call(
        body,
        grid=(ncol, nt_),
        in_specs=conv_specs(T, tt, tc, 0, "ji") + conv_specs(T, tt, tc, ncol, "ji")
        + [wspec(0), wspec(ncol), bspec(0), bspec(ncol), tile],
        out_specs=[tile, tile, wspec(0), wspec(0), bspec(0), bspec(0)],
        out_shape=[
            jax.ShapeDtypeStruct((T, F), F32), jax.ShapeDtypeStruct((T, F), F32),
            jax.ShapeDtypeStruct((3, 1, F), F32), jax.ShapeDtypeStruct((3, 1, F), F32),
            jax.ShapeDtypeStruct((1, F), F32), jax.ShapeDtypeStruct((1, F), F32),
        ],
        compiler_params=_cp(("parallel", "arbitrary")),
        name=name,
    )(u, u, u, u, u, u, cw, cw, cb, cb, df)


CONV_HALO = 16


def conv_halo_specs(T, tt, tc, off):
    r = tt // CONV_HALO
    last = T // CONV_HALO - 1
    return [
        pl.BlockSpec((tt, tc), lambda j, i: (i, j + off)),
        pl.BlockSpec((CONV_HALO, tc), lambda j, i: (jnp.maximum(i * r - 1, 0), j + off)),
        pl.BlockSpec((CONV_HALO, tc), lambda j, i: (jnp.minimum((i + 1) * r, last), j + off)),
    ]


def conv_swiglu_bwd_fused(u, cw, cb, df, name):
    T, F2 = u.shape
    F = F2 // 2
    tt = min(512, T)
    tc = _pick(F, (512, 256, 128))
    nt_, ncol = T // tt, F // tc
    H = CONV_HALO
    n = tt + 2 * H

    def body(ua, uap, uan, ug, ugp, ugn, dm, dp_, dn, wa, wg, ba, bg, dua_ref, dug_ref, dwa, dwg, dba, dbg):
        i = pl.program_id(1)

        @pl.when(i == 0)
        def _():
            for r in (dwa, dwg, dba, dbg):
                r[...] = jnp.zeros_like(r)

        keep_p = (i > 0).astype(F32)
        keep_n = (i < nt_ - 1).astype(F32)

        def ext(m, p, nx):
            return jnp.concatenate([p[...].astype(F32) * keep_p, m[...].astype(F32), nx[...].astype(F32) * keep_n],
                                   axis=0)

        d = ext(dm, dp_, dn)
        conv, parts = [], []
        for m, p, nx, w, b in ((ua, uap, uan, wa, ba), (ug, ugp, ugn, wg, bg)):
            x = ext(m, p, nx)
            down, up = pltpu.roll(x, 1, 0), pltpu.roll(x, n - 1, 0)
            parts.append((down, x, up))
            conv.append(w[0] * down + w[1] * x + w[2] * up + b[...])
        a, g = conv
        sg = _sig_tanh(a)
        da = d * g * sg * (1.0 + a * (1.0 - sg))
        dg = d * a * sg
        for dd, w, (down, x, up), o_ref, dw, db in ((da, wa, parts[0], dua_ref, dwa, dba),
                                                    (dg, wg, parts[1], dug_ref, dwg, dbg)):
            du = w[0] * pltpu.roll(dd, n - 1, 0) + w[1] * dd + w[2] * pltpu.roll(dd, 1, 0)
            o_ref[...] = du[H:H + tt].astype(o_ref.dtype)
            ddm = dd[H:H + tt]
            dw[0] += jnp.sum(ddm * down[H:H + tt], axis=0, keepdims=True)
            dw[1] += jnp.sum(ddm * x[H:H + tt], axis=0, keepdims=True)
            dw[2] += jnp.sum(ddm * up[H:H + tt], axis=0, keepdims=True)
            db[...] += jnp.sum(ddm, axis=0, keepdims=True)

    wspec = lambda off: pl.BlockSpec((3, 1, tc), lambda j, i: (0, 0, j + off))
    bspec = lambda off: pl.BlockSpec((1, tc), lambda j, i: (0, j + off))
    tile = pl.BlockSpec((tt, tc), lambda j, i: (i, j))
    return pl.pallas_call(
        body,
        grid=(ncol, nt_),
        in_specs=conv_halo_specs(T, tt, tc, 0) + conv_halo_specs(T, tt, tc, ncol) + conv_halo_specs(T, tt, tc, 0)
        + [wspec(0), wspec(ncol), bspec(0), bspec(ncol)],
        out_specs=[tile, tile, wspec(0), wspec(0), bspec(0), bspec(0)],
        out_shape=[
            jax.ShapeDtypeStruct((T, F), BF16), jax.ShapeDtypeStruct((T, F), BF16),
            jax.ShapeDtypeStruct((3, 1, F), F32), jax.ShapeDtypeStruct((3, 1, F), F32),
            jax.ShapeDtypeStruct((1, F), F32), jax.ShapeDtypeStruct((1, F), F32),
        ],
        compiler_params=_cp(("parallel", "arbitrary")),
        name=name,
    )(u, u, u, u, u, u, df, df, df, cw, cw, cb, cb)


def conv_transpose(d, cw, off, name):
    T, F = d.shape
    tt = min(256, T)
    tc = _pick(F, (512, 256, 128))
    nt_, ncol = T // tt, F // tc
    offb = off // tc

    def body(dm, dp_, dn, w, o_ref):
        i = pl.program_id(0)
        x = dm[...]
        down, up = _shifted(x, dp_[...], dn[...], i, nt_)
        o_ref[...] = (w[0] * up + w[1] * x + w[2] * down).astype(o_ref.dtype)

    return pl.pallas_call(
        body,
        grid=(nt_, ncol),
        in_specs=conv_specs(T, tt, tc, 0, "ij") + [pl.BlockSpec((3, 1, tc), lambda i, j: (0, 0, j + offb))],
        out_specs=pl.BlockSpec((tt, tc), lambda i, j: (i, j)),
        out_shape=jax.ShapeDtypeStruct((T, F), BF16),
        compiler_params=_cp(("parallel", "parallel")),
        name=name,
    )(d, d, d, cw)


def rope_tables(n, hd):
    rows = n // GRID_W
    row = jnp.repeat(jnp.arange(rows), GRID_W)
    col = jnp.tile(jnp.arange(GRID_W), rows)
    n_freq = hd // 4
    inv = ROPE_THETA ** (-jnp.arange(n_freq, dtype=F32) / n_freq)
    ang = jnp.concatenate([row[:, None] * inv, col[:, None] * inv], axis=-1)
    cos, sin = jnp.cos(ang), jnp.sin(ang)
    return jnp.concatenate([cos, cos], axis=-1), jnp.concatenate([-sin, sin], axis=-1)


def local_step(x, ctx, tgt, mod, modc, W, P):
    T, D = x.shape
    L = ctx.shape[0]
    hd, hq, hkv, gh = P["hd"], P["hq"], P["hkv"], P["gh"]
    sh1, sc1, g1, sh2, sc2, g2 = mod
    csh1, csc1 = modc
    kvw = hkv * hd
    gkw = W["gqk"].shape[1] // 2
    gdv = D // gh
    gdk = gkw // gh

    h = modulate_fwd(x, P["g_mix"], sh1, sc1, "mod1")
    hc = modulate_fwd(ctx, P["g_mix"], csh1, csc1, "mod1_ctx")
    pq = matmul(h, W["q"], "nn", F32, "proj_q")
    pkv = matmul(h, W["kv"], "nn", F32, "proj_kv")
    pgqk = matmul(h, W["gqk"], "nn", F32, "proj_gqk")
    pgv = matmul(h, W["gv"], "nn", F32, "proj_gv")
    prb = matmul(h, W["rb"], "nn", F32, "proj_rb")
    plr = matmul(h, W["lr"], "nn", F32, "proj_lr")
    pgab = matmul(h, W["gab"], "nn", F32, "proj_gab")
    pkv_c = matmul(hc, W["kv"], "nn", F32, "proj_kv_ctx")
    pgqk_c = matmul(hc, W["gqk"], "nn", F32, "proj_gqk_ctx")
    pgv_c = matmul(hc, W["gv"], "nn", F32, "proj_gv_ctx")
    plr_c = matmul(hc, W["lr"], "nn", F32, "proj_lr_ctx")

    cosf, sinf = rope_tables(T, hd)
    one_c, zero_c = jnp.ones((L, hd), F32), jnp.zeros((L, hd), F32)
    qr = norm_rope_fwd(pq, hq * hd, 0, P["q_norm"], cosf, sinf, hd, "qnorm")
    kr = norm_rope_fwd(pkv, kvw, 0, P["k_norm"], cosf, sinf, hd, "knorm")
    kcr = norm_rope_fwd(pkv_c, kvw, 0, P["k_norm"], one_c, zero_c, hd, "knorm_ctx")
    sink = P["attn_sink"].reshape(hq, 1, 1)
    o_attn, lse, lse_row = attention_fwd(qr, kr, pkv, kcr, pkv_c, sink, hkv, hd, "attn_fwd")

    gf, gb = gate_fwd(plr, W["gate_f"], W["gate_b"], P["b_gate_f"], P["b_gate_b"], "gates")
    gfc, gbc = gate_fwd(plr_c, W["gate_f"], W["gate_b"], P["b_gate_f"], P["b_gate_b"], "gates_ctx")
    zero_state = jnp.zeros((gh, gdv, gdk), F32)
    _, st_cf, s_cf = gla_fwd(pgqk_c, pgv_c, gfc, zero_state, gh, False, "gla_ctx_f")
    _, st_cb, s_cb = gla_fwd(pgqk_c, pgv_c, gbc, zero_state, gh, True, "gla_ctx_b")
    of, st_f, _ = gla_fwd(pgqk, pgv, gf, s_cf, gh, False, "gla_f")
    og, st_b, _ = gla_fwd(pgqk, pgv, gb, s_cb, gh, True, "gla_b", o_add=of)
    ug = gla_out_fwd(og, prb, P["gla_norm"], gh, "gla_out")

    ya = matmul(o_attn, W["attn_o"], "nn", F32, "attn_o")
    yg = matmul(ug, W["gla_o"], "nn", F32, "gla_o")

    def merge_fn(ya, yg, ga, gb_):
        return _sig(ga) * ya + _sig(gb_) * yg

    z = rowwise(merge_fn, [ya, yg, (pgab, D, 0), (pgab, D, 1)], [], [(D, BF16)], [], 256, "merge")[0]
    mo = matmul(z, W["out"], "nn", F32, "w_out")

    def res_fn(x, mo, g1, gffn, sh2, sc2):
        x1 = x + g1 * mo
        return x1, x1 * _rstd(x1) * gffn * (1.0 + sc2) + sh2

    x1, h2 = rowwise(res_fn, [x, mo], [g1, P["g_ffn"], sh2, sc2], [(D, F32), (D, BF16)], [], 256, "res_mod2")
    u = matmul(h2, W["up"], "nn", F32, "w_up")
    cw3 = W["conv_w"].reshape(3, 1, -1)
    f = conv_swiglu_fwd(u, cw3, P["conv_b"], "conv_swiglu")
    fo = matmul(f, W["down"], "nn", F32, "w_down")

    def final_fn(x1, fo, tgt, g2):
        e = x1 + g2 * fo - tgt
        dy = e * (1.0 / D)
        lsum = jnp.sum(jnp.sum(e * e, axis=1, keepdims=True), axis=0, keepdims=True)
        return dy, dy * g2, jnp.broadcast_to(lsum, (1, 128)), jnp.sum(dy * fo, axis=0, keepdims=True)

    dy, dfo, lsum, dg2 = rowwise(final_fn, [x1, fo, tgt], [g2], [(D, F32), (D, BF16)], [(1, 128), (1, D)], 256, "loss")
    df = matmul(dfo, W["down"], "nt", BF16, "d_f")
    dw_down = matmul(f, dfo, "tn", BF16, "dw_down")
    du_a, du_g, dcw_a, dcw_g, dcb_a, dcb_g = conv_swiglu_bwd_fused(u, cw3, P["conv_b"], df, "conv_swiglu_bwd")
    Fh = du_a.shape[1]
    dh2 = matmul(du_a, W["up"][:, :Fh], "nt", F32, "d_h2_a")
    dh2 = matmul(du_g, W["up"][:, Fh:], "nt", F32, "d_h2_g", add=dh2)
    dw_up = [matmul(h2, du_a, "tn", BF16, "dw_up_a"), matmul(h2, du_g, "tn", BF16, "dw_up_g")]

    def mod2_bwd_fn(x1, dh, dy, mo, gffn, sc2, g1):
        r = _rstd(x1)
        n = x1 * r
        dyy = dh * (1.0 + sc2)
        dn = dyy * gffn
        dx1 = dy + r * (dn - n * jnp.mean(dn * n, axis=-1, keepdims=True))
        s0 = lambda a: jnp.sum(a, axis=0, keepdims=True)
        return dx1, dx1 * g1, s0(dyy * n), s0(dh), s0(dh * n * gffn), s0(dx1 * mo)

    dx1, dmo, dg_ffn, dsh2, dsc2, dg1 = rowwise(
        mod2_bwd_fn, [x1, dh2, dy, mo], [P["g_ffn"], sc2, g1], [(D, F32), (D, BF16)], [(1, D)] * 4, 128, "mod2_bwd")
    dz = matmul(dmo, W["out"], "nt", F32, "d_z")
    dw_out = matmul(z, dmo, "tn", BF16, "dw_out")

    def merge_bwd_fn(dz, ya, yg, ga, gb_):
        sa, sb = _sig(ga), _sig(gb_)
        return dz * sa, dz * sb, jnp.concatenate([dz * ya * sa * (1.0 - sa), dz * yg * sb * (1.0 - sb)], axis=1)

    dya, dyg, dpgab = rowwise(merge_bwd_fn, [dz, ya, yg, (pgab, D, 0), (pgab, D, 1)], [],
                              [(D, BF16), (D, BF16), (2 * D, BF16)], [], 128, "merge_bwd")
    do_attn = matmul(dya, W["attn_o"], "nt", BF16, "d_oattn")
    dw_attn_o = matmul(o_attn, dya, "tn", BF16, "dw_attn_o")
    dug = matmul(dyg, W["gla_o"], "nt", F32, "d_ug")
    dw_gla_o = matmul(ug, dyg, "tn", BF16, "dw_gla_o")
    dog, dprb, dgn = gla_out_bwd(og, prb, dug, P["gla_norm"], gh, "gla_out_bwd")

    dq1, dk1, dv1, dgf, ds_cf = gla_bwd(pgqk, pgv, gf, st_f, dog, zero_state, gh, False, "gla_f_bwd")
    dgq, dgk, dpgv, dgb, ds_cb = gla_bwd(pgqk, pgv, gb, st_b, dog, zero_state, gh, True, "gla_b_bwd",
                                          acc=(dq1, dk1, dv1))
    dpgqk = jnp.concatenate([dgq, dgk], axis=1)
    zero_do = jnp.zeros((L, gh * gdv), F32)
    cq1, ck1, cv1, dgfc, _ = gla_bwd(pgqk_c, pgv_c, gfc, st_cf, zero_do, ds_cf, gh, False, "gla_ctx_f_bwd")
    cq, ck, dpgv_c, dgbc, _ = gla_bwd(pgqk_c, pgv_c, gbc, st_cb, zero_do, ds_cb, gh, True, "gla_ctx_b_bwd",
                                      acc=(cq1, ck1, cv1))
    dpgqk_c = jnp.concatenate([cq, ck], axis=1)
    dplr, dwgf, dbgf, dwgb, dbgb = gate_bwd(plr, dgf, dgb, W["gate_f"], W["gate_b"], P["b_gate_f"], P["b_gate_b"], "gates_bwd")
    dplr_c, dwgf_c, dbgf_c, dwgb_c, dbgb_c = gate_bwd(plr_c, dgfc, dgbc, W["gate_f"], W["gate_b"], P["b_gate_f"],
                                                      P["b_gate_b"], "gates_ctx_bwd")

    dqr, dkc_r, dvc, dsink, dr_row = attention_bwd_q(qr, kr, pkv, kcr, pkv_c, sink, do_attn, o_attn, lse, hkv, hd,
                                                     "attn_bwd_q")
    dkr, dv = attention_bwd_kv(qr, kr, pkv, do_attn, lse_row, dr_row, hkv, hd, "attn_bwd_kv")
    dpq, dqn = norm_rope_bwd(pq, hq * hd, 0, dqr, P["q_norm"], cosf, sinf, hd, "qnorm_bwd")
    dpk, dkn = norm_rope_bwd(pkv, kvw, 0, dkr, P["k_norm"], cosf, sinf, hd, "knorm_bwd")
    dpk_c, dkn_c = norm_rope_bwd(pkv_c, kvw, 0, dkc_r, P["k_norm"], one_c, zero_c, hd, "knorm_ctx_bwd")
    dpkv = jnp.concatenate([dpk, dv], axis=1)
    dpkv_c = jnp.concatenate([dpk_c, dvc.astype(BF16)], axis=1)

    dw_q = matmul(h, dpq, "tn", BF16, "dw_q")
    dw_kv = matmul(h, dpkv, "tn", BF16, "dw_kv", add=matmul(hc, dpkv_c, "tn", F32, "dw_kv_ctx"))
    dw_gqk = matmul(h, dpgqk, "tn", BF16, "dw_gqk", add=matmul(hc, dpgqk_c, "tn", F32, "dw_gqk_ctx"))
    dw_gv = matmul(h, dpgv, "tn", BF16, "dw_gv", add=matmul(hc, dpgv_c, "tn", F32, "dw_gv_ctx"))
    dw_rb = matmul(h, dprb, "tn", BF16, "dw_rb")
    dw_lr = matmul(h, dplr, "tn", BF16, "dw_lr", add=matmul(hc, dplr_c, "tn", F32, "dw_lr_ctx"))
    dw_gab = matmul(h, dpgab, "tn", BF16, "dw_gab")
    lrw = P["lowrank"]
    dw_in = [dw_q, dw_kv, dw_gqk, dw_gv, dw_rb, dw_lr[:, :2 * lrw], dw_gab]

    dh = matmul(dpq, W["q"], "nt", F32, "dh_q")
    dh = matmul(dpkv, W["kv"], "nt", F32, "dh_kv", add=dh)
    dh = matmul(dpgqk, W["gqk"], "nt", F32, "dh_gqk", add=dh)
    dh = matmul(dpgv, W["gv"], "nt", F32, "dh_gv", add=dh)
    dh = matmul(dprb, W["rb"], "nt", F32, "dh_rb", add=dh)
    dh = matmul(dplr, W["lr"], "nt", F32, "dh_lr", add=dh)
    dh = matmul(dpgab, W["gab"], "nt", F32, "dh_gab", add=dh)
    dhc = matmul(dpkv_c, W["kv"], "nt", F32, "dhc_kv")
    dhc = matmul(dpgqk_c, W["gqk"], "nt", F32, "dhc_gqk", add=dhc)
    dhc = matmul(dpgv_c, W["gv"], "nt", F32, "dhc_gv", add=dhc)
    dhc = matmul(dplr_c, W["lr"], "nt", F32, "dhc_lr", add=dhc)

    def mod1_bwd_fn(x, dh, dres, g, sc):
        r = _rstd(x)
        n = x * r
        dyy = dh * (1.0 + sc)
        dn = dyy * g
        dx = dres + r * (dn - n * jnp.mean(dn * n, axis=-1, keepdims=True))
        s0 = lambda a: jnp.sum(a, axis=0, keepdims=True)
        return dx, s0(dyy * n), s0(dh), s0(dh * n * g)

    grad_x, dgmix, dsh1, dsc1 = rowwise(mod1_bwd_fn, [x, dh, dx1], [P["g_mix"], sc1], [(D, F32)], [(1, D)] * 3,
                                        128, "mod1_bwd")
    _, dgmix_c, dcsh1, dcsc1 = rowwise(mod1_bwd_fn, [ctx, dhc, jnp.zeros_like(ctx)], [P["g_mix"], csc1], [(D, F32)],
                                       [(1, D)] * 3, 128, "mod1_ctx_bwd")

    zD = jnp.zeros((1, D), F32)
    grads = dict(
        w_in=dw_in, w_attn_o=dw_attn_o, w_gla_o=dw_gla_o, w_out=dw_out, w_up=dw_up, w_down=dw_down,
        dmod_x=jnp.concatenate([dsh1, dsc1, dg1, dsh2, dsc2, dg2], axis=1),
        dmod_c=jnp.concatenate([dcsh1, dcsc1, zD, zD, zD, zD], axis=1),
        g_mix=dgmix + dgmix_c, q_norm=dqn, k_norm=dkn + dkn_c, attn_sink=dsink.reshape(1, hq),
        w_gate_f=(dwgf + dwgf_c)[:lrw], b_gate_f=dbgf + dbgf_c,
        w_gate_b=(dwgb + dwgb_c)[lrw:2 * lrw], b_gate_b=dbgb + dbgb_c,
        gla_norm=dgn, g_ffn=dg_ffn,
        conv_w=jnp.concatenate([dcw_a, dcw_g], axis=2).reshape(3, -1),
        conv_b=jnp.concatenate([dcb_a, dcb_g], axis=1),
    )
    return lsum[0, 0], grad_x, grads


SMALL_REPL = ("c_ctx", "b_mod", "g_mix", "q_norm", "k_norm", "attn_sink", "b_gate_f", "b_gate_b", "gla_norm", "g_ffn",
              "conv_b")
SMALL_SHARD = ("w_gate_f", "w_gate_b", "conv_w")
ORDER = ("c_ctx", "w_mod", "b_mod", "g_mix", "w_in", "q_norm", "k_norm", "attn_sink", "w_gate_f", "b_gate_f",
         "w_gate_b", "b_gate_b", "gla_norm", "w_attn_o", "w_gla_o", "w_out", "g_ffn", "w_up", "conv_w", "conv_b",
         "w_down")


def kernel(x, c, ctx, c_ctx, w_mod, b_mod, g_mix, w_in, q_norm, k_norm, attn_sink, w_gate_f, b_gate_f, w_gate_b, b_gate_b, gla_norm, w_attn_o, w_gla_o, w_out, g_ffn, w_up, conv_w, conv_b, w_down, loss_target, m_c_ctx, m_w_mod, m_b_mod, m_g_mix, m_w_in, m_q_norm, m_k_norm, m_attn_sink, m_w_gate_f, m_b_gate_f, m_w_gate_b, m_b_gate_b, m_gla_norm, m_w_attn_o, m_w_gla_o, m_w_out, m_g_ffn, m_w_up, m_conv_w, m_conv_b, m_w_down, v_c_ctx, v_w_mod, v_b_mod, v_g_mix, v_w_in, v_q_norm, v_k_norm, v_attn_sink, v_w_gate_f, v_b_gate_f, v_w_gate_b, v_b_gate_b, v_gla_norm, v_w_attn_o, v_w_gla_o, v_w_out, v_g_ffn, v_w_up, v_conv_w, v_conv_b, v_w_down):
    loc = dict(locals())
    Wt = {n: loc[n] for n in ORDER}
    Mt = {n: loc["m_" + n] for n in ORDER}
    Vt = {n: loc["v_" + n] for n in ORDER}
    me = 4 * lax.axis_index("x") + 2 * lax.axis_index("y") + lax.axis_index("c")

    D = x.shape[-1]
    hd = q_norm.shape[-1]
    hq = attn_sink.shape[-1]
    gdv = gla_norm.shape[-1]
    gh = D // gdv
    gdk = D // 2 // gh
    lrw = w_gate_f.shape[1]
    in_w = NDEV * w_in.shape[-1]
    kvw = (in_w - hq * hd - 2 * gh * gdk - 2 * gh * gdv - 2 * lrw - 2 * D) // 2
    hkv = kvw // hd
    gcols = w_gate_f.shape[-1]
    F2 = NDEV * w_up.shape[-1]
    mcols = w_mod.shape[-1]

    x2, ctx2, tgt2 = x[0], ctx[0], loss_target[0]

    c_all = exchange([jnp.pad(c, ((0, 7), (0, 0)))], True, "gather_c")[0][:, 0, :]
    c9 = jnp.concatenate([c_all, c_ctx[None, :], jnp.zeros((7, D), F32)], axis=0)
    s9 = rowwise(lambda a: a * _sig(a), [c9], [], [(D, F32)], [], 16, "silu_c")[0]
    bias = jnp.broadcast_to(lax.dynamic_slice_in_dim(b_mod, me * mcols, mcols, axis=1), (16, mcols))
    mod_cols = matmul(s9, w_mod[0], "nn", F32, "mod_cols", add=bias)
    mod_all = exchange([mod_cols], True, "gather_mod")[0]
    mod_all = jnp.transpose(mod_all, (1, 0, 2)).reshape(16, NDEV * mcols)
    mod_me = lax.dynamic_slice_in_dim(mod_all, me, 1, axis=0)
    mod = [mod_me[:, i * D:(i + 1) * D] for i in range(6)]
    modc = [mod_all[8:9, i * D:(i + 1) * D] for i in range(2)]

    o3 = jnp.stack([w_attn_o[0], w_gla_o[0], w_out[0]]).astype(BF16)
    small_w = pack([w_gate_f[0], w_gate_b[0], conv_w[0]])
    g_in, g_o3, g_up, g_down, g_small = gather_two_level(
        [w_in[0].astype(BF16), o3, w_up[0].astype(BF16), w_down[0].astype(BF16), small_w], "gather_w")
    seg = segments_from_blocks(g_in, [hq * hd, 2 * kvw, 2 * gh * gdk, gh * gdv, gh * gdv, 2 * lrw, 2 * D])
    small_parts = [unpack(g_small[j], [w_gate_f[0].shape, w_gate_b[0].shape, conv_w[0].shape]) for j in range(NDEV)]
    wgf = jnp.concatenate([p[0] for p in small_parts], axis=1)
    wgb = jnp.concatenate([p[1] for p in small_parts], axis=1)
    cw_full = jnp.concatenate([p[2] for p in small_parts], axis=1)
    o3f = [g_o3[:, i].reshape(-1, D) for i in range(3)]
    W = dict(
        q=seg[0], kv=seg[1], gqk=seg[2], gv=seg[3], rb=seg[4],
        lr=jnp.pad(seg[5], ((0, 0), (0, 128 - 2 * lrw))), gab=seg[6],
        gate_f=jnp.pad(wgf, ((0, 128 - lrw), (0, 0))),
        gate_b=jnp.pad(wgb, ((lrw, 128 - 2 * lrw), (0, 0))),
        attn_o=o3f[0], gla_o=o3f[1], out=o3f[2],
        up=jnp.concatenate([g_up[j] for j in range(NDEV)], axis=1),
        down=g_down.reshape(-1, D),
        conv_w=cw_full,
    )
    P = dict(hd=hd, hq=hq, hkv=hkv, gh=gh, lowrank=lrw, g_mix=g_mix, q_norm=q_norm, k_norm=k_norm, attn_sink=attn_sink,
             b_gate_f=b_gate_f, b_gate_b=b_gate_b, gla_norm=gla_norm, g_ffn=g_ffn, conv_b=conv_b)

    lsum, grad_x, G = local_step(x2, ctx2, tgt2, mod, modc, W, P)
    loss = lax.psum(0.5 * lsum / D, ("x", "y", "c"))

    dm = exchange([jnp.concatenate([G["dmod_x"], G["dmod_c"], jnp.zeros((6, 6 * D), F32)], axis=0)], True,
                  "gather_dmod")[0]
    dmc = reduce_parts(dm[:, 1:2, :].reshape(NDEV, 6 * D // 128, 128), "sum_dmod_ctx").reshape(1, 6 * D)
    dM = jnp.concatenate([dm[:, 0, :], dmc, jnp.zeros((7, 6 * D), F32)], axis=0)
    dM_cols = lax.dynamic_slice_in_dim(dM, me * mcols, mcols, axis=1)
    g_w_mod = matmul(s9, dM_cols, "tn", F32, "dw_mod")
    g_b_mod = reduce_parts(dM.reshape(16, 6 * D // 128, 128), "sum_db_mod").reshape(1, 6 * D)
    dsc = matmul(dM_cols[8:16], w_mod[0], "nt", F32, "d_silu_ctx")
    cc = jnp.broadcast_to(c_ctx[None, :], (8, D))

    def dsilu_fn(d, a):
        sg = _sig(a)
        return d * sg * (1.0 + a * (1.0 - sg))

    g_cctx_part = rowwise(dsilu_fn, [dsc, cc], [], [(D, F32)], [], 8, "d_c_ctx")[0][0:1]

    small_names = ("c_ctx", "g_mix", "q_norm", "k_norm", "attn_sink", "b_gate_f", "b_gate_b", "gla_norm", "g_ffn",
                   "conv_b", "w_gate_f", "w_gate_b", "conv_w")
    G["c_ctx"] = g_cctx_part
    sm_shapes = [G[n].shape for n in small_names]
    sm_all = exchange([pack([G[n] for n in small_names])], True, "gather_small_grads")[0]
    sm_tot = unpack(reduce_parts(sm_all, "sum_small_grads"), sm_shapes)
    gs = dict(zip(small_names, sm_tot))
    gs["b_mod"] = g_b_mod
    gs["w_gate_f"] = lax.dynamic_slice_in_dim(gs["w_gate_f"], me * gcols, gcols, axis=1)
    gs["w_gate_b"] = lax.dynamic_slice_in_dim(gs["w_gate_b"], me * gcols, gcols, axis=1)
    ccols = conv_w.shape[-1]
    gs["conv_w"] = lax.dynamic_slice_in_dim(gs["conv_w"], me * ccols, ccols, axis=1)

    orows = w_attn_o.shape[1]
    s_in = blocks_from_segments(G["w_in"], w_in.shape[-1])
    s_o3 = jnp.concatenate([rows_to_blocks(G["w_attn_o"]), rows_to_blocks(G["w_gla_o"]), rows_to_blocks(G["w_out"])],
                           axis=1)
    s_up = blocks_from_segments(G["w_up"], w_up.shape[-1])
    s_down = rows_to_blocks(G["w_down"])
    r_in, r_o3, r_up, r_down = scatter_reduce([s_in, s_o3, s_up, s_down], "scatter_grads")

    out = {}
    out["w_in"] = adam_reduce(r_in, w_in[0], m_w_in[0], v_w_in[0], "adam_w_in")
    o3w = jnp.concatenate([w_attn_o[0], w_gla_o[0], w_out[0]], axis=0)
    o3m = jnp.concatenate([m_w_attn_o[0], m_w_gla_o[0], m_w_out[0]], axis=0)
    o3v = jnp.concatenate([v_w_attn_o[0], v_w_gla_o[0], v_w_out[0]], axis=0)
    ro3 = adam_reduce(r_o3, o3w, o3m, o3v, "adam_o3")
    for i, n in enumerate(("w_attn_o", "w_gla_o", "w_out")):
        out[n] = [a[i * orows:(i + 1) * orows] for a in ro3]
    out["w_up"] = adam_reduce(r_up, w_up[0], m_w_up[0], v_w_up[0], "adam_w_up")
    out["w_down"] = adam_reduce(r_down, w_down[0], m_w_down[0], v_w_down[0], "adam_w_down")
    out["w_mod"] = adam_reduce(g_w_mod[None], w_mod[0], m_w_mod[0], v_w_mod[0], "adam_w_mod")
    sm_names = SMALL_REPL + SMALL_SHARD
    shapes = [Wt[n].shape for n in sm_names]
    rs = adam_reduce(pack([gs[n] for n in sm_names])[None], pack([Wt[n] for n in sm_names]),
                     pack([Mt[n] for n in sm_names]), pack([Vt[n] for n in sm_names]), "adam_small")
    rs = [unpack(a, shapes) for a in rs]
    for i, n in enumerate(sm_names):
        out[n] = [a[i] for a in rs]

    res = [loss, grad_x[None]]
    for k in range(4):
        for n in ORDER:
            res.append(out[n][k].reshape(Wt[n].shape))
    return tuple(res)
```

```python
import jax
import jax.numpy as jnp
import numpy as np
from jax import lax
from jax.experimental import pallas as pl
from jax.experimental.pallas import tpu as pltpu

F32 = jnp.float32
BF16 = jnp.bfloat16

NDEV = 8
NCHIP = 4
EPS = 1e-6
WINDOW = 128
BLOCK = 128
GRID_W = 64
ROPE_THETA = 10000.0
GLA_CHUNK = 128
GLA_GATE_NORM = 16.0
ADAM_LR = 0.001
ADAM_B1 = 0.9
ADAM_B2 = 0.999
ADAM_EPS = 1e-08
ADAM_WD = 0.01
ADAM_STEP = 10
V7X_VMEM_LIMIT = 56 * 1024 * 1024
MATMUL_VMEM_BUDGET = 40 * 1024 * 1024
NEG = -1e30

NN = ((1,), (0,))
NT = ((1,), (1,))
TN = ((0,), (0,))


def _dot(a, b, dims):
    return lax.dot_general(a, b, (dims, ((), ())), preferred_element_type=F32)


def _cp(sem):
    return pltpu.CompilerParams(dimension_semantics=sem, vmem_limit_bytes=V7X_VMEM_LIMIT)


def _pick(n, cands):
    for c in cands:
        if n % c == 0:
            return c
    return n


def _sig(x):
    return 1.0 / (1.0 + jnp.exp(-x))


def _sig_tanh(x):
    return 0.5 * jnp.tanh(0.5 * x) + 0.5


def _rstd(x):
    return lax.rsqrt(jnp.mean(x * x, axis=-1, keepdims=True) + EPS)


_ANY = pl.BlockSpec(memory_space=pl.ANY)


def _place():
    return lax.axis_index("x"), lax.axis_index("y"), lax.axis_index("c")


def exchange(srcs, bcast, name, group="all"):
    n = len(srcs)
    ndev = NDEV if group == "all" else NCHIP
    ks = tuple(range(1, NDEV)) if group == "all" else (2, 4, 6)
    out_shape = [jax.ShapeDtypeStruct((ndev,) + (s.shape if bcast else s.shape[1:]), s.dtype) for s in srcs]

    def body(*refs):
        src, dst = refs[:n], refs[n:2 * n]
        send_sems, recv_sems, loc_sems = refs[2 * n:]
        x, y, c = _place()

        def idx(px, py, pc):
            return 4 * px + 2 * py + pc if group == "all" else 2 * px + py

        me = idx(x, y, c)
        copies = []
        for a in range(n):
            cp = pltpu.make_async_copy(src[a] if bcast else src[a].at[me], dst[a].at[me], loc_sems.at[a])
            cp.start()
            copies.append(cp)
        for s, k in enumerate(ks):
            px, py, pc = x ^ ((k >> 2) & 1), y ^ ((k >> 1) & 1), c ^ (k & 1)
            for a in range(n):
                cp = pltpu.make_async_remote_copy(
                    src_ref=src[a] if bcast else src[a].at[idx(px, py, pc)],
                    dst_ref=dst[a].at[me],
                    send_sem=send_sems.at[a, s],
                    recv_sem=recv_sems.at[a, s],
                    device_id=(px, py, pc),
                    device_id_type=pl.DeviceIdType.MESH,
                )
                cp.start()
                copies.append(cp)
        for cp in copies:
            cp.wait()

    return pl.pallas_call(
        body,
        out_shape=out_shape,
        in_specs=[_ANY] * n,
        out_specs=[_ANY] * n,
        scratch_shapes=[
            pltpu.SemaphoreType.DMA((n, len(ks))),
            pltpu.SemaphoreType.DMA((n, len(ks))),
            pltpu.SemaphoreType.DMA((n,)),
        ],
        name=name,
    )(*srcs)


def gather_two_level(srcs, name):
    n = len(srcs)
    out_shape = [jax.ShapeDtypeStruct((NDEV,) + s.shape, s.dtype) for s in srcs]

    def body(*refs):
        src, dst = refs[:n], refs[n:2 * n]
        send_sems, recv_sems, loc_sems = refs[2 * n:]
        x, y, c = _place()
        me = 4 * x + 2 * y + c
        sib = (x, y, 1 - c)
        first = (x ^ (1 - c), y ^ c)
        second = (x ^ c, y ^ (1 - c))
        diag = (x ^ 1, y ^ 1)

        def row(chip, core):
            return 4 * chip[0] + 2 * chip[1] + core

        def copy(a, s, block, to, from_src=False):
            return pltpu.make_async_remote_copy(
                src_ref=src[a] if from_src else dst[a].at[block], dst_ref=dst[a].at[block],
                send_sem=send_sems.at[a, s], recv_sem=recv_sems.at[a, s],
                device_id=to, device_id_type=pl.DeviceIdType.MESH)

        local = [pltpu.make_async_copy(src[a], dst[a].at[me], loc_sems.at[a]) for a in range(n)]
        sent = [copy(a, 0, me, sib, True) for a in range(n)]
        sent += [copy(a, 1, me, (*first, c), True) for a in range(n)]
        sent += [copy(a, 2, me, (*second, c), True) for a in range(n)]
        for cp in local + sent:
            cp.start()
        for a in range(n):
            copy(a, 1, row(first, c), (*first, c)).wait_recv()
            for cp in (copy(a, 3, row(first, c), (*second, c)), copy(a, 5, row(first, c), sib)):
                cp.start()
                sent.append(cp)
        for a in range(n):
            copy(a, 2, row(second, c), (*second, c)).wait_recv()
            cp = copy(a, 4, row(second, c), sib)
            cp.start()
            sent.append(cp)
        for a in range(n):
            copy(a, 3, row(diag, c), (*second, c)).wait_recv()
            cp = copy(a, 6, row(diag, c), sib)
            cp.start()
            sent.append(cp)
        for a in range(n):
            copy(a, 0, row((x, y), 1 - c), sib).wait_recv()
            copy(a, 4, row(first, 1 - c), sib).wait_recv()
            copy(a, 5, row(second, 1 - c), sib).wait_recv()
            copy(a, 6, row(diag, 1 - c), sib).wait_recv()
        for cp in local:
            cp.wait()
        for cp in sent:
            cp.wait_send()

    return pl.pallas_call(
        body,
        out_shape=out_shape,
        in_specs=[_ANY] * n,
        out_specs=[_ANY] * n,
        scratch_shapes=[
            pltpu.SemaphoreType.DMA((n, NDEV - 1)),
            pltpu.SemaphoreType.DMA((n, NDEV - 1)),
            pltpu.SemaphoreType.DMA((n,)),
        ],
        name=name,
    )(*srcs)


def _chip_across(core, da, db):
    x, y, _ = _place()
    return x ^ (da * (1 - core) + db * core), y ^ (db * (1 - core) + da * core)


def pair_swap(srcs, name, axis="c"):
    n = len(srcs)

    def body(*refs):
        src, dst = refs[:n], refs[n:2 * n]
        send_sems, recv_sems = refs[2 * n:]
        x, y, c = _place()
        partner = {"c": (x, y, 1 - c), "first": (*_chip_across(c, 1, 0), c), "second": (*_chip_across(c, 0, 1), c)}[axis]
        copies = []
        for a in range(n):
            cp = pltpu.make_async_remote_copy(
                src_ref=src[a], dst_ref=dst[a], send_sem=send_sems.at[a], recv_sem=recv_sems.at[a],
                device_id=partner, device_id_type=pl.DeviceIdType.MESH)
            cp.start()
            copies.append(cp)
        for cp in copies:
            cp.wait()

    return pl.pallas_call(
        body,
        out_shape=[jax.ShapeDtypeStruct(s.shape, s.dtype) for s in srcs],
        in_specs=[_ANY] * n,
        out_specs=[_ANY] * n,
        scratch_shapes=[pltpu.SemaphoreType.DMA((n,)), pltpu.SemaphoreType.DMA((n,))],
        name=name,
    )(*srcs)


_OFFSETS = ((0, 0), (0, 1), (1, 0), (1, 1))


def sibling_swap_blocks(blocks, name):
    n = len(blocks)

    def body(*refs):
        src, dst = refs[:n], refs[n:2 * n]
        send_sems, recv_sems = refs[2 * n:]
        x, y, c = _place()
        copies = []
        for a in range(n):
            for j, (da, db) in enumerate(_OFFSETS):
                px, py = _chip_across(1 - c, da, db)
                cp = pltpu.make_async_remote_copy(
                    src_ref=src[a].at[4 * px + 2 * py + (1 - c)], dst_ref=dst[a].at[j],
                    send_sem=send_sems.at[a, j], recv_sem=recv_sems.at[a, j],
                    device_id=(x, y, 1 - c), device_id_type=pl.DeviceIdType.MESH)
                cp.start()
                copies.append(cp)
        for cp in copies:
            cp.wait()

    return pl.pallas_call(
        body,
        out_shape=[jax.ShapeDtypeStruct((4,) + b.shape[1:], b.dtype) for b in blocks],
        in_specs=[_ANY] * n,
        out_specs=[_ANY] * n,
        scratch_shapes=[pltpu.SemaphoreType.DMA((n, 4)), pltpu.SemaphoreType.DMA((n, 4))],
        name=name,
    )(*blocks)


def add_own_blocks(blocks, got, name):
    _, R, C = blocks.shape
    tile = _pick(R, (256, 128, 64))

    def body(*refs):
        for j in range(4):
            refs[8 + j][...] = (refs[j][...].astype(F32) + refs[4 + j][...].astype(F32)).astype(refs[8 + j].dtype)

    def own(da, db):
        def index(i):
            c = lax.axis_index("c")
            px, py = _chip_across(c, da, db)
            return 4 * px + 2 * py + c, i, 0
        return pl.BlockSpec((None, tile, C), index)

    return pl.pallas_call(
        body,
        grid=(R // tile,),
        in_specs=[own(da, db) for da, db in _OFFSETS]
        + [pl.BlockSpec((None, tile, C), lambda i, j=j: (j, i, 0)) for j in range(4)],
        out_specs=[pl.BlockSpec((tile, C), lambda i: (i, 0))] * 4,
        out_shape=[jax.ShapeDtypeStruct((R, C), blocks.dtype)] * 4,
        compiler_params=_cp(("parallel",)),
        name=name,
    )(blocks, blocks, blocks, blocks, got, got, got, got)


def scatter_reduce(blocks, name):
    def add(n_out, ins, label):
        fn = lambda *a: [a[i].astype(F32) + a[n_out + i].astype(F32) for i in range(n_out)]
        rows, cols = ins[0].shape
        return rowwise(fn, ins, [], [(cols, ins[0].dtype)] * n_out, [], _pick(rows, (256, 128, 64)), label)

    nb = len(blocks)
    got = sibling_swap_blocks(blocks, name + "_d2d")
    q = [add_own_blocks(blocks[i], got[i], f"{name}_sum0_{i}") for i in range(nb)]
    r1 = pair_swap([q[i][j] for i in range(nb) for j in (2, 3)], name + "_ici1", "first")
    k = [add(2, [q[i][0], q[i][1], r1[2 * i], r1[2 * i + 1]], f"{name}_sum1_{i}") for i in range(nb)]
    r2 = pair_swap([k[i][1] for i in range(nb)], name + "_ici2", "second")
    return [jnp.stack([k[i][0], r2[i]]) for i in range(nb)]


def matmul(a, b, mode, out_dtype, name, add=None):
    if mode == "nn":
        (M, K), N = a.shape, b.shape[1]
    elif mode == "nt":
        (M, K), N = a.shape, b.shape[0]
    else:
        (K, M), N = a.shape, b.shape[1]
    tm = _pick(M, (1024, 512, 256, 128))
    tn = _pick(N, (1024, 512, 256, 128))
    osz = jnp.dtype(out_dtype).itemsize

    def vmem_bytes(tk):
        ops = 2 * tk * (tm * a.dtype.itemsize + tn * b.dtype.itemsize)
        return ops + tm * tn * (2 * osz + (4 if tk < K else 0) + (8 if add is not None else 0))

    tk = next((t for t in (K, 2816, 2048, 1408, 1024, 512, 256, 128) if K % t == 0 and vmem_bytes(t) <= MATMUL_VMEM_BUDGET), K)
    nk = K // tk
    dims = {"nn": NN, "nt": NT, "tn": TN}[mode]

    def body(*refs):
        if add is None:
            a_ref, b_ref, o_ref = refs[:3]
            c_ref = None
        else:
            a_ref, b_ref, c_ref, o_ref = refs[:4]

        def prod():
            return _dot(a_ref[...].astype(BF16), b_ref[...].astype(BF16), dims)

        def finish(r):
            if c_ref is not None:
                r = r + c_ref[...].astype(F32)
            o_ref[...] = r.astype(o_ref.dtype)

        if nk == 1:
            finish(prod())
            return
        acc = refs[-1]
        k = pl.program_id(2)

        @pl.when(k == 0)
        def _():
            acc[...] = prod()

        if nk > 2:
            @pl.when((k > 0) & (k < nk - 1))
            def _():
                acc[...] += prod()

        @pl.when(k == nk - 1)
        def _():
            finish(acc[...] + prod())

    a_spec = pl.BlockSpec((tk, tm), lambda i, j, k: (k, i)) if mode == "tn" else pl.BlockSpec((tm, tk), lambda i, j, k: (i, k))
    b_spec = pl.BlockSpec((tn, tk), lambda i, j, k: (j, k)) if mode == "nt" else pl.BlockSpec((tk, tn), lambda i, j, k: (k, j))
    o_spec = pl.BlockSpec((tm, tn), lambda i, j, k: (i, j))
    ins, specs = [a, b], [a_spec, b_spec]
    if add is not None:
        ins.append(add)
        specs.append(o_spec)
    return pl.pallas_call(
        body,
        grid=(M // tm, N // tn, nk),
        in_specs=specs,
        out_specs=o_spec,
        out_shape=jax.ShapeDtypeStruct((M, N), out_dtype),
        scratch_shapes=[pltpu.VMEM((tm, tn), F32)] if nk > 1 else [],
        compiler_params=_cp(("parallel", "parallel", "arbitrary")),
        name=name,
    )(*ins)


def rowwise(fn, tiled, full, out_tiled, out_acc, tile, name):
    tiled = [t if isinstance(t, tuple) else (t, t.shape[1], 0) for t in tiled]
    rows = tiled[0][0].shape[0]
    tile = min(tile, rows)
    assert rows % tile == 0
    nt, nf, no = len(tiled), len(full), len(out_tiled)

    def body(*refs):
        ins = [r[...] for r in refs[:nt + nf]]
        res = fn(*ins)
        if not isinstance(res, (tuple, list)):
            res = (res,)
        outs = refs[nt + nf:]
        for r, v in zip(outs[:no], res[:no]):
            r[...] = v.astype(r.dtype)
        if out_acc:
            @pl.when(pl.program_id(0) == 0)
            def _():
                for r in outs[no:]:
                    r[...] = jnp.zeros_like(r)

            for r, v in zip(outs[no:], res[no:]):
                r[...] += v

    in_specs = [pl.BlockSpec((tile, w), lambda i, cb=cb: (i, cb)) for (_, w, cb) in tiled]
    in_specs += [pl.BlockSpec(f.shape, lambda i, nd=f.ndim: (0,) * nd) for f in full]
    out_specs = [pl.BlockSpec((tile, w), lambda i: (i, 0)) for (w, _) in out_tiled]
    out_specs += [pl.BlockSpec(s, lambda i, nd=len(s): (0,) * nd) for s in out_acc]
    out_shape = [jax.ShapeDtypeStruct((rows, w), dt) for (w, dt) in out_tiled]
    out_shape += [jax.ShapeDtypeStruct(s, F32) for s in out_acc]
    res = pl.pallas_call(
        body,
        grid=(rows // tile,),
        in_specs=in_specs,
        out_specs=out_specs,
        out_shape=out_shape,
        compiler_params=_cp(("arbitrary",) if out_acc else ("parallel",)),
        name=name,
    )(*[t[0] for t in tiled], *full)
    return res


def adam_reduce(parts, w, m, v, name):
    P, R, C = parts.shape
    tr = _pick(R, (64, 32, 16, 8))
    c1 = 1.0 - ADAM_B1 ** ADAM_STEP
    c2 = 1.0 - ADAM_B2 ** ADAM_STEP

    def body(p_ref, w_ref, m_ref, v_ref, g_ref, d_ref, nm_ref, nv_ref):
        g = p_ref[0].astype(F32)
        for j in range(1, P):
            g = g + p_ref[j].astype(F32)
        mm = ADAM_B1 * m_ref[...] + (1.0 - ADAM_B1) * g
        vv = ADAM_B2 * v_ref[...] + (1.0 - ADAM_B2) * (g * g)
        m_hat = mm / c1
        v_hat = vv / c2
        g_ref[...] = g
        d_ref[...] = -ADAM_LR * (m_hat / (jnp.sqrt(v_hat) + ADAM_EPS) + ADAM_WD * w_ref[...])
        nm_ref[...] = mm
        nv_ref[...] = vv

    spec = pl.BlockSpec((tr, C), lambda i: (i, 0))
    return pl.pallas_call(
        body,
        grid=(R // tr,),
        in_specs=[pl.BlockSpec((P, tr, C), lambda i: (0, i, 0)), spec, spec, spec],
        out_specs=[spec] * 4,
        out_shape=[jax.ShapeDtypeStruct((R, C), F32)] * 4,
        compiler_params=_cp(("parallel",)),
        name=name,
    )(parts, w, m, v)


def reduce_parts(parts, name):
    P, R, C = parts.shape
    tr = _pick(R, (64, 32, 16, 8))

    def body(p_ref, g_ref):
        g = p_ref[0]
        for j in range(1, P):
            g = g + p_ref[j]
        g_ref[...] = g

    return pl.pallas_call(
        body,
        grid=(R // tr,),
        in_specs=[pl.BlockSpec((P, tr, C), lambda i: (0, i, 0))],
        out_specs=pl.BlockSpec((tr, C), lambda i: (i, 0)),
        out_shape=jax.ShapeDtypeStruct((R, C), F32),
        compiler_params=_cp(("parallel",)),
        name=name,
    )(parts)


def pack(arrs):
    flat = jnp.concatenate([a.reshape(-1).astype(F32) for a in arrs])
    n = flat.shape[0]
    padded = -(-n // 1024) * 1024
    return jnp.pad(flat, (0, padded - n)).reshape(padded // 128, 128)


def blocks_from_segments(segs, ncols):
    offs = np.cumsum([0] + [s.shape[1] for s in segs]).tolist()
    blocks = []
    for j in range(NDEV):
        lo, hi = j * ncols, (j + 1) * ncols
        parts = [s[:, max(lo, o) - o:min(hi, o + s.shape[1]) - o]
                 for s, o in zip(segs, offs[:-1]) if max(lo, o) < min(hi, o + s.shape[1])]
        blocks.append(jnp.concatenate(parts, axis=1) if len(parts) > 1 else parts[0])
    return jnp.stack(blocks)


def rows_to_blocks(g):
    return g.reshape(NDEV, -1, g.shape[1])


def segments_from_blocks(g, widths):
    ncols = g.shape[2]
    offs = np.cumsum([0] + list(widths)).tolist()
    out = []
    for o, w in zip(offs[:-1], widths):
        parts = [g[j][:, max(j * ncols, o) - j * ncols:min((j + 1) * ncols, o + w) - j * ncols]
                 for j in range(NDEV) if max(j * ncols, o) < min((j + 1) * ncols, o + w)]
        out.append(jnp.concatenate(parts, axis=1) if len(parts) > 1 else parts[0])
    return out


def unpack(slab, shapes):
    flat = slab.reshape(-1)
    out, off = [], 0
    for s in shapes:
        size = int(np.prod(s))
        out.append(flat[off:off + size].reshape(s))
        off += size
    return out


def modulate_fwd(x, g, sh, sc, name):
    def fn(x, g, sh, sc):
        return x * _rstd(x) * g * (1.0 + sc) + sh

    return rowwise(fn, [x], [g, sh, sc], [(x.shape[1], BF16)], [], 256, name)[0]


def norm_rope_fwd(p, width, cb, w, cosf, sinf, hd, name):
    nh = width // hd

    def fn(x, cosf, sinf, w):
        outs = []
        for h in range(nh):
            xh = x[:, h * hd:(h + 1) * hd]
            y = xh * _rstd(xh) * w
            outs.append(y * cosf + pltpu.roll(y, hd // 2, 1) * sinf)
        return jnp.concatenate(outs, axis=1) if nh > 1 else outs[0]

    return rowwise(fn, [(p, width, cb), cosf, sinf], [w], [(width, BF16)], [], 256, name)[0]


def norm_rope_bwd(p, width, cb, d, w, cosf, sinf, hd, name):
    nh = width // hd

    def fn(x, d, cosf, sinf, w):
        outs = []
        dw = jnp.zeros((1, hd), F32)
        for h in range(nh):
            xh = x[:, h * hd:(h + 1) * hd]
            dh = d[:, h * hd:(h + 1) * hd].astype(F32)
            r = _rstd(xh)
            n = xh * r
            dy = dh * cosf + pltpu.roll(dh * sinf, hd // 2, 1)
            dw = dw + jnp.sum(dy * n, axis=0, keepdims=True)
            dn = dy * w
            outs.append(r * (dn - n * jnp.mean(dn * n, axis=-1, keepdims=True)))
        return (jnp.concatenate(outs, axis=1) if nh > 1 else outs[0]), dw

    return rowwise(fn, [(p, width, cb), d, cosf, sinf], [w], [(width, BF16)], [(1, hd)], 256, name)


def attention_fwd(qr, kr, pkv, kcr, pkv_c, sink, hkv, hd, name):
    T, L = qr.shape[0], kcr.shape[0]
    G = qr.shape[1] // (hkv * hd)
    nb = T // BLOCK
    scale = hd ** -0.5

    def body(q_ref, kp, kc, kn, vp, vc, vn, ck_ref, cv_ref, sink_ref, o_ref, lse_ref, lser_ref):
        i = pl.program_id(1)
        kwin = jnp.concatenate([kp[...], kc[...], kn[...]], axis=0)
        vwin = jnp.concatenate([vp[...], vc[...], vn[...]], axis=0).astype(BF16)
        ck, cv = ck_ref[...], cv_ref[...].astype(BF16)
        row = lax.broadcasted_iota(jnp.int32, (BLOCK, 3 * BLOCK), 0)
        col = lax.broadcasted_iota(jnp.int32, (BLOCK, 3 * BLOCK), 1)
        rel = col - BLOCK - row
        valid = (jnp.abs(rel) <= WINDOW) & ((col >= BLOCK) | (i > 0)) & ((col < 2 * BLOCK) | (i < nb - 1))
        R = range(G)
        qa = q_ref[...]
        qs = [qa[:, g * hd:(g + 1) * hd] for g in R]
        sks = [sink_ref[g] for g in R]
        ss = [jnp.where(valid, _dot(qs[g], kwin, NT) * scale, NEG) for g in R]
        scs = [_dot(qs[g], ck, NT) * scale for g in R]
        ms = [jnp.maximum(jnp.maximum(jnp.max(ss[g], axis=1, keepdims=True), jnp.max(scs[g], axis=1, keepdims=True)),
                          sks[g]) for g in R]
        ps = [jnp.exp(ss[g] - ms[g]) for g in R]
        pcs = [jnp.exp(scs[g] - ms[g]) for g in R]
        nums = [_dot(ps[g].astype(BF16), vwin, NN) + _dot(pcs[g].astype(BF16), cv, NN) for g in R]
        dens = [jnp.exp(sks[g] - ms[g]) + jnp.sum(ps[g], axis=1, keepdims=True) + jnp.sum(pcs[g], axis=1, keepdims=True)
                for g in R]
        o_ref[...] = jnp.concatenate([(nums[g] / dens[g]).astype(o_ref.dtype) for g in R], axis=1)
        eye = (lax.broadcasted_iota(jnp.int32, (BLOCK, BLOCK), 0)
               == lax.broadcasted_iota(jnp.int32, (BLOCK, BLOCK), 1)).astype(F32)
        for g in R:
            lg = ms[g] + jnp.log(dens[g])
            lse_ref[g] = lg
            lser_ref[g] = jnp.sum(lg * eye, axis=0, keepdims=True)

    kv_specs = [
        pl.BlockSpec((BLOCK, hd), lambda h, i: (jnp.maximum(i - 1, 0), h)),
        pl.BlockSpec((BLOCK, hd), lambda h, i: (i, h)),
        pl.BlockSpec((BLOCK, hd), lambda h, i: (jnp.minimum(i + 1, nb - 1), h)),
    ]
    v_specs = [
        pl.BlockSpec((BLOCK, hd), lambda h, i: (jnp.maximum(i - 1, 0), hkv + h)),
        pl.BlockSpec((BLOCK, hd), lambda h, i: (i, hkv + h)),
        pl.BlockSpec((BLOCK, hd), lambda h, i: (jnp.minimum(i + 1, nb - 1), hkv + h)),
    ]
    return pl.pallas_call(
        body,
        grid=(hkv, nb),
        in_specs=[pl.BlockSpec((BLOCK, G * hd), lambda h, i: (i, h))] + kv_specs + v_specs + [
            pl.BlockSpec((L, hd), lambda h, i: (0, h)),
            pl.BlockSpec((L, hd), lambda h, i: (0, hkv + h)),
            pl.BlockSpec((G, 1, 1), lambda h, i: (h, 0, 0)),
        ],
        out_specs=[
            pl.BlockSpec((BLOCK, G * hd), lambda h, i: (i, h)),
            pl.BlockSpec((G, BLOCK, 1), lambda h, i: (h, i, 0)),
            pl.BlockSpec((G, 1, BLOCK), lambda h, i: (h, 0, i)),
        ],
        out_shape=[jax.ShapeDtypeStruct(qr.shape, BF16), jax.ShapeDtypeStruct((hkv * G, T, 1), F32),
                   jax.ShapeDtypeStruct((hkv * G, 1, T), F32)],
        compiler_params=_cp(("parallel", "parallel")),
        name=name,
    )(qr, kr, kr, kr, pkv, pkv, pkv, kcr, pkv_c, sink)


def attention_bwd_q(qr, kr, pkv, kcr, pkv_c, sink, do, o, lse, hkv, hd, name):
    T, L = qr.shape[0], kcr.shape[0]
    G = qr.shape[1] // (hkv * hd)
    nb = T // BLOCK
    scale = hd ** -0.5

    def body(q_ref, kp, kc, kn, vp, vc, vn, ck_ref, cv_ref, sink_ref, do_ref, o_ref, lse_ref,
             dq_ref, dck_ref, dcv_ref, dsink_ref, drr_ref):
        i = pl.program_id(1)

        @pl.when(i == 0)
        def _():
            dck_ref[...] = jnp.zeros_like(dck_ref)
            dcv_ref[...] = jnp.zeros_like(dcv_ref)
            dsink_ref[...] = jnp.zeros_like(dsink_ref)

        kwin = jnp.concatenate([kp[...], kc[...], kn[...]], axis=0)
        vwin = jnp.concatenate([vp[...], vc[...], vn[...]], axis=0).astype(BF16)
        ck, cv = ck_ref[...], cv_ref[...].astype(BF16)
        row = lax.broadcasted_iota(jnp.int32, (BLOCK, 3 * BLOCK), 0)
        col = lax.broadcasted_iota(jnp.int32, (BLOCK, 3 * BLOCK), 1)
        rel = col - BLOCK - row
        valid = (jnp.abs(rel) <= WINDOW) & ((col >= BLOCK) | (i > 0)) & ((col < 2 * BLOCK) | (i < nb - 1))
        R = range(G)
        qa, doa, oa = q_ref[...], do_ref[...], o_ref[...]
        qs = [qa[:, g * hd:(g + 1) * hd] for g in R]
        dos = [doa[:, g * hd:(g + 1) * hd] for g in R]
        lgs = [lse_ref[g] for g in R]
        sks = [sink_ref[g] for g in R]
        ss = [jnp.where(valid, _dot(qs[g], kwin, NT) * scale, NEG) for g in R]
        scs = [_dot(qs[g], ck, NT) * scale for g in R]
        dps = [_dot(dos[g], vwin, NT) for g in R]
        dpcs = [_dot(dos[g], cv, NT) for g in R]
        drs = [jnp.sum(dos[g].astype(F32) * oa[:, g * hd:(g + 1) * hd].astype(F32), axis=1, keepdims=True) for g in R]
        ps = [jnp.exp(ss[g] - lgs[g]) for g in R]
        pcs = [jnp.exp(scs[g] - lgs[g]) for g in R]
        dss = [(ps[g] * (dps[g] - drs[g]) * scale).astype(BF16) for g in R]
        dscs = [(pcs[g] * (dpcs[g] - drs[g]) * scale).astype(BF16) for g in R]
        dqs = [_dot(dss[g], kwin, NN) + _dot(dscs[g], ck, NN) for g in R]
        dcks = [_dot(dscs[g], qs[g], TN) for g in R]
        dcvs = [_dot(pcs[g].astype(BF16), dos[g], TN) for g in R]
        dq_ref[...] = jnp.concatenate(dqs, axis=1)
        dck_ref[...] += (dcks[0] + dcks[1]) + (dcks[2] + dcks[3]) if G == 4 else sum(dcks[1:], dcks[0])
        dcv_ref[...] += (dcvs[0] + dcvs[1]) + (dcvs[2] + dcvs[3]) if G == 4 else sum(dcvs[1:], dcvs[0])
        eye = (lax.broadcasted_iota(jnp.int32, (BLOCK, BLOCK), 0)
               == lax.broadcasted_iota(jnp.int32, (BLOCK, BLOCK), 1)).astype(F32)
        for g in R:
            dsink_ref[g] += -jnp.sum(jnp.exp(sks[g] - lgs[g]) * drs[g], axis=0, keepdims=True)
            drr_ref[g] = jnp.sum(drs[g] * eye, axis=0, keepdims=True)

    kv_specs = [
        pl.BlockSpec((BLOCK, hd), lambda h, i: (jnp.maximum(i - 1, 0), h)),
        pl.BlockSpec((BLOCK, hd), lambda h, i: (i, h)),
        pl.BlockSpec((BLOCK, hd), lambda h, i: (jnp.minimum(i + 1, nb - 1), h)),
    ]
    v_specs = [
        pl.BlockSpec((BLOCK, hd), lambda h, i: (jnp.maximum(i - 1, 0), hkv + h)),
        pl.BlockSpec((BLOCK, hd), lambda h, i: (i, hkv + h)),
        pl.BlockSpec((BLOCK, hd), lambda h, i: (jnp.minimum(i + 1, nb - 1), hkv + h)),
    ]
    qspec = pl.BlockSpec((BLOCK, G * hd), lambda h, i: (i, h))
    return pl.pallas_call(
        body,
        grid=(hkv, nb),
        in_specs=[qspec] + kv_specs + v_specs + [
            pl.BlockSpec((L, hd), lambda h, i: (0, h)),
            pl.BlockSpec((L, hd), lambda h, i: (0, hkv + h)),
            pl.BlockSpec((G, 1, 1), lambda h, i: (h, 0, 0)),
            qspec, qspec,
            pl.BlockSpec((G, BLOCK, 1), lambda h, i: (h, i, 0)),
        ],
        out_specs=[
            qspec,
            pl.BlockSpec((L, hd), lambda h, i: (0, h)),
            pl.BlockSpec((L, hd), lambda h, i: (0, h)),
            pl.BlockSpec((G, 1, 1), lambda h, i: (h, 0, 0)),
            pl.BlockSpec((G, 1, BLOCK), lambda h, i: (h, 0, i)),
        ],
        out_shape=[
            jax.ShapeDtypeStruct(qr.shape, F32),
            jax.ShapeDtypeStruct((L, hkv * hd), F32),
            jax.ShapeDtypeStruct((L, hkv * hd), F32),
            jax.ShapeDtypeStruct((hkv * G, 1, 1), F32),
            jax.ShapeDtypeStruct((hkv * G, 1, T), F32),
        ],
        compiler_params=_cp(("parallel", "arbitrary")),
        name=name,
    )(qr, kr, kr, kr, pkv, pkv, pkv, kcr, pkv_c, sink, do, o, lse)


def attention_bwd_kv(qr, kr, pkv, do, lse_row, dr_row, hkv, hd, name):
    T = qr.shape[0]
    G = qr.shape[1] // (hkv * hd)
    nb = T // BLOCK
    scale = hd ** -0.5

    def body(k_ref, v_ref, *refs):
        qs, dos, lses, drs = refs[0:3], refs[3:6], refs[6:9], refs[9:12]
        dk_ref, dv_ref = refs[12:]
        j = pl.program_id(1)
        k = k_ref[...]
        v = v_ref[...].astype(BF16)
        row = lax.broadcasted_iota(jnp.int32, (BLOCK, BLOCK), 0)
        col = lax.broadcasted_iota(jnp.int32, (BLOCK, BLOCK), 1)
        bias = []
        for d in range(3):
            iq = j + d - 1
            rel = row - col - (d - 1) * BLOCK
            valid = (jnp.abs(rel) <= WINDOW) & (iq >= 0) & (iq < nb)
            bias += [jnp.where(valid, 0.0, NEG)] * G
        bias = jnp.concatenate(bias, axis=1)

        def stack(refs):
            vals = [r[...] for r in refs]
            return jnp.concatenate([a[:, g * hd:(g + 1) * hd] for a in vals for g in range(G)], axis=0)

        q, dob = stack(qs), stack(dos)
        lrow = jnp.concatenate([r[g] for r in lses for g in range(G)], axis=1)
        drow = jnp.concatenate([r[g] for r in drs for g in range(G)], axis=1)
        st = _dot(k, q, NT) * scale + bias
        pt = jnp.exp(st - lrow)
        dpt = _dot(v, dob, NT)
        dst = (pt * (dpt - drow) * scale).astype(BF16)
        dk_ref[...] = _dot(dst, q, NN).astype(dk_ref.dtype)
        dv_ref[...] = _dot(pt.astype(BF16), dob, NN).astype(dv_ref.dtype)

    def q3(width_block):
        return [
            pl.BlockSpec(width_block, lambda h, j: (jnp.maximum(j - 1, 0), h)),
            pl.BlockSpec(width_block, lambda h, j: (j, h)),
            pl.BlockSpec(width_block, lambda h, j: (jnp.minimum(j + 1, nb - 1), h)),
        ]

    row3 = [
        pl.BlockSpec((G, 1, BLOCK), lambda h, j: (h, 0, jnp.maximum(j - 1, 0))),
        pl.BlockSpec((G, 1, BLOCK), lambda h, j: (h, 0, j)),
        pl.BlockSpec((G, 1, BLOCK), lambda h, j: (h, 0, jnp.minimum(j + 1, nb - 1))),
    ]
    qb = (BLOCK, G * hd)
    return pl.pallas_call(
        body,
        grid=(hkv, nb),
        in_specs=[pl.BlockSpec((BLOCK, hd), lambda h, j: (j, h)), pl.BlockSpec((BLOCK, hd), lambda h, j: (j, hkv + h))]
        + q3(qb) + q3(qb) + row3 + row3,
        out_specs=[pl.BlockSpec((BLOCK, hd), lambda h, j: (j, h))] * 2,
        out_shape=[jax.ShapeDtypeStruct((T, hkv * hd), BF16)] * 2,
        compiler_params=_cp(("parallel", "parallel")),
        name=name,
    )(kr, pkv, qr, qr, qr, do, do, do, lse_row, lse_row, lse_row, dr_row, dr_row, dr_row)


def gate_fwd(plr, wf, wb, bf, bb, name):
    n = wf.shape[1]

    def fn(lr, wf, wb, bf, bb):
        lrb = lr.astype(BF16)
        outs = []
        for w, b in ((wf, bf), (wb, bb)):
            z = _dot(lrb, w.astype(BF16), NN) + b
            outs.append((jnp.minimum(z, 0.0) - jnp.log(1.0 + jnp.exp(-jnp.abs(z)))) / GLA_GATE_NORM)
        return outs

    return rowwise(fn, [plr], [wf, wb, bf, bb], [(n, F32), (n, F32)], [], 256, name)


def gate_bwd(plr, dgf, dgb, wf, wb, bf, bb, name):
    n = wf.shape[1]

    def fn(lr, dgf, dgb, wf, wb, bf, bb):
        lrb = lr.astype(BF16)
        dlr = jnp.zeros(lr.shape, F32)
        res = []
        for w, b, dg in ((wf, bf, dgf), (wb, bb, dgb)):
            wb16 = w.astype(BF16)
            z = _dot(lrb, wb16, NN) + b
            dz = dg * _sig(-z) / GLA_GATE_NORM
            dzb = dz.astype(BF16)
            dlr = dlr + _dot(dzb, wb16, NT)
            res += [_dot(lrb, dzb, TN), jnp.sum(dz, axis=0, keepdims=True)]
        return [dlr] + res

    return rowwise(fn, [plr, dgf, dgb], [wf, wb, bf, bb], [(128, BF16)],
                   [(128, n), (1, n), (128, n), (1, n)], 256, name)


def _tri_dot(tri_b, x):
    x1 = x.astype(BF16)
    r1 = x - x1.astype(F32)
    x2 = r1.astype(BF16)
    x3 = (r1 - x2.astype(F32)).astype(BF16)
    return _dot(tri_b, x1, NN) + _dot(tri_b, x2, NN) + _dot(tri_b, x3, NN)


def gla_fwd(pqk, pv, gl, s0, heads, reverse, name, o_add=None):
    T = pqk.shape[0]
    dk = pqk.shape[1] // (2 * heads)
    dv = pv.shape[1] // heads
    C = GLA_CHUNK
    nc = T // C
    qscale = dk ** -0.5

    def body(*refs):
        if o_add is None:
            q_ref, k_ref, v_ref, g_ref, s0_ref, o_ref, st_ref, sf_ref, S = refs
            oa_ref = None
        else:
            q_ref, k_ref, v_ref, g_ref, s0_ref, oa_ref, o_ref, st_ref, sf_ref, S = refs
        n = pl.program_id(0)

        @pl.when(n == 0)
        def _():
            S[...] = s0_ref[...]

        r = lax.broadcasted_iota(jnp.int32, (C, C), 0)
        c = lax.broadcasted_iota(jnp.int32, (C, C), 1)
        tri = (r <= c) if reverse else (r >= c)
        trib = tri.astype(BF16)
        ga, qa, ka, va = g_ref[...], q_ref[...], k_ref[...], v_ref[...]
        sts = [S[h] for h in range(heads)]
        H = range(heads)
        gs = [ga[:, h * dk:(h + 1) * dk] for h in H]
        bs = [_tri_dot(trib, g) for g in gs]
        bls = [jnp.sum(g, axis=0, keepdims=True) for g in gs]
        mid = lax.broadcasted_iota(jnp.int32, (C, 1), 0) == C // 2
        bms = [jnp.sum(jnp.where(mid, b, 0.0), axis=0, keepdims=True) for b in bs]
        vs = [va[:, h * dv:(h + 1) * dv].astype(BF16) for h in H]
        qs = [qa[:, h * dk:(h + 1) * dk].astype(F32) * qscale for h in H]
        qes = [(qs[h] * jnp.exp(bs[h])).astype(BF16) for h in H]
        qms = [(qs[h] * jnp.exp(bs[h] - bms[h])).astype(BF16) for h in H]
        kms = [(ka[:, h * dk:(h + 1) * dk].astype(F32) * jnp.exp(bms[h] - bs[h])).astype(BF16) for h in H]
        kls = [(ka[:, h * dk:(h + 1) * dk].astype(F32) * jnp.exp(bls[h] - bs[h])).astype(BF16) for h in H]
        inter = [_dot(qes[h], sts[h].astype(BF16), NT) for h in H]
        upd = [_dot(vs[h], kls[h], TN) for h in H]
        As = [jnp.where(tri, _dot(qms[h], kms[h], NT), 0.0).astype(BF16) for h in H]
        outs = [inter[h] + _dot(As[h], vs[h], NN) for h in H]
        news = [sts[h] * jnp.exp(bls[h]) + upd[h] for h in H]
        o = jnp.concatenate(outs, axis=1)
        if oa_ref is not None:
            o = o + oa_ref[...]
        o_ref[...] = o
        for h in range(heads):
            st_ref[0, h] = sts[h]
            S[h] = news[h]

        @pl.when(n == nc - 1)
        def _():
            for h in range(heads):
                sf_ref[h] = news[h]

    def ci(n):
        return (nc - 1 - n) if reverse else n

    specs = [
        pl.BlockSpec((C, heads * dk), lambda n: (ci(n), 0)),
        pl.BlockSpec((C, heads * dk), lambda n: (ci(n), 1)),
        pl.BlockSpec((C, heads * dv), lambda n: (ci(n), 0)),
        pl.BlockSpec((C, heads * dk), lambda n: (ci(n), 0)),
        pl.BlockSpec((heads, dv, dk), lambda n: (0, 0, 0)),
    ]
    ins = [pqk, pqk, pv, gl, s0]
    if o_add is not None:
        specs.append(pl.BlockSpec((C, heads * dv), lambda n: (ci(n), 0)))
        ins.append(o_add)
    return pl.pallas_call(
        body,
        grid=(nc,),
        in_specs=specs,
        out_specs=[
            pl.BlockSpec((C, heads * dv), lambda n: (ci(n), 0)),
            pl.BlockSpec((1, heads, dv, dk), lambda n: (ci(n), 0, 0, 0)),
            pl.BlockSpec((heads, dv, dk), lambda n: (0, 0, 0)),
        ],
        out_shape=[
            jax.ShapeDtypeStruct((T, heads * dv), F32),
            jax.ShapeDtypeStruct((nc, heads, dv, dk), F32),
            jax.ShapeDtypeStruct((heads, dv, dk), F32),
        ],
        scratch_shapes=[pltpu.VMEM((heads, dv, dk), F32)],
        compiler_params=_cp(("arbitrary",)),
        name=name,
    )(*ins)


def gla_bwd(pqk, pv, gl, states, do, dsf, heads, reverse, name, acc=None):
    T = pqk.shape[0]
    dk = pqk.shape[1] // (2 * heads)
    dv = pv.shape[1] // heads
    C = GLA_CHUNK
    nc = T // C
    qscale = dk ** -0.5

    def body(*refs):
        if acc is None:
            q_ref, k_ref, v_ref, g_ref, st_ref, do_ref, dsf_ref, dq_ref, dk_ref, dv_ref, dg_ref, ds0_ref, dS = refs
            aq = ak = av = None
        else:
            (q_ref, k_ref, v_ref, g_ref, st_ref, do_ref, dsf_ref, aq, ak, av,
             dq_ref, dk_ref, dv_ref, dg_ref, ds0_ref, dS) = refs
        n = pl.program_id(0)

        @pl.when(n == 0)
        def _():
            dS[...] = dsf_ref[...]

        r = lax.broadcasted_iota(jnp.int32, (C, C), 0)
        c = lax.broadcasted_iota(jnp.int32, (C, C), 1)
        tri = (r <= c) if reverse else (r >= c)
        tri_t = (r >= c) if reverse else (r <= c)
        trib, tritb = tri.astype(BF16), tri_t.astype(BF16)
        ga, qa, ka, va, doa = g_ref[...], q_ref[...], k_ref[...], v_ref[...], do_ref[...]
        sts = [st_ref[0, h] for h in range(heads)]
        dsts = [dS[h] for h in range(heads)]
        H = range(heads)
        gs = [ga[:, h * dk:(h + 1) * dk] for h in H]
        bs = [_tri_dot(trib, g) for g in gs]
        bls = [jnp.sum(g, axis=0, keepdims=True) for g in gs]
        mid = lax.broadcasted_iota(jnp.int32, (C, 1), 0) == C // 2
        bms = [jnp.sum(jnp.where(mid, b, 0.0), axis=0, keepdims=True) for b in bs]
        ebs = [jnp.exp(b) for b in bs]
        embs = [jnp.exp(bs[h] - bms[h]) for h in H]
        enbs = [jnp.exp(bms[h] - bs[h]) for h in H]
        elbs = [jnp.exp(bls[h] - bs[h]) for h in H]
        ebls = [jnp.exp(bl) for bl in bls]
        vbs = [va[:, h * dv:(h + 1) * dv].astype(BF16) for h in H]
        dobs = [doa[:, h * dv:(h + 1) * dv].astype(BF16) for h in H]
        qs = [qa[:, h * dk:(h + 1) * dk].astype(F32) * qscale for h in H]
        qes = [qs[h] * ebs[h] for h in H]
        qms = [qs[h] * embs[h] for h in H]
        kms = [ka[:, h * dk:(h + 1) * dk].astype(F32) * enbs[h] for h in H]
        kls = [ka[:, h * dk:(h + 1) * dk].astype(F32) * elbs[h] for h in H]
        qebs = [a.astype(BF16) for a in qes]
        qmbs = [a.astype(BF16) for a in qms]
        kmbs = [a.astype(BF16) for a in kms]
        klbs = [a.astype(BF16) for a in kls]
        stbs = [a.astype(BF16) for a in sts]
        dstbs = [a.astype(BF16) for a in dsts]
        ps = [jnp.where(tri, _dot(qmbs[h], kmbs[h], NT), 0.0).astype(BF16) for h in H]
        dps = [jnp.where(tri, _dot(dobs[h], vbs[h], NT), 0.0).astype(BF16) for h in H]
        dqes = [_dot(dobs[h], stbs[h], NN) for h in H]
        dkls = [_dot(vbs[h], dstbs[h], NN) for h in H]
        dv1 = [_dot(klbs[h], dstbs[h], NT) for h in H]
        dsn1 = [_dot(dobs[h], qebs[h], TN) for h in H]
        dqms = [_dot(dps[h], kmbs[h], NN) for h in H]
        dkms = [_dot(dps[h], qmbs[h], TN) for h in H]
        dvs = [_dot(ps[h], dobs[h], TN) + dv1[h] for h in H]
        dbls = [ebls[h] * jnp.sum(dsts[h] * sts[h], axis=0, keepdims=True)
                + jnp.sum(dkls[h] * kls[h], axis=0, keepdims=True) for h in H]
        dsns = [dsn1[h] + dsts[h] * ebls[h] for h in H]
        dqs = [(dqes[h] * ebs[h] + dqms[h] * embs[h]) * qscale for h in H]
        dks = [dkms[h] * enbs[h] + dkls[h] * elbs[h] for h in H]
        dbs = [dqes[h] * qes[h] + dqms[h] * qms[h] - dkms[h] * kms[h] - dkls[h] * kls[h] for h in H]
        dgs = [_tri_dot(tritb, dbs[h]) + dbls[h] for h in H]
        dq, dkk, dvv = (jnp.concatenate(a, axis=1) for a in (dqs, dks, dvs))
        if aq is not None:
            dq = dq + aq[...].astype(F32)
            dkk = dkk + ak[...].astype(F32)
            dvv = dvv + av[...].astype(F32)
        dq_ref[...] = dq.astype(dq_ref.dtype)
        dk_ref[...] = dkk.astype(dk_ref.dtype)
        dv_ref[...] = dvv.astype(dv_ref.dtype)
        dg_ref[...] = jnp.concatenate(dgs, axis=1)
        for h in range(heads):
            dS[h] = dsns[h]

        @pl.when(n == nc - 1)
        def _():
            for h in range(heads):
                ds0_ref[h] = dsns[h]

    def ci(n):
        return n if reverse else (nc - 1 - n)

    kspec = pl.BlockSpec((C, heads * dk), lambda n: (ci(n), 0))
    vspec = pl.BlockSpec((C, heads * dv), lambda n: (ci(n), 0))
    sspec = pl.BlockSpec((heads, dv, dk), lambda n: (0, 0, 0))
    specs = [
        kspec,
        pl.BlockSpec((C, heads * dk), lambda n: (ci(n), 1)),
        vspec,
        kspec,
        pl.BlockSpec((1, heads, dv, dk), lambda n: (ci(n), 0, 0, 0)),
        vspec,
        sspec,
    ]
    ins = [pqk, pqk, pv, gl, states, do, dsf]
    odt = F32 if acc is None else BF16
    if acc is not None:
        specs += [kspec, kspec, vspec]
        ins += list(acc)
    return pl.pallas_call(
        body,
        grid=(nc,),
        in_specs=specs,
        out_specs=[kspec, kspec, vspec, kspec, sspec],
        out_shape=[
            jax.ShapeDtypeStruct((T, heads * dk), odt),
            jax.ShapeDtypeStruct((T, heads * dk), odt),
            jax.ShapeDtypeStruct((T, heads * dv), odt),
            jax.ShapeDtypeStruct((T, heads * dk), F32),
            jax.ShapeDtypeStruct((heads, dv, dk), F32),
        ],
        scratch_shapes=[pltpu.VMEM((heads, dv, dk), F32)],
        compiler_params=_cp(("arbitrary",)),
        name=name,
    )(*ins)


def gla_out_fwd(og, prb, gn, heads, name):
    dv = og.shape[1] // heads

    def fn(og, rb, gn):
        outs = []
        for h in range(heads):
            oh = og[:, h * dv:(h + 1) * dv]
            outs.append(oh * _rstd(oh) * gn)
        y = jnp.concatenate(outs, axis=1)
        return y * (rb * _sig(rb))

    return rowwise(fn, [og, prb], [gn], [(og.shape[1], BF16)], [], 256, name)[0]


def gla_out_bwd(og, prb, du, gn, heads, name):
    dv = og.shape[1] // heads

    def fn(og, rb, du, gn):
        sg = _sig(rb)
        silu = rb * sg
        dsilu = sg * (1.0 + rb * (1.0 - sg))
        dog, ys = [], []
        dgn = jnp.zeros((1, dv), F32)
        for h in range(heads):
            sl = slice(h * dv, (h + 1) * dv)
            oh = og[:, sl]
            r = _rstd(oh)
            n = oh * r
            ys.append(n * gn)
            dy = du[:, sl] * silu[:, sl]
            dgn = dgn + jnp.sum(dy * n, axis=0, keepdims=True)
            dn = dy * gn
            dog.append(r * (dn - n * jnp.mean(dn * n, axis=-1, keepdims=True)))
        y = jnp.concatenate(ys, axis=1)
        return jnp.concatenate(dog, axis=1), du * y * dsilu, dgn

    return rowwise(fn, [og, prb, du], [gn], [(og.shape[1], F32), (og.shape[1], BF16)], [(1, dv)], 128, name)


def conv_specs(T, tt, tc, off, order):
    r8 = tt // 8
    last8 = T // 8 - 1
    if order == "ij":
        return [
            pl.BlockSpec((tt, tc), lambda i, j: (i, j + off)),
            pl.BlockSpec((8, tc), lambda i, j: (jnp.maximum(i * r8 - 1, 0), j + off)),
            pl.BlockSpec((8, tc), lambda i, j: (jnp.minimum((i + 1) * r8, last8), j + off)),
        ]
    return [
        pl.BlockSpec((tt, tc), lambda j, i: (i, j + off)),
        pl.BlockSpec((8, tc), lambda j, i: (jnp.maximum(i * r8 - 1, 0), j + off)),
        pl.BlockSpec((8, tc), lambda j, i: (jnp.minimum((i + 1) * r8, last8), j + off)),
    ]


def _shifted(u, hp, hn, i, nt_):
    tt = u.shape[0]
    row = lax.broadcasted_iota(jnp.int32, u.shape, 0)
    r8 = lax.broadcasted_iota(jnp.int32, hp.shape, 0)
    prev = jnp.sum(jnp.where(r8 == 7, hp, 0.0), axis=0, keepdims=True) * (i > 0).astype(F32)
    nxt = jnp.sum(jnp.where(r8 == 0, hn, 0.0), axis=0, keepdims=True) * (i < nt_ - 1).astype(F32)
    down = jnp.where(row == 0, prev, pltpu.roll(u, 1, 0))
    up = jnp.where(row == tt - 1, nxt, pltpu.roll(u, tt - 1, 0))
    return down, up


def conv_swiglu_fwd(u, cw, cb, name):
    T, F2 = u.shape
    F = F2 // 2
    tt = min(512, T)
    tc = _pick(F, (512, 256, 128))
    nt_, ncol = T // tt, F // tc
    H = CONV_HALO
    n = tt + 2 * H

    def body(ua, uap, uan, ug, ugp, ugn, wa, wg, ba, bg, f_ref):
        i = pl.program_id(1)
        keep_p = (i > 0).astype(F32)
        keep_n = (i < nt_ - 1).astype(F32)
        res = []
        for m, p, nx, w, b in ((ua, uap, uan, wa, ba), (ug, ugp, ugn, wg, bg)):
            x = jnp.concatenate([p[...] * keep_p, m[...], nx[...] * keep_n], axis=0)
            down, up = pltpu.roll(x, 1, 0)[H:H + tt], pltpu.roll(x, n - 1, 0)[H:H + tt]
            res.append(w[0] * down + w[1] * x[H:H + tt] + w[2] * up + b[...])
        a, g = res
        f_ref[...] = (a * _sig_tanh(a) * g).astype(f_ref.dtype)

    wspec = lambda off: pl.BlockSpec((3, 1, tc), lambda j, i: (0, 0, j + off))
    bspec = lambda off: pl.BlockSpec((1, tc), lambda j, i: (0, j + off))
    return pl.pallas_call(
        body,
        grid=(ncol, nt_),
        in_specs=conv_halo_specs(T, tt, tc, 0) + conv_halo_specs(T, tt, tc, ncol)
        + [wspec(0), wspec(ncol), bspec(0), bspec(ncol)],
        out_specs=pl.BlockSpec((tt, tc), lambda j, i: (i, j)),
        out_shape=jax.ShapeDtypeStruct((T, F), BF16),
        compiler_params=_cp(("parallel", "parallel")),
        name=name,
    )(u, u, u, u, u, u, cw, cw, cb, cb)


def conv_swiglu_bwd(u, cw, cb, df, name):
    T, F2 = u.shape
    F = F2 // 2
    tt = min(256, T)
    tc = _pick(F, (512, 256, 128))
    nt_, ncol = T // tt, F // tc

    def body(ua, uap, uan, ug, ugp, ugn, wa, wg, ba, bg, df_ref, da_ref, dg_ref, dwa, dwg, dba, dbg):
        i = pl.program_id(1)

        @pl.when(i == 0)
        def _():
            for r in (dwa, dwg, dba, dbg):
                r[...] = jnp.zeros_like(r)

        sh = []
        res = []
        for um, up_, un, w, b in ((ua, uap, uan, wa, ba), (ug, ugp, ugn, wg, bg)):
            x = um[...]
            down, up = _shifted(x, up_[...], un[...], i, nt_)
            sh.append((down, x, up))
            res.append(w[0] * down + w[1] * x + w[2] * up + b[...])
        a, g = res
        d = df_ref[...].astype(F32)
        sg = _sig(a)
        da = d * g * sg * (1.0 + a * (1.0 - sg))
        dg = d * a * sg
        da_ref[...] = da
        dg_ref[...] = dg
        for dd, (down, x, up), dw, db in ((da, sh[0], dwa, dba), (dg, sh[1], dwg, dbg)):
            dw[0] += jnp.sum(dd * down, axis=0, keepdims=True)
            dw[1] += jnp.sum(dd * x, axis=0, keepdims=True)
            dw[2] += jnp.sum(dd * up, axis=0, keepdims=True)
            db[...] += jnp.sum(dd, axis=0, keepdims=True)

    wspec = lambda off: pl.BlockSpec((3, 1, tc), lambda j, i: (0, 0, j + off))
    bspec = lambda off: pl.BlockSpec((1, tc), lambda j, i: (0, j + off))
    tile = pl.BlockSpec((tt, tc), lambda j, i: (i, j))
    return pl.pallas_call(
        body,
        grid=(ncol, nt_),
        in_specs=conv_specs(T, tt, tc, 0, "ji") + conv_specs(T, tt, tc, ncol, "ji")
        + [wspec(0), wspec(ncol), bspec(0), bspec(ncol), tile],
        out_specs=[tile, tile, wspec(0), wspec(0), bspec(0), bspec(0)],
        out_shape=[
            jax.ShapeDtypeStruct((T, F), F32), jax.ShapeDtypeStruct((T, F), F32),
            jax.ShapeDtypeStruct((3, 1, F), F32), jax.ShapeDtypeStruct((3, 1, F), F32),
            jax.ShapeDtypeStruct((1, F), F32), jax.ShapeDtypeStruct((1, F), F32),
        ],
        compiler_params=_cp(("parallel", "arbitrary")),
        name=name,
    )(u, u, u, u, u, u, cw, cw, cb, cb, df)


CONV_HALO = 16


def conv_halo_specs(T, tt, tc, off):
    r = tt // CONV_HALO
    last = T // CONV_HALO - 1
    return [
        pl.BlockSpec((tt, tc), lambda j, i: (i, j + off)),
        pl.BlockSpec((CONV_HALO, tc), lambda j, i: (jnp.maximum(i * r - 1, 0), j + off)),
        pl.BlockSpec((CONV_HALO, tc), lambda j, i: (jnp.minimum((i + 1) * r, last), j + off)),
    ]


def conv_swiglu_bwd_fused(u, cw, cb, df, name):
    T, F2 = u.shape
    F = F2 // 2
    tt = min(512, T)
    tc = _pick(F, (512, 256, 128))
    nt_, ncol = T // tt, F // tc
    H = CONV_HALO
    n = tt + 2 * H

    def body(ua, uap, uan, ug, ugp, ugn, dm, dp_, dn, wa, wg, ba, bg, dua_ref, dug_ref, dwa, dwg, dba, dbg):
        i = pl.program_id(1)

        @pl.when(i == 0)
        def _():
            for r in (dwa, dwg, dba, dbg):
                r[...] = jnp.zeros_like(r)

        keep_p = (i > 0).astype(F32)
        keep_n = (i < nt_ - 1).astype(F32)

        def ext(m, p, nx):
            return jnp.concatenate([p[...].astype(F32) * keep_p, m[...].astype(F32), nx[...].astype(F32) * keep_n],
                                   axis=0)

        d = ext(dm, dp_, dn)
        conv, parts = [], []
        for m, p, nx, w, b in ((ua, uap, uan, wa, ba), (ug, ugp, ugn, wg, bg)):
            x = ext(m, p, nx)
            down, up = pltpu.roll(x, 1, 0), pltpu.roll(x, n - 1, 0)
            parts.append((down, x, up))
            conv.append(w[0] * down + w[1] * x + w[2] * up + b[...])
        a, g = conv
        sg = _sig_tanh(a)
        da = d * g * sg * (1.0 + a * (1.0 - sg))
        dg = d * a * sg
        for dd, w, (down, x, up), o_ref, dw, db in ((da, wa, parts[0], dua_ref, dwa, dba),
                                                    (dg, wg, parts[1], dug_ref, dwg, dbg)):
            du = w[0] * pltpu.roll(dd, n - 1, 0) + w[1] * dd + w[2] * pltpu.roll(dd, 1, 0)
            o_ref[...] = du[H:H + tt].astype(o_ref.dtype)
            ddm = dd[H:H + tt]
            dw[0] += jnp.sum(ddm * down[H:H + tt], axis=0, keepdims=True)
            dw[1] += jnp.sum(ddm * x[H:H + tt], axis=0, keepdims=True)
            dw[2] += jnp.sum(ddm * up[H:H + tt], axis=0, keepdims=True)
            db[...] += jnp.sum(ddm, axis=0, keepdims=True)

    wspec = lambda off: pl.BlockSpec((3, 1, tc), lambda j, i: (0, 0, j + off))
    bspec = lambda off: pl.BlockSpec((1, tc), lambda j, i: (0, j + off))
    tile = pl.BlockSpec((tt, tc), lambda j, i: (i, j))
    return pl.pallas_call(
        body,
        grid=(ncol, nt_),
        in_specs=conv_halo_specs(T, tt, tc, 0) + conv_halo_specs(T, tt, tc, ncol) + conv_halo_specs(T, tt, tc, 0)
        + [wspec(0), wspec(ncol), bspec(0), bspec(ncol)],
        out_specs=[tile, tile, wspec(0), wspec(0), bspec(0), bspec(0)],
        out_shape=[
            jax.ShapeDtypeStruct((T, F), BF16), jax.ShapeDtypeStruct((T, F), BF16),
            jax.ShapeDtypeStruct((3, 1, F), F32), jax.ShapeDtypeStruct((3, 1, F), F32),
            jax.ShapeDtypeStruct((1, F), F32), jax.ShapeDtypeStruct((1, F), F32),
        ],
        compiler_params=_cp(("parallel", "arbitrary")),
        name=name,
    )(u, u, u, u, u, u, df, df, df, cw, cw, cb, cb)


def conv_transpose(d, cw, off, name):
    T, F = d.shape
    tt = min(256, T)
    tc = _pick(F, (512, 256, 128))
    nt_, ncol = T // tt, F // tc
    offb = off // tc

    def body(dm, dp_, dn, w, o_ref):
        i = pl.program_id(0)
        x = dm[...]
        down, up = _shifted(x, dp_[...], dn[...], i, nt_)
        o_ref[...] = (w[0] * up + w[1] * x + w[2] * down).astype(o_ref.dtype)

    return pl.pallas_call(
        body,
        grid=(nt_, ncol),
        in_specs=conv_specs(T, tt, tc, 0, "ij") + [pl.BlockSpec((3, 1, tc), lambda i, j: (0, 0, j + offb))],
        out_specs=pl.BlockSpec((tt, tc), lambda i, j: (i, j)),
        out_shape=jax.ShapeDtypeStruct((T, F), BF16),
        compiler_params=_cp(("parallel", "parallel")),
        name=name,
    )(d, d, d, cw)


def rope_tables(n, hd):
    rows = n // GRID_W
    row = jnp.repeat(jnp.arange(rows), GRID_W)
    col = jnp.tile(jnp.arange(GRID_W), rows)
    n_freq = hd // 4
    inv = ROPE_THETA ** (-jnp.arange(n_freq, dtype=F32) / n_freq)
    ang = jnp.concatenate([row[:, None] * inv, col[:, None] * inv], axis=-1)
    cos, sin = jnp.cos(ang), jnp.sin(ang)
    return jnp.concatenate([cos, cos], axis=-1), jnp.concatenate([-sin, sin], axis=-1)


def local_step(x, ctx, tgt, mod, modc, W, P):
    T, D = x.shape
    L = ctx.shape[0]
    hd, hq, hkv, gh = P["hd"], P["hq"], P["hkv"], P["gh"]
    sh1, sc1, g1, sh2, sc2, g2 = mod
    csh1, csc1 = modc
    kvw = hkv * hd
    gkw = W["gqk"].shape[1] // 2
    gdv = D // gh
    gdk = gkw // gh

    h = modulate_fwd(x, P["g_mix"], sh1, sc1, "mod1")
    hc = modulate_fwd(ctx, P["g_mix"], csh1, csc1, "mod1_ctx")
    pq = matmul(h, W["q"], "nn", F32, "proj_q")
    pkv = matmul(h, W["kv"], "nn", F32, "proj_kv")
    pgqk = matmul(h, W["gqk"], "nn", F32, "proj_gqk")
    pgv = matmul(h, W["gv"], "nn", F32, "proj_gv")
    prb = matmul(h, W["rb"], "nn", F32, "proj_rb")
    plr = matmul(h, W["lr"], "nn", F32, "proj_lr")
    pgab = matmul(h, W["gab"], "nn", F32, "proj_gab")
    pkv_c = matmul(hc, W["kv"], "nn", F32, "proj_kv_ctx")
    pgqk_c = matmul(hc, W["gqk"], "nn", F32, "proj_gqk_ctx")
    pgv_c = matmul(hc, W["gv"], "nn", F32, "proj_gv_ctx")
    plr_c = matmul(hc, W["lr"], "nn", F32, "proj_lr_ctx")

    cosf, sinf = rope_tables(T, hd)
    one_c, zero_c = jnp.ones((L, hd), F32), jnp.zeros((L, hd), F32)
    qr = norm_rope_fwd(pq, hq * hd, 0, P["q_norm"], cosf, sinf, hd, "qnorm")
    kr = norm_rope_fwd(pkv, kvw, 0, P["k_norm"], cosf, sinf, hd, "knorm")
    kcr = norm_rope_fwd(pkv_c, kvw, 0, P["k_norm"], one_c, zero_c, hd, "knorm_ctx")
    sink = P["attn_sink"].reshape(hq, 1, 1)
    o_attn, lse, lse_row = attention_fwd(qr, kr, pkv, kcr, pkv_c, sink, hkv, hd, "attn_fwd")

    gf, gb = gate_fwd(plr, W["gate_f"], W["gate_b"], P["b_gate_f"], P["b_gate_b"], "gates")
    gfc, gbc = gate_fwd(plr_c, W["gate_f"], W["gate_b"], P["b_gate_f"], P["b_gate_b"], "gates_ctx")
    zero_state = jnp.zeros((gh, gdv, gdk), F32)
    _, st_cf, s_cf = gla_fwd(pgqk_c, pgv_c, gfc, zero_state, gh, False, "gla_ctx_f")
    _, st_cb, s_cb = gla_fwd(pgqk_c, pgv_c, gbc, zero_state, gh, True, "gla_ctx_b")
    of, st_f, _ = gla_fwd(pgqk, pgv, gf, s_cf, gh, False, "gla_f")
    og, st_b, _ = gla_fwd(pgqk, pgv, gb, s_cb, gh, True, "gla_b", o_add=of)
    ug = gla_out_fwd(og, prb, P["gla_norm"], gh, "gla_out")

    ya = matmul(o_attn, W["attn_o"], "nn", F32, "attn_o")
    yg = matmul(ug, W["gla_o"], "nn", F32, "gla_o")

    def merge_fn(ya, yg, ga, gb_):
        return _sig(ga) * ya + _sig(gb_) * yg

    z = rowwise(merge_fn, [ya, yg, (pgab, D, 0), (pgab, D, 1)], [], [(D, BF16)], [], 256, "merge")[0]
    mo = matmul(z, W["out"], "nn", F32, "w_out")

    def res_fn(x, mo, g1, gffn, sh2, sc2):
        x1 = x + g1 * mo
        return x1, x1 * _rstd(x1) * gffn * (1.0 + sc2) + sh2

    x1, h2 = rowwise(res_fn, [x, mo], [g1, P["g_ffn"], sh2, sc2], [(D, F32), (D, BF16)], [], 256, "res_mod2")
    u = matmul(h2, W["up"], "nn", F32, "w_up")
    cw3 = W["conv_w"].reshape(3, 1, -1)
    f = conv_swiglu_fwd(u, cw3, P["conv_b"], "conv_swiglu")
    fo = matmul(f, W["down"], "nn", F32, "w_down")

    def final_fn(x1, fo, tgt, g2):
        e = x1 + g2 * fo - tgt
        dy = e * (1.0 / D)
        lsum = jnp.sum(jnp.sum(e * e, axis=1, keepdims=True), axis=0, keepdims=True)
        return dy, dy * g2, jnp.broadcast_to(lsum, (1, 128)), jnp.sum(dy * fo, axis=0, keepdims=True)

    dy, dfo, lsum, dg2 = rowwise(final_fn, [x1, fo, tgt], [g2], [(D, F32), (D, BF16)], [(1, 128), (1, D)], 256, "loss")
    df = matmul(dfo, W["down"], "nt", BF16, "d_f")
    dw_down = matmul(f, dfo, "tn", BF16, "dw_down")
    du_a, du_g, dcw_a, dcw_g, dcb_a, dcb_g = conv_swiglu_bwd_fused(u, cw3, P["conv_b"], df, "conv_swiglu_bwd")
    Fh = du_a.shape[1]
    dh2 = matmul(du_a, W["up"][:, :Fh], "nt", F32, "d_h2_a")
    dh2 = matmul(du_g, W["up"][:, Fh:], "nt", F32, "d_h2_g", add=dh2)
    dw_up = [matmul(h2, du_a, "tn", BF16, "dw_up_a"), matmul(h2, du_g, "tn", BF16, "dw_up_g")]

    def mod2_bwd_fn(x1, dh, dy, mo, gffn, sc2, g1):
        r = _rstd(x1)
        n = x1 * r
        dyy = dh * (1.0 + sc2)
        dn = dyy * gffn
        dx1 = dy + r * (dn - n * jnp.mean(dn * n, axis=-1, keepdims=True))
        s0 = lambda a: jnp.sum(a, axis=0, keepdims=True)
        return dx1, dx1 * g1, s0(dyy * n), s0(dh), s0(dh * n * gffn), s0(dx1 * mo)

    dx1, dmo, dg_ffn, dsh2, dsc2, dg1 = rowwise(
        mod2_bwd_fn, [x1, dh2, dy, mo], [P["g_ffn"], sc2, g1], [(D, F32), (D, BF16)], [(1, D)] * 4, 128, "mod2_bwd")
    dz = matmul(dmo, W["out"], "nt", F32, "d_z")
    dw_out = matmul(z, dmo, "tn", BF16, "dw_out")

    def merge_bwd_fn(dz, ya, yg, ga, gb_):
        sa, sb = _sig(ga), _sig(gb_)
        return dz * sa, dz * sb, jnp.concatenate([dz * ya * sa * (1.0 - sa), dz * yg * sb * (1.0 - sb)], axis=1)

    dya, dyg, dpgab = rowwise(merge_bwd_fn, [dz, ya, yg, (pgab, D, 0), (pgab, D, 1)], [],
                              [(D, BF16), (D, BF16), (2 * D, BF16)], [], 128, "merge_bwd")
    do_attn = matmul(dya, W["attn_o"], "nt", BF16, "d_oattn")
    dw_attn_o = matmul(o_attn, dya, "tn", BF16, "dw_attn_o")
    dug = matmul(dyg, W["gla_o"], "nt", F32, "d_ug")
    dw_gla_o = matmul(ug, dyg, "tn", BF16, "dw_gla_o")
    dog, dprb, dgn = gla_out_bwd(og, prb, dug, P["gla_norm"], gh, "gla_out_bwd")

    dq1, dk1, dv1, dgf, ds_cf = gla_bwd(pgqk, pgv, gf, st_f, dog, zero_state, gh, False, "gla_f_bwd")
    dgq, dgk, dpgv, dgb, ds_cb = gla_bwd(pgqk, pgv, gb, st_b, dog, zero_state, gh, True, "gla_b_bwd",
                                          acc=(dq1, dk1, dv1))
    dpgqk = jnp.concatenate([dgq, dgk], axis=1)
    zero_do = jnp.zeros((L, gh * gdv), F32)
    cq1, ck1, cv1, dgfc, _ = gla_bwd(pgqk_c, pgv_c, gfc, st_cf, zero_do, ds_cf, gh, False, "gla_ctx_f_bwd")
    cq, ck, dpgv_c, dgbc, _ = gla_bwd(pgqk_c, pgv_c, gbc, st_cb, zero_do, ds_cb, gh, True, "gla_ctx_b_bwd",
                                      acc=(cq1, ck1, cv1))
    dpgqk_c = jnp.concatenate([cq, ck], axis=1)
    dplr, dwgf, dbgf, dwgb, dbgb = gate_bwd(plr, dgf, dgb, W["gate_f"], W["gate_b"], P["b_gate_f"], P["b_gate_b"], "gates_bwd")
    dplr_c, dwgf_c, dbgf_c, dwgb_c, dbgb_c = gate_bwd(plr_c, dgfc, dgbc, W["gate_f"], W["gate_b"], P["b_gate_f"],
                                                      P["b_gate_b"], "gates_ctx_bwd")

    dqr, dkc_r, dvc, dsink, dr_row = attention_bwd_q(qr, kr, pkv, kcr, pkv_c, sink, do_attn, o_attn, lse, hkv, hd,
                                                     "attn_bwd_q")
    dkr, dv = attention_bwd_kv(qr, kr, pkv, do_attn, lse_row, dr_row, hkv, hd, "attn_bwd_kv")
    dpq, dqn = norm_rope_bwd(pq, hq * hd, 0, dqr, P["q_norm"], cosf, sinf, hd, "qnorm_bwd")
    dpk, dkn = norm_rope_bwd(pkv, kvw, 0, dkr, P["k_norm"], cosf, sinf, hd, "knorm_bwd")
    dpk_c, dkn_c = norm_rope_bwd(pkv_c, kvw, 0, dkc_r, P["k_norm"], one_c, zero_c, hd, "knorm_ctx_bwd")
    dpkv = jnp.concatenate([dpk, dv], axis=1)
    dpkv_c = jnp.concatenate([dpk_c, dvc.astype(BF16)], axis=1)

    dw_q = matmul(h, dpq, "tn", BF16, "dw_q")
    dw_kv = matmul(h, dpkv, "tn", BF16, "dw_kv", add=matmul(hc, dpkv_c, "tn", F32, "dw_kv_ctx"))
    dw_gqk = matmul(h, dpgqk, "tn", BF16, "dw_gqk", add=matmul(hc, dpgqk_c, "tn", F32, "dw_gqk_ctx"))
    dw_gv = matmul(h, dpgv, "tn", BF16, "dw_gv", add=matmul(hc, dpgv_c, "tn", F32, "dw_gv_ctx"))
    dw_rb = matmul(h, dprb, "tn", BF16, "dw_rb")
    dw_lr = matmul(h, dplr, "tn", BF16, "dw_lr", add=matmul(hc, dplr_c, "tn", F32, "dw_lr_ctx"))
    dw_gab = matmul(h, dpgab, "tn", BF16, "dw_gab")
    lrw = P["lowrank"]
    dw_in = [dw_q, dw_kv, dw_gqk, dw_gv, dw_rb, dw_lr[:, :2 * lrw], dw_gab]

    dh = matmul(dpq, W["q"], "nt", F32, "dh_q")
    dh = matmul(dpkv, W["kv"], "nt", F32, "dh_kv", add=dh)
    dh = matmul(dpgqk, W["gqk"], "nt", F32, "dh_gqk", add=dh)
    dh = matmul(dpgv, W["gv"], "nt", F32, "dh_gv", add=dh)
    dh = matmul(dprb, W["rb"], "nt", F32, "dh_rb", add=dh)
    dh = matmul(dplr, W["lr"], "nt", F32, "dh_lr", add=dh)
    dh = matmul(dpgab, W["gab"], "nt", F32, "dh_gab", add=dh)
    dhc = matmul(dpkv_c, W["kv"], "nt", F32, "dhc_kv")
    dhc = matmul(dpgqk_c, W["gqk"], "nt", F32, "dhc_gqk", add=dhc)
    dhc = matmul(dpgv_c, W["gv"], "nt", F32, "dhc_gv", add=dhc)
    dhc = matmul(dplr_c, W["lr"], "nt", F32, "dhc_lr", add=dhc)

    def mod1_bwd_fn(x, dh, dres, g, sc):
        r = _rstd(x)
        n = x * r
        dyy = dh * (1.0 + sc)
        dn = dyy * g
        dx = dres + r * (dn - n * jnp.mean(dn * n, axis=-1, keepdims=True))
        s0 = lambda a: jnp.sum(a, axis=0, keepdims=True)
        return dx, s0(dyy * n), s0(dh), s0(dh * n * g)

    grad_x, dgmix, dsh1, dsc1 = rowwise(mod1_bwd_fn, [x, dh, dx1], [P["g_mix"], sc1], [(D, F32)], [(1, D)] * 3,
                                        128, "mod1_bwd")
    _, dgmix_c, dcsh1, dcsc1 = rowwise(mod1_bwd_fn, [ctx, dhc, jnp.zeros_like(ctx)], [P["g_mix"], csc1], [(D, F32)],
                                       [(1, D)] * 3, 128, "mod1_ctx_bwd")

    zD = jnp.zeros((1, D), F32)
    grads = dict(
        w_in=dw_in, w_attn_o=dw_attn_o, w_gla_o=dw_gla_o, w_out=dw_out, w_up=dw_up, w_down=dw_down,
        dmod_x=jnp.concatenate([dsh1, dsc1, dg1, dsh2, dsc2, dg2], axis=1),
        dmod_c=jnp.concatenate([dcsh1, dcsc1, zD, zD, zD, zD], axis=1),
        g_mix=dgmix + dgmix_c, q_norm=dqn, k_norm=dkn + dkn_c, attn_sink=dsink.reshape(1, hq),
        w_gate_f=(dwgf + dwgf_c)[:lrw], b_gate_f=dbgf + dbgf_c,
        w_gate_b=(dwgb + dwgb_c)[lrw:2 * lrw], b_gate_b=dbgb + dbgb_c,
        gla_norm=dgn, g_ffn=dg_ffn,
        conv_w=jnp.concatenate([dcw_a, dcw_g], axis=2).reshape(3, -1),
        conv_b=jnp.concatenate([dcb_a, dcb_g], axis=1),
    )
    return lsum[0, 0], grad_x, grads


SMALL_REPL = ("c_ctx", "b_mod", "g_mix", "q_norm", "k_norm", "attn_sink", "b_gate_f", "b_gate_b", "gla_norm", "g_ffn",
              "conv_b")
SMALL_SHARD = ("w_gate_f", "w_gate_b", "conv_w")
ORDER = ("c_ctx", "w_mod", "b_mod", "g_mix", "w_in", "q_norm", "k_norm", "attn_sink", "w_gate_f", "b_gate_f",
         "w_gate_b", "b_gate_b", "gla_norm", "w_attn_o", "w_gla_o", "w_out", "g_ffn", "w_up", "conv_w", "conv_b",
         "w_down")


def kernel(x, c, ctx, c_ctx, w_mod, b_mod, g_mix, w_in, q_norm, k_norm, attn_sink, w_gate_f, b_gate_f, w_gate_b, b_gate_b, gla_norm, w_attn_o, w_gla_o, w_out, g_ffn, w_up, conv_w, conv_b, w_down, loss_target, m_c_ctx, m_w_mod, m_b_mod, m_g_mix, m_w_in, m_q_norm, m_k_norm, m_attn_sink, m_w_gate_f, m_b_gate_f, m_w_gate_b, m_b_gate_b, m_gla_norm, m_w_attn_o, m_w_gla_o, m_w_out, m_g_ffn, m_w_up, m_conv_w, m_conv_b, m_w_down, v_c_ctx, v_w_mod, v_b_mod, v_g_mix, v_w_in, v_q_norm, v_k_norm, v_attn_sink, v_w_gate_f, v_b_gate_f, v_w_gate_b, v_b_gate_b, v_gla_norm, v_w_attn_o, v_w_gla_o, v_w_out, v_g_ffn, v_w_up, v_conv_w, v_conv_b, v_w_down):
    loc = dict(locals())
    Wt = {n: loc[n] for n in ORDER}
    Mt = {n: loc["m_" + n] for n in ORDER}
    Vt = {n: loc["v_" + n] for n in ORDER}
    me = 4 * lax.axis_index("x") + 2 * lax.axis_index("y") + lax.axis_index("c")

    D = x.shape[-1]
    hd = q_norm.shape[-1]
    hq = attn_sink.shape[-1]
    gdv = gla_norm.shape[-1]
    gh = D // gdv
    gdk = D // 2 // gh
    lrw = w_gate_f.shape[1]
    in_w = NDEV * w_in.shape[-1]
    kvw = (in_w - hq * hd - 2 * gh * gdk - 2 * gh * gdv - 2 * lrw - 2 * D) // 2
    hkv = kvw // hd
    gcols = w_gate_f.shape[-1]
    F2 = NDEV * w_up.shape[-1]
    mcols = w_mod.shape[-1]

    x2, ctx2, tgt2 = x[0], ctx[0], loss_target[0]

    c_all = exchange([jnp.pad(c, ((0, 7), (0, 0)))], True, "gather_c")[0][:, 0, :]
    c9 = jnp.concatenate([c_all, c_ctx[None, :], jnp.zeros((7, D), F32)], axis=0)
    s9 = rowwise(lambda a: a * _sig(a), [c9], [], [(D, F32)], [], 16, "silu_c")[0]
    bias = jnp.broadcast_to(lax.dynamic_slice_in_dim(b_mod, me * mcols, mcols, axis=1), (16, mcols))
    mod_cols = matmul(s9, w_mod[0], "nn", F32, "mod_cols", add=bias)
    mod_all = exchange([mod_cols], True, "gather_mod")[0]
    mod_all = jnp.transpose(mod_all, (1, 0, 2)).reshape(16, NDEV * mcols)
    mod_me = lax.dynamic_slice_in_dim(mod_all, me, 1, axis=0)
    mod = [mod_me[:, i * D:(i + 1) * D] for i in range(6)]
    modc = [mod_all[8:9, i * D:(i + 1) * D] for i in range(2)]

    o3 = jnp.stack([w_attn_o[0], w_gla_o[0], w_out[0]]).astype(BF16)
    small_w = pack([w_gate_f[0], w_gate_b[0], conv_w[0]])
    g_in, g_o3, g_up, g_down, g_small = gather_two_level(
        [w_in[0].astype(BF16), o3, w_up[0].astype(BF16), w_down[0].astype(BF16), small_w], "gather_w")
    seg = segments_from_blocks(g_in, [hq * hd, 2 * kvw, 2 * gh * gdk, gh * gdv, gh * gdv, 2 * lrw, 2 * D])
    small_parts = [unpack(g_small[j], [w_gate_f[0].shape, w_gate_b[0].shape, conv_w[0].shape]) for j in range(NDEV)]
    wgf = jnp.concatenate([p[0] for p in small_parts], axis=1)
    wgb = jnp.concatenate([p[1] for p in small_parts], axis=1)
    cw_full = jnp.concatenate([p[2] for p in small_parts], axis=1)
    o3f = [g_o3[:, i].reshape(-1, D) for i in range(3)]
    W = dict(
        q=seg[0], kv=seg[1], gqk=seg[2], gv=seg[3], rb=seg[4],
        lr=jnp.pad(seg[5], ((0, 0), (0, 128 - 2 * lrw))), gab=seg[6],
        gate_f=jnp.pad(wgf, ((0, 128 - lrw), (0, 0))),
        gate_b=jnp.pad(wgb, ((lrw, 128 - 2 * lrw), (0, 0))),
        attn_o=o3f[0], gla_o=o3f[1], out=o3f[2],
        up=jnp.concatenate([g_up[j] for j in range(NDEV)], axis=1),
        down=g_down.reshape(-1, D),
        conv_w=cw_full,
    )
    P = dict(hd=hd, hq=hq, hkv=hkv, gh=gh, lowrank=lrw, g_mix=g_mix, q_norm=q_norm, k_norm=k_norm, attn_sink=attn_sink,
             b_gate_f=b_gate_f, b_gate_b=b_gate_b, gla_norm=gla_norm, g_ffn=g_ffn, conv_b=conv_b)

    lsum, grad_x, G = local_step(x2, ctx2, tgt2, mod, modc, W, P)
    loss = lax.psum(0.5 * lsum / D, ("x", "y", "c"))

    dm = exchange([jnp.concatenate([G["dmod_x"], G["dmod_c"], jnp.zeros((6, 6 * D), F32)], axis=0)], True,
                  "gather_dmod")[0]
    dmc = reduce_parts(dm[:, 1:2, :].reshape(NDEV, 6 * D // 128, 128), "sum_dmod_ctx").reshape(1, 6 * D)
    dM = jnp.concatenate([dm[:, 0, :], dmc, jnp.zeros((7, 6 * D), F32)], axis=0)
    dM_cols = lax.dynamic_slice_in_dim(dM, me * mcols, mcols, axis=1)
    g_w_mod = matmul(s9, dM_cols, "tn", F32, "dw_mod")
    g_b_mod = reduce_parts(dM.reshape(16, 6 * D // 128, 128), "sum_db_mod").reshape(1, 6 * D)
    dsc = matmul(dM_cols[8:16], w_mod[0], "nt", F32, "d_silu_ctx")
    cc = jnp.broadcast_to(c_ctx[None, :], (8, D))

    def dsilu_fn(d, a):
        sg = _sig(a)
        return d * sg * (1.0 + a * (1.0 - sg))

    g_cctx_part = rowwise(dsilu_fn, [dsc, cc], [], [(D, F32)], [], 8, "d_c_ctx")[0][0:1]

    small_names = ("c_ctx", "g_mix", "q_norm", "k_norm", "attn_sink", "b_gate_f", "b_gate_b", "gla_norm", "g_ffn",
                   "conv_b", "w_gate_f", "w_gate_b", "conv_w")
    G["c_ctx"] = g_cctx_part
    sm_shapes = [G[n].shape for n in small_names]
    sm_all = exchange([pack([G[n] for n in small_names])], True, "gather_small_grads")[0]
    sm_tot = unpack(reduce_parts(sm_all, "sum_small_grads"), sm_shapes)
    gs = dict(zip(small_names, sm_tot))
    gs["b_mod"] = g_b_mod
    gs["w_gate_f"] = lax.dynamic_slice_in_dim(gs["w_gate_f"], me * gcols, gcols, axis=1)
    gs["w_gate_b"] = lax.dynamic_slice_in_dim(gs["w_gate_b"], me * gcols, gcols, axis=1)
    ccols = conv_w.shape[-1]
    gs["conv_w"] = lax.dynamic_slice_in_dim(gs["conv_w"], me * ccols, ccols, axis=1)

    orows = w_attn_o.shape[1]
    s_in = blocks_from_segments(G["w_in"], w_in.shape[-1])
    s_o3 = jnp.concatenate([rows_to_blocks(G["w_attn_o"]), rows_to_blocks(G["w_gla_o"]), rows_to_blocks(G["w_out"])],
                           axis=1)
    s_up = blocks_from_segments(G["w_up"], w_up.shape[-1])
    s_down = rows_to_blocks(G["w_down"])
    r_in, r_o3, r_up, r_down = scatter_reduce([s_in, s_o3, s_up, s_down], "scatter_grads")

    out = {}
    out["w_in"] = adam_reduce(r_in, w_in[0], m_w_in[0], v_w_in[0], "adam_w_in")
    o3w = jnp.concatenate([w_attn_o[0], w_gla_o[0], w_out[0]], axis=0)
    o3m = jnp.concatenate([m_w_attn_o[0], m_w_gla_o[0], m_w_out[0]], axis=0)
    o3v = jnp.concatenate([v_w_attn_o[0], v_w_gla_o[0], v_w_out[0]], axis=0)
    ro3 = adam_reduce(r_o3, o3w, o3m, o3v, "adam_o3")
    for i, n in enumerate(("w_attn_o", "w_gla_o", "w_out")):
        out[n] = [a[i * orows:(i + 1) * orows] for a in ro3]
    out["w_up"] = adam_reduce(r_up, w_up[0], m_w_up[0], v_w_up[0], "adam_w_up")
    out["w_down"] = adam_reduce(r_down, w_down[0], m_w_down[0], v_w_down[0], "adam_w_down")
    out["w_mod"] = adam_reduce(g_w_mod[None], w_mod[0], m_w_mod[0], v_w_mod[0], "adam_w_mod")
    sm_names = SMALL_REPL + SMALL_SHARD
    shapes = [Wt[n].shape for n in sm_names]
    rs = adam_reduce(pack([gs[n] for n in sm_names])[None], pack([Wt[n] for n in sm_names]),
                     pack([Mt[n] for n in sm_names]), pack([Vt[n] for n in sm_names]), "adam_small")
    rs = [unpack(a, shapes) for a in rs]
    for i, n in enumerate(sm_names):
        out[n] = [a[i] for a in rs]

    res = [loss, grad_x[None]]
    for k in range(4):
        for n in ORDER:
            res.append(out[n][k].reshape(Wt[n].shape))
    return tuple(res)
```

```python
import jax
import jax.numpy as jnp
import numpy as np
from jax import lax
from jax.experimental import pallas as pl
from jax.experimental.pallas import tpu as pltpu

F32 = jnp.float32
BF16 = jnp.bfloat16

NDEV = 8
NCHIP = 4
EPS = 1e-6
WINDOW = 128
BLOCK = 128
GRID_W = 64
ROPE_THETA = 10000.0
GLA_CHUNK = 128
GLA_GATE_NORM = 16.0
ADAM_LR = 0.001
ADAM_B1 = 0.9
ADAM_B2 = 0.999
ADAM_EPS = 1e-08
ADAM_WD = 0.01
ADAM_STEP = 10
V7X_VMEM_LIMIT = 56 * 1024 * 1024
MATMUL_VMEM_BUDGET = 40 * 1024 * 1024
NEG = -1e30

NN = ((1,), (0,))
NT = ((1,), (1,))
TN = ((0,), (0,))


def _dot(a, b, dims):
    return lax.dot_general(a, b, (dims, ((), ())), preferred_element_type=F32)


def _cp(sem):
    return pltpu.CompilerParams(dimension_semantics=sem, vmem_limit_bytes=V7X_VMEM_LIMIT)


def _pick(n, cands):
    for c in cands:
        if n % c == 0:
            return c
    return n


def _sig(x):
    return 1.0 / (1.0 + jnp.exp(-x))


def _sig_tanh(x):
    return 0.5 * jnp.tanh(0.5 * x) + 0.5


def _rstd(x):
    return lax.rsqrt(jnp.mean(x * x, axis=-1, keepdims=True) + EPS)


_ANY = pl.BlockSpec(memory_space=pl.ANY)


def _place():
    return lax.axis_index("x"), lax.axis_index("y"), lax.axis_index("c")


def exchange(srcs, bcast, name, group="all"):
    n = len(srcs)
    ndev = NDEV if group == "all" else NCHIP
    ks = tuple(range(1, NDEV)) if group == "all" else (2, 4, 6)
    out_shape = [jax.ShapeDtypeStruct((ndev,) + (s.shape if bcast else s.shape[1:]), s.dtype) for s in srcs]

    def body(*refs):
        src, dst = refs[:n], refs[n:2 * n]
        send_sems, recv_sems, loc_sems = refs[2 * n:]
        x, y, c = _place()

        def idx(px, py, pc):
            return 4 * px + 2 * py + pc if group == "all" else 2 * px + py

        me = idx(x, y, c)
        copies = []
        for a in range(n):
            cp = pltpu.make_async_copy(src[a] if bcast else src[a].at[me], dst[a].at[me], loc_sems.at[a])
            cp.start()
            copies.append(cp)
        for s, k in enumerate(ks):
            px, py, pc = x ^ ((k >> 2) & 1), y ^ ((k >> 1) & 1), c ^ (k & 1)
            for a in range(n):
                cp = pltpu.make_async_remote_copy(
                    src_ref=src[a] if bcast else src[a].at[idx(px, py, pc)],
                    dst_ref=dst[a].at[me],
                    send_sem=send_sems.at[a, s],
                    recv_sem=recv_sems.at[a, s],
                    device_id=(px, py, pc),
                    device_id_type=pl.DeviceIdType.MESH,
                )
                cp.start()
                copies.append(cp)
        for cp in copies:
            cp.wait()

    return pl.pallas_call(
        body,
        out_shape=out_shape,
        in_specs=[_ANY] * n,
        out_specs=[_ANY] * n,
        scratch_shapes=[
            pltpu.SemaphoreType.DMA((n, len(ks))),
            pltpu.SemaphoreType.DMA((n, len(ks))),
            pltpu.SemaphoreType.DMA((n,)),
        ],
        name=name,
    )(*srcs)


def gather_two_level(srcs, name):
    n = len(srcs)
    out_shape = [jax.ShapeDtypeStruct((NDEV,) + s.shape, s.dtype) for s in srcs]

    def body(*refs):
        src, dst = refs[:n], refs[n:2 * n]
        send_sems, recv_sems, loc_sems = refs[2 * n:]
        x, y, c = _place()
        me = 4 * x + 2 * y + c
        sib = (x, y, 1 - c)
        first = (x ^ (1 - c), y ^ c)
        second = (x ^ c, y ^ (1 - c))
        diag = (x ^ 1, y ^ 1)

        def row(chip, core):
            return 4 * chip[0] + 2 * chip[1] + core

        def copy(a, s, block, to, from_src=False):
            return pltpu.make_async_remote_copy(
                src_ref=src[a] if from_src else dst[a].at[block], dst_ref=dst[a].at[block],
                send_sem=send_sems.at[a, s], recv_sem=recv_sems.at[a, s],
                device_id=to, device_id_type=pl.DeviceIdType.MESH)

        local = [pltpu.make_async_copy(src[a], dst[a].at[me], loc_sems.at[a]) for a in range(n)]
        sent = [copy(a, 0, me, sib, True) for a in range(n)]
        sent += [copy(a, 1, me, (*first, c), True) for a in range(n)]
        sent += [copy(a, 2, me, (*second, c), True) for a in range(n)]
        for cp in local + sent:
            cp.start()
        for a in range(n):
            copy(a, 1, row(first, c), (*first, c)).wait_recv()
            for cp in (copy(a, 3, row(first, c), (*second, c)), copy(a, 5, row(first, c), sib)):
                cp.start()
                sent.append(cp)
        for a in range(n):
            copy(a, 2, row(second, c), (*second, c)).wait_recv()
            cp = copy(a, 4, row(second, c), sib)
            cp.start()
            sent.append(cp)
        for a in range(n):
            copy(a, 3, row(diag, c), (*second, c)).wait_recv()
            cp = copy(a, 6, row(diag, c), sib)
            cp.start()
            sent.append(cp)
        for a in range(n):
            copy(a, 0, row((x, y), 1 - c), sib).wait_recv()
            copy(a, 4, row(first, 1 - c), sib).wait_recv()
            copy(a, 5, row(second, 1 - c), sib).wait_recv()
            copy(a, 6, row(diag, 1 - c), sib).wait_recv()
        for cp in local:
            cp.wait()
        for cp in sent:
            cp.wait_send()

    return pl.pallas_call(
        body,
        out_shape=out_shape,
        in_specs=[_ANY] * n,
        out_specs=[_ANY] * n,
        scratch_shapes=[
            pltpu.SemaphoreType.DMA((n, NDEV - 1)),
            pltpu.SemaphoreType.DMA((n, NDEV - 1)),
            pltpu.SemaphoreType.DMA((n,)),
        ],
        name=name,
    )(*srcs)


def _chip_across(core, da, db):
    x, y, _ = _place()
    return x ^ (da * (1 - core) + db * core), y ^ (db * (1 - core) + da * core)


def pair_swap(srcs, name, axis="c"):
    n = len(srcs)

    def body(*refs):
        src, dst = refs[:n], refs[n:2 * n]
        send_sems, recv_sems = refs[2 * n:]
        x, y, c = _place()
        partner = {"c": (x, y, 1 - c), "first": (*_chip_across(c, 1, 0), c), "second": (*_chip_across(c, 0, 1), c)}[axis]
        copies = []
        for a in range(n):
            cp = pltpu.make_async_remote_copy(
                src_ref=src[a], dst_ref=dst[a], send_sem=send_sems.at[a], recv_sem=recv_sems.at[a],
                device_id=partner, device_id_type=pl.DeviceIdType.MESH)
            cp.start()
            copies.append(cp)
        for cp in copies:
            cp.wait()

    return pl.pallas_call(
        body,
        out_shape=[jax.ShapeDtypeStruct(s.shape, s.dtype) for s in srcs],
        in_specs=[_ANY] * n,
        out_specs=[_ANY] * n,
        scratch_shapes=[pltpu.SemaphoreType.DMA((n,)), pltpu.SemaphoreType.DMA((n,))],
        name=name,
    )(*srcs)


_OFFSETS = ((0, 0), (0, 1), (1, 0), (1, 1))


def sibling_swap_blocks(blocks, name):
    n = len(blocks)

    def body(*refs):
        src, dst = refs[:n], refs[n:2 * n]
        send_sems, recv_sems = refs[2 * n:]
        x, y, c = _place()
        copies = []
        for a in range(n):
            for j, (da, db) in enumerate(_OFFSETS):
                px, py = _chip_across(1 - c, da, db)
                cp = pltpu.make_async_remote_copy(
                    src_ref=src[a].at[4 * px + 2 * py + (1 - c)], dst_ref=dst[a].at[j],
                    send_sem=send_sems.at[a, j], recv_sem=recv_sems.at[a, j],
                    device_id=(x, y, 1 - c), device_id_type=pl.DeviceIdType.MESH)
                cp.start()
                copies.append(cp)
        for cp in copies:
            cp.wait()

    return pl.pallas_call(
        body,
        out_shape=[jax.ShapeDtypeStruct((4,) + b.shape[1:], b.dtype) for b in blocks],
        in_specs=[_ANY] * n,
        out_specs=[_ANY] * n,
        scratch_shapes=[pltpu.SemaphoreType.DMA((n, 4)), pltpu.SemaphoreType.DMA((n, 4))],
        name=name,
    )(*blocks)


def add_own_blocks(blocks, got, name):
    _, R, C = blocks.shape
    tile = _pick(R, (256, 128, 64))

    def body(*refs):
        for j in range(4):
            refs[8 + j][...] = (refs[j][...].astype(F32) + refs[4 + j][...].astype(F32)).astype(refs[8 + j].dtype)

    def own(da, db):
        def index(i):
            c = lax.axis_index("c")
            px, py = _chip_across(c, da, db)
            return 4 * px + 2 * py + c, i, 0
        return pl.BlockSpec((None, tile, C), index)

    return pl.pallas_call(
        body,
        grid=(R // tile,),
        in_specs=[own(da, db) for da, db in _OFFSETS]
        + [pl.BlockSpec((None, tile, C), lambda i, j=j: (j, i, 0)) for j in range(4)],
        out_specs=[pl.BlockSpec((tile, C), lambda i: (i, 0))] * 4,
        out_shape=[jax.ShapeDtypeStruct((R, C), blocks.dtype)] * 4,
        compiler_params=_cp(("parallel",)),
        name=name,
    )(blocks, blocks, blocks, blocks, got, got, got, got)


def scatter_reduce(blocks, name):
    def add(n_out, ins, label):
        fn = lambda *a: [a[i].astype(F32) + a[n_out + i].astype(F32) for i in range(n_out)]
        rows, cols = ins[0].shape
        return rowwise(fn, ins, [], [(cols, ins[0].dtype)] * n_out, [], _pick(rows, (256, 128, 64)), label)

    nb = len(blocks)
    got = sibling_swap_blocks(blocks, name + "_d2d")
    q = [add_own_blocks(blocks[i], got[i], f"{name}_sum0_{i}") for i in range(nb)]
    r1 = pair_swap([q[i][j] for i in range(nb) for j in (2, 3)], name + "_ici1", "first")
    k = [add(2, [q[i][0], q[i][1], r1[2 * i], r1[2 * i + 1]], f"{name}_sum1_{i}") for i in range(nb)]
    r2 = pair_swap([k[i][1] for i in range(nb)], name + "_ici2", "second")
    return [jnp.stack([k[i][0], r2[i]]) for i in range(nb)]


def matmul(a, b, mode, out_dtype, name, add=None):
    if mode == "nn":
        (M, K), N = a.shape, b.shape[1]
    elif mode == "nt":
        (M, K), N = a.shape, b.shape[0]
    else:
        (K, M), N = a.shape, b.shape[1]
    tm = _pick(M, (1024, 512, 256, 128))
    tn = _pick(N, (1024, 512, 256, 128))
    osz = jnp.dtype(out_dtype).itemsize

    def vmem_bytes(tk):
        ops = 2 * tk * (tm * a.dtype.itemsize + tn * b.dtype.itemsize)
        return ops + tm * tn * (2 * osz + (4 if tk < K else 0) + (8 if add is not None else 0))

    tk = next((t for t in (K, 2816, 2048, 1408, 1024, 512, 256, 128) if K % t == 0 and vmem_bytes(t) <= MATMUL_VMEM_BUDGET), K)
    nk = K // tk
    dims = {"nn": NN, "nt": NT, "tn": TN}[mode]

    def body(*refs):
        if add is None:
            a_ref, b_ref, o_ref = refs[:3]
            c_ref = None
        else:
            a_ref, b_ref, c_ref, o_ref = refs[:4]

        def prod():
            return _dot(a_ref[...].astype(BF16), b_ref[...].astype(BF16), dims)

        def finish(r):
            if c_ref is not None:
                r = r + c_ref[...].astype(F32)
            o_ref[...] = r.astype(o_ref.dtype)

        if nk == 1:
            finish(prod())
            return
        acc = refs[-1]
        k = pl.program_id(2)

        @pl.when(k == 0)
        def _():
            acc[...] = prod()

        if nk > 2:
            @pl.when((k > 0) & (k < nk - 1))
            def _():
                acc[...] += prod()

        @pl.when(k == nk - 1)
        def _():
            finish(acc[...] + prod())

    a_spec = pl.BlockSpec((tk, tm), lambda i, j, k: (k, i)) if mode == "tn" else pl.BlockSpec((tm, tk), lambda i, j, k: (i, k))
    b_spec = pl.BlockSpec((tn, tk), lambda i, j, k: (j, k)) if mode == "nt" else pl.BlockSpec((tk, tn), lambda i, j, k: (k, j))
    o_spec = pl.BlockSpec((tm, tn), lambda i, j, k: (i, j))
    ins, specs = [a, b], [a_spec, b_spec]
    if add is not None:
        ins.append(add)
        specs.append(o_spec)
    return pl.pallas_call(
        body,
        grid=(M // tm, N // tn, nk),
        in_specs=specs,
        out_specs=o_spec,
        out_shape=jax.ShapeDtypeStruct((M, N), out_dtype),
        scratch_shapes=[pltpu.VMEM((tm, tn), F32)] if nk > 1 else [],
        compiler_params=_cp(("parallel", "parallel", "arbitrary")),
        name=name,
    )(*ins)


def rowwise(fn, tiled, full, out_tiled, out_acc, tile, name, out_t=()):
    tiled = [t if isinstance(t, tuple) else (t, t.shape[1], 0) for t in tiled]
    rows = tiled[0][0].shape[0]
    tile = min(tile, rows)
    assert rows % tile == 0
    nt, nf, nrow = len(tiled), len(full), len(out_tiled)
    no = nrow + len(out_t)

    def body(*refs):
        ins = [r[...] for r in refs[:nt + nf]]
        res = fn(*ins)
        if not isinstance(res, (tuple, list)):
            res = (res,)
        outs = refs[nt + nf:]
        for r, v in zip(outs[:nrow], res[:nrow]):
            r[...] = v.astype(r.dtype)
        for r, v in zip(outs[nrow:no], res[nrow:no]):
            r[...] = v.astype(r.dtype).T
        if out_acc:
            @pl.when(pl.program_id(0) == 0)
            def _():
                for r in outs[no:]:
                    r[...] = jnp.zeros_like(r)

            for r, v in zip(outs[no:], res[no:]):
                r[...] += v

    in_specs = [pl.BlockSpec((tile, w), lambda i, cb=cb: (i, cb)) for (_, w, cb) in tiled]
    in_specs += [pl.BlockSpec(f.shape, lambda i, nd=f.ndim: (0,) * nd) for f in full]
    out_specs = [pl.BlockSpec((tile, w), lambda i: (i, 0)) for (w, _) in out_tiled]
    out_specs += [pl.BlockSpec((w, tile), lambda i: (0, i)) for (w, _) in out_t]
    out_specs += [pl.BlockSpec(s, lambda i, nd=len(s): (0,) * nd) for s in out_acc]
    out_shape = [jax.ShapeDtypeStruct((rows, w), dt) for (w, dt) in out_tiled]
    out_shape += [jax.ShapeDtypeStruct((w, rows), dt) for (w, dt) in out_t]
    out_shape += [jax.ShapeDtypeStruct(s, F32) for s in out_acc]
    res = pl.pallas_call(
        body,
        grid=(rows // tile,),
        in_specs=in_specs,
        out_specs=out_specs,
        out_shape=out_shape,
        compiler_params=_cp(("arbitrary",) if out_acc else ("parallel",)),
        name=name,
    )(*[t[0] for t in tiled], *full)
    return res


def adam_reduce(parts, w, m, v, name):
    P, R, C = parts.shape
    tr = _pick(R, (64, 32, 16, 8))
    c1 = 1.0 - ADAM_B1 ** ADAM_STEP
    c2 = 1.0 - ADAM_B2 ** ADAM_STEP

    def body(p_ref, w_ref, m_ref, v_ref, g_ref, d_ref, nm_ref, nv_ref):
        g = p_ref[0].astype(F32)
        for j in range(1, P):
            g = g + p_ref[j].astype(F32)
        mm = ADAM_B1 * m_ref[...] + (1.0 - ADAM_B1) * g
        vv = ADAM_B2 * v_ref[...] + (1.0 - ADAM_B2) * (g * g)
        m_hat = mm / c1
        v_hat = vv / c2
        g_ref[...] = g
        d_ref[...] = -ADAM_LR * (m_hat / (jnp.sqrt(v_hat) + ADAM_EPS) + ADAM_WD * w_ref[...])
        nm_ref[...] = mm
        nv_ref[...] = vv

    spec = pl.BlockSpec((tr, C), lambda i: (i, 0))
    return pl.pallas_call(
        body,
        grid=(R // tr,),
        in_specs=[pl.BlockSpec((P, tr, C), lambda i: (0, i, 0)), spec, spec, spec],
        out_specs=[spec] * 4,
        out_shape=[jax.ShapeDtypeStruct((R, C), F32)] * 4,
        compiler_params=_cp(("parallel",)),
        name=name,
    )(parts, w, m, v)


def reduce_parts(parts, name):
    P, R, C = parts.shape
    tr = _pick(R, (64, 32, 16, 8))

    def body(p_ref, g_ref):
        g = p_ref[0]
        for j in range(1, P):
            g = g + p_ref[j]
        g_ref[...] = g

    return pl.pallas_call(
        body,
        grid=(R // tr,),
        in_specs=[pl.BlockSpec((P, tr, C), lambda i: (0, i, 0))],
        out_specs=pl.BlockSpec((tr, C), lambda i: (i, 0)),
        out_shape=jax.ShapeDtypeStruct((R, C), F32),
        compiler_params=_cp(("parallel",)),
        name=name,
    )(parts)


def pack(arrs):
    flat = jnp.concatenate([a.reshape(-1).astype(F32) for a in arrs])
    n = flat.shape[0]
    padded = -(-n // 1024) * 1024
    return jnp.pad(flat, (0, padded - n)).reshape(padded // 128, 128)


def blocks_from_segments(segs, ncols):
    offs = np.cumsum([0] + [s.shape[1] for s in segs]).tolist()
    blocks = []
    for j in range(NDEV):
        lo, hi = j * ncols, (j + 1) * ncols
        parts = [s[:, max(lo, o) - o:min(hi, o + s.shape[1]) - o]
                 for s, o in zip(segs, offs[:-1]) if max(lo, o) < min(hi, o + s.shape[1])]
        blocks.append(jnp.concatenate(parts, axis=1) if len(parts) > 1 else parts[0])
    return jnp.stack(blocks)


def rows_to_blocks(g):
    return g.reshape(NDEV, -1, g.shape[1])


def segments_from_blocks(g, widths):
    ncols = g.shape[2]
    offs = np.cumsum([0] + list(widths)).tolist()
    out = []
    for o, w in zip(offs[:-1], widths):
        parts = [g[j][:, max(j * ncols, o) - j * ncols:min((j + 1) * ncols, o + w) - j * ncols]
                 for j in range(NDEV) if max(j * ncols, o) < min((j + 1) * ncols, o + w)]
        out.append(jnp.concatenate(parts, axis=1) if len(parts) > 1 else parts[0])
    return out


def unpack(slab, shapes):
    flat = slab.reshape(-1)
    out, off = [], 0
    for s in shapes:
        size = int(np.prod(s))
        out.append(flat[off:off + size].reshape(s))
        off += size
    return out


def modulate_fwd(x, g, sh, sc, name, with_t=False):
    def fn(x, g, sh, sc):
        h = x * _rstd(x) * g * (1.0 + sc) + sh
        return (h, h) if with_t else h

    D = x.shape[1]
    return rowwise(fn, [x], [g, sh, sc], [(D, BF16)], [], 256, name, out_t=[(D, BF16)] if with_t else ())


def norm_rope_fwd(p, width, cb, w, cosf, sinf, hd, name):
    nh = width // hd

    def fn(x, cosf, sinf, w):
        outs = []
        for h in range(nh):
            xh = x[:, h * hd:(h + 1) * hd]
            y = xh * _rstd(xh) * w
            outs.append(y * cosf + pltpu.roll(y, hd // 2, 1) * sinf)
        return jnp.concatenate(outs, axis=1) if nh > 1 else outs[0]

    return rowwise(fn, [(p, width, cb), cosf, sinf], [w], [(width, BF16)], [], 256, name)[0]


def norm_rope_bwd(p, width, cb, d, w, cosf, sinf, hd, name):
    nh = width // hd

    def fn(x, d, cosf, sinf, w):
        outs = []
        dw = jnp.zeros((1, hd), F32)
        for h in range(nh):
            xh = x[:, h * hd:(h + 1) * hd]
            dh = d[:, h * hd:(h + 1) * hd].astype(F32)
            r = _rstd(xh)
            n = xh * r
            dy = dh * cosf + pltpu.roll(dh * sinf, hd // 2, 1)
            dw = dw + jnp.sum(dy * n, axis=0, keepdims=True)
            dn = dy * w
            outs.append(r * (dn - n * jnp.mean(dn * n, axis=-1, keepdims=True)))
        return (jnp.concatenate(outs, axis=1) if nh > 1 else outs[0]), dw

    return rowwise(fn, [(p, width, cb), d, cosf, sinf], [w], [(width, BF16)], [(1, hd)], 256, name)


def attention_fwd(qr, kr, pkv, kcr, pkv_c, sink, hkv, hd, name):
    T, L = qr.shape[0], kcr.shape[0]
    G = qr.shape[1] // (hkv * hd)
    nb = T // BLOCK
    scale = hd ** -0.5

    def body(q_ref, kp, kc, kn, vp, vc, vn, ck_ref, cv_ref, sink_ref, o_ref, lse_ref, lser_ref):
        i = pl.program_id(1)
        kwin = jnp.concatenate([kp[...], kc[...], kn[...]], axis=0)
        vwin = jnp.concatenate([vp[...], vc[...], vn[...]], axis=0).astype(BF16)
        ck, cv = ck_ref[...], cv_ref[...].astype(BF16)
        row = lax.broadcasted_iota(jnp.int32, (BLOCK, 3 * BLOCK), 0)
        col = lax.broadcasted_iota(jnp.int32, (BLOCK, 3 * BLOCK), 1)
        rel = col - BLOCK - row
        valid = (jnp.abs(rel) <= WINDOW) & ((col >= BLOCK) | (i > 0)) & ((col < 2 * BLOCK) | (i < nb - 1))
        R = range(G)
        qa = q_ref[...]
        qs = [qa[:, g * hd:(g + 1) * hd] for g in R]
        sks = [sink_ref[g] for g in R]
        ss = [jnp.where(valid, _dot(qs[g], kwin, NT) * scale, NEG) for g in R]
        scs = [_dot(qs[g], ck, NT) * scale for g in R]
        ms = [jnp.maximum(jnp.maximum(jnp.max(ss[g], axis=1, keepdims=True), jnp.max(scs[g], axis=1, keepdims=True)),
                          sks[g]) for g in R]
        ps = [jnp.exp(ss[g] - ms[g]) for g in R]
        pcs = [jnp.exp(scs[g] - ms[g]) for g in R]
        nums = [_dot(ps[g].astype(BF16), vwin, NN) + _dot(pcs[g].astype(BF16), cv, NN) for g in R]
        dens = [jnp.exp(sks[g] - ms[g]) + jnp.sum(ps[g], axis=1, keepdims=True) + jnp.sum(pcs[g], axis=1, keepdims=True)
                for g in R]
        o_ref[...] = jnp.concatenate([(nums[g] / dens[g]).astype(o_ref.dtype) for g in R], axis=1)
        eye = (lax.broadcasted_iota(jnp.int32, (BLOCK, BLOCK), 0)
               == lax.broadcasted_iota(jnp.int32, (BLOCK, BLOCK), 1)).astype(F32)
        for g in R:
            lg = ms[g] + jnp.log(dens[g])
            lse_ref[g] = lg
            lser_ref[g] = jnp.sum(lg * eye, axis=0, keepdims=True)

    kv_specs = [
        pl.BlockSpec((BLOCK, hd), lambda h, i: (jnp.maximum(i - 1, 0), h)),
        pl.BlockSpec((BLOCK, hd), lambda h, i: (i, h)),
        pl.BlockSpec((BLOCK, hd), lambda h, i: (jnp.minimum(i + 1, nb - 1), h)),
    ]
    v_specs = [
        pl.BlockSpec((BLOCK, hd), lambda h, i: (jnp.maximum(i - 1, 0), hkv + h)),
        pl.BlockSpec((BLOCK, hd), lambda h, i: (i, hkv + h)),
        pl.BlockSpec((BLOCK, hd), lambda h, i: (jnp.minimum(i + 1, nb - 1), hkv + h)),
    ]
    return pl.pallas_call(
        body,
        grid=(hkv, nb),
        in_specs=[pl.BlockSpec((BLOCK, G * hd), lambda h, i: (i, h))] + kv_specs + v_specs + [
            pl.BlockSpec((L, hd), lambda h, i: (0, h)),
            pl.BlockSpec((L, hd), lambda h, i: (0, hkv + h)),
            pl.BlockSpec((G, 1, 1), lambda h, i: (h, 0, 0)),
        ],
        out_specs=[
            pl.BlockSpec((BLOCK, G * hd), lambda h, i: (i, h)),
            pl.BlockSpec((G, BLOCK, 1), lambda h, i: (h, i, 0)),
            pl.BlockSpec((G, 1, BLOCK), lambda h, i: (h, 0, i)),
        ],
        out_shape=[jax.ShapeDtypeStruct(qr.shape, BF16), jax.ShapeDtypeStruct((hkv * G, T, 1), F32),
                   jax.ShapeDtypeStruct((hkv * G, 1, T), F32)],
        compiler_params=_cp(("parallel", "parallel")),
        name=name,
    )(qr, kr, kr, kr, pkv, pkv, pkv, kcr, pkv_c, sink)


def attention_bwd_q(qr, kr, pkv, kcr, pkv_c, sink, do, o, lse, hkv, hd, name):
    T, L = qr.shape[0], kcr.shape[0]
    G = qr.shape[1] // (hkv * hd)
    nb = T // BLOCK
    scale = hd ** -0.5

    def body(q_ref, kp, kc, kn, vp, vc, vn, ck_ref, cv_ref, sink_ref, do_ref, o_ref, lse_ref,
             dq_ref, dck_ref, dcv_ref, dsink_ref, drr_ref):
        i = pl.program_id(1)

        @pl.when(i == 0)
        def _():
            dck_ref[...] = jnp.zeros_like(dck_ref)
            dcv_ref[...] = jnp.zeros_like(dcv_ref)
            dsink_ref[...] = jnp.zeros_like(dsink_ref)

        kwin = jnp.concatenate([kp[...], kc[...], kn[...]], axis=0)
        vwin = jnp.concatenate([vp[...], vc[...], vn[...]], axis=0).astype(BF16)
        ck, cv = ck_ref[...], cv_ref[...].astype(BF16)
        row = lax.broadcasted_iota(jnp.int32, (BLOCK, 3 * BLOCK), 0)
        col = lax.broadcasted_iota(jnp.int32, (BLOCK, 3 * BLOCK), 1)
        rel = col - BLOCK - row
        valid = (jnp.abs(rel) <= WINDOW) & ((col >= BLOCK) | (i > 0)) & ((col < 2 * BLOCK) | (i < nb - 1))
        R = range(G)
        qa, doa, oa = q_ref[...], do_ref[...], o_ref[...]
        qs = [qa[:, g * hd:(g + 1) * hd] for g in R]
        dos = [doa[:, g * hd:(g + 1) * hd] for g in R]
        lgs = [lse_ref[g] for g in R]
        sks = [sink_ref[g] for g in R]
        ss = [jnp.where(valid, _dot(qs[g], kwin, NT) * scale, NEG) for g in R]
        scs = [_dot(qs[g], ck, NT) * scale for g in R]
        dps = [_dot(dos[g], vwin, NT) for g in R]
        dpcs = [_dot(dos[g], cv, NT) for g in R]
        drs = [jnp.sum(dos[g].astype(F32) * oa[:, g * hd:(g + 1) * hd].astype(F32), axis=1, keepdims=True) for g in R]
        ps = [jnp.exp(ss[g] - lgs[g]) for g in R]
        pcs = [jnp.exp(scs[g] - lgs[g]) for g in R]
        dss = [(ps[g] * (dps[g] - drs[g]) * scale).astype(BF16) for g in R]
        dscs = [(pcs[g] * (dpcs[g] - drs[g]) * scale).astype(BF16) for g in R]
        dqs = [_dot(dss[g], kwin, NN) + _dot(dscs[g], ck, NN) for g in R]
        dcks = [_dot(dscs[g], qs[g], TN) for g in R]
        dcvs = [_dot(pcs[g].astype(BF16), dos[g], TN) for g in R]
        dq_ref[...] = jnp.concatenate(dqs, axis=1)
        dck_ref[...] += (dcks[0] + dcks[1]) + (dcks[2] + dcks[3]) if G == 4 else sum(dcks[1:], dcks[0])
        dcv_ref[...] += (dcvs[0] + dcvs[1]) + (dcvs[2] + dcvs[3]) if G == 4 else sum(dcvs[1:], dcvs[0])
        eye = (lax.broadcasted_iota(jnp.int32, (BLOCK, BLOCK), 0)
               == lax.broadcasted_iota(jnp.int32, (BLOCK, BLOCK), 1)).astype(F32)
        for g in R:
            dsink_ref[g] += -jnp.sum(jnp.exp(sks[g] - lgs[g]) * drs[g], axis=0, keepdims=True)
            drr_ref[g] = jnp.sum(drs[g] * eye, axis=0, keepdims=True)

    kv_specs = [
        pl.BlockSpec((BLOCK, hd), lambda h, i: (jnp.maximum(i - 1, 0), h)),
        pl.BlockSpec((BLOCK, hd), lambda h, i: (i, h)),
        pl.BlockSpec((BLOCK, hd), lambda h, i: (jnp.minimum(i + 1, nb - 1), h)),
    ]
    v_specs = [
        pl.BlockSpec((BLOCK, hd), lambda h, i: (jnp.maximum(i - 1, 0), hkv + h)),
        pl.BlockSpec((BLOCK, hd), lambda h, i: (i, hkv + h)),
        pl.BlockSpec((BLOCK, hd), lambda h, i: (jnp.minimum(i + 1, nb - 1), hkv + h)),
    ]
    qspec = pl.BlockSpec((BLOCK, G * hd), lambda h, i: (i, h))
    return pl.pallas_call(
        body,
        grid=(hkv, nb),
        in_specs=[qspec] + kv_specs + v_specs + [
            pl.BlockSpec((L, hd), lambda h, i: (0, h)),
            pl.BlockSpec((L, hd), lambda h, i: (0, hkv + h)),
            pl.BlockSpec((G, 1, 1), lambda h, i: (h, 0, 0)),
            qspec, qspec,
            pl.BlockSpec((G, BLOCK, 1), lambda h, i: (h, i, 0)),
        ],
        out_specs=[
            qspec,
            pl.BlockSpec((L, hd), lambda h, i: (0, h)),
            pl.BlockSpec((L, hd), lambda h, i: (0, h)),
            pl.BlockSpec((G, 1, 1), lambda h, i: (h, 0, 0)),
            pl.BlockSpec((G, 1, BLOCK), lambda h, i: (h, 0, i)),
        ],
        out_shape=[
            jax.ShapeDtypeStruct(qr.shape, F32),
            jax.ShapeDtypeStruct((L, hkv * hd), F32),
            jax.ShapeDtypeStruct((L, hkv * hd), F32),
            jax.ShapeDtypeStruct((hkv * G, 1, 1), F32),
            jax.ShapeDtypeStruct((hkv * G, 1, T), F32),
        ],
        compiler_params=_cp(("parallel", "arbitrary")),
        name=name,
    )(qr, kr, kr, kr, pkv, pkv, pkv, kcr, pkv_c, sink, do, o, lse)


def attention_bwd_kv(qr, kr, pkv, do, lse_row, dr_row, hkv, hd, name):
    T = qr.shape[0]
    G = qr.shape[1] // (hkv * hd)
    nb = T // BLOCK
    scale = hd ** -0.5

    def body(k_ref, v_ref, *refs):
        qs, dos, lses, drs = refs[0:3], refs[3:6], refs[6:9], refs[9:12]
        dk_ref, dv_ref = refs[12:]
        j = pl.program_id(1)
        k = k_ref[...]
        v = v_ref[...].astype(BF16)
        row = lax.broadcasted_iota(jnp.int32, (BLOCK, BLOCK), 0)
        col = lax.broadcasted_iota(jnp.int32, (BLOCK, BLOCK), 1)
        bias = []
        for d in range(3):
            iq = j + d - 1
            rel = row - col - (d - 1) * BLOCK
            valid = (jnp.abs(rel) <= WINDOW) & (iq >= 0) & (iq < nb)
            bias += [jnp.where(valid, 0.0, NEG)] * G
        bias = jnp.concatenate(bias, axis=1)

        def stack(refs):
            vals = [r[...] for r in refs]
            return jnp.concatenate([a[:, g * hd:(g + 1) * hd] for a in vals for g in range(G)], axis=0)

        q, dob = stack(qs), stack(dos)
        lrow = jnp.concatenate([r[g] for r in lses for g in range(G)], axis=1)
        drow = jnp.concatenate([r[g] for r in drs for g in range(G)], axis=1)
        st = _dot(k, q, NT) * scale + bias
        pt = jnp.exp(st - lrow)
        dpt = _dot(v, dob, NT)
        dst = (pt * (dpt - drow) * scale).astype(BF16)
        dk_ref[...] = _dot(dst, q, NN).astype(dk_ref.dtype)
        dv_ref[...] = _dot(pt.astype(BF16), dob, NN).astype(dv_ref.dtype)

    def q3(width_block):
        return [
            pl.BlockSpec(width_block, lambda h, j: (jnp.maximum(j - 1, 0), h)),
            pl.BlockSpec(width_block, lambda h, j: (j, h)),
            pl.BlockSpec(width_block, lambda h, j: (jnp.minimum(j + 1, nb - 1), h)),
        ]

    row3 = [
        pl.BlockSpec((G, 1, BLOCK), lambda h, j: (h, 0, jnp.maximum(j - 1, 0))),
        pl.BlockSpec((G, 1, BLOCK), lambda h, j: (h, 0, j)),
        pl.BlockSpec((G, 1, BLOCK), lambda h, j: (h, 0, jnp.minimum(j + 1, nb - 1))),
    ]
    qb = (BLOCK, G * hd)
    return pl.pallas_call(
        body,
        grid=(hkv, nb),
        in_specs=[pl.BlockSpec((BLOCK, hd), lambda h, j: (j, h)), pl.BlockSpec((BLOCK, hd), lambda h, j: (j, hkv + h))]
        + q3(qb) + q3(qb) + row3 + row3,
        out_specs=[pl.BlockSpec((BLOCK, hd), lambda h, j: (j, h))] * 2,
        out_shape=[jax.ShapeDtypeStruct((T, hkv * hd), BF16)] * 2,
        compiler_params=_cp(("parallel", "parallel")),
        name=name,
    )(kr, pkv, qr, qr, qr, do, do, do, lse_row, lse_row, lse_row, dr_row, dr_row, dr_row)


def gate_fwd(plr, wf, wb, bf, bb, name):
    n = wf.shape[1]

    def fn(lr, wf, wb, bf, bb):
        lrb = lr.astype(BF16)
        outs = []
        for w, b in ((wf, bf), (wb, bb)):
            z = _dot(lrb, w.astype(BF16), NN) + b
            outs.append((jnp.minimum(z, 0.0) - jnp.log(1.0 + jnp.exp(-jnp.abs(z)))) / GLA_GATE_NORM)
        return outs

    return rowwise(fn, [plr], [wf, wb, bf, bb], [(n, F32), (n, F32)], [], 256, name)


def gate_bwd(plr, dgf, dgb, wf, wb, bf, bb, name):
    n = wf.shape[1]

    def fn(lr, dgf, dgb, wf, wb, bf, bb):
        lrb = lr.astype(BF16)
        dlr = jnp.zeros(lr.shape, F32)
        res = []
        for w, b, dg in ((wf, bf, dgf), (wb, bb, dgb)):
            wb16 = w.astype(BF16)
            z = _dot(lrb, wb16, NN) + b
            dz = dg * _sig(-z) / GLA_GATE_NORM
            dzb = dz.astype(BF16)
            dlr = dlr + _dot(dzb, wb16, NT)
            res += [_dot(lrb, dzb, TN), jnp.sum(dz, axis=0, keepdims=True)]
        return [dlr] + res

    return rowwise(fn, [plr, dgf, dgb], [wf, wb, bf, bb], [(128, BF16)],
                   [(128, n), (1, n), (128, n), (1, n)], 256, name)


def _tri_dot(tri_b, x):
    x1 = x.astype(BF16)
    r1 = x - x1.astype(F32)
    x2 = r1.astype(BF16)
    x3 = (r1 - x2.astype(F32)).astype(BF16)
    return _dot(tri_b, x1, NN) + _dot(tri_b, x2, NN) + _dot(tri_b, x3, NN)


def gla_fwd(pqk, pv, gl, s0, heads, reverse, name, o_add=None):
    T = pqk.shape[0]
    dk = pqk.shape[1] // (2 * heads)
    dv = pv.shape[1] // heads
    C = GLA_CHUNK
    nc = T // C
    qscale = dk ** -0.5

    def body(*refs):
        if o_add is None:
            q_ref, k_ref, v_ref, g_ref, s0_ref, o_ref, st_ref, sf_ref, S = refs
            oa_ref = None
        else:
            q_ref, k_ref, v_ref, g_ref, s0_ref, oa_ref, o_ref, st_ref, sf_ref, S = refs
        n = pl.program_id(0)

        @pl.when(n == 0)
        def _():
            S[...] = s0_ref[...]

        r = lax.broadcasted_iota(jnp.int32, (C, C), 0)
        c = lax.broadcasted_iota(jnp.int32, (C, C), 1)
        tri = (r <= c) if reverse else (r >= c)
        trib = tri.astype(BF16)
        ga, qa, ka, va = g_ref[...], q_ref[...], k_ref[...], v_ref[...]
        sts = [S[h] for h in range(heads)]
        H = range(heads)
        gs = [ga[:, h * dk:(h + 1) * dk] for h in H]
        bs = [_tri_dot(trib, g) for g in gs]
        bls = [jnp.sum(g, axis=0, keepdims=True) for g in gs]
        mid = lax.broadcasted_iota(jnp.int32, (C, 1), 0) == C // 2
        bms = [jnp.sum(jnp.where(mid, b, 0.0), axis=0, keepdims=True) for b in bs]
        vs = [va[:, h * dv:(h + 1) * dv].astype(BF16) for h in H]
        qs = [qa[:, h * dk:(h + 1) * dk].astype(F32) * qscale for h in H]
        qes = [(qs[h] * jnp.exp(bs[h])).astype(BF16) for h in H]
        qms = [(qs[h] * jnp.exp(bs[h] - bms[h])).astype(BF16) for h in H]
        kms = [(ka[:, h * dk:(h + 1) * dk].astype(F32) * jnp.exp(bms[h] - bs[h])).astype(BF16) for h in H]
        kls = [(ka[:, h * dk:(h + 1) * dk].astype(F32) * jnp.exp(bls[h] - bs[h])).astype(BF16) for h in H]
        inter = [_dot(qes[h], sts[h].astype(BF16), NT) for h in H]
        upd = [_dot(vs[h], kls[h], TN) for h in H]
        As = [jnp.where(tri, _dot(qms[h], kms[h], NT), 0.0).astype(BF16) for h in H]
        outs = [inter[h] + _dot(As[h], vs[h], NN) for h in H]
        news = [sts[h] * jnp.exp(bls[h]) + upd[h] for h in H]
        o = jnp.concatenate(outs, axis=1)
        if oa_ref is not None:
            o = o + oa_ref[...]
        o_ref[...] = o
        for h in range(heads):
            st_ref[0, h] = sts[h]
            S[h] = news[h]

        @pl.when(n == nc - 1)
        def _():
            for h in range(heads):
                sf_ref[h] = news[h]

    def ci(n):
        return (nc - 1 - n) if reverse else n

    specs = [
        pl.BlockSpec((C, heads * dk), lambda n: (ci(n), 0)),
        pl.BlockSpec((C, heads * dk), lambda n: (ci(n), 1)),
        pl.BlockSpec((C, heads * dv), lambda n: (ci(n), 0)),
        pl.BlockSpec((C, heads * dk), lambda n: (ci(n), 0)),
        pl.BlockSpec((heads, dv, dk), lambda n: (0, 0, 0)),
    ]
    ins = [pqk, pqk, pv, gl, s0]
    if o_add is not None:
        specs.append(pl.BlockSpec((C, heads * dv), lambda n: (ci(n), 0)))
        ins.append(o_add)
    return pl.pallas_call(
        body,
        grid=(nc,),
        in_specs=specs,
        out_specs=[
            pl.BlockSpec((C, heads * dv), lambda n: (ci(n), 0)),
            pl.BlockSpec((1, heads, dv, dk), lambda n: (ci(n), 0, 0, 0)),
            pl.BlockSpec((heads, dv, dk), lambda n: (0, 0, 0)),
        ],
        out_shape=[
            jax.ShapeDtypeStruct((T, heads * dv), F32),
            jax.ShapeDtypeStruct((nc, heads, dv, dk), F32),
            jax.ShapeDtypeStruct((heads, dv, dk), F32),
        ],
        scratch_shapes=[pltpu.VMEM((heads, dv, dk), F32)],
        compiler_params=_cp(("arbitrary",)),
        name=name,
    )(*ins)


def gla_bwd(pqk, pv, gl, states, do, dsf, heads, reverse, name, acc=None):
    T = pqk.shape[0]
    dk = pqk.shape[1] // (2 * heads)
    dv = pv.shape[1] // heads
    C = GLA_CHUNK
    nc = T // C
    qscale = dk ** -0.5

    def body(*refs):
        if acc is None:
            q_ref, k_ref, v_ref, g_ref, st_ref, do_ref, dsf_ref, dq_ref, dk_ref, dv_ref, dg_ref, ds0_ref, dS = refs
            aq = ak = av = None
        else:
            (q_ref, k_ref, v_ref, g_ref, st_ref, do_ref, dsf_ref, aq, ak, av,
             dq_ref, dk_ref, dv_ref, dg_ref, ds0_ref, dS) = refs
        n = pl.program_id(0)

        @pl.when(n == 0)
        def _():
            dS[...] = dsf_ref[...]

        r = lax.broadcasted_iota(jnp.int32, (C, C), 0)
        c = lax.broadcasted_iota(jnp.int32, (C, C), 1)
        tri = (r <= c) if reverse else (r >= c)
        tri_t = (r >= c) if reverse else (r <= c)
        trib, tritb = tri.astype(BF16), tri_t.astype(BF16)
        ga, qa, ka, va, doa = g_ref[...], q_ref[...], k_ref[...], v_ref[...], do_ref[...]
        sts = [st_ref[0, h] for h in range(heads)]
        dsts = [dS[h] for h in range(heads)]
        H = range(heads)
        gs = [ga[:, h * dk:(h + 1) * dk] for h in H]
        bs = [_tri_dot(trib, g) for g in gs]
        bls = [jnp.sum(g, axis=0, keepdims=True) for g in gs]
        mid = lax.broadcasted_iota(jnp.int32, (C, 1), 0) == C // 2
        bms = [jnp.sum(jnp.where(mid, b, 0.0), axis=0, keepdims=True) for b in bs]
        ebs = [jnp.exp(b) for b in bs]
        embs = [jnp.exp(bs[h] - bms[h]) for h in H]
        enbs = [jnp.exp(bms[h] - bs[h]) for h in H]
        elbs = [jnp.exp(bls[h] - bs[h]) for h in H]
        ebls = [jnp.exp(bl) for bl in bls]
        vbs = [va[:, h * dv:(h + 1) * dv].astype(BF16) for h in H]
        dobs = [doa[:, h * dv:(h + 1) * dv].astype(BF16) for h in H]
        qs = [qa[:, h * dk:(h + 1) * dk].astype(F32) * qscale for h in H]
        qes = [qs[h] * ebs[h] for h in H]
        qms = [qs[h] * embs[h] for h in H]
        kms = [ka[:, h * dk:(h + 1) * dk].astype(F32) * enbs[h] for h in H]
        kls = [ka[:, h * dk:(h + 1) * dk].astype(F32) * elbs[h] for h in H]
        qebs = [a.astype(BF16) for a in qes]
        qmbs = [a.astype(BF16) for a in qms]
        kmbs = [a.astype(BF16) for a in kms]
        klbs = [a.astype(BF16) for a in kls]
        stbs = [a.astype(BF16) for a in sts]
        dstbs = [a.astype(BF16) for a in dsts]
        ps = [jnp.where(tri, _dot(qmbs[h], kmbs[h], NT), 0.0).astype(BF16) for h in H]
        dps = [jnp.where(tri, _dot(dobs[h], vbs[h], NT), 0.0).astype(BF16) for h in H]
        dqes = [_dot(dobs[h], stbs[h], NN) for h in H]
        dkls = [_dot(vbs[h], dstbs[h], NN) for h in H]
        dv1 = [_dot(klbs[h], dstbs[h], NT) for h in H]
        dsn1 = [_dot(dobs[h], qebs[h], TN) for h in H]
        dqms = [_dot(dps[h], kmbs[h], NN) for h in H]
        dkms = [_dot(dps[h], qmbs[h], TN) for h in H]
        dvs = [_dot(ps[h], dobs[h], TN) + dv1[h] for h in H]
        dbls = [ebls[h] * jnp.sum(dsts[h] * sts[h], axis=0, keepdims=True)
                + jnp.sum(dkls[h] * kls[h], axis=0, keepdims=True) for h in H]
        dsns = [dsn1[h] + dsts[h] * ebls[h] for h in H]
        dqs = [(dqes[h] * ebs[h] + dqms[h] * embs[h]) * qscale for h in H]
        dks = [dkms[h] * enbs[h] + dkls[h] * elbs[h] for h in H]
        dbs = [dqes[h] * qes[h] + dqms[h] * qms[h] - dkms[h] * kms[h] - dkls[h] * kls[h] for h in H]
        dgs = [_tri_dot(tritb, dbs[h]) + dbls[h] for h in H]
        dq, dkk, dvv = (jnp.concatenate(a, axis=1) for a in (dqs, dks, dvs))
        if aq is not None:
            dq = dq + aq[...].astype(F32)
            dkk = dkk + ak[...].astype(F32)
            dvv = dvv + av[...].astype(F32)
        dq_ref[...] = dq.astype(dq_ref.dtype)
        dk_ref[...] = dkk.astype(dk_ref.dtype)
        dv_ref[...] = dvv.astype(dv_ref.dtype)
        dg_ref[...] = jnp.concatenate(dgs, axis=1)
        for h in range(heads):
            dS[h] = dsns[h]

        @pl.when(n == nc - 1)
        def _():
            for h in range(heads):
                ds0_ref[h] = dsns[h]

    def ci(n):
        return n if reverse else (nc - 1 - n)

    kspec = pl.BlockSpec((C, heads * dk), lambda n: (ci(n), 0))
    vspec = pl.BlockSpec((C, heads * dv), lambda n: (ci(n), 0))
    sspec = pl.BlockSpec((heads, dv, dk), lambda n: (0, 0, 0))
    specs = [
        kspec,
        pl.BlockSpec((C, heads * dk), lambda n: (ci(n), 1)),
        vspec,
        kspec,
        pl.BlockSpec((1, heads, dv, dk), lambda n: (ci(n), 0, 0, 0)),
        vspec,
        sspec,
    ]
    ins = [pqk, pqk, pv, gl, states, do, dsf]
    odt = F32 if acc is None else BF16
    if acc is not None:
        specs += [kspec, kspec, vspec]
        ins += list(acc)
    return pl.pallas_call(
        body,
        grid=(nc,),
        in_specs=specs,
        out_specs=[kspec, kspec, vspec, kspec, sspec],
        out_shape=[
            jax.ShapeDtypeStruct((T, heads * dk), odt),
            jax.ShapeDtypeStruct((T, heads * dk), odt),
            jax.ShapeDtypeStruct((T, heads * dv), odt),
            jax.ShapeDtypeStruct((T, heads * dk), F32),
            jax.ShapeDtypeStruct((heads, dv, dk), F32),
        ],
        scratch_shapes=[pltpu.VMEM((heads, dv, dk), F32)],
        compiler_params=_cp(("arbitrary",)),
        name=name,
    )(*ins)


def gla_out_fwd(og, prb, gn, heads, name):
    dv = og.shape[1] // heads

    def fn(og, rb, gn):
        outs = []
        for h in range(heads):
            oh = og[:, h * dv:(h + 1) * dv]
            outs.append(oh * _rstd(oh) * gn)
        y = jnp.concatenate(outs, axis=1)
        return y * (rb * _sig(rb))

    return rowwise(fn, [og, prb], [gn], [(og.shape[1], BF16)], [], 256, name)[0]


def gla_out_bwd(og, prb, du, gn, heads, name):
    dv = og.shape[1] // heads

    def fn(og, rb, du, gn):
        sg = _sig(rb)
        silu = rb * sg
        dsilu = sg * (1.0 + rb * (1.0 - sg))
        dog, ys = [], []
        dgn = jnp.zeros((1, dv), F32)
        for h in range(heads):
            sl = slice(h * dv, (h + 1) * dv)
            oh = og[:, sl]
            r = _rstd(oh)
            n = oh * r
            ys.append(n * gn)
            dy = du[:, sl] * silu[:, sl]
            dgn = dgn + jnp.sum(dy * n, axis=0, keepdims=True)
            dn = dy * gn
            dog.append(r * (dn - n * jnp.mean(dn * n, axis=-1, keepdims=True)))
        y = jnp.concatenate(ys, axis=1)
        return jnp.concatenate(dog, axis=1), du * y * dsilu, dgn

    return rowwise(fn, [og, prb, du], [gn], [(og.shape[1], F32), (og.shape[1], BF16)], [(1, dv)], 128, name)


def conv_specs(T, tt, tc, off, order):
    r8 = tt // 8
    last8 = T // 8 - 1
    if order == "ij":
        return [
            pl.BlockSpec((tt, tc), lambda i, j: (i, j + off)),
            pl.BlockSpec((8, tc), lambda i, j: (jnp.maximum(i * r8 - 1, 0), j + off)),
            pl.BlockSpec((8, tc), lambda i, j: (jnp.minimum((i + 1) * r8, last8), j + off)),
        ]
    return [
        pl.BlockSpec((tt, tc), lambda j, i: (i, j + off)),
        pl.BlockSpec((8, tc), lambda j, i: (jnp.maximum(i * r8 - 1, 0), j + off)),
        pl.BlockSpec((8, tc), lambda j, i: (jnp.minimum((i + 1) * r8, last8), j + off)),
    ]


def _shifted(u, hp, hn, i, nt_):
    tt = u.shape[0]
    row = lax.broadcasted_iota(jnp.int32, u.shape, 0)
    r8 = lax.broadcasted_iota(jnp.int32, hp.shape, 0)
    prev = jnp.sum(jnp.where(r8 == 7, hp, 0.0), axis=0, keepdims=True) * (i > 0).astype(F32)
    nxt = jnp.sum(jnp.where(r8 == 0, hn, 0.0), axis=0, keepdims=True) * (i < nt_ - 1).astype(F32)
    down = jnp.where(row == 0, prev, pltpu.roll(u, 1, 0))
    up = jnp.where(row == tt - 1, nxt, pltpu.roll(u, tt - 1, 0))
    return down, up


def conv_swiglu_fwd(u, cw, cb, name):
    T, F2 = u.shape
    F = F2 // 2
    tt = min(512, T)
    tc = _pick(F, (512, 256, 128))
    nt_, ncol = T // tt, F // tc
    H = CONV_HALO
    n = tt + 2 * H

    def body(ua, uap, uan, ug, ugp, ugn, wa, wg, ba, bg, f_ref, ft_ref):
        i = pl.program_id(1)
        keep_p = (i > 0).astype(F32)
        keep_n = (i < nt_ - 1).astype(F32)
        res = []
        for m, p, nx, w, b in ((ua, uap, uan, wa, ba), (ug, ugp, ugn, wg, bg)):
            x = jnp.concatenate([p[...] * keep_p, m[...], nx[...] * keep_n], axis=0)
            down, up = pltpu.roll(x, 1, 0)[H:H + tt], pltpu.roll(x, n - 1, 0)[H:H + tt]
            res.append(w[0] * down + w[1] * x[H:H + tt] + w[2] * up + b[...])
        a, g = res
        f = (a * _sig_tanh(a) * g).astype(f_ref.dtype)
        f_ref[...] = f
        ft_ref[...] = f.T

    wspec = lambda off: pl.BlockSpec((3, 1, tc), lambda j, i: (0, 0, j + off))
    bspec = lambda off: pl.BlockSpec((1, tc), lambda j, i: (0, j + off))
    return pl.pallas_call(
        body,
        grid=(ncol, nt_),
        in_specs=conv_halo_specs(T, tt, tc, 0) + conv_halo_specs(T, tt, tc, ncol)
        + [wspec(0), wspec(ncol), bspec(0), bspec(ncol)],
        out_specs=[pl.BlockSpec((tt, tc), lambda j, i: (i, j)), pl.BlockSpec((tc, tt), lambda j, i: (j, i))],
        out_shape=[jax.ShapeDtypeStruct((T, F), BF16), jax.ShapeDtypeStruct((F, T), BF16)],
        compiler_params=_cp(("parallel", "parallel")),
        name=name,
    )(u, u, u, u, u, u, cw, cw, cb, cb)


def conv_swiglu_bwd(u, cw, cb, df, name):
    T, F2 = u.shape
    F = F2 // 2
    tt = min(256, T)
    tc = _pick(F, (512, 256, 128))
    nt_, ncol = T // tt, F // tc

    def body(ua, uap, uan, ug, ugp, ugn, wa, wg, ba, bg, df_ref, da_ref, dg_ref, dwa, dwg, dba, dbg):
        i = pl.program_id(1)

        @pl.when(i == 0)
        def _():
            for r in (dwa, dwg, dba, dbg):
                r[...] = jnp.zeros_like(r)

        sh = []
        res = []
        for um, up_, un, w, b in ((ua, uap, uan, wa, ba), (ug, ugp, ugn, wg, bg)):
            x = um[...]
            down, up = _shifted(x, up_[...], un[...], i, nt_)
            sh.append((down, x, up))
            res.append(w[0] * down + w[1] * x + w[2] * up + b[...])
        a, g = res
        d = df_ref[...].astype(F32)
        sg = _sig(a)
        da = d * g * sg * (1.0 + a * (1.0 - sg))
        dg = d * a * sg
        da_ref[...] = da
        dg_ref[...] = dg
        for dd, (down, x, up), dw, db in ((da, sh[0], dwa, dba), (dg, sh[1], dwg, dbg)):
            dw[0] += jnp.sum(dd * down, axis=0, keepdims=True)
            dw[1] += jnp.sum(dd * x, axis=0, keepdims=True)
            dw[2] += jnp.sum(dd * up, axis=0, keepdims=True)
            db[...] += jnp.sum(dd, axis=0, keepdims=True)

    wspec = lambda off: pl.BlockSpec((3, 1, tc), lambda j, i: (0, 0, j + off))
    bspec = lambda off: pl.BlockSpec((1, tc), lambda j, i: (0, j + off))
    tile = pl.BlockSpec((tt, tc), lambda j, i: (i, j))
    return pl.pallas_call(
        body,
        grid=(ncol, nt_),
        in_specs=conv_specs(T, tt, tc, 0, "ji") + conv_specs(T, tt, tc, ncol, "ji")
        + [wspec(0), wspec(ncol), bspec(0), bspec(ncol), tile],
        out_specs=[tile, tile, wspec(0), wspec(0), bspec(0), bspec(0)],
        out_shape=[
            jax.ShapeDtypeStruct((T, F), F32), jax.ShapeDtypeStruct((T, F), F32),
            jax.ShapeDtypeStruct((3, 1, F), F32), jax.ShapeDtypeStruct((3, 1, F), F32),
            jax.ShapeDtypeStruct((1, F), F32), jax.ShapeDtypeStruct((1, F), F32),
        ],
        compiler_params=_cp(("parallel", "arbitrary")),
        name=name,
    )(u, u, u, u, u, u, cw, cw, cb, cb, df)


CONV_HALO = 16


def conv_halo_specs(T, tt, tc, off):
    r = tt // CONV_HALO
    last = T // CONV_HALO - 1
    return [
        pl.BlockSpec((tt, tc), lambda j, i: (i, j + off)),
        pl.BlockSpec((CONV_HALO, tc), lambda j, i: (jnp.maximum(i * r - 1, 0), j + off)),
        pl.BlockSpec((CONV_HALO, tc), lambda j, i: (jnp.minimum((i + 1) * r, last), j + off)),
    ]


def conv_swiglu_bwd_fused(u, cw, cb, df, name):
    T, F2 = u.shape
    F = F2 // 2
    tt = min(512, T)
    tc = _pick(F, (512, 256, 128))
    nt_, ncol = T // tt, F // tc
    H = CONV_HALO
    n = tt + 2 * H

    def body(ua, uap, uan, ug, ugp, ugn, dm, dp_, dn, wa, wg, ba, bg, dua_ref, dug_ref, dwa, dwg, dba, dbg):
        i = pl.program_id(1)

        @pl.when(i == 0)
        def _():
            for r in (dwa, dwg, dba, dbg):
                r[...] = jnp.zeros_like(r)

        keep_p = (i > 0).astype(F32)
        keep_n = (i < nt_ - 1).astype(F32)

        def ext(m, p, nx):
            return jnp.concatenate([p[...].astype(F32) * keep_p, m[...].astype(F32), nx[...].astype(F32) * keep_n],
                                   axis=0)

        d = ext(dm, dp_, dn)
        conv, parts = [], []
        for m, p, nx, w, b in ((ua, uap, uan, wa, ba), (ug, ugp, ugn, wg, bg)):
            x = ext(m, p, nx)
            down, up = pltpu.roll(x, 1, 0), pltpu.roll(x, n - 1, 0)
            parts.append((down, x, up))
            conv.append(w[0] * down + w[1] * x + w[2] * up + b[...])
        a, g = conv
        sg = _sig_tanh(a)
        da = d * g * sg * (1.0 + a * (1.0 - sg))
        dg = d * a * sg
        for dd, w, (down, x, up), o_ref, dw, db in ((da, wa, parts[0], dua_ref, dwa, dba),
                                                    (dg, wg, parts[1], dug_ref, dwg, dbg)):
            du = w[0] * pltpu.roll(dd, n - 1, 0) + w[1] * dd + w[2] * pltpu.roll(dd, 1, 0)
            o_ref[...] = du[H:H + tt].astype(o_ref.dtype)
            ddm = dd[H:H + tt]
            dw[0] += jnp.sum(ddm * down[H:H + tt], axis=0, keepdims=True)
            dw[1] += jnp.sum(ddm * x[H:H + tt], axis=0, keepdims=True)
            dw[2] += jnp.sum(ddm * up[H:H + tt], axis=0, keepdims=True)
            db[...] += jnp.sum(ddm, axis=0, keepdims=True)

    wspec = lambda off: pl.BlockSpec((3, 1, tc), lambda j, i: (0, 0, j + off))
    bspec = lambda off: pl.BlockSpec((1, tc), lambda j, i: (0, j + off))
    tile = pl.BlockSpec((tt, tc), lambda j, i: (i, j))
    return pl.pallas_call(
        body,
        grid=(ncol, nt_),
        in_specs=conv_halo_specs(T, tt, tc, 0) + conv_halo_specs(T, tt, tc, ncol) + conv_halo_specs(T, tt, tc, 0)
        + [wspec(0), wspec(ncol), bspec(0), bspec(ncol)],
        out_specs=[tile, tile, wspec(0), wspec(0), bspec(0), bspec(0)],
        out_shape=[
            jax.ShapeDtypeStruct((T, F), BF16), jax.ShapeDtypeStruct((T, F), BF16),
            jax.ShapeDtypeStruct((3, 1, F), F32), jax.ShapeDtypeStruct((3, 1, F), F32),
            jax.ShapeDtypeStruct((1, F), F32), jax.ShapeDtypeStruct((1, F), F32),
        ],
        compiler_params=_cp(("parallel", "arbitrary")),
        name=name,
    )(u, u, u, u, u, u, df, df, df, cw, cw, cb, cb)


def conv_transpose(d, cw, off, name):
    T, F = d.shape
    tt = min(256, T)
    tc = _pick(F, (512, 256, 128))
    nt_, ncol = T // tt, F // tc
    offb = off // tc

    def body(dm, dp_, dn, w, o_ref):
        i = pl.program_id(0)
        x = dm[...]
        down, up = _shifted(x, dp_[...], dn[...], i, nt_)
        o_ref[...] = (w[0] * up + w[1] * x + w[2] * down).astype(o_ref.dtype)

    return pl.pallas_call(
        body,
        grid=(nt_, ncol),
        in_specs=conv_specs(T, tt, tc, 0, "ij") + [pl.BlockSpec((3, 1, tc), lambda i, j: (0, 0, j + offb))],
        out_specs=pl.BlockSpec((tt, tc), lambda i, j: (i, j)),
        out_shape=jax.ShapeDtypeStruct((T, F), BF16),
        compiler_params=_cp(("parallel", "parallel")),
        name=name,
    )(d, d, d, cw)


def rope_tables(n, hd):
    rows = n // GRID_W
    row = jnp.repeat(jnp.arange(rows), GRID_W)
    col = jnp.tile(jnp.arange(GRID_W), rows)
    n_freq = hd // 4
    inv = ROPE_THETA ** (-jnp.arange(n_freq, dtype=F32) / n_freq)
    ang = jnp.concatenate([row[:, None] * inv, col[:, None] * inv], axis=-1)
    cos, sin = jnp.cos(ang), jnp.sin(ang)
    return jnp.concatenate([cos, cos], axis=-1), jnp.concatenate([-sin, sin], axis=-1)


def local_step(x, ctx, tgt, mod, modc, W, P):
    T, D = x.shape
    L = ctx.shape[0]
    hd, hq, hkv, gh = P["hd"], P["hq"], P["hkv"], P["gh"]
    sh1, sc1, g1, sh2, sc2, g2 = mod
    csh1, csc1 = modc
    kvw = hkv * hd
    gkw = W["gqk"].shape[1] // 2
    gdv = D // gh
    gdk = gkw // gh

    h, ht = modulate_fwd(x, P["g_mix"], sh1, sc1, "mod1", with_t=True)
    hc = modulate_fwd(ctx, P["g_mix"], csh1, csc1, "mod1_ctx")[0]
    pq = matmul(h, W["q"], "nn", F32, "proj_q")
    pkv = matmul(h, W["kv"], "nn", F32, "proj_kv")
    pgqk = matmul(h, W["gqk"], "nn", F32, "proj_gqk")
    pgv = matmul(h, W["gv"], "nn", F32, "proj_gv")
    prb = matmul(h, W["rb"], "nn", F32, "proj_rb")
    plr = matmul(h, W["lr"], "nn", F32, "proj_lr")
    pgab = matmul(h, W["gab"], "nn", F32, "proj_gab")
    pkv_c = matmul(hc, W["kv"], "nn", F32, "proj_kv_ctx")
    pgqk_c = matmul(hc, W["gqk"], "nn", F32, "proj_gqk_ctx")
    pgv_c = matmul(hc, W["gv"], "nn", F32, "proj_gv_ctx")
    plr_c = matmul(hc, W["lr"], "nn", F32, "proj_lr_ctx")

    cosf, sinf = rope_tables(T, hd)
    one_c, zero_c = jnp.ones((L, hd), F32), jnp.zeros((L, hd), F32)
    qr = norm_rope_fwd(pq, hq * hd, 0, P["q_norm"], cosf, sinf, hd, "qnorm")
    kr = norm_rope_fwd(pkv, kvw, 0, P["k_norm"], cosf, sinf, hd, "knorm")
    kcr = norm_rope_fwd(pkv_c, kvw, 0, P["k_norm"], one_c, zero_c, hd, "knorm_ctx")
    sink = P["attn_sink"].reshape(hq, 1, 1)
    o_attn, lse, lse_row = attention_fwd(qr, kr, pkv, kcr, pkv_c, sink, hkv, hd, "attn_fwd")

    gf, gb = gate_fwd(plr, W["gate_f"], W["gate_b"], P["b_gate_f"], P["b_gate_b"], "gates")
    gfc, gbc = gate_fwd(plr_c, W["gate_f"], W["gate_b"], P["b_gate_f"], P["b_gate_b"], "gates_ctx")
    zero_state = jnp.zeros((gh, gdv, gdk), F32)
    _, st_cf, s_cf = gla_fwd(pgqk_c, pgv_c, gfc, zero_state, gh, False, "gla_ctx_f")
    _, st_cb, s_cb = gla_fwd(pgqk_c, pgv_c, gbc, zero_state, gh, True, "gla_ctx_b")
    of, st_f, _ = gla_fwd(pgqk, pgv, gf, s_cf, gh, False, "gla_f")
    og, st_b, _ = gla_fwd(pgqk, pgv, gb, s_cb, gh, True, "gla_b", o_add=of)
    ug = gla_out_fwd(og, prb, P["gla_norm"], gh, "gla_out")

    ya = matmul(o_attn, W["attn_o"], "nn", F32, "attn_o")
    yg = matmul(ug, W["gla_o"], "nn", F32, "gla_o")

    def merge_fn(ya, yg, ga, gb_):
        return _sig(ga) * ya + _sig(gb_) * yg

    z = rowwise(merge_fn, [ya, yg, (pgab, D, 0), (pgab, D, 1)], [], [(D, BF16)], [], 256, "merge")[0]
    mo = matmul(z, W["out"], "nn", F32, "w_out")

    def res_fn(x, mo, g1, gffn, sh2, sc2):
        x1 = x + g1 * mo
        h2 = x1 * _rstd(x1) * gffn * (1.0 + sc2) + sh2
        return x1, h2, h2

    x1, h2, h2t = rowwise(res_fn, [x, mo], [g1, P["g_ffn"], sh2, sc2], [(D, F32), (D, BF16)], [], 256, "res_mod2",
                          out_t=[(D, BF16)])
    u = matmul(h2, W["up"], "nn", F32, "w_up")
    cw3 = W["conv_w"].reshape(3, 1, -1)
    f, ft = conv_swiglu_fwd(u, cw3, P["conv_b"], "conv_swiglu")
    fo = matmul(f, W["down"], "nn", F32, "w_down")

    def final_fn(x1, fo, tgt, g2):
        e = x1 + g2 * fo - tgt
        dy = e * (1.0 / D)
        lsum = jnp.sum(jnp.sum(e * e, axis=1, keepdims=True), axis=0, keepdims=True)
        return dy, dy * g2, jnp.broadcast_to(lsum, (1, 128)), jnp.sum(dy * fo, axis=0, keepdims=True)

    dy, dfo, lsum, dg2 = rowwise(final_fn, [x1, fo, tgt], [g2], [(D, F32), (D, BF16)], [(1, 128), (1, D)], 256, "loss")
    df = matmul(dfo, W["down"], "nt", BF16, "d_f")
    dw_down = matmul(ft, dfo, "nn", BF16, "dw_down")
    du_a, du_g, dcw_a, dcw_g, dcb_a, dcb_g = conv_swiglu_bwd_fused(u, cw3, P["conv_b"], df, "conv_swiglu_bwd")
    Fh = du_a.shape[1]
    dh2 = matmul(du_a, W["up"][:, :Fh], "nt", F32, "d_h2_a")
    dh2 = matmul(du_g, W["up"][:, Fh:], "nt", F32, "d_h2_g", add=dh2)
    dw_up = [matmul(h2t, du_a, "nn", BF16, "dw_up_a"), matmul(h2t, du_g, "nn", BF16, "dw_up_g")]

    def mod2_bwd_fn(x1, dh, dy, mo, gffn, sc2, g1):
        r = _rstd(x1)
        n = x1 * r
        dyy = dh * (1.0 + sc2)
        dn = dyy * gffn
        dx1 = dy + r * (dn - n * jnp.mean(dn * n, axis=-1, keepdims=True))
        s0 = lambda a: jnp.sum(a, axis=0, keepdims=True)
        return dx1, dx1 * g1, s0(dyy * n), s0(dh), s0(dh * n * gffn), s0(dx1 * mo)

    dx1, dmo, dg_ffn, dsh2, dsc2, dg1 = rowwise(
        mod2_bwd_fn, [x1, dh2, dy, mo], [P["g_ffn"], sc2, g1], [(D, F32), (D, BF16)], [(1, D)] * 4, 128, "mod2_bwd")
    dz = matmul(dmo, W["out"], "nt", F32, "d_z")
    dw_out = matmul(z, dmo, "tn", BF16, "dw_out")

    def merge_bwd_fn(dz, ya, yg, ga, gb_):
        sa, sb = _sig(ga), _sig(gb_)
        return dz * sa, dz * sb, jnp.concatenate([dz * ya * sa * (1.0 - sa), dz * yg * sb * (1.0 - sb)], axis=1)

    dya, dyg, dpgab = rowwise(merge_bwd_fn, [dz, ya, yg, (pgab, D, 0), (pgab, D, 1)], [],
                              [(D, BF16), (D, BF16), (2 * D, BF16)], [], 128, "merge_bwd")
    do_attn = matmul(dya, W["attn_o"], "nt", BF16, "d_oattn")
    dw_attn_o = matmul(o_attn, dya, "tn", BF16, "dw_attn_o")
    dug = matmul(dyg, W["gla_o"], "nt", F32, "d_ug")
    dw_gla_o = matmul(ug, dyg, "tn", BF16, "dw_gla_o")
    dog, dprb, dgn = gla_out_bwd(og, prb, dug, P["gla_norm"], gh, "gla_out_bwd")

    dq1, dk1, dv1, dgf, ds_cf = gla_bwd(pgqk, pgv, gf, st_f, dog, zero_state, gh, False, "gla_f_bwd")
    dgq, dgk, dpgv, dgb, ds_cb = gla_bwd(pgqk, pgv, gb, st_b, dog, zero_state, gh, True, "gla_b_bwd",
                                          acc=(dq1, dk1, dv1))
    dpgqk = jnp.concatenate([dgq, dgk], axis=1)
    zero_do = jnp.zeros((L, gh * gdv), F32)
    cq1, ck1, cv1, dgfc, _ = gla_bwd(pgqk_c, pgv_c, gfc, st_cf, zero_do, ds_cf, gh, False, "gla_ctx_f_bwd")
    cq, ck, dpgv_c, dgbc, _ = gla_bwd(pgqk_c, pgv_c, gbc, st_cb, zero_do, ds_cb, gh, True, "gla_ctx_b_bwd",
                                      acc=(cq1, ck1, cv1))
    dpgqk_c = jnp.concatenate([cq, ck], axis=1)
    dplr, dwgf, dbgf, dwgb, dbgb = gate_bwd(plr, dgf, dgb, W["gate_f"], W["gate_b"], P["b_gate_f"], P["b_gate_b"], "gates_bwd")
    dplr_c, dwgf_c, dbgf_c, dwgb_c, dbgb_c = gate_bwd(plr_c, dgfc, dgbc, W["gate_f"], W["gate_b"], P["b_gate_f"],
                                                      P["b_gate_b"], "gates_ctx_bwd")

    dqr, dkc_r, dvc, dsink, dr_row = attention_bwd_q(qr, kr, pkv, kcr, pkv_c, sink, do_attn, o_attn, lse, hkv, hd,
                                                     "attn_bwd_q")
    dkr, dv = attention_bwd_kv(qr, kr, pkv, do_attn, lse_row, dr_row, hkv, hd, "attn_bwd_kv")
    dpq, dqn = norm_rope_bwd(pq, hq * hd, 0, dqr, P["q_norm"], cosf, sinf, hd, "qnorm_bwd")
    dpk, dkn = norm_rope_bwd(pkv, kvw, 0, dkr, P["k_norm"], cosf, sinf, hd, "knorm_bwd")
    dpk_c, dkn_c = norm_rope_bwd(pkv_c, kvw, 0, dkc_r, P["k_norm"], one_c, zero_c, hd, "knorm_ctx_bwd")
    dpkv = jnp.concatenate([dpk, dv], axis=1)
    dpkv_c = jnp.concatenate([dpk_c, dvc.astype(BF16)], axis=1)

    dw_q = matmul(ht, dpq, "nn", BF16, "dw_q")
    dw_kv = matmul(ht, dpkv, "nn", BF16, "dw_kv", add=matmul(hc, dpkv_c, "tn", F32, "dw_kv_ctx"))
    dw_gqk = matmul(ht, dpgqk, "nn", BF16, "dw_gqk", add=matmul(hc, dpgqk_c, "tn", F32, "dw_gqk_ctx"))
    dw_gv = matmul(ht, dpgv, "nn", BF16, "dw_gv", add=matmul(hc, dpgv_c, "tn", F32, "dw_gv_ctx"))
    dw_rb = matmul(ht, dprb, "nn", BF16, "dw_rb")
    dw_lr = matmul(ht, dplr, "nn", BF16, "dw_lr", add=matmul(hc, dplr_c, "tn", F32, "dw_lr_ctx"))
    dw_gab = matmul(ht, dpgab, "nn", BF16, "dw_gab")
    lrw = P["lowrank"]
    dw_in = [dw_q, dw_kv, dw_gqk, dw_gv, dw_rb, dw_lr[:, :2 * lrw], dw_gab]

    dh = matmul(dpq, W["q"], "nt", F32, "dh_q")
    dh = matmul(dpkv, W["kv"], "nt", F32, "dh_kv", add=dh)
    dh = matmul(dpgqk, W["gqk"], "nt", F32, "dh_gqk", add=dh)
    dh = matmul(dpgv, W["gv"], "nt", F32, "dh_gv", add=dh)
    dh = matmul(dprb, W["rb"], "nt", F32, "dh_rb", add=dh)
    dh = matmul(dplr, W["lr"], "nt", F32, "dh_lr", add=dh)
    dh = matmul(dpgab, W["gab"], "nt", F32, "dh_gab", add=dh)
    dhc = matmul(dpkv_c, W["kv"], "nt", F32, "dhc_kv")
    dhc = matmul(dpgqk_c, W["gqk"], "nt", F32, "dhc_gqk", add=dhc)
    dhc = matmul(dpgv_c, W["gv"], "nt", F32, "dhc_gv", add=dhc)
    dhc = matmul(dplr_c, W["lr"], "nt", F32, "dhc_lr", add=dhc)

    def mod1_bwd_fn(x, dh, dres, g, sc):
        r = _rstd(x)
        n = x * r
        dyy = dh * (1.0 + sc)
        dn = dyy * g
        dx = dres + r * (dn - n * jnp.mean(dn * n, axis=-1, keepdims=True))
        s0 = lambda a: jnp.sum(a, axis=0, keepdims=True)
        return dx, s0(dyy * n), s0(dh), s0(dh * n * g)

    grad_x, dgmix, dsh1, dsc1 = rowwise(mod1_bwd_fn, [x, dh, dx1], [P["g_mix"], sc1], [(D, F32)], [(1, D)] * 3,
                                        128, "mod1_bwd")
    _, dgmix_c, dcsh1, dcsc1 = rowwise(mod1_bwd_fn, [ctx, dhc, jnp.zeros_like(ctx)], [P["g_mix"], csc1], [(D, F32)],
                                       [(1, D)] * 3, 128, "mod1_ctx_bwd")

    zD = jnp.zeros((1, D), F32)
    grads = dict(
        w_in=dw_in, w_attn_o=dw_attn_o, w_gla_o=dw_gla_o, w_out=dw_out, w_up=dw_up, w_down=dw_down,
        dmod_x=jnp.concatenate([dsh1, dsc1, dg1, dsh2, dsc2, dg2], axis=1),
        dmod_c=jnp.concatenate([dcsh1, dcsc1, zD, zD, zD, zD], axis=1),
        g_mix=dgmix + dgmix_c, q_norm=dqn, k_norm=dkn + dkn_c, attn_sink=dsink.reshape(1, hq),
        w_gate_f=(dwgf + dwgf_c)[:lrw], b_gate_f=dbgf + dbgf_c,
        w_gate_b=(dwgb + dwgb_c)[lrw:2 * lrw], b_gate_b=dbgb + dbgb_c,
        gla_norm=dgn, g_ffn=dg_ffn,
        conv_w=jnp.concatenate([dcw_a, dcw_g], axis=2).reshape(3, -1),
        conv_b=jnp.concatenate([dcb_a, dcb_g], axis=1),
    )
    return lsum[0, 0], grad_x, grads


SMALL_REPL = ("c_ctx", "b_mod", "g_mix", "q_norm", "k_norm", "attn_sink", "b_gate_f", "b_gate_b", "gla_norm", "g_ffn",
              "conv_b")
SMALL_SHARD = ("w_gate_f", "w_gate_b", "conv_w")
ORDER = ("c_ctx", "w_mod", "b_mod", "g_mix", "w_in", "q_norm", "k_norm", "attn_sink", "w_gate_f", "b_gate_f",
         "w_gate_b", "b_gate_b", "gla_norm", "w_attn_o", "w_gla_o", "w_out", "g_ffn", "w_up", "conv_w", "conv_b",
         "w_down")


def kernel(x, c, ctx, c_ctx, w_mod, b_mod, g_mix, w_in, q_norm, k_norm, attn_sink, w_gate_f, b_gate_f, w_gate_b, b_gate_b, gla_norm, w_attn_o, w_gla_o, w_out, g_ffn, w_up, conv_w, conv_b, w_down, loss_target, m_c_ctx, m_w_mod, m_b_mod, m_g_mix, m_w_in, m_q_norm, m_k_norm, m_attn_sink, m_w_gate_f, m_b_gate_f, m_w_gate_b, m_b_gate_b, m_gla_norm, m_w_attn_o, m_w_gla_o, m_w_out, m_g_ffn, m_w_up, m_conv_w, m_conv_b, m_w_down, v_c_ctx, v_w_mod, v_b_mod, v_g_mix, v_w_in, v_q_norm, v_k_norm, v_attn_sink, v_w_gate_f, v_b_gate_f, v_w_gate_b, v_b_gate_b, v_gla_norm, v_w_attn_o, v_w_gla_o, v_w_out, v_g_ffn, v_w_up, v_conv_w, v_conv_b, v_w_down):
    loc = dict(locals())
    Wt = {n: loc[n] for n in ORDER}
    Mt = {n: loc["m_" + n] for n in ORDER}
    Vt = {n: loc["v_" + n] for n in ORDER}
    me = 4 * lax.axis_index("x") + 2 * lax.axis_index("y") + lax.axis_index("c")

    D = x.shape[-1]
    hd = q_norm.shape[-1]
    hq = attn_sink.shape[-1]
    gdv = gla_norm.shape[-1]
    gh = D // gdv
    gdk = D // 2 // gh
    lrw = w_gate_f.shape[1]
    in_w = NDEV * w_in.shape[-1]
    kvw = (in_w - hq * hd - 2 * gh * gdk - 2 * gh * gdv - 2 * lrw - 2 * D) // 2
    hkv = kvw // hd
    gcols = w_gate_f.shape[-1]
    F2 = NDEV * w_up.shape[-1]
    mcols = w_mod.shape[-1]

    x2, ctx2, tgt2 = x[0], ctx[0], loss_target[0]

    c_all = exchange([jnp.pad(c, ((0, 7), (0, 0)))], True, "gather_c")[0][:, 0, :]
    c9 = jnp.concatenate([c_all, c_ctx[None, :], jnp.zeros((7, D), F32)], axis=0)
    s9 = rowwise(lambda a: a * _sig(a), [c9], [], [(D, F32)], [], 16, "silu_c")[0]
    bias = jnp.broadcast_to(lax.dynamic_slice_in_dim(b_mod, me * mcols, mcols, axis=1), (16, mcols))
    mod_cols = matmul(s9, w_mod[0], "nn", F32, "mod_cols", add=bias)
    mod_all = exchange([mod_cols], True, "gather_mod")[0]
    mod_all = jnp.transpose(mod_all, (1, 0, 2)).reshape(16, NDEV * mcols)
    mod_me = lax.dynamic_slice_in_dim(mod_all, me, 1, axis=0)
    mod = [mod_me[:, i * D:(i + 1) * D] for i in range(6)]
    modc = [mod_all[8:9, i * D:(i + 1) * D] for i in range(2)]

    o3 = jnp.stack([w_attn_o[0], w_gla_o[0], w_out[0]]).astype(BF16)
    small_w = pack([w_gate_f[0], w_gate_b[0], conv_w[0]])
    g_in, g_o3, g_up, g_down, g_small = gather_two_level(
        [w_in[0].astype(BF16), o3, w_up[0].astype(BF16), w_down[0].astype(BF16), small_w], "gather_w")
    seg = segments_from_blocks(g_in, [hq * hd, 2 * kvw, 2 * gh * gdk, gh * gdv, gh * gdv, 2 * lrw, 2 * D])
    small_parts = [unpack(g_small[j], [w_gate_f[0].shape, w_gate_b[0].shape, conv_w[0].shape]) for j in range(NDEV)]
    wgf = jnp.concatenate([p[0] for p in small_parts], axis=1)
    wgb = jnp.concatenate([p[1] for p in small_parts], axis=1)
    cw_full = jnp.concatenate([p[2] for p in small_parts], axis=1)
    o3f = [g_o3[:, i].reshape(-1, D) for i in range(3)]
    W = dict(
        q=seg[0], kv=seg[1], gqk=seg[2], gv=seg[3], rb=seg[4],
        lr=jnp.pad(seg[5], ((0, 0), (0, 128 - 2 * lrw))), gab=seg[6],
        gate_f=jnp.pad(wgf, ((0, 128 - lrw), (0, 0))),
        gate_b=jnp.pad(wgb, ((lrw, 128 - 2 * lrw), (0, 0))),
        attn_o=o3f[0], gla_o=o3f[1], out=o3f[2],
        up=jnp.concatenate([g_up[j] for j in range(NDEV)], axis=1),
        down=g_down.reshape(-1, D),
        conv_w=cw_full,
    )
    P = dict(hd=hd, hq=hq, hkv=hkv, gh=gh, lowrank=lrw, g_mix=g_mix, q_norm=q_norm, k_norm=k_norm, attn_sink=attn_sink,
             b_gate_f=b_gate_f, b_gate_b=b_gate_b, gla_norm=gla_norm, g_ffn=g_ffn, conv_b=conv_b)

    lsum, grad_x, G = local_step(x2, ctx2, tgt2, mod, modc, W, P)
    loss = lax.psum(0.5 * lsum / D, ("x", "y", "c"))

    dm = exchange([jnp.concatenate([G["dmod_x"], G["dmod_c"], jnp.zeros((6, 6 * D), F32)], axis=0)], True,
                  "gather_dmod")[0]
    dmc = reduce_parts(dm[:, 1:2, :].reshape(NDEV, 6 * D // 128, 128), "sum_dmod_ctx").reshape(1, 6 * D)
    dM = jnp.concatenate([dm[:, 0, :], dmc, jnp.zeros((7, 6 * D), F32)], axis=0)
    dM_cols = lax.dynamic_slice_in_dim(dM, me * mcols, mcols, axis=1)
    g_w_mod = matmul(s9, dM_cols, "tn", F32, "dw_mod")
    g_b_mod = reduce_parts(dM.reshape(16, 6 * D // 128, 128), "sum_db_mod").reshape(1, 6 * D)
    dsc = matmul(dM_cols[8:16], w_mod[0], "nt", F32, "d_silu_ctx")
    cc = jnp.broadcast_to(c_ctx[None, :], (8, D))

    def dsilu_fn(d, a):
        sg = _sig(a)
        return d * sg * (1.0 + a * (1.0 - sg))

    g_cctx_part = rowwise(dsilu_fn, [dsc, cc], [], [(D, F32)], [], 8, "d_c_ctx")[0][0:1]

    small_names = ("c_ctx", "g_mix", "q_norm", "k_norm", "attn_sink", "b_gate_f", "b_gate_b", "gla_norm", "g_ffn",
                   "conv_b", "w_gate_f", "w_gate_b", "conv_w")
    G["c_ctx"] = g_cctx_part
    sm_shapes = [G[n].shape for n in small_names]
    sm_all = exchange([pack([G[n] for n in small_names])], True, "gather_small_grads")[0]
    sm_tot = unpack(reduce_parts(sm_all, "sum_small_grads"), sm_shapes)
    gs = dict(zip(small_names, sm_tot))
    gs["b_mod"] = g_b_mod
    gs["w_gate_f"] = lax.dynamic_slice_in_dim(gs["w_gate_f"], me * gcols, gcols, axis=1)
    gs["w_gate_b"] = lax.dynamic_slice_in_dim(gs["w_gate_b"], me * gcols, gcols, axis=1)
    ccols = conv_w.shape[-1]
    gs["conv_w"] = lax.dynamic_slice_in_dim(gs["conv_w"], me * ccols, ccols, axis=1)

    orows = w_attn_o.shape[1]
    s_in = blocks_from_segments(G["w_in"], w_in.shape[-1])
    s_o3 = jnp.concatenate([rows_to_blocks(G["w_attn_o"]), rows_to_blocks(G["w_gla_o"]), rows_to_blocks(G["w_out"])],
                           axis=1)
    s_up = blocks_from_segments(G["w_up"], w_up.shape[-1])
    s_down = rows_to_blocks(G["w_down"])
    r_in, r_o3, r_up, r_down = scatter_reduce([s_in, s_o3, s_up, s_down], "scatter_grads")

    out = {}
    out["w_in"] = adam_reduce(r_in, w_in[0], m_w_in[0], v_w_in[0], "adam_w_in")
    o3w = jnp.concatenate([w_attn_o[0], w_gla_o[0], w_out[0]], axis=0)
    o3m = jnp.concatenate([m_w_attn_o[0], m_w_gla_o[0], m_w_out[0]], axis=0)
    o3v = jnp.concatenate([v_w_attn_o[0], v_w_gla_o[0], v_w_out[0]], axis=0)
    ro3 = adam_reduce(r_o3, o3w, o3m, o3v, "adam_o3")
    for i, n in enumerate(("w_attn_o", "w_gla_o", "w_out")):
        out[n] = [a[i * orows:(i + 1) * orows] for a in ro3]
    out["w_up"] = adam_reduce(r_up, w_up[0], m_w_up[0], v_w_up[0], "adam_w_up")
    out["w_down"] = adam_reduce(r_down, w_down[0], m_w_down[0], v_w_down[0], "adam_w_down")
    out["w_mod"] = adam_reduce(g_w_mod[None], w_mod[0], m_w_mod[0], v_w_mod[0], "adam_w_mod")
    sm_names = SMALL_REPL + SMALL_SHARD
    shapes = [Wt[n].shape for n in sm_names]
    rs = adam_reduce(pack([gs[n] for n in sm_names])[None], pack([Wt[n] for n in sm_names]),
                     pack([Mt[n] for n in sm_names]), pack([Vt[n] for n in sm_names]), "adam_small")
    rs = [unpack(a, shapes) for a in rs]
    for i, n in enumerate(sm_names):
        out[n] = [a[i] for a in rs]

    res = [loss, grad_x[None]]
    for k in range(4):
        for n in ORDER:
            res.append(out[n][k].reshape(Wt[n].shape))
    return tuple(res)
```

```python
import jax
import jax.numpy as jnp
import numpy as np
from jax import lax
from jax.experimental import pallas as pl
from jax.experimental.pallas import tpu as pltpu

F32 = jnp.float32
BF16 = jnp.bfloat16

NDEV = 8
NCHIP = 4
EPS = 1e-6
WINDOW = 128
BLOCK = 128
GRID_W = 64
ROPE_THETA = 10000.0
GLA_CHUNK = 128
GLA_GATE_NORM = 16.0
ADAM_LR = 0.001
ADAM_B1 = 0.9
ADAM_B2 = 0.999
ADAM_EPS = 1e-08
ADAM_WD = 0.01
ADAM_STEP = 10
V7X_VMEM_LIMIT = 56 * 1024 * 1024
MATMUL_VMEM_BUDGET = 40 * 1024 * 1024
MATMUL_TILES = (1024, 1408, 512, 256, 128)
NEG = -1e30

NN = ((1,), (0,))
NT = ((1,), (1,))
TN = ((0,), (0,))


def _dot(a, b, dims):
    return lax.dot_general(a, b, (dims, ((), ())), preferred_element_type=F32)


def _cp(sem):
    return pltpu.CompilerParams(dimension_semantics=sem, vmem_limit_bytes=V7X_VMEM_LIMIT)


def _pick(n, cands):
    for c in cands:
        if n % c == 0:
            return c
    return n


def _sig(x):
    return 1.0 / (1.0 + jnp.exp(-x))


def _sig_tanh(x):
    return 0.5 * jnp.tanh(0.5 * x) + 0.5


def _rstd(x):
    return lax.rsqrt(jnp.mean(x * x, axis=-1, keepdims=True) + EPS)


_ANY = pl.BlockSpec(memory_space=pl.ANY)


def _place():
    return lax.axis_index("x"), lax.axis_index("y"), lax.axis_index("c")


def exchange(srcs, bcast, name, group="all"):
    n = len(srcs)
    ndev = NDEV if group == "all" else NCHIP
    ks = tuple(range(1, NDEV)) if group == "all" else (2, 4, 6)
    out_shape = [jax.ShapeDtypeStruct((ndev,) + (s.shape if bcast else s.shape[1:]), s.dtype) for s in srcs]

    def body(*refs):
        src, dst = refs[:n], refs[n:2 * n]
        send_sems, recv_sems, loc_sems = refs[2 * n:]
        x, y, c = _place()

        def idx(px, py, pc):
            return 4 * px + 2 * py + pc if group == "all" else 2 * px + py

        me = idx(x, y, c)
        copies = []
        for a in range(n):
            cp = pltpu.make_async_copy(src[a] if bcast else src[a].at[me], dst[a].at[me], loc_sems.at[a])
            cp.start()
            copies.append(cp)
        for s, k in enumerate(ks):
            px, py, pc = x ^ ((k >> 2) & 1), y ^ ((k >> 1) & 1), c ^ (k & 1)
            for a in range(n):
                cp = pltpu.make_async_remote_copy(
                    src_ref=src[a] if bcast else src[a].at[idx(px, py, pc)],
                    dst_ref=dst[a].at[me],
                    send_sem=send_sems.at[a, s],
                    recv_sem=recv_sems.at[a, s],
                    device_id=(px, py, pc),
                    device_id_type=pl.DeviceIdType.MESH,
                )
                cp.start()
                copies.append(cp)
        for cp in copies:
            cp.wait()

    return pl.pallas_call(
        body,
        out_shape=out_shape,
        in_specs=[_ANY] * n,
        out_specs=[_ANY] * n,
        scratch_shapes=[
            pltpu.SemaphoreType.DMA((n, len(ks))),
            pltpu.SemaphoreType.DMA((n, len(ks))),
            pltpu.SemaphoreType.DMA((n,)),
        ],
        name=name,
    )(*srcs)


def gather_two_level(srcs, name):
    n = len(srcs)
    out_shape = [jax.ShapeDtypeStruct((NDEV,) + s.shape, s.dtype) for s in srcs]

    def body(*refs):
        src, dst = refs[:n], refs[n:2 * n]
        send_sems, recv_sems, loc_sems = refs[2 * n:]
        x, y, c = _place()
        me = 4 * x + 2 * y + c
        sib = (x, y, 1 - c)
        first = (x ^ (1 - c), y ^ c)
        second = (x ^ c, y ^ (1 - c))
        diag = (x ^ 1, y ^ 1)

        def row(chip, core):
            return 4 * chip[0] + 2 * chip[1] + core

        def copy(a, s, block, to, from_src=False):
            return pltpu.make_async_remote_copy(
                src_ref=src[a] if from_src else dst[a].at[block], dst_ref=dst[a].at[block],
                send_sem=send_sems.at[a, s], recv_sem=recv_sems.at[a, s],
                device_id=to, device_id_type=pl.DeviceIdType.MESH)

        local = [pltpu.make_async_copy(src[a], dst[a].at[me], loc_sems.at[a]) for a in range(n)]
        sent = [copy(a, 0, me, sib, True) for a in range(n)]
        sent += [copy(a, 1, me, (*first, c), True) for a in range(n)]
        sent += [copy(a, 2, me, (*second, c), True) for a in range(n)]
        for cp in local + sent:
            cp.start()
        for a in range(n):
            copy(a, 1, row(first, c), (*first, c)).wait_recv()
            for cp in (copy(a, 3, row(first, c), (*second, c)), copy(a, 5, row(first, c), sib)):
                cp.start()
                sent.append(cp)
        for a in range(n):
            copy(a, 2, row(second, c), (*second, c)).wait_recv()
            cp = copy(a, 4, row(second, c), sib)
            cp.start()
            sent.append(cp)
        for a in range(n):
            copy(a, 3, row(diag, c), (*second, c)).wait_recv()
            cp = copy(a, 6, row(diag, c), sib)
            cp.start()
            sent.append(cp)
        for a in range(n):
            copy(a, 0, row((x, y), 1 - c), sib).wait_recv()
            copy(a, 4, row(first, 1 - c), sib).wait_recv()
            copy(a, 5, row(second, 1 - c), sib).wait_recv()
            copy(a, 6, row(diag, 1 - c), sib).wait_recv()
        for cp in local:
            cp.wait()
        for cp in sent:
            cp.wait_send()

    return pl.pallas_call(
        body,
        out_shape=out_shape,
        in_specs=[_ANY] * n,
        out_specs=[_ANY] * n,
        scratch_shapes=[
            pltpu.SemaphoreType.DMA((n, NDEV - 1)),
            pltpu.SemaphoreType.DMA((n, NDEV - 1)),
            pltpu.SemaphoreType.DMA((n,)),
        ],
        name=name,
    )(*srcs)


def _chip_across(core, da, db):
    x, y, _ = _place()
    return x ^ (da * (1 - core) + db * core), y ^ (db * (1 - core) + da * core)


def pair_swap(srcs, name, axis="c"):
    n = len(srcs)

    def body(*refs):
        src, dst = refs[:n], refs[n:2 * n]
        send_sems, recv_sems = refs[2 * n:]
        x, y, c = _place()
        partner = {"c": (x, y, 1 - c), "first": (*_chip_across(c, 1, 0), c), "second": (*_chip_across(c, 0, 1), c)}[axis]
        copies = []
        for a in range(n):
            cp = pltpu.make_async_remote_copy(
                src_ref=src[a], dst_ref=dst[a], send_sem=send_sems.at[a], recv_sem=recv_sems.at[a],
                device_id=partner, device_id_type=pl.DeviceIdType.MESH)
            cp.start()
            copies.append(cp)
        for cp in copies:
            cp.wait()

    return pl.pallas_call(
        body,
        out_shape=[jax.ShapeDtypeStruct(s.shape, s.dtype) for s in srcs],
        in_specs=[_ANY] * n,
        out_specs=[_ANY] * n,
        scratch_shapes=[pltpu.SemaphoreType.DMA((n,)), pltpu.SemaphoreType.DMA((n,))],
        name=name,
    )(*srcs)


_OFFSETS = ((0, 0), (0, 1), (1, 0), (1, 1))


def sibling_swap_blocks(blocks, name):
    n = len(blocks)

    def body(*refs):
        src, dst = refs[:n], refs[n:2 * n]
        send_sems, recv_sems = refs[2 * n:]
        x, y, c = _place()
        copies = []
        for a in range(n):
            for j, (da, db) in enumerate(_OFFSETS):
                px, py = _chip_across(1 - c, da, db)
                cp = pltpu.make_async_remote_copy(
                    src_ref=src[a].at[4 * px + 2 * py + (1 - c)], dst_ref=dst[a].at[j],
                    send_sem=send_sems.at[a, j], recv_sem=recv_sems.at[a, j],
                    device_id=(x, y, 1 - c), device_id_type=pl.DeviceIdType.MESH)
                cp.start()
                copies.append(cp)
        for cp in copies:
            cp.wait()

    return pl.pallas_call(
        body,
        out_shape=[jax.ShapeDtypeStruct((4,) + b.shape[1:], b.dtype) for b in blocks],
        in_specs=[_ANY] * n,
        out_specs=[_ANY] * n,
        scratch_shapes=[pltpu.SemaphoreType.DMA((n, 4)), pltpu.SemaphoreType.DMA((n, 4))],
        name=name,
    )(*blocks)


def add_own_blocks(blocks, got, name):
    _, R, C = blocks.shape
    tile = _pick(R, (256, 128, 64))

    def body(*refs):
        for j in range(4):
            refs[8 + j][...] = (refs[j][...].astype(F32) + refs[4 + j][...].astype(F32)).astype(refs[8 + j].dtype)

    def own(da, db):
        def index(i):
            c = lax.axis_index("c")
            px, py = _chip_across(c, da, db)
            return 4 * px + 2 * py + c, i, 0
        return pl.BlockSpec((None, tile, C), index)

    return pl.pallas_call(
        body,
        grid=(R // tile,),
        in_specs=[own(da, db) for da, db in _OFFSETS]
        + [pl.BlockSpec((None, tile, C), lambda i, j=j: (j, i, 0)) for j in range(4)],
        out_specs=[pl.BlockSpec((tile, C), lambda i: (i, 0))] * 4,
        out_shape=[jax.ShapeDtypeStruct((R, C), blocks.dtype)] * 4,
        compiler_params=_cp(("parallel",)),
        name=name,
    )(blocks, blocks, blocks, blocks, got, got, got, got)


def scatter_reduce(blocks, name):
    def add(n_out, ins, label):
        fn = lambda *a: [a[i].astype(F32) + a[n_out + i].astype(F32) for i in range(n_out)]
        rows, cols = ins[0].shape
        return rowwise(fn, ins, [], [(cols, ins[0].dtype)] * n_out, [], _pick(rows, (256, 128, 64)), label)

    nb = len(blocks)
    got = sibling_swap_blocks(blocks, name + "_d2d")
    q = [add_own_blocks(blocks[i], got[i], f"{name}_sum0_{i}") for i in range(nb)]
    r1 = pair_swap([q[i][j] for i in range(nb) for j in (2, 3)], name + "_ici1", "first")
    k = [add(2, [q[i][0], q[i][1], r1[2 * i], r1[2 * i + 1]], f"{name}_sum1_{i}") for i in range(nb)]
    r2 = pair_swap([k[i][1] for i in range(nb)], name + "_ici2", "second")
    return [jnp.stack([k[i][0], r2[i]]) for i in range(nb)]


def matmul(a, b, mode, out_dtype, name, add=None):
    if mode == "nn":
        (M, K), N = a.shape, b.shape[1]
    elif mode == "nt":
        (M, K), N = a.shape, b.shape[0]
    else:
        (K, M), N = a.shape, b.shape[1]
    tm = _pick(M, MATMUL_TILES)
    tn = _pick(N, MATMUL_TILES)
    osz = jnp.dtype(out_dtype).itemsize

    def vmem_bytes(tk):
        ops = 2 * tk * (tm * a.dtype.itemsize + tn * b.dtype.itemsize)
        return ops + tm * tn * (2 * osz + (4 if tk < K else 0) + (8 if add is not None else 0))

    tk = next((t for t in (K, 2816, 2048, 1408, 1024, 512, 256, 128) if K % t == 0 and vmem_bytes(t) <= MATMUL_VMEM_BUDGET), K)
    nk = K // tk
    dims = {"nn": NN, "nt": NT, "tn": TN}[mode]

    def body(*refs):
        if add is None:
            a_ref, b_ref, o_ref = refs[:3]
            c_ref = None
        else:
            a_ref, b_ref, c_ref, o_ref = refs[:4]

        def prod():
            return _dot(a_ref[...].astype(BF16), b_ref[...].astype(BF16), dims)

        def finish(r):
            if c_ref is not None:
                r = r + c_ref[...].astype(F32)
            o_ref[...] = r.astype(o_ref.dtype)

        if nk == 1:
            finish(prod())
            return
        acc = refs[-1]
        k = pl.program_id(2)

        @pl.when(k == 0)
        def _():
            acc[...] = prod()

        if nk > 2:
            @pl.when((k > 0) & (k < nk - 1))
            def _():
                acc[...] += prod()

        @pl.when(k == nk - 1)
        def _():
            finish(acc[...] + prod())

    a_spec = pl.BlockSpec((tk, tm), lambda i, j, k: (k, i)) if mode == "tn" else pl.BlockSpec((tm, tk), lambda i, j, k: (i, k))
    b_spec = pl.BlockSpec((tn, tk), lambda i, j, k: (j, k)) if mode == "nt" else pl.BlockSpec((tk, tn), lambda i, j, k: (k, j))
    o_spec = pl.BlockSpec((tm, tn), lambda i, j, k: (i, j))
    ins, specs = [a, b], [a_spec, b_spec]
    if add is not None:
        ins.append(add)
        specs.append(o_spec)
    return pl.pallas_call(
        body,
        grid=(M // tm, N // tn, nk),
        in_specs=specs,
        out_specs=o_spec,
        out_shape=jax.ShapeDtypeStruct((M, N), out_dtype),
        scratch_shapes=[pltpu.VMEM((tm, tn), F32)] if nk > 1 else [],
        compiler_params=_cp(("parallel", "parallel", "arbitrary")),
        name=name,
    )(*ins)


def rowwise(fn, tiled, full, out_tiled, out_acc, tile, name):
    tiled = [t if isinstance(t, tuple) else (t, t.shape[1], 0) for t in tiled]
    rows = tiled[0][0].shape[0]
    tile = min(tile, rows)
    assert rows % tile == 0
    nt, nf, no = len(tiled), len(full), len(out_tiled)

    def body(*refs):
        ins = [r[...] for r in refs[:nt + nf]]
        res = fn(*ins)
        if not isinstance(res, (tuple, list)):
            res = (res,)
        outs = refs[nt + nf:]
        for r, v in zip(outs[:no], res[:no]):
            r[...] = v.astype(r.dtype)
        if out_acc:
            @pl.when(pl.program_id(0) == 0)
            def _():
                for r in outs[no:]:
                    r[...] = jnp.zeros_like(r)

            for r, v in zip(outs[no:], res[no:]):
                r[...] += v

    in_specs = [pl.BlockSpec((tile, w), lambda i, cb=cb: (i, cb)) for (_, w, cb) in tiled]
    in_specs += [pl.BlockSpec(f.shape, lambda i, nd=f.ndim: (0,) * nd) for f in full]
    out_specs = [pl.BlockSpec((tile, w), lambda i: (i, 0)) for (w, _) in out_tiled]
    out_specs += [pl.BlockSpec(s, lambda i, nd=len(s): (0,) * nd) for s in out_acc]
    out_shape = [jax.ShapeDtypeStruct((rows, w), dt) for (w, dt) in out_tiled]
    out_shape += [jax.ShapeDtypeStruct(s, F32) for s in out_acc]
    res = pl.pallas_call(
        body,
        grid=(rows // tile,),
        in_specs=in_specs,
        out_specs=out_specs,
        out_shape=out_shape,
        compiler_params=_cp(("arbitrary",) if out_acc else ("parallel",)),
        name=name,
    )(*[t[0] for t in tiled], *full)
    return res


def adam_reduce(parts, w, m, v, name):
    P, R, C = parts.shape
    tr = _pick(R, (64, 32, 16, 8))
    c1 = 1.0 - ADAM_B1 ** ADAM_STEP
    c2 = 1.0 - ADAM_B2 ** ADAM_STEP

    def body(p_ref, w_ref, m_ref, v_ref, g_ref, d_ref, nm_ref, nv_ref):
        g = p_ref[0].astype(F32)
        for j in range(1, P):
            g = g + p_ref[j].astype(F32)
        mm = ADAM_B1 * m_ref[...] + (1.0 - ADAM_B1) * g
        vv = ADAM_B2 * v_ref[...] + (1.0 - ADAM_B2) * (g * g)
        m_hat = mm / c1
        v_hat = vv / c2
        g_ref[...] = g
        d_ref[...] = -ADAM_LR * (m_hat / (jnp.sqrt(v_hat) + ADAM_EPS) + ADAM_WD * w_ref[...])
        nm_ref[...] = mm
        nv_ref[...] = vv

    spec = pl.BlockSpec((tr, C), lambda i: (i, 0))
    return pl.pallas_call(
        body,
        grid=(R // tr,),
        in_specs=[pl.BlockSpec((P, tr, C), lambda i: (0, i, 0)), spec, spec, spec],
        out_specs=[spec] * 4,
        out_shape=[jax.ShapeDtypeStruct((R, C), F32)] * 4,
        compiler_params=_cp(("parallel",)),
        name=name,
    )(parts, w, m, v)


def reduce_parts(parts, name):
    P, R, C = parts.shape
    tr = _pick(R, (64, 32, 16, 8))

    def body(p_ref, g_ref):
        g = p_ref[0]
        for j in range(1, P):
            g = g + p_ref[j]
        g_ref[...] = g

    return pl.pallas_call(
        body,
        grid=(R // tr,),
        in_specs=[pl.BlockSpec((P, tr, C), lambda i: (0, i, 0))],
        out_specs=pl.BlockSpec((tr, C), lambda i: (i, 0)),
        out_shape=jax.ShapeDtypeStruct((R, C), F32),
        compiler_params=_cp(("parallel",)),
        name=name,
    )(parts)


def pack(arrs):
    flat = jnp.concatenate([a.reshape(-1).astype(F32) for a in arrs])
    n = flat.shape[0]
    padded = -(-n // 1024) * 1024
    return jnp.pad(flat, (0, padded - n)).reshape(padded // 128, 128)


def blocks_from_segments(segs, ncols):
    offs = np.cumsum([0] + [s.shape[1] for s in segs]).tolist()
    blocks = []
    for j in range(NDEV):
        lo, hi = j * ncols, (j + 1) * ncols
        parts = [s[:, max(lo, o) - o:min(hi, o + s.shape[1]) - o]
                 for s, o in zip(segs, offs[:-1]) if max(lo, o) < min(hi, o + s.shape[1])]
        blocks.append(jnp.concatenate(parts, axis=1) if len(parts) > 1 else parts[0])
    return jnp.stack(blocks)


def rows_to_blocks(g):
    return g.reshape(NDEV, -1, g.shape[1])


def segments_from_blocks(g, widths):
    ncols = g.shape[2]
    offs = np.cumsum([0] + list(widths)).tolist()
    out = []
    for o, w in zip(offs[:-1], widths):
        parts = [g[j][:, max(j * ncols, o) - j * ncols:min((j + 1) * ncols, o + w) - j * ncols]
                 for j in range(NDEV) if max(j * ncols, o) < min((j + 1) * ncols, o + w)]
        out.append(jnp.concatenate(parts, axis=1) if len(parts) > 1 else parts[0])
    return out


def unpack(slab, shapes):
    flat = slab.reshape(-1)
    out, off = [], 0
    for s in shapes:
        size = int(np.prod(s))
        out.append(flat[off:off + size].reshape(s))
        off += size
    return out


def modulate_fwd(x, g, sh, sc, name):
    def fn(x, g, sh, sc):
        return x * _rstd(x) * g * (1.0 + sc) + sh

    return rowwise(fn, [x], [g, sh, sc], [(x.shape[1], BF16)], [], 256, name)[0]


def norm_rope_fwd(p, width, cb, w, cosf, sinf, hd, name):
    nh = width // hd

    def fn(x, cosf, sinf, w):
        outs = []
        for h in range(nh):
            xh = x[:, h * hd:(h + 1) * hd]
            y = xh * _rstd(xh) * w
            outs.append(y * cosf + pltpu.roll(y, hd // 2, 1) * sinf)
        return jnp.concatenate(outs, axis=1) if nh > 1 else outs[0]

    return rowwise(fn, [(p, width, cb), cosf, sinf], [w], [(width, BF16)], [], 256, name)[0]


def norm_rope_bwd(p, width, cb, d, w, cosf, sinf, hd, name):
    nh = width // hd

    def fn(x, d, cosf, sinf, w):
        outs = []
        dw = jnp.zeros((1, hd), F32)
        for h in range(nh):
            xh = x[:, h * hd:(h + 1) * hd]
            dh = d[:, h * hd:(h + 1) * hd].astype(F32)
            r = _rstd(xh)
            n = xh * r
            dy = dh * cosf + pltpu.roll(dh * sinf, hd // 2, 1)
            dw = dw + jnp.sum(dy * n, axis=0, keepdims=True)
            dn = dy * w
            outs.append(r * (dn - n * jnp.mean(dn * n, axis=-1, keepdims=True)))
        return (jnp.concatenate(outs, axis=1) if nh > 1 else outs[0]), dw

    return rowwise(fn, [(p, width, cb), d, cosf, sinf], [w], [(width, BF16)], [(1, hd)], 256, name)


def attention_fwd(qr, kr, pkv, kcr, pkv_c, sink, hkv, hd, name):
    T, L = qr.shape[0], kcr.shape[0]
    G = qr.shape[1] // (hkv * hd)
    nb = T // BLOCK
    scale = hd ** -0.5

    def body(q_ref, kp, kc, kn, vp, vc, vn, ck_ref, cv_ref, sink_ref, o_ref, lse_ref, lser_ref):
        i = pl.program_id(1)
        kwin = jnp.concatenate([kp[...], kc[...], kn[...]], axis=0)
        vwin = jnp.concatenate([vp[...], vc[...], vn[...]], axis=0).astype(BF16)
        ck, cv = ck_ref[...], cv_ref[...].astype(BF16)
        row = lax.broadcasted_iota(jnp.int32, (BLOCK, 3 * BLOCK), 0)
        col = lax.broadcasted_iota(jnp.int32, (BLOCK, 3 * BLOCK), 1)
        rel = col - BLOCK - row
        valid = (jnp.abs(rel) <= WINDOW) & ((col >= BLOCK) | (i > 0)) & ((col < 2 * BLOCK) | (i < nb - 1))
        R = range(G)
        qa = q_ref[...]
        qs = [qa[:, g * hd:(g + 1) * hd] for g in R]
        sks = [sink_ref[g] for g in R]
        ss = [jnp.where(valid, _dot(qs[g], kwin, NT) * scale, NEG) for g in R]
        scs = [_dot(qs[g], ck, NT) * scale for g in R]
        ms = [jnp.maximum(jnp.maximum(jnp.max(ss[g], axis=1, keepdims=True), jnp.max(scs[g], axis=1, keepdims=True)),
                          sks[g]) for g in R]
        ps = [jnp.exp(ss[g] - ms[g]) for g in R]
        pcs = [jnp.exp(scs[g] - ms[g]) for g in R]
        nums = [_dot(ps[g].astype(BF16), vwin, NN) + _dot(pcs[g].astype(BF16), cv, NN) for g in R]
        dens = [jnp.exp(sks[g] - ms[g]) + jnp.sum(ps[g], axis=1, keepdims=True) + jnp.sum(pcs[g], axis=1, keepdims=True)
                for g in R]
        o_ref[...] = jnp.concatenate([(nums[g] / dens[g]).astype(o_ref.dtype) for g in R], axis=1)
        eye = (lax.broadcasted_iota(jnp.int32, (BLOCK, BLOCK), 0)
               == lax.broadcasted_iota(jnp.int32, (BLOCK, BLOCK), 1)).astype(F32)
        for g in R:
            lg = ms[g] + jnp.log(dens[g])
            lse_ref[g] = lg
            lser_ref[g] = jnp.sum(lg * eye, axis=0, keepdims=True)

    kv_specs = [
        pl.BlockSpec((BLOCK, hd), lambda h, i: (jnp.maximum(i - 1, 0), h)),
        pl.BlockSpec((BLOCK, hd), lambda h, i: (i, h)),
        pl.BlockSpec((BLOCK, hd), lambda h, i: (jnp.minimum(i + 1, nb - 1), h)),
    ]
    v_specs = [
        pl.BlockSpec((BLOCK, hd), lambda h, i: (jnp.maximum(i - 1, 0), hkv + h)),
        pl.BlockSpec((BLOCK, hd), lambda h, i: (i, hkv + h)),
        pl.BlockSpec((BLOCK, hd), lambda h, i: (jnp.minimum(i + 1, nb - 1), hkv + h)),
    ]
    return pl.pallas_call(
        body,
        grid=(hkv, nb),
        in_specs=[pl.BlockSpec((BLOCK, G * hd), lambda h, i: (i, h))] + kv_specs + v_specs + [
            pl.BlockSpec((L, hd), lambda h, i: (0, h)),
            pl.BlockSpec((L, hd), lambda h, i: (0, hkv + h)),
            pl.BlockSpec((G, 1, 1), lambda h, i: (h, 0, 0)),
        ],
        out_specs=[
            pl.BlockSpec((BLOCK, G * hd), lambda h, i: (i, h)),
            pl.BlockSpec((G, BLOCK, 1), lambda h, i: (h, i, 0)),
            pl.BlockSpec((G, 1, BLOCK), lambda h, i: (h, 0, i)),
        ],
        out_shape=[jax.ShapeDtypeStruct(qr.shape, BF16), jax.ShapeDtypeStruct((hkv * G, T, 1), F32),
                   jax.ShapeDtypeStruct((hkv * G, 1, T), F32)],
        compiler_params=_cp(("parallel", "parallel")),
        name=name,
    )(qr, kr, kr, kr, pkv, pkv, pkv, kcr, pkv_c, sink)


def attention_bwd_q(qr, kr, pkv, kcr, pkv_c, sink, do, o, lse, hkv, hd, name):
    T, L = qr.shape[0], kcr.shape[0]
    G = qr.shape[1] // (hkv * hd)
    nb = T // BLOCK
    scale = hd ** -0.5

    def body(q_ref, kp, kc, kn, vp, vc, vn, ck_ref, cv_ref, sink_ref, do_ref, o_ref, lse_ref,
             dq_ref, dck_ref, dcv_ref, dsink_ref, drr_ref):
        i = pl.program_id(1)

        @pl.when(i == 0)
        def _():
            dck_ref[...] = jnp.zeros_like(dck_ref)
            dcv_ref[...] = jnp.zeros_like(dcv_ref)
            dsink_ref[...] = jnp.zeros_like(dsink_ref)

        kwin = jnp.concatenate([kp[...], kc[...], kn[...]], axis=0)
        vwin = jnp.concatenate([vp[...], vc[...], vn[...]], axis=0).astype(BF16)
        ck, cv = ck_ref[...], cv_ref[...].astype(BF16)
        row = lax.broadcasted_iota(jnp.int32, (BLOCK, 3 * BLOCK), 0)
        col = lax.broadcasted_iota(jnp.int32, (BLOCK, 3 * BLOCK), 1)
        rel = col - BLOCK - row
        valid = (jnp.abs(rel) <= WINDOW) & ((col >= BLOCK) | (i > 0)) & ((col < 2 * BLOCK) | (i < nb - 1))
        R = range(G)
        qa, doa, oa = q_ref[...], do_ref[...], o_ref[...]
        qs = [qa[:, g * hd:(g + 1) * hd] for g in R]
        dos = [doa[:, g * hd:(g + 1) * hd] for g in R]
        lgs = [lse_ref[g] for g in R]
        sks = [sink_ref[g] for g in R]
        ss = [jnp.where(valid, _dot(qs[g], kwin, NT) * scale, NEG) for g in R]
        scs = [_dot(qs[g], ck, NT) * scale for g in R]
        dps = [_dot(dos[g], vwin, NT) for g in R]
        dpcs = [_dot(dos[g], cv, NT) for g in R]
        drs = [jnp.sum(dos[g].astype(F32) * oa[:, g * hd:(g + 1) * hd].astype(F32), axis=1, keepdims=True) for g in R]
        ps = [jnp.exp(ss[g] - lgs[g]) for g in R]
        pcs = [jnp.exp(scs[g] - lgs[g]) for g in R]
        dss = [(ps[g] * (dps[g] - drs[g]) * scale).astype(BF16) for g in R]
        dscs = [(pcs[g] * (dpcs[g] - drs[g]) * scale).astype(BF16) for g in R]
        dqs = [_dot(dss[g], kwin, NN) + _dot(dscs[g], ck, NN) for g in R]
        dcks = [_dot(dscs[g], qs[g], TN) for g in R]
        dcvs = [_dot(pcs[g].astype(BF16), dos[g], TN) for g in R]
        dq_ref[...] = jnp.concatenate(dqs, axis=1)
        dck_ref[...] += (dcks[0] + dcks[1]) + (dcks[2] + dcks[3]) if G == 4 else sum(dcks[1:], dcks[0])
        dcv_ref[...] += (dcvs[0] + dcvs[1]) + (dcvs[2] + dcvs[3]) if G == 4 else sum(dcvs[1:], dcvs[0])
        eye = (lax.broadcasted_iota(jnp.int32, (BLOCK, BLOCK), 0)
               == lax.broadcasted_iota(jnp.int32, (BLOCK, BLOCK), 1)).astype(F32)
        for g in R:
            dsink_ref[g] += -jnp.sum(jnp.exp(sks[g] - lgs[g]) * drs[g], axis=0, keepdims=True)
            drr_ref[g] = jnp.sum(drs[g] * eye, axis=0, keepdims=True)

    kv_specs = [
        pl.BlockSpec((BLOCK, hd), lambda h, i: (jnp.maximum(i - 1, 0), h)),
        pl.BlockSpec((BLOCK, hd), lambda h, i: (i, h)),
        pl.BlockSpec((BLOCK, hd), lambda h, i: (jnp.minimum(i + 1, nb - 1), h)),
    ]
    v_specs = [
        pl.BlockSpec((BLOCK, hd), lambda h, i: (jnp.maximum(i - 1, 0), hkv + h)),
        pl.BlockSpec((BLOCK, hd), lambda h, i: (i, hkv + h)),
        pl.BlockSpec((BLOCK, hd), lambda h, i: (jnp.minimum(i + 1, nb - 1), hkv + h)),
    ]
    qspec = pl.BlockSpec((BLOCK, G * hd), lambda h, i: (i, h))
    return pl.pallas_call(
        body,
        grid=(hkv, nb),
        in_specs=[qspec] + kv_specs + v_specs + [
            pl.BlockSpec((L, hd), lambda h, i: (0, h)),
            pl.BlockSpec((L, hd), lambda h, i: (0, hkv + h)),
            pl.BlockSpec((G, 1, 1), lambda h, i: (h, 0, 0)),
            qspec, qspec,
            pl.BlockSpec((G, BLOCK, 1), lambda h, i: (h, i, 0)),
        ],
        out_specs=[
            qspec,
            pl.BlockSpec((L, hd), lambda h, i: (0, h)),
            pl.BlockSpec((L, hd), lambda h, i: (0, h)),
            pl.BlockSpec((G, 1, 1), lambda h, i: (h, 0, 0)),
            pl.BlockSpec((G, 1, BLOCK), lambda h, i: (h, 0, i)),
        ],
        out_shape=[
            jax.ShapeDtypeStruct(qr.shape, F32),
            jax.ShapeDtypeStruct((L, hkv * hd), F32),
            jax.ShapeDtypeStruct((L, hkv * hd), F32),
            jax.ShapeDtypeStruct((hkv * G, 1, 1), F32),
            jax.ShapeDtypeStruct((hkv * G, 1, T), F32),
        ],
        compiler_params=_cp(("parallel", "arbitrary")),
        name=name,
    )(qr, kr, kr, kr, pkv, pkv, pkv, kcr, pkv_c, sink, do, o, lse)


def attention_bwd_kv(qr, kr, pkv, do, lse_row, dr_row, hkv, hd, name):
    T = qr.shape[0]
    G = qr.shape[1] // (hkv * hd)
    nb = T // BLOCK
    scale = hd ** -0.5

    def body(k_ref, v_ref, *refs):
        qs, dos, lses, drs = refs[0:3], refs[3:6], refs[6:9], refs[9:12]
        dk_ref, dv_ref = refs[12:]
        j = pl.program_id(1)
        k = k_ref[...]
        v = v_ref[...].astype(BF16)
        row = lax.broadcasted_iota(jnp.int32, (BLOCK, BLOCK), 0)
        col = lax.broadcasted_iota(jnp.int32, (BLOCK, BLOCK), 1)
        bias = []
        for d in range(3):
            iq = j + d - 1
            rel = row - col - (d - 1) * BLOCK
            valid = (jnp.abs(rel) <= WINDOW) & (iq >= 0) & (iq < nb)
            bias += [jnp.where(valid, 0.0, NEG)] * G
        bias = jnp.concatenate(bias, axis=1)

        def stack(refs):
            vals = [r[...] for r in refs]
            return jnp.concatenate([a[:, g * hd:(g + 1) * hd] for a in vals for g in range(G)], axis=0)

        q, dob = stack(qs), stack(dos)
        lrow = jnp.concatenate([r[g] for r in lses for g in range(G)], axis=1)
        drow = jnp.concatenate([r[g] for r in drs for g in range(G)], axis=1)
        st = _dot(k, q, NT) * scale + bias
        pt = jnp.exp(st - lrow)
        dpt = _dot(v, dob, NT)
        dst = (pt * (dpt - drow) * scale).astype(BF16)
        dk_ref[...] = _dot(dst, q, NN).astype(dk_ref.dtype)
        dv_ref[...] = _dot(pt.astype(BF16), dob, NN).astype(dv_ref.dtype)

    def q3(width_block):
        return [
            pl.BlockSpec(width_block, lambda h, j: (jnp.maximum(j - 1, 0), h)),
            pl.BlockSpec(width_block, lambda h, j: (j, h)),
            pl.BlockSpec(width_block, lambda h, j: (jnp.minimum(j + 1, nb - 1), h)),
        ]

    row3 = [
        pl.BlockSpec((G, 1, BLOCK), lambda h, j: (h, 0, jnp.maximum(j - 1, 0))),
        pl.BlockSpec((G, 1, BLOCK), lambda h, j: (h, 0, j)),
        pl.BlockSpec((G, 1, BLOCK), lambda h, j: (h, 0, jnp.minimum(j + 1, nb - 1))),
    ]
    qb = (BLOCK, G * hd)
    return pl.pallas_call(
        body,
        grid=(hkv, nb),
        in_specs=[pl.BlockSpec((BLOCK, hd), lambda h, j: (j, h)), pl.BlockSpec((BLOCK, hd), lambda h, j: (j, hkv + h))]
        + q3(qb) + q3(qb) + row3 + row3,
        out_specs=[pl.BlockSpec((BLOCK, hd), lambda h, j: (j, h))] * 2,
        out_shape=[jax.ShapeDtypeStruct((T, hkv * hd), BF16)] * 2,
        compiler_params=_cp(("parallel", "parallel")),
        name=name,
    )(kr, pkv, qr, qr, qr, do, do, do, lse_row, lse_row, lse_row, dr_row, dr_row, dr_row)


def gate_fwd(plr, wf, wb, bf, bb, name):
    n = wf.shape[1]

    def fn(lr, wf, wb, bf, bb):
        lrb = lr.astype(BF16)
        outs = []
        for w, b in ((wf, bf), (wb, bb)):
            z = _dot(lrb, w.astype(BF16), NN) + b
            outs.append((jnp.minimum(z, 0.0) - jnp.log(1.0 + jnp.exp(-jnp.abs(z)))) / GLA_GATE_NORM)
        return outs

    return rowwise(fn, [plr], [wf, wb, bf, bb], [(n, F32), (n, F32)], [], 256, name)


def gate_bwd(plr, dgf, dgb, wf, wb, bf, bb, name):
    n = wf.shape[1]

    def fn(lr, dgf, dgb, wf, wb, bf, bb):
        lrb = lr.astype(BF16)
        dlr = jnp.zeros(lr.shape, F32)
        res = []
        for w, b, dg in ((wf, bf, dgf), (wb, bb, dgb)):
            wb16 = w.astype(BF16)
            z = _dot(lrb, wb16, NN) + b
            dz = dg * _sig(-z) / GLA_GATE_NORM
            dzb = dz.astype(BF16)
            dlr = dlr + _dot(dzb, wb16, NT)
            res += [_dot(lrb, dzb, TN), jnp.sum(dz, axis=0, keepdims=True)]
        return [dlr] + res

    return rowwise(fn, [plr, dgf, dgb], [wf, wb, bf, bb], [(128, BF16)],
                   [(128, n), (1, n), (128, n), (1, n)], 256, name)


def _tri_dot(tri_b, x):
    x1 = x.astype(BF16)
    r1 = x - x1.astype(F32)
    x2 = r1.astype(BF16)
    x3 = (r1 - x2.astype(F32)).astype(BF16)
    return _dot(tri_b, x1, NN) + _dot(tri_b, x2, NN) + _dot(tri_b, x3, NN)


def gla_fwd(pqk, pv, gl, s0, heads, reverse, name, o_add=None):
    T = pqk.shape[0]
    dk = pqk.shape[1] // (2 * heads)
    dv = pv.shape[1] // heads
    C = GLA_CHUNK
    nc = T // C
    qscale = dk ** -0.5

    def body(*refs):
        if o_add is None:
            q_ref, k_ref, v_ref, g_ref, s0_ref, o_ref, st_ref, sf_ref, S = refs
            oa_ref = None
        else:
            q_ref, k_ref, v_ref, g_ref, s0_ref, oa_ref, o_ref, st_ref, sf_ref, S = refs
        n = pl.program_id(0)

        @pl.when(n == 0)
        def _():
            S[...] = s0_ref[...]

        r = lax.broadcasted_iota(jnp.int32, (C, C), 0)
        c = lax.broadcasted_iota(jnp.int32, (C, C), 1)
        tri = (r <= c) if reverse else (r >= c)
        trib = tri.astype(BF16)
        ga, qa, ka, va = g_ref[...], q_ref[...], k_ref[...], v_ref[...]
        sts = [S[h] for h in range(heads)]
        H = range(heads)
        gs = [ga[:, h * dk:(h + 1) * dk] for h in H]
        bs = [_tri_dot(trib, g) for g in gs]
        bls = [jnp.sum(g, axis=0, keepdims=True) for g in gs]
        mid = lax.broadcasted_iota(jnp.int32, (C, 1), 0) == C // 2
        bms = [jnp.sum(jnp.where(mid, b, 0.0), axis=0, keepdims=True) for b in bs]
        vs = [va[:, h * dv:(h + 1) * dv].astype(BF16) for h in H]
        qs = [qa[:, h * dk:(h + 1) * dk].astype(F32) * qscale for h in H]
        qes = [(qs[h] * jnp.exp(bs[h])).astype(BF16) for h in H]
        qms = [(qs[h] * jnp.exp(bs[h] - bms[h])).astype(BF16) for h in H]
        kms = [(ka[:, h * dk:(h + 1) * dk].astype(F32) * jnp.exp(bms[h] - bs[h])).astype(BF16) for h in H]
        kls = [(ka[:, h * dk:(h + 1) * dk].astype(F32) * jnp.exp(bls[h] - bs[h])).astype(BF16) for h in H]
        inter = [_dot(qes[h], sts[h].astype(BF16), NT) for h in H]
        upd = [_dot(vs[h], kls[h], TN) for h in H]
        As = [jnp.where(tri, _dot(qms[h], kms[h], NT), 0.0).astype(BF16) for h in H]
        outs = [inter[h] + _dot(As[h], vs[h], NN) for h in H]
        news = [sts[h] * jnp.exp(bls[h]) + upd[h] for h in H]
        o = jnp.concatenate(outs, axis=1)
        if oa_ref is not None:
            o = o + oa_ref[...]
        o_ref[...] = o
        for h in range(heads):
            st_ref[0, h] = sts[h]
            S[h] = news[h]

        @pl.when(n == nc - 1)
        def _():
            for h in range(heads):
                sf_ref[h] = news[h]

    def ci(n):
        return (nc - 1 - n) if reverse else n

    specs = [
        pl.BlockSpec((C, heads * dk), lambda n: (ci(n), 0)),
        pl.BlockSpec((C, heads * dk), lambda n: (ci(n), 1)),
        pl.BlockSpec((C, heads * dv), lambda n: (ci(n), 0)),
        pl.BlockSpec((C, heads * dk), lambda n: (ci(n), 0)),
        pl.BlockSpec((heads, dv, dk), lambda n: (0, 0, 0)),
    ]
    ins = [pqk, pqk, pv, gl, s0]
    if o_add is not None:
        specs.append(pl.BlockSpec((C, heads * dv), lambda n: (ci(n), 0)))
        ins.append(o_add)
    return pl.pallas_call(
        body,
        grid=(nc,),
        in_specs=specs,
        out_specs=[
            pl.BlockSpec((C, heads * dv), lambda n: (ci(n), 0)),
            pl.BlockSpec((1, heads, dv, dk), lambda n: (ci(n), 0, 0, 0)),
            pl.BlockSpec((heads, dv, dk), lambda n: (0, 0, 0)),
        ],
        out_shape=[
            jax.ShapeDtypeStruct((T, heads * dv), F32),
            jax.ShapeDtypeStruct((nc, heads, dv, dk), F32),
            jax.ShapeDtypeStruct((heads, dv, dk), F32),
        ],
        scratch_shapes=[pltpu.VMEM((heads, dv, dk), F32)],
        compiler_params=_cp(("arbitrary",)),
        name=name,
    )(*ins)


def gla_bwd(pqk, pv, gl, states, do, dsf, heads, reverse, name, acc=None):
    T = pqk.shape[0]
    dk = pqk.shape[1] // (2 * heads)
    dv = pv.shape[1] // heads
    C = GLA_CHUNK
    nc = T // C
    qscale = dk ** -0.5

    def body(*refs):
        if acc is None:
            q_ref, k_ref, v_ref, g_ref, st_ref, do_ref, dsf_ref, dq_ref, dk_ref, dv_ref, dg_ref, ds0_ref, dS = refs
            aq = ak = av = None
        else:
            (q_ref, k_ref, v_ref, g_ref, st_ref, do_ref, dsf_ref, aq, ak, av,
             dq_ref, dk_ref, dv_ref, dg_ref, ds0_ref, dS) = refs
        n = pl.program_id(0)

        @pl.when(n == 0)
        def _():
            dS[...] = dsf_ref[...]

        r = lax.broadcasted_iota(jnp.int32, (C, C), 0)
        c = lax.broadcasted_iota(jnp.int32, (C, C), 1)
        tri = (r <= c) if reverse else (r >= c)
        tri_t = (r >= c) if reverse else (r <= c)
        trib, tritb = tri.astype(BF16), tri_t.astype(BF16)
        ga, qa, ka, va, doa = g_ref[...], q_ref[...], k_ref[...], v_ref[...], do_ref[...]
        sts = [st_ref[0, h] for h in range(heads)]
        dsts = [dS[h] for h in range(heads)]
        H = range(heads)
        gs = [ga[:, h * dk:(h + 1) * dk] for h in H]
        bs = [_tri_dot(trib, g) for g in gs]
        bls = [jnp.sum(g, axis=0, keepdims=True) for g in gs]
        mid = lax.broadcasted_iota(jnp.int32, (C, 1), 0) == C // 2
        bms = [jnp.sum(jnp.where(mid, b, 0.0), axis=0, keepdims=True) for b in bs]
        ebs = [jnp.exp(b) for b in bs]
        embs = [jnp.exp(bs[h] - bms[h]) for h in H]
        enbs = [jnp.exp(bms[h] - bs[h]) for h in H]
        elbs = [jnp.exp(bls[h] - bs[h]) for h in H]
        ebls = [jnp.exp(bl) for bl in bls]
        vbs = [va[:, h * dv:(h + 1) * dv].astype(BF16) for h in H]
        dobs = [doa[:, h * dv:(h + 1) * dv].astype(BF16) for h in H]
        qs = [qa[:, h * dk:(h + 1) * dk].astype(F32) * qscale for h in H]
        qes = [qs[h] * ebs[h] for h in H]
        qms = [qs[h] * embs[h] for h in H]
        kms = [ka[:, h * dk:(h + 1) * dk].astype(F32) * enbs[h] for h in H]
        kls = [ka[:, h * dk:(h + 1) * dk].astype(F32) * elbs[h] for h in H]
        qebs = [a.astype(BF16) for a in qes]
        qmbs = [a.astype(BF16) for a in qms]
        kmbs = [a.astype(BF16) for a in kms]
        klbs = [a.astype(BF16) for a in kls]
        stbs = [a.astype(BF16) for a in sts]
        dstbs = [a.astype(BF16) for a in dsts]
        ps = [jnp.where(tri, _dot(qmbs[h], kmbs[h], NT), 0.0).astype(BF16) for h in H]
        dps = [jnp.where(tri, _dot(dobs[h], vbs[h], NT), 0.0).astype(BF16) for h in H]
        dqes = [_dot(dobs[h], stbs[h], NN) for h in H]
        dkls = [_dot(vbs[h], dstbs[h], NN) for h in H]
        dv1 = [_dot(klbs[h], dstbs[h], NT) for h in H]
        dsn1 = [_dot(dobs[h], qebs[h], TN) for h in H]
        dqms = [_dot(dps[h], kmbs[h], NN) for h in H]
        dkms = [_dot(dps[h], qmbs[h], TN) for h in H]
        dvs = [_dot(ps[h], dobs[h], TN) + dv1[h] for h in H]
        dbls = [ebls[h] * jnp.sum(dsts[h] * sts[h], axis=0, keepdims=True)
                + jnp.sum(dkls[h] * kls[h], axis=0, keepdims=True) for h in H]
        dsns = [dsn1[h] + dsts[h] * ebls[h] for h in H]
        dqs = [(dqes[h] * ebs[h] + dqms[h] * embs[h]) * qscale for h in H]
        dks = [dkms[h] * enbs[h] + dkls[h] * elbs[h] for h in H]
        dbs = [dqes[h] * qes[h] + dqms[h] * qms[h] - dkms[h] * kms[h] - dkls[h] * kls[h] for h in H]
        dgs = [_tri_dot(tritb, dbs[h]) + dbls[h] for h in H]
        dq, dkk, dvv = (jnp.concatenate(a, axis=1) for a in (dqs, dks, dvs))
        if aq is not None:
            dq = dq + aq[...].astype(F32)
            dkk = dkk + ak[...].astype(F32)
            dvv = dvv + av[...].astype(F32)
        dq_ref[...] = dq.astype(dq_ref.dtype)
        dk_ref[...] = dkk.astype(dk_ref.dtype)
        dv_ref[...] = dvv.astype(dv_ref.dtype)
        dg_ref[...] = jnp.concatenate(dgs, axis=1)
        for h in range(heads):
            dS[h] = dsns[h]

        @pl.when(n == nc - 1)
        def _():
            for h in range(heads):
                ds0_ref[h] = dsns[h]

    def ci(n):
        return n if reverse else (nc - 1 - n)

    kspec = pl.BlockSpec((C, heads * dk), lambda n: (ci(n), 0))
    vspec = pl.BlockSpec((C, heads * dv), lambda n: (ci(n), 0))
    sspec = pl.BlockSpec((heads, dv, dk), lambda n: (0, 0, 0))
    specs = [
        kspec,
        pl.BlockSpec((C, heads * dk), lambda n: (ci(n), 1)),
        vspec,
        kspec,
        pl.BlockSpec((1, heads, dv, dk), lambda n: (ci(n), 0, 0, 0)),
        vspec,
        sspec,
    ]
    ins = [pqk, pqk, pv, gl, states, do, dsf]
    odt = F32 if acc is None else BF16
    if acc is not None:
        specs += [kspec, kspec, vspec]
        ins += list(acc)
    return pl.pallas_call(
        body,
        grid=(nc,),
        in_specs=specs,
        out_specs=[kspec, kspec, vspec, kspec, sspec],
        out_shape=[
            jax.ShapeDtypeStruct((T, heads * dk), odt),
            jax.ShapeDtypeStruct((T, heads * dk), odt),
            jax.ShapeDtypeStruct((T, heads * dv), odt),
            jax.ShapeDtypeStruct((T, heads * dk), F32),
            jax.ShapeDtypeStruct((heads, dv, dk), F32),
        ],
        scratch_shapes=[pltpu.VMEM((heads, dv, dk), F32)],
        compiler_params=_cp(("arbitrary",)),
        name=name,
    )(*ins)


def gla_out_fwd(og, prb, gn, heads, name):
    dv = og.shape[1] // heads

    def fn(og, rb, gn):
        outs = []
        for h in range(heads):
            oh = og[:, h * dv:(h + 1) * dv]
            outs.append(oh * _rstd(oh) * gn)
        y = jnp.concatenate(outs, axis=1)
        return y * (rb * _sig(rb))

    return rowwise(fn, [og, prb], [gn], [(og.shape[1], BF16)], [], 256, name)[0]


def gla_out_bwd(og, prb, du, gn, heads, name):
    dv = og.shape[1] // heads

    def fn(og, rb, du, gn):
        sg = _sig(rb)
        silu = rb * sg
        dsilu = sg * (1.0 + rb * (1.0 - sg))
        dog, ys = [], []
        dgn = jnp.zeros((1, dv), F32)
        for h in range(heads):
            sl = slice(h * dv, (h + 1) * dv)
            oh = og[:, sl]
            r = _rstd(oh)
            n = oh * r
            ys.append(n * gn)
            dy = du[:, sl] * silu[:, sl]
            dgn = dgn + jnp.sum(dy * n, axis=0, keepdims=True)
            dn = dy * gn
            dog.append(r * (dn - n * jnp.mean(dn * n, axis=-1, keepdims=True)))
        y = jnp.concatenate(ys, axis=1)
        return jnp.concatenate(dog, axis=1), du * y * dsilu, dgn

    return rowwise(fn, [og, prb, du], [gn], [(og.shape[1], F32), (og.shape[1], BF16)], [(1, dv)], 128, name)


def conv_swiglu_fwd(u, cw, cb, name):
    T, F2 = u.shape
    F = F2 // 2
    tt = min(512, T)
    tc = _pick(F, (512, 256, 128))
    nt_, ncol = T // tt, F // tc
    H = CONV_HALO
    n = tt + 2 * H

    def body(ua, uap, uan, ug, ugp, ugn, wa, wg, ba, bg, f_ref):
        i = pl.program_id(1)
        keep_p = (i > 0).astype(F32)
        keep_n = (i < nt_ - 1).astype(F32)
        res = []
        for m, p, nx, w, b in ((ua, uap, uan, wa, ba), (ug, ugp, ugn, wg, bg)):
            x = jnp.concatenate([p[...] * keep_p, m[...], nx[...] * keep_n], axis=0)
            down, up = pltpu.roll(x, 1, 0)[H:H + tt], pltpu.roll(x, n - 1, 0)[H:H + tt]
            res.append(w[0] * down + w[1] * x[H:H + tt] + w[2] * up + b[...])
        a, g = res
        f_ref[...] = (a * _sig_tanh(a) * g).astype(f_ref.dtype)

    wspec = lambda off: pl.BlockSpec((3, 1, tc), lambda j, i: (0, 0, j + off))
    bspec = lambda off: pl.BlockSpec((1, tc), lambda j, i: (0, j + off))
    return pl.pallas_call(
        body,
        grid=(ncol, nt_),
        in_specs=conv_halo_specs(T, tt, tc, 0) + conv_halo_specs(T, tt, tc, ncol)
        + [wspec(0), wspec(ncol), bspec(0), bspec(ncol)],
        out_specs=pl.BlockSpec((tt, tc), lambda j, i: (i, j)),
        out_shape=jax.ShapeDtypeStruct((T, F), BF16),
        compiler_params=_cp(("parallel", "parallel")),
        name=name,
    )(u, u, u, u, u, u, cw, cw, cb, cb)


CONV_HALO = 16


def conv_halo_specs(T, tt, tc, off):
    r = tt // CONV_HALO
    last = T // CONV_HALO - 1
    return [
        pl.BlockSpec((tt, tc), lambda j, i: (i, j + off)),
        pl.BlockSpec((CONV_HALO, tc), lambda j, i: (jnp.maximum(i * r - 1, 0), j + off)),
        pl.BlockSpec((CONV_HALO, tc), lambda j, i: (jnp.minimum((i + 1) * r, last), j + off)),
    ]


def conv_swiglu_bwd_fused(u, cw, cb, df, name):
    T, F2 = u.shape
    F = F2 // 2
    tt = min(512, T)
    tc = _pick(F, (512, 256, 128))
    nt_, ncol = T // tt, F // tc
    H = CONV_HALO
    n = tt + 2 * H

    def body(ua, uap, uan, ug, ugp, ugn, dm, dp_, dn, wa, wg, ba, bg, dua_ref, dug_ref, dwa, dwg, dba, dbg):
        i = pl.program_id(1)

        @pl.when(i == 0)
        def _():
            for r in (dwa, dwg, dba, dbg):
                r[...] = jnp.zeros_like(r)

        keep_p = (i > 0).astype(F32)
        keep_n = (i < nt_ - 1).astype(F32)

        def ext(m, p, nx):
            return jnp.concatenate([p[...].astype(F32) * keep_p, m[...].astype(F32), nx[...].astype(F32) * keep_n],
                                   axis=0)

        d = ext(dm, dp_, dn)
        conv, parts = [], []
        for m, p, nx, w, b in ((ua, uap, uan, wa, ba), (ug, ugp, ugn, wg, bg)):
            x = ext(m, p, nx)
            down, up = pltpu.roll(x, 1, 0), pltpu.roll(x, n - 1, 0)
            parts.append((down, x, up))
            conv.append(w[0] * down + w[1] * x + w[2] * up + b[...])
        a, g = conv
        sg = _sig_tanh(a)
        da = d * g * sg * (1.0 + a * (1.0 - sg))
        dg = d * a * sg
        for dd, w, (down, x, up), o_ref, dw, db in ((da, wa, parts[0], dua_ref, dwa, dba),
                                                    (dg, wg, parts[1], dug_ref, dwg, dbg)):
            du = w[0] * pltpu.roll(dd, n - 1, 0) + w[1] * dd + w[2] * pltpu.roll(dd, 1, 0)
            o_ref[...] = du[H:H + tt].astype(o_ref.dtype)
            ddm = dd[H:H + tt]
            dw[0] += jnp.sum(ddm * down[H:H + tt], axis=0, keepdims=True)
            dw[1] += jnp.sum(ddm * x[H:H + tt], axis=0, keepdims=True)
            dw[2] += jnp.sum(ddm * up[H:H + tt], axis=0, keepdims=True)
            db[...] += jnp.sum(ddm, axis=0, keepdims=True)

    wspec = lambda off: pl.BlockSpec((3, 1, tc), lambda j, i: (0, 0, j + off))
    bspec = lambda off: pl.BlockSpec((1, tc), lambda j, i: (0, j + off))
    tile = pl.BlockSpec((tt, tc), lambda j, i: (i, j))
    return pl.pallas_call(
        body,
        grid=(ncol, nt_),
        in_specs=conv_halo_specs(T, tt, tc, 0) + conv_halo_specs(T, tt, tc, ncol) + conv_halo_specs(T, tt, tc, 0)
        + [wspec(0), wspec(ncol), bspec(0), bspec(ncol)],
        out_specs=[tile, tile, wspec(0), wspec(0), bspec(0), bspec(0)],
        out_shape=[
            jax.ShapeDtypeStruct((T, F), BF16), jax.ShapeDtypeStruct((T, F), BF16),
            jax.ShapeDtypeStruct((3, 1, F), F32), jax.ShapeDtypeStruct((3, 1, F), F32),
            jax.ShapeDtypeStruct((1, F), F32), jax.ShapeDtypeStruct((1, F), F32),
        ],
        compiler_params=_cp(("parallel", "arbitrary")),
        name=name,
    )(u, u, u, u, u, u, df, df, df, cw, cw, cb, cb)


def rope_tables(n, hd):
    rows = n // GRID_W
    row = jnp.repeat(jnp.arange(rows), GRID_W)
    col = jnp.tile(jnp.arange(GRID_W), rows)
    n_freq = hd // 4
    inv = ROPE_THETA ** (-jnp.arange(n_freq, dtype=F32) / n_freq)
    ang = jnp.concatenate([row[:, None] * inv, col[:, None] * inv], axis=-1)
    cos, sin = jnp.cos(ang), jnp.sin(ang)
    return jnp.concatenate([cos, cos], axis=-1), jnp.concatenate([-sin, sin], axis=-1)


def local_step(x, ctx, tgt, mod, modc, W, P):
    T, D = x.shape
    L = ctx.shape[0]
    hd, hq, hkv, gh = P["hd"], P["hq"], P["hkv"], P["gh"]
    sh1, sc1, g1, sh2, sc2, g2 = mod
    csh1, csc1 = modc
    kvw = hkv * hd
    gkw = W["gqk"].shape[1] // 2
    gdv = D // gh
    gdk = gkw // gh

    h = modulate_fwd(x, P["g_mix"], sh1, sc1, "mod1")
    hc = modulate_fwd(ctx, P["g_mix"], csh1, csc1, "mod1_ctx")
    pq = matmul(h, W["q"], "nn", F32, "proj_q")
    pkv = matmul(h, W["kv"], "nn", F32, "proj_kv")
    pgqk = matmul(h, W["gqk"], "nn", F32, "proj_gqk")
    pgv = matmul(h, W["gv"], "nn", F32, "proj_gv")
    prb = matmul(h, W["rb"], "nn", F32, "proj_rb")
    plr = matmul(h, W["lr"], "nn", F32, "proj_lr")
    pgab = matmul(h, W["gab"], "nn", F32, "proj_gab")
    pkv_c = matmul(hc, W["kv"], "nn", F32, "proj_kv_ctx")
    pgqk_c = matmul(hc, W["gqk"], "nn", F32, "proj_gqk_ctx")
    pgv_c = matmul(hc, W["gv"], "nn", F32, "proj_gv_ctx")
    plr_c = matmul(hc, W["lr"], "nn", F32, "proj_lr_ctx")

    cosf, sinf = rope_tables(T, hd)
    one_c, zero_c = jnp.ones((L, hd), F32), jnp.zeros((L, hd), F32)
    qr = norm_rope_fwd(pq, hq * hd, 0, P["q_norm"], cosf, sinf, hd, "qnorm")
    kr = norm_rope_fwd(pkv, kvw, 0, P["k_norm"], cosf, sinf, hd, "knorm")
    kcr = norm_rope_fwd(pkv_c, kvw, 0, P["k_norm"], one_c, zero_c, hd, "knorm_ctx")
    sink = P["attn_sink"].reshape(hq, 1, 1)
    o_attn, lse, lse_row = attention_fwd(qr, kr, pkv, kcr, pkv_c, sink, hkv, hd, "attn_fwd")

    gf, gb = gate_fwd(plr, W["gate_f"], W["gate_b"], P["b_gate_f"], P["b_gate_b"], "gates")
    gfc, gbc = gate_fwd(plr_c, W["gate_f"], W["gate_b"], P["b_gate_f"], P["b_gate_b"], "gates_ctx")
    zero_state = jnp.zeros((gh, gdv, gdk), F32)
    _, st_cf, s_cf = gla_fwd(pgqk_c, pgv_c, gfc, zero_state, gh, False, "gla_ctx_f")
    _, st_cb, s_cb = gla_fwd(pgqk_c, pgv_c, gbc, zero_state, gh, True, "gla_ctx_b")
    of, st_f, _ = gla_fwd(pgqk, pgv, gf, s_cf, gh, False, "gla_f")
    og, st_b, _ = gla_fwd(pgqk, pgv, gb, s_cb, gh, True, "gla_b", o_add=of)
    ug = gla_out_fwd(og, prb, P["gla_norm"], gh, "gla_out")

    ya = matmul(o_attn, W["attn_o"], "nn", F32, "attn_o")
    yg = matmul(ug, W["gla_o"], "nn", F32, "gla_o")

    def merge_fn(ya, yg, ga, gb_):
        return _sig(ga) * ya + _sig(gb_) * yg

    z = rowwise(merge_fn, [ya, yg, (pgab, D, 0), (pgab, D, 1)], [], [(D, BF16)], [], 256, "merge")[0]
    mo = matmul(z, W["out"], "nn", F32, "w_out")

    def res_fn(x, mo, g1, gffn, sh2, sc2):
        x1 = x + g1 * mo
        return x1, x1 * _rstd(x1) * gffn * (1.0 + sc2) + sh2

    x1, h2 = rowwise(res_fn, [x, mo], [g1, P["g_ffn"], sh2, sc2], [(D, F32), (D, BF16)], [], 256, "res_mod2")
    u = matmul(h2, W["up"], "nn", F32, "w_up")
    cw3 = W["conv_w"].reshape(3, 1, -1)
    f = conv_swiglu_fwd(u, cw3, P["conv_b"], "conv_swiglu")
    fo = matmul(f, W["down"], "nn", F32, "w_down")

    def final_fn(x1, fo, tgt, g2):
        e = x1 + g2 * fo - tgt
        dy = e * (1.0 / D)
        lsum = jnp.sum(jnp.sum(e * e, axis=1, keepdims=True), axis=0, keepdims=True)
        return dy, dy * g2, jnp.broadcast_to(lsum, (1, 128)), jnp.sum(dy * fo, axis=0, keepdims=True)

    dy, dfo, lsum, dg2 = rowwise(final_fn, [x1, fo, tgt], [g2], [(D, F32), (D, BF16)], [(1, 128), (1, D)], 256, "loss")
    df = matmul(dfo, W["down"], "nt", BF16, "d_f")
    dw_down = matmul(f, dfo, "tn", BF16, "dw_down")
    du_a, du_g, dcw_a, dcw_g, dcb_a, dcb_g = conv_swiglu_bwd_fused(u, cw3, P["conv_b"], df, "conv_swiglu_bwd")
    Fh = du_a.shape[1]
    dh2 = matmul(du_a, W["up"][:, :Fh], "nt", F32, "d_h2_a")
    dh2 = matmul(du_g, W["up"][:, Fh:], "nt", F32, "d_h2_g", add=dh2)
    dw_up = [matmul(h2, du_a, "tn", BF16, "dw_up_a"), matmul(h2, du_g, "tn", BF16, "dw_up_g")]

    def mod2_bwd_fn(x1, dh, dy, mo, gffn, sc2, g1):
        r = _rstd(x1)
        n = x1 * r
        dyy = dh * (1.0 + sc2)
        dn = dyy * gffn
        dx1 = dy + r * (dn - n * jnp.mean(dn * n, axis=-1, keepdims=True))
        s0 = lambda a: jnp.sum(a, axis=0, keepdims=True)
        return dx1, dx1 * g1, s0(dyy * n), s0(dh), s0(dh * n * gffn), s0(dx1 * mo)

    dx1, dmo, dg_ffn, dsh2, dsc2, dg1 = rowwise(
        mod2_bwd_fn, [x1, dh2, dy, mo], [P["g_ffn"], sc2, g1], [(D, F32), (D, BF16)], [(1, D)] * 4, 128, "mod2_bwd")
    dz = matmul(dmo, W["out"], "nt", F32, "d_z")
    dw_out = matmul(z, dmo, "tn", BF16, "dw_out")

    def merge_bwd_fn(dz, ya, yg, ga, gb_):
        sa, sb = _sig(ga), _sig(gb_)
        return dz * sa, dz * sb, jnp.concatenate([dz * ya * sa * (1.0 - sa), dz * yg * sb * (1.0 - sb)], axis=1)

    dya, dyg, dpgab = rowwise(merge_bwd_fn, [dz, ya, yg, (pgab, D, 0), (pgab, D, 1)], [],
                              [(D, BF16), (D, BF16), (2 * D, BF16)], [], 128, "merge_bwd")
    do_attn = matmul(dya, W["attn_o"], "nt", BF16, "d_oattn")
    dw_attn_o = matmul(o_attn, dya, "tn", BF16, "dw_attn_o")
    dug = matmul(dyg, W["gla_o"], "nt", F32, "d_ug")
    dw_gla_o = matmul(ug, dyg, "tn", BF16, "dw_gla_o")
    dog, dprb, dgn = gla_out_bwd(og, prb, dug, P["gla_norm"], gh, "gla_out_bwd")

    dq1, dk1, dv1, dgf, ds_cf = gla_bwd(pgqk, pgv, gf, st_f, dog, zero_state, gh, False, "gla_f_bwd")
    dgq, dgk, dpgv, dgb, ds_cb = gla_bwd(pgqk, pgv, gb, st_b, dog, zero_state, gh, True, "gla_b_bwd",
                                          acc=(dq1, dk1, dv1))
    dpgqk = jnp.concatenate([dgq, dgk], axis=1)
    zero_do = jnp.zeros((L, gh * gdv), F32)
    cq1, ck1, cv1, dgfc, _ = gla_bwd(pgqk_c, pgv_c, gfc, st_cf, zero_do, ds_cf, gh, False, "gla_ctx_f_bwd")
    cq, ck, dpgv_c, dgbc, _ = gla_bwd(pgqk_c, pgv_c, gbc, st_cb, zero_do, ds_cb, gh, True, "gla_ctx_b_bwd",
                                      acc=(cq1, ck1, cv1))
    dpgqk_c = jnp.concatenate([cq, ck], axis=1)
    dplr, dwgf, dbgf, dwgb, dbgb = gate_bwd(plr, dgf, dgb, W["gate_f"], W["gate_b"], P["b_gate_f"], P["b_gate_b"], "gates_bwd")
    dplr_c, dwgf_c, dbgf_c, dwgb_c, dbgb_c = gate_bwd(plr_c, dgfc, dgbc, W["gate_f"], W["gate_b"], P["b_gate_f"],
                                                      P["b_gate_b"], "gates_ctx_bwd")

    dqr, dkc_r, dvc, dsink, dr_row = attention_bwd_q(qr, kr, pkv, kcr, pkv_c, sink, do_attn, o_attn, lse, hkv, hd,
                                                     "attn_bwd_q")
    dkr, dv = attention_bwd_kv(qr, kr, pkv, do_attn, lse_row, dr_row, hkv, hd, "attn_bwd_kv")
    dpq, dqn = norm_rope_bwd(pq, hq * hd, 0, dqr, P["q_norm"], cosf, sinf, hd, "qnorm_bwd")
    dpk, dkn = norm_rope_bwd(pkv, kvw, 0, dkr, P["k_norm"], cosf, sinf, hd, "knorm_bwd")
    dpk_c, dkn_c = norm_rope_bwd(pkv_c, kvw, 0, dkc_r, P["k_norm"], one_c, zero_c, hd, "knorm_ctx_bwd")
    dpkv = jnp.concatenate([dpk, dv], axis=1)
    dpkv_c = jnp.concatenate([dpk_c, dvc.astype(BF16)], axis=1)

    dw_q = matmul(h, dpq, "tn", BF16, "dw_q")
    dw_kv = matmul(h, dpkv, "tn", BF16, "dw_kv", add=matmul(hc, dpkv_c, "tn", F32, "dw_kv_ctx"))
    dw_gqk = matmul(h, dpgqk, "tn", BF16, "dw_gqk", add=matmul(hc, dpgqk_c, "tn", F32, "dw_gqk_ctx"))
    dw_gv = matmul(h, dpgv, "tn", BF16, "dw_gv", add=matmul(hc, dpgv_c, "tn", F32, "dw_gv_ctx"))
    dw_rb = matmul(h, dprb, "tn", BF16, "dw_rb")
    dw_lr = matmul(h, dplr, "tn", BF16, "dw_lr", add=matmul(hc, dplr_c, "tn", F32, "dw_lr_ctx"))
    dw_gab = matmul(h, dpgab, "tn", BF16, "dw_gab")
    lrw = P["lowrank"]
    dw_in = [dw_q, dw_kv, dw_gqk, dw_gv, dw_rb, dw_lr[:, :2 * lrw], dw_gab]

    dh = matmul(dpq, W["q"], "nt", F32, "dh_q")
    dh = matmul(dpkv, W["kv"], "nt", F32, "dh_kv", add=dh)
    dh = matmul(dpgqk, W["gqk"], "nt", F32, "dh_gqk", add=dh)
    dh = matmul(dpgv, W["gv"], "nt", F32, "dh_gv", add=dh)
    dh = matmul(dprb, W["rb"], "nt", F32, "dh_rb", add=dh)
    dh = matmul(dplr, W["lr"], "nt", F32, "dh_lr", add=dh)
    dh = matmul(dpgab, W["gab"], "nt", F32, "dh_gab", add=dh)
    dhc = matmul(dpkv_c, W["kv"], "nt", F32, "dhc_kv")
    dhc = matmul(dpgqk_c, W["gqk"], "nt", F32, "dhc_gqk", add=dhc)
    dhc = matmul(dpgv_c, W["gv"], "nt", F32, "dhc_gv", add=dhc)
    dhc = matmul(dplr_c, W["lr"], "nt", F32, "dhc_lr", add=dhc)

    def mod1_bwd_fn(x, dh, dres, g, sc):
        r = _rstd(x)
        n = x * r
        dyy = dh * (1.0 + sc)
        dn = dyy * g
        dx = dres + r * (dn - n * jnp.mean(dn * n, axis=-1, keepdims=True))
        s0 = lambda a: jnp.sum(a, axis=0, keepdims=True)
        return dx, s0(dyy * n), s0(dh), s0(dh * n * g)

    grad_x, dgmix, dsh1, dsc1 = rowwise(mod1_bwd_fn, [x, dh, dx1], [P["g_mix"], sc1], [(D, F32)], [(1, D)] * 3,
                                        128, "mod1_bwd")
    _, dgmix_c, dcsh1, dcsc1 = rowwise(mod1_bwd_fn, [ctx, dhc, jnp.zeros_like(ctx)], [P["g_mix"], csc1], [(D, F32)],
                                       [(1, D)] * 3, 128, "mod1_ctx_bwd")

    zD = jnp.zeros((1, D), F32)
    grads = dict(
        w_in=dw_in, w_attn_o=dw_attn_o, w_gla_o=dw_gla_o, w_out=dw_out, w_up=dw_up, w_down=dw_down,
        dmod_x=jnp.concatenate([dsh1, dsc1, dg1, dsh2, dsc2, dg2], axis=1),
        dmod_c=jnp.concatenate([dcsh1, dcsc1, zD, zD, zD, zD], axis=1),
        g_mix=dgmix + dgmix_c, q_norm=dqn, k_norm=dkn + dkn_c, attn_sink=dsink.reshape(1, hq),
        w_gate_f=(dwgf + dwgf_c)[:lrw], b_gate_f=dbgf + dbgf_c,
        w_gate_b=(dwgb + dwgb_c)[lrw:2 * lrw], b_gate_b=dbgb + dbgb_c,
        gla_norm=dgn, g_ffn=dg_ffn,
        conv_w=jnp.concatenate([dcw_a, dcw_g], axis=2).reshape(3, -1),
        conv_b=jnp.concatenate([dcb_a, dcb_g], axis=1),
    )
    return lsum[0, 0], grad_x, grads


SMALL_REPL = ("c_ctx", "b_mod", "g_mix", "q_norm", "k_norm", "attn_sink", "b_gate_f", "b_gate_b", "gla_norm", "g_ffn",
              "conv_b")
SMALL_SHARD = ("w_gate_f", "w_gate_b", "conv_w")
ORDER = ("c_ctx", "w_mod", "b_mod", "g_mix", "w_in", "q_norm", "k_norm", "attn_sink", "w_gate_f", "b_gate_f",
         "w_gate_b", "b_gate_b", "gla_norm", "w_attn_o", "w_gla_o", "w_out", "g_ffn", "w_up", "conv_w", "conv_b",
         "w_down")


def kernel(x, c, ctx, c_ctx, w_mod, b_mod, g_mix, w_in, q_norm, k_norm, attn_sink, w_gate_f, b_gate_f, w_gate_b, b_gate_b, gla_norm, w_attn_o, w_gla_o, w_out, g_ffn, w_up, conv_w, conv_b, w_down, loss_target, m_c_ctx, m_w_mod, m_b_mod, m_g_mix, m_w_in, m_q_norm, m_k_norm, m_attn_sink, m_w_gate_f, m_b_gate_f, m_w_gate_b, m_b_gate_b, m_gla_norm, m_w_attn_o, m_w_gla_o, m_w_out, m_g_ffn, m_w_up, m_conv_w, m_conv_b, m_w_down, v_c_ctx, v_w_mod, v_b_mod, v_g_mix, v_w_in, v_q_norm, v_k_norm, v_attn_sink, v_w_gate_f, v_b_gate_f, v_w_gate_b, v_b_gate_b, v_gla_norm, v_w_attn_o, v_w_gla_o, v_w_out, v_g_ffn, v_w_up, v_conv_w, v_conv_b, v_w_down):
    loc = dict(locals())
    Wt = {n: loc[n] for n in ORDER}
    Mt = {n: loc["m_" + n] for n in ORDER}
    Vt = {n: loc["v_" + n] for n in ORDER}
    me = 4 * lax.axis_index("x") + 2 * lax.axis_index("y") + lax.axis_index("c")

    D = x.shape[-1]
    hd = q_norm.shape[-1]
    hq = attn_sink.shape[-1]
    gdv = gla_norm.shape[-1]
    gh = D // gdv
    gdk = D // 2 // gh
    lrw = w_gate_f.shape[1]
    in_w = NDEV * w_in.shape[-1]
    kvw = (in_w - hq * hd - 2 * gh * gdk - 2 * gh * gdv - 2 * lrw - 2 * D) // 2
    hkv = kvw // hd
    gcols = w_gate_f.shape[-1]
    F2 = NDEV * w_up.shape[-1]
    mcols = w_mod.shape[-1]

    x2, ctx2, tgt2 = x[0], ctx[0], loss_target[0]

    c_all = exchange([jnp.pad(c, ((0, 7), (0, 0)))], True, "gather_c")[0][:, 0, :]
    c9 = jnp.concatenate([c_all, c_ctx[None, :], jnp.zeros((7, D), F32)], axis=0)
    s9 = rowwise(lambda a: a * _sig(a), [c9], [], [(D, F32)], [], 16, "silu_c")[0]
    bias = jnp.broadcast_to(lax.dynamic_slice_in_dim(b_mod, me * mcols, mcols, axis=1), (16, mcols))
    mod_cols = matmul(s9, w_mod[0], "nn", F32, "mod_cols", add=bias)
    mod_all = exchange([mod_cols], True, "gather_mod")[0]
    mod_all = jnp.transpose(mod_all, (1, 0, 2)).reshape(16, NDEV * mcols)
    mod_me = lax.dynamic_slice_in_dim(mod_all, me, 1, axis=0)
    mod = [mod_me[:, i * D:(i + 1) * D] for i in range(6)]
    modc = [mod_all[8:9, i * D:(i + 1) * D] for i in range(2)]

    o3 = jnp.stack([w_attn_o[0], w_gla_o[0], w_out[0]]).astype(BF16)
    small_w = pack([w_gate_f[0], w_gate_b[0], conv_w[0]])
    g_in, g_o3, g_up, g_down, g_small = gather_two_level(
        [w_in[0].astype(BF16), o3, w_up[0].astype(BF16), w_down[0].astype(BF16), small_w], "gather_w")
    seg = segments_from_blocks(g_in, [hq * hd, 2 * kvw, 2 * gh * gdk, gh * gdv, gh * gdv, 2 * lrw, 2 * D])
    small_parts = [unpack(g_small[j], [w_gate_f[0].shape, w_gate_b[0].shape, conv_w[0].shape]) for j in range(NDEV)]
    wgf = jnp.concatenate([p[0] for p in small_parts], axis=1)
    wgb = jnp.concatenate([p[1] for p in small_parts], axis=1)
    cw_full = jnp.concatenate([p[2] for p in small_parts], axis=1)
    o3f = [g_o3[:, i].reshape(-1, D) for i in range(3)]
    W = dict(
        q=seg[0], kv=seg[1], gqk=seg[2], gv=seg[3], rb=seg[4],
        lr=jnp.pad(seg[5], ((0, 0), (0, 128 - 2 * lrw))), gab=seg[6],
        gate_f=jnp.pad(wgf, ((0, 128 - lrw), (0, 0))),
        gate_b=jnp.pad(wgb, ((lrw, 128 - 2 * lrw), (0, 0))),
        attn_o=o3f[0], gla_o=o3f[1], out=o3f[2],
        up=jnp.concatenate([g_up[j] for j in range(NDEV)], axis=1),
        down=g_down.reshape(-1, D),
        conv_w=cw_full,
    )
    P = dict(hd=hd, hq=hq, hkv=hkv, gh=gh, lowrank=lrw, g_mix=g_mix, q_norm=q_norm, k_norm=k_norm, attn_sink=attn_sink,
             b_gate_f=b_gate_f, b_gate_b=b_gate_b, gla_norm=gla_norm, g_ffn=g_ffn, conv_b=conv_b)

    lsum, grad_x, G = local_step(x2, ctx2, tgt2, mod, modc, W, P)
    loss = lax.psum(0.5 * lsum / D, ("x", "y", "c"))

    dm = exchange([jnp.concatenate([G["dmod_x"], G["dmod_c"], jnp.zeros((6, 6 * D), F32)], axis=0)], True,
                  "gather_dmod")[0]
    dmc = reduce_parts(dm[:, 1:2, :].reshape(NDEV, 6 * D // 128, 128), "sum_dmod_ctx").reshape(1, 6 * D)
    dM = jnp.concatenate([dm[:, 0, :], dmc, jnp.zeros((7, 6 * D), F32)], axis=0)
    dM_cols = lax.dynamic_slice_in_dim(dM, me * mcols, mcols, axis=1)
    g_w_mod = matmul(s9, dM_cols, "tn", F32, "dw_mod")
    g_b_mod = reduce_parts(dM.reshape(16, 6 * D // 128, 128), "sum_db_mod").reshape(1, 6 * D)
    dsc = matmul(dM_cols[8:16], w_mod[0], "nt", F32, "d_silu_ctx")
    cc = jnp.broadcast_to(c_ctx[None, :], (8, D))

    def dsilu_fn(d, a):
        sg = _sig(a)
        return d * sg * (1.0 + a * (1.0 - sg))

    g_cctx_part = rowwise(dsilu_fn, [dsc, cc], [], [(D, F32)], [], 8, "d_c_ctx")[0][0:1]

    small_names = ("c_ctx", "g_mix", "q_norm", "k_norm", "attn_sink", "b_gate_f", "b_gate_b", "gla_norm", "g_ffn",
                   "conv_b", "w_gate_f", "w_gate_b", "conv_w")
    G["c_ctx"] = g_cctx_part
    sm_shapes = [G[n].shape for n in small_names]
    sm_all = exchange([pack([G[n] for n in small_names])], True, "gather_small_grads")[0]
    sm_tot = unpack(reduce_parts(sm_all, "sum_small_grads"), sm_shapes)
    gs = dict(zip(small_names, sm_tot))
    gs["b_mod"] = g_b_mod
    gs["w_gate_f"] = lax.dynamic_slice_in_dim(gs["w_gate_f"], me * gcols, gcols, axis=1)
    gs["w_gate_b"] = lax.dynamic_slice_in_dim(gs["w_gate_b"], me * gcols, gcols, axis=1)
    ccols = conv_w.shape[-1]
    gs["conv_w"] = lax.dynamic_slice_in_dim(gs["conv_w"], me * ccols, ccols, axis=1)

    orows = w_attn_o.shape[1]
    s_in = blocks_from_segments(G["w_in"], w_in.shape[-1])
    s_o3 = jnp.concatenate([rows_to_blocks(G["w_attn_o"]), rows_to_blocks(G["w_gla_o"]), rows_to_blocks(G["w_out"])],
                           axis=1)
    s_up = blocks_from_segments(G["w_up"], w_up.shape[-1])
    s_down = rows_to_blocks(G["w_down"])
    r_in, r_o3, r_up, r_down = scatter_reduce([s_in, s_o3, s_up, s_down], "scatter_grads")

    out = {}
    out["w_in"] = adam_reduce(r_in, w_in[0], m_w_in[0], v_w_in[0], "adam_w_in")
    o3w = jnp.concatenate([w_attn_o[0], w_gla_o[0], w_out[0]], axis=0)
    o3m = jnp.concatenate([m_w_attn_o[0], m_w_gla_o[0], m_w_out[0]], axis=0)
    o3v = jnp.concatenate([v_w_attn_o[0], v_w_gla_o[0], v_w_out[0]], axis=0)
    ro3 = adam_reduce(r_o3, o3w, o3m, o3v, "adam_o3")
    for i, n in enumerate(("w_attn_o", "w_gla_o", "w_out")):
        out[n] = [a[i * orows:(i + 1) * orows] for a in ro3]
    out["w_up"] = adam_reduce(r_up, w_up[0], m_w_up[0], v_w_up[0], "adam_w_up")
    out["w_down"] = adam_reduce(r_down, w_down[0], m_w_down[0], v_w_down[0], "adam_w_down")
    out["w_mod"] = adam_reduce(g_w_mod[None], w_mod[0], m_w_mod[0], v_w_mod[0], "adam_w_mod")
    sm_names = SMALL_REPL + SMALL_SHARD
    shapes = [Wt[n].shape for n in sm_names]
    rs = adam_reduce(pack([gs[n] for n in sm_names])[None], pack([Wt[n] for n in sm_names]),
                     pack([Mt[n] for n in sm_names]), pack([Vt[n] for n in sm_names]), "adam_small")
    rs = [unpack(a, shapes) for a in rs]
    for i, n in enumerate(sm_names):
        out[n] = [a[i] for a in rs]

    res = [loss, grad_x[None]]
    for k in range(4):
        for n in ORDER:
            res.append(out[n][k].reshape(Wt[n].shape))
    return tuple(res)
```

```python
import jax
import jax.numpy as jnp
import numpy as np
from jax import lax
from jax.experimental import pallas as pl
from jax.experimental.pallas import tpu as pltpu

F32 = jnp.float32
BF16 = jnp.bfloat16

NDEV = 8
NCHIP = 4
EPS = 1e-6
WINDOW = 128
BLOCK = 128
GRID_W = 64
ROPE_THETA = 10000.0
GLA_CHUNK = 128
GLA_GATE_NORM = 16.0
ADAM_LR = 0.001
ADAM_B1 = 0.9
ADAM_B2 = 0.999
ADAM_EPS = 1e-08
ADAM_WD = 0.01
ADAM_STEP = 10
V7X_VMEM_LIMIT = 56 * 1024 * 1024
MATMUL_VMEM_BUDGET = 40 * 1024 * 1024
MATMUL_TILES = (1024, 1408, 512, 256, 128)
NEG = -1e30

NN = ((1,), (0,))
NT = ((1,), (1,))
TN = ((0,), (0,))


def _dot(a, b, dims):
    return lax.dot_general(a, b, (dims, ((), ())), preferred_element_type=F32)


def _cp(sem):
    return pltpu.CompilerParams(dimension_semantics=sem, vmem_limit_bytes=V7X_VMEM_LIMIT)


def _pick(n, cands):
    for c in cands:
        if n % c == 0:
            return c
    return n


def _sig(x):
    return 1.0 / (1.0 + jnp.exp(-x))


def _sig_tanh(x):
    return 0.5 * jnp.tanh(0.5 * x) + 0.5


def _rstd(x):
    return lax.rsqrt(jnp.mean(x * x, axis=-1, keepdims=True) + EPS)


_ANY = pl.BlockSpec(memory_space=pl.ANY)


def _place():
    return lax.axis_index("x"), lax.axis_index("y"), lax.axis_index("c")


def exchange(srcs, bcast, name, group="all"):
    n = len(srcs)
    ndev = NDEV if group == "all" else NCHIP
    ks = tuple(range(1, NDEV)) if group == "all" else (2, 4, 6)
    out_shape = [jax.ShapeDtypeStruct((ndev,) + (s.shape if bcast else s.shape[1:]), s.dtype) for s in srcs]

    def body(*refs):
        src, dst = refs[:n], refs[n:2 * n]
        send_sems, recv_sems, loc_sems = refs[2 * n:]
        x, y, c = _place()

        def idx(px, py, pc):
            return 4 * px + 2 * py + pc if group == "all" else 2 * px + py

        me = idx(x, y, c)
        copies = []
        for a in range(n):
            cp = pltpu.make_async_copy(src[a] if bcast else src[a].at[me], dst[a].at[me], loc_sems.at[a])
            cp.start()
            copies.append(cp)
        for s, k in enumerate(ks):
            px, py, pc = x ^ ((k >> 2) & 1), y ^ ((k >> 1) & 1), c ^ (k & 1)
            for a in range(n):
                cp = pltpu.make_async_remote_copy(
                    src_ref=src[a] if bcast else src[a].at[idx(px, py, pc)],
                    dst_ref=dst[a].at[me],
                    send_sem=send_sems.at[a, s],
                    recv_sem=recv_sems.at[a, s],
                    device_id=(px, py, pc),
                    device_id_type=pl.DeviceIdType.MESH,
                )
                cp.start()
                copies.append(cp)
        for cp in copies:
            cp.wait()

    return pl.pallas_call(
        body,
        out_shape=out_shape,
        in_specs=[_ANY] * n,
        out_specs=[_ANY] * n,
        scratch_shapes=[
            pltpu.SemaphoreType.DMA((n, len(ks))),
            pltpu.SemaphoreType.DMA((n, len(ks))),
            pltpu.SemaphoreType.DMA((n,)),
        ],
        name=name,
    )(*srcs)


def gather_two_level(srcs, name):
    n = len(srcs)
    out_shape = [jax.ShapeDtypeStruct((NDEV,) + s.shape, s.dtype) for s in srcs]

    def body(*refs):
        src, dst = refs[:n], refs[n:2 * n]
        send_sems, recv_sems, loc_sems = refs[2 * n:]
        x, y, c = _place()
        me = 4 * x + 2 * y + c
        sib = (x, y, 1 - c)
        first = (x ^ (1 - c), y ^ c)
        second = (x ^ c, y ^ (1 - c))
        diag = (x ^ 1, y ^ 1)

        def row(chip, core):
            return 4 * chip[0] + 2 * chip[1] + core

        def copy(a, s, block, to, from_src=False):
            return pltpu.make_async_remote_copy(
                src_ref=src[a] if from_src else dst[a].at[block], dst_ref=dst[a].at[block],
                send_sem=send_sems.at[a, s], recv_sem=recv_sems.at[a, s],
                device_id=to, device_id_type=pl.DeviceIdType.MESH)

        local = [pltpu.make_async_copy(src[a], dst[a].at[me], loc_sems.at[a]) for a in range(n)]
        sent = [copy(a, 0, me, sib, True) for a in range(n)]
        sent += [copy(a, 1, me, (*first, c), True) for a in range(n)]
        sent += [copy(a, 2, me, (*second, c), True) for a in range(n)]
        for cp in local + sent:
            cp.start()
        for a in range(n):
            copy(a, 1, row(first, c), (*first, c)).wait_recv()
            for cp in (copy(a, 3, row(first, c), (*second, c)), copy(a, 5, row(first, c), sib)):
                cp.start()
                sent.append(cp)
        for a in range(n):
            copy(a, 2, row(second, c), (*second, c)).wait_recv()
            cp = copy(a, 4, row(second, c), sib)
            cp.start()
            sent.append(cp)
        for a in range(n):
            copy(a, 3, row(diag, c), (*second, c)).wait_recv()
            cp = copy(a, 6, row(diag, c), sib)
            cp.start()
            sent.append(cp)
        for a in range(n):
            copy(a, 0, row((x, y), 1 - c), sib).wait_recv()
            copy(a, 4, row(first, 1 - c), sib).wait_recv()
            copy(a, 5, row(second, 1 - c), sib).wait_recv()
            copy(a, 6, row(diag, 1 - c), sib).wait_recv()
        for cp in local:
            cp.wait()
        for cp in sent:
            cp.wait_send()

    return pl.pallas_call(
        body,
        out_shape=out_shape,
        in_specs=[_ANY] * n,
        out_specs=[_ANY] * n,
        scratch_shapes=[
            pltpu.SemaphoreType.DMA((n, NDEV - 1)),
            pltpu.SemaphoreType.DMA((n, NDEV - 1)),
            pltpu.SemaphoreType.DMA((n,)),
        ],
        name=name,
    )(*srcs)


def _chip_across(core, da, db):
    x, y, _ = _place()
    return x ^ (da * (1 - core) + db * core), y ^ (db * (1 - core) + da * core)


def pair_swap(srcs, name, axis="c"):
    n = len(srcs)

    def body(*refs):
        src, dst = refs[:n], refs[n:2 * n]
        send_sems, recv_sems = refs[2 * n:]
        x, y, c = _place()
        partner = {"c": (x, y, 1 - c), "first": (*_chip_across(c, 1, 0), c), "second": (*_chip_across(c, 0, 1), c)}[axis]
        copies = []
        for a in range(n):
            cp = pltpu.make_async_remote_copy(
                src_ref=src[a], dst_ref=dst[a], send_sem=send_sems.at[a], recv_sem=recv_sems.at[a],
                device_id=partner, device_id_type=pl.DeviceIdType.MESH)
            cp.start()
            copies.append(cp)
        for cp in copies:
            cp.wait()

    return pl.pallas_call(
        body,
        out_shape=[jax.ShapeDtypeStruct(s.shape, s.dtype) for s in srcs],
        in_specs=[_ANY] * n,
        out_specs=[_ANY] * n,
        scratch_shapes=[pltpu.SemaphoreType.DMA((n,)), pltpu.SemaphoreType.DMA((n,))],
        name=name,
    )(*srcs)


_OFFSETS = ((0, 0), (0, 1), (1, 0), (1, 1))


def sibling_swap_blocks(blocks, name):
    n = len(blocks)

    def body(*refs):
        src, dst = refs[:n], refs[n:2 * n]
        send_sems, recv_sems = refs[2 * n:]
        x, y, c = _place()
        copies = []
        for a in range(n):
            for j, (da, db) in enumerate(_OFFSETS):
                px, py = _chip_across(1 - c, da, db)
                cp = pltpu.make_async_remote_copy(
                    src_ref=src[a].at[4 * px + 2 * py + (1 - c)], dst_ref=dst[a].at[j],
                    send_sem=send_sems.at[a, j], recv_sem=recv_sems.at[a, j],
                    device_id=(x, y, 1 - c), device_id_type=pl.DeviceIdType.MESH)
                cp.start()
                copies.append(cp)
        for cp in copies:
            cp.wait()

    return pl.pallas_call(
        body,
        out_shape=[jax.ShapeDtypeStruct((4,) + b.shape[1:], b.dtype) for b in blocks],
        in_specs=[_ANY] * n,
        out_specs=[_ANY] * n,
        scratch_shapes=[pltpu.SemaphoreType.DMA((n, 4)), pltpu.SemaphoreType.DMA((n, 4))],
        name=name,
    )(*blocks)


def add_own_blocks(blocks, got, name):
    _, R, C = blocks.shape
    tile = _pick(R, (256, 128, 64))

    def body(*refs):
        for j in range(4):
            refs[8 + j][...] = (refs[j][...].astype(F32) + refs[4 + j][...].astype(F32)).astype(refs[8 + j].dtype)

    def own(da, db):
        def index(i):
            c = lax.axis_index("c")
            px, py = _chip_across(c, da, db)
            return 4 * px + 2 * py + c, i, 0
        return pl.BlockSpec((None, tile, C), index)

    return pl.pallas_call(
        body,
        grid=(R // tile,),
        in_specs=[own(da, db) for da, db in _OFFSETS]
        + [pl.BlockSpec((None, tile, C), lambda i, j=j: (j, i, 0)) for j in range(4)],
        out_specs=[pl.BlockSpec((tile, C), lambda i: (i, 0))] * 4,
        out_shape=[jax.ShapeDtypeStruct((R, C), blocks.dtype)] * 4,
        compiler_params=_cp(("parallel",)),
        name=name,
    )(blocks, blocks, blocks, blocks, got, got, got, got)


def scatter_reduce(blocks, name):
    def add(n_out, ins, label):
        fn = lambda *a: [a[i].astype(F32) + a[n_out + i].astype(F32) for i in range(n_out)]
        rows, cols = ins[0].shape
        return rowwise(fn, ins, [], [(cols, ins[0].dtype)] * n_out, [], _pick(rows, (256, 128, 64)), label)

    nb = len(blocks)
    got = sibling_swap_blocks(blocks, name + "_d2d")
    q = [add_own_blocks(blocks[i], got[i], f"{name}_sum0_{i}") for i in range(nb)]
    r1 = pair_swap([q[i][j] for i in range(nb) for j in (2, 3)], name + "_ici1", "first")
    k = [add(2, [q[i][0], q[i][1], r1[2 * i], r1[2 * i + 1]], f"{name}_sum1_{i}") for i in range(nb)]
    r2 = pair_swap([k[i][1] for i in range(nb)], name + "_ici2", "second")
    return [jnp.stack([k[i][0], r2[i]]) for i in range(nb)]


def matmul(a, b, mode, out_dtype, name, add=None):
    if mode == "nn":
        (M, K), N = a.shape, b.shape[1]
    elif mode == "nt":
        (M, K), N = a.shape, b.shape[0]
    else:
        (K, M), N = a.shape, b.shape[1]
    tm = _pick(M, MATMUL_TILES)
    tn = _pick(N, MATMUL_TILES)
    osz = jnp.dtype(out_dtype).itemsize

    def vmem_bytes(tk):
        ops = 2 * tk * (tm * a.dtype.itemsize + tn * b.dtype.itemsize)
        return ops + tm * tn * (2 * osz + (4 if tk < K else 0) + (8 if add is not None else 0))

    tk = next((t for t in (K, 2816, 2048, 1408, 1024, 512, 256, 128) if K % t == 0 and vmem_bytes(t) <= MATMUL_VMEM_BUDGET), K)
    nk = K // tk
    dims = {"nn": NN, "nt": NT, "tn": TN}[mode]

    def body(*refs):
        if add is None:
            a_ref, b_ref, o_ref = refs[:3]
            c_ref = None
        else:
            a_ref, b_ref, c_ref, o_ref = refs[:4]

        def prod():
            return _dot(a_ref[...].astype(BF16), b_ref[...].astype(BF16), dims)

        def finish(r):
            if c_ref is not None:
                r = r + c_ref[...].astype(F32)
            o_ref[...] = r.astype(o_ref.dtype)

        if nk == 1:
            finish(prod())
            return
        acc = refs[-1]
        k = pl.program_id(2)

        @pl.when(k == 0)
        def _():
            acc[...] = prod()

        if nk > 2:
            @pl.when((k > 0) & (k < nk - 1))
            def _():
                acc[...] += prod()

        @pl.when(k == nk - 1)
        def _():
            finish(acc[...] + prod())

    a_spec = pl.BlockSpec((tk, tm), lambda i, j, k: (k, i)) if mode == "tn" else pl.BlockSpec((tm, tk), lambda i, j, k: (i, k))
    b_spec = pl.BlockSpec((tn, tk), lambda i, j, k: (j, k)) if mode == "nt" else pl.BlockSpec((tk, tn), lambda i, j, k: (k, j))
    o_spec = pl.BlockSpec((tm, tn), lambda i, j, k: (i, j))
    ins, specs = [a, b], [a_spec, b_spec]
    if add is not None:
        ins.append(add)
        specs.append(o_spec)
    return pl.pallas_call(
        body,
        grid=(M // tm, N // tn, nk),
        in_specs=specs,
        out_specs=o_spec,
        out_shape=jax.ShapeDtypeStruct((M, N), out_dtype),
        scratch_shapes=[pltpu.VMEM((tm, tn), F32)] if nk > 1 else [],
        compiler_params=_cp(("parallel", "parallel", "arbitrary")),
        name=name,
    )(*ins)


def rowwise(fn, tiled, full, out_tiled, out_acc, tile, name):
    tiled = [t if isinstance(t, tuple) else (t, t.shape[1], 0) for t in tiled]
    rows = tiled[0][0].shape[0]
    tile = min(tile, rows)
    assert rows % tile == 0
    nt, nf, no = len(tiled), len(full), len(out_tiled)

    def body(*refs):
        ins = [r[...] for r in refs[:nt + nf]]
        res = fn(*ins)
        if not isinstance(res, (tuple, list)):
            res = (res,)
        outs = refs[nt + nf:]
        for r, v in zip(outs[:no], res[:no]):
            r[...] = v.astype(r.dtype)
        if out_acc:
            @pl.when(pl.program_id(0) == 0)
            def _():
                for r in outs[no:]:
                    r[...] = jnp.zeros_like(r)

            for r, v in zip(outs[no:], res[no:]):
                r[...] += v

    in_specs = [pl.BlockSpec((tile, w), lambda i, cb=cb: (i, cb)) for (_, w, cb) in tiled]
    in_specs += [pl.BlockSpec(f.shape, lambda i, nd=f.ndim: (0,) * nd) for f in full]
    out_specs = [pl.BlockSpec((tile, w), lambda i: (i, 0)) for (w, _) in out_tiled]
    out_specs += [pl.BlockSpec(s, lambda i, nd=len(s): (0,) * nd) for s in out_acc]
    out_shape = [jax.ShapeDtypeStruct((rows, w), dt) for (w, dt) in out_tiled]
    out_shape += [jax.ShapeDtypeStruct(s, F32) for s in out_acc]
    res = pl.pallas_call(
        body,
        grid=(rows // tile,),
        in_specs=in_specs,
        out_specs=out_specs,
        out_shape=out_shape,
        compiler_params=_cp(("arbitrary",) if out_acc else ("parallel",)),
        name=name,
    )(*[t[0] for t in tiled], *full)
    return res


def adam_reduce(parts, w, m, v, name):
    P, R, C = parts.shape
    tr = _pick(R, (256, 128, 64, 32, 16, 8))
    c1 = 1.0 - ADAM_B1 ** ADAM_STEP
    c2 = 1.0 - ADAM_B2 ** ADAM_STEP

    def body(p_ref, w_ref, m_ref, v_ref, g_ref, d_ref, nm_ref, nv_ref):
        g = p_ref[0].astype(F32)
        for j in range(1, P):
            g = g + p_ref[j].astype(F32)
        mm = ADAM_B1 * m_ref[...] + (1.0 - ADAM_B1) * g
        vv = ADAM_B2 * v_ref[...] + (1.0 - ADAM_B2) * (g * g)
        m_hat = mm / c1
        v_hat = vv / c2
        g_ref[...] = g
        d_ref[...] = -ADAM_LR * (m_hat / (jnp.sqrt(v_hat) + ADAM_EPS) + ADAM_WD * w_ref[...])
        nm_ref[...] = mm
        nv_ref[...] = vv

    spec = pl.BlockSpec((tr, C), lambda i: (i, 0))
    return pl.pallas_call(
        body,
        grid=(R // tr,),
        in_specs=[pl.BlockSpec((P, tr, C), lambda i: (0, i, 0)), spec, spec, spec],
        out_specs=[spec] * 4,
        out_shape=[jax.ShapeDtypeStruct((R, C), F32)] * 4,
        compiler_params=_cp(("parallel",)),
        name=name,
    )(parts, w, m, v)


def reduce_parts(parts, name):
    P, R, C = parts.shape
    tr = _pick(R, (64, 32, 16, 8))

    def body(p_ref, g_ref):
        g = p_ref[0]
        for j in range(1, P):
            g = g + p_ref[j]
        g_ref[...] = g

    return pl.pallas_call(
        body,
        grid=(R // tr,),
        in_specs=[pl.BlockSpec((P, tr, C), lambda i: (0, i, 0))],
        out_specs=pl.BlockSpec((tr, C), lambda i: (i, 0)),
        out_shape=jax.ShapeDtypeStruct((R, C), F32),
        compiler_params=_cp(("parallel",)),
        name=name,
    )(parts)


def pack(arrs):
    flat = jnp.concatenate([a.reshape(-1).astype(F32) for a in arrs])
    n = flat.shape[0]
    padded = -(-n // 1024) * 1024
    return jnp.pad(flat, (0, padded - n)).reshape(padded // 128, 128)


def blocks_from_segments(segs, ncols):
    offs = np.cumsum([0] + [s.shape[1] for s in segs]).tolist()
    blocks = []
    for j in range(NDEV):
        lo, hi = j * ncols, (j + 1) * ncols
        parts = [s[:, max(lo, o) - o:min(hi, o + s.shape[1]) - o]
                 for s, o in zip(segs, offs[:-1]) if max(lo, o) < min(hi, o + s.shape[1])]
        blocks.append(jnp.concatenate(parts, axis=1) if len(parts) > 1 else parts[0])
    return jnp.stack(blocks)


def rows_to_blocks(g):
    return g.reshape(NDEV, -1, g.shape[1])


def segments_from_blocks(g, widths):
    ncols = g.shape[2]
    offs = np.cumsum([0] + list(widths)).tolist()
    out = []
    for o, w in zip(offs[:-1], widths):
        parts = [g[j][:, max(j * ncols, o) - j * ncols:min((j + 1) * ncols, o + w) - j * ncols]
                 for j in range(NDEV) if max(j * ncols, o) < min((j + 1) * ncols, o + w)]
        out.append(jnp.concatenate(parts, axis=1) if len(parts) > 1 else parts[0])
    return out


def unpack(slab, shapes):
    flat = slab.reshape(-1)
    out, off = [], 0
    for s in shapes:
        size = int(np.prod(s))
        out.append(flat[off:off + size].reshape(s))
        off += size
    return out


def modulate_fwd(x, g, sh, sc, name):
    def fn(x, g, sh, sc):
        return x * _rstd(x) * g * (1.0 + sc) + sh

    return rowwise(fn, [x], [g, sh, sc], [(x.shape[1], BF16)], [], 256, name)[0]


def norm_rope_fwd(p, width, cb, w, cosf, sinf, hd, name):
    nh = width // hd

    def fn(x, cosf, sinf, w):
        outs = []
        for h in range(nh):
            xh = x[:, h * hd:(h + 1) * hd]
            y = xh * _rstd(xh) * w
            outs.append(y * cosf + pltpu.roll(y, hd // 2, 1) * sinf)
        return jnp.concatenate(outs, axis=1) if nh > 1 else outs[0]

    return rowwise(fn, [(p, width, cb), cosf, sinf], [w], [(width, BF16)], [], 256, name)[0]


def norm_rope_bwd(p, width, cb, d, w, cosf, sinf, hd, name):
    nh = width // hd

    def fn(x, d, cosf, sinf, w):
        outs = []
        dw = jnp.zeros((1, hd), F32)
        for h in range(nh):
            xh = x[:, h * hd:(h + 1) * hd]
            dh = d[:, h * hd:(h + 1) * hd].astype(F32)
            r = _rstd(xh)
            n = xh * r
            dy = dh * cosf + pltpu.roll(dh * sinf, hd // 2, 1)
            dw = dw + jnp.sum(dy * n, axis=0, keepdims=True)
            dn = dy * w
            outs.append(r * (dn - n * jnp.mean(dn * n, axis=-1, keepdims=True)))
        return (jnp.concatenate(outs, axis=1) if nh > 1 else outs[0]), dw

    return rowwise(fn, [(p, width, cb), d, cosf, sinf], [w], [(width, BF16)], [(1, hd)], 256, name)


def attention_fwd(qr, kr, pkv, kcr, pkv_c, sink, hkv, hd, name):
    T, L = qr.shape[0], kcr.shape[0]
    G = qr.shape[1] // (hkv * hd)
    nb = T // BLOCK
    scale = hd ** -0.5

    def body(q_ref, kp, kc, kn, vp, vc, vn, ck_ref, cv_ref, sink_ref, o_ref, lse_ref, lser_ref):
        i = pl.program_id(1)
        kwin = jnp.concatenate([kp[...], kc[...], kn[...]], axis=0)
        vwin = jnp.concatenate([vp[...], vc[...], vn[...]], axis=0).astype(BF16)
        ck, cv = ck_ref[...], cv_ref[...].astype(BF16)
        row = lax.broadcasted_iota(jnp.int32, (BLOCK, 3 * BLOCK), 0)
        col = lax.broadcasted_iota(jnp.int32, (BLOCK, 3 * BLOCK), 1)
        rel = col - BLOCK - row
        valid = (jnp.abs(rel) <= WINDOW) & ((col >= BLOCK) | (i > 0)) & ((col < 2 * BLOCK) | (i < nb - 1))
        R = range(G)
        qa = q_ref[...]
        qs = [qa[:, g * hd:(g + 1) * hd] for g in R]
        sks = [sink_ref[g] for g in R]
        ss = [jnp.where(valid, _dot(qs[g], kwin, NT) * scale, NEG) for g in R]
        scs = [_dot(qs[g], ck, NT) * scale for g in R]
        ms = [jnp.maximum(jnp.maximum(jnp.max(ss[g], axis=1, keepdims=True), jnp.max(scs[g], axis=1, keepdims=True)),
                          sks[g]) for g in R]
        ps = [jnp.exp(ss[g] - ms[g]) for g in R]
        pcs = [jnp.exp(scs[g] - ms[g]) for g in R]
        nums = [_dot(ps[g].astype(BF16), vwin, NN) + _dot(pcs[g].astype(BF16), cv, NN) for g in R]
        dens = [jnp.exp(sks[g] - ms[g]) + jnp.sum(ps[g], axis=1, keepdims=True) + jnp.sum(pcs[g], axis=1, keepdims=True)
                for g in R]
        o_ref[...] = jnp.concatenate([(nums[g] / dens[g]).astype(o_ref.dtype) for g in R], axis=1)
        eye = (lax.broadcasted_iota(jnp.int32, (BLOCK, BLOCK), 0)
               == lax.broadcasted_iota(jnp.int32, (BLOCK, BLOCK), 1)).astype(F32)
        for g in R:
            lg = ms[g] + jnp.log(dens[g])
            lse_ref[g] = lg
            lser_ref[g] = jnp.sum(lg * eye, axis=0, keepdims=True)

    kv_specs = [
        pl.BlockSpec((BLOCK, hd), lambda h, i: (jnp.maximum(i - 1, 0), h)),
        pl.BlockSpec((BLOCK, hd), lambda h, i: (i, h)),
        pl.BlockSpec((BLOCK, hd), lambda h, i: (jnp.minimum(i + 1, nb - 1), h)),
    ]
    v_specs = [
        pl.BlockSpec((BLOCK, hd), lambda h, i: (jnp.maximum(i - 1, 0), hkv + h)),
        pl.BlockSpec((BLOCK, hd), lambda h, i: (i, hkv + h)),
        pl.BlockSpec((BLOCK, hd), lambda h, i: (jnp.minimum(i + 1, nb - 1), hkv + h)),
    ]
    return pl.pallas_call(
        body,
        grid=(hkv, nb),
        in_specs=[pl.BlockSpec((BLOCK, G * hd), lambda h, i: (i, h))] + kv_specs + v_specs + [
            pl.BlockSpec((L, hd), lambda h, i: (0, h)),
            pl.BlockSpec((L, hd), lambda h, i: (0, hkv + h)),
            pl.BlockSpec((G, 1, 1), lambda h, i: (h, 0, 0)),
        ],
        out_specs=[
            pl.BlockSpec((BLOCK, G * hd), lambda h, i: (i, h)),
            pl.BlockSpec((G, BLOCK, 1), lambda h, i: (h, i, 0)),
            pl.BlockSpec((G, 1, BLOCK), lambda h, i: (h, 0, i)),
        ],
        out_shape=[jax.ShapeDtypeStruct(qr.shape, BF16), jax.ShapeDtypeStruct((hkv * G, T, 1), F32),
                   jax.ShapeDtypeStruct((hkv * G, 1, T), F32)],
        compiler_params=_cp(("parallel", "parallel")),
        name=name,
    )(qr, kr, kr, kr, pkv, pkv, pkv, kcr, pkv_c, sink)


def attention_bwd_q(qr, kr, pkv, kcr, pkv_c, sink, do, o, lse, hkv, hd, name):
    T, L = qr.shape[0], kcr.shape[0]
    G = qr.shape[1] // (hkv * hd)
    nb = T // BLOCK
    scale = hd ** -0.5

    def body(q_ref, kp, kc, kn, vp, vc, vn, ck_ref, cv_ref, sink_ref, do_ref, o_ref, lse_ref,
             dq_ref, dck_ref, dcv_ref, dsink_ref, drr_ref):
        i = pl.program_id(1)

        @pl.when(i == 0)
        def _():
            dck_ref[...] = jnp.zeros_like(dck_ref)
            dcv_ref[...] = jnp.zeros_like(dcv_ref)
            dsink_ref[...] = jnp.zeros_like(dsink_ref)

        kwin = jnp.concatenate([kp[...], kc[...], kn[...]], axis=0)
        vwin = jnp.concatenate([vp[...], vc[...], vn[...]], axis=0).astype(BF16)
        ck, cv = ck_ref[...], cv_ref[...].astype(BF16)
        row = lax.broadcasted_iota(jnp.int32, (BLOCK, 3 * BLOCK), 0)
        col = lax.broadcasted_iota(jnp.int32, (BLOCK, 3 * BLOCK), 1)
        rel = col - BLOCK - row
        valid = (jnp.abs(rel) <= WINDOW) & ((col >= BLOCK) | (i > 0)) & ((col < 2 * BLOCK) | (i < nb - 1))
        R = range(G)
        qa, doa, oa = q_ref[...], do_ref[...], o_ref[...]
        qs = [qa[:, g * hd:(g + 1) * hd] for g in R]
        dos = [doa[:, g * hd:(g + 1) * hd] for g in R]
        lgs = [lse_ref[g] for g in R]
        sks = [sink_ref[g] for g in R]
        ss = [jnp.where(valid, _dot(qs[g], kwin, NT) * scale, NEG) for g in R]
        scs = [_dot(qs[g], ck, NT) * scale for g in R]
        dps = [_dot(dos[g], vwin, NT) for g in R]
        dpcs = [_dot(dos[g], cv, NT) for g in R]
        drs = [jnp.sum(dos[g].astype(F32) * oa[:, g * hd:(g + 1) * hd].astype(F32), axis=1, keepdims=True) for g in R]
        ps = [jnp.exp(ss[g] - lgs[g]) for g in R]
        pcs = [jnp.exp(scs[g] - lgs[g]) for g in R]
        dss = [(ps[g] * (dps[g] - drs[g]) * scale).astype(BF16) for g in R]
        dscs = [(pcs[g] * (dpcs[g] - drs[g]) * scale).astype(BF16) for g in R]
        dqs = [_dot(dss[g], kwin, NN) + _dot(dscs[g], ck, NN) for g in R]
        dcks = [_dot(dscs[g], qs[g], TN) for g in R]
        dcvs = [_dot(pcs[g].astype(BF16), dos[g], TN) for g in R]
        dq_ref[...] = jnp.concatenate(dqs, axis=1)
        dck_ref[...] += (dcks[0] + dcks[1]) + (dcks[2] + dcks[3]) if G == 4 else sum(dcks[1:], dcks[0])
        dcv_ref[...] += (dcvs[0] + dcvs[1]) + (dcvs[2] + dcvs[3]) if G == 4 else sum(dcvs[1:], dcvs[0])
        eye = (lax.broadcasted_iota(jnp.int32, (BLOCK, BLOCK), 0)
               == lax.broadcasted_iota(jnp.int32, (BLOCK, BLOCK), 1)).astype(F32)
        for g in R:
            dsink_ref[g] += -jnp.sum(jnp.exp(sks[g] - lgs[g]) * drs[g], axis=0, keepdims=True)
            drr_ref[g] = jnp.sum(drs[g] * eye, axis=0, keepdims=True)

    kv_specs = [
        pl.BlockSpec((BLOCK, hd), lambda h, i: (jnp.maximum(i - 1, 0), h)),
        pl.BlockSpec((BLOCK, hd), lambda h, i: (i, h)),
        pl.BlockSpec((BLOCK, hd), lambda h, i: (jnp.minimum(i + 1, nb - 1), h)),
    ]
    v_specs = [
        pl.BlockSpec((BLOCK, hd), lambda h, i: (jnp.maximum(i - 1, 0), hkv + h)),
        pl.BlockSpec((BLOCK, hd), lambda h, i: (i, hkv + h)),
        pl.BlockSpec((BLOCK, hd), lambda h, i: (jnp.minimum(i + 1, nb - 1), hkv + h)),
    ]
    qspec = pl.BlockSpec((BLOCK, G * hd), lambda h, i: (i, h))
    return pl.pallas_call(
        body,
        grid=(hkv, nb),
        in_specs=[qspec] + kv_specs + v_specs + [
            pl.BlockSpec((L, hd), lambda h, i: (0, h)),
            pl.BlockSpec((L, hd), lambda h, i: (0, hkv + h)),
            pl.BlockSpec((G, 1, 1), lambda h, i: (h, 0, 0)),
            qspec, qspec,
            pl.BlockSpec((G, BLOCK, 1), lambda h, i: (h, i, 0)),
        ],
        out_specs=[
            qspec,
            pl.BlockSpec((L, hd), lambda h, i: (0, h)),
            pl.BlockSpec((L, hd), lambda h, i: (0, h)),
            pl.BlockSpec((G, 1, 1), lambda h, i: (h, 0, 0)),
            pl.BlockSpec((G, 1, BLOCK), lambda h, i: (h, 0, i)),
        ],
        out_shape=[
            jax.ShapeDtypeStruct(qr.shape, F32),
            jax.ShapeDtypeStruct((L, hkv * hd), F32),
            jax.ShapeDtypeStruct((L, hkv * hd), F32),
            jax.ShapeDtypeStruct((hkv * G, 1, 1), F32),
            jax.ShapeDtypeStruct((hkv * G, 1, T), F32),
        ],
        compiler_params=_cp(("parallel", "arbitrary")),
        name=name,
    )(qr, kr, kr, kr, pkv, pkv, pkv, kcr, pkv_c, sink, do, o, lse)


def attention_bwd_kv(qr, kr, pkv, do, lse_row, dr_row, hkv, hd, name):
    T = qr.shape[0]
    G = qr.shape[1] // (hkv * hd)
    nb = T // BLOCK
    scale = hd ** -0.5

    def body(k_ref, v_ref, *refs):
        qs, dos, lses, drs = refs[0:3], refs[3:6], refs[6:9], refs[9:12]
        dk_ref, dv_ref = refs[12:]
        j = pl.program_id(1)
        k = k_ref[...]
        v = v_ref[...].astype(BF16)
        row = lax.broadcasted_iota(jnp.int32, (BLOCK, BLOCK), 0)
        col = lax.broadcasted_iota(jnp.int32, (BLOCK, BLOCK), 1)
        bias = []
        for d in range(3):
            iq = j + d - 1
            rel = row - col - (d - 1) * BLOCK
            valid = (jnp.abs(rel) <= WINDOW) & (iq >= 0) & (iq < nb)
            bias += [jnp.where(valid, 0.0, NEG)] * G
        bias = jnp.concatenate(bias, axis=1)

        def stack(refs):
            vals = [r[...] for r in refs]
            return jnp.concatenate([a[:, g * hd:(g + 1) * hd] for a in vals for g in range(G)], axis=0)

        q, dob = stack(qs), stack(dos)
        lrow = jnp.concatenate([r[g] for r in lses for g in range(G)], axis=1)
        drow = jnp.concatenate([r[g] for r in drs for g in range(G)], axis=1)
        st = _dot(k, q, NT) * scale + bias
        pt = jnp.exp(st - lrow)
        dpt = _dot(v, dob, NT)
        dst = (pt * (dpt - drow) * scale).astype(BF16)
        dk_ref[...] = _dot(dst, q, NN).astype(dk_ref.dtype)
        dv_ref[...] = _dot(pt.astype(BF16), dob, NN).astype(dv_ref.dtype)

    def q3(width_block):
        return [
            pl.BlockSpec(width_block, lambda h, j: (jnp.maximum(j - 1, 0), h)),
            pl.BlockSpec(width_block, lambda h, j: (j, h)),
            pl.BlockSpec(width_block, lambda h, j: (jnp.minimum(j + 1, nb - 1), h)),
        ]

    row3 = [
        pl.BlockSpec((G, 1, BLOCK), lambda h, j: (h, 0, jnp.maximum(j - 1, 0))),
        pl.BlockSpec((G, 1, BLOCK), lambda h, j: (h, 0, j)),
        pl.BlockSpec((G, 1, BLOCK), lambda h, j: (h, 0, jnp.minimum(j + 1, nb - 1))),
    ]
    qb = (BLOCK, G * hd)
    return pl.pallas_call(
        body,
        grid=(hkv, nb),
        in_specs=[pl.BlockSpec((BLOCK, hd), lambda h, j: (j, h)), pl.BlockSpec((BLOCK, hd), lambda h, j: (j, hkv + h))]
        + q3(qb) + q3(qb) + row3 + row3,
        out_specs=[pl.BlockSpec((BLOCK, hd), lambda h, j: (j, h))] * 2,
        out_shape=[jax.ShapeDtypeStruct((T, hkv * hd), BF16)] * 2,
        compiler_params=_cp(("parallel", "parallel")),
        name=name,
    )(kr, pkv, qr, qr, qr, do, do, do, lse_row, lse_row, lse_row, dr_row, dr_row, dr_row)


def gate_fwd(plr, wf, wb, bf, bb, name):
    n = wf.shape[1]

    def fn(lr, wf, wb, bf, bb):
        lrb = lr.astype(BF16)
        outs = []
        for w, b in ((wf, bf), (wb, bb)):
            z = _dot(lrb, w.astype(BF16), NN) + b
            outs.append((jnp.minimum(z, 0.0) - jnp.log(1.0 + jnp.exp(-jnp.abs(z)))) / GLA_GATE_NORM)
        return outs

    return rowwise(fn, [plr], [wf, wb, bf, bb], [(n, F32), (n, F32)], [], 256, name)


def gate_bwd(plr, dgf, dgb, wf, wb, bf, bb, name):
    n = wf.shape[1]

    def fn(lr, dgf, dgb, wf, wb, bf, bb):
        lrb = lr.astype(BF16)
        dlr = jnp.zeros(lr.shape, F32)
        res = []
        for w, b, dg in ((wf, bf, dgf), (wb, bb, dgb)):
            wb16 = w.astype(BF16)
            z = _dot(lrb, wb16, NN) + b
            dz = dg * _sig(-z) / GLA_GATE_NORM
            dzb = dz.astype(BF16)
            dlr = dlr + _dot(dzb, wb16, NT)
            res += [_dot(lrb, dzb, TN), jnp.sum(dz, axis=0, keepdims=True)]
        return [dlr] + res

    return rowwise(fn, [plr, dgf, dgb], [wf, wb, bf, bb], [(128, BF16)],
                   [(128, n), (1, n), (128, n), (1, n)], 256, name)


def _tri_dot(tri_b, x):
    x1 = x.astype(BF16)
    r1 = x - x1.astype(F32)
    x2 = r1.astype(BF16)
    x3 = (r1 - x2.astype(F32)).astype(BF16)
    return _dot(tri_b, x1, NN) + _dot(tri_b, x2, NN) + _dot(tri_b, x3, NN)


def gla_fwd(pqk, pv, gl, s0, heads, reverse, name, o_add=None):
    T = pqk.shape[0]
    dk = pqk.shape[1] // (2 * heads)
    dv = pv.shape[1] // heads
    C = GLA_CHUNK
    nc = T // C
    qscale = dk ** -0.5

    def body(*refs):
        if o_add is None:
            q_ref, k_ref, v_ref, g_ref, s0_ref, o_ref, st_ref, sf_ref, S = refs
            oa_ref = None
        else:
            q_ref, k_ref, v_ref, g_ref, s0_ref, oa_ref, o_ref, st_ref, sf_ref, S = refs
        n = pl.program_id(0)

        @pl.when(n == 0)
        def _():
            S[...] = s0_ref[...]

        r = lax.broadcasted_iota(jnp.int32, (C, C), 0)
        c = lax.broadcasted_iota(jnp.int32, (C, C), 1)
        tri = (r <= c) if reverse else (r >= c)
        trib = tri.astype(BF16)
        ga, qa, ka, va = g_ref[...], q_ref[...], k_ref[...], v_ref[...]
        sts = [S[h] for h in range(heads)]
        H = range(heads)
        gs = [ga[:, h * dk:(h + 1) * dk] for h in H]
        bs = [_tri_dot(trib, g) for g in gs]
        bls = [jnp.sum(g, axis=0, keepdims=True) for g in gs]
        mid = lax.broadcasted_iota(jnp.int32, (C, 1), 0) == C // 2
        bms = [jnp.sum(jnp.where(mid, b, 0.0), axis=0, keepdims=True) for b in bs]
        vs = [va[:, h * dv:(h + 1) * dv].astype(BF16) for h in H]
        qs = [qa[:, h * dk:(h + 1) * dk].astype(F32) * qscale for h in H]
        qes = [(qs[h] * jnp.exp(bs[h])).astype(BF16) for h in H]
        qms = [(qs[h] * jnp.exp(bs[h] - bms[h])).astype(BF16) for h in H]
        kms = [(ka[:, h * dk:(h + 1) * dk].astype(F32) * jnp.exp(bms[h] - bs[h])).astype(BF16) for h in H]
        kls = [(ka[:, h * dk:(h + 1) * dk].astype(F32) * jnp.exp(bls[h] - bs[h])).astype(BF16) for h in H]
        inter = [_dot(qes[h], sts[h].astype(BF16), NT) for h in H]
        upd = [_dot(vs[h], kls[h], TN) for h in H]
        As = [jnp.where(tri, _dot(qms[h], kms[h], NT), 0.0).astype(BF16) for h in H]
        outs = [inter[h] + _dot(As[h], vs[h], NN) for h in H]
        news = [sts[h] * jnp.exp(bls[h]) + upd[h] for h in H]
        o = jnp.concatenate(outs, axis=1)
        if oa_ref is not None:
            o = o + oa_ref[...]
        o_ref[...] = o
        for h in range(heads):
            st_ref[0, h] = sts[h]
            S[h] = news[h]

        @pl.when(n == nc - 1)
        def _():
            for h in range(heads):
                sf_ref[h] = news[h]

    def ci(n):
        return (nc - 1 - n) if reverse else n

    specs = [
        pl.BlockSpec((C, heads * dk), lambda n: (ci(n), 0)),
        pl.BlockSpec((C, heads * dk), lambda n: (ci(n), 1)),
        pl.BlockSpec((C, heads * dv), lambda n: (ci(n), 0)),
        pl.BlockSpec((C, heads * dk), lambda n: (ci(n), 0)),
        pl.BlockSpec((heads, dv, dk), lambda n: (0, 0, 0)),
    ]
    ins = [pqk, pqk, pv, gl, s0]
    if o_add is not None:
        specs.append(pl.BlockSpec((C, heads * dv), lambda n: (ci(n), 0)))
        ins.append(o_add)
    return pl.pallas_call(
        body,
        grid=(nc,),
        in_specs=specs,
        out_specs=[
            pl.BlockSpec((C, heads * dv), lambda n: (ci(n), 0)),
            pl.BlockSpec((1, heads, dv, dk), lambda n: (ci(n), 0, 0, 0)),
            pl.BlockSpec((heads, dv, dk), lambda n: (0, 0, 0)),
        ],
        out_shape=[
            jax.ShapeDtypeStruct((T, heads * dv), F32),
            jax.ShapeDtypeStruct((nc, heads, dv, dk), F32),
            jax.ShapeDtypeStruct((heads, dv, dk), F32),
        ],
        scratch_shapes=[pltpu.VMEM((heads, dv, dk), F32)],
        compiler_params=_cp(("arbitrary",)),
        name=name,
    )(*ins)


def gla_bwd(pqk, pv, gl, states, do, dsf, heads, reverse, name, acc=None):
    T = pqk.shape[0]
    dk = pqk.shape[1] // (2 * heads)
    dv = pv.shape[1] // heads
    C = GLA_CHUNK
    nc = T // C
    qscale = dk ** -0.5

    def body(*refs):
        if acc is None:
            q_ref, k_ref, v_ref, g_ref, st_ref, do_ref, dsf_ref, dq_ref, dk_ref, dv_ref, dg_ref, ds0_ref, dS = refs
            aq = ak = av = None
        else:
            (q_ref, k_ref, v_ref, g_ref, st_ref, do_ref, dsf_ref, aq, ak, av,
             dq_ref, dk_ref, dv_ref, dg_ref, ds0_ref, dS) = refs
        n = pl.program_id(0)

        @pl.when(n == 0)
        def _():
            dS[...] = dsf_ref[...]

        r = lax.broadcasted_iota(jnp.int32, (C, C), 0)
        c = lax.broadcasted_iota(jnp.int32, (C, C), 1)
        tri = (r <= c) if reverse else (r >= c)
        tri_t = (r >= c) if reverse else (r <= c)
        trib, tritb = tri.astype(BF16), tri_t.astype(BF16)
        ga, qa, ka, va, doa = g_ref[...], q_ref[...], k_ref[...], v_ref[...], do_ref[...]
        sts = [st_ref[0, h] for h in range(heads)]
        dsts = [dS[h] for h in range(heads)]
        H = range(heads)
        gs = [ga[:, h * dk:(h + 1) * dk] for h in H]
        bs = [_tri_dot(trib, g) for g in gs]
        bls = [jnp.sum(g, axis=0, keepdims=True) for g in gs]
        mid = lax.broadcasted_iota(jnp.int32, (C, 1), 0) == C // 2
        bms = [jnp.sum(jnp.where(mid, b, 0.0), axis=0, keepdims=True) for b in bs]
        ebs = [jnp.exp(b) for b in bs]
        embs = [jnp.exp(bs[h] - bms[h]) for h in H]
        enbs = [jnp.exp(bms[h] - bs[h]) for h in H]
        elbs = [jnp.exp(bls[h] - bs[h]) for h in H]
        ebls = [jnp.exp(bl) for bl in bls]
        vbs = [va[:, h * dv:(h + 1) * dv].astype(BF16) for h in H]
        dobs = [doa[:, h * dv:(h + 1) * dv].astype(BF16) for h in H]
        qs = [qa[:, h * dk:(h + 1) * dk].astype(F32) * qscale for h in H]
        qes = [qs[h] * ebs[h] for h in H]
        qms = [qs[h] * embs[h] for h in H]
        kms = [ka[:, h * dk:(h + 1) * dk].astype(F32) * enbs[h] for h in H]
        kls = [ka[:, h * dk:(h + 1) * dk].astype(F32) * elbs[h] for h in H]
        qebs = [a.astype(BF16) for a in qes]
        qmbs = [a.astype(BF16) for a in qms]
        kmbs = [a.astype(BF16) for a in kms]
        klbs = [a.astype(BF16) for a in kls]
        stbs = [a.astype(BF16) for a in sts]
        dstbs = [a.astype(BF16) for a in dsts]
        ps = [jnp.where(tri, _dot(qmbs[h], kmbs[h], NT), 0.0).astype(BF16) for h in H]
        dps = [jnp.where(tri, _dot(dobs[h], vbs[h], NT), 0.0).astype(BF16) for h in H]
        dqes = [_dot(dobs[h], stbs[h], NN) for h in H]
        dkls = [_dot(vbs[h], dstbs[h], NN) for h in H]
        dv1 = [_dot(klbs[h], dstbs[h], NT) for h in H]
        dsn1 = [_dot(dobs[h], qebs[h], TN) for h in H]
        dqms = [_dot(dps[h], kmbs[h], NN) for h in H]
        dkms = [_dot(dps[h], qmbs[h], TN) for h in H]
        dvs = [_dot(ps[h], dobs[h], TN) + dv1[h] for h in H]
        dbls = [ebls[h] * jnp.sum(dsts[h] * sts[h], axis=0, keepdims=True)
                + jnp.sum(dkls[h] * kls[h], axis=0, keepdims=True) for h in H]
        dsns = [dsn1[h] + dsts[h] * ebls[h] for h in H]
        dqs = [(dqes[h] * ebs[h] + dqms[h] * embs[h]) * qscale for h in H]
        dks = [dkms[h] * enbs[h] + dkls[h] * elbs[h] for h in H]
        dbs = [dqes[h] * qes[h] + dqms[h] * qms[h] - dkms[h] * kms[h] - dkls[h] * kls[h] for h in H]
        dgs = [_tri_dot(tritb, dbs[h]) + dbls[h] for h in H]
        dq, dkk, dvv = (jnp.concatenate(a, axis=1) for a in (dqs, dks, dvs))
        if aq is not None:
            dq = dq + aq[...].astype(F32)
            dkk = dkk + ak[...].astype(F32)
            dvv = dvv + av[...].astype(F32)
        dq_ref[...] = dq.astype(dq_ref.dtype)
        dk_ref[...] = dkk.astype(dk_ref.dtype)
        dv_ref[...] = dvv.astype(dv_ref.dtype)
        dg_ref[...] = jnp.concatenate(dgs, axis=1)
        for h in range(heads):
            dS[h] = dsns[h]

        @pl.when(n == nc - 1)
        def _():
            for h in range(heads):
                ds0_ref[h] = dsns[h]

    def ci(n):
        return n if reverse else (nc - 1 - n)

    kspec = pl.BlockSpec((C, heads * dk), lambda n: (ci(n), 0))
    vspec = pl.BlockSpec((C, heads * dv), lambda n: (ci(n), 0))
    sspec = pl.BlockSpec((heads, dv, dk), lambda n: (0, 0, 0))
    specs = [
        kspec,
        pl.BlockSpec((C, heads * dk), lambda n: (ci(n), 1)),
        vspec,
        kspec,
        pl.BlockSpec((1, heads, dv, dk), lambda n: (ci(n), 0, 0, 0)),
        vspec,
        sspec,
    ]
    ins = [pqk, pqk, pv, gl, states, do, dsf]
    odt = F32 if acc is None else BF16
    if acc is not None:
        specs += [kspec, kspec, vspec]
        ins += list(acc)
    return pl.pallas_call(
        body,
        grid=(nc,),
        in_specs=specs,
        out_specs=[kspec, kspec, vspec, kspec, sspec],
        out_shape=[
            jax.ShapeDtypeStruct((T, heads * dk), odt),
            jax.ShapeDtypeStruct((T, heads * dk), odt),
            jax.ShapeDtypeStruct((T, heads * dv), odt),
            jax.ShapeDtypeStruct((T, heads * dk), F32),
            jax.ShapeDtypeStruct((heads, dv, dk), F32),
        ],
        scratch_shapes=[pltpu.VMEM((heads, dv, dk), F32)],
        compiler_params=_cp(("arbitrary",)),
        name=name,
    )(*ins)


def gla_out_fwd(og, prb, gn, heads, name):
    dv = og.shape[1] // heads

    def fn(og, rb, gn):
        outs = []
        for h in range(heads):
            oh = og[:, h * dv:(h + 1) * dv]
            outs.append(oh * _rstd(oh) * gn)
        y = jnp.concatenate(outs, axis=1)
        return y * (rb * _sig(rb))

    return rowwise(fn, [og, prb], [gn], [(og.shape[1], BF16)], [], 256, name)[0]


def gla_out_bwd(og, prb, du, gn, heads, name):
    dv = og.shape[1] // heads

    def fn(og, rb, du, gn):
        sg = _sig(rb)
        silu = rb * sg
        dsilu = sg * (1.0 + rb * (1.0 - sg))
        dog, ys = [], []
        dgn = jnp.zeros((1, dv), F32)
        for h in range(heads):
            sl = slice(h * dv, (h + 1) * dv)
            oh = og[:, sl]
            r = _rstd(oh)
            n = oh * r
            ys.append(n * gn)
            dy = du[:, sl] * silu[:, sl]
            dgn = dgn + jnp.sum(dy * n, axis=0, keepdims=True)
            dn = dy * gn
            dog.append(r * (dn - n * jnp.mean(dn * n, axis=-1, keepdims=True)))
        y = jnp.concatenate(ys, axis=1)
        return jnp.concatenate(dog, axis=1), du * y * dsilu, dgn

    return rowwise(fn, [og, prb, du], [gn], [(og.shape[1], F32), (og.shape[1], BF16)], [(1, dv)], 128, name)


def conv_swiglu_fwd(ua_, ug_, cw, cb, name):
    T, F = ua_.shape
    tt = min(512, T)
    tc = _pick(F, (512, 256, 128))
    nt_, ncol = T // tt, F // tc
    H = CONV_HALO
    n = tt + 2 * H

    def body(ua, uap, uan, ug, ugp, ugn, wa, wg, ba, bg, f_ref):
        i = pl.program_id(1)
        keep_p = (i > 0).astype(F32)
        keep_n = (i < nt_ - 1).astype(F32)
        res = []
        for m, p, nx, w, b in ((ua, uap, uan, wa, ba), (ug, ugp, ugn, wg, bg)):
            x = jnp.concatenate([p[...] * keep_p, m[...], nx[...] * keep_n], axis=0)
            down, up = pltpu.roll(x, 1, 0)[H:H + tt], pltpu.roll(x, n - 1, 0)[H:H + tt]
            res.append(w[0] * down + w[1] * x[H:H + tt] + w[2] * up + b[...])
        a, g = res
        f_ref[...] = (a * _sig_tanh(a) * g).astype(f_ref.dtype)

    wspec = lambda off: pl.BlockSpec((3, 1, tc), lambda j, i: (0, 0, j + off))
    bspec = lambda off: pl.BlockSpec((1, tc), lambda j, i: (0, j + off))
    return pl.pallas_call(
        body,
        grid=(ncol, nt_),
        in_specs=conv_halo_specs(T, tt, tc, 0) + conv_halo_specs(T, tt, tc, 0)
        + [wspec(0), wspec(ncol), bspec(0), bspec(ncol)],
        out_specs=pl.BlockSpec((tt, tc), lambda j, i: (i, j)),
        out_shape=jax.ShapeDtypeStruct((T, F), BF16),
        compiler_params=_cp(("parallel", "parallel")),
        name=name,
    )(ua_, ua_, ua_, ug_, ug_, ug_, cw, cw, cb, cb)


CONV_HALO = 16


def conv_halo_specs(T, tt, tc, off):
    r = tt // CONV_HALO
    last = T // CONV_HALO - 1
    return [
        pl.BlockSpec((tt, tc), lambda j, i: (i, j + off)),
        pl.BlockSpec((CONV_HALO, tc), lambda j, i: (jnp.maximum(i * r - 1, 0), j + off)),
        pl.BlockSpec((CONV_HALO, tc), lambda j, i: (jnp.minimum((i + 1) * r, last), j + off)),
    ]


def conv_swiglu_bwd_fused(ua_, ug_, cw, cb, df, name):
    T, F = ua_.shape
    tt = min(512, T)
    tc = _pick(F, (512, 256, 128))
    nt_, ncol = T // tt, F // tc
    H = CONV_HALO
    n = tt + 2 * H

    def body(ua, uap, uan, ug, ugp, ugn, dm, dp_, dn, wa, wg, ba, bg, dua_ref, dug_ref, dwa, dwg, dba, dbg):
        i = pl.program_id(1)

        @pl.when(i == 0)
        def _():
            for r in (dwa, dwg, dba, dbg):
                r[...] = jnp.zeros_like(r)

        keep_p = (i > 0).astype(F32)
        keep_n = (i < nt_ - 1).astype(F32)

        def ext(m, p, nx):
            return jnp.concatenate([p[...].astype(F32) * keep_p, m[...].astype(F32), nx[...].astype(F32) * keep_n],
                                   axis=0)

        d = ext(dm, dp_, dn)
        conv, parts = [], []
        for m, p, nx, w, b in ((ua, uap, uan, wa, ba), (ug, ugp, ugn, wg, bg)):
            x = ext(m, p, nx)
            down, up = pltpu.roll(x, 1, 0), pltpu.roll(x, n - 1, 0)
            parts.append((down, x, up))
            conv.append(w[0] * down + w[1] * x + w[2] * up + b[...])
        a, g = conv
        sg = _sig_tanh(a)
        da = d * g * sg * (1.0 + a * (1.0 - sg))
        dg = d * a * sg
        for dd, w, (down, x, up), o_ref, dw, db in ((da, wa, parts[0], dua_ref, dwa, dba),
                                                    (dg, wg, parts[1], dug_ref, dwg, dbg)):
            du = w[0] * pltpu.roll(dd, n - 1, 0) + w[1] * dd + w[2] * pltpu.roll(dd, 1, 0)
            o_ref[...] = du[H:H + tt].astype(o_ref.dtype)
            ddm = dd[H:H + tt]
            dw[0] += jnp.sum(ddm * down[H:H + tt], axis=0, keepdims=True)
            dw[1] += jnp.sum(ddm * x[H:H + tt], axis=0, keepdims=True)
            dw[2] += jnp.sum(ddm * up[H:H + tt], axis=0, keepdims=True)
            db[...] += jnp.sum(ddm, axis=0, keepdims=True)

    wspec = lambda off: pl.BlockSpec((3, 1, tc), lambda j, i: (0, 0, j + off))
    bspec = lambda off: pl.BlockSpec((1, tc), lambda j, i: (0, j + off))
    tile = pl.BlockSpec((tt, tc), lambda j, i: (i, j))
    return pl.pallas_call(
        body,
        grid=(ncol, nt_),
        in_specs=conv_halo_specs(T, tt, tc, 0) + conv_halo_specs(T, tt, tc, 0) + conv_halo_specs(T, tt, tc, 0)
        + [wspec(0), wspec(ncol), bspec(0), bspec(ncol)],
        out_specs=[tile, tile, wspec(0), wspec(0), bspec(0), bspec(0)],
        out_shape=[
            jax.ShapeDtypeStruct((T, F), BF16), jax.ShapeDtypeStruct((T, F), BF16),
            jax.ShapeDtypeStruct((3, 1, F), F32), jax.ShapeDtypeStruct((3, 1, F), F32),
            jax.ShapeDtypeStruct((1, F), F32), jax.ShapeDtypeStruct((1, F), F32),
        ],
        compiler_params=_cp(("parallel", "arbitrary")),
        name=name,
    )(ua_, ua_, ua_, ug_, ug_, ug_, df, df, df, cw, cw, cb, cb)


def rope_tables(n, hd):
    rows = n // GRID_W
    row = jnp.repeat(jnp.arange(rows), GRID_W)
    col = jnp.tile(jnp.arange(GRID_W), rows)
    n_freq = hd // 4
    inv = ROPE_THETA ** (-jnp.arange(n_freq, dtype=F32) / n_freq)
    ang = jnp.concatenate([row[:, None] * inv, col[:, None] * inv], axis=-1)
    cos, sin = jnp.cos(ang), jnp.sin(ang)
    return jnp.concatenate([cos, cos], axis=-1), jnp.concatenate([-sin, sin], axis=-1)


def local_step(x, ctx, tgt, mod, modc, W, P):
    T, D = x.shape
    L = ctx.shape[0]
    hd, hq, hkv, gh = P["hd"], P["hq"], P["hkv"], P["gh"]
    sh1, sc1, g1, sh2, sc2, g2 = mod
    csh1, csc1 = modc
    kvw = hkv * hd
    gkw = W["gqk"].shape[1] // 2
    gdv = D // gh
    gdk = gkw // gh

    h = modulate_fwd(x, P["g_mix"], sh1, sc1, "mod1")
    hc = modulate_fwd(ctx, P["g_mix"], csh1, csc1, "mod1_ctx")
    pq = matmul(h, W["q"], "nn", F32, "proj_q")
    pkv = matmul(h, W["kv"], "nn", F32, "proj_kv")
    pgqk = matmul(h, W["gqk"], "nn", F32, "proj_gqk")
    pgv = matmul(h, W["gv"], "nn", F32, "proj_gv")
    prb = matmul(h, W["rb"], "nn", F32, "proj_rb")
    plr = matmul(h, W["lr"], "nn", F32, "proj_lr")
    pgab = matmul(h, W["gab"], "nn", F32, "proj_gab")
    pkv_c = matmul(hc, W["kv"], "nn", F32, "proj_kv_ctx")
    pgqk_c = matmul(hc, W["gqk"], "nn", F32, "proj_gqk_ctx")
    pgv_c = matmul(hc, W["gv"], "nn", F32, "proj_gv_ctx")
    plr_c = matmul(hc, W["lr"], "nn", F32, "proj_lr_ctx")

    cosf, sinf = rope_tables(T, hd)
    one_c, zero_c = jnp.ones((L, hd), F32), jnp.zeros((L, hd), F32)
    qr = norm_rope_fwd(pq, hq * hd, 0, P["q_norm"], cosf, sinf, hd, "qnorm")
    kr = norm_rope_fwd(pkv, kvw, 0, P["k_norm"], cosf, sinf, hd, "knorm")
    kcr = norm_rope_fwd(pkv_c, kvw, 0, P["k_norm"], one_c, zero_c, hd, "knorm_ctx")
    sink = P["attn_sink"].reshape(hq, 1, 1)
    o_attn, lse, lse_row = attention_fwd(qr, kr, pkv, kcr, pkv_c, sink, hkv, hd, "attn_fwd")

    gf, gb = gate_fwd(plr, W["gate_f"], W["gate_b"], P["b_gate_f"], P["b_gate_b"], "gates")
    gfc, gbc = gate_fwd(plr_c, W["gate_f"], W["gate_b"], P["b_gate_f"], P["b_gate_b"], "gates_ctx")
    zero_state = jnp.zeros((gh, gdv, gdk), F32)
    _, st_cf, s_cf = gla_fwd(pgqk_c, pgv_c, gfc, zero_state, gh, False, "gla_ctx_f")
    _, st_cb, s_cb = gla_fwd(pgqk_c, pgv_c, gbc, zero_state, gh, True, "gla_ctx_b")
    of, st_f, _ = gla_fwd(pgqk, pgv, gf, s_cf, gh, False, "gla_f")
    og, st_b, _ = gla_fwd(pgqk, pgv, gb, s_cb, gh, True, "gla_b", o_add=of)
    ug = gla_out_fwd(og, prb, P["gla_norm"], gh, "gla_out")

    ya = matmul(o_attn, W["attn_o"], "nn", F32, "attn_o")
    yg = matmul(ug, W["gla_o"], "nn", F32, "gla_o")

    def merge_fn(ya, yg, ga, gb_):
        return _sig(ga) * ya + _sig(gb_) * yg

    z = rowwise(merge_fn, [ya, yg, (pgab, D, 0), (pgab, D, 1)], [], [(D, BF16)], [], 256, "merge")[0]
    mo = matmul(z, W["out"], "nn", F32, "w_out")

    def res_fn(x, mo, g1, gffn, sh2, sc2):
        x1 = x + g1 * mo
        return x1, x1 * _rstd(x1) * gffn * (1.0 + sc2) + sh2

    x1, h2 = rowwise(res_fn, [x, mo], [g1, P["g_ffn"], sh2, sc2], [(D, F32), (D, BF16)], [], 256, "res_mod2")
    u_a = matmul(h2, W["up_a"], "nn", F32, "w_up_a")
    u_g = matmul(h2, W["up_g"], "nn", F32, "w_up_g")
    cw3 = W["conv_w"].reshape(3, 1, -1)
    f = conv_swiglu_fwd(u_a, u_g, cw3, P["conv_b"], "conv_swiglu")
    fo = matmul(f, W["down"], "nn", F32, "w_down")

    def final_fn(x1, fo, tgt, g2):
        e = x1 + g2 * fo - tgt
        dy = e * (1.0 / D)
        lsum = jnp.sum(jnp.sum(e * e, axis=1, keepdims=True), axis=0, keepdims=True)
        return dy, dy * g2, jnp.broadcast_to(lsum, (1, 128)), jnp.sum(dy * fo, axis=0, keepdims=True)

    dy, dfo, lsum, dg2 = rowwise(final_fn, [x1, fo, tgt], [g2], [(D, F32), (D, BF16)], [(1, 128), (1, D)], 256, "loss")
    df = matmul(dfo, W["down"], "nt", BF16, "d_f")
    dw_down = matmul(f, dfo, "tn", BF16, "dw_down")
    du_a, du_g, dcw_a, dcw_g, dcb_a, dcb_g = conv_swiglu_bwd_fused(u_a, u_g, cw3, P["conv_b"], df, "conv_swiglu_bwd")
    dh2 = matmul(du_a, W["up_a"], "nt", F32, "d_h2_a")
    dh2 = matmul(du_g, W["up_g"], "nt", F32, "d_h2_g", add=dh2)
    dw_up = [matmul(h2, du_a, "tn", BF16, "dw_up_a"), matmul(h2, du_g, "tn", BF16, "dw_up_g")]

    def mod2_bwd_fn(x1, dh, dy, mo, gffn, sc2, g1):
        r = _rstd(x1)
        n = x1 * r
        dyy = dh * (1.0 + sc2)
        dn = dyy * gffn
        dx1 = dy + r * (dn - n * jnp.mean(dn * n, axis=-1, keepdims=True))
        s0 = lambda a: jnp.sum(a, axis=0, keepdims=True)
        return dx1, dx1 * g1, s0(dyy * n), s0(dh), s0(dh * n * gffn), s0(dx1 * mo)

    dx1, dmo, dg_ffn, dsh2, dsc2, dg1 = rowwise(
        mod2_bwd_fn, [x1, dh2, dy, mo], [P["g_ffn"], sc2, g1], [(D, F32), (D, BF16)], [(1, D)] * 4, 128, "mod2_bwd")
    dz = matmul(dmo, W["out"], "nt", F32, "d_z")
    dw_out = matmul(z, dmo, "tn", BF16, "dw_out")

    def merge_bwd_fn(dz, ya, yg, ga, gb_):
        sa, sb = _sig(ga), _sig(gb_)
        return dz * sa, dz * sb, jnp.concatenate([dz * ya * sa * (1.0 - sa), dz * yg * sb * (1.0 - sb)], axis=1)

    dya, dyg, dpgab = rowwise(merge_bwd_fn, [dz, ya, yg, (pgab, D, 0), (pgab, D, 1)], [],
                              [(D, BF16), (D, BF16), (2 * D, BF16)], [], 128, "merge_bwd")
    do_attn = matmul(dya, W["attn_o"], "nt", BF16, "d_oattn")
    dw_attn_o = matmul(o_attn, dya, "tn", BF16, "dw_attn_o")
    dug = matmul(dyg, W["gla_o"], "nt", F32, "d_ug")
    dw_gla_o = matmul(ug, dyg, "tn", BF16, "dw_gla_o")
    dog, dprb, dgn = gla_out_bwd(og, prb, dug, P["gla_norm"], gh, "gla_out_bwd")

    dq1, dk1, dv1, dgf, ds_cf = gla_bwd(pgqk, pgv, gf, st_f, dog, zero_state, gh, False, "gla_f_bwd")
    dgq, dgk, dpgv, dgb, ds_cb = gla_bwd(pgqk, pgv, gb, st_b, dog, zero_state, gh, True, "gla_b_bwd",
                                          acc=(dq1, dk1, dv1))
    dpgqk = jnp.concatenate([dgq, dgk], axis=1)
    zero_do = jnp.zeros((L, gh * gdv), F32)
    cq1, ck1, cv1, dgfc, _ = gla_bwd(pgqk_c, pgv_c, gfc, st_cf, zero_do, ds_cf, gh, False, "gla_ctx_f_bwd")
    cq, ck, dpgv_c, dgbc, _ = gla_bwd(pgqk_c, pgv_c, gbc, st_cb, zero_do, ds_cb, gh, True, "gla_ctx_b_bwd",
                                      acc=(cq1, ck1, cv1))
    dpgqk_c = jnp.concatenate([cq, ck], axis=1)
    dplr, dwgf, dbgf, dwgb, dbgb = gate_bwd(plr, dgf, dgb, W["gate_f"], W["gate_b"], P["b_gate_f"], P["b_gate_b"], "gates_bwd")
    dplr_c, dwgf_c, dbgf_c, dwgb_c, dbgb_c = gate_bwd(plr_c, dgfc, dgbc, W["gate_f"], W["gate_b"], P["b_gate_f"],
                                                      P["b_gate_b"], "gates_ctx_bwd")

    dqr, dkc_r, dvc, dsink, dr_row = attention_bwd_q(qr, kr, pkv, kcr, pkv_c, sink, do_attn, o_attn, lse, hkv, hd,
                                                     "attn_bwd_q")
    dkr, dv = attention_bwd_kv(qr, kr, pkv, do_attn, lse_row, dr_row, hkv, hd, "attn_bwd_kv")
    dpq, dqn = norm_rope_bwd(pq, hq * hd, 0, dqr, P["q_norm"], cosf, sinf, hd, "qnorm_bwd")
    dpk, dkn = norm_rope_bwd(pkv, kvw, 0, dkr, P["k_norm"], cosf, sinf, hd, "knorm_bwd")
    dpk_c, dkn_c = norm_rope_bwd(pkv_c, kvw, 0, dkc_r, P["k_norm"], one_c, zero_c, hd, "knorm_ctx_bwd")
    dpkv = jnp.concatenate([dpk, dv], axis=1)
    dpkv_c = jnp.concatenate([dpk_c, dvc.astype(BF16)], axis=1)

    dw_q = matmul(h, dpq, "tn", BF16, "dw_q")
    dw_kv = matmul(h, dpkv, "tn", BF16, "dw_kv", add=matmul(hc, dpkv_c, "tn", F32, "dw_kv_ctx"))
    dw_gqk = matmul(h, dpgqk, "tn", BF16, "dw_gqk", add=matmul(hc, dpgqk_c, "tn", F32, "dw_gqk_ctx"))
    dw_gv = matmul(h, dpgv, "tn", BF16, "dw_gv", add=matmul(hc, dpgv_c, "tn", F32, "dw_gv_ctx"))
    dw_rb = matmul(h, dprb, "tn", BF16, "dw_rb")
    dw_lr = matmul(h, dplr, "tn", BF16, "dw_lr", add=matmul(hc, dplr_c, "tn", F32, "dw_lr_ctx"))
    dw_gab = matmul(h, dpgab, "tn", BF16, "dw_gab")
    lrw = P["lowrank"]
    dw_in = [dw_q, dw_kv, dw_gqk, dw_gv, dw_rb, dw_lr[:, :2 * lrw], dw_gab]

    dh = matmul(dpq, W["q"], "nt", F32, "dh_q")
    dh = matmul(dpkv, W["kv"], "nt", F32, "dh_kv", add=dh)
    dh = matmul(dpgqk, W["gqk"], "nt", F32, "dh_gqk", add=dh)
    dh = matmul(dpgv, W["gv"], "nt", F32, "dh_gv", add=dh)
    dh = matmul(dprb, W["rb"], "nt", F32, "dh_rb", add=dh)
    dh = matmul(dplr, W["lr"], "nt", F32, "dh_lr", add=dh)
    dh = matmul(dpgab, W["gab"], "nt", F32, "dh_gab", add=dh)
    dhc = matmul(dpkv_c, W["kv"], "nt", F32, "dhc_kv")
    dhc = matmul(dpgqk_c, W["gqk"], "nt", F32, "dhc_gqk", add=dhc)
    dhc = matmul(dpgv_c, W["gv"], "nt", F32, "dhc_gv", add=dhc)
    dhc = matmul(dplr_c, W["lr"], "nt", F32, "dhc_lr", add=dhc)

    def mod1_bwd_fn(x, dh, dres, g, sc):
        r = _rstd(x)
        n = x * r
        dyy = dh * (1.0 + sc)
        dn = dyy * g
        dx = dres + r * (dn - n * jnp.mean(dn * n, axis=-1, keepdims=True))
        s0 = lambda a: jnp.sum(a, axis=0, keepdims=True)
        return dx, s0(dyy * n), s0(dh), s0(dh * n * g)

    grad_x, dgmix, dsh1, dsc1 = rowwise(mod1_bwd_fn, [x, dh, dx1], [P["g_mix"], sc1], [(D, F32)], [(1, D)] * 3,
                                        128, "mod1_bwd")
    _, dgmix_c, dcsh1, dcsc1 = rowwise(mod1_bwd_fn, [ctx, dhc, jnp.zeros_like(ctx)], [P["g_mix"], csc1], [(D, F32)],
                                       [(1, D)] * 3, 128, "mod1_ctx_bwd")

    zD = jnp.zeros((1, D), F32)
    grads = dict(
        w_in=dw_in, w_attn_o=dw_attn_o, w_gla_o=dw_gla_o, w_out=dw_out, w_up=dw_up, w_down=dw_down,
        dmod_x=jnp.concatenate([dsh1, dsc1, dg1, dsh2, dsc2, dg2], axis=1),
        dmod_c=jnp.concatenate([dcsh1, dcsc1, zD, zD, zD, zD], axis=1),
        g_mix=dgmix + dgmix_c, q_norm=dqn, k_norm=dkn + dkn_c, attn_sink=dsink.reshape(1, hq),
        w_gate_f=(dwgf + dwgf_c)[:lrw], b_gate_f=dbgf + dbgf_c,
        w_gate_b=(dwgb + dwgb_c)[lrw:2 * lrw], b_gate_b=dbgb + dbgb_c,
        gla_norm=dgn, g_ffn=dg_ffn,
        conv_w=jnp.concatenate([dcw_a, dcw_g], axis=2).reshape(3, -1),
        conv_b=jnp.concatenate([dcb_a, dcb_g], axis=1),
    )
    return lsum[0, 0], grad_x, grads


SMALL_REPL = ("c_ctx", "b_mod", "g_mix", "q_norm", "k_norm", "attn_sink", "b_gate_f", "b_gate_b", "gla_norm", "g_ffn",
              "conv_b")
SMALL_SHARD = ("w_gate_f", "w_gate_b", "conv_w")
ORDER = ("c_ctx", "w_mod", "b_mod", "g_mix", "w_in", "q_norm", "k_norm", "attn_sink", "w_gate_f", "b_gate_f",
         "w_gate_b", "b_gate_b", "gla_norm", "w_attn_o", "w_gla_o", "w_out", "g_ffn", "w_up", "conv_w", "conv_b",
         "w_down")


def kernel(x, c, ctx, c_ctx, w_mod, b_mod, g_mix, w_in, q_norm, k_norm, attn_sink, w_gate_f, b_gate_f, w_gate_b, b_gate_b, gla_norm, w_attn_o, w_gla_o, w_out, g_ffn, w_up, conv_w, conv_b, w_down, loss_target, m_c_ctx, m_w_mod, m_b_mod, m_g_mix, m_w_in, m_q_norm, m_k_norm, m_attn_sink, m_w_gate_f, m_b_gate_f, m_w_gate_b, m_b_gate_b, m_gla_norm, m_w_attn_o, m_w_gla_o, m_w_out, m_g_ffn, m_w_up, m_conv_w, m_conv_b, m_w_down, v_c_ctx, v_w_mod, v_b_mod, v_g_mix, v_w_in, v_q_norm, v_k_norm, v_attn_sink, v_w_gate_f, v_b_gate_f, v_w_gate_b, v_b_gate_b, v_gla_norm, v_w_attn_o, v_w_gla_o, v_w_out, v_g_ffn, v_w_up, v_conv_w, v_conv_b, v_w_down):
    loc = dict(locals())
    Wt = {n: loc[n] for n in ORDER}
    Mt = {n: loc["m_" + n] for n in ORDER}
    Vt = {n: loc["v_" + n] for n in ORDER}
    me = 4 * lax.axis_index("x") + 2 * lax.axis_index("y") + lax.axis_index("c")

    D = x.shape[-1]
    hd = q_norm.shape[-1]
    hq = attn_sink.shape[-1]
    gdv = gla_norm.shape[-1]
    gh = D // gdv
    gdk = D // 2 // gh
    lrw = w_gate_f.shape[1]
    in_w = NDEV * w_in.shape[-1]
    kvw = (in_w - hq * hd - 2 * gh * gdk - 2 * gh * gdv - 2 * lrw - 2 * D) // 2
    hkv = kvw // hd
    gcols = w_gate_f.shape[-1]
    mcols = w_mod.shape[-1]

    x2, ctx2, tgt2 = x[0], ctx[0], loss_target[0]

    c_all = exchange([jnp.pad(c, ((0, 7), (0, 0)))], True, "gather_c")[0][:, 0, :]
    c9 = jnp.concatenate([c_all, c_ctx[None, :], jnp.zeros((7, D), F32)], axis=0)
    s9 = rowwise(lambda a: a * _sig(a), [c9], [], [(D, F32)], [], 16, "silu_c")[0]
    bias = jnp.broadcast_to(lax.dynamic_slice_in_dim(b_mod, me * mcols, mcols, axis=1), (16, mcols))
    mod_cols = matmul(s9, w_mod[0], "nn", F32, "mod_cols", add=bias)
    mod_all = exchange([mod_cols], True, "gather_mod")[0]
    mod_all = jnp.transpose(mod_all, (1, 0, 2)).reshape(16, NDEV * mcols)
    mod_me = lax.dynamic_slice_in_dim(mod_all, me, 1, axis=0)
    mod = [mod_me[:, i * D:(i + 1) * D] for i in range(6)]
    modc = [mod_all[8:9, i * D:(i + 1) * D] for i in range(2)]

    o3 = jnp.stack([w_attn_o[0], w_gla_o[0], w_out[0]]).astype(BF16)
    small_w = pack([w_gate_f[0], w_gate_b[0], conv_w[0]])
    g_in, g_o3, g_up, g_down, g_small = gather_two_level(
        [w_in[0].astype(BF16), o3, w_up[0].astype(BF16), w_down[0].astype(BF16), small_w], "gather_w")
    seg = segments_from_blocks(g_in, [hq * hd, 2 * kvw, 2 * gh * gdk, gh * gdv, gh * gdv, 2 * lrw, 2 * D])
    small_parts = [unpack(g_small[j], [w_gate_f[0].shape, w_gate_b[0].shape, conv_w[0].shape]) for j in range(NDEV)]
    wgf = jnp.concatenate([p[0] for p in small_parts], axis=1)
    wgb = jnp.concatenate([p[1] for p in small_parts], axis=1)
    cw_full = jnp.concatenate([p[2] for p in small_parts], axis=1)
    o3f = [g_o3[:, i].reshape(-1, D) for i in range(3)]
    W = dict(
        q=seg[0], kv=seg[1], gqk=seg[2], gv=seg[3], rb=seg[4],
        lr=jnp.pad(seg[5], ((0, 0), (0, 128 - 2 * lrw))), gab=seg[6],
        gate_f=jnp.pad(wgf, ((0, 128 - lrw), (0, 0))),
        gate_b=jnp.pad(wgb, ((lrw, 128 - 2 * lrw), (0, 0))),
        attn_o=o3f[0], gla_o=o3f[1], out=o3f[2],
        up_a=jnp.concatenate([g_up[j] for j in range(NDEV // 2)], axis=1),
        up_g=jnp.concatenate([g_up[j] for j in range(NDEV // 2, NDEV)], axis=1),
        down=g_down.reshape(-1, D),
        conv_w=cw_full,
    )
    P = dict(hd=hd, hq=hq, hkv=hkv, gh=gh, lowrank=lrw, g_mix=g_mix, q_norm=q_norm, k_norm=k_norm, attn_sink=attn_sink,
             b_gate_f=b_gate_f, b_gate_b=b_gate_b, gla_norm=gla_norm, g_ffn=g_ffn, conv_b=conv_b)

    lsum, grad_x, G = local_step(x2, ctx2, tgt2, mod, modc, W, P)
    loss = lax.psum(0.5 * lsum / D, ("x", "y", "c"))

    dm = exchange([jnp.concatenate([G["dmod_x"], G["dmod_c"], jnp.zeros((6, 6 * D), F32)], axis=0)], True,
                  "gather_dmod")[0]
    dmc = reduce_parts(dm[:, 1:2, :].reshape(NDEV, 6 * D // 128, 128), "sum_dmod_ctx").reshape(1, 6 * D)
    dM = jnp.concatenate([dm[:, 0, :], dmc, jnp.zeros((7, 6 * D), F32)], axis=0)
    dM_cols = lax.dynamic_slice_in_dim(dM, me * mcols, mcols, axis=1)
    g_w_mod = matmul(s9, dM_cols, "tn", F32, "dw_mod")
    g_b_mod = reduce_parts(dM.reshape(16, 6 * D // 128, 128), "sum_db_mod").reshape(1, 6 * D)
    dsc = matmul(dM_cols[8:16], w_mod[0], "nt", F32, "d_silu_ctx")
    cc = jnp.broadcast_to(c_ctx[None, :], (8, D))

    def dsilu_fn(d, a):
        sg = _sig(a)
        return d * sg * (1.0 + a * (1.0 - sg))

    g_cctx_part = rowwise(dsilu_fn, [dsc, cc], [], [(D, F32)], [], 8, "d_c_ctx")[0][0:1]

    small_names = ("c_ctx", "g_mix", "q_norm", "k_norm", "attn_sink", "b_gate_f", "b_gate_b", "gla_norm", "g_ffn",
                   "conv_b", "w_gate_f", "w_gate_b", "conv_w")
    G["c_ctx"] = g_cctx_part
    sm_shapes = [G[n].shape for n in small_names]
    sm_all = exchange([pack([G[n] for n in small_names])], True, "gather_small_grads")[0]
    sm_tot = unpack(reduce_parts(sm_all, "sum_small_grads"), sm_shapes)
    gs = dict(zip(small_names, sm_tot))
    gs["b_mod"] = g_b_mod
    gs["w_gate_f"] = lax.dynamic_slice_in_dim(gs["w_gate_f"], me * gcols, gcols, axis=1)
    gs["w_gate_b"] = lax.dynamic_slice_in_dim(gs["w_gate_b"], me * gcols, gcols, axis=1)
    ccols = conv_w.shape[-1]
    gs["conv_w"] = lax.dynamic_slice_in_dim(gs["conv_w"], me * ccols, ccols, axis=1)

    orows = w_attn_o.shape[1]
    s_in = blocks_from_segments(G["w_in"], w_in.shape[-1])
    s_o3 = jnp.concatenate([rows_to_blocks(G["w_attn_o"]), rows_to_blocks(G["w_gla_o"]), rows_to_blocks(G["w_out"])],
                           axis=1)
    s_up = blocks_from_segments(G["w_up"], w_up.shape[-1])
    s_down = rows_to_blocks(G["w_down"])
    r_in, r_o3, r_up, r_down = scatter_reduce([s_in, s_o3, s_up, s_down], "scatter_grads")

    out = {}
    out["w_in"] = adam_reduce(r_in, w_in[0], m_w_in[0], v_w_in[0], "adam_w_in")
    o3w = jnp.concatenate([w_attn_o[0], w_gla_o[0], w_out[0]], axis=0)
    o3m = jnp.concatenate([m_w_attn_o[0], m_w_gla_o[0], m_w_out[0]], axis=0)
    o3v = jnp.concatenate([v_w_attn_o[0], v_w_gla_o[0], v_w_out[0]], axis=0)
    ro3 = adam_reduce(r_o3, o3w, o3m, o3v, "adam_o3")
    for i, n in enumerate(("w_attn_o", "w_gla_o", "w_out")):
        out[n] = [a[i * orows:(i + 1) * orows] for a in ro3]
    out["w_up"] = adam_reduce(r_up, w_up[0], m_w_up[0], v_w_up[0], "adam_w_up")
    out["w_down"] = adam_reduce(r_down, w_down[0], m_w_down[0], v_w_down[0], "adam_w_down")
    out["w_mod"] = adam_reduce(g_w_mod[None], w_mod[0], m_w_mod[0], v_w_mod[0], "adam_w_mod")
    sm_names = SMALL_REPL + SMALL_SHARD
    shapes = [Wt[n].shape for n in sm_names]
    rs = adam_reduce(pack([gs[n] for n in sm_names])[None], pack([Wt[n] for n in sm_names]),
                     pack([Mt[n] for n in sm_names]), pack([Vt[n] for n in sm_names]), "adam_small")
    rs = [unpack(a, shapes) for a in rs]
    for i, n in enumerate(sm_names):
        out[n] = [a[i] for a in rs]

    res = [loss, grad_x[None]]
    for k in range(4):
        for n in ORDER:
            res.append(out[n][k].reshape(Wt[n].shape))
    return tuple(res)
```

```python
import jax
import jax.numpy as jnp
import numpy as np
from jax import lax
from jax.experimental import pallas as pl
from jax.experimental.pallas import tpu as pltpu

F32 = jnp.float32
BF16 = jnp.bfloat16

NDEV = 8
NCHIP = 4
EPS = 1e-6
WINDOW = 128
BLOCK = 128
GRID_W = 64
ROPE_THETA = 10000.0
GLA_CHUNK = 128
GLA_GATE_NORM = 16.0
ADAM_LR = 0.001
ADAM_B1 = 0.9
ADAM_B2 = 0.999
ADAM_EPS = 1e-08
ADAM_WD = 0.01
ADAM_STEP = 10
V7X_VMEM_LIMIT = 56 * 1024 * 1024
MATMUL_VMEM_BUDGET = 40 * 1024 * 1024
MATMUL_TILES = (1024, 1408, 512, 256, 128)
NEG = -1e30

NN = ((1,), (0,))
NT = ((1,), (1,))
TN = ((0,), (0,))


def _dot(a, b, dims):
    return lax.dot_general(a, b, (dims, ((), ())), preferred_element_type=F32)


def _cp(sem):
    return pltpu.CompilerParams(dimension_semantics=sem, vmem_limit_bytes=V7X_VMEM_LIMIT)


def _pick(n, cands):
    for c in cands:
        if n % c == 0:
            return c
    return n


def _sig(x):
    return 1.0 / (1.0 + jnp.exp(-x))


def _sig_tanh(x):
    return 0.5 * jnp.tanh(0.5 * x) + 0.5


def _rstd(x):
    return lax.rsqrt(jnp.mean(x * x, axis=-1, keepdims=True) + EPS)


_ANY = pl.BlockSpec(memory_space=pl.ANY)


def _place():
    return lax.axis_index("x"), lax.axis_index("y"), lax.axis_index("c")


def exchange(srcs, bcast, name, group="all"):
    n = len(srcs)
    ndev = NDEV if group == "all" else NCHIP
    ks = tuple(range(1, NDEV)) if group == "all" else (2, 4, 6)
    out_shape = [jax.ShapeDtypeStruct((ndev,) + (s.shape if bcast else s.shape[1:]), s.dtype) for s in srcs]

    def body(*refs):
        src, dst = refs[:n], refs[n:2 * n]
        send_sems, recv_sems, loc_sems = refs[2 * n:]
        x, y, c = _place()

        def idx(px, py, pc):
            return 4 * px + 2 * py + pc if group == "all" else 2 * px + py

        me = idx(x, y, c)
        copies = []
        for a in range(n):
            cp = pltpu.make_async_copy(src[a] if bcast else src[a].at[me], dst[a].at[me], loc_sems.at[a])
            cp.start()
            copies.append(cp)
        for s, k in enumerate(ks):
            px, py, pc = x ^ ((k >> 2) & 1), y ^ ((k >> 1) & 1), c ^ (k & 1)
            for a in range(n):
                cp = pltpu.make_async_remote_copy(
                    src_ref=src[a] if bcast else src[a].at[idx(px, py, pc)],
                    dst_ref=dst[a].at[me],
                    send_sem=send_sems.at[a, s],
                    recv_sem=recv_sems.at[a, s],
                    device_id=(px, py, pc),
                    device_id_type=pl.DeviceIdType.MESH,
                )
                cp.start()
                copies.append(cp)
        for cp in copies:
            cp.wait()

    return pl.pallas_call(
        body,
        out_shape=out_shape,
        in_specs=[_ANY] * n,
        out_specs=[_ANY] * n,
        scratch_shapes=[
            pltpu.SemaphoreType.DMA((n, len(ks))),
            pltpu.SemaphoreType.DMA((n, len(ks))),
            pltpu.SemaphoreType.DMA((n,)),
        ],
        name=name,
    )(*srcs)


def gather_two_level(srcs, name):
    n = len(srcs)
    out_shape = [jax.ShapeDtypeStruct((NDEV,) + s.shape, s.dtype) for s in srcs]

    def body(*refs):
        src, dst = refs[:n], refs[n:2 * n]
        send_sems, recv_sems, loc_sems = refs[2 * n:]
        x, y, c = _place()
        me = 4 * x + 2 * y + c
        sib = (x, y, 1 - c)
        first = (x ^ (1 - c), y ^ c)
        second = (x ^ c, y ^ (1 - c))
        diag = (x ^ 1, y ^ 1)

        def row(chip, core):
            return 4 * chip[0] + 2 * chip[1] + core

        def copy(a, s, block, to, from_src=False):
            return pltpu.make_async_remote_copy(
                src_ref=src[a] if from_src else dst[a].at[block], dst_ref=dst[a].at[block],
                send_sem=send_sems.at[a, s], recv_sem=recv_sems.at[a, s],
                device_id=to, device_id_type=pl.DeviceIdType.MESH)

        local = [pltpu.make_async_copy(src[a], dst[a].at[me], loc_sems.at[a]) for a in range(n)]
        sent = [copy(a, 0, me, sib, True) for a in range(n)]
        sent += [copy(a, 1, me, (*first, c), True) for a in range(n)]
        sent += [copy(a, 2, me, (*second, c), True) for a in range(n)]
        for cp in local + sent:
            cp.start()
        for a in range(n):
            copy(a, 1, row(first, c), (*first, c)).wait_recv()
            for cp in (copy(a, 3, row(first, c), (*second, c)), copy(a, 5, row(first, c), sib)):
                cp.start()
                sent.append(cp)
        for a in range(n):
            copy(a, 2, row(second, c), (*second, c)).wait_recv()
            cp = copy(a, 4, row(second, c), sib)
            cp.start()
            sent.append(cp)
        for a in range(n):
            copy(a, 3, row(diag, c), (*second, c)).wait_recv()
            cp = copy(a, 6, row(diag, c), sib)
            cp.start()
            sent.append(cp)
        for a in range(n):
            copy(a, 0, row((x, y), 1 - c), sib).wait_recv()
            copy(a, 4, row(first, 1 - c), sib).wait_recv()
            copy(a, 5, row(second, 1 - c), sib).wait_recv()
            copy(a, 6, row(diag, 1 - c), sib).wait_recv()
        for cp in local:
            cp.wait()
        for cp in sent:
            cp.wait_send()

    return pl.pallas_call(
        body,
        out_shape=out_shape,
        in_specs=[_ANY] * n,
        out_specs=[_ANY] * n,
        scratch_shapes=[
            pltpu.SemaphoreType.DMA((n, NDEV - 1)),
            pltpu.SemaphoreType.DMA((n, NDEV - 1)),
            pltpu.SemaphoreType.DMA((n,)),
        ],
        name=name,
    )(*srcs)


def _chip_across(core, da, db):
    x, y, _ = _place()
    return x ^ (da * (1 - core) + db * core), y ^ (db * (1 - core) + da * core)


def pair_swap(srcs, name, axis="c"):
    n = len(srcs)

    def body(*refs):
        src, dst = refs[:n], refs[n:2 * n]
        send_sems, recv_sems = refs[2 * n:]
        x, y, c = _place()
        partner = {"c": (x, y, 1 - c), "first": (*_chip_across(c, 1, 0), c), "second": (*_chip_across(c, 0, 1), c)}[axis]
        copies = []
        for a in range(n):
            cp = pltpu.make_async_remote_copy(
                src_ref=src[a], dst_ref=dst[a], send_sem=send_sems.at[a], recv_sem=recv_sems.at[a],
                device_id=partner, device_id_type=pl.DeviceIdType.MESH)
            cp.start()
            copies.append(cp)
        for cp in copies:
            cp.wait()

    return pl.pallas_call(
        body,
        out_shape=[jax.ShapeDtypeStruct(s.shape, s.dtype) for s in srcs],
        in_specs=[_ANY] * n,
        out_specs=[_ANY] * n,
        scratch_shapes=[pltpu.SemaphoreType.DMA((n,)), pltpu.SemaphoreType.DMA((n,))],
        name=name,
    )(*srcs)


_OFFSETS = ((0, 0), (0, 1), (1, 0), (1, 1))


def sibling_swap_blocks(blocks, name):
    n = len(blocks)

    def body(*refs):
        src, dst = refs[:n], refs[n:2 * n]
        send_sems, recv_sems = refs[2 * n:]
        x, y, c = _place()
        copies = []
        for a in range(n):
            for j, (da, db) in enumerate(_OFFSETS):
                px, py = _chip_across(1 - c, da, db)
                cp = pltpu.make_async_remote_copy(
                    src_ref=src[a].at[4 * px + 2 * py + (1 - c)], dst_ref=dst[a].at[j],
                    send_sem=send_sems.at[a, j], recv_sem=recv_sems.at[a, j],
                    device_id=(x, y, 1 - c), device_id_type=pl.DeviceIdType.MESH)
                cp.start()
                copies.append(cp)
        for cp in copies:
            cp.wait()

    return pl.pallas_call(
        body,
        out_shape=[jax.ShapeDtypeStruct((4,) + b.shape[1:], b.dtype) for b in blocks],
        in_specs=[_ANY] * n,
        out_specs=[_ANY] * n,
        scratch_shapes=[pltpu.SemaphoreType.DMA((n, 4)), pltpu.SemaphoreType.DMA((n, 4))],
        name=name,
    )(*blocks)


def add_own_blocks(blocks, got, name):
    _, R, C = blocks.shape
    tile = _pick(R, (256, 128, 64))

    def body(*refs):
        for j in range(4):
            refs[8 + j][...] = (refs[j][...].astype(F32) + refs[4 + j][...].astype(F32)).astype(refs[8 + j].dtype)

    def own(da, db):
        def index(i):
            c = lax.axis_index("c")
            px, py = _chip_across(c, da, db)
            return 4 * px + 2 * py + c, i, 0
        return pl.BlockSpec((None, tile, C), index)

    return pl.pallas_call(
        body,
        grid=(R // tile,),
        in_specs=[own(da, db) for da, db in _OFFSETS]
        + [pl.BlockSpec((None, tile, C), lambda i, j=j: (j, i, 0)) for j in range(4)],
        out_specs=[pl.BlockSpec((tile, C), lambda i: (i, 0))] * 4,
        out_shape=[jax.ShapeDtypeStruct((R, C), blocks.dtype)] * 4,
        compiler_params=_cp(("parallel",)),
        name=name,
    )(blocks, blocks, blocks, blocks, got, got, got, got)


def scatter_reduce(blocks, name):
    def add(n_out, ins, label):
        fn = lambda *a: [a[i].astype(F32) + a[n_out + i].astype(F32) for i in range(n_out)]
        rows, cols = ins[0].shape
        return rowwise(fn, ins, [], [(cols, ins[0].dtype)] * n_out, [], _pick(rows, (256, 128, 64)), label)

    nb = len(blocks)
    got = sibling_swap_blocks(blocks, name + "_d2d")
    q = [add_own_blocks(blocks[i], got[i], f"{name}_sum0_{i}") for i in range(nb)]
    r1 = pair_swap([q[i][j] for i in range(nb) for j in (2, 3)], name + "_ici1", "first")
    k = [add(2, [q[i][0], q[i][1], r1[2 * i], r1[2 * i + 1]], f"{name}_sum1_{i}") for i in range(nb)]
    r2 = pair_swap([k[i][1] for i in range(nb)], name + "_ici2", "second")
    return [jnp.stack([k[i][0], r2[i]]) for i in range(nb)]


def matmul(a, b, mode, out_dtype, name, add=None):
    if mode == "nn":
        (M, K), N = a.shape, b.shape[1]
    elif mode == "nt":
        (M, K), N = a.shape, b.shape[0]
    else:
        (K, M), N = a.shape, b.shape[1]
    tm = _pick(M, MATMUL_TILES)
    tn = _pick(N, MATMUL_TILES)
    osz = jnp.dtype(out_dtype).itemsize

    def vmem_bytes(tk):
        ops = 2 * tk * (tm * a.dtype.itemsize + tn * b.dtype.itemsize)
        return ops + tm * tn * (2 * osz + (4 if tk < K else 0) + (8 if add is not None else 0))

    tk = next((t for t in (K, 2816, 2048, 1408, 1024, 512, 256, 128) if K % t == 0 and vmem_bytes(t) <= MATMUL_VMEM_BUDGET), K)
    nk = K // tk
    dims = {"nn": NN, "nt": NT, "tn": TN}[mode]

    def body(*refs):
        if add is None:
            a_ref, b_ref, o_ref = refs[:3]
            c_ref = None
        else:
            a_ref, b_ref, c_ref, o_ref = refs[:4]

        def prod():
            return _dot(a_ref[...].astype(BF16), b_ref[...].astype(BF16), dims)

        def finish(r):
            if c_ref is not None:
                r = r + c_ref[...].astype(F32)
            o_ref[...] = r.astype(o_ref.dtype)

        if nk == 1:
            finish(prod())
            return
        acc = refs[-1]
        k = pl.program_id(2)

        @pl.when(k == 0)
        def _():
            acc[...] = prod()

        if nk > 2:
            @pl.when((k > 0) & (k < nk - 1))
            def _():
                acc[...] += prod()

        @pl.when(k == nk - 1)
        def _():
            finish(acc[...] + prod())

    a_spec = pl.BlockSpec((tk, tm), lambda i, j, k: (k, i)) if mode == "tn" else pl.BlockSpec((tm, tk), lambda i, j, k: (i, k))
    b_spec = pl.BlockSpec((tn, tk), lambda i, j, k: (j, k)) if mode == "nt" else pl.BlockSpec((tk, tn), lambda i, j, k: (k, j))
    o_spec = pl.BlockSpec((tm, tn), lambda i, j, k: (i, j))
    ins, specs = [a, b], [a_spec, b_spec]
    if add is not None:
        ins.append(add)
        specs.append(o_spec)
    return pl.pallas_call(
        body,
        grid=(M // tm, N // tn, nk),
        in_specs=specs,
        out_specs=o_spec,
        out_shape=jax.ShapeDtypeStruct((M, N), out_dtype),
        scratch_shapes=[pltpu.VMEM((tm, tn), F32)] if nk > 1 else [],
        compiler_params=_cp(("parallel", "parallel", "arbitrary")),
        name=name,
    )(*ins)


def rowwise(fn, tiled, full, out_tiled, out_acc, tile, name):
    tiled = [t if isinstance(t, tuple) else (t, t.shape[1], 0) for t in tiled]
    rows = tiled[0][0].shape[0]
    tile = min(tile, rows)
    assert rows % tile == 0
    nt, nf, no = len(tiled), len(full), len(out_tiled)

    def body(*refs):
        ins = [r[...] for r in refs[:nt + nf]]
        res = fn(*ins)
        if not isinstance(res, (tuple, list)):
            res = (res,)
        outs = refs[nt + nf:]
        for r, v in zip(outs[:no], res[:no]):
            r[...] = v.astype(r.dtype)
        if out_acc:
            @pl.when(pl.program_id(0) == 0)
            def _():
                for r in outs[no:]:
                    r[...] = jnp.zeros_like(r)

            for r, v in zip(outs[no:], res[no:]):
                r[...] += v

    in_specs = [pl.BlockSpec((tile, w), lambda i, cb=cb: (i, cb)) for (_, w, cb) in tiled]
    in_specs += [pl.BlockSpec(f.shape, lambda i, nd=f.ndim: (0,) * nd) for f in full]
    out_specs = [pl.BlockSpec((tile, w), lambda i: (i, 0)) for (w, _) in out_tiled]
    out_specs += [pl.BlockSpec(s, lambda i, nd=len(s): (0,) * nd) for s in out_acc]
    out_shape = [jax.ShapeDtypeStruct((rows, w), dt) for (w, dt) in out_tiled]
    out_shape += [jax.ShapeDtypeStruct(s, F32) for s in out_acc]
    res = pl.pallas_call(
        body,
        grid=(rows // tile,),
        in_specs=in_specs,
        out_specs=out_specs,
        out_shape=out_shape,
        compiler_params=_cp(("arbitrary",) if out_acc else ("parallel",)),
        name=name,
    )(*[t[0] for t in tiled], *full)
    return res


def adam_reduce(parts, w, m, v, name):
    P, R, C = parts.shape
    tr = _pick(R, (256, 128, 64, 32, 16, 8))
    c1 = 1.0 - ADAM_B1 ** ADAM_STEP
    c2 = 1.0 - ADAM_B2 ** ADAM_STEP

    def body(p_ref, w_ref, m_ref, v_ref, g_ref, d_ref, nm_ref, nv_ref):
        g = p_ref[0].astype(F32)
        for j in range(1, P):
            g = g + p_ref[j].astype(F32)
        mm = ADAM_B1 * m_ref[...] + (1.0 - ADAM_B1) * g
        vv = ADAM_B2 * v_ref[...] + (1.0 - ADAM_B2) * (g * g)
        m_hat = mm / c1
        v_hat = vv / c2
        g_ref[...] = g
        d_ref[...] = -ADAM_LR * (m_hat / (jnp.sqrt(v_hat) + ADAM_EPS) + ADAM_WD * w_ref[...])
        nm_ref[...] = mm
        nv_ref[...] = vv

    spec = pl.BlockSpec((tr, C), lambda i: (i, 0))
    return pl.pallas_call(
        body,
        grid=(R // tr,),
        in_specs=[pl.BlockSpec((P, tr, C), lambda i: (0, i, 0)), spec, spec, spec],
        out_specs=[spec] * 4,
        out_shape=[jax.ShapeDtypeStruct((R, C), F32)] * 4,
        compiler_params=_cp(("parallel",)),
        name=name,
    )(parts, w, m, v)


def reduce_parts(parts, name):
    P, R, C = parts.shape
    tr = _pick(R, (64, 32, 16, 8))

    def body(p_ref, g_ref):
        g = p_ref[0]
        for j in range(1, P):
            g = g + p_ref[j]
        g_ref[...] = g

    return pl.pallas_call(
        body,
        grid=(R // tr,),
        in_specs=[pl.BlockSpec((P, tr, C), lambda i: (0, i, 0))],
        out_specs=pl.BlockSpec((tr, C), lambda i: (i, 0)),
        out_shape=jax.ShapeDtypeStruct((R, C), F32),
        compiler_params=_cp(("parallel",)),
        name=name,
    )(parts)


def pack(arrs):
    flat = jnp.concatenate([a.reshape(-1).astype(F32) for a in arrs])
    n = flat.shape[0]
    padded = -(-n // 1024) * 1024
    return jnp.pad(flat, (0, padded - n)).reshape(padded // 128, 128)


def blocks_from_segments(segs, ncols):
    offs = np.cumsum([0] + [s.shape[1] for s in segs]).tolist()
    blocks = []
    for j in range(NDEV):
        lo, hi = j * ncols, (j + 1) * ncols
        parts = [s[:, max(lo, o) - o:min(hi, o + s.shape[1]) - o]
                 for s, o in zip(segs, offs[:-1]) if max(lo, o) < min(hi, o + s.shape[1])]
        blocks.append(jnp.concatenate(parts, axis=1) if len(parts) > 1 else parts[0])
    return jnp.stack(blocks)


def rows_to_blocks(g):
    return g.reshape(NDEV, -1, g.shape[1])


def segments_from_blocks(g, widths):
    ncols = g.shape[2]
    offs = np.cumsum([0] + list(widths)).tolist()
    out = []
    for o, w in zip(offs[:-1], widths):
        parts = [g[j][:, max(j * ncols, o) - j * ncols:min((j + 1) * ncols, o + w) - j * ncols]
                 for j in range(NDEV) if max(j * ncols, o) < min((j + 1) * ncols, o + w)]
        out.append(jnp.concatenate(parts, axis=1) if len(parts) > 1 else parts[0])
    return out


def unpack(slab, shapes):
    flat = slab.reshape(-1)
    out, off = [], 0
    for s in shapes:
        size = int(np.prod(s))
        out.append(flat[off:off + size].reshape(s))
        off += size
    return out


def modulate_fwd(x, g, sh, sc, name):
    def fn(x, g, sh, sc):
        return x * _rstd(x) * g * (1.0 + sc) + sh

    return rowwise(fn, [x], [g, sh, sc], [(x.shape[1], BF16)], [], 256, name)[0]


def norm_rope_fwd(p, width, cb, w, cosf, sinf, hd, name):
    nh = width // hd

    def fn(x, cosf, sinf, w):
        outs = []
        for h in range(nh):
            xh = x[:, h * hd:(h + 1) * hd]
            y = xh * _rstd(xh) * w
            outs.append(y * cosf + pltpu.roll(y, hd // 2, 1) * sinf)
        return jnp.concatenate(outs, axis=1) if nh > 1 else outs[0]

    return rowwise(fn, [(p, width, cb), cosf, sinf], [w], [(width, BF16)], [], 256, name)[0]


def norm_rope_bwd(p, width, cb, d, w, cosf, sinf, hd, name):
    nh = width // hd

    def fn(x, d, cosf, sinf, w):
        outs = []
        dw = jnp.zeros((1, hd), F32)
        for h in range(nh):
            xh = x[:, h * hd:(h + 1) * hd]
            dh = d[:, h * hd:(h + 1) * hd].astype(F32)
            r = _rstd(xh)
            n = xh * r
            dy = dh * cosf + pltpu.roll(dh * sinf, hd // 2, 1)
            dw = dw + jnp.sum(dy * n, axis=0, keepdims=True)
            dn = dy * w
            outs.append(r * (dn - n * jnp.mean(dn * n, axis=-1, keepdims=True)))
        return (jnp.concatenate(outs, axis=1) if nh > 1 else outs[0]), dw

    return rowwise(fn, [(p, width, cb), d, cosf, sinf], [w], [(width, BF16)], [(1, hd)], 256, name)


def attention_fwd(qr, kr, pkv, kcr, pkv_c, sink, hkv, hd, name):
    T, L = qr.shape[0], kcr.shape[0]
    G = qr.shape[1] // (hkv * hd)
    nb = T // BLOCK
    scale = hd ** -0.5

    def body(q_ref, kp, kc, kn, vp, vc, vn, ck_ref, cv_ref, sink_ref, o_ref, lse_ref, lser_ref):
        i = pl.program_id(1)
        kwin = jnp.concatenate([kp[...], kc[...], kn[...]], axis=0)
        vwin = jnp.concatenate([vp[...], vc[...], vn[...]], axis=0).astype(BF16)
        ck, cv = ck_ref[...], cv_ref[...].astype(BF16)
        row = lax.broadcasted_iota(jnp.int32, (BLOCK, 3 * BLOCK), 0)
        col = lax.broadcasted_iota(jnp.int32, (BLOCK, 3 * BLOCK), 1)
        rel = col - BLOCK - row
        valid = (jnp.abs(rel) <= WINDOW) & ((col >= BLOCK) | (i > 0)) & ((col < 2 * BLOCK) | (i < nb - 1))
        R = range(G)
        qa = q_ref[...]
        qs = [qa[:, g * hd:(g + 1) * hd] for g in R]
        sks = [sink_ref[g] for g in R]
        ss = [jnp.where(valid, _dot(qs[g], kwin, NT) * scale, NEG) for g in R]
        scs = [_dot(qs[g], ck, NT) * scale for g in R]
        ms = [jnp.maximum(jnp.maximum(jnp.max(ss[g], axis=1, keepdims=True), jnp.max(scs[g], axis=1, keepdims=True)),
                          sks[g]) for g in R]
        ps = [jnp.exp(ss[g] - ms[g]) for g in R]
        pcs = [jnp.exp(scs[g] - ms[g]) for g in R]
        nums = [_dot(ps[g].astype(BF16), vwin, NN) + _dot(pcs[g].astype(BF16), cv, NN) for g in R]
        dens = [jnp.exp(sks[g] - ms[g]) + jnp.sum(ps[g], axis=1, keepdims=True) + jnp.sum(pcs[g], axis=1, keepdims=True)
                for g in R]
        o_ref[...] = jnp.concatenate([(nums[g] / dens[g]).astype(o_ref.dtype) for g in R], axis=1)
        eye = (lax.broadcasted_iota(jnp.int32, (BLOCK, BLOCK), 0)
               == lax.broadcasted_iota(jnp.int32, (BLOCK, BLOCK), 1)).astype(F32)
        for g in R:
            lg = ms[g] + jnp.log(dens[g])
            lse_ref[g] = lg
            lser_ref[g] = jnp.sum(lg * eye, axis=0, keepdims=True)

    kv_specs = [
        pl.BlockSpec((BLOCK, hd), lambda h, i: (jnp.maximum(i - 1, 0), h)),
        pl.BlockSpec((BLOCK, hd), lambda h, i: (i, h)),
        pl.BlockSpec((BLOCK, hd), lambda h, i: (jnp.minimum(i + 1, nb - 1), h)),
    ]
    v_specs = [
        pl.BlockSpec((BLOCK, hd), lambda h, i: (jnp.maximum(i - 1, 0), hkv + h)),
        pl.BlockSpec((BLOCK, hd), lambda h, i: (i, hkv + h)),
        pl.BlockSpec((BLOCK, hd), lambda h, i: (jnp.minimum(i + 1, nb - 1), hkv + h)),
    ]
    return pl.pallas_call(
        body,
        grid=(hkv, nb),
        in_specs=[pl.BlockSpec((BLOCK, G * hd), lambda h, i: (i, h))] + kv_specs + v_specs + [
            pl.BlockSpec((L, hd), lambda h, i: (0, h)),
            pl.BlockSpec((L, hd), lambda h, i: (0, hkv + h)),
            pl.BlockSpec((G, 1, 1), lambda h, i: (h, 0, 0)),
        ],
        out_specs=[
            pl.BlockSpec((BLOCK, G * hd), lambda h, i: (i, h)),
            pl.BlockSpec((G, BLOCK, 1), lambda h, i: (h, i, 0)),
            pl.BlockSpec((G, 1, BLOCK), lambda h, i: (h, 0, i)),
        ],
        out_shape=[jax.ShapeDtypeStruct(qr.shape, BF16), jax.ShapeDtypeStruct((hkv * G, T, 1), F32),
                   jax.ShapeDtypeStruct((hkv * G, 1, T), F32)],
        compiler_params=_cp(("parallel", "parallel")),
        name=name,
    )(qr, kr, kr, kr, pkv, pkv, pkv, kcr, pkv_c, sink)


def attention_bwd_q(qr, kr, pkv, kcr, pkv_c, sink, do, o, lse, hkv, hd, name):
    T, L = qr.shape[0], kcr.shape[0]
    G = qr.shape[1] // (hkv * hd)
    nb = T // BLOCK
    scale = hd ** -0.5

    def body(q_ref, kp, kc, kn, vp, vc, vn, ck_ref, cv_ref, sink_ref, do_ref, o_ref, lse_ref,
             dq_ref, dck_ref, dcv_ref, dsink_ref, drr_ref):
        i = pl.program_id(1)

        @pl.when(i == 0)
        def _():
            dck_ref[...] = jnp.zeros_like(dck_ref)
            dcv_ref[...] = jnp.zeros_like(dcv_ref)
            dsink_ref[...] = jnp.zeros_like(dsink_ref)

        kwin = jnp.concatenate([kp[...], kc[...], kn[...]], axis=0)
        vwin = jnp.concatenate([vp[...], vc[...], vn[...]], axis=0).astype(BF16)
        ck, cv = ck_ref[...], cv_ref[...].astype(BF16)
        row = lax.broadcasted_iota(jnp.int32, (BLOCK, 3 * BLOCK), 0)
        col = lax.broadcasted_iota(jnp.int32, (BLOCK, 3 * BLOCK), 1)
        rel = col - BLOCK - row
        valid = (jnp.abs(rel) <= WINDOW) & ((col >= BLOCK) | (i > 0)) & ((col < 2 * BLOCK) | (i < nb - 1))
        R = range(G)
        qa, doa, oa = q_ref[...], do_ref[...], o_ref[...]
        qs = [qa[:, g * hd:(g + 1) * hd] for g in R]
        dos = [doa[:, g * hd:(g + 1) * hd] for g in R]
        lgs = [lse_ref[g] for g in R]
        sks = [sink_ref[g] for g in R]
        ss = [jnp.where(valid, _dot(qs[g], kwin, NT) * scale, NEG) for g in R]
        scs = [_dot(qs[g], ck, NT) * scale for g in R]
        dps = [_dot(dos[g], vwin, NT) for g in R]
        dpcs = [_dot(dos[g], cv, NT) for g in R]
        drs = [jnp.sum(dos[g].astype(F32) * oa[:, g * hd:(g + 1) * hd].astype(F32), axis=1, keepdims=True) for g in R]
        ps = [jnp.exp(ss[g] - lgs[g]) for g in R]
        pcs = [jnp.exp(scs[g] - lgs[g]) for g in R]
        dss = [(ps[g] * (dps[g] - drs[g]) * scale).astype(BF16) for g in R]
        dscs = [(pcs[g] * (dpcs[g] - drs[g]) * scale).astype(BF16) for g in R]
        dqs = [_dot(dss[g], kwin, NN) + _dot(dscs[g], ck, NN) for g in R]
        dcks = [_dot(dscs[g], qs[g], TN) for g in R]
        dcvs = [_dot(pcs[g].astype(BF16), dos[g], TN) for g in R]
        dq_ref[...] = jnp.concatenate(dqs, axis=1)
        dck_ref[...] += (dcks[0] + dcks[1]) + (dcks[2] + dcks[3]) if G == 4 else sum(dcks[1:], dcks[0])
        dcv_ref[...] += (dcvs[0] + dcvs[1]) + (dcvs[2] + dcvs[3]) if G == 4 else sum(dcvs[1:], dcvs[0])
        eye = (lax.broadcasted_iota(jnp.int32, (BLOCK, BLOCK), 0)
               == lax.broadcasted_iota(jnp.int32, (BLOCK, BLOCK), 1)).astype(F32)
        for g in R:
            dsink_ref[g] += -jnp.sum(jnp.exp(sks[g] - lgs[g]) * drs[g], axis=0, keepdims=True)
            drr_ref[g] = jnp.sum(drs[g] * eye, axis=0, keepdims=True)

    kv_specs = [
        pl.BlockSpec((BLOCK, hd), lambda h, i: (jnp.maximum(i - 1, 0), h)),
        pl.BlockSpec((BLOCK, hd), lambda h, i: (i, h)),
        pl.BlockSpec((BLOCK, hd), lambda h, i: (jnp.minimum(i + 1, nb - 1), h)),
    ]
    v_specs = [
        pl.BlockSpec((BLOCK, hd), lambda h, i: (jnp.maximum(i - 1, 0), hkv + h)),
        pl.BlockSpec((BLOCK, hd), lambda h, i: (i, hkv + h)),
        pl.BlockSpec((BLOCK, hd), lambda h, i: (jnp.minimum(i + 1, nb - 1), hkv + h)),
    ]
    qspec = pl.BlockSpec((BLOCK, G * hd), lambda h, i: (i, h))
    return pl.pallas_call(
        body,
        grid=(hkv, nb),
        in_specs=[qspec] + kv_specs + v_specs + [
            pl.BlockSpec((L, hd), lambda h, i: (0, h)),
            pl.BlockSpec((L, hd), lambda h, i: (0, hkv + h)),
            pl.BlockSpec((G, 1, 1), lambda h, i: (h, 0, 0)),
            qspec, qspec,
            pl.BlockSpec((G, BLOCK, 1), lambda h, i: (h, i, 0)),
        ],
        out_specs=[
            qspec,
            pl.BlockSpec((L, hd), lambda h, i: (0, h)),
            pl.BlockSpec((L, hd), lambda h, i: (0, h)),
            pl.BlockSpec((G, 1, 1), lambda h, i: (h, 0, 0)),
            pl.BlockSpec((G, 1, BLOCK), lambda h, i: (h, 0, i)),
        ],
        out_shape=[
            jax.ShapeDtypeStruct(qr.shape, F32),
            jax.ShapeDtypeStruct((L, hkv * hd), F32),
            jax.ShapeDtypeStruct((L, hkv * hd), F32),
            jax.ShapeDtypeStruct((hkv * G, 1, 1), F32),
            jax.ShapeDtypeStruct((hkv * G, 1, T), F32),
        ],
        compiler_params=_cp(("parallel", "arbitrary")),
        name=name,
    )(qr, kr, kr, kr, pkv, pkv, pkv, kcr, pkv_c, sink, do, o, lse)


def attention_bwd_kv(qr, kr, pkv, do, lse_row, dr_row, hkv, hd, name):
    T = qr.shape[0]
    G = qr.shape[1] // (hkv * hd)
    nb = T // BLOCK
    scale = hd ** -0.5

    def body(k_ref, v_ref, *refs):
        qs, dos, lses, drs = refs[0:3], refs[3:6], refs[6:9], refs[9:12]
        dk_ref, dv_ref = refs[12:]
        j = pl.program_id(1)
        k = k_ref[...]
        v = v_ref[...].astype(BF16)
        row = lax.broadcasted_iota(jnp.int32, (BLOCK, BLOCK), 0)
        col = lax.broadcasted_iota(jnp.int32, (BLOCK, BLOCK), 1)
        bias = []
        for d in range(3):
            iq = j + d - 1
            rel = row - col - (d - 1) * BLOCK
            valid = (jnp.abs(rel) <= WINDOW) & (iq >= 0) & (iq < nb)
            bias += [jnp.where(valid, 0.0, NEG)] * G
        bias = jnp.concatenate(bias, axis=1)

        def stack(refs):
            vals = [r[...] for r in refs]
            return jnp.concatenate([a[:, g * hd:(g + 1) * hd] for a in vals for g in range(G)], axis=0)

        q, dob = stack(qs), stack(dos)
        lrow = jnp.concatenate([r[g] for r in lses for g in range(G)], axis=1)
        drow = jnp.concatenate([r[g] for r in drs for g in range(G)], axis=1)
        st = _dot(k, q, NT) * scale + bias
        pt = jnp.exp(st - lrow)
        dpt = _dot(v, dob, NT)
        dst = (pt * (dpt - drow) * scale).astype(BF16)
        dk_ref[...] = _dot(dst, q, NN).astype(dk_ref.dtype)
        dv_ref[...] = _dot(pt.astype(BF16), dob, NN).astype(dv_ref.dtype)

    def q3(width_block):
        return [
            pl.BlockSpec(width_block, lambda h, j: (jnp.maximum(j - 1, 0), h)),
            pl.BlockSpec(width_block, lambda h, j: (j, h)),
            pl.BlockSpec(width_block, lambda h, j: (jnp.minimum(j + 1, nb - 1), h)),
        ]

    row3 = [
        pl.BlockSpec((G, 1, BLOCK), lambda h, j: (h, 0, jnp.maximum(j - 1, 0))),
        pl.BlockSpec((G, 1, BLOCK), lambda h, j: (h, 0, j)),
        pl.BlockSpec((G, 1, BLOCK), lambda h, j: (h, 0, jnp.minimum(j + 1, nb - 1))),
    ]
    qb = (BLOCK, G * hd)
    return pl.pallas_call(
        body,
        grid=(hkv, nb),
        in_specs=[pl.BlockSpec((BLOCK, hd), lambda h, j: (j, h)), pl.BlockSpec((BLOCK, hd), lambda h, j: (j, hkv + h))]
        + q3(qb) + q3(qb) + row3 + row3,
        out_specs=[pl.BlockSpec((BLOCK, hd), lambda h, j: (j, h))] * 2,
        out_shape=[jax.ShapeDtypeStruct((T, hkv * hd), BF16)] * 2,
        compiler_params=_cp(("parallel", "parallel")),
        name=name,
    )(kr, pkv, qr, qr, qr, do, do, do, lse_row, lse_row, lse_row, dr_row, dr_row, dr_row)


def gate_fwd(plr, wf, wb, bf, bb, name):
    n = wf.shape[1]

    def fn(lr, wf, wb, bf, bb):
        lrb = lr.astype(BF16)
        outs = []
        for w, b in ((wf, bf), (wb, bb)):
            z = _dot(lrb, w.astype(BF16), NN) + b
            outs.append((jnp.minimum(z, 0.0) - jnp.log(1.0 + jnp.exp(-jnp.abs(z)))) / GLA_GATE_NORM)
        return outs

    return rowwise(fn, [plr], [wf, wb, bf, bb], [(n, F32), (n, F32)], [], 256, name)


def gate_bwd(plr, dgf, dgb, wf, wb, bf, bb, name):
    n = wf.shape[1]

    def fn(lr, dgf, dgb, wf, wb, bf, bb):
        lrb = lr.astype(BF16)
        dlr = jnp.zeros(lr.shape, F32)
        res = []
        for w, b, dg in ((wf, bf, dgf), (wb, bb, dgb)):
            wb16 = w.astype(BF16)
            z = _dot(lrb, wb16, NN) + b
            dz = dg * _sig(-z) / GLA_GATE_NORM
            dzb = dz.astype(BF16)
            dlr = dlr + _dot(dzb, wb16, NT)
            res += [_dot(lrb, dzb, TN), jnp.sum(dz, axis=0, keepdims=True)]
        return [dlr] + res

    return rowwise(fn, [plr, dgf, dgb], [wf, wb, bf, bb], [(128, BF16)],
                   [(128, n), (1, n), (128, n), (1, n)], 256, name)


def _tri_dot(tri_b, x):
    x1 = x.astype(BF16)
    r1 = x - x1.astype(F32)
    x2 = r1.astype(BF16)
    x3 = (r1 - x2.astype(F32)).astype(BF16)
    return _dot(tri_b, x1, NN) + _dot(tri_b, x2, NN) + _dot(tri_b, x3, NN)


def gla_fwd(pqk, pv, gl, s0, heads, reverse, name, o_add=None):
    T = pqk.shape[0]
    dk = pqk.shape[1] // (2 * heads)
    dv = pv.shape[1] // heads
    C = GLA_CHUNK
    nc = T // C
    qscale = dk ** -0.5

    def body(*refs):
        if o_add is None:
            q_ref, k_ref, v_ref, g_ref, s0_ref, o_ref, st_ref, sf_ref, S = refs
            oa_ref = None
        else:
            q_ref, k_ref, v_ref, g_ref, s0_ref, oa_ref, o_ref, st_ref, sf_ref, S = refs
        n = pl.program_id(0)

        @pl.when(n == 0)
        def _():
            S[...] = s0_ref[...]

        r = lax.broadcasted_iota(jnp.int32, (C, C), 0)
        c = lax.broadcasted_iota(jnp.int32, (C, C), 1)
        tri = (r <= c) if reverse else (r >= c)
        trib = tri.astype(BF16)
        ga, qa, ka, va = g_ref[...], q_ref[...], k_ref[...], v_ref[...]
        sts = [S[h] for h in range(heads)]
        H = range(heads)
        gs = [ga[:, h * dk:(h + 1) * dk] for h in H]
        bs = [_tri_dot(trib, g) for g in gs]
        bls = [jnp.sum(g, axis=0, keepdims=True) for g in gs]
        mid = lax.broadcasted_iota(jnp.int32, (C, 1), 0) == C // 2
        bms = [jnp.sum(jnp.where(mid, b, 0.0), axis=0, keepdims=True) for b in bs]
        vs = [va[:, h * dv:(h + 1) * dv].astype(BF16) for h in H]
        qs = [qa[:, h * dk:(h + 1) * dk].astype(F32) * qscale for h in H]
        qes = [(qs[h] * jnp.exp(bs[h])).astype(BF16) for h in H]
        qms = [(qs[h] * jnp.exp(bs[h] - bms[h])).astype(BF16) for h in H]
        kms = [(ka[:, h * dk:(h + 1) * dk].astype(F32) * jnp.exp(bms[h] - bs[h])).astype(BF16) for h in H]
        kls = [(ka[:, h * dk:(h + 1) * dk].astype(F32) * jnp.exp(bls[h] - bs[h])).astype(BF16) for h in H]
        inter = [_dot(qes[h], sts[h].astype(BF16), NT) for h in H]
        upd = [_dot(vs[h], kls[h], TN) for h in H]
        As = [jnp.where(tri, _dot(qms[h], kms[h], NT), 0.0).astype(BF16) for h in H]
        outs = [inter[h] + _dot(As[h], vs[h], NN) for h in H]
        news = [sts[h] * jnp.exp(bls[h]) + upd[h] for h in H]
        o = jnp.concatenate(outs, axis=1)
        if oa_ref is not None:
            o = o + oa_ref[...]
        o_ref[...] = o
        for h in range(heads):
            st_ref[0, h] = sts[h]
            S[h] = news[h]

        @pl.when(n == nc - 1)
        def _():
            for h in range(heads):
                sf_ref[h] = news[h]

    def ci(n):
        return (nc - 1 - n) if reverse else n

    specs = [
        pl.BlockSpec((C, heads * dk), lambda n: (ci(n), 0)),
        pl.BlockSpec((C, heads * dk), lambda n: (ci(n), 1)),
        pl.BlockSpec((C, heads * dv), lambda n: (ci(n), 0)),
        pl.BlockSpec((C, heads * dk), lambda n: (ci(n), 0)),
        pl.BlockSpec((heads, dv, dk), lambda n: (0, 0, 0)),
    ]
    ins = [pqk, pqk, pv, gl, s0]
    if o_add is not None:
        specs.append(pl.BlockSpec((C, heads * dv), lambda n: (ci(n), 0)))
        ins.append(o_add)
    return pl.pallas_call(
        body,
        grid=(nc,),
        in_specs=specs,
        out_specs=[
            pl.BlockSpec((C, heads * dv), lambda n: (ci(n), 0)),
            pl.BlockSpec((1, heads, dv, dk), lambda n: (ci(n), 0, 0, 0)),
            pl.BlockSpec((heads, dv, dk), lambda n: (0, 0, 0)),
        ],
        out_shape=[
            jax.ShapeDtypeStruct((T, heads * dv), F32),
            jax.ShapeDtypeStruct((nc, heads, dv, dk), F32),
            jax.ShapeDtypeStruct((heads, dv, dk), F32),
        ],
        scratch_shapes=[pltpu.VMEM((heads, dv, dk), F32)],
        compiler_params=_cp(("arbitrary",)),
        name=name,
    )(*ins)


def gla_bwd(pqk, pv, gl, states, do, dsf, heads, reverse, name, acc=None):
    T = pqk.shape[0]
    dk = pqk.shape[1] // (2 * heads)
    dv = pv.shape[1] // heads
    C = GLA_CHUNK
    nc = T // C
    qscale = dk ** -0.5

    def body(*refs):
        if acc is None:
            q_ref, k_ref, v_ref, g_ref, st_ref, do_ref, dsf_ref, dq_ref, dk_ref, dv_ref, dg_ref, ds0_ref, dS = refs
            aq = ak = av = None
        else:
            (q_ref, k_ref, v_ref, g_ref, st_ref, do_ref, dsf_ref, aq, ak, av,
             dq_ref, dk_ref, dv_ref, dg_ref, ds0_ref, dS) = refs
        n = pl.program_id(0)

        @pl.when(n == 0)
        def _():
            dS[...] = dsf_ref[...]

        r = lax.broadcasted_iota(jnp.int32, (C, C), 0)
        c = lax.broadcasted_iota(jnp.int32, (C, C), 1)
        tri = (r <= c) if reverse else (r >= c)
        tri_t = (r >= c) if reverse else (r <= c)
        trib, tritb = tri.astype(BF16), tri_t.astype(BF16)
        ga, qa, ka, va, doa = g_ref[...], q_ref[...], k_ref[...], v_ref[...], do_ref[...]
        sts = [st_ref[0, h] for h in range(heads)]
        dsts = [dS[h] for h in range(heads)]
        H = range(heads)
        gs = [ga[:, h * dk:(h + 1) * dk] for h in H]
        bs = [_tri_dot(trib, g) for g in gs]
        bls = [jnp.sum(g, axis=0, keepdims=True) for g in gs]
        mid = lax.broadcasted_iota(jnp.int32, (C, 1), 0) == C // 2
        bms = [jnp.sum(jnp.where(mid, b, 0.0), axis=0, keepdims=True) for b in bs]
        ebs = [jnp.exp(b) for b in bs]
        embs = [jnp.exp(bs[h] - bms[h]) for h in H]
        enbs = [jnp.exp(bms[h] - bs[h]) for h in H]
        elbs = [jnp.exp(bls[h] - bs[h]) for h in H]
        ebls = [jnp.exp(bl) for bl in bls]
        vbs = [va[:, h * dv:(h + 1) * dv].astype(BF16) for h in H]
        dobs = [doa[:, h * dv:(h + 1) * dv].astype(BF16) for h in H]
        qs = [qa[:, h * dk:(h + 1) * dk].astype(F32) * qscale for h in H]
        qes = [qs[h] * ebs[h] for h in H]
        qms = [qs[h] * embs[h] for h in H]
        kms = [ka[:, h * dk:(h + 1) * dk].astype(F32) * enbs[h] for h in H]
        kls = [ka[:, h * dk:(h + 1) * dk].astype(F32) * elbs[h] for h in H]
        qebs = [a.astype(BF16) for a in qes]
        qmbs = [a.astype(BF16) for a in qms]
        kmbs = [a.astype(BF16) for a in kms]
        klbs = [a.astype(BF16) for a in kls]
        stbs = [a.astype(BF16) for a in sts]
        dstbs = [a.astype(BF16) for a in dsts]
        ps = [jnp.where(tri, _dot(qmbs[h], kmbs[h], NT), 0.0).astype(BF16) for h in H]
        dps = [jnp.where(tri, _dot(dobs[h], vbs[h], NT), 0.0).astype(BF16) for h in H]
        dqes = [_dot(dobs[h], stbs[h], NN) for h in H]
        dkls = [_dot(vbs[h], dstbs[h], NN) for h in H]
        dv1 = [_dot(klbs[h], dstbs[h], NT) for h in H]
        dsn1 = [_dot(dobs[h], qebs[h], TN) for h in H]
        dqms = [_dot(dps[h], kmbs[h], NN) for h in H]
        dkms = [_dot(dps[h], qmbs[h], TN) for h in H]
        dvs = [_dot(ps[h], dobs[h], TN) + dv1[h] for h in H]
        dbls = [ebls[h] * jnp.sum(dsts[h] * sts[h], axis=0, keepdims=True)
                + jnp.sum(dkls[h] * kls[h], axis=0, keepdims=True) for h in H]
        dsns = [dsn1[h] + dsts[h] * ebls[h] for h in H]
        dqs = [(dqes[h] * ebs[h] + dqms[h] * embs[h]) * qscale for h in H]
        dks = [dkms[h] * enbs[h] + dkls[h] * elbs[h] for h in H]
        dbs = [dqes[h] * qes[h] + dqms[h] * qms[h] - dkms[h] * kms[h] - dkls[h] * kls[h] for h in H]
        dgs = [_tri_dot(tritb, dbs[h]) + dbls[h] for h in H]
        dq, dkk, dvv = (jnp.concatenate(a, axis=1) for a in (dqs, dks, dvs))
        if aq is not None:
            dq = dq + aq[...].astype(F32)
            dkk = dkk + ak[...].astype(F32)
            dvv = dvv + av[...].astype(F32)
        dq_ref[...] = dq.astype(dq_ref.dtype)
        dk_ref[...] = dkk.astype(dk_ref.dtype)
        dv_ref[...] = dvv.astype(dv_ref.dtype)
        dg_ref[...] = jnp.concatenate(dgs, axis=1)
        for h in range(heads):
            dS[h] = dsns[h]

        @pl.when(n == nc - 1)
        def _():
            for h in range(heads):
                ds0_ref[h] = dsns[h]

    def ci(n):
        return n if reverse else (nc - 1 - n)

    kspec = pl.BlockSpec((C, heads * dk), lambda n: (ci(n), 0))
    vspec = pl.BlockSpec((C, heads * dv), lambda n: (ci(n), 0))
    sspec = pl.BlockSpec((heads, dv, dk), lambda n: (0, 0, 0))
    specs = [
        kspec,
        pl.BlockSpec((C, heads * dk), lambda n: (ci(n), 1)),
        vspec,
        kspec,
        pl.BlockSpec((1, heads, dv, dk), lambda n: (ci(n), 0, 0, 0)),
        vspec,
        sspec,
    ]
    ins = [pqk, pqk, pv, gl, states, do, dsf]
    odt = F32 if acc is None else BF16
    if acc is not None:
        specs += [kspec, kspec, vspec]
        ins += list(acc)
    return pl.pallas_call(
        body,
        grid=(nc,),
        in_specs=specs,
        out_specs=[kspec, kspec, vspec, kspec, sspec],
        out_shape=[
            jax.ShapeDtypeStruct((T, heads * dk), odt),
            jax.ShapeDtypeStruct((T, heads * dk), odt),
            jax.ShapeDtypeStruct((T, heads * dv), odt),
            jax.ShapeDtypeStruct((T, heads * dk), F32),
            jax.ShapeDtypeStruct((heads, dv, dk), F32),
        ],
        scratch_shapes=[pltpu.VMEM((heads, dv, dk), F32)],
        compiler_params=_cp(("arbitrary",)),
        name=name,
    )(*ins)


def gla_out_fwd(og, prb, gn, heads, name):
    dv = og.shape[1] // heads

    def fn(og, rb, gn):
        outs = []
        for h in range(heads):
            oh = og[:, h * dv:(h + 1) * dv]
            outs.append(oh * _rstd(oh) * gn)
        y = jnp.concatenate(outs, axis=1)
        return y * (rb * _sig(rb))

    return rowwise(fn, [og, prb], [gn], [(og.shape[1], BF16)], [], 256, name)[0]


def gla_out_bwd(og, prb, du, gn, heads, name):
    dv = og.shape[1] // heads

    def fn(og, rb, du, gn):
        sg = _sig(rb)
        silu = rb * sg
        dsilu = sg * (1.0 + rb * (1.0 - sg))
        dog, ys = [], []
        dgn = jnp.zeros((1, dv), F32)
        for h in range(heads):
            sl = slice(h * dv, (h + 1) * dv)
            oh = og[:, sl]
            r = _rstd(oh)
            n = oh * r
            ys.append(n * gn)
            dy = du[:, sl] * silu[:, sl]
            dgn = dgn + jnp.sum(dy * n, axis=0, keepdims=True)
            dn = dy * gn
            dog.append(r * (dn - n * jnp.mean(dn * n, axis=-1, keepdims=True)))
        y = jnp.concatenate(ys, axis=1)
        return jnp.concatenate(dog, axis=1), du * y * dsilu, dgn

    return rowwise(fn, [og, prb, du], [gn], [(og.shape[1], F32), (og.shape[1], BF16)], [(1, dv)], 256, name)


def conv_swiglu_fwd(ua_, ug_, cw, cb, name):
    T, F = ua_.shape
    tt = min(512, T)
    tc = _pick(F, (512, 256, 128))
    nt_, ncol = T // tt, F // tc
    H = CONV_HALO
    n = tt + 2 * H

    def body(ua, uap, uan, ug, ugp, ugn, wa, wg, ba, bg, f_ref):
        i = pl.program_id(1)
        keep_p = (i > 0).astype(F32)
        keep_n = (i < nt_ - 1).astype(F32)
        res = []
        for m, p, nx, w, b in ((ua, uap, uan, wa, ba), (ug, ugp, ugn, wg, bg)):
            x = jnp.concatenate([p[...] * keep_p, m[...], nx[...] * keep_n], axis=0)
            down, up = pltpu.roll(x, 1, 0)[H:H + tt], pltpu.roll(x, n - 1, 0)[H:H + tt]
            res.append(w[0] * down + w[1] * x[H:H + tt] + w[2] * up + b[...])
        a, g = res
        f_ref[...] = (a * _sig_tanh(a) * g).astype(f_ref.dtype)

    wspec = lambda off: pl.BlockSpec((3, 1, tc), lambda j, i: (0, 0, j + off))
    bspec = lambda off: pl.BlockSpec((1, tc), lambda j, i: (0, j + off))
    return pl.pallas_call(
        body,
        grid=(ncol, nt_),
        in_specs=conv_halo_specs(T, tt, tc, 0) + conv_halo_specs(T, tt, tc, 0)
        + [wspec(0), wspec(ncol), bspec(0), bspec(ncol)],
        out_specs=pl.BlockSpec((tt, tc), lambda j, i: (i, j)),
        out_shape=jax.ShapeDtypeStruct((T, F), BF16),
        compiler_params=_cp(("parallel", "parallel")),
        name=name,
    )(ua_, ua_, ua_, ug_, ug_, ug_, cw, cw, cb, cb)


CONV_HALO = 16


def conv_halo_specs(T, tt, tc, off):
    r = tt // CONV_HALO
    last = T // CONV_HALO - 1
    return [
        pl.BlockSpec((tt, tc), lambda j, i: (i, j + off)),
        pl.BlockSpec((CONV_HALO, tc), lambda j, i: (jnp.maximum(i * r - 1, 0), j + off)),
        pl.BlockSpec((CONV_HALO, tc), lambda j, i: (jnp.minimum((i + 1) * r, last), j + off)),
    ]


def conv_swiglu_bwd_fused(ua_, ug_, cw, cb, df, name):
    T, F = ua_.shape
    tt = min(512, T)
    tc = _pick(F, (512, 256, 128))
    nt_, ncol = T // tt, F // tc
    H = CONV_HALO
    n = tt + 2 * H

    def body(ua, uap, uan, ug, ugp, ugn, dm, dp_, dn, wa, wg, ba, bg, dua_ref, dug_ref, dwa, dwg, dba, dbg):
        i = pl.program_id(1)

        @pl.when(i == 0)
        def _():
            for r in (dwa, dwg, dba, dbg):
                r[...] = jnp.zeros_like(r)

        keep_p = (i > 0).astype(F32)
        keep_n = (i < nt_ - 1).astype(F32)

        def ext(m, p, nx):
            return jnp.concatenate([p[...].astype(F32) * keep_p, m[...].astype(F32), nx[...].astype(F32) * keep_n],
                                   axis=0)

        d = ext(dm, dp_, dn)
        conv, parts = [], []
        for m, p, nx, w, b in ((ua, uap, uan, wa, ba), (ug, ugp, ugn, wg, bg)):
            x = ext(m, p, nx)
            down, up = pltpu.roll(x, 1, 0), pltpu.roll(x, n - 1, 0)
            parts.append((down, x, up))
            conv.append(w[0] * down + w[1] * x + w[2] * up + b[...])
        a, g = conv
        sg = _sig_tanh(a)
        da = d * g * sg * (1.0 + a * (1.0 - sg))
        dg = d * a * sg
        for dd, w, (down, x, up), o_ref, dw, db in ((da, wa, parts[0], dua_ref, dwa, dba),
                                                    (dg, wg, parts[1], dug_ref, dwg, dbg)):
            du = w[0] * pltpu.roll(dd, n - 1, 0) + w[1] * dd + w[2] * pltpu.roll(dd, 1, 0)
            o_ref[...] = du[H:H + tt].astype(o_ref.dtype)
            ddm = dd[H:H + tt]
            dw[0] += jnp.sum(ddm * down[H:H + tt], axis=0, keepdims=True)
            dw[1] += jnp.sum(ddm * x[H:H + tt], axis=0, keepdims=True)
            dw[2] += jnp.sum(ddm * up[H:H + tt], axis=0, keepdims=True)
            db[...] += jnp.sum(ddm, axis=0, keepdims=True)

    wspec = lambda off: pl.BlockSpec((3, 1, tc), lambda j, i: (0, 0, j + off))
    bspec = lambda off: pl.BlockSpec((1, tc), lambda j, i: (0, j + off))
    tile = pl.BlockSpec((tt, tc), lambda j, i: (i, j))
    return pl.pallas_call(
        body,
        grid=(ncol, nt_),
        in_specs=conv_halo_specs(T, tt, tc, 0) + conv_halo_specs(T, tt, tc, 0) + conv_halo_specs(T, tt, tc, 0)
        + [wspec(0), wspec(ncol), bspec(0), bspec(ncol)],
        out_specs=[tile, tile, wspec(0), wspec(0), bspec(0), bspec(0)],
        out_shape=[
            jax.ShapeDtypeStruct((T, F), BF16), jax.ShapeDtypeStruct((T, F), BF16),
            jax.ShapeDtypeStruct((3, 1, F), F32), jax.ShapeDtypeStruct((3, 1, F), F32),
            jax.ShapeDtypeStruct((1, F), F32), jax.ShapeDtypeStruct((1, F), F32),
        ],
        compiler_params=_cp(("parallel", "arbitrary")),
        name=name,
    )(ua_, ua_, ua_, ug_, ug_, ug_, df, df, df, cw, cw, cb, cb)


def rope_tables(n, hd):
    rows = n // GRID_W
    row = jnp.repeat(jnp.arange(rows), GRID_W)
    col = jnp.tile(jnp.arange(GRID_W), rows)
    n_freq = hd // 4
    inv = ROPE_THETA ** (-jnp.arange(n_freq, dtype=F32) / n_freq)
    ang = jnp.concatenate([row[:, None] * inv, col[:, None] * inv], axis=-1)
    cos, sin = jnp.cos(ang), jnp.sin(ang)
    return jnp.concatenate([cos, cos], axis=-1), jnp.concatenate([-sin, sin], axis=-1)


def local_step(x, ctx, tgt, mod, modc, W, P):
    T, D = x.shape
    L = ctx.shape[0]
    hd, hq, hkv, gh = P["hd"], P["hq"], P["hkv"], P["gh"]
    sh1, sc1, g1, sh2, sc2, g2 = mod
    csh1, csc1 = modc
    kvw = hkv * hd
    gkw = W["gqk"].shape[1] // 2
    gdv = D // gh
    gdk = gkw // gh

    h = modulate_fwd(x, P["g_mix"], sh1, sc1, "mod1")
    hc = modulate_fwd(ctx, P["g_mix"], csh1, csc1, "mod1_ctx")
    pq = matmul(h, W["q"], "nn", F32, "proj_q")
    pkv = matmul(h, W["kv"], "nn", F32, "proj_kv")
    pgqk = matmul(h, W["gqk"], "nn", F32, "proj_gqk")
    pgv = matmul(h, W["gv"], "nn", F32, "proj_gv")
    prb = matmul(h, W["rb"], "nn", F32, "proj_rb")
    plr = matmul(h, W["lr"], "nn", F32, "proj_lr")
    pgab = matmul(h, W["gab"], "nn", F32, "proj_gab")
    pkv_c = matmul(hc, W["kv"], "nn", F32, "proj_kv_ctx")
    pgqk_c = matmul(hc, W["gqk"], "nn", F32, "proj_gqk_ctx")
    pgv_c = matmul(hc, W["gv"], "nn", F32, "proj_gv_ctx")
    plr_c = matmul(hc, W["lr"], "nn", F32, "proj_lr_ctx")

    cosf, sinf = rope_tables(T, hd)
    one_c, zero_c = jnp.ones((L, hd), F32), jnp.zeros((L, hd), F32)
    qr = norm_rope_fwd(pq, hq * hd, 0, P["q_norm"], cosf, sinf, hd, "qnorm")
    kr = norm_rope_fwd(pkv, kvw, 0, P["k_norm"], cosf, sinf, hd, "knorm")
    kcr = norm_rope_fwd(pkv_c, kvw, 0, P["k_norm"], one_c, zero_c, hd, "knorm_ctx")
    sink = P["attn_sink"].reshape(hq, 1, 1)
    o_attn, lse, lse_row = attention_fwd(qr, kr, pkv, kcr, pkv_c, sink, hkv, hd, "attn_fwd")

    gf, gb = gate_fwd(plr, W["gate_f"], W["gate_b"], P["b_gate_f"], P["b_gate_b"], "gates")
    gfc, gbc = gate_fwd(plr_c, W["gate_f"], W["gate_b"], P["b_gate_f"], P["b_gate_b"], "gates_ctx")
    zero_state = jnp.zeros((gh, gdv, gdk), F32)
    _, st_cf, s_cf = gla_fwd(pgqk_c, pgv_c, gfc, zero_state, gh, False, "gla_ctx_f")
    _, st_cb, s_cb = gla_fwd(pgqk_c, pgv_c, gbc, zero_state, gh, True, "gla_ctx_b")
    of, st_f, _ = gla_fwd(pgqk, pgv, gf, s_cf, gh, False, "gla_f")
    og, st_b, _ = gla_fwd(pgqk, pgv, gb, s_cb, gh, True, "gla_b", o_add=of)
    ug = gla_out_fwd(og, prb, P["gla_norm"], gh, "gla_out")

    ya = matmul(o_attn, W["attn_o"], "nn", F32, "attn_o")
    yg = matmul(ug, W["gla_o"], "nn", F32, "gla_o")

    def merge_fn(ya, yg, ga, gb_):
        return _sig(ga) * ya + _sig(gb_) * yg

    z = rowwise(merge_fn, [ya, yg, (pgab, D, 0), (pgab, D, 1)], [], [(D, BF16)], [], 256, "merge")[0]
    mo = matmul(z, W["out"], "nn", F32, "w_out")

    def res_fn(x, mo, g1, gffn, sh2, sc2):
        x1 = x + g1 * mo
        return x1, x1 * _rstd(x1) * gffn * (1.0 + sc2) + sh2

    x1, h2 = rowwise(res_fn, [x, mo], [g1, P["g_ffn"], sh2, sc2], [(D, F32), (D, BF16)], [], 256, "res_mod2")
    u_a = matmul(h2, W["up_a"], "nn", F32, "w_up_a")
    u_g = matmul(h2, W["up_g"], "nn", F32, "w_up_g")
    cw3 = W["conv_w"].reshape(3, 1, -1)
    f = conv_swiglu_fwd(u_a, u_g, cw3, P["conv_b"], "conv_swiglu")
    fo = matmul(f, W["down"], "nn", F32, "w_down")

    def final_fn(x1, fo, tgt, g2):
        e = x1 + g2 * fo - tgt
        dy = e * (1.0 / D)
        lsum = jnp.sum(jnp.sum(e * e, axis=1, keepdims=True), axis=0, keepdims=True)
        return dy, dy * g2, jnp.broadcast_to(lsum, (1, 128)), jnp.sum(dy * fo, axis=0, keepdims=True)

    dy, dfo, lsum, dg2 = rowwise(final_fn, [x1, fo, tgt], [g2], [(D, F32), (D, BF16)], [(1, 128), (1, D)], 256, "loss")
    df = matmul(dfo, W["down"], "nt", BF16, "d_f")
    dw_down = matmul(f, dfo, "tn", BF16, "dw_down")
    du_a, du_g, dcw_a, dcw_g, dcb_a, dcb_g = conv_swiglu_bwd_fused(u_a, u_g, cw3, P["conv_b"], df, "conv_swiglu_bwd")
    dh2 = matmul(du_a, W["up_a"], "nt", F32, "d_h2_a")
    dh2 = matmul(du_g, W["up_g"], "nt", F32, "d_h2_g", add=dh2)
    dw_up = [matmul(h2, du_a, "tn", BF16, "dw_up_a"), matmul(h2, du_g, "tn", BF16, "dw_up_g")]

    def mod2_bwd_fn(x1, dh, dy, mo, gffn, sc2, g1):
        r = _rstd(x1)
        n = x1 * r
        dyy = dh * (1.0 + sc2)
        dn = dyy * gffn
        dx1 = dy + r * (dn - n * jnp.mean(dn * n, axis=-1, keepdims=True))
        s0 = lambda a: jnp.sum(a, axis=0, keepdims=True)
        return dx1, dx1 * g1, s0(dyy * n), s0(dh), s0(dh * n * gffn), s0(dx1 * mo)

    dx1, dmo, dg_ffn, dsh2, dsc2, dg1 = rowwise(
        mod2_bwd_fn, [x1, dh2, dy, mo], [P["g_ffn"], sc2, g1], [(D, F32), (D, BF16)], [(1, D)] * 4, 256, "mod2_bwd")
    dz = matmul(dmo, W["out"], "nt", F32, "d_z")
    dw_out = matmul(z, dmo, "tn", BF16, "dw_out")

    def merge_bwd_fn(dz, ya, yg, ga, gb_):
        sa, sb = _sig(ga), _sig(gb_)
        return dz * sa, dz * sb, jnp.concatenate([dz * ya * sa * (1.0 - sa), dz * yg * sb * (1.0 - sb)], axis=1)

    dya, dyg, dpgab = rowwise(merge_bwd_fn, [dz, ya, yg, (pgab, D, 0), (pgab, D, 1)], [],
                              [(D, BF16), (D, BF16), (2 * D, BF16)], [], 256, "merge_bwd")
    do_attn = matmul(dya, W["attn_o"], "nt", BF16, "d_oattn")
    dw_attn_o = matmul(o_attn, dya, "tn", BF16, "dw_attn_o")
    dug = matmul(dyg, W["gla_o"], "nt", F32, "d_ug")
    dw_gla_o = matmul(ug, dyg, "tn", BF16, "dw_gla_o")
    dog, dprb, dgn = gla_out_bwd(og, prb, dug, P["gla_norm"], gh, "gla_out_bwd")

    dq1, dk1, dv1, dgf, ds_cf = gla_bwd(pgqk, pgv, gf, st_f, dog, zero_state, gh, False, "gla_f_bwd")
    dgq, dgk, dpgv, dgb, ds_cb = gla_bwd(pgqk, pgv, gb, st_b, dog, zero_state, gh, True, "gla_b_bwd",
                                          acc=(dq1, dk1, dv1))
    dpgqk = jnp.concatenate([dgq, dgk], axis=1)
    zero_do = jnp.zeros((L, gh * gdv), F32)
    cq1, ck1, cv1, dgfc, _ = gla_bwd(pgqk_c, pgv_c, gfc, st_cf, zero_do, ds_cf, gh, False, "gla_ctx_f_bwd")
    cq, ck, dpgv_c, dgbc, _ = gla_bwd(pgqk_c, pgv_c, gbc, st_cb, zero_do, ds_cb, gh, True, "gla_ctx_b_bwd",
                                      acc=(cq1, ck1, cv1))
    dpgqk_c = jnp.concatenate([cq, ck], axis=1)
    dplr, dwgf, dbgf, dwgb, dbgb = gate_bwd(plr, dgf, dgb, W["gate_f"], W["gate_b"], P["b_gate_f"], P["b_gate_b"], "gates_bwd")
    dplr_c, dwgf_c, dbgf_c, dwgb_c, dbgb_c = gate_bwd(plr_c, dgfc, dgbc, W["gate_f"], W["gate_b"], P["b_gate_f"],
                                                      P["b_gate_b"], "gates_ctx_bwd")

    dqr, dkc_r, dvc, dsink, dr_row = attention_bwd_q(qr, kr, pkv, kcr, pkv_c, sink, do_attn, o_attn, lse, hkv, hd,
                                                     "attn_bwd_q")
    dkr, dv = attention_bwd_kv(qr, kr, pkv, do_attn, lse_row, dr_row, hkv, hd, "attn_bwd_kv")
    dpq, dqn = norm_rope_bwd(pq, hq * hd, 0, dqr, P["q_norm"], cosf, sinf, hd, "qnorm_bwd")
    dpk, dkn = norm_rope_bwd(pkv, kvw, 0, dkr, P["k_norm"], cosf, sinf, hd, "knorm_bwd")
    dpk_c, dkn_c = norm_rope_bwd(pkv_c, kvw, 0, dkc_r, P["k_norm"], one_c, zero_c, hd, "knorm_ctx_bwd")
    dpkv = jnp.concatenate([dpk, dv], axis=1)
    dpkv_c = jnp.concatenate([dpk_c, dvc.astype(BF16)], axis=1)

    dw_q = matmul(h, dpq, "tn", BF16, "dw_q")
    dw_kv = matmul(h, dpkv, "tn", BF16, "dw_kv", add=matmul(hc, dpkv_c, "tn", F32, "dw_kv_ctx"))
    dw_gqk = matmul(h, dpgqk, "tn", BF16, "dw_gqk", add=matmul(hc, dpgqk_c, "tn", F32, "dw_gqk_ctx"))
    dw_gv = matmul(h, dpgv, "tn", BF16, "dw_gv", add=matmul(hc, dpgv_c, "tn", F32, "dw_gv_ctx"))
    dw_rb = matmul(h, dprb, "tn", BF16, "dw_rb")
    dw_lr = matmul(h, dplr, "tn", BF16, "dw_lr", add=matmul(hc, dplr_c, "tn", F32, "dw_lr_ctx"))
    dw_gab = matmul(h, dpgab, "tn", BF16, "dw_gab")
    lrw = P["lowrank"]
    dw_in = [dw_q, dw_kv, dw_gqk, dw_gv, dw_rb, dw_lr[:, :2 * lrw], dw_gab]

    dh = matmul(dpq, W["q"], "nt", F32, "dh_q")
    dh = matmul(dpkv, W["kv"], "nt", F32, "dh_kv", add=dh)
    dh = matmul(dpgqk, W["gqk"], "nt", F32, "dh_gqk", add=dh)
    dh = matmul(dpgv, W["gv"], "nt", F32, "dh_gv", add=dh)
    dh = matmul(dprb, W["rb"], "nt", F32, "dh_rb", add=dh)
    dh = matmul(dplr, W["lr"], "nt", F32, "dh_lr", add=dh)
    dh = matmul(dpgab, W["gab"], "nt", F32, "dh_gab", add=dh)
    dhc = matmul(dpkv_c, W["kv"], "nt", F32, "dhc_kv")
    dhc = matmul(dpgqk_c, W["gqk"], "nt", F32, "dhc_gqk", add=dhc)
    dhc = matmul(dpgv_c, W["gv"], "nt", F32, "dhc_gv", add=dhc)
    dhc = matmul(dplr_c, W["lr"], "nt", F32, "dhc_lr", add=dhc)

    def mod1_bwd_fn(x, dh, dres, g, sc):
        r = _rstd(x)
        n = x * r
        dyy = dh * (1.0 + sc)
        dn = dyy * g
        dx = dres + r * (dn - n * jnp.mean(dn * n, axis=-1, keepdims=True))
        s0 = lambda a: jnp.sum(a, axis=0, keepdims=True)
        return dx, s0(dyy * n), s0(dh), s0(dh * n * g)

    grad_x, dgmix, dsh1, dsc1 = rowwise(mod1_bwd_fn, [x, dh, dx1], [P["g_mix"], sc1], [(D, F32)], [(1, D)] * 3,
                                        256, "mod1_bwd")
    _, dgmix_c, dcsh1, dcsc1 = rowwise(mod1_bwd_fn, [ctx, dhc, jnp.zeros_like(ctx)], [P["g_mix"], csc1], [(D, F32)],
                                       [(1, D)] * 3, 128, "mod1_ctx_bwd")

    zD = jnp.zeros((1, D), F32)
    grads = dict(
        w_in=dw_in, w_attn_o=dw_attn_o, w_gla_o=dw_gla_o, w_out=dw_out, w_up=dw_up, w_down=dw_down,
        dmod_x=jnp.concatenate([dsh1, dsc1, dg1, dsh2, dsc2, dg2], axis=1),
        dmod_c=jnp.concatenate([dcsh1, dcsc1, zD, zD, zD, zD], axis=1),
        g_mix=dgmix + dgmix_c, q_norm=dqn, k_norm=dkn + dkn_c, attn_sink=dsink.reshape(1, hq),
        w_gate_f=(dwgf + dwgf_c)[:lrw], b_gate_f=dbgf + dbgf_c,
        w_gate_b=(dwgb + dwgb_c)[lrw:2 * lrw], b_gate_b=dbgb + dbgb_c,
        gla_norm=dgn, g_ffn=dg_ffn,
        conv_w=jnp.concatenate([dcw_a, dcw_g], axis=2).reshape(3, -1),
        conv_b=jnp.concatenate([dcb_a, dcb_g], axis=1),
    )
    return lsum[0, 0], grad_x, grads


SMALL_REPL = ("c_ctx", "b_mod", "g_mix", "q_norm", "k_norm", "attn_sink", "b_gate_f", "b_gate_b", "gla_norm", "g_ffn",
              "conv_b")
SMALL_SHARD = ("w_gate_f", "w_gate_b", "conv_w")
ORDER = ("c_ctx", "w_mod", "b_mod", "g_mix", "w_in", "q_norm", "k_norm", "attn_sink", "w_gate_f", "b_gate_f",
         "w_gate_b", "b_gate_b", "gla_norm", "w_attn_o", "w_gla_o", "w_out", "g_ffn", "w_up", "conv_w", "conv_b",
         "w_down")


def kernel(x, c, ctx, c_ctx, w_mod, b_mod, g_mix, w_in, q_norm, k_norm, attn_sink, w_gate_f, b_gate_f, w_gate_b, b_gate_b, gla_norm, w_attn_o, w_gla_o, w_out, g_ffn, w_up, conv_w, conv_b, w_down, loss_target, m_c_ctx, m_w_mod, m_b_mod, m_g_mix, m_w_in, m_q_norm, m_k_norm, m_attn_sink, m_w_gate_f, m_b_gate_f, m_w_gate_b, m_b_gate_b, m_gla_norm, m_w_attn_o, m_w_gla_o, m_w_out, m_g_ffn, m_w_up, m_conv_w, m_conv_b, m_w_down, v_c_ctx, v_w_mod, v_b_mod, v_g_mix, v_w_in, v_q_norm, v_k_norm, v_attn_sink, v_w_gate_f, v_b_gate_f, v_w_gate_b, v_b_gate_b, v_gla_norm, v_w_attn_o, v_w_gla_o, v_w_out, v_g_ffn, v_w_up, v_conv_w, v_conv_b, v_w_down):
    loc = dict(locals())
    Wt = {n: loc[n] for n in ORDER}
    Mt = {n: loc["m_" + n] for n in ORDER}
    Vt = {n: loc["v_" + n] for n in ORDER}
    me = 4 * lax.axis_index("x") + 2 * lax.axis_index("y") + lax.axis_index("c")

    D = x.shape[-1]
    hd = q_norm.shape[-1]
    hq = attn_sink.shape[-1]
    gdv = gla_norm.shape[-1]
    gh = D // gdv
    gdk = D // 2 // gh
    lrw = w_gate_f.shape[1]
    in_w = NDEV * w_in.shape[-1]
    kvw = (in_w - hq * hd - 2 * gh * gdk - 2 * gh * gdv - 2 * lrw - 2 * D) // 2
    hkv = kvw // hd
    gcols = w_gate_f.shape[-1]
    mcols = w_mod.shape[-1]

    x2, ctx2, tgt2 = x[0], ctx[0], loss_target[0]

    c_all = exchange([jnp.pad(c, ((0, 7), (0, 0)))], True, "gather_c")[0][:, 0, :]
    c9 = jnp.concatenate([c_all, c_ctx[None, :], jnp.zeros((7, D), F32)], axis=0)
    s9 = rowwise(lambda a: a * _sig(a), [c9], [], [(D, F32)], [], 16, "silu_c")[0]
    bias = jnp.broadcast_to(lax.dynamic_slice_in_dim(b_mod, me * mcols, mcols, axis=1), (16, mcols))
    mod_cols = matmul(s9, w_mod[0], "nn", F32, "mod_cols", add=bias)
    mod_all = exchange([mod_cols], True, "gather_mod")[0]
    mod_all = jnp.transpose(mod_all, (1, 0, 2)).reshape(16, NDEV * mcols)
    mod_me = lax.dynamic_slice_in_dim(mod_all, me, 1, axis=0)
    mod = [mod_me[:, i * D:(i + 1) * D] for i in range(6)]
    modc = [mod_all[8:9, i * D:(i + 1) * D] for i in range(2)]

    o3 = jnp.stack([w_attn_o[0], w_gla_o[0], w_out[0]]).astype(BF16)
    small_w = pack([w_gate_f[0], w_gate_b[0], conv_w[0]])
    g_in, g_o3, g_up, g_down, g_small = gather_two_level(
        [w_in[0].astype(BF16), o3, w_up[0].astype(BF16), w_down[0].astype(BF16), small_w], "gather_w")
    seg = segments_from_blocks(g_in, [hq * hd, 2 * kvw, 2 * gh * gdk, gh * gdv, gh * gdv, 2 * lrw, 2 * D])
    small_parts = [unpack(g_small[j], [w_gate_f[0].shape, w_gate_b[0].shape, conv_w[0].shape]) for j in range(NDEV)]
    wgf = jnp.concatenate([p[0] for p in small_parts], axis=1)
    wgb = jnp.concatenate([p[1] for p in small_parts], axis=1)
    cw_full = jnp.concatenate([p[2] for p in small_parts], axis=1)
    o3f = [g_o3[:, i].reshape(-1, D) for i in range(3)]
    W = dict(
        q=seg[0], kv=seg[1], gqk=seg[2], gv=seg[3], rb=seg[4],
        lr=jnp.pad(seg[5], ((0, 0), (0, 128 - 2 * lrw))), gab=seg[6],
        gate_f=jnp.pad(wgf, ((0, 128 - lrw), (0, 0))),
        gate_b=jnp.pad(wgb, ((lrw, 128 - 2 * lrw), (0, 0))),
        attn_o=o3f[0], gla_o=o3f[1], out=o3f[2],
        up_a=jnp.concatenate([g_up[j] for j in range(NDEV // 2)], axis=1),
        up_g=jnp.concatenate([g_up[j] for j in range(NDEV // 2, NDEV)], axis=1),
        down=g_down.reshape(-1, D),
        conv_w=cw_full,
    )
    P = dict(hd=hd, hq=hq, hkv=hkv, gh=gh, lowrank=lrw, g_mix=g_mix, q_norm=q_norm, k_norm=k_norm, attn_sink=attn_sink,
             b_gate_f=b_gate_f, b_gate_b=b_gate_b, gla_norm=gla_norm, g_ffn=g_ffn, conv_b=conv_b)

    lsum, grad_x, G = local_step(x2, ctx2, tgt2, mod, modc, W, P)
    loss = lax.psum(0.5 * lsum / D, ("x", "y", "c"))

    dm = exchange([jnp.concatenate([G["dmod_x"], G["dmod_c"], jnp.zeros((6, 6 * D), F32)], axis=0)], True,
                  "gather_dmod")[0]
    dmc = reduce_parts(dm[:, 1:2, :].reshape(NDEV, 6 * D // 128, 128), "sum_dmod_ctx").reshape(1, 6 * D)
    dM = jnp.concatenate([dm[:, 0, :], dmc, jnp.zeros((7, 6 * D), F32)], axis=0)
    dM_cols = lax.dynamic_slice_in_dim(dM, me * mcols, mcols, axis=1)
    g_w_mod = matmul(s9, dM_cols, "tn", F32, "dw_mod")
    g_b_mod = reduce_parts(dM.reshape(16, 6 * D // 128, 128), "sum_db_mod").reshape(1, 6 * D)
    dsc = matmul(dM_cols[8:16], w_mod[0], "nt", F32, "d_silu_ctx")
    cc = jnp.broadcast_to(c_ctx[None, :], (8, D))

    def dsilu_fn(d, a):
        sg = _sig(a)
        return d * sg * (1.0 + a * (1.0 - sg))

    g_cctx_part = rowwise(dsilu_fn, [dsc, cc], [], [(D, F32)], [], 8, "d_c_ctx")[0][0:1]

    small_names = ("c_ctx", "g_mix", "q_norm", "k_norm", "attn_sink", "b_gate_f", "b_gate_b", "gla_norm", "g_ffn",
                   "conv_b", "w_gate_f", "w_gate_b", "conv_w")
    G["c_ctx"] = g_cctx_part
    sm_shapes = [G[n].shape for n in small_names]
    sm_all = exchange([pack([G[n] for n in small_names])], True, "gather_small_grads")[0]
    sm_tot = unpack(reduce_parts(sm_all, "sum_small_grads"), sm_shapes)
    gs = dict(zip(small_names, sm_tot))
    gs["b_mod"] = g_b_mod
    gs["w_gate_f"] = lax.dynamic_slice_in_dim(gs["w_gate_f"], me * gcols, gcols, axis=1)
    gs["w_gate_b"] = lax.dynamic_slice_in_dim(gs["w_gate_b"], me * gcols, gcols, axis=1)
    ccols = conv_w.shape[-1]
    gs["conv_w"] = lax.dynamic_slice_in_dim(gs["conv_w"], me * ccols, ccols, axis=1)

    orows = w_attn_o.shape[1]
    s_in = blocks_from_segments(G["w_in"], w_in.shape[-1])
    s_o3 = jnp.concatenate([rows_to_blocks(G["w_attn_o"]), rows_to_blocks(G["w_gla_o"]), rows_to_blocks(G["w_out"])],
                           axis=1)
    s_up = blocks_from_segments(G["w_up"], w_up.shape[-1])
    s_down = rows_to_blocks(G["w_down"])
    r_in, r_o3, r_up, r_down = scatter_reduce([s_in, s_o3, s_up, s_down], "scatter_grads")

    out = {}
    out["w_in"] = adam_reduce(r_in, w_in[0], m_w_in[0], v_w_in[0], "adam_w_in")
    o3w = jnp.concatenate([w_attn_o[0], w_gla_o[0], w_out[0]], axis=0)
    o3m = jnp.concatenate([m_w_attn_o[0], m_w_gla_o[0], m_w_out[0]], axis=0)
    o3v = jnp.concatenate([v_w_attn_o[0], v_w_gla_o[0], v_w_out[0]], axis=0)
    ro3 = adam_reduce(r_o3, o3w, o3m, o3v, "adam_o3")
    for i, n in enumerate(("w_attn_o", "w_gla_o", "w_out")):
        out[n] = [a[i * orows:(i + 1) * orows] for a in ro3]
    out["w_up"] = adam_reduce(r_up, w_up[0], m_w_up[0], v_w_up[0], "adam_w_up")
    out["w_down"] = adam_reduce(r_down, w_down[0], m_w_down[0], v_w_down[0], "adam_w_down")
    out["w_mod"] = adam_reduce(g_w_mod[None], w_mod[0], m_w_mod[0], v_w_mod[0], "adam_w_mod")
    sm_names = SMALL_REPL + SMALL_SHARD
    shapes = [Wt[n].shape for n in sm_names]
    rs = adam_reduce(pack([gs[n] for n in sm_names])[None], pack([Wt[n] for n in sm_names]),
                     pack([Mt[n] for n in sm_names]), pack([Vt[n] for n in sm_names]), "adam_small")
    rs = [unpack(a, shapes) for a in rs]
    for i, n in enumerate(sm_names):
        out[n] = [a[i] for a in rs]

    res = [loss, grad_x[None]]
    for k in range(4):
        for n in ORDER:
            res.append(out[n][k].reshape(Wt[n].shape))
    return tuple(res)
```

```python
import jax
import jax.numpy as jnp
import numpy as np
from jax import lax
from jax.experimental import pallas as pl
from jax.experimental.pallas import tpu as pltpu

F32 = jnp.float32
BF16 = jnp.bfloat16

NDEV = 8
NCHIP = 4
EPS = 1e-6
WINDOW = 128
BLOCK = 128
GRID_W = 64
ROPE_THETA = 10000.0
GLA_CHUNK = 128
GLA_GATE_NORM = 16.0
ADAM_LR = 0.001
ADAM_B1 = 0.9
ADAM_B2 = 0.999
ADAM_EPS = 1e-08
ADAM_WD = 0.01
ADAM_STEP = 10
V7X_VMEM_LIMIT = 56 * 1024 * 1024
MATMUL_VMEM_BUDGET = 44 * 1024 * 1024
MATMUL_TILES = (1024, 1408, 512, 256, 128)
NEG = -1e30

NN = ((1,), (0,))
NT = ((1,), (1,))
TN = ((0,), (0,))


def _dot(a, b, dims):
    return lax.dot_general(a, b, (dims, ((), ())), preferred_element_type=F32)


def _cp(sem):
    return pltpu.CompilerParams(dimension_semantics=sem, vmem_limit_bytes=V7X_VMEM_LIMIT)


def _pick(n, cands):
    for c in cands:
        if n % c == 0:
            return c
    return n


def _sig(x):
    return 1.0 / (1.0 + jnp.exp(-x))


def _sig_tanh(x):
    return 0.5 * jnp.tanh(0.5 * x) + 0.5


def _rstd(x):
    return lax.rsqrt(jnp.mean(x * x, axis=-1, keepdims=True) + EPS)


_ANY = pl.BlockSpec(memory_space=pl.ANY)


def _place():
    return lax.axis_index("x"), lax.axis_index("y"), lax.axis_index("c")


def exchange(srcs, bcast, name, group="all"):
    n = len(srcs)
    ndev = NDEV if group == "all" else NCHIP
    ks = tuple(range(1, NDEV)) if group == "all" else (2, 4, 6)
    out_shape = [jax.ShapeDtypeStruct((ndev,) + (s.shape if bcast else s.shape[1:]), s.dtype) for s in srcs]

    def body(*refs):
        src, dst = refs[:n], refs[n:2 * n]
        send_sems, recv_sems, loc_sems = refs[2 * n:]
        x, y, c = _place()

        def idx(px, py, pc):
            return 4 * px + 2 * py + pc if group == "all" else 2 * px + py

        me = idx(x, y, c)
        copies = []
        for a in range(n):
            cp = pltpu.make_async_copy(src[a] if bcast else src[a].at[me], dst[a].at[me], loc_sems.at[a])
            cp.start()
            copies.append(cp)
        for s, k in enumerate(ks):
            px, py, pc = x ^ ((k >> 2) & 1), y ^ ((k >> 1) & 1), c ^ (k & 1)
            for a in range(n):
                cp = pltpu.make_async_remote_copy(
                    src_ref=src[a] if bcast else src[a].at[idx(px, py, pc)],
                    dst_ref=dst[a].at[me],
                    send_sem=send_sems.at[a, s],
                    recv_sem=recv_sems.at[a, s],
                    device_id=(px, py, pc),
                    device_id_type=pl.DeviceIdType.MESH,
                )
                cp.start()
                copies.append(cp)
        for cp in copies:
            cp.wait()

    return pl.pallas_call(
        body,
        out_shape=out_shape,
        in_specs=[_ANY] * n,
        out_specs=[_ANY] * n,
        scratch_shapes=[
            pltpu.SemaphoreType.DMA((n, len(ks))),
            pltpu.SemaphoreType.DMA((n, len(ks))),
            pltpu.SemaphoreType.DMA((n,)),
        ],
        name=name,
    )(*srcs)


def gather_two_level(srcs, name):
    n = len(srcs)
    out_shape = [jax.ShapeDtypeStruct((NDEV,) + s.shape, s.dtype) for s in srcs]

    def body(*refs):
        src, dst = refs[:n], refs[n:2 * n]
        send_sems, recv_sems, loc_sems = refs[2 * n:]
        x, y, c = _place()
        me = 4 * x + 2 * y + c
        sib = (x, y, 1 - c)
        first = (x ^ (1 - c), y ^ c)
        second = (x ^ c, y ^ (1 - c))
        diag = (x ^ 1, y ^ 1)

        def row(chip, core):
            return 4 * chip[0] + 2 * chip[1] + core

        def copy(a, s, block, to, from_src=False):
            return pltpu.make_async_remote_copy(
                src_ref=src[a] if from_src else dst[a].at[block], dst_ref=dst[a].at[block],
                send_sem=send_sems.at[a, s], recv_sem=recv_sems.at[a, s],
                device_id=to, device_id_type=pl.DeviceIdType.MESH)

        local = [pltpu.make_async_copy(src[a], dst[a].at[me], loc_sems.at[a]) for a in range(n)]
        sent = [copy(a, 0, me, sib, True) for a in range(n)]
        sent += [copy(a, 1, me, (*first, c), True) for a in range(n)]
        sent += [copy(a, 2, me, (*second, c), True) for a in range(n)]
        for cp in local + sent:
            cp.start()
        for a in range(n):
            copy(a, 1, row(first, c), (*first, c)).wait_recv()
            for cp in (copy(a, 3, row(first, c), (*second, c)), copy(a, 5, row(first, c), sib)):
                cp.start()
                sent.append(cp)
        for a in range(n):
            copy(a, 2, row(second, c), (*second, c)).wait_recv()
            cp = copy(a, 4, row(second, c), sib)
            cp.start()
            sent.append(cp)
        for a in range(n):
            copy(a, 3, row(diag, c), (*second, c)).wait_recv()
            cp = copy(a, 6, row(diag, c), sib)
            cp.start()
            sent.append(cp)
        for a in range(n):
            copy(a, 0, row((x, y), 1 - c), sib).wait_recv()
            copy(a, 4, row(first, 1 - c), sib).wait_recv()
            copy(a, 5, row(second, 1 - c), sib).wait_recv()
            copy(a, 6, row(diag, 1 - c), sib).wait_recv()
        for cp in local:
            cp.wait()
        for cp in sent:
            cp.wait_send()

    return pl.pallas_call(
        body,
        out_shape=out_shape,
        in_specs=[_ANY] * n,
        out_specs=[_ANY] * n,
        scratch_shapes=[
            pltpu.SemaphoreType.DMA((n, NDEV - 1)),
            pltpu.SemaphoreType.DMA((n, NDEV - 1)),
            pltpu.SemaphoreType.DMA((n,)),
        ],
        name=name,
    )(*srcs)


def _chip_across(core, da, db):
    x, y, _ = _place()
    return x ^ (da * (1 - core) + db * core), y ^ (db * (1 - core) + da * core)


def pair_swap(srcs, name, axis="c"):
    n = len(srcs)

    def body(*refs):
        src, dst = refs[:n], refs[n:2 * n]
        send_sems, recv_sems = refs[2 * n:]
        x, y, c = _place()
        partner = {"c": (x, y, 1 - c), "first": (*_chip_across(c, 1, 0), c), "second": (*_chip_across(c, 0, 1), c)}[axis]
        copies = []
        for a in range(n):
            cp = pltpu.make_async_remote_copy(
                src_ref=src[a], dst_ref=dst[a], send_sem=send_sems.at[a], recv_sem=recv_sems.at[a],
                device_id=partner, device_id_type=pl.DeviceIdType.MESH)
            cp.start()
            copies.append(cp)
        for cp in copies:
            cp.wait()

    return pl.pallas_call(
        body,
        out_shape=[jax.ShapeDtypeStruct(s.shape, s.dtype) for s in srcs],
        in_specs=[_ANY] * n,
        out_specs=[_ANY] * n,
        scratch_shapes=[pltpu.SemaphoreType.DMA((n,)), pltpu.SemaphoreType.DMA((n,))],
        name=name,
    )(*srcs)


_OFFSETS = ((0, 0), (0, 1), (1, 0), (1, 1))


def sibling_swap_blocks(blocks, name):
    n = len(blocks)

    def body(*refs):
        src, dst = refs[:n], refs[n:2 * n]
        send_sems, recv_sems = refs[2 * n:]
        x, y, c = _place()
        copies = []
        for a in range(n):
            for j, (da, db) in enumerate(_OFFSETS):
                px, py = _chip_across(1 - c, da, db)
                cp = pltpu.make_async_remote_copy(
                    src_ref=src[a].at[4 * px + 2 * py + (1 - c)], dst_ref=dst[a].at[j],
                    send_sem=send_sems.at[a, j], recv_sem=recv_sems.at[a, j],
                    device_id=(x, y, 1 - c), device_id_type=pl.DeviceIdType.MESH)
                cp.start()
                copies.append(cp)
        for cp in copies:
            cp.wait()

    return pl.pallas_call(
        body,
        out_shape=[jax.ShapeDtypeStruct((4,) + b.shape[1:], b.dtype) for b in blocks],
        in_specs=[_ANY] * n,
        out_specs=[_ANY] * n,
        scratch_shapes=[pltpu.SemaphoreType.DMA((n, 4)), pltpu.SemaphoreType.DMA((n, 4))],
        name=name,
    )(*blocks)


def add_own_blocks(blocks, got, name):
    _, R, C = blocks.shape
    tile = _pick(R, (256, 128, 64))

    def body(*refs):
        for j in range(4):
            refs[8 + j][...] = (refs[j][...].astype(F32) + refs[4 + j][...].astype(F32)).astype(refs[8 + j].dtype)

    def own(da, db):
        def index(i):
            c = lax.axis_index("c")
            px, py = _chip_across(c, da, db)
            return 4 * px + 2 * py + c, i, 0
        return pl.BlockSpec((None, tile, C), index)

    return pl.pallas_call(
        body,
        grid=(R // tile,),
        in_specs=[own(da, db) for da, db in _OFFSETS]
        + [pl.BlockSpec((None, tile, C), lambda i, j=j: (j, i, 0)) for j in range(4)],
        out_specs=[pl.BlockSpec((tile, C), lambda i: (i, 0))] * 4,
        out_shape=[jax.ShapeDtypeStruct((R, C), blocks.dtype)] * 4,
        compiler_params=_cp(("parallel",)),
        name=name,
    )(blocks, blocks, blocks, blocks, got, got, got, got)


def scatter_reduce(blocks, name):
    def add(n_out, ins, label):
        fn = lambda *a: [a[i].astype(F32) + a[n_out + i].astype(F32) for i in range(n_out)]
        rows, cols = ins[0].shape
        return rowwise(fn, ins, [], [(cols, ins[0].dtype)] * n_out, [], _pick(rows, (256, 128, 64)), label)

    nb = len(blocks)
    got = sibling_swap_blocks(blocks, name + "_d2d")
    q = [add_own_blocks(blocks[i], got[i], f"{name}_sum0_{i}") for i in range(nb)]
    r1 = pair_swap([q[i][j] for i in range(nb) for j in (2, 3)], name + "_ici1", "first")
    k = [add(2, [q[i][0], q[i][1], r1[2 * i], r1[2 * i + 1]], f"{name}_sum1_{i}") for i in range(nb)]
    r2 = pair_swap([k[i][1] for i in range(nb)], name + "_ici2", "second")
    return [jnp.stack([k[i][0], r2[i]]) for i in range(nb)]


def matmul(a, b, mode, out_dtype, name, add=None):
    if mode == "nn":
        (M, K), N = a.shape, b.shape[1]
    elif mode == "nt":
        (M, K), N = a.shape, b.shape[0]
    else:
        (K, M), N = a.shape, b.shape[1]
    tm = _pick(M, MATMUL_TILES)
    tn = _pick(N, MATMUL_TILES)
    osz = jnp.dtype(out_dtype).itemsize

    def vmem_bytes(tk):
        ops = 2 * tk * (tm * a.dtype.itemsize + tn * b.dtype.itemsize)
        return ops + tm * tn * (2 * osz + (4 if tk < K else 0) + (8 if add is not None else 0))

    tk = next((t for t in (K, 4096, 2816, 2048, 1408, 1024, 512, 256, 128)
               if K % t == 0 and vmem_bytes(t) <= MATMUL_VMEM_BUDGET), K)
    nk = K // tk
    dims = {"nn": NN, "nt": NT, "tn": TN}[mode]

    def body(*refs):
        if add is None:
            a_ref, b_ref, o_ref = refs[:3]
            c_ref = None
        else:
            a_ref, b_ref, c_ref, o_ref = refs[:4]

        def prod():
            return _dot(a_ref[...].astype(BF16), b_ref[...].astype(BF16), dims)

        def finish(r):
            if c_ref is not None:
                r = r + c_ref[...].astype(F32)
            o_ref[...] = r.astype(o_ref.dtype)

        if nk == 1:
            finish(prod())
            return
        acc = refs[-1]
        k = pl.program_id(2)

        @pl.when(k == 0)
        def _():
            acc[...] = prod()

        if nk > 2:
            @pl.when((k > 0) & (k < nk - 1))
            def _():
                acc[...] += prod()

        @pl.when(k == nk - 1)
        def _():
            finish(acc[...] + prod())

    a_spec = pl.BlockSpec((tk, tm), lambda i, j, k: (k, i)) if mode == "tn" else pl.BlockSpec((tm, tk), lambda i, j, k: (i, k))
    b_spec = pl.BlockSpec((tn, tk), lambda i, j, k: (j, k)) if mode == "nt" else pl.BlockSpec((tk, tn), lambda i, j, k: (k, j))
    o_spec = pl.BlockSpec((tm, tn), lambda i, j, k: (i, j))
    ins, specs = [a, b], [a_spec, b_spec]
    if add is not None:
        ins.append(add)
        specs.append(o_spec)
    return pl.pallas_call(
        body,
        grid=(M // tm, N // tn, nk),
        in_specs=specs,
        out_specs=o_spec,
        out_shape=jax.ShapeDtypeStruct((M, N), out_dtype),
        scratch_shapes=[pltpu.VMEM((tm, tn), F32)] if nk > 1 else [],
        compiler_params=_cp(("parallel", "parallel", "arbitrary")),
        name=name,
    )(*ins)


def rowwise(fn, tiled, full, out_tiled, out_acc, tile, name):
    tiled = [t if isinstance(t, tuple) else (t, t.shape[1], 0) for t in tiled]
    rows = tiled[0][0].shape[0]
    tile = min(tile, rows)
    assert rows % tile == 0
    nt, nf, no = len(tiled), len(full), len(out_tiled)

    def body(*refs):
        ins = [r[...] for r in refs[:nt + nf]]
        res = fn(*ins)
        if not isinstance(res, (tuple, list)):
            res = (res,)
        outs = refs[nt + nf:]
        for r, v in zip(outs[:no], res[:no]):
            r[...] = v.astype(r.dtype)
        if out_acc:
            @pl.when(pl.program_id(0) == 0)
            def _():
                for r in outs[no:]:
                    r[...] = jnp.zeros_like(r)

            for r, v in zip(outs[no:], res[no:]):
                r[...] += v

    in_specs = [pl.BlockSpec((tile, w), lambda i, cb=cb: (i, cb)) for (_, w, cb) in tiled]
    in_specs += [pl.BlockSpec(f.shape, lambda i, nd=f.ndim: (0,) * nd) for f in full]
    out_specs = [pl.BlockSpec((tile, w), lambda i: (i, 0)) for (w, _) in out_tiled]
    out_specs += [pl.BlockSpec(s, lambda i, nd=len(s): (0,) * nd) for s in out_acc]
    out_shape = [jax.ShapeDtypeStruct((rows, w), dt) for (w, dt) in out_tiled]
    out_shape += [jax.ShapeDtypeStruct(s, F32) for s in out_acc]
    res = pl.pallas_call(
        body,
        grid=(rows // tile,),
        in_specs=in_specs,
        out_specs=out_specs,
        out_shape=out_shape,
        compiler_params=_cp(("arbitrary",) if out_acc else ("parallel",)),
        name=name,
    )(*[t[0] for t in tiled], *full)
    return res


def adam_reduce(parts, w, m, v, name):
    P, R, C = parts.shape
    tr = _pick(R, (256, 128, 64, 32, 16, 8))
    c1 = 1.0 - ADAM_B1 ** ADAM_STEP
    c2 = 1.0 - ADAM_B2 ** ADAM_STEP

    def body(p_ref, w_ref, m_ref, v_ref, g_ref, d_ref, nm_ref, nv_ref):
        g = p_ref[0].astype(F32)
        for j in range(1, P):
            g = g + p_ref[j].astype(F32)
        mm = ADAM_B1 * m_ref[...] + (1.0 - ADAM_B1) * g
        vv = ADAM_B2 * v_ref[...] + (1.0 - ADAM_B2) * (g * g)
        m_hat = mm / c1
        v_hat = vv / c2
        g_ref[...] = g
        d_ref[...] = -ADAM_LR * (m_hat / (jnp.sqrt(v_hat) + ADAM_EPS) + ADAM_WD * w_ref[...])
        nm_ref[...] = mm
        nv_ref[...] = vv

    spec = pl.BlockSpec((tr, C), lambda i: (i, 0))
    return pl.pallas_call(
        body,
        grid=(R // tr,),
        in_specs=[pl.BlockSpec((P, tr, C), lambda i: (0, i, 0)), spec, spec, spec],
        out_specs=[spec] * 4,
        out_shape=[jax.ShapeDtypeStruct((R, C), F32)] * 4,
        compiler_params=_cp(("parallel",)),
        name=name,
    )(parts, w, m, v)


def reduce_parts(parts, name):
    P, R, C = parts.shape
    tr = _pick(R, (64, 32, 16, 8))

    def body(p_ref, g_ref):
        g = p_ref[0]
        for j in range(1, P):
            g = g + p_ref[j]
        g_ref[...] = g

    return pl.pallas_call(
        body,
        grid=(R // tr,),
        in_specs=[pl.BlockSpec((P, tr, C), lambda i: (0, i, 0))],
        out_specs=pl.BlockSpec((tr, C), lambda i: (i, 0)),
        out_shape=jax.ShapeDtypeStruct((R, C), F32),
        compiler_params=_cp(("parallel",)),
        name=name,
    )(parts)


def pack(arrs):
    flat = jnp.concatenate([a.reshape(-1).astype(F32) for a in arrs])
    n = flat.shape[0]
    padded = -(-n // 1024) * 1024
    return jnp.pad(flat, (0, padded - n)).reshape(padded // 128, 128)


def blocks_from_segments(segs, ncols):
    offs = np.cumsum([0] + [s.shape[1] for s in segs]).tolist()
    blocks = []
    for j in range(NDEV):
        lo, hi = j * ncols, (j + 1) * ncols
        parts = [s[:, max(lo, o) - o:min(hi, o + s.shape[1]) - o]
                 for s, o in zip(segs, offs[:-1]) if max(lo, o) < min(hi, o + s.shape[1])]
        blocks.append(jnp.concatenate(parts, axis=1) if len(parts) > 1 else parts[0])
    return jnp.stack(blocks)


def rows_to_blocks(g):
    return g.reshape(NDEV, -1, g.shape[1])


def segments_from_blocks(g, widths):
    ncols = g.shape[2]
    offs = np.cumsum([0] + list(widths)).tolist()
    out = []
    for o, w in zip(offs[:-1], widths):
        parts = [g[j][:, max(j * ncols, o) - j * ncols:min((j + 1) * ncols, o + w) - j * ncols]
                 for j in range(NDEV) if max(j * ncols, o) < min((j + 1) * ncols, o + w)]
        out.append(jnp.concatenate(parts, axis=1) if len(parts) > 1 else parts[0])
    return out


def unpack(slab, shapes):
    flat = slab.reshape(-1)
    out, off = [], 0
    for s in shapes:
        size = int(np.prod(s))
        out.append(flat[off:off + size].reshape(s))
        off += size
    return out


def modulate_fwd(x, g, sh, sc, name):
    def fn(x, g, sh, sc):
        return x * _rstd(x) * g * (1.0 + sc) + sh

    return rowwise(fn, [x], [g, sh, sc], [(x.shape[1], BF16)], [], 256, name)[0]


def norm_rope_fwd(p, width, cb, w, cosf, sinf, hd, name):
    nh = width // hd

    def fn(x, cosf, sinf, w):
        outs = []
        for h in range(nh):
            xh = x[:, h * hd:(h + 1) * hd]
            y = xh * _rstd(xh) * w
            outs.append(y * cosf + pltpu.roll(y, hd // 2, 1) * sinf)
        return jnp.concatenate(outs, axis=1) if nh > 1 else outs[0]

    return rowwise(fn, [(p, width, cb), cosf, sinf], [w], [(width, BF16)], [], 256, name)[0]


def norm_rope_bwd(p, width, cb, d, w, cosf, sinf, hd, name):
    nh = width // hd

    def fn(x, d, cosf, sinf, w):
        outs = []
        dw = jnp.zeros((1, hd), F32)
        for h in range(nh):
            xh = x[:, h * hd:(h + 1) * hd]
            dh = d[:, h * hd:(h + 1) * hd].astype(F32)
            r = _rstd(xh)
            n = xh * r
            dy = dh * cosf + pltpu.roll(dh * sinf, hd // 2, 1)
            dw = dw + jnp.sum(dy * n, axis=0, keepdims=True)
            dn = dy * w
            outs.append(r * (dn - n * jnp.mean(dn * n, axis=-1, keepdims=True)))
        return (jnp.concatenate(outs, axis=1) if nh > 1 else outs[0]), dw

    return rowwise(fn, [(p, width, cb), d, cosf, sinf], [w], [(width, BF16)], [(1, hd)], 256, name)


def attention_fwd(qr, kr, pkv, kcr, pkv_c, sink, hkv, hd, name):
    T, L = qr.shape[0], kcr.shape[0]
    G = qr.shape[1] // (hkv * hd)
    nb = T // BLOCK
    scale = hd ** -0.5

    def body(q_ref, kp, kc, kn, vp, vc, vn, ck_ref, cv_ref, sink_ref, o_ref, lse_ref, lser_ref):
        i = pl.program_id(1)
        kwin = jnp.concatenate([kp[...], kc[...], kn[...]], axis=0)
        vwin = jnp.concatenate([vp[...], vc[...], vn[...]], axis=0).astype(BF16)
        ck, cv = ck_ref[...], cv_ref[...].astype(BF16)
        row = lax.broadcasted_iota(jnp.int32, (BLOCK, 3 * BLOCK), 0)
        col = lax.broadcasted_iota(jnp.int32, (BLOCK, 3 * BLOCK), 1)
        rel = col - BLOCK - row
        valid = (jnp.abs(rel) <= WINDOW) & ((col >= BLOCK) | (i > 0)) & ((col < 2 * BLOCK) | (i < nb - 1))
        R = range(G)
        qa = q_ref[...]
        qs = [qa[:, g * hd:(g + 1) * hd] for g in R]
        sks = [sink_ref[g] for g in R]
        ss = [jnp.where(valid, _dot(qs[g], kwin, NT) * scale, NEG) for g in R]
        scs = [_dot(qs[g], ck, NT) * scale for g in R]
        ms = [jnp.maximum(jnp.maximum(jnp.max(ss[g], axis=1, keepdims=True), jnp.max(scs[g], axis=1, keepdims=True)),
                          sks[g]) for g in R]
        ps = [jnp.exp(ss[g] - ms[g]) for g in R]
        pcs = [jnp.exp(scs[g] - ms[g]) for g in R]
        nums = [_dot(ps[g].astype(BF16), vwin, NN) + _dot(pcs[g].astype(BF16), cv, NN) for g in R]
        dens = [jnp.exp(sks[g] - ms[g]) + jnp.sum(ps[g], axis=1, keepdims=True) + jnp.sum(pcs[g], axis=1, keepdims=True)
                for g in R]
        o_ref[...] = jnp.concatenate([(nums[g] / dens[g]).astype(o_ref.dtype) for g in R], axis=1)
        eye = (lax.broadcasted_iota(jnp.int32, (BLOCK, BLOCK), 0)
               == lax.broadcasted_iota(jnp.int32, (BLOCK, BLOCK), 1)).astype(F32)
        for g in R:
            lg = ms[g] + jnp.log(dens[g])
            lse_ref[g] = lg
            lser_ref[g] = jnp.sum(lg * eye, axis=0, keepdims=True)

    kv_specs = [
        pl.BlockSpec((BLOCK, hd), lambda h, i: (jnp.maximum(i - 1, 0), h)),
        pl.BlockSpec((BLOCK, hd), lambda h, i: (i, h)),
        pl.BlockSpec((BLOCK, hd), lambda h, i: (jnp.minimum(i + 1, nb - 1), h)),
    ]
    v_specs = [
        pl.BlockSpec((BLOCK, hd), lambda h, i: (jnp.maximum(i - 1, 0), hkv + h)),
        pl.BlockSpec((BLOCK, hd), lambda h, i: (i, hkv + h)),
        pl.BlockSpec((BLOCK, hd), lambda h, i: (jnp.minimum(i + 1, nb - 1), hkv + h)),
    ]
    return pl.pallas_call(
        body,
        grid=(hkv, nb),
        in_specs=[pl.BlockSpec((BLOCK, G * hd), lambda h, i: (i, h))] + kv_specs + v_specs + [
            pl.BlockSpec((L, hd), lambda h, i: (0, h)),
            pl.BlockSpec((L, hd), lambda h, i: (0, hkv + h)),
            pl.BlockSpec((G, 1, 1), lambda h, i: (h, 0, 0)),
        ],
        out_specs=[
            pl.BlockSpec((BLOCK, G * hd), lambda h, i: (i, h)),
            pl.BlockSpec((G, BLOCK, 1), lambda h, i: (h, i, 0)),
            pl.BlockSpec((G, 1, BLOCK), lambda h, i: (h, 0, i)),
        ],
        out_shape=[jax.ShapeDtypeStruct(qr.shape, BF16), jax.ShapeDtypeStruct((hkv * G, T, 1), F32),
                   jax.ShapeDtypeStruct((hkv * G, 1, T), F32)],
        compiler_params=_cp(("parallel", "parallel")),
        name=name,
    )(qr, kr, kr, kr, pkv, pkv, pkv, kcr, pkv_c, sink)


def attention_bwd_q(qr, kr, pkv, kcr, pkv_c, sink, do, o, lse, hkv, hd, name):
    T, L = qr.shape[0], kcr.shape[0]
    G = qr.shape[1] // (hkv * hd)
    nb = T // BLOCK
    scale = hd ** -0.5

    def body(q_ref, kp, kc, kn, vp, vc, vn, ck_ref, cv_ref, sink_ref, do_ref, o_ref, lse_ref,
             dq_ref, dck_ref, dcv_ref, dsink_ref, drr_ref):
        i = pl.program_id(1)

        @pl.when(i == 0)
        def _():
            dck_ref[...] = jnp.zeros_like(dck_ref)
            dcv_ref[...] = jnp.zeros_like(dcv_ref)
            dsink_ref[...] = jnp.zeros_like(dsink_ref)

        kwin = jnp.concatenate([kp[...], kc[...], kn[...]], axis=0)
        vwin = jnp.concatenate([vp[...], vc[...], vn[...]], axis=0).astype(BF16)
        ck, cv = ck_ref[...], cv_ref[...].astype(BF16)
        row = lax.broadcasted_iota(jnp.int32, (BLOCK, 3 * BLOCK), 0)
        col = lax.broadcasted_iota(jnp.int32, (BLOCK, 3 * BLOCK), 1)
        rel = col - BLOCK - row
        valid = (jnp.abs(rel) <= WINDOW) & ((col >= BLOCK) | (i > 0)) & ((col < 2 * BLOCK) | (i < nb - 1))
        R = range(G)
        qa, doa, oa = q_ref[...], do_ref[...], o_ref[...]
        qs = [qa[:, g * hd:(g + 1) * hd] for g in R]
        dos = [doa[:, g * hd:(g + 1) * hd] for g in R]
        lgs = [lse_ref[g] for g in R]
        sks = [sink_ref[g] for g in R]
        ss = [jnp.where(valid, _dot(qs[g], kwin, NT) * scale, NEG) for g in R]
        scs = [_dot(qs[g], ck, NT) * scale for g in R]
        dps = [_dot(dos[g], vwin, NT) for g in R]
        dpcs = [_dot(dos[g], cv, NT) for g in R]
        drs = [jnp.sum(dos[g].astype(F32) * oa[:, g * hd:(g + 1) * hd].astype(F32), axis=1, keepdims=True) for g in R]
        ps = [jnp.exp(ss[g] - lgs[g]) for g in R]
        pcs = [jnp.exp(scs[g] - lgs[g]) for g in R]
        dss = [(ps[g] * (dps[g] - drs[g]) * scale).astype(BF16) for g in R]
        dscs = [(pcs[g] * (dpcs[g] - drs[g]) * scale).astype(BF16) for g in R]
        dqs = [_dot(dss[g], kwin, NN) + _dot(dscs[g], ck, NN) for g in R]
        dcks = [_dot(dscs[g], qs[g], TN) for g in R]
        dcvs = [_dot(pcs[g].astype(BF16), dos[g], TN) for g in R]
        dq_ref[...] = jnp.concatenate(dqs, axis=1)
        dck_ref[...] += (dcks[0] + dcks[1]) + (dcks[2] + dcks[3]) if G == 4 else sum(dcks[1:], dcks[0])
        dcv_ref[...] += (dcvs[0] + dcvs[1]) + (dcvs[2] + dcvs[3]) if G == 4 else sum(dcvs[1:], dcvs[0])
        eye = (lax.broadcasted_iota(jnp.int32, (BLOCK, BLOCK), 0)
               == lax.broadcasted_iota(jnp.int32, (BLOCK, BLOCK), 1)).astype(F32)
        for g in R:
            dsink_ref[g] += -jnp.sum(jnp.exp(sks[g] - lgs[g]) * drs[g], axis=0, keepdims=True)
            drr_ref[g] = jnp.sum(drs[g] * eye, axis=0, keepdims=True)

    kv_specs = [
        pl.BlockSpec((BLOCK, hd), lambda h, i: (jnp.maximum(i - 1, 0), h)),
        pl.BlockSpec((BLOCK, hd), lambda h, i: (i, h)),
        pl.BlockSpec((BLOCK, hd), lambda h, i: (jnp.minimum(i + 1, nb - 1), h)),
    ]
    v_specs = [
        pl.BlockSpec((BLOCK, hd), lambda h, i: (jnp.maximum(i - 1, 0), hkv + h)),
        pl.BlockSpec((BLOCK, hd), lambda h, i: (i, hkv + h)),
        pl.BlockSpec((BLOCK, hd), lambda h, i: (jnp.minimum(i + 1, nb - 1), hkv + h)),
    ]
    qspec = pl.BlockSpec((BLOCK, G * hd), lambda h, i: (i, h))
    return pl.pallas_call(
        body,
        grid=(hkv, nb),
        in_specs=[qspec] + kv_specs + v_specs + [
            pl.BlockSpec((L, hd), lambda h, i: (0, h)),
            pl.BlockSpec((L, hd), lambda h, i: (0, hkv + h)),
            pl.BlockSpec((G, 1, 1), lambda h, i: (h, 0, 0)),
            qspec, qspec,
            pl.BlockSpec((G, BLOCK, 1), lambda h, i: (h, i, 0)),
        ],
        out_specs=[
            qspec,
            pl.BlockSpec((L, hd), lambda h, i: (0, h)),
            pl.BlockSpec((L, hd), lambda h, i: (0, h)),
            pl.BlockSpec((G, 1, 1), lambda h, i: (h, 0, 0)),
            pl.BlockSpec((G, 1, BLOCK), lambda h, i: (h, 0, i)),
        ],
        out_shape=[
            jax.ShapeDtypeStruct(qr.shape, F32),
            jax.ShapeDtypeStruct((L, hkv * hd), F32),
            jax.ShapeDtypeStruct((L, hkv * hd), F32),
            jax.ShapeDtypeStruct((hkv * G, 1, 1), F32),
            jax.ShapeDtypeStruct((hkv * G, 1, T), F32),
        ],
        compiler_params=_cp(("parallel", "arbitrary")),
        name=name,
    )(qr, kr, kr, kr, pkv, pkv, pkv, kcr, pkv_c, sink, do, o, lse)


def attention_bwd_kv(qr, kr, pkv, do, lse_row, dr_row, hkv, hd, name):
    T = qr.shape[0]
    G = qr.shape[1] // (hkv * hd)
    nb = T // BLOCK
    scale = hd ** -0.5

    def body(k_ref, v_ref, *refs):
        qs, dos, lses, drs = refs[0:3], refs[3:6], refs[6:9], refs[9:12]
        dk_ref, dv_ref = refs[12:]
        j = pl.program_id(1)
        k = k_ref[...]
        v = v_ref[...].astype(BF16)
        row = lax.broadcasted_iota(jnp.int32, (BLOCK, BLOCK), 0)
        col = lax.broadcasted_iota(jnp.int32, (BLOCK, BLOCK), 1)
        bias = []
        for d in range(3):
            iq = j + d - 1
            rel = row - col - (d - 1) * BLOCK
            valid = (jnp.abs(rel) <= WINDOW) & (iq >= 0) & (iq < nb)
            bias += [jnp.where(valid, 0.0, NEG)] * G
        bias = jnp.concatenate(bias, axis=1)

        def stack(refs):
            vals = [r[...] for r in refs]
            return jnp.concatenate([a[:, g * hd:(g + 1) * hd] for a in vals for g in range(G)], axis=0)

        q, dob = stack(qs), stack(dos)
        lrow = jnp.concatenate([r[g] for r in lses for g in range(G)], axis=1)
        drow = jnp.concatenate([r[g] for r in drs for g in range(G)], axis=1)
        st = _dot(k, q, NT) * scale + bias
        pt = jnp.exp(st - lrow)
        dpt = _dot(v, dob, NT)
        dst = (pt * (dpt - drow) * scale).astype(BF16)
        dk_ref[...] = _dot(dst, q, NN).astype(dk_ref.dtype)
        dv_ref[...] = _dot(pt.astype(BF16), dob, NN).astype(dv_ref.dtype)

    def q3(width_block):
        return [
            pl.BlockSpec(width_block, lambda h, j: (jnp.maximum(j - 1, 0), h)),
            pl.BlockSpec(width_block, lambda h, j: (j, h)),
            pl.BlockSpec(width_block, lambda h, j: (jnp.minimum(j + 1, nb - 1), h)),
        ]

    row3 = [
        pl.BlockSpec((G, 1, BLOCK), lambda h, j: (h, 0, jnp.maximum(j - 1, 0))),
        pl.BlockSpec((G, 1, BLOCK), lambda h, j: (h, 0, j)),
        pl.BlockSpec((G, 1, BLOCK), lambda h, j: (h, 0, jnp.minimum(j + 1, nb - 1))),
    ]
    qb = (BLOCK, G * hd)
    return pl.pallas_call(
        body,
        grid=(hkv, nb),
        in_specs=[pl.BlockSpec((BLOCK, hd), lambda h, j: (j, h)), pl.BlockSpec((BLOCK, hd), lambda h, j: (j, hkv + h))]
        + q3(qb) + q3(qb) + row3 + row3,
        out_specs=[pl.BlockSpec((BLOCK, hd), lambda h, j: (j, h))] * 2,
        out_shape=[jax.ShapeDtypeStruct((T, hkv * hd), BF16)] * 2,
        compiler_params=_cp(("parallel", "parallel")),
        name=name,
    )(kr, pkv, qr, qr, qr, do, do, do, lse_row, lse_row, lse_row, dr_row, dr_row, dr_row)


def gate_fwd(plr, wf, wb, bf, bb, name):
    n = wf.shape[1]

    def fn(lr, wf, wb, bf, bb):
        lrb = lr.astype(BF16)
        outs = []
        for w, b in ((wf, bf), (wb, bb)):
            z = _dot(lrb, w.astype(BF16), NN) + b
            outs.append((jnp.minimum(z, 0.0) - jnp.log(1.0 + jnp.exp(-jnp.abs(z)))) / GLA_GATE_NORM)
        return outs

    return rowwise(fn, [plr], [wf, wb, bf, bb], [(n, F32), (n, F32)], [], 256, name)


def gate_bwd(plr, dgf, dgb, wf, wb, bf, bb, name):
    n = wf.shape[1]

    def fn(lr, dgf, dgb, wf, wb, bf, bb):
        lrb = lr.astype(BF16)
        dlr = jnp.zeros(lr.shape, F32)
        res = []
        for w, b, dg in ((wf, bf, dgf), (wb, bb, dgb)):
            wb16 = w.astype(BF16)
            z = _dot(lrb, wb16, NN) + b
            dz = dg * _sig(-z) / GLA_GATE_NORM
            dzb = dz.astype(BF16)
            dlr = dlr + _dot(dzb, wb16, NT)
            res += [_dot(lrb, dzb, TN), jnp.sum(dz, axis=0, keepdims=True)]
        return [dlr] + res

    return rowwise(fn, [plr, dgf, dgb], [wf, wb, bf, bb], [(128, BF16)],
                   [(128, n), (1, n), (128, n), (1, n)], 256, name)


def _tri_dot(tri_b, x):
    x1 = x.astype(BF16)
    r1 = x - x1.astype(F32)
    x2 = r1.astype(BF16)
    x3 = (r1 - x2.astype(F32)).astype(BF16)
    return _dot(tri_b, x1, NN) + _dot(tri_b, x2, NN) + _dot(tri_b, x3, NN)


def gla_fwd(pqk, pv, gl, s0, heads, reverse, name, o_add=None):
    T = pqk.shape[0]
    dk = pqk.shape[1] // (2 * heads)
    dv = pv.shape[1] // heads
    C = GLA_CHUNK
    nc = T // C
    qscale = dk ** -0.5

    def body(*refs):
        if o_add is None:
            q_ref, k_ref, v_ref, g_ref, s0_ref, o_ref, st_ref, sf_ref, S = refs
            oa_ref = None
        else:
            q_ref, k_ref, v_ref, g_ref, s0_ref, oa_ref, o_ref, st_ref, sf_ref, S = refs
        n = pl.program_id(0)

        @pl.when(n == 0)
        def _():
            S[...] = s0_ref[...]

        r = lax.broadcasted_iota(jnp.int32, (C, C), 0)
        c = lax.broadcasted_iota(jnp.int32, (C, C), 1)
        tri = (r <= c) if reverse else (r >= c)
        trib = tri.astype(BF16)
        ga, qa, ka, va = g_ref[...], q_ref[...], k_ref[...], v_ref[...]
        sts = [S[h] for h in range(heads)]
        H = range(heads)
        gs = [ga[:, h * dk:(h + 1) * dk] for h in H]
        bs = [_tri_dot(trib, g) for g in gs]
        bls = [jnp.sum(g, axis=0, keepdims=True) for g in gs]
        mid = lax.broadcasted_iota(jnp.int32, (C, 1), 0) == C // 2
        bms = [jnp.sum(jnp.where(mid, b, 0.0), axis=0, keepdims=True) for b in bs]
        vs = [va[:, h * dv:(h + 1) * dv].astype(BF16) for h in H]
        qs = [qa[:, h * dk:(h + 1) * dk].astype(F32) * qscale for h in H]
        qes = [(qs[h] * jnp.exp(bs[h])).astype(BF16) for h in H]
        qms = [(qs[h] * jnp.exp(bs[h] - bms[h])).astype(BF16) for h in H]
        kms = [(ka[:, h * dk:(h + 1) * dk].astype(F32) * jnp.exp(bms[h] - bs[h])).astype(BF16) for h in H]
        kls = [(ka[:, h * dk:(h + 1) * dk].astype(F32) * jnp.exp(bls[h] - bs[h])).astype(BF16) for h in H]
        inter = [_dot(qes[h], sts[h].astype(BF16), NT) for h in H]
        upd = [_dot(vs[h], kls[h], TN) for h in H]
        As = [jnp.where(tri, _dot(qms[h], kms[h], NT), 0.0).astype(BF16) for h in H]
        outs = [inter[h] + _dot(As[h], vs[h], NN) for h in H]
        news = [sts[h] * jnp.exp(bls[h]) + upd[h] for h in H]
        o = jnp.concatenate(outs, axis=1)
        if oa_ref is not None:
            o = o + oa_ref[...]
        o_ref[...] = o
        for h in range(heads):
            st_ref[0, h] = sts[h]
            S[h] = news[h]

        @pl.when(n == nc - 1)
        def _():
            for h in range(heads):
                sf_ref[h] = news[h]

    def ci(n):
        return (nc - 1 - n) if reverse else n

    specs = [
        pl.BlockSpec((C, heads * dk), lambda n: (ci(n), 0)),
        pl.BlockSpec((C, heads * dk), lambda n: (ci(n), 1)),
        pl.BlockSpec((C, heads * dv), lambda n: (ci(n), 0)),
        pl.BlockSpec((C, heads * dk), lambda n: (ci(n), 0)),
        pl.BlockSpec((heads, dv, dk), lambda n: (0, 0, 0)),
    ]
    ins = [pqk, pqk, pv, gl, s0]
    if o_add is not None:
        specs.append(pl.BlockSpec((C, heads * dv), lambda n: (ci(n), 0)))
        ins.append(o_add)
    return pl.pallas_call(
        body,
        grid=(nc,),
        in_specs=specs,
        out_specs=[
            pl.BlockSpec((C, heads * dv), lambda n: (ci(n), 0)),
            pl.BlockSpec((1, heads, dv, dk), lambda n: (ci(n), 0, 0, 0)),
            pl.BlockSpec((heads, dv, dk), lambda n: (0, 0, 0)),
        ],
        out_shape=[
            jax.ShapeDtypeStruct((T, heads * dv), F32),
            jax.ShapeDtypeStruct((nc, heads, dv, dk), F32),
            jax.ShapeDtypeStruct((heads, dv, dk), F32),
        ],
        scratch_shapes=[pltpu.VMEM((heads, dv, dk), F32)],
        compiler_params=_cp(("arbitrary",)),
        name=name,
    )(*ins)


def gla_bwd(pqk, pv, gl, states, do, dsf, heads, reverse, name, acc=None):
    T = pqk.shape[0]
    dk = pqk.shape[1] // (2 * heads)
    dv = pv.shape[1] // heads
    C = GLA_CHUNK
    nc = T // C
    qscale = dk ** -0.5

    def body(*refs):
        if acc is None:
            q_ref, k_ref, v_ref, g_ref, st_ref, do_ref, dsf_ref, dq_ref, dk_ref, dv_ref, dg_ref, ds0_ref, dS = refs
            aq = ak = av = None
        else:
            (q_ref, k_ref, v_ref, g_ref, st_ref, do_ref, dsf_ref, aq, ak, av,
             dq_ref, dk_ref, dv_ref, dg_ref, ds0_ref, dS) = refs
        n = pl.program_id(0)

        @pl.when(n == 0)
        def _():
            dS[...] = dsf_ref[...]

        r = lax.broadcasted_iota(jnp.int32, (C, C), 0)
        c = lax.broadcasted_iota(jnp.int32, (C, C), 1)
        tri = (r <= c) if reverse else (r >= c)
        tri_t = (r >= c) if reverse else (r <= c)
        trib, tritb = tri.astype(BF16), tri_t.astype(BF16)
        ga, qa, ka, va, doa = g_ref[...], q_ref[...], k_ref[...], v_ref[...], do_ref[...]
        sts = [st_ref[0, h] for h in range(heads)]
        dsts = [dS[h] for h in range(heads)]
        H = range(heads)
        gs = [ga[:, h * dk:(h + 1) * dk] for h in H]
        bs = [_tri_dot(trib, g) for g in gs]
        bls = [jnp.sum(g, axis=0, keepdims=True) for g in gs]
        mid = lax.broadcasted_iota(jnp.int32, (C, 1), 0) == C // 2
        bms = [jnp.sum(jnp.where(mid, b, 0.0), axis=0, keepdims=True) for b in bs]
        ebs = [jnp.exp(b) for b in bs]
        embs = [jnp.exp(bs[h] - bms[h]) for h in H]
        enbs = [jnp.exp(bms[h] - bs[h]) for h in H]
        elbs = [jnp.exp(bls[h] - bs[h]) for h in H]
        ebls = [jnp.exp(bl) for bl in bls]
        vbs = [va[:, h * dv:(h + 1) * dv].astype(BF16) for h in H]
        dobs = [doa[:, h * dv:(h + 1) * dv].astype(BF16) for h in H]
        qs = [qa[:, h * dk:(h + 1) * dk].astype(F32) * qscale for h in H]
        qes = [qs[h] * ebs[h] for h in H]
        qms = [qs[h] * embs[h] for h in H]
        kms = [ka[:, h * dk:(h + 1) * dk].astype(F32) * enbs[h] for h in H]
        kls = [ka[:, h * dk:(h + 1) * dk].astype(F32) * elbs[h] for h in H]
        qebs = [a.astype(BF16) for a in qes]
        qmbs = [a.astype(BF16) for a in qms]
        kmbs = [a.astype(BF16) for a in kms]
        klbs = [a.astype(BF16) for a in kls]
        stbs = [a.astype(BF16) for a in sts]
        dstbs = [a.astype(BF16) for a in dsts]
        ps = [jnp.where(tri, _dot(qmbs[h], kmbs[h], NT), 0.0).astype(BF16) for h in H]
        dps = [jnp.where(tri, _dot(dobs[h], vbs[h], NT), 0.0).astype(BF16) for h in H]
        dqes = [_dot(dobs[h], stbs[h], NN) for h in H]
        dkls = [_dot(vbs[h], dstbs[h], NN) for h in H]
        dv1 = [_dot(klbs[h], dstbs[h], NT) for h in H]
        dsn1 = [_dot(dobs[h], qebs[h], TN) for h in H]
        dqms = [_dot(dps[h], kmbs[h], NN) for h in H]
        dkms = [_dot(dps[h], qmbs[h], TN) for h in H]
        dvs = [_dot(ps[h], dobs[h], TN) + dv1[h] for h in H]
        dbls = [ebls[h] * jnp.sum(dsts[h] * sts[h], axis=0, keepdims=True)
                + jnp.sum(dkls[h] * kls[h], axis=0, keepdims=True) for h in H]
        dsns = [dsn1[h] + dsts[h] * ebls[h] for h in H]
        dqs = [(dqes[h] * ebs[h] + dqms[h] * embs[h]) * qscale for h in H]
        dks = [dkms[h] * enbs[h] + dkls[h] * elbs[h] for h in H]
        dbs = [dqes[h] * qes[h] + dqms[h] * qms[h] - dkms[h] * kms[h] - dkls[h] * kls[h] for h in H]
        dgs = [_tri_dot(tritb, dbs[h]) + dbls[h] for h in H]
        dq, dkk, dvv = (jnp.concatenate(a, axis=1) for a in (dqs, dks, dvs))
        if aq is not None:
            dq = dq + aq[...].astype(F32)
            dkk = dkk + ak[...].astype(F32)
            dvv = dvv + av[...].astype(F32)
        dq_ref[...] = dq.astype(dq_ref.dtype)
        dk_ref[...] = dkk.astype(dk_ref.dtype)
        dv_ref[...] = dvv.astype(dv_ref.dtype)
        dg_ref[...] = jnp.concatenate(dgs, axis=1)
        for h in range(heads):
            dS[h] = dsns[h]

        @pl.when(n == nc - 1)
        def _():
            for h in range(heads):
                ds0_ref[h] = dsns[h]

    def ci(n):
        return n if reverse else (nc - 1 - n)

    kspec = pl.BlockSpec((C, heads * dk), lambda n: (ci(n), 0))
    vspec = pl.BlockSpec((C, heads * dv), lambda n: (ci(n), 0))
    sspec = pl.BlockSpec((heads, dv, dk), lambda n: (0, 0, 0))
    specs = [
        kspec,
        pl.BlockSpec((C, heads * dk), lambda n: (ci(n), 1)),
        vspec,
        kspec,
        pl.BlockSpec((1, heads, dv, dk), lambda n: (ci(n), 0, 0, 0)),
        vspec,
        sspec,
    ]
    ins = [pqk, pqk, pv, gl, states, do, dsf]
    odt = F32 if acc is None else BF16
    if acc is not None:
        specs += [kspec, kspec, vspec]
        ins += list(acc)
    return pl.pallas_call(
        body,
        grid=(nc,),
        in_specs=specs,
        out_specs=[kspec, kspec, vspec, kspec, sspec],
        out_shape=[
            jax.ShapeDtypeStruct((T, heads * dk), odt),
            jax.ShapeDtypeStruct((T, heads * dk), odt),
            jax.ShapeDtypeStruct((T, heads * dv), odt),
            jax.ShapeDtypeStruct((T, heads * dk), F32),
            jax.ShapeDtypeStruct((heads, dv, dk), F32),
        ],
        scratch_shapes=[pltpu.VMEM((heads, dv, dk), F32)],
        compiler_params=_cp(("arbitrary",)),
        name=name,
    )(*ins)


def gla_out_fwd(og, prb, gn, heads, name):
    dv = og.shape[1] // heads

    def fn(og, rb, gn):
        outs = []
        for h in range(heads):
            oh = og[:, h * dv:(h + 1) * dv]
            outs.append(oh * _rstd(oh) * gn)
        y = jnp.concatenate(outs, axis=1)
        return y * (rb * _sig(rb))

    return rowwise(fn, [og, prb], [gn], [(og.shape[1], BF16)], [], 256, name)[0]


def gla_out_bwd(og, prb, du, gn, heads, name):
    dv = og.shape[1] // heads

    def fn(og, rb, du, gn):
        sg = _sig(rb)
        silu = rb * sg
        dsilu = sg * (1.0 + rb * (1.0 - sg))
        dog, ys = [], []
        dgn = jnp.zeros((1, dv), F32)
        for h in range(heads):
            sl = slice(h * dv, (h + 1) * dv)
            oh = og[:, sl]
            r = _rstd(oh)
            n = oh * r
            ys.append(n * gn)
            dy = du[:, sl] * silu[:, sl]
            dgn = dgn + jnp.sum(dy * n, axis=0, keepdims=True)
            dn = dy * gn
            dog.append(r * (dn - n * jnp.mean(dn * n, axis=-1, keepdims=True)))
        y = jnp.concatenate(ys, axis=1)
        return jnp.concatenate(dog, axis=1), du * y * dsilu, dgn

    return rowwise(fn, [og, prb, du], [gn], [(og.shape[1], F32), (og.shape[1], BF16)], [(1, dv)], 256, name)


def conv_swiglu_fwd(ua_, ug_, cw, cb, name):
    T, F = ua_.shape
    tt = min(512, T)
    tc = _pick(F, (512, 256, 128))
    nt_, ncol = T // tt, F // tc
    H = CONV_HALO
    n = tt + 2 * H

    def body(ua, uap, uan, ug, ugp, ugn, wa, wg, ba, bg, f_ref):
        i = pl.program_id(1)
        keep_p = (i > 0).astype(F32)
        keep_n = (i < nt_ - 1).astype(F32)
        res = []
        for m, p, nx, w, b in ((ua, uap, uan, wa, ba), (ug, ugp, ugn, wg, bg)):
            x = jnp.concatenate([p[...] * keep_p, m[...], nx[...] * keep_n], axis=0)
            down, up = pltpu.roll(x, 1, 0)[H:H + tt], pltpu.roll(x, n - 1, 0)[H:H + tt]
            res.append(w[0] * down + w[1] * x[H:H + tt] + w[2] * up + b[...])
        a, g = res
        f_ref[...] = (a * _sig_tanh(a) * g).astype(f_ref.dtype)

    wspec = lambda off: pl.BlockSpec((3, 1, tc), lambda j, i: (0, 0, j + off))
    bspec = lambda off: pl.BlockSpec((1, tc), lambda j, i: (0, j + off))
    return pl.pallas_call(
        body,
        grid=(ncol, nt_),
        in_specs=conv_halo_specs(T, tt, tc, 0) + conv_halo_specs(T, tt, tc, 0)
        + [wspec(0), wspec(ncol), bspec(0), bspec(ncol)],
        out_specs=pl.BlockSpec((tt, tc), lambda j, i: (i, j)),
        out_shape=jax.ShapeDtypeStruct((T, F), BF16),
        compiler_params=_cp(("parallel", "parallel")),
        name=name,
    )(ua_, ua_, ua_, ug_, ug_, ug_, cw, cw, cb, cb)


CONV_HALO = 16


def conv_halo_specs(T, tt, tc, off):
    r = tt // CONV_HALO
    last = T // CONV_HALO - 1
    return [
        pl.BlockSpec((tt, tc), lambda j, i: (i, j + off)),
        pl.BlockSpec((CONV_HALO, tc), lambda j, i: (jnp.maximum(i * r - 1, 0), j + off)),
        pl.BlockSpec((CONV_HALO, tc), lambda j, i: (jnp.minimum((i + 1) * r, last), j + off)),
    ]


def conv_swiglu_bwd_fused(ua_, ug_, cw, cb, df, name):
    T, F = ua_.shape
    tt = min(512, T)
    tc = _pick(F, (512, 256, 128))
    nt_, ncol = T // tt, F // tc
    H = CONV_HALO
    n = tt + 2 * H

    def body(ua, uap, uan, ug, ugp, ugn, dm, dp_, dn, wa, wg, ba, bg, dua_ref, dug_ref, dwa, dwg, dba, dbg):
        i = pl.program_id(1)

        @pl.when(i == 0)
        def _():
            for r in (dwa, dwg, dba, dbg):
                r[...] = jnp.zeros_like(r)

        keep_p = (i > 0).astype(F32)
        keep_n = (i < nt_ - 1).astype(F32)

        def ext(m, p, nx):
            return jnp.concatenate([p[...].astype(F32) * keep_p, m[...].astype(F32), nx[...].astype(F32) * keep_n],
                                   axis=0)

        d = ext(dm, dp_, dn)
        conv, parts = [], []
        for m, p, nx, w, b in ((ua, uap, uan, wa, ba), (ug, ugp, ugn, wg, bg)):
            x = ext(m, p, nx)
            down, up = pltpu.roll(x, 1, 0), pltpu.roll(x, n - 1, 0)
            parts.append((down, x, up))
            conv.append(w[0] * down + w[1] * x + w[2] * up + b[...])
        a, g = conv
        sg = _sig_tanh(a)
        da = d * g * sg * (1.0 + a * (1.0 - sg))
        dg = d * a * sg
        for dd, w, (down, x, up), o_ref, dw, db in ((da, wa, parts[0], dua_ref, dwa, dba),
                                                    (dg, wg, parts[1], dug_ref, dwg, dbg)):
            du = w[0] * pltpu.roll(dd, n - 1, 0) + w[1] * dd + w[2] * pltpu.roll(dd, 1, 0)
            o_ref[...] = du[H:H + tt].astype(o_ref.dtype)
            ddm = dd[H:H + tt]
            dw[0] += jnp.sum(ddm * down[H:H + tt], axis=0, keepdims=True)
            dw[1] += jnp.sum(ddm * x[H:H + tt], axis=0, keepdims=True)
            dw[2] += jnp.sum(ddm * up[H:H + tt], axis=0, keepdims=True)
            db[...] += jnp.sum(ddm, axis=0, keepdims=True)

    wspec = lambda off: pl.BlockSpec((3, 1, tc), lambda j, i: (0, 0, j + off))
    bspec = lambda off: pl.BlockSpec((1, tc), lambda j, i: (0, j + off))
    tile = pl.BlockSpec((tt, tc), lambda j, i: (i, j))
    return pl.pallas_call(
        body,
        grid=(ncol, nt_),
        in_specs=conv_halo_specs(T, tt, tc, 0) + conv_halo_specs(T, tt, tc, 0) + conv_halo_specs(T, tt, tc, 0)
        + [wspec(0), wspec(ncol), bspec(0), bspec(ncol)],
        out_specs=[tile, tile, wspec(0), wspec(0), bspec(0), bspec(0)],
        out_shape=[
            jax.ShapeDtypeStruct((T, F), BF16), jax.ShapeDtypeStruct((T, F), BF16),
            jax.ShapeDtypeStruct((3, 1, F), F32), jax.ShapeDtypeStruct((3, 1, F), F32),
            jax.ShapeDtypeStruct((1, F), F32), jax.ShapeDtypeStruct((1, F), F32),
        ],
        compiler_params=_cp(("parallel", "arbitrary")),
        name=name,
    )(ua_, ua_, ua_, ug_, ug_, ug_, df, df, df, cw, cw, cb, cb)


def rope_tables(n, hd):
    rows = n // GRID_W
    row = jnp.repeat(jnp.arange(rows), GRID_W)
    col = jnp.tile(jnp.arange(GRID_W), rows)
    n_freq = hd // 4
    inv = ROPE_THETA ** (-jnp.arange(n_freq, dtype=F32) / n_freq)
    ang = jnp.concatenate([row[:, None] * inv, col[:, None] * inv], axis=-1)
    cos, sin = jnp.cos(ang), jnp.sin(ang)
    return jnp.concatenate([cos, cos], axis=-1), jnp.concatenate([-sin, sin], axis=-1)


def local_step(x, ctx, tgt, mod, modc, W, P):
    T, D = x.shape
    L = ctx.shape[0]
    hd, hq, hkv, gh = P["hd"], P["hq"], P["hkv"], P["gh"]
    sh1, sc1, g1, sh2, sc2, g2 = mod
    csh1, csc1 = modc
    kvw = hkv * hd
    gkw = W["gqk"].shape[1] // 2
    gdv = D // gh
    gdk = gkw // gh

    h = modulate_fwd(x, P["g_mix"], sh1, sc1, "mod1")
    hc = modulate_fwd(ctx, P["g_mix"], csh1, csc1, "mod1_ctx")
    pq = matmul(h, W["q"], "nn", F32, "proj_q")
    pkv = matmul(h, W["kv"], "nn", F32, "proj_kv")
    pgqk = matmul(h, W["gqk"], "nn", F32, "proj_gqk")
    pgv = matmul(h, W["gv"], "nn", F32, "proj_gv")
    prb = matmul(h, W["rb"], "nn", F32, "proj_rb")
    plr = matmul(h, W["lr"], "nn", F32, "proj_lr")
    pgab = matmul(h, W["gab"], "nn", F32, "proj_gab")
    pkv_c = matmul(hc, W["kv"], "nn", F32, "proj_kv_ctx")
    pgqk_c = matmul(hc, W["gqk"], "nn", F32, "proj_gqk_ctx")
    pgv_c = matmul(hc, W["gv"], "nn", F32, "proj_gv_ctx")
    plr_c = matmul(hc, W["lr"], "nn", F32, "proj_lr_ctx")

    cosf, sinf = rope_tables(T, hd)
    one_c, zero_c = jnp.ones((L, hd), F32), jnp.zeros((L, hd), F32)
    qr = norm_rope_fwd(pq, hq * hd, 0, P["q_norm"], cosf, sinf, hd, "qnorm")
    kr = norm_rope_fwd(pkv, kvw, 0, P["k_norm"], cosf, sinf, hd, "knorm")
    kcr = norm_rope_fwd(pkv_c, kvw, 0, P["k_norm"], one_c, zero_c, hd, "knorm_ctx")
    sink = P["attn_sink"].reshape(hq, 1, 1)
    o_attn, lse, lse_row = attention_fwd(qr, kr, pkv, kcr, pkv_c, sink, hkv, hd, "attn_fwd")

    gf, gb = gate_fwd(plr, W["gate_f"], W["gate_b"], P["b_gate_f"], P["b_gate_b"], "gates")
    gfc, gbc = gate_fwd(plr_c, W["gate_f"], W["gate_b"], P["b_gate_f"], P["b_gate_b"], "gates_ctx")
    zero_state = jnp.zeros((gh, gdv, gdk), F32)
    _, st_cf, s_cf = gla_fwd(pgqk_c, pgv_c, gfc, zero_state, gh, False, "gla_ctx_f")
    _, st_cb, s_cb = gla_fwd(pgqk_c, pgv_c, gbc, zero_state, gh, True, "gla_ctx_b")
    of, st_f, _ = gla_fwd(pgqk, pgv, gf, s_cf, gh, False, "gla_f")
    og, st_b, _ = gla_fwd(pgqk, pgv, gb, s_cb, gh, True, "gla_b", o_add=of)
    ug = gla_out_fwd(og, prb, P["gla_norm"], gh, "gla_out")

    ya = matmul(o_attn, W["attn_o"], "nn", F32, "attn_o")
    yg = matmul(ug, W["gla_o"], "nn", F32, "gla_o")

    def merge_fn(ya, yg, ga, gb_):
        return _sig(ga) * ya + _sig(gb_) * yg

    z = rowwise(merge_fn, [ya, yg, (pgab, D, 0), (pgab, D, 1)], [], [(D, BF16)], [], 256, "merge")[0]
    mo = matmul(z, W["out"], "nn", F32, "w_out")

    def res_fn(x, mo, g1, gffn, sh2, sc2):
        x1 = x + g1 * mo
        return x1, x1 * _rstd(x1) * gffn * (1.0 + sc2) + sh2

    x1, h2 = rowwise(res_fn, [x, mo], [g1, P["g_ffn"], sh2, sc2], [(D, F32), (D, BF16)], [], 256, "res_mod2")
    u_a = matmul(h2, W["up_a"], "nn", F32, "w_up_a")
    u_g = matmul(h2, W["up_g"], "nn", F32, "w_up_g")
    cw3 = W["conv_w"].reshape(3, 1, -1)
    f = conv_swiglu_fwd(u_a, u_g, cw3, P["conv_b"], "conv_swiglu")
    fo = matmul(f, W["down"], "nn", F32, "w_down")

    def final_fn(x1, fo, tgt, g2):
        e = x1 + g2 * fo - tgt
        dy = e * (1.0 / D)
        lsum = jnp.sum(jnp.sum(e * e, axis=1, keepdims=True), axis=0, keepdims=True)
        return dy, dy * g2, jnp.broadcast_to(lsum, (1, 128)), jnp.sum(dy * fo, axis=0, keepdims=True)

    dy, dfo, lsum, dg2 = rowwise(final_fn, [x1, fo, tgt], [g2], [(D, F32), (D, BF16)], [(1, 128), (1, D)], 256, "loss")
    df = matmul(dfo, W["down"], "nt", BF16, "d_f")
    dw_down = matmul(f, dfo, "tn", BF16, "dw_down")
    du_a, du_g, dcw_a, dcw_g, dcb_a, dcb_g = conv_swiglu_bwd_fused(u_a, u_g, cw3, P["conv_b"], df, "conv_swiglu_bwd")
    dh2 = matmul(du_a, W["up_a"], "nt", F32, "d_h2_a")
    dh2 = matmul(du_g, W["up_g"], "nt", F32, "d_h2_g", add=dh2)
    dw_up = [matmul(h2, du_a, "tn", BF16, "dw_up_a"), matmul(h2, du_g, "tn", BF16, "dw_up_g")]

    def mod2_bwd_fn(x1, dh, dy, mo, gffn, sc2, g1):
        r = _rstd(x1)
        n = x1 * r
        dyy = dh * (1.0 + sc2)
        dn = dyy * gffn
        dx1 = dy + r * (dn - n * jnp.mean(dn * n, axis=-1, keepdims=True))
        s0 = lambda a: jnp.sum(a, axis=0, keepdims=True)
        return dx1, dx1 * g1, s0(dyy * n), s0(dh), s0(dh * n * gffn), s0(dx1 * mo)

    dx1, dmo, dg_ffn, dsh2, dsc2, dg1 = rowwise(
        mod2_bwd_fn, [x1, dh2, dy, mo], [P["g_ffn"], sc2, g1], [(D, F32), (D, BF16)], [(1, D)] * 4, 256, "mod2_bwd")
    dz = matmul(dmo, W["out"], "nt", F32, "d_z")
    dw_out = matmul(z, dmo, "tn", BF16, "dw_out")

    def merge_bwd_fn(dz, ya, yg, ga, gb_):
        sa, sb = _sig(ga), _sig(gb_)
        return dz * sa, dz * sb, jnp.concatenate([dz * ya * sa * (1.0 - sa), dz * yg * sb * (1.0 - sb)], axis=1)

    dya, dyg, dpgab = rowwise(merge_bwd_fn, [dz, ya, yg, (pgab, D, 0), (pgab, D, 1)], [],
                              [(D, BF16), (D, BF16), (2 * D, BF16)], [], 256, "merge_bwd")
    do_attn = matmul(dya, W["attn_o"], "nt", BF16, "d_oattn")
    dw_attn_o = matmul(o_attn, dya, "tn", BF16, "dw_attn_o")
    dug = matmul(dyg, W["gla_o"], "nt", F32, "d_ug")
    dw_gla_o = matmul(ug, dyg, "tn", BF16, "dw_gla_o")
    dog, dprb, dgn = gla_out_bwd(og, prb, dug, P["gla_norm"], gh, "gla_out_bwd")

    dq1, dk1, dv1, dgf, ds_cf = gla_bwd(pgqk, pgv, gf, st_f, dog, zero_state, gh, False, "gla_f_bwd")
    dgq, dgk, dpgv, dgb, ds_cb = gla_bwd(pgqk, pgv, gb, st_b, dog, zero_state, gh, True, "gla_b_bwd",
                                          acc=(dq1, dk1, dv1))
    dpgqk = jnp.concatenate([dgq, dgk], axis=1)
    zero_do = jnp.zeros((L, gh * gdv), F32)
    cq1, ck1, cv1, dgfc, _ = gla_bwd(pgqk_c, pgv_c, gfc, st_cf, zero_do, ds_cf, gh, False, "gla_ctx_f_bwd")
    cq, ck, dpgv_c, dgbc, _ = gla_bwd(pgqk_c, pgv_c, gbc, st_cb, zero_do, ds_cb, gh, True, "gla_ctx_b_bwd",
                                      acc=(cq1, ck1, cv1))
    dpgqk_c = jnp.concatenate([cq, ck], axis=1)
    dplr, dwgf, dbgf, dwgb, dbgb = gate_bwd(plr, dgf, dgb, W["gate_f"], W["gate_b"], P["b_gate_f"], P["b_gate_b"], "gates_bwd")
    dplr_c, dwgf_c, dbgf_c, dwgb_c, dbgb_c = gate_bwd(plr_c, dgfc, dgbc, W["gate_f"], W["gate_b"], P["b_gate_f"],
                                                      P["b_gate_b"], "gates_ctx_bwd")

    dqr, dkc_r, dvc, dsink, dr_row = attention_bwd_q(qr, kr, pkv, kcr, pkv_c, sink, do_attn, o_attn, lse, hkv, hd,
                                                     "attn_bwd_q")
    dkr, dv = attention_bwd_kv(qr, kr, pkv, do_attn, lse_row, dr_row, hkv, hd, "attn_bwd_kv")
    dpq, dqn = norm_rope_bwd(pq, hq * hd, 0, dqr, P["q_norm"], cosf, sinf, hd, "qnorm_bwd")
    dpk, dkn = norm_rope_bwd(pkv, kvw, 0, dkr, P["k_norm"], cosf, sinf, hd, "knorm_bwd")
    dpk_c, dkn_c = norm_rope_bwd(pkv_c, kvw, 0, dkc_r, P["k_norm"], one_c, zero_c, hd, "knorm_ctx_bwd")
    dpkv = jnp.concatenate([dpk, dv], axis=1)
    dpkv_c = jnp.concatenate([dpk_c, dvc.astype(BF16)], axis=1)

    dw_q = matmul(h, dpq, "tn", BF16, "dw_q")
    dw_kv = matmul(h, dpkv, "tn", BF16, "dw_kv", add=matmul(hc, dpkv_c, "tn", F32, "dw_kv_ctx"))
    dw_gqk = matmul(h, dpgqk, "tn", BF16, "dw_gqk", add=matmul(hc, dpgqk_c, "tn", F32, "dw_gqk_ctx"))
    dw_gv = matmul(h, dpgv, "tn", BF16, "dw_gv", add=matmul(hc, dpgv_c, "tn", F32, "dw_gv_ctx"))
    dw_rb = matmul(h, dprb, "tn", BF16, "dw_rb")
    dw_lr = matmul(h, dplr, "tn", BF16, "dw_lr", add=matmul(hc, dplr_c, "tn", F32, "dw_lr_ctx"))
    dw_gab = matmul(h, dpgab, "tn", BF16, "dw_gab")
    lrw = P["lowrank"]
    dw_in = [dw_q, dw_kv, dw_gqk, dw_gv, dw_rb, dw_lr[:, :2 * lrw], dw_gab]

    dh = matmul(dpq, W["q"], "nt", F32, "dh_q")
    dh = matmul(dpkv, W["kv"], "nt", F32, "dh_kv", add=dh)
    dh = matmul(dpgqk, W["gqk"], "nt", F32, "dh_gqk", add=dh)
    dh = matmul(dpgv, W["gv"], "nt", F32, "dh_gv", add=dh)
    dh = matmul(dprb, W["rb"], "nt", F32, "dh_rb", add=dh)
    dh = matmul(dplr, W["lr"], "nt", F32, "dh_lr", add=dh)
    dh = matmul(dpgab, W["gab"], "nt", F32, "dh_gab", add=dh)
    dhc = matmul(dpkv_c, W["kv"], "nt", F32, "dhc_kv")
    dhc = matmul(dpgqk_c, W["gqk"], "nt", F32, "dhc_gqk", add=dhc)
    dhc = matmul(dpgv_c, W["gv"], "nt", F32, "dhc_gv", add=dhc)
    dhc = matmul(dplr_c, W["lr"], "nt", F32, "dhc_lr", add=dhc)

    def mod1_bwd_fn(x, dh, dres, g, sc):
        r = _rstd(x)
        n = x * r
        dyy = dh * (1.0 + sc)
        dn = dyy * g
        dx = dres + r * (dn - n * jnp.mean(dn * n, axis=-1, keepdims=True))
        s0 = lambda a: jnp.sum(a, axis=0, keepdims=True)
        return dx, s0(dyy * n), s0(dh), s0(dh * n * g)

    grad_x, dgmix, dsh1, dsc1 = rowwise(mod1_bwd_fn, [x, dh, dx1], [P["g_mix"], sc1], [(D, F32)], [(1, D)] * 3,
                                        256, "mod1_bwd")
    _, dgmix_c, dcsh1, dcsc1 = rowwise(mod1_bwd_fn, [ctx, dhc, jnp.zeros_like(ctx)], [P["g_mix"], csc1], [(D, F32)],
                                       [(1, D)] * 3, 128, "mod1_ctx_bwd")

    zD = jnp.zeros((1, D), F32)
    grads = dict(
        w_in=dw_in, w_attn_o=dw_attn_o, w_gla_o=dw_gla_o, w_out=dw_out, w_up=dw_up, w_down=dw_down,
        dmod_x=jnp.concatenate([dsh1, dsc1, dg1, dsh2, dsc2, dg2], axis=1),
        dmod_c=jnp.concatenate([dcsh1, dcsc1, zD, zD, zD, zD], axis=1),
        g_mix=dgmix + dgmix_c, q_norm=dqn, k_norm=dkn + dkn_c, attn_sink=dsink.reshape(1, hq),
        w_gate_f=(dwgf + dwgf_c)[:lrw], b_gate_f=dbgf + dbgf_c,
        w_gate_b=(dwgb + dwgb_c)[lrw:2 * lrw], b_gate_b=dbgb + dbgb_c,
        gla_norm=dgn, g_ffn=dg_ffn,
        conv_w=jnp.concatenate([dcw_a, dcw_g], axis=2).reshape(3, -1),
        conv_b=jnp.concatenate([dcb_a, dcb_g], axis=1),
    )
    return lsum[0, 0], grad_x, grads


SMALL_REPL = ("c_ctx", "b_mod", "g_mix", "q_norm", "k_norm", "attn_sink", "b_gate_f", "b_gate_b", "gla_norm", "g_ffn",
              "conv_b")
SMALL_SHARD = ("w_gate_f", "w_gate_b", "conv_w")
ORDER = ("c_ctx", "w_mod", "b_mod", "g_mix", "w_in", "q_norm", "k_norm", "attn_sink", "w_gate_f", "b_gate_f",
         "w_gate_b", "b_gate_b", "gla_norm", "w_attn_o", "w_gla_o", "w_out", "g_ffn", "w_up", "conv_w", "conv_b",
         "w_down")


def kernel(x, c, ctx, c_ctx, w_mod, b_mod, g_mix, w_in, q_norm, k_norm, attn_sink, w_gate_f, b_gate_f, w_gate_b, b_gate_b, gla_norm, w_attn_o, w_gla_o, w_out, g_ffn, w_up, conv_w, conv_b, w_down, loss_target, m_c_ctx, m_w_mod, m_b_mod, m_g_mix, m_w_in, m_q_norm, m_k_norm, m_attn_sink, m_w_gate_f, m_b_gate_f, m_w_gate_b, m_b_gate_b, m_gla_norm, m_w_attn_o, m_w_gla_o, m_w_out, m_g_ffn, m_w_up, m_conv_w, m_conv_b, m_w_down, v_c_ctx, v_w_mod, v_b_mod, v_g_mix, v_w_in, v_q_norm, v_k_norm, v_attn_sink, v_w_gate_f, v_b_gate_f, v_w_gate_b, v_b_gate_b, v_gla_norm, v_w_attn_o, v_w_gla_o, v_w_out, v_g_ffn, v_w_up, v_conv_w, v_conv_b, v_w_down):
    loc = dict(locals())
    Wt = {n: loc[n] for n in ORDER}
    Mt = {n: loc["m_" + n] for n in ORDER}
    Vt = {n: loc["v_" + n] for n in ORDER}
    me = 4 * lax.axis_index("x") + 2 * lax.axis_index("y") + lax.axis_index("c")

    D = x.shape[-1]
    hd = q_norm.shape[-1]
    hq = attn_sink.shape[-1]
    gdv = gla_norm.shape[-1]
    gh = D // gdv
    gdk = D // 2 // gh
    lrw = w_gate_f.shape[1]
    in_w = NDEV * w_in.shape[-1]
    kvw = (in_w - hq * hd - 2 * gh * gdk - 2 * gh * gdv - 2 * lrw - 2 * D) // 2
    hkv = kvw // hd
    gcols = w_gate_f.shape[-1]
    mcols = w_mod.shape[-1]

    x2, ctx2, tgt2 = x[0], ctx[0], loss_target[0]

    c_all = exchange([jnp.pad(c, ((0, 7), (0, 0)))], True, "gather_c")[0][:, 0, :]
    c9 = jnp.concatenate([c_all, c_ctx[None, :], jnp.zeros((7, D), F32)], axis=0)
    s9 = rowwise(lambda a: a * _sig(a), [c9], [], [(D, F32)], [], 16, "silu_c")[0]
    bias = jnp.broadcast_to(lax.dynamic_slice_in_dim(b_mod, me * mcols, mcols, axis=1), (16, mcols))
    mod_cols = matmul(s9, w_mod[0], "nn", F32, "mod_cols", add=bias)
    mod_all = exchange([mod_cols], True, "gather_mod")[0]
    mod_all = jnp.transpose(mod_all, (1, 0, 2)).reshape(16, NDEV * mcols)
    mod_me = lax.dynamic_slice_in_dim(mod_all, me, 1, axis=0)
    mod = [mod_me[:, i * D:(i + 1) * D] for i in range(6)]
    modc = [mod_all[8:9, i * D:(i + 1) * D] for i in range(2)]

    o3 = jnp.stack([w_attn_o[0], w_gla_o[0], w_out[0]]).astype(BF16)
    small_w = pack([w_gate_f[0], w_gate_b[0], conv_w[0]])
    g_in, g_o3, g_up, g_down, g_small = gather_two_level(
        [w_in[0].astype(BF16), o3, w_up[0].astype(BF16), w_down[0].astype(BF16), small_w], "gather_w")
    seg = segments_from_blocks(g_in, [hq * hd, 2 * kvw, 2 * gh * gdk, gh * gdv, gh * gdv, 2 * lrw, 2 * D])
    small_parts = [unpack(g_small[j], [w_gate_f[0].shape, w_gate_b[0].shape, conv_w[0].shape]) for j in range(NDEV)]
    wgf = jnp.concatenate([p[0] for p in small_parts], axis=1)
    wgb = jnp.concatenate([p[1] for p in small_parts], axis=1)
    cw_full = jnp.concatenate([p[2] for p in small_parts], axis=1)
    o3f = [g_o3[:, i].reshape(-1, D) for i in range(3)]
    W = dict(
        q=seg[0], kv=seg[1], gqk=seg[2], gv=seg[3], rb=seg[4],
        lr=jnp.pad(seg[5], ((0, 0), (0, 128 - 2 * lrw))), gab=seg[6],
        gate_f=jnp.pad(wgf, ((0, 128 - lrw), (0, 0))),
        gate_b=jnp.pad(wgb, ((lrw, 128 - 2 * lrw), (0, 0))),
        attn_o=o3f[0], gla_o=o3f[1], out=o3f[2],
        up_a=jnp.concatenate([g_up[j] for j in range(NDEV // 2)], axis=1),
        up_g=jnp.concatenate([g_up[j] for j in range(NDEV // 2, NDEV)], axis=1),
        down=g_down.reshape(-1, D),
        conv_w=cw_full,
    )
    P = dict(hd=hd, hq=hq, hkv=hkv, gh=gh, lowrank=lrw, g_mix=g_mix, q_norm=q_norm, k_norm=k_norm, attn_sink=attn_sink,
             b_gate_f=b_gate_f, b_gate_b=b_gate_b, gla_norm=gla_norm, g_ffn=g_ffn, conv_b=conv_b)

    lsum, grad_x, G = local_step(x2, ctx2, tgt2, mod, modc, W, P)
    loss = lax.psum(0.5 * lsum / D, ("x", "y", "c"))

    dm = exchange([jnp.concatenate([G["dmod_x"], G["dmod_c"], jnp.zeros((6, 6 * D), F32)], axis=0)], True,
                  "gather_dmod")[0]
    dmc = reduce_parts(dm[:, 1:2, :].reshape(NDEV, 6 * D // 128, 128), "sum_dmod_ctx").reshape(1, 6 * D)
    dM = jnp.concatenate([dm[:, 0, :], dmc, jnp.zeros((7, 6 * D), F32)], axis=0)
    dM_cols = lax.dynamic_slice_in_dim(dM, me * mcols, mcols, axis=1)
    g_w_mod = matmul(s9, dM_cols, "tn", F32, "dw_mod")
    g_b_mod = reduce_parts(dM.reshape(16, 6 * D // 128, 128), "sum_db_mod").reshape(1, 6 * D)
    dsc = matmul(dM_cols[8:16], w_mod[0], "nt", F32, "d_silu_ctx")
    cc = jnp.broadcast_to(c_ctx[None, :], (8, D))

    def dsilu_fn(d, a):
        sg = _sig(a)
        return d * sg * (1.0 + a * (1.0 - sg))

    g_cctx_part = rowwise(dsilu_fn, [dsc, cc], [], [(D, F32)], [], 8, "d_c_ctx")[0][0:1]

    small_names = ("c_ctx", "g_mix", "q_norm", "k_norm", "attn_sink", "b_gate_f", "b_gate_b", "gla_norm", "g_ffn",
                   "conv_b", "w_gate_f", "w_gate_b", "conv_w")
    G["c_ctx"] = g_cctx_part
    sm_shapes = [G[n].shape for n in small_names]
    sm_all = exchange([pack([G[n] for n in small_names])], True, "gather_small_grads")[0]
    sm_tot = unpack(reduce_parts(sm_all, "sum_small_grads"), sm_shapes)
    gs = dict(zip(small_names, sm_tot))
    gs["b_mod"] = g_b_mod
    gs["w_gate_f"] = lax.dynamic_slice_in_dim(gs["w_gate_f"], me * gcols, gcols, axis=1)
    gs["w_gate_b"] = lax.dynamic_slice_in_dim(gs["w_gate_b"], me * gcols, gcols, axis=1)
    ccols = conv_w.shape[-1]
    gs["conv_w"] = lax.dynamic_slice_in_dim(gs["conv_w"], me * ccols, ccols, axis=1)

    orows = w_attn_o.shape[1]
    s_in = blocks_from_segments(G["w_in"], w_in.shape[-1])
    s_o3 = jnp.concatenate([rows_to_blocks(G["w_attn_o"]), rows_to_blocks(G["w_gla_o"]), rows_to_blocks(G["w_out"])],
                           axis=1)
    s_up = blocks_from_segments(G["w_up"], w_up.shape[-1])
    s_down = rows_to_blocks(G["w_down"])
    r_in, r_o3, r_up, r_down = scatter_reduce([s_in, s_o3, s_up, s_down], "scatter_grads")

    out = {}
    out["w_in"] = adam_reduce(r_in, w_in[0], m_w_in[0], v_w_in[0], "adam_w_in")
    o3w = jnp.concatenate([w_attn_o[0], w_gla_o[0], w_out[0]], axis=0)
    o3m = jnp.concatenate([m_w_attn_o[0], m_w_gla_o[0], m_w_out[0]], axis=0)
    o3v = jnp.concatenate([v_w_attn_o[0], v_w_gla_o[0], v_w_out[0]], axis=0)
    ro3 = adam_reduce(r_o3, o3w, o3m, o3v, "adam_o3")
    for i, n in enumerate(("w_attn_o", "w_gla_o", "w_out")):
        out[n] = [a[i * orows:(i + 1) * orows] for a in ro3]
    out["w_up"] = adam_reduce(r_up, w_up[0], m_w_up[0], v_w_up[0], "adam_w_up")
    out["w_down"] = adam_reduce(r_down, w_down[0], m_w_down[0], v_w_down[0], "adam_w_down")
    out["w_mod"] = adam_reduce(g_w_mod[None], w_mod[0], m_w_mod[0], v_w_mod[0], "adam_w_mod")
    sm_names = SMALL_REPL + SMALL_SHARD
    shapes = [Wt[n].shape for n in sm_names]
    rs = adam_reduce(pack([gs[n] for n in sm_names])[None], pack([Wt[n] for n in sm_names]),
                     pack([Mt[n] for n in sm_names]), pack([Vt[n] for n in sm_names]), "adam_small")
    rs = [unpack(a, shapes) for a in rs]
    for i, n in enumerate(sm_names):
        out[n] = [a[i] for a in rs]

    res = [loss, grad_x[None]]
    for k in range(4):
        for n in ORDER:
            res.append(out[n][k].reshape(Wt[n].shape))
    return tuple(res)
```

```python
import jax
import jax.numpy as jnp
import numpy as np
from jax import lax
from jax.experimental import pallas as pl
from jax.experimental.pallas import tpu as pltpu

F32 = jnp.float32
BF16 = jnp.bfloat16

NDEV = 8
NCHIP = 4
EPS = 1e-6
WINDOW = 128
BLOCK = 128
GRID_W = 64
ROPE_THETA = 10000.0
GLA_CHUNK = 128
GLA_GATE_NORM = 16.0
ADAM_LR = 0.001
ADAM_B1 = 0.9
ADAM_B2 = 0.999
ADAM_EPS = 1e-08
ADAM_WD = 0.01
ADAM_STEP = 10
V7X_VMEM_LIMIT = 56 * 1024 * 1024
MATMUL_VMEM_BUDGET = 44 * 1024 * 1024
MATMUL_TILES = (1024, 1408, 512, 256, 128)
NEG = -1e30

NN = ((1,), (0,))
NT = ((1,), (1,))
TN = ((0,), (0,))


def _dot(a, b, dims):
    return lax.dot_general(a, b, (dims, ((), ())), preferred_element_type=F32)


def _cp(sem):
    return pltpu.CompilerParams(dimension_semantics=sem, vmem_limit_bytes=V7X_VMEM_LIMIT)


def _pick(n, cands):
    for c in cands:
        if n % c == 0:
            return c
    return n


def _sig(x):
    return 1.0 / (1.0 + jnp.exp(-x))


def _sig_tanh(x):
    return 0.5 * jnp.tanh(0.5 * x) + 0.5


def _rstd(x):
    return lax.rsqrt(jnp.mean(x * x, axis=-1, keepdims=True) + EPS)


_ANY = pl.BlockSpec(memory_space=pl.ANY)


def _place():
    return lax.axis_index("x"), lax.axis_index("y"), lax.axis_index("c")


def exchange(srcs, bcast, name, group="all"):
    n = len(srcs)
    ndev = NDEV if group == "all" else NCHIP
    ks = tuple(range(1, NDEV)) if group == "all" else (2, 4, 6)
    out_shape = [jax.ShapeDtypeStruct((ndev,) + (s.shape if bcast else s.shape[1:]), s.dtype) for s in srcs]

    def body(*refs):
        src, dst = refs[:n], refs[n:2 * n]
        send_sems, recv_sems, loc_sems = refs[2 * n:]
        x, y, c = _place()

        def idx(px, py, pc):
            return 4 * px + 2 * py + pc if group == "all" else 2 * px + py

        me = idx(x, y, c)
        copies = []
        for a in range(n):
            cp = pltpu.make_async_copy(src[a] if bcast else src[a].at[me], dst[a].at[me], loc_sems.at[a])
            cp.start()
            copies.append(cp)
        for s, k in enumerate(ks):
            px, py, pc = x ^ ((k >> 2) & 1), y ^ ((k >> 1) & 1), c ^ (k & 1)
            for a in range(n):
                cp = pltpu.make_async_remote_copy(
                    src_ref=src[a] if bcast else src[a].at[idx(px, py, pc)],
                    dst_ref=dst[a].at[me],
                    send_sem=send_sems.at[a, s],
                    recv_sem=recv_sems.at[a, s],
                    device_id=(px, py, pc),
                    device_id_type=pl.DeviceIdType.MESH,
                )
                cp.start()
                copies.append(cp)
        for cp in copies:
            cp.wait()

    return pl.pallas_call(
        body,
        out_shape=out_shape,
        in_specs=[_ANY] * n,
        out_specs=[_ANY] * n,
        scratch_shapes=[
            pltpu.SemaphoreType.DMA((n, len(ks))),
            pltpu.SemaphoreType.DMA((n, len(ks))),
            pltpu.SemaphoreType.DMA((n,)),
        ],
        name=name,
    )(*srcs)


def gather_two_level(srcs, name):
    n = len(srcs)
    out_shape = [jax.ShapeDtypeStruct((NDEV,) + s.shape, s.dtype) for s in srcs]

    def body(*refs):
        src, dst = refs[:n], refs[n:2 * n]
        send_sems, recv_sems, loc_sems = refs[2 * n:]
        x, y, c = _place()
        me = 4 * x + 2 * y + c
        sib = (x, y, 1 - c)
        first = (x ^ (1 - c), y ^ c)
        second = (x ^ c, y ^ (1 - c))
        diag = (x ^ 1, y ^ 1)

        def row(chip, core):
            return 4 * chip[0] + 2 * chip[1] + core

        def copy(a, s, block, to, from_src=False):
            return pltpu.make_async_remote_copy(
                src_ref=src[a] if from_src else dst[a].at[block], dst_ref=dst[a].at[block],
                send_sem=send_sems.at[a, s], recv_sem=recv_sems.at[a, s],
                device_id=to, device_id_type=pl.DeviceIdType.MESH)

        local = [pltpu.make_async_copy(src[a], dst[a].at[me], loc_sems.at[a]) for a in range(n)]
        sent = [copy(a, 0, me, sib, True) for a in range(n)]
        sent += [copy(a, 1, me, (*first, c), True) for a in range(n)]
        sent += [copy(a, 2, me, (*second, c), True) for a in range(n)]
        for cp in local + sent:
            cp.start()
        for a in range(n):
            copy(a, 1, row(first, c), (*first, c)).wait_recv()
            for cp in (copy(a, 3, row(first, c), (*second, c)), copy(a, 5, row(first, c), sib)):
                cp.start()
                sent.append(cp)
        for a in range(n):
            copy(a, 2, row(second, c), (*second, c)).wait_recv()
            cp = copy(a, 4, row(second, c), sib)
            cp.start()
            sent.append(cp)
        for a in range(n):
            copy(a, 3, row(diag, c), (*second, c)).wait_recv()
            cp = copy(a, 6, row(diag, c), sib)
            cp.start()
            sent.append(cp)
        for a in range(n):
            copy(a, 0, row((x, y), 1 - c), sib).wait_recv()
            copy(a, 4, row(first, 1 - c), sib).wait_recv()
            copy(a, 5, row(second, 1 - c), sib).wait_recv()
            copy(a, 6, row(diag, 1 - c), sib).wait_recv()
        for cp in local:
            cp.wait()
        for cp in sent:
            cp.wait_send()

    return pl.pallas_call(
        body,
        out_shape=out_shape,
        in_specs=[_ANY] * n,
        out_specs=[_ANY] * n,
        scratch_shapes=[
            pltpu.SemaphoreType.DMA((n, NDEV - 1)),
            pltpu.SemaphoreType.DMA((n, NDEV - 1)),
            pltpu.SemaphoreType.DMA((n,)),
        ],
        name=name,
    )(*srcs)


def _chip_across(core, da, db):
    x, y, _ = _place()
    return x ^ (da * (1 - core) + db * core), y ^ (db * (1 - core) + da * core)


def pair_swap(srcs, name, axis="c"):
    n = len(srcs)

    def body(*refs):
        src, dst = refs[:n], refs[n:2 * n]
        send_sems, recv_sems = refs[2 * n:]
        x, y, c = _place()
        partner = {"c": (x, y, 1 - c), "first": (*_chip_across(c, 1, 0), c), "second": (*_chip_across(c, 0, 1), c)}[axis]
        copies = []
        for a in range(n):
            cp = pltpu.make_async_remote_copy(
                src_ref=src[a], dst_ref=dst[a], send_sem=send_sems.at[a], recv_sem=recv_sems.at[a],
                device_id=partner, device_id_type=pl.DeviceIdType.MESH)
            cp.start()
            copies.append(cp)
        for cp in copies:
            cp.wait()

    return pl.pallas_call(
        body,
        out_shape=[jax.ShapeDtypeStruct(s.shape, s.dtype) for s in srcs],
        in_specs=[_ANY] * n,
        out_specs=[_ANY] * n,
        scratch_shapes=[pltpu.SemaphoreType.DMA((n,)), pltpu.SemaphoreType.DMA((n,))],
        name=name,
    )(*srcs)


_OFFSETS = ((0, 0), (0, 1), (1, 0), (1, 1))


def sibling_swap_blocks(blocks, name):
    n = len(blocks)

    def body(*refs):
        src, dst = refs[:n], refs[n:2 * n]
        send_sems, recv_sems = refs[2 * n:]
        x, y, c = _place()
        copies = []
        for a in range(n):
            for j, (da, db) in enumerate(_OFFSETS):
                px, py = _chip_across(1 - c, da, db)
                cp = pltpu.make_async_remote_copy(
                    src_ref=src[a].at[4 * px + 2 * py + (1 - c)], dst_ref=dst[a].at[j],
                    send_sem=send_sems.at[a, j], recv_sem=recv_sems.at[a, j],
                    device_id=(x, y, 1 - c), device_id_type=pl.DeviceIdType.MESH)
                cp.start()
                copies.append(cp)
        for cp in copies:
            cp.wait()

    return pl.pallas_call(
        body,
        out_shape=[jax.ShapeDtypeStruct((4,) + b.shape[1:], b.dtype) for b in blocks],
        in_specs=[_ANY] * n,
        out_specs=[_ANY] * n,
        scratch_shapes=[pltpu.SemaphoreType.DMA((n, 4)), pltpu.SemaphoreType.DMA((n, 4))],
        name=name,
    )(*blocks)


def add_own_blocks(blocks, got, name):
    _, R, C = blocks.shape
    tile = _pick(R, (256, 128, 64))

    def body(*refs):
        for j in range(4):
            refs[8 + j][...] = (refs[j][...].astype(F32) + refs[4 + j][...].astype(F32)).astype(refs[8 + j].dtype)

    def own(da, db):
        def index(i):
            c = lax.axis_index("c")
            px, py = _chip_across(c, da, db)
            return 4 * px + 2 * py + c, i, 0
        return pl.BlockSpec((None, tile, C), index)

    return pl.pallas_call(
        body,
        grid=(R // tile,),
        in_specs=[own(da, db) for da, db in _OFFSETS]
        + [pl.BlockSpec((None, tile, C), lambda i, j=j: (j, i, 0)) for j in range(4)],
        out_specs=[pl.BlockSpec((tile, C), lambda i: (i, 0))] * 4,
        out_shape=[jax.ShapeDtypeStruct((R, C), blocks.dtype)] * 4,
        compiler_params=_cp(("parallel",)),
        name=name,
    )(blocks, blocks, blocks, blocks, got, got, got, got)


def scatter_reduce(blocks, name):
    def add(n_out, ins, label):
        fn = lambda *a: [a[i].astype(F32) + a[n_out + i].astype(F32) for i in range(n_out)]
        rows, cols = ins[0].shape
        return rowwise(fn, ins, [], [(cols, ins[0].dtype)] * n_out, [], _pick(rows, (256, 128, 64)), label)

    nb = len(blocks)
    got = sibling_swap_blocks(blocks, name + "_d2d")
    q = [add_own_blocks(blocks[i], got[i], f"{name}_sum0_{i}") for i in range(nb)]
    r1 = pair_swap([q[i][j] for i in range(nb) for j in (2, 3)], name + "_ici1", "first")
    k = [add(2, [q[i][0], q[i][1], r1[2 * i], r1[2 * i + 1]], f"{name}_sum1_{i}") for i in range(nb)]
    r2 = pair_swap([k[i][1] for i in range(nb)], name + "_ici2", "second")
    return [jnp.stack([k[i][0], r2[i]]) for i in range(nb)]


def matmul(a, b, mode, out_dtype, name, add=None):
    if mode == "nn":
        (M, K), N = a.shape, b.shape[1]
    elif mode == "nt":
        (M, K), N = a.shape, b.shape[0]
    else:
        (K, M), N = a.shape, b.shape[1]
    tm = _pick(M, MATMUL_TILES)
    tn = _pick(N, MATMUL_TILES)
    osz = jnp.dtype(out_dtype).itemsize

    def vmem_bytes(tk):
        ops = 2 * tk * (tm * a.dtype.itemsize + tn * b.dtype.itemsize)
        return ops + tm * tn * (2 * osz + (4 if tk < K else 0) + (8 if add is not None else 0))

    tk = next((t for t in (K, 4096, 2816, 2048, 1408, 1024, 512, 256, 128)
               if K % t == 0 and vmem_bytes(t) <= MATMUL_VMEM_BUDGET), K)
    nk = K // tk
    dims = {"nn": NN, "nt": NT, "tn": TN}[mode]

    def body(*refs):
        if add is None:
            a_ref, b_ref, o_ref = refs[:3]
            c_ref = None
        else:
            a_ref, b_ref, c_ref, o_ref = refs[:4]

        def prod():
            return _dot(a_ref[...].astype(BF16), b_ref[...].astype(BF16), dims)

        def finish(r):
            if c_ref is not None:
                r = r + c_ref[...].astype(F32)
            o_ref[...] = r.astype(o_ref.dtype)

        if nk == 1:
            finish(prod())
            return
        acc = refs[-1]
        k = pl.program_id(2)

        @pl.when(k == 0)
        def _():
            acc[...] = prod()

        if nk > 2:
            @pl.when((k > 0) & (k < nk - 1))
            def _():
                acc[...] += prod()

        @pl.when(k == nk - 1)
        def _():
            finish(acc[...] + prod())

    a_spec = pl.BlockSpec((tk, tm), lambda i, j, k: (k, i)) if mode == "tn" else pl.BlockSpec((tm, tk), lambda i, j, k: (i, k))
    b_spec = pl.BlockSpec((tn, tk), lambda i, j, k: (j, k)) if mode == "nt" else pl.BlockSpec((tk, tn), lambda i, j, k: (k, j))
    o_spec = pl.BlockSpec((tm, tn), lambda i, j, k: (i, j))
    ins, specs = [a, b], [a_spec, b_spec]
    if add is not None:
        ins.append(add)
        specs.append(o_spec)
    return pl.pallas_call(
        body,
        grid=(M // tm, N // tn, nk),
        in_specs=specs,
        out_specs=o_spec,
        out_shape=jax.ShapeDtypeStruct((M, N), out_dtype),
        scratch_shapes=[pltpu.VMEM((tm, tn), F32)] if nk > 1 else [],
        compiler_params=_cp(("parallel", "parallel", "arbitrary")),
        name=name,
    )(*ins)


def rowwise(fn, tiled, full, out_tiled, out_acc, tile, name):
    tiled = [t if isinstance(t, tuple) else (t, t.shape[1], 0) for t in tiled]
    rows = tiled[0][0].shape[0]
    tile = min(tile, rows)
    assert rows % tile == 0
    nt, nf, no = len(tiled), len(full), len(out_tiled)

    def body(*refs):
        ins = [r[...] for r in refs[:nt + nf]]
        res = fn(*ins)
        if not isinstance(res, (tuple, list)):
            res = (res,)
        outs = refs[nt + nf:]
        for r, v in zip(outs[:no], res[:no]):
            r[...] = v.astype(r.dtype)
        if out_acc:
            @pl.when(pl.program_id(0) == 0)
            def _():
                for r in outs[no:]:
                    r[...] = jnp.zeros_like(r)

            for r, v in zip(outs[no:], res[no:]):
                r[...] += v

    in_specs = [pl.BlockSpec((tile, w), lambda i, cb=cb: (i, cb)) for (_, w, cb) in tiled]
    in_specs += [pl.BlockSpec(f.shape, lambda i, nd=f.ndim: (0,) * nd) for f in full]
    out_specs = [pl.BlockSpec((tile, w), lambda i: (i, 0)) for (w, _) in out_tiled]
    out_specs += [pl.BlockSpec(s, lambda i, nd=len(s): (0,) * nd) for s in out_acc]
    out_shape = [jax.ShapeDtypeStruct((rows, w), dt) for (w, dt) in out_tiled]
    out_shape += [jax.ShapeDtypeStruct(s, F32) for s in out_acc]
    res = pl.pallas_call(
        body,
        grid=(rows // tile,),
        in_specs=in_specs,
        out_specs=out_specs,
        out_shape=out_shape,
        compiler_params=_cp(("arbitrary",) if out_acc else ("parallel",)),
        name=name,
    )(*[t[0] for t in tiled], *full)
    return res


def adam_reduce(parts, w, m, v, name):
    P, R, C = parts.shape
    tr = _pick(R, (256, 128, 64, 32, 16, 8))
    c1 = 1.0 - ADAM_B1 ** ADAM_STEP
    c2 = 1.0 - ADAM_B2 ** ADAM_STEP

    def body(p_ref, w_ref, m_ref, v_ref, g_ref, d_ref, nm_ref, nv_ref):
        g = p_ref[0].astype(F32)
        for j in range(1, P):
            g = g + p_ref[j].astype(F32)
        mm = ADAM_B1 * m_ref[...] + (1.0 - ADAM_B1) * g
        vv = ADAM_B2 * v_ref[...] + (1.0 - ADAM_B2) * (g * g)
        m_hat = mm / c1
        v_hat = vv / c2
        g_ref[...] = g
        d_ref[...] = -ADAM_LR * (m_hat / (jnp.sqrt(v_hat) + ADAM_EPS) + ADAM_WD * w_ref[...])
        nm_ref[...] = mm
        nv_ref[...] = vv

    spec = pl.BlockSpec((tr, C), lambda i: (i, 0))
    return pl.pallas_call(
        body,
        grid=(R // tr,),
        in_specs=[pl.BlockSpec((P, tr, C), lambda i: (0, i, 0)), spec, spec, spec],
        out_specs=[spec] * 4,
        out_shape=[jax.ShapeDtypeStruct((R, C), F32)] * 4,
        compiler_params=_cp(("parallel",)),
        name=name,
    )(parts, w, m, v)


def reduce_parts(parts, name):
    P, R, C = parts.shape
    tr = _pick(R, (64, 32, 16, 8))

    def body(p_ref, g_ref):
        g = p_ref[0]
        for j in range(1, P):
            g = g + p_ref[j]
        g_ref[...] = g

    return pl.pallas_call(
        body,
        grid=(R // tr,),
        in_specs=[pl.BlockSpec((P, tr, C), lambda i: (0, i, 0))],
        out_specs=pl.BlockSpec((tr, C), lambda i: (i, 0)),
        out_shape=jax.ShapeDtypeStruct((R, C), F32),
        compiler_params=_cp(("parallel",)),
        name=name,
    )(parts)


def pack(arrs):
    flat = jnp.concatenate([a.reshape(-1).astype(F32) for a in arrs])
    n = flat.shape[0]
    padded = -(-n // 1024) * 1024
    return jnp.pad(flat, (0, padded - n)).reshape(padded // 128, 128)


def blocks_from_segments(segs, ncols):
    offs = np.cumsum([0] + [s.shape[1] for s in segs]).tolist()
    blocks = []
    for j in range(NDEV):
        lo, hi = j * ncols, (j + 1) * ncols
        parts = [s[:, max(lo, o) - o:min(hi, o + s.shape[1]) - o]
                 for s, o in zip(segs, offs[:-1]) if max(lo, o) < min(hi, o + s.shape[1])]
        blocks.append(jnp.concatenate(parts, axis=1) if len(parts) > 1 else parts[0])
    return jnp.stack(blocks)


def rows_to_blocks(g):
    return g.reshape(NDEV, -1, g.shape[1])


def segments_from_blocks(g, widths):
    ncols = g.shape[2]
    offs = np.cumsum([0] + list(widths)).tolist()
    out = []
    for o, w in zip(offs[:-1], widths):
        parts = [g[j][:, max(j * ncols, o) - j * ncols:min((j + 1) * ncols, o + w) - j * ncols]
                 for j in range(NDEV) if max(j * ncols, o) < min((j + 1) * ncols, o + w)]
        out.append(jnp.concatenate(parts, axis=1) if len(parts) > 1 else parts[0])
    return out


def unpack(slab, shapes):
    flat = slab.reshape(-1)
    out, off = [], 0
    for s in shapes:
        size = int(np.prod(s))
        out.append(flat[off:off + size].reshape(s))
        off += size
    return out


def modulate_fwd(x, g, sh, sc, name):
    def fn(x, g, sh, sc):
        return x * _rstd(x) * g * (1.0 + sc) + sh

    return rowwise(fn, [x], [g, sh, sc], [(x.shape[1], BF16)], [], 256, name)[0]


def norm_rope_fwd(p, width, cb, w, cosf, sinf, hd, name):
    nh = width // hd

    def fn(x, cosf, sinf, w):
        outs = []
        for h in range(nh):
            xh = x[:, h * hd:(h + 1) * hd]
            y = xh * _rstd(xh) * w
            outs.append(y * cosf + pltpu.roll(y, hd // 2, 1) * sinf)
        return jnp.concatenate(outs, axis=1) if nh > 1 else outs[0]

    return rowwise(fn, [(p, width, cb), cosf, sinf], [w], [(width, BF16)], [], 256, name)[0]


def norm_rope_bwd(p, width, cb, d, w, cosf, sinf, hd, name):
    nh = width // hd

    def fn(x, d, cosf, sinf, w):
        outs = []
        dw = jnp.zeros((1, hd), F32)
        for h in range(nh):
            xh = x[:, h * hd:(h + 1) * hd]
            dh = d[:, h * hd:(h + 1) * hd].astype(F32)
            r = _rstd(xh)
            n = xh * r
            dy = dh * cosf + pltpu.roll(dh * sinf, hd // 2, 1)
            dw = dw + jnp.sum(dy * n, axis=0, keepdims=True)
            dn = dy * w
            outs.append(r * (dn - n * jnp.mean(dn * n, axis=-1, keepdims=True)))
        return (jnp.concatenate(outs, axis=1) if nh > 1 else outs[0]), dw

    return rowwise(fn, [(p, width, cb), d, cosf, sinf], [w], [(width, BF16)], [(1, hd)], 256, name)


def attention_fwd(qr, kr, pkv, kcr, pkv_c, sink, hkv, hd, name):
    T, L = qr.shape[0], kcr.shape[0]
    G = qr.shape[1] // (hkv * hd)
    nb = T // BLOCK
    scale = hd ** -0.5

    def body(q_ref, kp, kc, kn, vp, vc, vn, ck_ref, cv_ref, sink_ref, o_ref, lse_ref, lser_ref):
        i = pl.program_id(1)
        kwin = jnp.concatenate([kp[...], kc[...], kn[...]], axis=0)
        vwin = jnp.concatenate([vp[...], vc[...], vn[...]], axis=0).astype(BF16)
        ck, cv = ck_ref[...], cv_ref[...].astype(BF16)
        row = lax.broadcasted_iota(jnp.int32, (BLOCK, 3 * BLOCK), 0)
        col = lax.broadcasted_iota(jnp.int32, (BLOCK, 3 * BLOCK), 1)
        rel = col - BLOCK - row
        valid = (jnp.abs(rel) <= WINDOW) & ((col >= BLOCK) | (i > 0)) & ((col < 2 * BLOCK) | (i < nb - 1))
        R = range(G)
        qa = q_ref[...]
        qs = [qa[:, g * hd:(g + 1) * hd] for g in R]
        sks = [sink_ref[g] for g in R]
        ss = [jnp.where(valid, _dot(qs[g], kwin, NT) * scale, NEG) for g in R]
        scs = [_dot(qs[g], ck, NT) * scale for g in R]
        ms = [jnp.maximum(jnp.maximum(jnp.max(ss[g], axis=1, keepdims=True), jnp.max(scs[g], axis=1, keepdims=True)),
                          sks[g]) for g in R]
        ps = [jnp.exp(ss[g] - ms[g]) for g in R]
        pcs = [jnp.exp(scs[g] - ms[g]) for g in R]
        nums = [_dot(ps[g].astype(BF16), vwin, NN) + _dot(pcs[g].astype(BF16), cv, NN) for g in R]
        dens = [jnp.exp(sks[g] - ms[g]) + jnp.sum(ps[g], axis=1, keepdims=True) + jnp.sum(pcs[g], axis=1, keepdims=True)
                for g in R]
        o_ref[...] = jnp.concatenate([(nums[g] / dens[g]).astype(o_ref.dtype) for g in R], axis=1)
        eye = (lax.broadcasted_iota(jnp.int32, (BLOCK, BLOCK), 0)
               == lax.broadcasted_iota(jnp.int32, (BLOCK, BLOCK), 1)).astype(F32)
        for g in R:
            lg = ms[g] + jnp.log(dens[g])
            lse_ref[g] = lg
            lser_ref[g] = jnp.sum(lg * eye, axis=0, keepdims=True)

    kv_specs = [
        pl.BlockSpec((BLOCK, hd), lambda h, i: (jnp.maximum(i - 1, 0), h)),
        pl.BlockSpec((BLOCK, hd), lambda h, i: (i, h)),
        pl.BlockSpec((BLOCK, hd), lambda h, i: (jnp.minimum(i + 1, nb - 1), h)),
    ]
    v_specs = [
        pl.BlockSpec((BLOCK, hd), lambda h, i: (jnp.maximum(i - 1, 0), hkv + h)),
        pl.BlockSpec((BLOCK, hd), lambda h, i: (i, hkv + h)),
        pl.BlockSpec((BLOCK, hd), lambda h, i: (jnp.minimum(i + 1, nb - 1), hkv + h)),
    ]
    return pl.pallas_call(
        body,
        grid=(hkv, nb),
        in_specs=[pl.BlockSpec((BLOCK, G * hd), lambda h, i: (i, h))] + kv_specs + v_specs + [
            pl.BlockSpec((L, hd), lambda h, i: (0, h)),
            pl.BlockSpec((L, hd), lambda h, i: (0, hkv + h)),
            pl.BlockSpec((G, 1, 1), lambda h, i: (h, 0, 0)),
        ],
        out_specs=[
            pl.BlockSpec((BLOCK, G * hd), lambda h, i: (i, h)),
            pl.BlockSpec((G, BLOCK, 1), lambda h, i: (h, i, 0)),
            pl.BlockSpec((G, 1, BLOCK), lambda h, i: (h, 0, i)),
        ],
        out_shape=[jax.ShapeDtypeStruct(qr.shape, BF16), jax.ShapeDtypeStruct((hkv * G, T, 1), F32),
                   jax.ShapeDtypeStruct((hkv * G, 1, T), F32)],
        compiler_params=_cp(("parallel", "parallel")),
        name=name,
    )(qr, kr, kr, kr, pkv, pkv, pkv, kcr, pkv_c, sink)


def attention_bwd_q(qr, kr, pkv, kcr, pkv_c, sink, do, o, lse, hkv, hd, name):
    T, L = qr.shape[0], kcr.shape[0]
    G = qr.shape[1] // (hkv * hd)
    nb = T // BLOCK
    scale = hd ** -0.5

    def body(q_ref, kp, kc, kn, vp, vc, vn, ck_ref, cv_ref, sink_ref, do_ref, o_ref, lse_ref,
             dq_ref, dck_ref, dcv_ref, dsink_ref, drr_ref):
        i = pl.program_id(1)

        @pl.when(i == 0)
        def _():
            dck_ref[...] = jnp.zeros_like(dck_ref)
            dcv_ref[...] = jnp.zeros_like(dcv_ref)
            dsink_ref[...] = jnp.zeros_like(dsink_ref)

        kwin = jnp.concatenate([kp[...], kc[...], kn[...]], axis=0)
        vwin = jnp.concatenate([vp[...], vc[...], vn[...]], axis=0).astype(BF16)
        ck, cv = ck_ref[...], cv_ref[...].astype(BF16)
        row = lax.broadcasted_iota(jnp.int32, (BLOCK, 3 * BLOCK), 0)
        col = lax.broadcasted_iota(jnp.int32, (BLOCK, 3 * BLOCK), 1)
        rel = col - BLOCK - row
        valid = (jnp.abs(rel) <= WINDOW) & ((col >= BLOCK) | (i > 0)) & ((col < 2 * BLOCK) | (i < nb - 1))
        R = range(G)
        qa, doa, oa = q_ref[...], do_ref[...], o_ref[...]
        qs = [qa[:, g * hd:(g + 1) * hd] for g in R]
        dos = [doa[:, g * hd:(g + 1) * hd] for g in R]
        lgs = [lse_ref[g] for g in R]
        sks = [sink_ref[g] for g in R]
        ss = [jnp.where(valid, _dot(qs[g], kwin, NT) * scale, NEG) for g in R]
        scs = [_dot(qs[g], ck, NT) * scale for g in R]
        dps = [_dot(dos[g], vwin, NT) for g in R]
        dpcs = [_dot(dos[g], cv, NT) for g in R]
        drs = [jnp.sum(dos[g].astype(F32) * oa[:, g * hd:(g + 1) * hd].astype(F32), axis=1, keepdims=True) for g in R]
        ps = [jnp.exp(ss[g] - lgs[g]) for g in R]
        pcs = [jnp.exp(scs[g] - lgs[g]) for g in R]
        dss = [(ps[g] * (dps[g] - drs[g]) * scale).astype(BF16) for g in R]
        dscs = [(pcs[g] * (dpcs[g] - drs[g]) * scale).astype(BF16) for g in R]
        dqs = [_dot(dss[g], kwin, NN) + _dot(dscs[g], ck, NN) for g in R]
        dcks = [_dot(dscs[g], qs[g], TN) for g in R]
        dcvs = [_dot(pcs[g].astype(BF16), dos[g], TN) for g in R]
        dq_ref[...] = jnp.concatenate(dqs, axis=1)
        dck_ref[...] += (dcks[0] + dcks[1]) + (dcks[2] + dcks[3]) if G == 4 else sum(dcks[1:], dcks[0])
        dcv_ref[...] += (dcvs[0] + dcvs[1]) + (dcvs[2] + dcvs[3]) if G == 4 else sum(dcvs[1:], dcvs[0])
        eye = (lax.broadcasted_iota(jnp.int32, (BLOCK, BLOCK), 0)
               == lax.broadcasted_iota(jnp.int32, (BLOCK, BLOCK), 1)).astype(F32)
        for g in R:
            dsink_ref[g] += -jnp.sum(jnp.exp(sks[g] - lgs[g]) * drs[g], axis=0, keepdims=True)
            drr_ref[g] = jnp.sum(drs[g] * eye, axis=0, keepdims=True)

    kv_specs = [
        pl.BlockSpec((BLOCK, hd), lambda h, i: (jnp.maximum(i - 1, 0), h)),
        pl.BlockSpec((BLOCK, hd), lambda h, i: (i, h)),
        pl.BlockSpec((BLOCK, hd), lambda h, i: (jnp.minimum(i + 1, nb - 1), h)),
    ]
    v_specs = [
        pl.BlockSpec((BLOCK, hd), lambda h, i: (jnp.maximum(i - 1, 0), hkv + h)),
        pl.BlockSpec((BLOCK, hd), lambda h, i: (i, hkv + h)),
        pl.BlockSpec((BLOCK, hd), lambda h, i: (jnp.minimum(i + 1, nb - 1), hkv + h)),
    ]
    qspec = pl.BlockSpec((BLOCK, G * hd), lambda h, i: (i, h))
    return pl.pallas_call(
        body,
        grid=(hkv, nb),
        in_specs=[qspec] + kv_specs + v_specs + [
            pl.BlockSpec((L, hd), lambda h, i: (0, h)),
            pl.BlockSpec((L, hd), lambda h, i: (0, hkv + h)),
            pl.BlockSpec((G, 1, 1), lambda h, i: (h, 0, 0)),
            qspec, qspec,
            pl.BlockSpec((G, BLOCK, 1), lambda h, i: (h, i, 0)),
        ],
        out_specs=[
            qspec,
            pl.BlockSpec((L, hd), lambda h, i: (0, h)),
            pl.BlockSpec((L, hd), lambda h, i: (0, h)),
            pl.BlockSpec((G, 1, 1), lambda h, i: (h, 0, 0)),
            pl.BlockSpec((G, 1, BLOCK), lambda h, i: (h, 0, i)),
        ],
        out_shape=[
            jax.ShapeDtypeStruct(qr.shape, F32),
            jax.ShapeDtypeStruct((L, hkv * hd), F32),
            jax.ShapeDtypeStruct((L, hkv * hd), F32),
            jax.ShapeDtypeStruct((hkv * G, 1, 1), F32),
            jax.ShapeDtypeStruct((hkv * G, 1, T), F32),
        ],
        compiler_params=_cp(("parallel", "arbitrary")),
        name=name,
    )(qr, kr, kr, kr, pkv, pkv, pkv, kcr, pkv_c, sink, do, o, lse)


def attention_bwd_kv(qr, kr, pkv, do, lse_row, dr_row, hkv, hd, name):
    T = qr.shape[0]
    G = qr.shape[1] // (hkv * hd)
    nb = T // BLOCK
    scale = hd ** -0.5

    def body(k_ref, v_ref, *refs):
        qs, dos, lses, drs = refs[0:3], refs[3:6], refs[6:9], refs[9:12]
        dk_ref, dv_ref = refs[12:]
        j = pl.program_id(1)
        k = k_ref[...]
        v = v_ref[...].astype(BF16)
        row = lax.broadcasted_iota(jnp.int32, (BLOCK, BLOCK), 0)
        col = lax.broadcasted_iota(jnp.int32, (BLOCK, BLOCK), 1)
        bias = []
        for d in range(3):
            iq = j + d - 1
            rel = row - col - (d - 1) * BLOCK
            valid = (jnp.abs(rel) <= WINDOW) & (iq >= 0) & (iq < nb)
            bias += [jnp.where(valid, 0.0, NEG)] * G
        bias = jnp.concatenate(bias, axis=1)

        def stack(refs):
            vals = [r[...] for r in refs]
            return jnp.concatenate([a[:, g * hd:(g + 1) * hd] for a in vals for g in range(G)], axis=0)

        q, dob = stack(qs), stack(dos)
        lrow = jnp.concatenate([r[g] for r in lses for g in range(G)], axis=1)
        drow = jnp.concatenate([r[g] for r in drs for g in range(G)], axis=1)
        st = _dot(k, q, NT) * scale + bias
        pt = jnp.exp(st - lrow)
        dpt = _dot(v, dob, NT)
        dst = (pt * (dpt - drow) * scale).astype(BF16)
        dk_ref[...] = _dot(dst, q, NN).astype(dk_ref.dtype)
        dv_ref[...] = _dot(pt.astype(BF16), dob, NN).astype(dv_ref.dtype)

    def q3(width_block):
        return [
            pl.BlockSpec(width_block, lambda h, j: (jnp.maximum(j - 1, 0), h)),
            pl.BlockSpec(width_block, lambda h, j: (j, h)),
            pl.BlockSpec(width_block, lambda h, j: (jnp.minimum(j + 1, nb - 1), h)),
        ]

    row3 = [
        pl.BlockSpec((G, 1, BLOCK), lambda h, j: (h, 0, jnp.maximum(j - 1, 0))),
        pl.BlockSpec((G, 1, BLOCK), lambda h, j: (h, 0, j)),
        pl.BlockSpec((G, 1, BLOCK), lambda h, j: (h, 0, jnp.minimum(j + 1, nb - 1))),
    ]
    qb = (BLOCK, G * hd)
    return pl.pallas_call(
        body,
        grid=(hkv, nb),
        in_specs=[pl.BlockSpec((BLOCK, hd), lambda h, j: (j, h)), pl.BlockSpec((BLOCK, hd), lambda h, j: (j, hkv + h))]
        + q3(qb) + q3(qb) + row3 + row3,
        out_specs=[pl.BlockSpec((BLOCK, hd), lambda h, j: (j, h))] * 2,
        out_shape=[jax.ShapeDtypeStruct((T, hkv * hd), BF16)] * 2,
        compiler_params=_cp(("parallel", "parallel")),
        name=name,
    )(kr, pkv, qr, qr, qr, do, do, do, lse_row, lse_row, lse_row, dr_row, dr_row, dr_row)


def gate_fwd(plr, wf, wb, bf, bb, name):
    n = wf.shape[1]

    def fn(lr, wf, wb, bf, bb):
        lrb = lr.astype(BF16)
        outs = []
        for w, b in ((wf, bf), (wb, bb)):
            z = _dot(lrb, w.astype(BF16), NN) + b
            outs.append((jnp.minimum(z, 0.0) - jnp.log(1.0 + jnp.exp(-jnp.abs(z)))) / GLA_GATE_NORM)
        return outs

    return rowwise(fn, [plr], [wf, wb, bf, bb], [(n, F32), (n, F32)], [], 256, name)


def gate_bwd(plr, dgf, dgb, wf, wb, bf, bb, name):
    n = wf.shape[1]

    def fn(lr, dgf, dgb, wf, wb, bf, bb):
        lrb = lr.astype(BF16)
        dlr = jnp.zeros(lr.shape, F32)
        res = []
        for w, b, dg in ((wf, bf, dgf), (wb, bb, dgb)):
            wb16 = w.astype(BF16)
            z = _dot(lrb, wb16, NN) + b
            dz = dg * _sig(-z) / GLA_GATE_NORM
            dzb = dz.astype(BF16)
            dlr = dlr + _dot(dzb, wb16, NT)
            res += [_dot(lrb, dzb, TN), jnp.sum(dz, axis=0, keepdims=True)]
        return [dlr] + res

    return rowwise(fn, [plr, dgf, dgb], [wf, wb, bf, bb], [(128, BF16)],
                   [(128, n), (1, n), (128, n), (1, n)], 256, name)


def _tri_dot(tri_b, x):
    x1 = x.astype(BF16)
    r1 = x - x1.astype(F32)
    x2 = r1.astype(BF16)
    x3 = (r1 - x2.astype(F32)).astype(BF16)
    return _dot(tri_b, x1, NN) + _dot(tri_b, x2, NN) + _dot(tri_b, x3, NN)


def gla_fwd(pqk, pv, gl, s0, heads, reverse, name, o_add=None):
    T = pqk.shape[0]
    dk = pqk.shape[1] // (2 * heads)
    dv = pv.shape[1] // heads
    C = GLA_CHUNK
    nc = T // C
    qscale = dk ** -0.5

    def body(*refs):
        if o_add is None:
            q_ref, k_ref, v_ref, g_ref, s0_ref, o_ref, st_ref, sf_ref, S = refs
            oa_ref = None
        else:
            q_ref, k_ref, v_ref, g_ref, s0_ref, oa_ref, o_ref, st_ref, sf_ref, S = refs
        n = pl.program_id(0)

        @pl.when(n == 0)
        def _():
            S[...] = s0_ref[...]

        r = lax.broadcasted_iota(jnp.int32, (C, C), 0)
        c = lax.broadcasted_iota(jnp.int32, (C, C), 1)
        tri = (r <= c) if reverse else (r >= c)
        trib = tri.astype(BF16)
        ga, qa, ka, va = g_ref[...], q_ref[...], k_ref[...], v_ref[...]
        sts = [S[h] for h in range(heads)]
        H = range(heads)
        gs = [ga[:, h * dk:(h + 1) * dk] for h in H]
        bs = [_tri_dot(trib, g) for g in gs]
        bls = [jnp.sum(g, axis=0, keepdims=True) for g in gs]
        mid = lax.broadcasted_iota(jnp.int32, (C, 1), 0) == C // 2
        bms = [jnp.sum(jnp.where(mid, b, 0.0), axis=0, keepdims=True) for b in bs]
        vs = [va[:, h * dv:(h + 1) * dv].astype(BF16) for h in H]
        qs = [qa[:, h * dk:(h + 1) * dk].astype(F32) * qscale for h in H]
        qes = [(qs[h] * jnp.exp(bs[h])).astype(BF16) for h in H]
        qms = [(qs[h] * jnp.exp(bs[h] - bms[h])).astype(BF16) for h in H]
        kms = [(ka[:, h * dk:(h + 1) * dk].astype(F32) * jnp.exp(bms[h] - bs[h])).astype(BF16) for h in H]
        kls = [(ka[:, h * dk:(h + 1) * dk].astype(F32) * jnp.exp(bls[h] - bs[h])).astype(BF16) for h in H]
        inter = [_dot(qes[h], sts[h].astype(BF16), NT) for h in H]
        upd = [_dot(vs[h], kls[h], TN) for h in H]
        As = [jnp.where(tri, _dot(qms[h], kms[h], NT), 0.0).astype(BF16) for h in H]
        outs = [inter[h] + _dot(As[h], vs[h], NN) for h in H]
        news = [sts[h] * jnp.exp(bls[h]) + upd[h] for h in H]
        o = jnp.concatenate(outs, axis=1)
        if oa_ref is not None:
            o = o + oa_ref[...]
        o_ref[...] = o
        for h in range(heads):
            st_ref[0, h] = sts[h]
            S[h] = news[h]

        @pl.when(n == nc - 1)
        def _():
            for h in range(heads):
                sf_ref[h] = news[h]

    def ci(n):
        return (nc - 1 - n) if reverse else n

    specs = [
        pl.BlockSpec((C, heads * dk), lambda n: (ci(n), 0)),
        pl.BlockSpec((C, heads * dk), lambda n: (ci(n), 1)),
        pl.BlockSpec((C, heads * dv), lambda n: (ci(n), 0)),
        pl.BlockSpec((C, heads * dk), lambda n: (ci(n), 0)),
        pl.BlockSpec((heads, dv, dk), lambda n: (0, 0, 0)),
    ]
    ins = [pqk, pqk, pv, gl, s0]
    if o_add is not None:
        specs.append(pl.BlockSpec((C, heads * dv), lambda n: (ci(n), 0)))
        ins.append(o_add)
    return pl.pallas_call(
        body,
        grid=(nc,),
        in_specs=specs,
        out_specs=[
            pl.BlockSpec((C, heads * dv), lambda n: (ci(n), 0)),
            pl.BlockSpec((1, heads, dv, dk), lambda n: (ci(n), 0, 0, 0)),
            pl.BlockSpec((heads, dv, dk), lambda n: (0, 0, 0)),
        ],
        out_shape=[
            jax.ShapeDtypeStruct((T, heads * dv), F32),
            jax.ShapeDtypeStruct((nc, heads, dv, dk), F32),
            jax.ShapeDtypeStruct((heads, dv, dk), F32),
        ],
        scratch_shapes=[pltpu.VMEM((heads, dv, dk), F32)],
        compiler_params=_cp(("arbitrary",)),
        name=name,
    )(*ins)


def gla_bwd(pqk, pv, gl, states, do, dsf, heads, reverse, name, acc=None):
    T = pqk.shape[0]
    dk = pqk.shape[1] // (2 * heads)
    dv = pv.shape[1] // heads
    C = GLA_CHUNK
    nc = T // C
    qscale = dk ** -0.5

    def body(*refs):
        if acc is None:
            q_ref, k_ref, v_ref, g_ref, st_ref, do_ref, dsf_ref, dq_ref, dk_ref, dv_ref, dg_ref, ds0_ref, dS = refs
            aq = ak = av = None
        else:
            (q_ref, k_ref, v_ref, g_ref, st_ref, do_ref, dsf_ref, aq, ak, av,
             dq_ref, dk_ref, dv_ref, dg_ref, ds0_ref, dS) = refs
        n = pl.program_id(0)

        @pl.when(n == 0)
        def _():
            dS[...] = dsf_ref[...]

        r = lax.broadcasted_iota(jnp.int32, (C, C), 0)
        c = lax.broadcasted_iota(jnp.int32, (C, C), 1)
        tri = (r <= c) if reverse else (r >= c)
        tri_t = (r >= c) if reverse else (r <= c)
        trib, tritb = tri.astype(BF16), tri_t.astype(BF16)
        ga, qa, ka, va, doa = g_ref[...], q_ref[...], k_ref[...], v_ref[...], do_ref[...]
        sts = [st_ref[0, h] for h in range(heads)]
        dsts = [dS[h] for h in range(heads)]
        H = range(heads)
        gs = [ga[:, h * dk:(h + 1) * dk] for h in H]
        bs = [_tri_dot(trib, g) for g in gs]
        bls = [jnp.sum(g, axis=0, keepdims=True) for g in gs]
        mid = lax.broadcasted_iota(jnp.int32, (C, 1), 0) == C // 2
        bms = [jnp.sum(jnp.where(mid, b, 0.0), axis=0, keepdims=True) for b in bs]
        ebs = [jnp.exp(b) for b in bs]
        embs = [jnp.exp(bs[h] - bms[h]) for h in H]
        enbs = [jnp.exp(bms[h] - bs[h]) for h in H]
        elbs = [jnp.exp(bls[h] - bs[h]) for h in H]
        ebls = [jnp.exp(bl) for bl in bls]
        vbs = [va[:, h * dv:(h + 1) * dv].astype(BF16) for h in H]
        dobs = [doa[:, h * dv:(h + 1) * dv].astype(BF16) for h in H]
        qs = [qa[:, h * dk:(h + 1) * dk].astype(F32) * qscale for h in H]
        qes = [qs[h] * ebs[h] for h in H]
        qms = [qs[h] * embs[h] for h in H]
        kms = [ka[:, h * dk:(h + 1) * dk].astype(F32) * enbs[h] for h in H]
        kls = [ka[:, h * dk:(h + 1) * dk].astype(F32) * elbs[h] for h in H]
        qebs = [a.astype(BF16) for a in qes]
        qmbs = [a.astype(BF16) for a in qms]
        kmbs = [a.astype(BF16) for a in kms]
        klbs = [a.astype(BF16) for a in kls]
        stbs = [a.astype(BF16) for a in sts]
        dstbs = [a.astype(BF16) for a in dsts]
        ps = [jnp.where(tri, _dot(qmbs[h], kmbs[h], NT), 0.0).astype(BF16) for h in H]
        dps = [jnp.where(tri, _dot(dobs[h], vbs[h], NT), 0.0).astype(BF16) for h in H]
        dqes = [_dot(dobs[h], stbs[h], NN) for h in H]
        dkls = [_dot(vbs[h], dstbs[h], NN) for h in H]
        dv1 = [_dot(klbs[h], dstbs[h], NT) for h in H]
        dsn1 = [_dot(dobs[h], qebs[h], TN) for h in H]
        dqms = [_dot(dps[h], kmbs[h], NN) for h in H]
        dkms = [_dot(dps[h], qmbs[h], TN) for h in H]
        dvs = [_dot(ps[h], dobs[h], TN) + dv1[h] for h in H]
        dbls = [ebls[h] * jnp.sum(dsts[h] * sts[h], axis=0, keepdims=True)
                + jnp.sum(dkls[h] * kls[h], axis=0, keepdims=True) for h in H]
        dsns = [dsn1[h] + dsts[h] * ebls[h] for h in H]
        dqs = [(dqes[h] * ebs[h] + dqms[h] * embs[h]) * qscale for h in H]
        dks = [dkms[h] * enbs[h] + dkls[h] * elbs[h] for h in H]
        dbs = [dqes[h] * qes[h] + dqms[h] * qms[h] - dkms[h] * kms[h] - dkls[h] * kls[h] for h in H]
        dgs = [_tri_dot(tritb, dbs[h]) + dbls[h] for h in H]
        dq, dkk, dvv = (jnp.concatenate(a, axis=1) for a in (dqs, dks, dvs))
        if aq is not None:
            dq = dq + aq[...].astype(F32)
            dkk = dkk + ak[...].astype(F32)
            dvv = dvv + av[...].astype(F32)
        dq_ref[...] = dq.astype(dq_ref.dtype)
        dk_ref[...] = dkk.astype(dk_ref.dtype)
        dv_ref[...] = dvv.astype(dv_ref.dtype)
        dg_ref[...] = jnp.concatenate(dgs, axis=1)
        for h in range(heads):
            dS[h] = dsns[h]

        @pl.when(n == nc - 1)
        def _():
            for h in range(heads):
                ds0_ref[h] = dsns[h]

    def ci(n):
        return n if reverse else (nc - 1 - n)

    kspec = pl.BlockSpec((C, heads * dk), lambda n: (ci(n), 0))
    vspec = pl.BlockSpec((C, heads * dv), lambda n: (ci(n), 0))
    sspec = pl.BlockSpec((heads, dv, dk), lambda n: (0, 0, 0))
    specs = [
        kspec,
        pl.BlockSpec((C, heads * dk), lambda n: (ci(n), 1)),
        vspec,
        kspec,
        pl.BlockSpec((1, heads, dv, dk), lambda n: (ci(n), 0, 0, 0)),
        vspec,
        sspec,
    ]
    ins = [pqk, pqk, pv, gl, states, do, dsf]
    odt = F32 if acc is None else BF16
    if acc is not None:
        specs += [kspec, kspec, vspec]
        ins += list(acc)
    return pl.pallas_call(
        body,
        grid=(nc,),
        in_specs=specs,
        out_specs=[kspec, kspec, vspec, kspec, sspec],
        out_shape=[
            jax.ShapeDtypeStruct((T, heads * dk), odt),
            jax.ShapeDtypeStruct((T, heads * dk), odt),
            jax.ShapeDtypeStruct((T, heads * dv), odt),
            jax.ShapeDtypeStruct((T, heads * dk), F32),
            jax.ShapeDtypeStruct((heads, dv, dk), F32),
        ],
        scratch_shapes=[pltpu.VMEM((heads, dv, dk), F32)],
        compiler_params=_cp(("arbitrary",)),
        name=name,
    )(*ins)


def gla_out_fwd(og, prb, gn, heads, name):
    dv = og.shape[1] // heads

    def fn(og, rb, gn):
        outs = []
        for h in range(heads):
            oh = og[:, h * dv:(h + 1) * dv]
            outs.append(oh * _rstd(oh) * gn)
        y = jnp.concatenate(outs, axis=1)
        return y * (rb * _sig(rb))

    return rowwise(fn, [og, prb], [gn], [(og.shape[1], BF16)], [], 256, name)[0]


def gla_out_bwd(og, prb, du, gn, heads, name):
    dv = og.shape[1] // heads

    def fn(og, rb, du, gn):
        sg = _sig(rb)
        silu = rb * sg
        dsilu = sg * (1.0 + rb * (1.0 - sg))
        dog, ys = [], []
        dgn = jnp.zeros((1, dv), F32)
        for h in range(heads):
            sl = slice(h * dv, (h + 1) * dv)
            oh = og[:, sl]
            r = _rstd(oh)
            n = oh * r
            ys.append(n * gn)
            dy = du[:, sl] * silu[:, sl]
            dgn = dgn + jnp.sum(dy * n, axis=0, keepdims=True)
            dn = dy * gn
            dog.append(r * (dn - n * jnp.mean(dn * n, axis=-1, keepdims=True)))
        y = jnp.concatenate(ys, axis=1)
        return jnp.concatenate(dog, axis=1), du * y * dsilu, dgn

    return rowwise(fn, [og, prb, du], [gn], [(og.shape[1], F32), (og.shape[1], BF16)], [(1, dv)], 256, name)


def conv_swiglu_fwd(ua_, ug_, cw, cb, name):
    T, F = ua_.shape
    tt = min(1024, T)
    tc = _pick(F, (512, 256, 128))
    nt_, ncol = T // tt, F // tc
    H = CONV_HALO
    n = tt + 2 * H

    def body(ua, uap, uan, ug, ugp, ugn, wa, wg, ba, bg, f_ref):
        i = pl.program_id(1)
        keep_p = (i > 0).astype(F32)
        keep_n = (i < nt_ - 1).astype(F32)
        res = []
        for m, p, nx, w, b in ((ua, uap, uan, wa, ba), (ug, ugp, ugn, wg, bg)):
            x = jnp.concatenate([p[...] * keep_p, m[...], nx[...] * keep_n], axis=0)
            down, up = pltpu.roll(x, 1, 0)[H:H + tt], pltpu.roll(x, n - 1, 0)[H:H + tt]
            res.append(w[0] * down + w[1] * x[H:H + tt] + w[2] * up + b[...])
        a, g = res
        f_ref[...] = (a * _sig_tanh(a) * g).astype(f_ref.dtype)

    wspec = lambda off: pl.BlockSpec((3, 1, tc), lambda j, i: (0, 0, j + off))
    bspec = lambda off: pl.BlockSpec((1, tc), lambda j, i: (0, j + off))
    return pl.pallas_call(
        body,
        grid=(ncol, nt_),
        in_specs=conv_halo_specs(T, tt, tc, 0) + conv_halo_specs(T, tt, tc, 0)
        + [wspec(0), wspec(ncol), bspec(0), bspec(ncol)],
        out_specs=pl.BlockSpec((tt, tc), lambda j, i: (i, j)),
        out_shape=jax.ShapeDtypeStruct((T, F), BF16),
        compiler_params=_cp(("parallel", "parallel")),
        name=name,
    )(ua_, ua_, ua_, ug_, ug_, ug_, cw, cw, cb, cb)


CONV_HALO = 16


def conv_halo_specs(T, tt, tc, off):
    r = tt // CONV_HALO
    last = T // CONV_HALO - 1
    return [
        pl.BlockSpec((tt, tc), lambda j, i: (i, j + off)),
        pl.BlockSpec((CONV_HALO, tc), lambda j, i: (jnp.maximum(i * r - 1, 0), j + off)),
        pl.BlockSpec((CONV_HALO, tc), lambda j, i: (jnp.minimum((i + 1) * r, last), j + off)),
    ]


def conv_swiglu_bwd_fused(ua_, ug_, cw, cb, df, name):
    T, F = ua_.shape
    tt = min(512, T)
    tc = _pick(F, (512, 256, 128))
    nt_, ncol = T // tt, F // tc
    H = CONV_HALO
    n = tt + 2 * H

    def body(ua, uap, uan, ug, ugp, ugn, dm, dp_, dn, wa, wg, ba, bg, dua_ref, dug_ref, dwa, dwg, dba, dbg):
        i = pl.program_id(1)

        @pl.when(i == 0)
        def _():
            for r in (dwa, dwg, dba, dbg):
                r[...] = jnp.zeros_like(r)

        keep_p = (i > 0).astype(F32)
        keep_n = (i < nt_ - 1).astype(F32)

        def ext(m, p, nx):
            return jnp.concatenate([p[...].astype(F32) * keep_p, m[...].astype(F32), nx[...].astype(F32) * keep_n],
                                   axis=0)

        d = ext(dm, dp_, dn)
        conv, parts = [], []
        for m, p, nx, w, b in ((ua, uap, uan, wa, ba), (ug, ugp, ugn, wg, bg)):
            x = ext(m, p, nx)
            down, up = pltpu.roll(x, 1, 0), pltpu.roll(x, n - 1, 0)
            parts.append((down, x, up))
            conv.append(w[0] * down + w[1] * x + w[2] * up + b[...])
        a, g = conv
        sg = _sig_tanh(a)
        da = d * g * sg * (1.0 + a * (1.0 - sg))
        dg = d * a * sg
        for dd, w, (down, x, up), o_ref, dw, db in ((da, wa, parts[0], dua_ref, dwa, dba),
                                                    (dg, wg, parts[1], dug_ref, dwg, dbg)):
            du = w[0] * pltpu.roll(dd, n - 1, 0) + w[1] * dd + w[2] * pltpu.roll(dd, 1, 0)
            o_ref[...] = du[H:H + tt].astype(o_ref.dtype)
            ddm = dd[H:H + tt]
            dw[0] += jnp.sum(ddm * down[H:H + tt], axis=0, keepdims=True)
            dw[1] += jnp.sum(ddm * x[H:H + tt], axis=0, keepdims=True)
            dw[2] += jnp.sum(ddm * up[H:H + tt], axis=0, keepdims=True)
            db[...] += jnp.sum(ddm, axis=0, keepdims=True)

    wspec = lambda off: pl.BlockSpec((3, 1, tc), lambda j, i: (0, 0, j + off))
    bspec = lambda off: pl.BlockSpec((1, tc), lambda j, i: (0, j + off))
    tile = pl.BlockSpec((tt, tc), lambda j, i: (i, j))
    return pl.pallas_call(
        body,
        grid=(ncol, nt_),
        in_specs=conv_halo_specs(T, tt, tc, 0) + conv_halo_specs(T, tt, tc, 0) + conv_halo_specs(T, tt, tc, 0)
        + [wspec(0), wspec(ncol), bspec(0), bspec(ncol)],
        out_specs=[tile, tile, wspec(0), wspec(0), bspec(0), bspec(0)],
        out_shape=[
            jax.ShapeDtypeStruct((T, F), BF16), jax.ShapeDtypeStruct((T, F), BF16),
            jax.ShapeDtypeStruct((3, 1, F), F32), jax.ShapeDtypeStruct((3, 1, F), F32),
            jax.ShapeDtypeStruct((1, F), F32), jax.ShapeDtypeStruct((1, F), F32),
        ],
        compiler_params=_cp(("parallel", "arbitrary")),
        name=name,
    )(ua_, ua_, ua_, ug_, ug_, ug_, df, df, df, cw, cw, cb, cb)


def rope_tables(n, hd):
    rows = n // GRID_W
    row = jnp.repeat(jnp.arange(rows), GRID_W)
    col = jnp.tile(jnp.arange(GRID_W), rows)
    n_freq = hd // 4
    inv = ROPE_THETA ** (-jnp.arange(n_freq, dtype=F32) / n_freq)
    ang = jnp.concatenate([row[:, None] * inv, col[:, None] * inv], axis=-1)
    cos, sin = jnp.cos(ang), jnp.sin(ang)
    return jnp.concatenate([cos, cos], axis=-1), jnp.concatenate([-sin, sin], axis=-1)


def local_step(x, ctx, tgt, mod, modc, W, P):
    T, D = x.shape
    L = ctx.shape[0]
    hd, hq, hkv, gh = P["hd"], P["hq"], P["hkv"], P["gh"]
    sh1, sc1, g1, sh2, sc2, g2 = mod
    csh1, csc1 = modc
    kvw = hkv * hd
    gkw = W["gqk"].shape[1] // 2
    gdv = D // gh
    gdk = gkw // gh

    h = modulate_fwd(x, P["g_mix"], sh1, sc1, "mod1")
    hc = modulate_fwd(ctx, P["g_mix"], csh1, csc1, "mod1_ctx")
    pq = matmul(h, W["q"], "nn", F32, "proj_q")
    pkv = matmul(h, W["kv"], "nn", F32, "proj_kv")
    pgqk = matmul(h, W["gqk"], "nn", F32, "proj_gqk")
    pgv = matmul(h, W["gv"], "nn", F32, "proj_gv")
    prb = matmul(h, W["rb"], "nn", F32, "proj_rb")
    plr = matmul(h, W["lr"], "nn", F32, "proj_lr")
    pgab = matmul(h, W["gab"], "nn", F32, "proj_gab")
    pkv_c = matmul(hc, W["kv"], "nn", F32, "proj_kv_ctx")
    pgqk_c = matmul(hc, W["gqk"], "nn", F32, "proj_gqk_ctx")
    pgv_c = matmul(hc, W["gv"], "nn", F32, "proj_gv_ctx")
    plr_c = matmul(hc, W["lr"], "nn", F32, "proj_lr_ctx")

    cosf, sinf = rope_tables(T, hd)
    one_c, zero_c = jnp.ones((L, hd), F32), jnp.zeros((L, hd), F32)
    qr = norm_rope_fwd(pq, hq * hd, 0, P["q_norm"], cosf, sinf, hd, "qnorm")
    kr = norm_rope_fwd(pkv, kvw, 0, P["k_norm"], cosf, sinf, hd, "knorm")
    kcr = norm_rope_fwd(pkv_c, kvw, 0, P["k_norm"], one_c, zero_c, hd, "knorm_ctx")
    sink = P["attn_sink"].reshape(hq, 1, 1)
    o_attn, lse, lse_row = attention_fwd(qr, kr, pkv, kcr, pkv_c, sink, hkv, hd, "attn_fwd")

    gf, gb = gate_fwd(plr, W["gate_f"], W["gate_b"], P["b_gate_f"], P["b_gate_b"], "gates")
    gfc, gbc = gate_fwd(plr_c, W["gate_f"], W["gate_b"], P["b_gate_f"], P["b_gate_b"], "gates_ctx")
    zero_state = jnp.zeros((gh, gdv, gdk), F32)
    _, st_cf, s_cf = gla_fwd(pgqk_c, pgv_c, gfc, zero_state, gh, False, "gla_ctx_f")
    _, st_cb, s_cb = gla_fwd(pgqk_c, pgv_c, gbc, zero_state, gh, True, "gla_ctx_b")
    of, st_f, _ = gla_fwd(pgqk, pgv, gf, s_cf, gh, False, "gla_f")
    og, st_b, _ = gla_fwd(pgqk, pgv, gb, s_cb, gh, True, "gla_b", o_add=of)
    ug = gla_out_fwd(og, prb, P["gla_norm"], gh, "gla_out")

    ya = matmul(o_attn, W["attn_o"], "nn", F32, "attn_o")
    yg = matmul(ug, W["gla_o"], "nn", F32, "gla_o")

    def merge_fn(ya, yg, ga, gb_):
        return _sig(ga) * ya + _sig(gb_) * yg

    z = rowwise(merge_fn, [ya, yg, (pgab, D, 0), (pgab, D, 1)], [], [(D, BF16)], [], 256, "merge")[0]
    mo = matmul(z, W["out"], "nn", F32, "w_out")

    def res_fn(x, mo, g1, gffn, sh2, sc2):
        x1 = x + g1 * mo
        return x1, x1 * _rstd(x1) * gffn * (1.0 + sc2) + sh2

    x1, h2 = rowwise(res_fn, [x, mo], [g1, P["g_ffn"], sh2, sc2], [(D, F32), (D, BF16)], [], 256, "res_mod2")
    u_a = matmul(h2, W["up_a"], "nn", F32, "w_up_a")
    u_g = matmul(h2, W["up_g"], "nn", F32, "w_up_g")
    cw3 = W["conv_w"].reshape(3, 1, -1)
    f = conv_swiglu_fwd(u_a, u_g, cw3, P["conv_b"], "conv_swiglu")
    fo = matmul(f, W["down"], "nn", F32, "w_down")

    def final_fn(x1, fo, tgt, g2):
        e = x1 + g2 * fo - tgt
        dy = e * (1.0 / D)
        lsum = jnp.sum(jnp.sum(e * e, axis=1, keepdims=True), axis=0, keepdims=True)
        return dy, dy * g2, jnp.broadcast_to(lsum, (1, 128)), jnp.sum(dy * fo, axis=0, keepdims=True)

    dy, dfo, lsum, dg2 = rowwise(final_fn, [x1, fo, tgt], [g2], [(D, F32), (D, BF16)], [(1, 128), (1, D)], 256, "loss")
    df = matmul(dfo, W["down"], "nt", BF16, "d_f")
    dw_down = matmul(f, dfo, "tn", BF16, "dw_down")
    du_a, du_g, dcw_a, dcw_g, dcb_a, dcb_g = conv_swiglu_bwd_fused(u_a, u_g, cw3, P["conv_b"], df, "conv_swiglu_bwd")
    dh2 = matmul(du_a, W["up_a"], "nt", F32, "d_h2_a")
    dh2 = matmul(du_g, W["up_g"], "nt", F32, "d_h2_g", add=dh2)
    dw_up = [matmul(h2, du_a, "tn", BF16, "dw_up_a"), matmul(h2, du_g, "tn", BF16, "dw_up_g")]

    def mod2_bwd_fn(x1, dh, dy, mo, gffn, sc2, g1):
        r = _rstd(x1)
        n = x1 * r
        dyy = dh * (1.0 + sc2)
        dn = dyy * gffn
        dx1 = dy + r * (dn - n * jnp.mean(dn * n, axis=-1, keepdims=True))
        s0 = lambda a: jnp.sum(a, axis=0, keepdims=True)
        return dx1, dx1 * g1, s0(dyy * n), s0(dh), s0(dh * n * gffn), s0(dx1 * mo)

    dx1, dmo, dg_ffn, dsh2, dsc2, dg1 = rowwise(
        mod2_bwd_fn, [x1, dh2, dy, mo], [P["g_ffn"], sc2, g1], [(D, F32), (D, BF16)], [(1, D)] * 4, 256, "mod2_bwd")
    dz = matmul(dmo, W["out"], "nt", F32, "d_z")
    dw_out = matmul(z, dmo, "tn", BF16, "dw_out")

    def merge_bwd_fn(dz, ya, yg, ga, gb_):
        sa, sb = _sig(ga), _sig(gb_)
        return dz * sa, dz * sb, jnp.concatenate([dz * ya * sa * (1.0 - sa), dz * yg * sb * (1.0 - sb)], axis=1)

    dya, dyg, dpgab = rowwise(merge_bwd_fn, [dz, ya, yg, (pgab, D, 0), (pgab, D, 1)], [],
                              [(D, BF16), (D, BF16), (2 * D, BF16)], [], 256, "merge_bwd")
    do_attn = matmul(dya, W["attn_o"], "nt", BF16, "d_oattn")
    dw_attn_o = matmul(o_attn, dya, "tn", BF16, "dw_attn_o")
    dug = matmul(dyg, W["gla_o"], "nt", F32, "d_ug")
    dw_gla_o = matmul(ug, dyg, "tn", BF16, "dw_gla_o")
    dog, dprb, dgn = gla_out_bwd(og, prb, dug, P["gla_norm"], gh, "gla_out_bwd")

    dq1, dk1, dv1, dgf, ds_cf = gla_bwd(pgqk, pgv, gf, st_f, dog, zero_state, gh, False, "gla_f_bwd")
    dgq, dgk, dpgv, dgb, ds_cb = gla_bwd(pgqk, pgv, gb, st_b, dog, zero_state, gh, True, "gla_b_bwd",
                                          acc=(dq1, dk1, dv1))
    dpgqk = jnp.concatenate([dgq, dgk], axis=1)
    zero_do = jnp.zeros((L, gh * gdv), F32)
    cq1, ck1, cv1, dgfc, _ = gla_bwd(pgqk_c, pgv_c, gfc, st_cf, zero_do, ds_cf, gh, False, "gla_ctx_f_bwd")
    cq, ck, dpgv_c, dgbc, _ = gla_bwd(pgqk_c, pgv_c, gbc, st_cb, zero_do, ds_cb, gh, True, "gla_ctx_b_bwd",
                                      acc=(cq1, ck1, cv1))
    dpgqk_c = jnp.concatenate([cq, ck], axis=1)
    dplr, dwgf, dbgf, dwgb, dbgb = gate_bwd(plr, dgf, dgb, W["gate_f"], W["gate_b"], P["b_gate_f"], P["b_gate_b"], "gates_bwd")
    dplr_c, dwgf_c, dbgf_c, dwgb_c, dbgb_c = gate_bwd(plr_c, dgfc, dgbc, W["gate_f"], W["gate_b"], P["b_gate_f"],
                                                      P["b_gate_b"], "gates_ctx_bwd")

    dqr, dkc_r, dvc, dsink, dr_row = attention_bwd_q(qr, kr, pkv, kcr, pkv_c, sink, do_attn, o_attn, lse, hkv, hd,
                                                     "attn_bwd_q")
    dkr, dv = attention_bwd_kv(qr, kr, pkv, do_attn, lse_row, dr_row, hkv, hd, "attn_bwd_kv")
    dpq, dqn = norm_rope_bwd(pq, hq * hd, 0, dqr, P["q_norm"], cosf, sinf, hd, "qnorm_bwd")
    dpk, dkn = norm_rope_bwd(pkv, kvw, 0, dkr, P["k_norm"], cosf, sinf, hd, "knorm_bwd")
    dpk_c, dkn_c = norm_rope_bwd(pkv_c, kvw, 0, dkc_r, P["k_norm"], one_c, zero_c, hd, "knorm_ctx_bwd")
    dpkv = jnp.concatenate([dpk, dv], axis=1)
    dpkv_c = jnp.concatenate([dpk_c, dvc.astype(BF16)], axis=1)

    dw_q = matmul(h, dpq, "tn", BF16, "dw_q")
    dw_kv = matmul(h, dpkv, "tn", BF16, "dw_kv", add=matmul(hc, dpkv_c, "tn", F32, "dw_kv_ctx"))
    dw_gqk = matmul(h, dpgqk, "tn", BF16, "dw_gqk", add=matmul(hc, dpgqk_c, "tn", F32, "dw_gqk_ctx"))
    dw_gv = matmul(h, dpgv, "tn", BF16, "dw_gv", add=matmul(hc, dpgv_c, "tn", F32, "dw_gv_ctx"))
    dw_rb = matmul(h, dprb, "tn", BF16, "dw_rb")
    dw_lr = matmul(h, dplr, "tn", BF16, "dw_lr", add=matmul(hc, dplr_c, "tn", F32, "dw_lr_ctx"))
    dw_gab = matmul(h, dpgab, "tn", BF16, "dw_gab")
    lrw = P["lowrank"]
    dw_in = [dw_q, dw_kv, dw_gqk, dw_gv, dw_rb, dw_lr[:, :2 * lrw], dw_gab]

    dh = matmul(dpq, W["q"], "nt", F32, "dh_q")
    dh = matmul(dpkv, W["kv"], "nt", F32, "dh_kv", add=dh)
    dh = matmul(dpgqk, W["gqk"], "nt", F32, "dh_gqk", add=dh)
    dh = matmul(dpgv, W["gv"], "nt", F32, "dh_gv", add=dh)
    dh = matmul(dprb, W["rb"], "nt", F32, "dh_rb", add=dh)
    dh = matmul(dplr, W["lr"], "nt", F32, "dh_lr", add=dh)
    dh = matmul(dpgab, W["gab"], "nt", F32, "dh_gab", add=dh)
    dhc = matmul(dpkv_c, W["kv"], "nt", F32, "dhc_kv")
    dhc = matmul(dpgqk_c, W["gqk"], "nt", F32, "dhc_gqk", add=dhc)
    dhc = matmul(dpgv_c, W["gv"], "nt", F32, "dhc_gv", add=dhc)
    dhc = matmul(dplr_c, W["lr"], "nt", F32, "dhc_lr", add=dhc)

    def mod1_bwd_fn(x, dh, dres, g, sc):
        r = _rstd(x)
        n = x * r
        dyy = dh * (1.0 + sc)
        dn = dyy * g
        dx = dres + r * (dn - n * jnp.mean(dn * n, axis=-1, keepdims=True))
        s0 = lambda a: jnp.sum(a, axis=0, keepdims=True)
        return dx, s0(dyy * n), s0(dh), s0(dh * n * g)

    grad_x, dgmix, dsh1, dsc1 = rowwise(mod1_bwd_fn, [x, dh, dx1], [P["g_mix"], sc1], [(D, F32)], [(1, D)] * 3,
                                        256, "mod1_bwd")
    _, dgmix_c, dcsh1, dcsc1 = rowwise(mod1_bwd_fn, [ctx, dhc, jnp.zeros_like(ctx)], [P["g_mix"], csc1], [(D, F32)],
                                       [(1, D)] * 3, 128, "mod1_ctx_bwd")

    zD = jnp.zeros((1, D), F32)
    grads = dict(
        w_in=dw_in, w_attn_o=dw_attn_o, w_gla_o=dw_gla_o, w_out=dw_out, w_up=dw_up, w_down=dw_down,
        dmod_x=jnp.concatenate([dsh1, dsc1, dg1, dsh2, dsc2, dg2], axis=1),
        dmod_c=jnp.concatenate([dcsh1, dcsc1, zD, zD, zD, zD], axis=1),
        g_mix=dgmix + dgmix_c, q_norm=dqn, k_norm=dkn + dkn_c, attn_sink=dsink.reshape(1, hq),
        w_gate_f=(dwgf + dwgf_c)[:lrw], b_gate_f=dbgf + dbgf_c,
        w_gate_b=(dwgb + dwgb_c)[lrw:2 * lrw], b_gate_b=dbgb + dbgb_c,
        gla_norm=dgn, g_ffn=dg_ffn,
        conv_w=jnp.concatenate([dcw_a, dcw_g], axis=2).reshape(3, -1),
        conv_b=jnp.concatenate([dcb_a, dcb_g], axis=1),
    )
    return lsum[0, 0], grad_x, grads


SMALL_REPL = ("c_ctx", "b_mod", "g_mix", "q_norm", "k_norm", "attn_sink", "b_gate_f", "b_gate_b", "gla_norm", "g_ffn",
              "conv_b")
SMALL_SHARD = ("w_gate_f", "w_gate_b", "conv_w")
ORDER = ("c_ctx", "w_mod", "b_mod", "g_mix", "w_in", "q_norm", "k_norm", "attn_sink", "w_gate_f", "b_gate_f",
         "w_gate_b", "b_gate_b", "gla_norm", "w_attn_o", "w_gla_o", "w_out", "g_ffn", "w_up", "conv_w", "conv_b",
         "w_down")


def kernel(x, c, ctx, c_ctx, w_mod, b_mod, g_mix, w_in, q_norm, k_norm, attn_sink, w_gate_f, b_gate_f, w_gate_b, b_gate_b, gla_norm, w_attn_o, w_gla_o, w_out, g_ffn, w_up, conv_w, conv_b, w_down, loss_target, m_c_ctx, m_w_mod, m_b_mod, m_g_mix, m_w_in, m_q_norm, m_k_norm, m_attn_sink, m_w_gate_f, m_b_gate_f, m_w_gate_b, m_b_gate_b, m_gla_norm, m_w_attn_o, m_w_gla_o, m_w_out, m_g_ffn, m_w_up, m_conv_w, m_conv_b, m_w_down, v_c_ctx, v_w_mod, v_b_mod, v_g_mix, v_w_in, v_q_norm, v_k_norm, v_attn_sink, v_w_gate_f, v_b_gate_f, v_w_gate_b, v_b_gate_b, v_gla_norm, v_w_attn_o, v_w_gla_o, v_w_out, v_g_ffn, v_w_up, v_conv_w, v_conv_b, v_w_down):
    loc = dict(locals())
    Wt = {n: loc[n] for n in ORDER}
    Mt = {n: loc["m_" + n] for n in ORDER}
    Vt = {n: loc["v_" + n] for n in ORDER}
    me = 4 * lax.axis_index("x") + 2 * lax.axis_index("y") + lax.axis_index("c")

    D = x.shape[-1]
    hd = q_norm.shape[-1]
    hq = attn_sink.shape[-1]
    gdv = gla_norm.shape[-1]
    gh = D // gdv
    gdk = D // 2 // gh
    lrw = w_gate_f.shape[1]
    in_w = NDEV * w_in.shape[-1]
    kvw = (in_w - hq * hd - 2 * gh * gdk - 2 * gh * gdv - 2 * lrw - 2 * D) // 2
    hkv = kvw // hd
    gcols = w_gate_f.shape[-1]
    mcols = w_mod.shape[-1]

    x2, ctx2, tgt2 = x[0], ctx[0], loss_target[0]

    c_all = exchange([jnp.pad(c, ((0, 7), (0, 0)))], True, "gather_c")[0][:, 0, :]
    c9 = jnp.concatenate([c_all, c_ctx[None, :], jnp.zeros((7, D), F32)], axis=0)
    s9 = rowwise(lambda a: a * _sig(a), [c9], [], [(D, F32)], [], 16, "silu_c")[0]
    bias = jnp.broadcast_to(lax.dynamic_slice_in_dim(b_mod, me * mcols, mcols, axis=1), (16, mcols))
    mod_cols = matmul(s9, w_mod[0], "nn", F32, "mod_cols", add=bias)
    mod_all = exchange([mod_cols], True, "gather_mod")[0]
    mod_all = jnp.transpose(mod_all, (1, 0, 2)).reshape(16, NDEV * mcols)
    mod_me = lax.dynamic_slice_in_dim(mod_all, me, 1, axis=0)
    mod = [mod_me[:, i * D:(i + 1) * D] for i in range(6)]
    modc = [mod_all[8:9, i * D:(i + 1) * D] for i in range(2)]

    o3 = jnp.stack([w_attn_o[0], w_gla_o[0], w_out[0]]).astype(BF16)
    small_w = pack([w_gate_f[0], w_gate_b[0], conv_w[0]])
    g_in, g_o3, g_up, g_down, g_small = gather_two_level(
        [w_in[0].astype(BF16), o3, w_up[0].astype(BF16), w_down[0].astype(BF16), small_w], "gather_w")
    seg = segments_from_blocks(g_in, [hq * hd, 2 * kvw, 2 * gh * gdk, gh * gdv, gh * gdv, 2 * lrw, 2 * D])
    small_parts = [unpack(g_small[j], [w_gate_f[0].shape, w_gate_b[0].shape, conv_w[0].shape]) for j in range(NDEV)]
    wgf = jnp.concatenate([p[0] for p in small_parts], axis=1)
    wgb = jnp.concatenate([p[1] for p in small_parts], axis=1)
    cw_full = jnp.concatenate([p[2] for p in small_parts], axis=1)
    o3f = [g_o3[:, i].reshape(-1, D) for i in range(3)]
    W = dict(
        q=seg[0], kv=seg[1], gqk=seg[2], gv=seg[3], rb=seg[4],
        lr=jnp.pad(seg[5], ((0, 0), (0, 128 - 2 * lrw))), gab=seg[6],
        gate_f=jnp.pad(wgf, ((0, 128 - lrw), (0, 0))),
        gate_b=jnp.pad(wgb, ((lrw, 128 - 2 * lrw), (0, 0))),
        attn_o=o3f[0], gla_o=o3f[1], out=o3f[2],
        up_a=jnp.concatenate([g_up[j] for j in range(NDEV // 2)], axis=1),
        up_g=jnp.concatenate([g_up[j] for j in range(NDEV // 2, NDEV)], axis=1),
        down=g_down.reshape(-1, D),
        conv_w=cw_full,
    )
    P = dict(hd=hd, hq=hq, hkv=hkv, gh=gh, lowrank=lrw, g_mix=g_mix, q_norm=q_norm, k_norm=k_norm, attn_sink=attn_sink,
             b_gate_f=b_gate_f, b_gate_b=b_gate_b, gla_norm=gla_norm, g_ffn=g_ffn, conv_b=conv_b)

    lsum, grad_x, G = local_step(x2, ctx2, tgt2, mod, modc, W, P)
    loss = lax.psum(0.5 * lsum / D, ("x", "y", "c"))

    dm = exchange([jnp.concatenate([G["dmod_x"], G["dmod_c"], jnp.zeros((6, 6 * D), F32)], axis=0)], True,
                  "gather_dmod")[0]
    dmc = reduce_parts(dm[:, 1:2, :].reshape(NDEV, 6 * D // 128, 128), "sum_dmod_ctx").reshape(1, 6 * D)
    dM = jnp.concatenate([dm[:, 0, :], dmc, jnp.zeros((7, 6 * D), F32)], axis=0)
    dM_cols = lax.dynamic_slice_in_dim(dM, me * mcols, mcols, axis=1)
    g_w_mod = matmul(s9, dM_cols, "tn", F32, "dw_mod")
    g_b_mod = reduce_parts(dM.reshape(16, 6 * D // 128, 128), "sum_db_mod").reshape(1, 6 * D)
    dsc = matmul(dM_cols[8:16], w_mod[0], "nt", F32, "d_silu_ctx")
    cc = jnp.broadcast_to(c_ctx[None, :], (8, D))

    def dsilu_fn(d, a):
        sg = _sig(a)
        return d * sg * (1.0 + a * (1.0 - sg))

    g_cctx_part = rowwise(dsilu_fn, [dsc, cc], [], [(D, F32)], [], 8, "d_c_ctx")[0][0:1]

    small_names = ("c_ctx", "g_mix", "q_norm", "k_norm", "attn_sink", "b_gate_f", "b_gate_b", "gla_norm", "g_ffn",
                   "conv_b", "w_gate_f", "w_gate_b", "conv_w")
    G["c_ctx"] = g_cctx_part
    sm_shapes = [G[n].shape for n in small_names]
    sm_all = exchange([pack([G[n] for n in small_names])], True, "gather_small_grads")[0]
    sm_tot = unpack(reduce_parts(sm_all, "sum_small_grads"), sm_shapes)
    gs = dict(zip(small_names, sm_tot))
    gs["b_mod"] = g_b_mod
    gs["w_gate_f"] = lax.dynamic_slice_in_dim(gs["w_gate_f"], me * gcols, gcols, axis=1)
    gs["w_gate_b"] = lax.dynamic_slice_in_dim(gs["w_gate_b"], me * gcols, gcols, axis=1)
    ccols = conv_w.shape[-1]
    gs["conv_w"] = lax.dynamic_slice_in_dim(gs["conv_w"], me * ccols, ccols, axis=1)

    orows = w_attn_o.shape[1]
    s_in = blocks_from_segments(G["w_in"], w_in.shape[-1])
    s_o3 = jnp.concatenate([rows_to_blocks(G["w_attn_o"]), rows_to_blocks(G["w_gla_o"]), rows_to_blocks(G["w_out"])],
                           axis=1)
    s_up = blocks_from_segments(G["w_up"], w_up.shape[-1])
    s_down = rows_to_blocks(G["w_down"])
    r_in, r_o3, r_up, r_down = scatter_reduce([s_in, s_o3, s_up, s_down], "scatter_grads")

    out = {}
    out["w_in"] = adam_reduce(r_in, w_in[0], m_w_in[0], v_w_in[0], "adam_w_in")
    o3w = jnp.concatenate([w_attn_o[0], w_gla_o[0], w_out[0]], axis=0)
    o3m = jnp.concatenate([m_w_attn_o[0], m_w_gla_o[0], m_w_out[0]], axis=0)
    o3v = jnp.concatenate([v_w_attn_o[0], v_w_gla_o[0], v_w_out[0]], axis=0)
    ro3 = adam_reduce(r_o3, o3w, o3m, o3v, "adam_o3")
    for i, n in enumerate(("w_attn_o", "w_gla_o", "w_out")):
        out[n] = [a[i * orows:(i + 1) * orows] for a in ro3]
    out["w_up"] = adam_reduce(r_up, w_up[0], m_w_up[0], v_w_up[0], "adam_w_up")
    out["w_down"] = adam_reduce(r_down, w_down[0], m_w_down[0], v_w_down[0], "adam_w_down")
    out["w_mod"] = adam_reduce(g_w_mod[None], w_mod[0], m_w_mod[0], v_w_mod[0], "adam_w_mod")
    sm_names = SMALL_REPL + SMALL_SHARD
    shapes = [Wt[n].shape for n in sm_names]
    rs = adam_reduce(pack([gs[n] for n in sm_names])[None], pack([Wt[n] for n in sm_names]),
                     pack([Mt[n] for n in sm_names]), pack([Vt[n] for n in sm_names]), "adam_small")
    rs = [unpack(a, shapes) for a in rs]
    for i, n in enumerate(sm_names):
        out[n] = [a[i] for a in rs]

    res = [loss, grad_x[None]]
    for k in range(4):
        for n in ORDER:
            res.append(out[n][k].reshape(Wt[n].shape))
    return tuple(res)
```
